```python
import jax, jax.numpy as jnp
from jax import lax
import numpy as np

D_MODEL = 2048
BATCH = 8
SEQ = 4096
DEPTH = 2

D_MIX = D_MODEL
ATTN_WIDTH = D_MIX // 2
HGRN_WIDTH = D_MIX - ATTN_WIDTH
HEAD_DIM = 64
N_Q_HEADS = ATTN_WIDTH // HEAD_DIM
N_KV_HEADS = 4
Q_PER_KV = N_Q_HEADS // N_KV_HEADS
WINDOW = 128
ATTN_BLOCK = WINDOW
ROPE_THETA = 10000.0
MASK_VALUE = -1e30
HGRN_EXPAND = 128
HGRN_HEADS = HGRN_WIDTH // HGRN_EXPAND
HGRN_VDIM = HGRN_WIDTH // HGRN_HEADS
HGRN_CHUNK = 64
D_FF = ((8 * D_MODEL // 3 + 255) // 256) * 256
D_PLE = 256
RMS_EPS = 1e-6
KV_WIDTH = N_KV_HEADS * HEAD_DIM
IN_SIZES = (ATTN_WIDTH, KV_WIDTH, KV_WIDTH, HGRN_WIDTH, HGRN_WIDTH, HGRN_WIDTH, HGRN_WIDTH)
SPLIT_POINTS = tuple(int(v) for v in np.cumsum(IN_SIZES)[:-1])
D_IN = sum(IN_SIZES)

kernel_name = "hymba_swa_sink_hgrn2_sandwich_ple"


def rms_norm(x, gain):
    xf = x.astype(jnp.float32)
    y = xf * lax.rsqrt(jnp.mean(xf * xf, axis=-1, keepdims=True) + RMS_EPS)
    return (y * gain.astype(jnp.float32)).astype(x.dtype)


def rope(x, positions):
    half = HEAD_DIM // 2
    inv_freq = ROPE_THETA ** (-jnp.arange(half, dtype=jnp.float32) / half)
    ang = positions.astype(jnp.float32)[..., None] * inv_freq
    cos = jnp.cos(ang)[:, :, None, :]
    sin = jnp.sin(ang)[:, :, None, :]
    xf = x.astype(jnp.float32)
    x1, x2 = xf[..., :half], xf[..., half:]
    out = jnp.concatenate([x1 * cos - x2 * sin, x2 * cos + x1 * sin], axis=-1)
    return out.astype(x.dtype)


def sliding_window_attention(q, k, v, sinks):
    B, S = q.shape[0], q.shape[1]
    nb = S // ATTN_BLOCK
    L = ATTN_BLOCK
    qb = q.reshape(B, nb, L, N_KV_HEADS, Q_PER_KV, HEAD_DIM)
    kb = k.reshape(B, nb, L, N_KV_HEADS, HEAD_DIM)
    vb = v.reshape(B, nb, L, N_KV_HEADS, HEAD_DIM)
    prev = lambda t: jnp.concatenate([jnp.zeros_like(t[:, :1]), t[:, :-1]], axis=1)
    kk = jnp.concatenate([prev(kb), kb], axis=2)
    vv = jnp.concatenate([prev(vb), vb], axis=2)
    scores = jnp.einsum('bnqhgd,bnkhd->bnhgqk', qb, kk,
                        preferred_element_type=jnp.float32) * (HEAD_DIM ** -0.5)
    qi = jnp.arange(L)[:, None] + L
    ki = jnp.arange(2 * L)[None, :]
    rel = qi - ki
    band = (rel >= 0) & (rel < WINDOW)
    valid = band[None] & ((jnp.arange(nb)[:, None, None] > 0) | (ki >= L)[None])
    scores = jnp.where(valid[None, :, None, None], scores, MASK_VALUE)
    sink = sinks.astype(jnp.float32).reshape(N_KV_HEADS, Q_PER_KV)[None, None, :, :, None, None]
    m = jnp.maximum(jnp.max(scores, axis=-1, keepdims=True), sink)
    e = jnp.exp(scores - m)
    probs = e / (jnp.sum(e, axis=-1, keepdims=True) + jnp.exp(sink - m))
    out = jnp.einsum('bnhgqk,bnkhd->bnqhgd', probs.astype(vv.dtype), vv)
    return out.reshape(B, S, N_Q_HEADS * HEAD_DIM)


def hgrn2_chunkwise(q, k, v, log_f):
    B, S = q.shape[0], q.shape[1]
    nc = S // HGRN_CHUNK
    C = HGRN_CHUNK

    def to_chunks(t):
        return t.astype(jnp.float32).reshape(B, nc, C, HGRN_HEADS, t.shape[-1]).transpose(1, 0, 3, 2, 4)

    qc, kc, vc, gc = to_chunks(q), to_chunks(k), to_chunks(v), to_chunks(log_f)
    causal = jnp.tril(jnp.ones((C, C), dtype=bool))

    def step(state, inp):
        q_, k_, v_, g_ = inp
        b = jnp.cumsum(g_, axis=2)
        diff = b[:, :, :, None, :] - b[:, :, None, :, :]
        decay = jnp.exp(jnp.where(causal[None, None, :, :, None], diff, MASK_VALUE))
        a = jnp.einsum('bhtk,bhsk,bhtsk->bhts', q_, k_, decay)
        o = jnp.einsum('bhts,bhsv->bhtv', a, v_) + \
            jnp.einsum('bhtk,bhkv->bhtv', q_ * jnp.exp(b), state)
        b_last = b[:, :, -1:, :]
        new_state = state * jnp.exp(b_last)[:, :, 0, :, None] + \
            jnp.einsum('bhsk,bhsv->bhkv', k_ * jnp.exp(b_last - b), v_)
        return new_state, o

    s0 = jnp.zeros((B, HGRN_HEADS, HGRN_EXPAND, HGRN_VDIM), jnp.float32)
    _, o = lax.scan(step, s0, (qc, kc, vc, gc))
    return o.transpose(1, 0, 3, 2, 4).reshape(B, S, HGRN_HEADS, HGRN_VDIM)


def hybrid_mixer(h, positions, w_in, sinks, lb, attn_gain, hgrn_gain, w_out):
    B, S = h.shape[0], h.shape[1]
    proj = h @ w_in
    q, k, v, hq, hf, hi, hg = jnp.split(proj, SPLIT_POINTS, axis=-1)
    q = rope(q.reshape(B, S, N_Q_HEADS, HEAD_DIM), positions)
    k = rope(k.reshape(B, S, N_KV_HEADS, HEAD_DIM), positions)
    v = v.reshape(B, S, N_KV_HEADS, HEAD_DIM)
    attn = rms_norm(sliding_window_attention(q, k, v, sinks), attn_gain)
    z = hf.astype(jnp.float32)
    lbf = lb.astype(jnp.float32)
    f = lbf + (1.0 - lbf) * jax.nn.sigmoid(z)
    log_f = jnp.log(f)
    k_in = (1.0 - lbf) * jax.nn.sigmoid(-z)
    hq_ = jax.nn.silu(hq.astype(jnp.float32))
    shp = (B, S, HGRN_HEADS, HGRN_EXPAND)
    o = hgrn2_chunkwise(hq_.reshape(shp), k_in.reshape(shp),
                        hi.reshape(B, S, HGRN_HEADS, HGRN_VDIM), log_f.reshape(shp))
    o = rms_norm(o, hgrn_gain.reshape(HGRN_HEADS, HGRN_VDIM)).reshape(B, S, HGRN_WIDTH)
    hgrn = (o * jax.nn.silu(hg.astype(jnp.float32))).astype(h.dtype)
    return jnp.concatenate([attn, hgrn], axis=-1) @ w_out


def _fwd_setup_inputs(seed: int = 0) -> dict:
    key = jax.random.key(seed)
    ks = jax.random.split(key, 24)
    f32 = jnp.float32
    nrm = lambda k, shape, scale: jax.random.normal(k, shape, f32) * scale
    gain = lambda k, shape: 1.0 + 0.05 * jax.random.normal(k, shape, f32)
    offsets = jax.random.randint(ks[2], (BATCH, 1), 0, 1024, dtype=jnp.int32)
    positions = offsets + jnp.arange(SEQ, dtype=jnp.int32)[None, :]
    return {
        "x": nrm(ks[0], (BATCH, SEQ, D_MODEL), 1.0),
        "p": nrm(ks[1], (DEPTH, BATCH, SEQ, D_PLE), 1.0),
        "positions": positions,
        "w_in": nrm(ks[3], (DEPTH, D_MODEL, D_IN), D_MODEL ** -0.5),
        "attn_sinks": nrm(ks[4], (DEPTH, N_Q_HEADS), 1.0),
        "hgrn_lb_logits": nrm(ks[5], (DEPTH, HGRN_WIDTH), 0.5),
        "attn_out_gain": gain(ks[6], (DEPTH, ATTN_WIDTH)),
        "hgrn_out_gain": gain(ks[7], (DEPTH, HGRN_WIDTH)),
        "w_out": nrm(ks[8], (DEPTH, D_MIX, D_MODEL), D_MIX ** -0.5),
        "pre_mix_gain": gain(ks[9], (DEPTH, D_MODEL)),
        "post_mix_gain": gain(ks[10], (DEPTH, D_MODEL)),
        "pre_ffn_gain": gain(ks[11], (DEPTH, D_MODEL)),
        "post_ffn_gain": gain(ks[12], (DEPTH, D_MODEL)),
        "w_ffn_gate": nrm(ks[13], (DEPTH, D_MODEL, D_FF), D_MODEL ** -0.5),
        "w_ffn_up": nrm(ks[14], (DEPTH, D_MODEL, D_FF), D_MODEL ** -0.5),
        "w_ffn_down": nrm(ks[15], (DEPTH, D_FF, D_MODEL), D_FF ** -0.5),
        "ple_gain": gain(ks[16], (DEPTH, D_MODEL)),
        "w_ple_gate": nrm(ks[17], (DEPTH, D_MODEL, D_MODEL), D_MODEL ** -0.5),
        "w_ple_proj": nrm(ks[18], (DEPTH, D_PLE, D_MODEL), 0.5 * D_PLE ** -0.5),
    }


def _fwd_reference(x, p, positions, w_in, attn_sinks, hgrn_lb_logits, attn_out_gain, hgrn_out_gain,
              w_out, pre_mix_gain, post_mix_gain, pre_ffn_gain, post_ffn_gain,
              w_ffn_gate, w_ffn_up, w_ffn_down, ple_gain, w_ple_gate, w_ple_proj):
    lb_soft = jax.nn.softmax(hgrn_lb_logits.astype(jnp.float32), axis=0)
    lower_bounds = jnp.cumsum(lb_soft, axis=0) - lb_soft[0:1]
    for i in range(DEPTH):
        h = rms_norm(x, pre_mix_gain[i])
        m = hybrid_mixer(h, positions, w_in[i], attn_sinks[i], lower_bounds[i],
                         attn_out_gain[i], hgrn_out_gain[i], w_out[i])
        x = x + rms_norm(m, post_mix_gain[i])
        h = rms_norm(x, pre_ffn_gain[i])
        f = (jax.nn.silu(h @ w_ffn_gate[i]) * (h @ w_ffn_up[i])) @ w_ffn_down[i]
        x = x + rms_norm(f, post_ffn_gain[i])
        gate = jax.nn.sigmoid(rms_norm(x, ple_gain[i]) @ w_ple_gate[i])
        x = x + (p[i] @ w_ple_proj[i]) * gate
    return x


import jax as _jax
import jax.numpy as _jnp

TWIN_FORMAT = 'train_step'
FWD_PARAMS = ['x', 'p', 'positions', 'w_in', 'attn_sinks', 'hgrn_lb_logits', 'attn_out_gain', 'hgrn_out_gain', 'w_out', 'pre_mix_gain', 'post_mix_gain', 'pre_ffn_gain', 'post_ffn_gain', 'w_ffn_gate', 'w_ffn_up', 'w_ffn_down', 'ple_gain', 'w_ple_gate', 'w_ple_proj']
TWIN_WEIGHTS = ['w_in', 'attn_sinks', 'hgrn_lb_logits', 'attn_out_gain', 'hgrn_out_gain', 'w_out', 'pre_mix_gain', 'post_mix_gain', 'pre_ffn_gain', 'post_ffn_gain', 'w_ffn_gate', 'w_ffn_up', 'w_ffn_down', 'ple_gain', 'w_ple_gate', 'w_ple_proj']
TWIN_DIFF_INPUT = 'x'
TWIN_INPUTS = ['x', 'p', 'positions', 'w_in', 'attn_sinks', 'hgrn_lb_logits', 'attn_out_gain', 'hgrn_out_gain', 'w_out', 'pre_mix_gain', 'post_mix_gain', 'pre_ffn_gain', 'post_ffn_gain', 'w_ffn_gate', 'w_ffn_up', 'w_ffn_down', 'ple_gain', 'w_ple_gate', 'w_ple_proj', 'loss_target', 'm_w_in', 'm_attn_sinks', 'm_hgrn_lb_logits', 'm_attn_out_gain', 'm_hgrn_out_gain', 'm_w_out', 'm_pre_mix_gain', 'm_post_mix_gain', 'm_pre_ffn_gain', 'm_post_ffn_gain', 'm_w_ffn_gate', 'm_w_ffn_up', 'm_w_ffn_down', 'm_ple_gain', 'm_w_ple_gate', 'm_w_ple_proj', 'v_w_in', 'v_attn_sinks', 'v_hgrn_lb_logits', 'v_attn_out_gain', 'v_hgrn_out_gain', 'v_w_out', 'v_pre_mix_gain', 'v_post_mix_gain', 'v_pre_ffn_gain', 'v_post_ffn_gain', 'v_w_ffn_gate', 'v_w_ffn_up', 'v_w_ffn_down', 'v_ple_gain', 'v_w_ple_gate', 'v_w_ple_proj']
TWIN_OUTPUTS = ['loss', 'grad_x', 'grad_w_in', 'grad_attn_sinks', 'grad_hgrn_lb_logits', 'grad_attn_out_gain', 'grad_hgrn_out_gain', 'grad_w_out', 'grad_pre_mix_gain', 'grad_post_mix_gain', 'grad_pre_ffn_gain', 'grad_post_ffn_gain', 'grad_w_ffn_gate', 'grad_w_ffn_up', 'grad_w_ffn_down', 'grad_ple_gain', 'grad_w_ple_gate', 'grad_w_ple_proj', 'delta_w_in', 'delta_attn_sinks', 'delta_hgrn_lb_logits', 'delta_attn_out_gain', 'delta_hgrn_out_gain', 'delta_w_out', 'delta_pre_mix_gain', 'delta_post_mix_gain', 'delta_pre_ffn_gain', 'delta_post_ffn_gain', 'delta_w_ffn_gate', 'delta_w_ffn_up', 'delta_w_ffn_down', 'delta_ple_gain', 'delta_w_ple_gate', 'delta_w_ple_proj', 'new_m_w_in', 'new_m_attn_sinks', 'new_m_hgrn_lb_logits', 'new_m_attn_out_gain', 'new_m_hgrn_out_gain', 'new_m_w_out', 'new_m_pre_mix_gain', 'new_m_post_mix_gain', 'new_m_pre_ffn_gain', 'new_m_post_ffn_gain', 'new_m_w_ffn_gate', 'new_m_w_ffn_up', 'new_m_w_ffn_down', 'new_m_ple_gain', 'new_m_w_ple_gate', 'new_m_w_ple_proj', 'new_v_w_in', 'new_v_attn_sinks', 'new_v_hgrn_lb_logits', 'new_v_attn_out_gain', 'new_v_hgrn_out_gain', 'new_v_w_out', 'new_v_pre_mix_gain', 'new_v_post_mix_gain', 'new_v_pre_ffn_gain', 'new_v_post_ffn_gain', 'new_v_w_ffn_gate', 'new_v_w_ffn_up', 'new_v_w_ffn_down', 'new_v_ple_gain', 'new_v_w_ple_gate', 'new_v_w_ple_proj']
TWIN_LEAF_KINDS = {'loss': 'loss', 'grad_x': 'grad_x', 'grad_w_in': 'grad_w', 'grad_attn_sinks': 'grad_w', 'grad_hgrn_lb_logits': 'grad_w', 'grad_attn_out_gain': 'grad_w', 'grad_hgrn_out_gain': 'grad_w', 'grad_w_out': 'grad_w', 'grad_pre_mix_gain': 'grad_w', 'grad_post_mix_gain': 'grad_w', 'grad_pre_ffn_gain': 'grad_w', 'grad_post_ffn_gain': 'grad_w', 'grad_w_ffn_gate': 'grad_w', 'grad_w_ffn_up': 'grad_w', 'grad_w_ffn_down': 'grad_w', 'grad_ple_gain': 'grad_w', 'grad_w_ple_gate': 'grad_w', 'grad_w_ple_proj': 'grad_w', 'delta_w_in': 'delta_w', 'delta_attn_sinks': 'delta_w', 'delta_hgrn_lb_logits': 'delta_w', 'delta_attn_out_gain': 'delta_w', 'delta_hgrn_out_gain': 'delta_w', 'delta_w_out': 'delta_w', 'delta_pre_mix_gain': 'delta_w', 'delta_post_mix_gain': 'delta_w', 'delta_pre_ffn_gain': 'delta_w', 'delta_post_ffn_gain': 'delta_w', 'delta_w_ffn_gate': 'delta_w', 'delta_w_ffn_up': 'delta_w', 'delta_w_ffn_down': 'delta_w', 'delta_ple_gain': 'delta_w', 'delta_w_ple_gate': 'delta_w', 'delta_w_ple_proj': 'delta_w', 'new_m_w_in': 'new_m', 'new_m_attn_sinks': 'new_m', 'new_m_hgrn_lb_logits': 'new_m', 'new_m_attn_out_gain': 'new_m', 'new_m_hgrn_out_gain': 'new_m', 'new_m_w_out': 'new_m', 'new_m_pre_mix_gain': 'new_m', 'new_m_post_mix_gain': 'new_m', 'new_m_pre_ffn_gain': 'new_m', 'new_m_post_ffn_gain': 'new_m', 'new_m_w_ffn_gate': 'new_m', 'new_m_w_ffn_up': 'new_m', 'new_m_w_ffn_down': 'new_m', 'new_m_ple_gain': 'new_m', 'new_m_w_ple_gate': 'new_m', 'new_m_w_ple_proj': 'new_m', 'new_v_w_in': 'new_v', 'new_v_attn_sinks': 'new_v', 'new_v_hgrn_lb_logits': 'new_v', 'new_v_attn_out_gain': 'new_v', 'new_v_hgrn_out_gain': 'new_v', 'new_v_w_out': 'new_v', 'new_v_pre_mix_gain': 'new_v', 'new_v_post_mix_gain': 'new_v', 'new_v_pre_ffn_gain': 'new_v', 'new_v_post_ffn_gain': 'new_v', 'new_v_w_ffn_gate': 'new_v', 'new_v_w_ffn_up': 'new_v', 'new_v_w_ffn_down': 'new_v', 'new_v_ple_gain': 'new_v', 'new_v_w_ple_gate': 'new_v', 'new_v_w_ple_proj': 'new_v'}


def _forward(args):
    return _fwd_reference(*[args[k] for k in FWD_PARAMS])


def _output_shape():
    def fwd():
        inp = _fwd_setup_inputs(0)
        return _fwd_reference(*[inp[k] for k in FWD_PARAMS])
    out = _jax.eval_shape(fwd)
    return out.shape, out.dtype

N_MICROBATCH = 1
ADAM_LR = 0.001
ADAM_B1 = 0.9
ADAM_B2 = 0.999
ADAM_EPS = 1e-08
ADAM_WD = 0.01
ADAM_STEP = 10
PER_EXAMPLE_BATCH_AXIS = {'x': 0, 'p': 1, 'positions': 0, 'loss_target': 0}
SHARED_INPUTS = []
_WEIGHT_DTYPES = {'w_in': _jnp.float32, 'attn_sinks': _jnp.float32, 'hgrn_lb_logits': _jnp.float32, 'attn_out_gain': _jnp.float32, 'hgrn_out_gain': _jnp.float32, 'w_out': _jnp.float32, 'pre_mix_gain': _jnp.float32, 'post_mix_gain': _jnp.float32, 'pre_ffn_gain': _jnp.float32, 'post_ffn_gain': _jnp.float32, 'w_ffn_gate': _jnp.float32, 'w_ffn_up': _jnp.float32, 'w_ffn_down': _jnp.float32, 'ple_gain': _jnp.float32, 'w_ple_gate': _jnp.float32, 'w_ple_proj': _jnp.float32}
MOMENT_SCALE = {'w_in': 5.176763e-01, 'attn_sinks': 3.610019e-01, 'hgrn_lb_logits': 1.399608e-02, 'attn_out_gain': 1.097219e+00, 'hgrn_out_gain': 2.939124e-01, 'w_out': 7.905778e-01, 'pre_mix_gain': 8.635467e-01, 'post_mix_gain': 1.585146e+01, 'pre_ffn_gain': 5.116745e-01, 'post_ffn_gain': 1.601949e+01, 'w_ffn_gate': 1.874130e-01, 'w_ffn_up': 2.376109e-01, 'w_ffn_down': 3.963014e-01, 'ple_gain': 1.096665e-01, 'w_ple_gate': 3.115928e-02, 'w_ple_proj': 1.767124e-01}


def _to_microbatches(a, axis):
    t = _jnp.moveaxis(a, axis, 0)
    t = t.reshape((N_MICROBATCH, t.shape[0] // N_MICROBATCH) + t.shape[1:])
    return _jnp.moveaxis(t, 1, axis + 1)


def setup_inputs(seed: int = 0) -> dict:
    inp = _fwd_setup_inputs(seed)
    key = _jax.random.fold_in(_jax.random.key(seed), 7919)
    shape, _ = _output_shape()
    out = dict(inp)
    out["loss_target"] = _jax.random.normal(_jax.random.fold_in(key, 0), shape, _jnp.float32)
    for i, name in enumerate(TWIN_WEIGHTS):
        w = inp[name].astype(_jnp.float32)
        if MOMENT_SCALE is None:
            s = _jnp.sqrt(_jnp.mean(_jnp.square(w)) + 1e-30)
        else:
            s = MOMENT_SCALE[name]
        km, kv = _jax.random.split(_jax.random.fold_in(key, i + 1))
        out[name] = w
        out["m_" + name] = s * _jax.random.normal(km, w.shape, _jnp.float32)
        out["v_" + name] = (s * s) * _jax.random.uniform(kv, w.shape, _jnp.float32, 0.5, 1.5)
    if N_MICROBATCH > 1:
        for name, axis in PER_EXAMPLE_BATCH_AXIS.items():
            out[name] = _to_microbatches(out[name], axis)
    return {'x': out['x'], 'p': out['p'], 'positions': out['positions'], 'w_in': out['w_in'], 'attn_sinks': out['attn_sinks'], 'hgrn_lb_logits': out['hgrn_lb_logits'], 'attn_out_gain': out['attn_out_gain'], 'hgrn_out_gain': out['hgrn_out_gain'], 'w_out': out['w_out'], 'pre_mix_gain': out['pre_mix_gain'], 'post_mix_gain': out['post_mix_gain'], 'pre_ffn_gain': out['pre_ffn_gain'], 'post_ffn_gain': out['post_ffn_gain'], 'w_ffn_gate': out['w_ffn_gate'], 'w_ffn_up': out['w_ffn_up'], 'w_ffn_down': out['w_ffn_down'], 'ple_gain': out['ple_gain'], 'w_ple_gate': out['w_ple_gate'], 'w_ple_proj': out['w_ple_proj'], 'loss_target': out['loss_target'], 'm_w_in': out['m_w_in'], 'm_attn_sinks': out['m_attn_sinks'], 'm_hgrn_lb_logits': out['m_hgrn_lb_logits'], 'm_attn_out_gain': out['m_attn_out_gain'], 'm_hgrn_out_gain': out['m_hgrn_out_gain'], 'm_w_out': out['m_w_out'], 'm_pre_mix_gain': out['m_pre_mix_gain'], 'm_post_mix_gain': out['m_post_mix_gain'], 'm_pre_ffn_gain': out['m_pre_ffn_gain'], 'm_post_ffn_gain': out['m_post_ffn_gain'], 'm_w_ffn_gate': out['m_w_ffn_gate'], 'm_w_ffn_up': out['m_w_ffn_up'], 'm_w_ffn_down': out['m_w_ffn_down'], 'm_ple_gain': out['m_ple_gain'], 'm_w_ple_gate': out['m_w_ple_gate'], 'm_w_ple_proj': out['m_w_ple_proj'], 'v_w_in': out['v_w_in'], 'v_attn_sinks': out['v_attn_sinks'], 'v_hgrn_lb_logits': out['v_hgrn_lb_logits'], 'v_attn_out_gain': out['v_attn_out_gain'], 'v_hgrn_out_gain': out['v_hgrn_out_gain'], 'v_w_out': out['v_w_out'], 'v_pre_mix_gain': out['v_pre_mix_gain'], 'v_post_mix_gain': out['v_post_mix_gain'], 'v_pre_ffn_gain': out['v_pre_ffn_gain'], 'v_post_ffn_gain': out['v_post_ffn_gain'], 'v_w_ffn_gate': out['v_w_ffn_gate'], 'v_w_ffn_up': out['v_w_ffn_up'], 'v_w_ffn_down': out['v_w_ffn_down'], 'v_ple_gain': out['v_ple_gain'], 'v_w_ple_gate': out['v_w_ple_gate'], 'v_w_ple_proj': out['v_w_ple_proj']}


def _loss(weights, diff, rest, loss_target):
    with _jax.named_scope("forward"):
        args = {**rest, TWIN_DIFF_INPUT: diff, **{k: w.astype(_WEIGHT_DTYPES[k]) for k, w in weights.items()}}
        y = _forward(args)
    with _jax.named_scope("loss_head"):
        err = _jnp.square(y.astype(_jnp.float32) - loss_target)
        return 0.5 * _jnp.sum(_jnp.mean(err, axis=-1)) if err.ndim else 0.5 * err


def _adamw(w, g, m, v):
    m = ADAM_B1 * m + (1.0 - ADAM_B1) * g
    v = ADAM_B2 * v + (1.0 - ADAM_B2) * _jnp.square(g)
    m_hat = m / (1.0 - ADAM_B1 ** ADAM_STEP)
    v_hat = v / (1.0 - ADAM_B2 ** ADAM_STEP)
    delta = -ADAM_LR * (m_hat / (_jnp.sqrt(v_hat) + ADAM_EPS) + ADAM_WD * w)
    return delta, m, v


def reference(x, p, positions, w_in, attn_sinks, hgrn_lb_logits, attn_out_gain, hgrn_out_gain, w_out, pre_mix_gain, post_mix_gain, pre_ffn_gain, post_ffn_gain, w_ffn_gate, w_ffn_up, w_ffn_down, ple_gain, w_ple_gate, w_ple_proj, loss_target, m_w_in, m_attn_sinks, m_hgrn_lb_logits, m_attn_out_gain, m_hgrn_out_gain, m_w_out, m_pre_mix_gain, m_post_mix_gain, m_pre_ffn_gain, m_post_ffn_gain, m_w_ffn_gate, m_w_ffn_up, m_w_ffn_down, m_ple_gain, m_w_ple_gate, m_w_ple_proj, v_w_in, v_attn_sinks, v_hgrn_lb_logits, v_attn_out_gain, v_hgrn_out_gain, v_w_out, v_pre_mix_gain, v_post_mix_gain, v_pre_ffn_gain, v_post_ffn_gain, v_w_ffn_gate, v_w_ffn_up, v_w_ffn_down, v_ple_gain, v_w_ple_gate, v_w_ple_proj):
    given = dict(x=x, p=p, positions=positions, w_in=w_in, attn_sinks=attn_sinks, hgrn_lb_logits=hgrn_lb_logits, attn_out_gain=attn_out_gain, hgrn_out_gain=hgrn_out_gain, w_out=w_out, pre_mix_gain=pre_mix_gain, post_mix_gain=post_mix_gain, pre_ffn_gain=pre_ffn_gain, post_ffn_gain=post_ffn_gain, w_ffn_gate=w_ffn_gate, w_ffn_up=w_ffn_up, w_ffn_down=w_ffn_down, ple_gain=ple_gain, w_ple_gate=w_ple_gate, w_ple_proj=w_ple_proj, loss_target=loss_target, m_w_in=m_w_in, m_attn_sinks=m_attn_sinks, m_hgrn_lb_logits=m_hgrn_lb_logits, m_attn_out_gain=m_attn_out_gain, m_hgrn_out_gain=m_hgrn_out_gain, m_w_out=m_w_out, m_pre_mix_gain=m_pre_mix_gain, m_post_mix_gain=m_post_mix_gain, m_pre_ffn_gain=m_pre_ffn_gain, m_post_ffn_gain=m_post_ffn_gain, m_w_ffn_gate=m_w_ffn_gate, m_w_ffn_up=m_w_ffn_up, m_w_ffn_down=m_w_ffn_down, m_ple_gain=m_ple_gain, m_w_ple_gate=m_w_ple_gate, m_w_ple_proj=m_w_ple_proj, v_w_in=v_w_in, v_attn_sinks=v_attn_sinks, v_hgrn_lb_logits=v_hgrn_lb_logits, v_attn_out_gain=v_attn_out_gain, v_hgrn_out_gain=v_hgrn_out_gain, v_w_out=v_w_out, v_pre_mix_gain=v_pre_mix_gain, v_post_mix_gain=v_post_mix_gain, v_pre_ffn_gain=v_pre_ffn_gain, v_post_ffn_gain=v_post_ffn_gain, v_w_ffn_gate=v_w_ffn_gate, v_w_ffn_up=v_w_ffn_up, v_w_ffn_down=v_w_ffn_down, v_ple_gain=v_ple_gain, v_w_ple_gate=v_w_ple_gate, v_w_ple_proj=v_w_ple_proj)
    weights = {n: given[n] for n in TWIN_WEIGHTS}
    shared = {n: given[n] for n in SHARED_INPUTS}
    per_example = {n: given[n] for n in ['x', 'p', 'positions']}
    grad_fn = _jax.value_and_grad(_loss, argnums=(0, 1))

    def one_microbatch(ex, loss_target):
        ex = dict(ex)
        diff = ex.pop(TWIN_DIFF_INPUT)
        return grad_fn(weights, diff, {**shared, **ex}, loss_target)

    if N_MICROBATCH == 1:
        loss, (grad_w, grad_x) = one_microbatch(per_example, given["loss_target"])
    else:
        def body(carry, xs):
            loss_sum, grad_sum = carry
            l_k, (gw_k, gx_k) = one_microbatch(xs[0], xs[1])
            with _jax.named_scope("update"):
                return (loss_sum + l_k, _jax.tree.map(_jnp.add, grad_sum, gw_k)), gx_k

        init = (_jnp.zeros((), _jnp.float32), _jax.tree.map(_jnp.zeros_like, weights))
        (loss, grad_w), grad_x = _jax.lax.scan(body, init, (per_example, given["loss_target"]))
    with _jax.named_scope("update"):
        delta_w, new_m, new_v = {}, {}, {}
        for n in TWIN_WEIGHTS:
            delta_w[n], new_m[n], new_v[n] = _adamw(weights[n], grad_w[n], given["m_" + n], given["v_" + n])
    return (loss, grad_x, *[grad_w[n] for n in TWIN_WEIGHTS], *[delta_w[n] for n in TWIN_WEIGHTS],
            *[new_m[n] for n in TWIN_WEIGHTS], *[new_v[n] for n in TWIN_WEIGHTS])
```

```python
import functools

import jax
import jax.numpy as jnp
from jax import lax
from jax.experimental import pallas as pl
from jax.experimental.pallas import tpu as pltpu

F32, BF16 = jnp.float32, jnp.bfloat16
SDS = jax.ShapeDtypeStruct
MESH = pl.DeviceIdType.MESH

D_MODEL = 2048
ATTN_WIDTH = 1024
HGRN_WIDTH = 1024
KV_WIDTH = 256
N_Q_HEADS = 16
N_KV_HEADS = 4
Q_PER_KV = 4
WINDOW = 128
MASK_VALUE = -1e30
ROPE_THETA = 10000.0
HGRN_HEADS = 8
HGRN_CHUNK = 16
D_FF = 5632
D_PLE = 256
RMS_EPS = 1e-6
LANES = 128
N_CHIPS = 4
COL_Q, COL_K, COL_V, COL_HQ, COL_HF, COL_HI, COL_HG = 0, 8, 10, 12, 20, 28, 36

ADAM_LR, ADAM_B1, ADAM_B2, ADAM_EPS, ADAM_WD, ADAM_STEP = 0.001, 0.9, 0.999, 1e-08, 0.01, 10

VMEM_LIMIT = 56 * 1024 * 1024
ROW_TILE = 256

_NN = (((1,), (0,)), ((), ()))
_NT = (((1,), (1,)), ((), ()))
_TN = (((0,), (0,)), ((), ()))


def _pc(body, *, name, out_shape, in_specs, out_specs, grid=(), scratch=(), sem=None, grid_spec=None, **kw):
    params = dict(vmem_limit_bytes=VMEM_LIMIT)
    if sem is not None:
        params["dimension_semantics"] = sem
    if grid_spec is not None:
        return pl.pallas_call(body, name=name, out_shape=out_shape, grid_spec=grid_spec,
                              compiler_params=pltpu.CompilerParams(**params), **kw)
    return pl.pallas_call(body, name=name, out_shape=out_shape, grid=grid, in_specs=in_specs, out_specs=out_specs,
                          scratch_shapes=list(scratch), compiler_params=pltpu.CompilerParams(**params), **kw)


def _sigmoid(x):
    return 1.0 / (1.0 + jnp.exp(-x))


def _rstd(x):
    return lax.rsqrt(jnp.mean(x * x, axis=-1, keepdims=True) + RMS_EPS)


def _rows(t, w, col=0):
    return pl.BlockSpec((t, w), lambda i, col=col: (i, col))


def _fixed(shape):
    return pl.BlockSpec(shape, lambda *_: (0,) * len(shape))


def _mm(name, a, b, *, dims, grid, a_spec, b_spec, o_spec, out_shape, acc_shape=None, kaxis=None, add=None,
        add_spec=None, sem=None):
    nk = grid[kaxis] if kaxis is not None else 1

    def body(*refs):
        if add is None:
            a_ref, b_ref, o_ref, *scr = refs
            add_ref = None
        else:
            a_ref, b_ref, add_ref, o_ref, *scr = refs
        r = lax.dot_general(a_ref[...].astype(BF16), b_ref[...].astype(BF16), dims, preferred_element_type=F32)
        if nk == 1:
            if add_ref is not None:
                r = r + add_ref[...]
            o_ref[...] = r.astype(o_ref.dtype)
            return
        acc = scr[0]
        k = pl.program_id(kaxis)

        @pl.when(k == 0)
        def _():
            acc[...] = r

        @pl.when(k > 0)
        def _():
            acc[...] += r

        @pl.when(k == nk - 1)
        def _():
            t = acc[...]
            if add_ref is not None:
                t = t + add_ref[...]
            o_ref[...] = t.astype(o_ref.dtype)

    ins = [a, b] + ([] if add is None else [add])
    specs = [a_spec, b_spec] + ([] if add is None else [add_spec])
    scratch = [] if nk == 1 else [pltpu.VMEM(acc_shape, F32)]
    return _pc(body, name=name, grid=grid, in_specs=specs, out_specs=o_spec, out_shape=out_shape, scratch=scratch,
               sem=sem)(*ins)


def _tile(n, t):
    if n <= t:
        return n
    while n % t:
        t //= 2
    assert t % 8 == 0
    return t


def mm_col(name, a, wg, out_dtype=F32):
    s, k = a.shape
    _, _, n = wg.shape
    tm = _tile(s, 512)
    return _mm(name, a, wg, dims=_NN, grid=(N_CHIPS, s // tm),
               a_spec=pl.BlockSpec((tm, k), lambda j, i: (i, 0)),
               b_spec=pl.BlockSpec((None, k, n), lambda j, i: (j, 0, 0)),
               o_spec=pl.BlockSpec((tm, n), lambda j, i: (i, j)),
               out_shape=SDS((s, N_CHIPS * n), out_dtype), sem=("parallel", "parallel"))


def mm_row(name, a, wg, out_dtype=F32):
    s, _ = a.shape
    _, r, n = wg.shape
    tm = _tile(s, 512)
    return _mm(name, a, wg, dims=_NN, grid=(s // tm, N_CHIPS), kaxis=1,
               a_spec=pl.BlockSpec((tm, r), lambda i, k: (i, k)),
               b_spec=pl.BlockSpec((None, r, n), lambda i, k: (k, 0, 0)),
               o_spec=pl.BlockSpec((tm, n), lambda i, k: (i, 0)),
               out_shape=SDS((s, n), out_dtype), acc_shape=(tm, n), sem=("parallel", "arbitrary"))


def mm_col_t(name, dy, wg, add=None, out_dtype=F32):
    s, _ = dy.shape
    _, k, n = wg.shape
    tm = _tile(s, 512)
    return _mm(name, dy, wg, dims=_NT, grid=(s // tm, N_CHIPS), kaxis=1,
               a_spec=pl.BlockSpec((tm, n), lambda i, kk: (i, kk)),
               b_spec=pl.BlockSpec((None, k, n), lambda i, kk: (kk, 0, 0)),
               o_spec=pl.BlockSpec((tm, k), lambda i, kk: (i, 0)),
               add=add, add_spec=pl.BlockSpec((tm, k), lambda i, kk: (i, 0)),
               out_shape=SDS((s, k), out_dtype), acc_shape=(tm, k), sem=("parallel", "arbitrary"))


def mm_row_t(name, dy, wg, out_dtype=F32):
    s, n = dy.shape
    _, r, _ = wg.shape
    tm = _tile(s, 512)
    return _mm(name, dy, wg, dims=_NT, grid=(N_CHIPS, s // tm),
               a_spec=pl.BlockSpec((tm, n), lambda j, i: (i, 0)),
               b_spec=pl.BlockSpec((None, r, n), lambda j, i: (j, 0, 0)),
               o_spec=pl.BlockSpec((tm, r), lambda j, i: (i, j)),
               out_shape=SDS((s, N_CHIPS * r), out_dtype), sem=("parallel", "parallel"))


def mm_wg_col(name, a, dy):
    s, k = a.shape
    n = dy.shape[1] // N_CHIPS
    tm = _tile(k // 2, 1024)
    ts = _tile(s, 1024)
    hb = (k // 2) // tm
    return _mm(name, a, dy, dims=_TN, grid=(N_CHIPS, k // tm, s // ts), kaxis=2,
               a_spec=pl.BlockSpec((ts, tm), lambda j, i, t: (t, i)),
               b_spec=pl.BlockSpec((ts, n), lambda j, i, t: (t, j)),
               o_spec=pl.BlockSpec((None, None, tm, n), lambda j, i, t: (i // hb, j, i % hb, 0)),
               out_shape=SDS((2, N_CHIPS, k // 2, n), F32), acc_shape=(tm, n),
               sem=("parallel", "parallel", "arbitrary"))


def mm_wg_row(name, a, dy):
    s, n = dy.shape
    r = a.shape[1] // N_CHIPS
    tn = _tile(n // 2, 1024)
    ts = _tile(s, 1024)
    nb = (n // 2) // tn
    return _mm(name, a, dy, dims=_TN, grid=(N_CHIPS, n // tn, s // ts), kaxis=2,
               a_spec=pl.BlockSpec((ts, r), lambda j, i, t: (t, j)),
               b_spec=pl.BlockSpec((ts, tn), lambda j, i, t: (t, i)),
               o_spec=pl.BlockSpec((None, None, r, tn), lambda j, i, t: (i // nb, j, 0, i % nb)),
               out_shape=SDS((2, N_CHIPS, r, n // 2), F32), acc_shape=(r, tn),
               sem=("parallel", "parallel", "arbitrary"))


def pre_norm(name, x, gain):
    s, d = x.shape
    t = _tile(s, ROW_TILE)

    def body(x_ref, g_ref, o_ref):
        xv = x_ref[...]
        o_ref[...] = ((xv * _rstd(xv)) * g_ref[...]).astype(BF16)

    return _pc(body, name=name, grid=(s // t,), in_specs=[_rows(t, d), _fixed((1, d))], out_specs=_rows(t, d),
               out_shape=SDS((s, d), BF16), sem=("parallel",))(x, gain)


def post_pre_norm(name, m, g_post, res, g_pre):
    s, d = m.shape
    t = _tile(s, ROW_TILE)

    def body(m_ref, gp_ref, r_ref, gn_ref, x_ref, h_ref):
        mv = m_ref[...]
        xn = r_ref[...] + (mv * _rstd(mv)) * gp_ref[...]
        x_ref[...] = xn
        h_ref[...] = ((xn * _rstd(xn)) * gn_ref[...]).astype(BF16)

    return _pc(body, name=name, grid=(s // t,),
               in_specs=[_rows(t, d), _fixed((1, d)), _rows(t, d), _fixed((1, d))],
               out_specs=[_rows(t, d), _rows(t, d)], out_shape=[SDS((s, d), F32), SDS((s, d), BF16)],
               sem=("parallel",))(m, g_post, res, g_pre)


def ple_fwd_mid(name, z, pp, x2, g_next):
    s, d = z.shape
    t = _tile(s, ROW_TILE)

    def body(z_ref, p_ref, x_ref, g_ref, xo_ref, h_ref):
        xn = x_ref[...] + p_ref[...] * _sigmoid(z_ref[...])
        xo_ref[...] = xn
        h_ref[...] = ((xn * _rstd(xn)) * g_ref[...]).astype(BF16)

    return _pc(body, name=name, grid=(s // t,),
               in_specs=[_rows(t, d), _rows(t, d), _rows(t, d), _fixed((1, d))],
               out_specs=[_rows(t, d), _rows(t, d)], out_shape=[SDS((s, d), F32), SDS((s, d), BF16)],
               sem=("parallel",))(z, pp, x2, g_next)


def ple_fwd_loss(name, z, pp, x2, target):
    s, d = z.shape
    t = _tile(s, ROW_TILE)

    def body(z_ref, p_ref, x_ref, t_ref, dy_ref, l_ref):
        err = (x_ref[...] + p_ref[...] * _sigmoid(z_ref[...])) - t_ref[...]
        dy_ref[...] = err * (1.0 / d)
        part = jnp.sum(jnp.sum(err * err, axis=-1, keepdims=True), axis=0, keepdims=True) * (0.5 / d)

        @pl.when(pl.program_id(0) == 0)
        def _():
            l_ref[...] = part

        @pl.when(pl.program_id(0) > 0)
        def _():
            l_ref[...] += part

    return _pc(body, name=name, grid=(s // t,),
               in_specs=[_rows(t, d), _rows(t, d), _rows(t, d), _rows(t, d)],
               out_specs=[_rows(t, d), _fixed((1, 1))], out_shape=[SDS((s, d), F32), SDS((1, 1), F32)],
               sem=("arbitrary",))(z, pp, x2, target)


def ple_bwd(name, dx3, z, pp):
    s, d = z.shape
    t = _tile(s, ROW_TILE)

    def body(d_ref, z_ref, p_ref, dpp_ref, dz_ref):
        gate = _sigmoid(z_ref[...])
        dv = d_ref[...]
        dpp_ref[...] = (dv * gate).astype(BF16)
        dz_ref[...] = ((dv * p_ref[...]) * (gate * (1.0 - gate))).astype(BF16)

    return _pc(body, name=name, grid=(s // t,), in_specs=[_rows(t, d)] * 3, out_specs=[_rows(t, d)] * 2,
               out_shape=[SDS((s, d), BF16)] * 2, sem=("parallel",))(dx3, z, pp)


def norm_bwd(name, xin, gain, dy, dres=None, out_dtype=F32):
    s, d = xin.shape
    t = _tile(s, ROW_TILE)

    def body(*refs):
        if dres is None:
            x_ref, g_ref, dy_ref, dx_ref, dg_ref = refs
            r_ref = None
        else:
            x_ref, g_ref, dy_ref, r_ref, dx_ref, dg_ref = refs
        xv = x_ref[...]
        r = _rstd(xv)
        xh = xv * r
        dyv = dy_ref[...].astype(F32)
        dyg = dyv * g_ref[...]
        c = jnp.mean(dyg * xh, axis=-1, keepdims=True)
        dx = r * (dyg - xh * c)
        if r_ref is not None:
            dx = dx + r_ref[...]
        dx_ref[...] = dx.astype(out_dtype)
        part = jnp.sum(dyv * xh, axis=0, keepdims=True)

        @pl.when(pl.program_id(0) == 0)
        def _():
            dg_ref[...] = part

        @pl.when(pl.program_id(0) > 0)
        def _():
            dg_ref[...] += part

    ins = [xin, gain, dy] + ([] if dres is None else [dres])
    specs = [_rows(t, d), _fixed((1, d)), _rows(t, d)] + ([] if dres is None else [_rows(t, d)])
    return _pc(body, name=name, grid=(s // t,), in_specs=specs, out_specs=[_rows(t, d), _fixed((1, d))],
               out_shape=[SDS((s, d), out_dtype), SDS((1, d), F32)], sem=("arbitrary",))(*ins)


def swiglu_fwd(name, g, u):
    s, f = g.shape
    t = _tile(s, ROW_TILE)
    w = f // N_CHIPS
    spec = pl.BlockSpec((t, w), lambda i, j: (i, j))

    def body(g_ref, u_ref, a_ref):
        gv = g_ref[...]
        a_ref[...] = ((gv * _sigmoid(gv)) * u_ref[...]).astype(BF16)

    return _pc(body, name=name, grid=(s // t, N_CHIPS), in_specs=[spec, spec], out_specs=spec,
               out_shape=SDS((s, f), BF16), sem=("parallel", "parallel"))(g, u)


def swiglu_bwd(name, g, u, da):
    s, f = g.shape
    t = _tile(s, ROW_TILE)
    w = f // N_CHIPS
    spec = pl.BlockSpec((t, w), lambda i, j: (i, j))

    def body(g_ref, u_ref, da_ref, dg_ref, du_ref):
        gv = g_ref[...]
        sg = _sigmoid(gv)
        dav = da_ref[...]
        du_ref[...] = (dav * (gv * sg)).astype(BF16)
        dg_ref[...] = ((dav * u_ref[...]) * (sg * (1.0 + gv * (1.0 - sg)))).astype(BF16)

    return _pc(body, name=name, grid=(s // t, N_CHIPS), in_specs=[spec] * 3, out_specs=[spec] * 2,
               out_shape=[SDS((s, f), BF16)] * 2, sem=("parallel", "parallel"))(g, u, da)


def _lane(shape):
    return lax.broadcasted_iota(jnp.int32, shape, 1)


def _swap_halves(x):
    lo = (_lane(x.shape) % 64) < 32
    return jnp.where(lo, pltpu.roll(x, 96, 1), pltpu.roll(x, 32, 1))


def rope_tables(name, pos_col, inv_freq):
    s = pos_col.shape[0]
    t = _tile(s, ROW_TILE)

    def body(p_ref, f_ref, c_ref, s_ref):
        ang = p_ref[...].astype(F32) * f_ref[...]
        lo = (_lane(ang.shape) % 64) < 32
        c_ref[...] = jnp.cos(ang)
        sn = jnp.sin(ang)
        s_ref[...] = jnp.where(lo, -sn, sn)

    return _pc(body, name=name, grid=(s // t,), in_specs=[_rows(t, 1), _fixed((1, LANES))],
               out_specs=[_rows(t, LANES)] * 2, out_shape=[SDS((s, LANES), F32)] * 2, sem=("parallel",))(pos_col, inv_freq)


def _pad_heads(chunk, lo_mask):
    zero = jnp.zeros_like(chunk)
    return jnp.where(lo_mask, chunk, zero), jnp.where(lo_mask, pltpu.roll(chunk, 64, 1), zero)


def rope_qkv(name, proj, cos, sin):
    s = proj.shape[0]
    t = _tile(s, ROW_TILE)

    def body(q_ref, kv_ref, c_ref, s_ref, qp_ref, kp_ref, vp_ref):
        cs, sn = c_ref[...], s_ref[...]
        lo_mask = _lane(cs.shape) < 64

        def rot(x):
            return x * cs + _swap_halves(x) * sn

        for j in range(ATTN_WIDTH // LANES):
            a, b = _pad_heads(rot(q_ref[:, j * LANES:(j + 1) * LANES]), lo_mask)
            qp_ref[:, (2 * j) * LANES:(2 * j + 1) * LANES] = a.astype(BF16)
            qp_ref[:, (2 * j + 1) * LANES:(2 * j + 2) * LANES] = b.astype(BF16)
        for j in range(KV_WIDTH // LANES):
            a, b = _pad_heads(rot(kv_ref[:, j * LANES:(j + 1) * LANES]), lo_mask)
            kp_ref[:, (2 * j) * LANES:(2 * j + 1) * LANES] = a.astype(BF16)
            kp_ref[:, (2 * j + 1) * LANES:(2 * j + 2) * LANES] = b.astype(BF16)
            a, b = _pad_heads(kv_ref[:, KV_WIDTH + j * LANES:KV_WIDTH + (j + 1) * LANES], lo_mask)
            vp_ref[:, (2 * j) * LANES:(2 * j + 1) * LANES] = a.astype(BF16)
            vp_ref[:, (2 * j + 1) * LANES:(2 * j + 2) * LANES] = b.astype(BF16)

    return _pc(body, name=name, grid=(s // t,),
               in_specs=[_rows(t, ATTN_WIDTH, 0), _rows(t, 2 * KV_WIDTH, 2), _rows(t, LANES), _rows(t, LANES)],
               out_specs=[_rows(t, N_Q_HEADS * LANES), _rows(t, N_KV_HEADS * LANES), _rows(t, N_KV_HEADS * LANES)],
               out_shape=[SDS((s, N_Q_HEADS * LANES), BF16), SDS((s, N_KV_HEADS * LANES), BF16),
                          SDS((s, N_KV_HEADS * LANES), BF16)],
               sem=("parallel",))(proj, proj, cos, sin)


def _attn_mask(n):
    L = WINDOW
    qi = lax.broadcasted_iota(jnp.int32, (L, 2 * L), 0) + L
    ki = lax.broadcasted_iota(jnp.int32, (L, 2 * L), 1)
    rel = qi - ki
    return (rel >= 0) & (rel < WINDOW) & ((n > 0) | (ki >= L))


def _attn_probs(qh, kk, valid, sink):
    sc = lax.dot_general(qh, kk, _NT, preferred_element_type=F32) * 0.125
    sc = jnp.where(valid, sc, MASK_VALUE)
    m = jnp.maximum(jnp.max(sc, axis=-1, keepdims=True), sink)
    e = jnp.exp(sc - m)
    es = jnp.exp(sink - m)
    den = jnp.sum(e, axis=-1, keepdims=True) + es
    return e / den, es / den


def _attn_specs(s):
    L = WINDOW
    cur = lambda n: (n, 0)
    prev = lambda n: (jnp.maximum(n - 1, 0), 0)
    kvw = N_KV_HEADS * LANES
    return [pl.BlockSpec((L, N_Q_HEADS * LANES), cur), pl.BlockSpec((L, kvw), cur), pl.BlockSpec((L, kvw), prev),
            pl.BlockSpec((L, kvw), cur), pl.BlockSpec((L, kvw), prev), pl.BlockSpec(memory_space=pltpu.SMEM)]


def attn_fwd(name, qp, kp, vp, sinks):
    s = qp.shape[0]
    L = WINDOW

    def body(q_ref, kc_ref, kp_ref, vc_ref, vp_ref, sk_ref, o_ref):
        valid = _attn_mask(pl.program_id(0))
        outs = []
        for kvh in range(N_KV_HEADS):
            cols = slice(kvh * LANES, (kvh + 1) * LANES)
            kk = jnp.concatenate([kp_ref[:, cols], kc_ref[:, cols]], axis=0)
            vv = jnp.concatenate([vp_ref[:, cols], vc_ref[:, cols]], axis=0)
            for g in range(Q_PER_KV):
                h = kvh * Q_PER_KV + g
                p, _ = _attn_probs(q_ref[:, h * LANES:(h + 1) * LANES], kk, valid, sk_ref[h])
                outs.append(jnp.dot(p.astype(BF16), vv, preferred_element_type=F32))
        for j in range(ATTN_WIDTH // LANES):
            o_ref[:, j * LANES:(j + 1) * LANES] = outs[2 * j] + pltpu.roll(outs[2 * j + 1], 64, 1)

    return _pc(body, name=name, grid=(s // L,), in_specs=_attn_specs(s),
               out_specs=pl.BlockSpec((L, ATTN_WIDTH), lambda n: (n, 0)),
               out_shape=SDS((s, ATTN_WIDTH), F32), sem=("parallel",))(qp, kp, kp, vp, vp, sinks)


def attn_bwd(name, qp, kp, vp, sinks, dattn):
    s = qp.shape[0]
    L = WINDOW
    kvw = N_KV_HEADS * LANES

    def body(q_ref, kc_ref, kp_ref, vc_ref, vp_ref, sk_ref, do_ref, dq_ref, dkc_ref, dkp_ref, dvc_ref, dvp_ref, ds_ref):
        n = pl.program_id(0)
        valid = _attn_mask(n)
        lo_mask = _lane((L, LANES)) < 64
        lane1 = _lane((1, LANES))
        dsink = jnp.zeros((1, LANES), F32)
        for kvh in range(N_KV_HEADS):
            cols = slice(kvh * LANES, (kvh + 1) * LANES)
            kk = jnp.concatenate([kp_ref[:, cols], kc_ref[:, cols]], axis=0)
            vv = jnp.concatenate([vp_ref[:, cols], vc_ref[:, cols]], axis=0)
            dkk = jnp.zeros((2 * L, LANES), F32)
            dvv = jnp.zeros((2 * L, LANES), F32)
            for g in range(Q_PER_KV):
                h = kvh * Q_PER_KV + g
                qh = q_ref[:, h * LANES:(h + 1) * LANES]
                p, ps = _attn_probs(qh, kk, valid, sk_ref[h])
                chunk = do_ref[:, (h // 2) * LANES:(h // 2 + 1) * LANES]
                if h % 2:
                    chunk = pltpu.roll(chunk, 64, 1)
                doh = jnp.where(lo_mask, chunk, 0.0).astype(BF16)
                dp = lax.dot_general(doh, vv, _NT, preferred_element_type=F32)
                delta = jnp.sum(p * dp, axis=-1, keepdims=True)
                dsb = ((p * (dp - delta)) * 0.125).astype(BF16)
                dsink = dsink + jnp.where(lane1 == h, -jnp.sum(ps * delta, axis=0, keepdims=True), 0.0)
                dq_ref[:, h * LANES:(h + 1) * LANES] = jnp.dot(dsb, kk, preferred_element_type=F32)
                dkk = dkk + lax.dot_general(dsb, qh, _TN, preferred_element_type=F32)
                dvv = dvv + lax.dot_general(p.astype(BF16), doh, _TN, preferred_element_type=F32)
            dkp_ref[:, cols] = dkk[:L]
            dkc_ref[:, cols] = dkk[L:]
            dvp_ref[:, cols] = dvv[:L]
            dvc_ref[:, cols] = dvv[L:]

        @pl.when(n == 0)
        def _():
            ds_ref[...] = dsink

        @pl.when(n > 0)
        def _():
            ds_ref[...] += dsink

    blk = lambda w: pl.BlockSpec((L, w), lambda n: (n, 0))
    return _pc(body, name=name, grid=(s // L,), in_specs=_attn_specs(s) + [blk(ATTN_WIDTH)],
               out_specs=[blk(N_Q_HEADS * LANES), blk(kvw), blk(kvw), blk(kvw), blk(kvw), _fixed((1, LANES))],
               out_shape=[SDS((s, N_Q_HEADS * LANES), F32)] + [SDS((s, kvw), F32)] * 4 + [SDS((1, LANES), F32)],
               sem=("arbitrary",))(qp, kp, kp, vp, vp, sinks, dattn)


def rope_bwd(name, dqp, dkc, dkp, dvc, dvp, cos, sin):
    s = dqp.shape[0]
    L = WINDOW
    nb = s // L
    kvw = N_KV_HEADS * LANES

    def body(dq_ref, dkc_ref, dkp_ref, dvc_ref, dvp_ref, c_ref, s_ref, o_ref):
        cs, sn = c_ref[...], s_ref[...]
        more = (pl.program_id(0) < nb - 1).astype(F32)

        def unrot(x):
            return x * cs - _swap_halves(x) * sn

        def compact(ref, j, nxt=None):
            a = ref[:, (2 * j) * LANES:(2 * j + 1) * LANES]
            b = ref[:, (2 * j + 1) * LANES:(2 * j + 2) * LANES]
            if nxt is not None:
                a = a + more * nxt[:, (2 * j) * LANES:(2 * j + 1) * LANES]
                b = b + more * nxt[:, (2 * j + 1) * LANES:(2 * j + 2) * LANES]
            return a + pltpu.roll(b, 64, 1)

        for j in range(ATTN_WIDTH // LANES):
            o_ref[:, j * LANES:(j + 1) * LANES] = unrot(compact(dq_ref, j)).astype(BF16)
        for j in range(KV_WIDTH // LANES):
            o_ref[:, (COL_K + j) * LANES:(COL_K + j + 1) * LANES] = unrot(compact(dkc_ref, j, dkp_ref)).astype(BF16)
            o_ref[:, (COL_V + j) * LANES:(COL_V + j + 1) * LANES] = compact(dvc_ref, j, dvp_ref).astype(BF16)

    cur = lambda n: (n, 0)
    nxt = lambda n: (jnp.minimum(n + 1, nb - 1), 0)
    return _pc(body, name=name, grid=(nb,),
               in_specs=[pl.BlockSpec((L, N_Q_HEADS * LANES), cur), pl.BlockSpec((L, kvw), cur), pl.BlockSpec((L, kvw), nxt),
                         pl.BlockSpec((L, kvw), cur), pl.BlockSpec((L, kvw), nxt), pl.BlockSpec((L, LANES), cur),
                         pl.BlockSpec((L, LANES), cur)],
               out_specs=pl.BlockSpec((L, COL_HQ * LANES), cur), out_shape=SDS((s, COL_HQ * LANES), BF16),
               sem=("parallel",))(dqp, dkc, dkp, dvc, dvp, cos, sin)


def _split3(x):
    a = x.astype(BF16)
    r = x - a.astype(F32)
    b = r.astype(BF16)
    c = (r - b.astype(F32)).astype(BF16)
    return a, b, c


def _chunk_sum(x, upper):
    t = x.shape[0]
    ri = lax.broadcasted_iota(jnp.int32, (t, t), 0)
    ci = lax.broadcasted_iota(jnp.int32, (t, t), 1)
    same = (ri // HGRN_CHUNK) == (ci // HGRN_CHUNK)
    tri = (ci >= ri) if upper else (ci <= ri)
    m = jnp.where(same & tri, 1.0, 0.0).astype(BF16)
    out = None
    for part in _split3(x):
        y = jnp.dot(m, part, preferred_element_type=F32)
        out = y if out is None else out + y
    return out


def _lower_bound(l_ref, layer):
    lv = l_ref[...]
    e = jnp.exp(lv - jnp.max(lv, axis=0, keepdims=True))
    sm = e / jnp.sum(e, axis=0, keepdims=True)
    s0 = sm[0:1]
    return (s0 - s0) if layer == 0 else ((s0 + sm[1:2]) - s0)


def _hgrn_gates(hq_ref, hf_ref, lb):
    z = hf_ref[...]
    sg = _sigmoid(z)
    f = lb + (1.0 - lb) * sg
    kin = (1.0 - lb) * _sigmoid(-z)
    hq = hq_ref[...]
    sq = _sigmoid(hq)
    return sg, f, kin, hq, sq


def _shift_down(x, d):
    return x if d == 0 else pltpu.roll(x, d, 0)


def _shift_up(x, d):
    return x if d == 0 else pltpu.roll(x, x.shape[0] - d, 0)


def _hgrn_specs(t, rev, nt):
    row = (lambda h, i: nt - 1 - i) if rev else (lambda h, i: i)
    col = lambda base: pl.BlockSpec((t, LANES), lambda h, i, base=base: (row(h, i), base + h))
    return col, row


def hgrn_fwd(name, proj, lb_logits, layer):
    s = proj.shape[0]
    t = _tile(s, ROW_TILE)
    nt = s // t
    nc = t // HGRN_CHUNK
    col, row = _hgrn_specs(t, False, nt)

    def body(hq_ref, hf_ref, hi_ref, l_ref, o_ref, st_ref, state, b_s, q_s, k_s, oi_s):
        @pl.when(pl.program_id(1) == 0)
        def _():
            state[...] = jnp.zeros_like(state)

        lb = _lower_bound(l_ref, layer)
        sg, f, kin, hq, sq = _hgrn_gates(hq_ref, hf_ref, lb)
        q = hq * sq
        v = hi_ref[...]
        b = _chunk_sum(jnp.log(f), False)
        tmod = lax.broadcasted_iota(jnp.int32, (t, LANES), 0) % HGRN_CHUNK
        o = jnp.sum(q * kin, axis=-1, keepdims=True) * v
        for d in range(1, HGRN_CHUNK):
            e = jnp.exp(jnp.where(tmod >= d, b - _shift_down(b, d), MASK_VALUE))
            a = jnp.sum((q * _shift_down(kin, d)) * e, axis=-1, keepdims=True)
            o = o + a * _shift_down(v, d)
        b_s[...] = b
        q_s[...] = q
        k_s[...] = kin

        def chunk(c, carry):
            r = pl.ds(pl.multiple_of(c * HGRN_CHUNK, HGRN_CHUNK), HGRN_CHUNK)
            bc = b_s[r, :]
            bl = bc[HGRN_CHUNK - 1:HGRN_CHUNK, :]
            qe = (q_s[r, :] * jnp.exp(bc)).astype(BF16)
            ke = (k_s[r, :] * jnp.exp(bl - bc)).astype(BF16)
            st = state[...]
            st_ref[c] = st
            oi_s[r, :] = lax.dot_general(qe, st.astype(BF16), _NT, preferred_element_type=F32)
            state[...] = st * jnp.exp(bl) + lax.dot_general(hi_ref[r, :].astype(BF16), ke, _TN, preferred_element_type=F32)
            return carry

        lax.fori_loop(0, nc, chunk, 0)
        o_ref[...] = o + oi_s[...]

    tile = pltpu.VMEM((t, LANES), F32)
    return _pc(body, name=name, grid=(HGRN_HEADS, nt),
               in_specs=[col(COL_HQ), col(COL_HF), col(COL_HI), pl.BlockSpec((2, LANES), lambda h, i: (0, h))],
               out_specs=[pl.BlockSpec((t, LANES), lambda h, i: (i, h)),
                          pl.BlockSpec((nc, None, LANES, LANES), lambda h, i: (i, h, 0, 0))],
               out_shape=[SDS((s, HGRN_WIDTH), F32), SDS((s // HGRN_CHUNK, HGRN_HEADS, LANES, LANES), F32)],
               scratch=[pltpu.VMEM((LANES, LANES), F32), tile, tile, tile, tile],
               sem=("parallel", "arbitrary"))(proj, proj, proj, lb_logits)


def hgrn_bwd(name, proj, lb_logits, layer, states, do):
    s = proj.shape[0]
    t = _tile(s, ROW_TILE)
    nt = s // t
    nc = t // HGRN_CHUNK
    col, row = _hgrn_specs(t, True, nt)

    def body(hq_ref, hf_ref, hi_ref, l_ref, st_ref, do_ref, dhq_ref, dhf_ref, dhi_ref, dlb_ref,
             dstate, b_s, q_s, k_s, dq_s, dk_s, dv_s, db_s):
        first = pl.program_id(1) == 0

        @pl.when(first)
        def _():
            dstate[...] = jnp.zeros_like(dstate)

        lb = _lower_bound(l_ref, layer)
        sg, f, kin, hq, sq = _hgrn_gates(hq_ref, hf_ref, lb)
        q = hq * sq
        v = hi_ref[...]
        b = _chunk_sum(jnp.log(f), False)
        dov = do_ref[...]
        b_s[...] = b
        q_s[...] = q
        k_s[...] = kin
        last_row = lax.broadcasted_iota(jnp.int32, (HGRN_CHUNK, LANES), 0) == HGRN_CHUNK - 1

        def chunk(cc, carry):
            c = nc - 1 - cc
            r = pl.ds(pl.multiple_of(c * HGRN_CHUNK, HGRN_CHUNK), HGRN_CHUNK)
            bc = b_s[r, :]
            bl = bc[HGRN_CHUNK - 1:HGRN_CHUNK, :]
            ebl = jnp.exp(bl)
            eb = jnp.exp(bc)
            ekb = jnp.exp(bl - bc)
            qe = q_s[r, :] * eb
            ke = k_s[r, :] * ekb
            doc = do_ref[r, :].astype(BF16)
            vc = hi_ref[r, :].astype(BF16)
            st = st_ref[c]
            dn = dstate[...]
            dnb = dn.astype(BF16)
            dqe = jnp.dot(doc, st.astype(BF16), preferred_element_type=F32)
            dke = jnp.dot(vc, dnb, preferred_element_type=F32)
            dv_s[r, :] = lax.dot_general(ke.astype(BF16), dnb, _NT, preferred_element_type=F32)
            dbl = jnp.sum(dn * st, axis=0, keepdims=True) * ebl + jnp.sum(dke * ke, axis=0, keepdims=True)
            dq_s[r, :] = dqe * eb
            dk_s[r, :] = dke * ekb
            db_s[r, :] = (dqe * qe - dke * ke) + jnp.where(last_row, dbl, 0.0)
            dstate[...] = dn * ebl + lax.dot_general(doc, qe.astype(BF16), _TN, preferred_element_type=F32)
            return carry

        lax.fori_loop(0, nc, chunk, 0)

        tmod = lax.broadcasted_iota(jnp.int32, (t, LANES), 0) % HGRN_CHUNK
        da = jnp.sum(dov * v, axis=-1, keepdims=True)
        a = jnp.sum(q * kin, axis=-1, keepdims=True)
        dq = dq_s[...] + da * kin
        dk = dk_s[...] + da * q
        dv = dv_s[...] + a * dov
        db = db_s[...]
        for d in range(1, HGRN_CHUNK):
            ks = _shift_down(kin, d)
            e = jnp.exp(jnp.where(tmod >= d, b - _shift_down(b, d), MASK_VALUE))
            qe = q * e
            w = qe * ks
            a = jnp.sum(w, axis=-1, keepdims=True)
            da = jnp.sum(dov * _shift_down(v, d), axis=-1, keepdims=True)
            dq = dq + da * (ks * e)
            daw = da * w
            db = db + daw - _shift_up(daw, d)
            dk = dk + _shift_up(da * qe, d)
            dv = dv + _shift_up(a * dov, d)
        dg = _chunk_sum(db, True)
        dhq_ref[...] = (dq * (sq * (1.0 + hq * (1.0 - sq)))).astype(BF16)
        dhi_ref[...] = dv.astype(BF16)
        dfk = dg / f - dk
        dhf_ref[...] = ((dfk * (1.0 - lb)) * (sg * (1.0 - sg))).astype(BF16)
        part = jnp.sum(dfk * (1.0 - sg), axis=0, keepdims=True)

        @pl.when(first)
        def _():
            dlb_ref[...] = part

        @pl.when(jnp.logical_not(first))
        def _():
            dlb_ref[...] += part

    tile = pltpu.VMEM((t, LANES), F32)
    out_col = pl.BlockSpec((t, LANES), lambda h, i: (nt - 1 - i, h))
    return _pc(body, name=name, grid=(HGRN_HEADS, nt),
               in_specs=[col(COL_HQ), col(COL_HF), col(COL_HI), pl.BlockSpec((2, LANES), lambda h, i: (0, h)),
                         pl.BlockSpec((nc, None, LANES, LANES), lambda h, i: (nt - 1 - i, h, 0, 0)), out_col],
               out_specs=[out_col, out_col, out_col, pl.BlockSpec((1, LANES), lambda h, i: (0, h))],
               out_shape=[SDS((s, HGRN_WIDTH), BF16)] * 3 + [SDS((1, HGRN_WIDTH), F32)],
               scratch=[pltpu.VMEM((LANES, LANES), F32)] + [tile] * 7,
               sem=("parallel", "arbitrary"))(proj, proj, proj, lb_logits, states, do)


def mix_out_fwd(name, attn, o, proj, g_attn, g_hgrn):
    s = attn.shape[0]
    t = _tile(s, ROW_TILE)
    half = HGRN_WIDTH // 2

    def body(a_ref, o_ref, hg0_ref, hg1_ref, ga_ref, gh_ref, c_ref):
        av = a_ref[...]
        c_ref[:, :ATTN_WIDTH] = ((av * _rstd(av)) * ga_ref[...]).astype(BF16)
        for j in range(HGRN_HEADS):
            cols = slice(j * LANES, (j + 1) * LANES)
            ov = o_ref[:, cols]
            hg_ref, hcols = (hg0_ref, cols) if j < 4 else (hg1_ref, slice((j - 4) * LANES, (j - 3) * LANES))
            hg = hg_ref[:, hcols]
            on = (ov * _rstd(ov)) * gh_ref[:, cols]
            c_ref[:, ATTN_WIDTH + j * LANES:ATTN_WIDTH + (j + 1) * LANES] = (on * (hg * _sigmoid(hg))).astype(BF16)

    return _pc(body, name=name, grid=(s // t,),
               in_specs=[_rows(t, ATTN_WIDTH), _rows(t, HGRN_WIDTH), _rows(t, half, COL_HG // 4), _rows(t, half, COL_HG // 4 + 1),
                         _fixed((1, ATTN_WIDTH)), _fixed((1, HGRN_WIDTH))],
               out_specs=_rows(t, D_MODEL), out_shape=SDS((s, D_MODEL), BF16), sem=("parallel",))(attn, o, proj, proj, g_attn, g_hgrn)


def mix_out_bwd(name, dcat, attn, o, proj, g_attn, g_hgrn):
    s = attn.shape[0]
    t = _tile(s, ROW_TILE)
    half = HGRN_WIDTH // 2

    def body(dc_ref, a_ref, o_ref, hg0_ref, hg1_ref, ga_ref, gh_ref, da_ref, do_ref, dhg_ref, dga_ref, dgh_ref, pa_s, ph_s):
        av = a_ref[...]
        r = _rstd(av)
        xh = av * r
        dyv = dc_ref[:, :ATTN_WIDTH]
        dyg = dyv * ga_ref[...]
        da_ref[...] = r * (dyg - xh * jnp.mean(dyg * xh, axis=-1, keepdims=True))
        pa_s[...] = jnp.sum(dyv * xh, axis=0, keepdims=True)
        for j in range(HGRN_HEADS):
            cols = slice(j * LANES, (j + 1) * LANES)
            ov = o_ref[:, cols]
            hg_ref, hcols = (hg0_ref, cols) if j < 4 else (hg1_ref, slice((j - 4) * LANES, (j - 3) * LANES))
            hg = hg_ref[:, hcols]
            sg = _sigmoid(hg)
            r = _rstd(ov)
            xh = ov * r
            gain = gh_ref[:, cols]
            dh = dc_ref[:, ATTN_WIDTH + j * LANES:ATTN_WIDTH + (j + 1) * LANES]
            dhg_ref[:, cols] = ((dh * (xh * gain)) * (sg * (1.0 + hg * (1.0 - sg)))).astype(BF16)
            dyv = dh * (hg * sg)
            dyg = dyv * gain
            do_ref[:, cols] = r * (dyg - xh * jnp.mean(dyg * xh, axis=-1, keepdims=True))
            ph_s[:, cols] = jnp.sum(dyv * xh, axis=0, keepdims=True)

        @pl.when(pl.program_id(0) == 0)
        def _():
            dga_ref[...] = pa_s[...]
            dgh_ref[...] = ph_s[...]

        @pl.when(pl.program_id(0) > 0)
        def _():
            dga_ref[...] += pa_s[...]
            dgh_ref[...] += ph_s[...]

    return _pc(body, name=name, grid=(s // t,),
               in_specs=[_rows(t, D_MODEL), _rows(t, ATTN_WIDTH), _rows(t, HGRN_WIDTH), _rows(t, half, COL_HG // 4),
                         _rows(t, half, COL_HG // 4 + 1), _fixed((1, ATTN_WIDTH)), _fixed((1, HGRN_WIDTH))],
               out_specs=[_rows(t, ATTN_WIDTH), _rows(t, HGRN_WIDTH), _rows(t, HGRN_WIDTH), _fixed((1, ATTN_WIDTH)),
                          _fixed((1, HGRN_WIDTH))],
               out_shape=[SDS((s, ATTN_WIDTH), F32), SDS((s, HGRN_WIDTH), F32), SDS((s, HGRN_WIDTH), BF16),
                          SDS((1, ATTN_WIDTH), F32), SDS((1, HGRN_WIDTH), F32)],
               scratch=[pltpu.VMEM((1, ATTN_WIDTH), F32), pltpu.VMEM((1, HGRN_WIDTH), F32)],
               sem=("arbitrary",))(dcat, attn, o, proj, proj, g_attn, g_hgrn)


BIG = (("w_in", 2048, 1408, 0), ("w_out", 512, 2048, 1), ("w_ffn_gate", 2048, 1408, 0), ("w_ffn_up", 2048, 1408, 0),
       ("w_ffn_down", 1408, 2048, 1), ("w_ple_gate", 512, 2048, 1), ("w_ple_proj", 256, 512, 0))
HBM_SPEC = pl.BlockSpec(memory_space=pltpu.HBM)


def cast_bf16(name, w):
    l, r, c = w.shape
    tr = _tile(r, 512)
    spec = pl.BlockSpec((None, tr, c), lambda a, i: (a, i, 0))

    def body(w_ref, o_ref):
        o_ref[...] = w_ref[...].astype(BF16)

    return _pc(body, name=name, grid=(l, r // tr), in_specs=[spec], out_specs=spec, out_shape=SDS(w.shape, BF16),
               sem=("parallel", "parallel"))(w)


def _place():
    x, y, c = lax.axis_index("x"), lax.axis_index("y"), lax.axis_index("c")
    chips = [(1 - x, y), (x, 1 - y), (1 - x, 1 - y)]
    return x, y, c, chips


def _half(ref, axis, c, rows, cols):
    if axis == 0:
        return ref.at[pl.ds(pl.multiple_of(c * (rows // 2), 16), rows // 2), :]
    return ref.at[:, pl.ds(pl.multiple_of(c * (cols // 2), LANES), cols // 2)]


def gather_weights(name, shards, layer):
    nt = len(shards)

    def body(*refs):
        srcs, dsts = refs[:nt], refs[nt:2 * nt]
        send, recv, send2, recv2, own_sem = refs[2 * nt:]
        x, y, c, chips = _place()
        me = 2 * x + y
        own, first, passed = [], [], []
        for t, (_, rows, cols, axis) in enumerate(BIG):
            cp = pltpu.make_async_copy(srcs[t].at[layer], dsts[t].at[me], own_sem.at[t])
            cp.start()
            own.append(cp)
            for k, (cx, cy) in enumerate(chips):
                cp = pltpu.make_async_remote_copy(
                    src_ref=_half(srcs[t].at[layer], axis, c, rows, cols), dst_ref=_half(dsts[t].at[me], axis, c, rows, cols),
                    send_sem=send.at[3 * t + k], recv_sem=recv.at[3 * t + k], device_id=(cx, cy, c), device_id_type=MESH)
                cp.start()
                first.append(cp)
        for t, (_, rows, cols, axis) in enumerate(BIG):
            for k, (cx, cy) in enumerate(chips):
                got = _half(dsts[t].at[2 * cx + cy], axis, c, rows, cols)
                cp = pltpu.make_async_remote_copy(src_ref=got, dst_ref=got, send_sem=send2.at[3 * t + k],
                                                  recv_sem=recv2.at[3 * t + k], device_id=(x, y, 1 - c), device_id_type=MESH)
                first[3 * t + k].wait_recv()
                cp.start()
                passed.append(cp)
        for t, (_, rows, cols, axis) in enumerate(BIG):
            for k, (cx, cy) in enumerate(chips):
                theirs = _half(dsts[t].at[2 * cx + cy], axis, 1 - c, rows, cols)
                pltpu.make_async_remote_copy(src_ref=theirs, dst_ref=theirs, send_sem=send2.at[3 * t + k],
                                             recv_sem=recv2.at[3 * t + k], device_id=(x, y, 1 - c), device_id_type=MESH).wait_recv()
        for cp in first + passed:
            cp.wait_send()
        for cp in own:
            cp.wait()

    out_shape = [SDS((N_CHIPS, rows, cols), BF16) for _, rows, cols, _ in BIG]
    n = 3 * nt
    return _pc(body, name=name, in_specs=[HBM_SPEC] * nt, out_specs=[HBM_SPEC] * nt, out_shape=out_shape,
               scratch=[pltpu.SemaphoreType.DMA((n,))] * 4 + [pltpu.SemaphoreType.DMA((nt,))])(*shards)


def reduce_to_sibling(name, grads):
    nt = len(grads)

    def body(*refs):
        srcs, dsts = refs[:nt], refs[nt:2 * nt]
        send, recv = refs[2 * nt:]
        x, y, c, _ = _place()
        cps = []
        for t in range(nt):
            cp = pltpu.make_async_remote_copy(src_ref=srcs[t].at[1 - c], dst_ref=dsts[t], send_sem=send.at[t],
                                              recv_sem=recv.at[t], device_id=(x, y, 1 - c), device_id_type=MESH)
            cp.start()
            cps.append(cp)
        for cp in cps:
            cp.wait()

    return _pc(body, name=name, in_specs=[HBM_SPEC] * nt, out_specs=[HBM_SPEC] * nt,
               out_shape=[SDS(g.shape[1:], F32) for g in grads],
               scratch=[pltpu.SemaphoreType.DMA((nt,))] * 2)(*grads)


def add_halves(name, core, grad, got):
    _, n, r, c = grad.shape
    tr = _tile(r, 256)

    def body(core_ref, g_ref, o_ref, out_ref):
        out_ref[...] = (g_ref[...] + o_ref[...]).astype(BF16)

    gs = pltpu.PrefetchScalarGridSpec(
        num_scalar_prefetch=1, grid=(n, r // tr),
        in_specs=[pl.BlockSpec((None, None, tr, c), lambda j, i, cr: (cr[0], j, i, 0)),
                  pl.BlockSpec((None, tr, c), lambda j, i, cr: (j, i, 0))],
        out_specs=pl.BlockSpec((None, tr, c), lambda j, i, cr: (j, i, 0)))
    return _pc(body, name=name, grid_spec=gs, in_specs=None, out_specs=None, out_shape=SDS((n, r, c), BF16),
               sem=("parallel", "parallel"))(core, grad, got)


def reduce_over_chips(name, parts):
    nt = len(parts)

    def body(*refs):
        srcs, dsts = refs[:nt], refs[nt:2 * nt]
        send, recv, own_sem = refs[2 * nt:]
        x, y, c, chips = _place()
        me = 2 * x + y
        cps, own = [], []
        for t in range(nt):
            cp = pltpu.make_async_copy(srcs[t].at[me], dsts[t].at[me], own_sem.at[t])
            cp.start()
            own.append(cp)
            for k, (cx, cy) in enumerate(chips):
                cp = pltpu.make_async_remote_copy(src_ref=srcs[t].at[2 * cx + cy], dst_ref=dsts[t].at[me],
                                                  send_sem=send.at[3 * t + k], recv_sem=recv.at[3 * t + k],
                                                  device_id=(cx, cy, c), device_id_type=MESH)
                cp.start()
                cps.append(cp)
        for t in range(nt):
            for k, (cx, cy) in enumerate(chips):
                slot = dsts[t].at[2 * cx + cy]
                pltpu.make_async_remote_copy(src_ref=slot, dst_ref=slot, send_sem=send.at[3 * t + k], recv_sem=recv.at[3 * t + k],
                                             device_id=(cx, cy, c), device_id_type=MESH).wait_recv()
        for cp in cps:
            cp.wait_send()
        for cp in own:
            cp.wait()

    return _pc(body, name=name, in_specs=[HBM_SPEC] * nt, out_specs=[HBM_SPEC] * nt,
               out_shape=[SDS(p.shape, BF16) for p in parts],
               scratch=[pltpu.SemaphoreType.DMA((3 * nt,))] * 2 + [pltpu.SemaphoreType.DMA((nt,))])(*parts)


def sum_chips(name, slots):
    n, r, c = slots.shape
    tr = _tile(r, 256)

    def body(s_ref, o_ref):
        acc = s_ref[0].astype(F32)
        for k in range(1, n):
            acc = acc + s_ref[k].astype(F32)
        o_ref[...] = acc

    return _pc(body, name=name, grid=(r // tr,), in_specs=[pl.BlockSpec((n, tr, c), lambda i: (0, i, 0))],
               out_specs=pl.BlockSpec((tr, c), lambda i: (i, 0)), out_shape=SDS((r, c), F32), sem=("parallel",))(slots)


def share_with_sibling(name, halves):
    nt = len(halves)

    def body(*refs):
        srcs, dsts = refs[:nt], refs[nt:2 * nt]
        send, recv, own_sem = refs[2 * nt:]
        x, y, c, _ = _place()
        cps = []
        for t in range(nt):
            cp = pltpu.make_async_copy(srcs[t], dsts[t].at[c], own_sem.at[t])
            cp.start()
            cps.append(cp)
            cp = pltpu.make_async_remote_copy(src_ref=srcs[t], dst_ref=dsts[t].at[c], send_sem=send.at[t], recv_sem=recv.at[t],
                                              device_id=(x, y, 1 - c), device_id_type=MESH)
            cp.start()
            cps.append(cp)
        for t in range(nt):
            theirs = dsts[t].at[1 - c]
            pltpu.make_async_remote_copy(src_ref=theirs, dst_ref=theirs, send_sem=send.at[t], recv_sem=recv.at[t],
                                         device_id=(x, y, 1 - c), device_id_type=MESH).wait_recv()
        for t in range(nt):
            cps[2 * t + 1].wait_send()
            cps[2 * t].wait()

    return _pc(body, name=name, in_specs=[HBM_SPEC] * nt, out_specs=[HBM_SPEC] * nt,
               out_shape=[SDS((2,) + h.shape, F32) for h in halves],
               scratch=[pltpu.SemaphoreType.DMA((nt,))] * 3)(*halves)


def _adamw(w, g, m, v):
    m = ADAM_B1 * m + (1.0 - ADAM_B1) * g
    v = ADAM_B2 * v + (1.0 - ADAM_B2) * (g * g)
    m_hat = m / (1.0 - ADAM_B1 ** ADAM_STEP)
    v_hat = v / (1.0 - ADAM_B2 ** ADAM_STEP)
    delta = -ADAM_LR * (m_hat / (jnp.sqrt(v_hat) + ADAM_EPS) + ADAM_WD * w)
    return delta, m, v


def adamw_big(name, w, m, v, g0, g1, axis):
    _, r, c = w.shape
    _, rh, ch = g0.shape
    tr = _tile(rh, 256)
    nb = rh // tr
    if axis == 0:
        wspec = pl.BlockSpec((None, tr, ch), lambda l, h, i: (l, h * nb + i, 0))
    else:
        wspec = pl.BlockSpec((None, tr, ch), lambda l, h, i: (l, i, h))
    g0spec = pl.BlockSpec((None, tr, ch), lambda l, h, i: (h * (1 - l), i * (1 - l), 0))
    g1spec = pl.BlockSpec((None, tr, ch), lambda l, h, i: (h * l, i * l, 0))

    def body(w_ref, m_ref, v_ref, g0_ref, g1_ref, go_ref, d_ref, mo_ref, vo_ref):
        def run(g_ref):
            g = g_ref[...]
            delta, mn, vn = _adamw(w_ref[...], g, m_ref[...], v_ref[...])
            go_ref[...] = g
            d_ref[...] = delta
            mo_ref[...] = mn
            vo_ref[...] = vn

        @pl.when(pl.program_id(0) == 0)
        def _():
            run(g0_ref)

        @pl.when(pl.program_id(0) == 1)
        def _():
            run(g1_ref)

    return _pc(body, name=name, grid=(2, 2, nb), in_specs=[wspec, wspec, wspec, g0spec, g1spec], out_specs=[wspec] * 4,
               out_shape=[SDS(w.shape, F32)] * 4, sem=("parallel", "parallel", "parallel"))(w, m, v, g0, g1)


SMALL = (("pre_mix_gain", 2048), ("post_mix_gain", 2048), ("pre_ffn_gain", 2048), ("post_ffn_gain", 2048), ("ple_gain", 2048),
         ("attn_out_gain", 1024), ("hgrn_out_gain", 1024), ("hgrn_lb_logits", 1024), ("attn_sinks", 128))
SMALL_ROWS = sum(2 * w // LANES for _, w in SMALL)
SMALL_PAD = -(-SMALL_ROWS // 8) * 8
LB_ROW = sum(2 * w // LANES for _, w in SMALL[:7])


def _pack_small(parts):
    rows = []
    for nm, w in SMALL:
        a = parts[nm].astype(F32)
        if a.shape[1] != w:
            a = jnp.pad(a, ((0, 0), (0, w - a.shape[1])))
        rows.append(a.reshape(2 * w // LANES, LANES))
    rows.append(jnp.zeros((SMALL_PAD - SMALL_ROWS, LANES), F32))
    return jnp.concatenate(rows, axis=0)


def _unpack_small(packed, widths):
    out, r = {}, 0
    for nm, w in SMALL:
        n = 2 * w // LANES
        out[nm] = packed[r:r + n].reshape(2, w)[:, :widths[nm]]
        r += n
    return out


def allreduce_small(name, packed):
    rows = packed.shape[0]

    def body(x_ref, o_ref, buf, send, recv, own_sem):
        x, y, c, _ = _place()
        me = 4 * x + 2 * y + c
        own = pltpu.make_async_copy(x_ref, buf.at[me], own_sem)
        own.start()
        cps = []
        for k in range(1, 8):
            px, py, pc = x ^ (k >> 2), y ^ ((k >> 1) & 1), c ^ (k & 1)
            cp = pltpu.make_async_remote_copy(src_ref=x_ref, dst_ref=buf.at[me], send_sem=send.at[k - 1], recv_sem=recv.at[k - 1],
                                              device_id=(px, py, pc), device_id_type=MESH)
            cp.start()
            cps.append(cp)
        for k in range(1, 8):
            px, py, pc = x ^ (k >> 2), y ^ ((k >> 1) & 1), c ^ (k & 1)
            slot = buf.at[4 * px + 2 * py + pc]
            pltpu.make_async_remote_copy(src_ref=slot, dst_ref=slot, send_sem=send.at[k - 1], recv_sem=recv.at[k - 1],
                                         device_id=(px, py, pc), device_id_type=MESH).wait_recv()
        for cp in cps:
            cp.wait_send()
        own.wait()
        acc = buf[0]
        for k in range(1, 8):
            acc = acc + buf[k]
        o_ref[...] = acc

    vm = pl.BlockSpec(memory_space=pltpu.VMEM)
    return _pc(body, name=name, in_specs=[vm], out_specs=vm, out_shape=SDS((rows, LANES), F32),
               scratch=[pltpu.VMEM((8, rows, LANES), F32), pltpu.SemaphoreType.DMA((7,)), pltpu.SemaphoreType.DMA((7,)),
                        pltpu.SemaphoreType.DMA])(packed)


def adamw_small(name, w, m, v, g):
    rows = w.shape[0]
    n = HGRN_WIDTH // LANES

    def body(w_ref, m_ref, v_ref, g_ref, go_ref, d_ref, mo_ref, vo_ref):
        go_ref[...] = g_ref[...]
        l0 = w_ref[LB_ROW:LB_ROW + n, :]
        l1 = w_ref[LB_ROW + n:LB_ROW + 2 * n, :]
        mx = jnp.maximum(l0, l1)
        e0, e1 = jnp.exp(l0 - mx), jnp.exp(l1 - mx)
        s0, s1 = e0 / (e0 + e1), e1 / (e0 + e1)
        dlb1 = g_ref[LB_ROW + n:LB_ROW + 2 * n, :]
        inner = s1 * dlb1
        go_ref[LB_ROW:LB_ROW + n, :] = s0 * (0.0 - inner)
        go_ref[LB_ROW + n:LB_ROW + 2 * n, :] = s1 * (dlb1 - inner)
        delta, mn, vn = _adamw(w_ref[...], go_ref[...], m_ref[...], v_ref[...])
        d_ref[...] = delta
        mo_ref[...] = mn
        vo_ref[...] = vn

    vm = pl.BlockSpec(memory_space=pltpu.VMEM)
    return _pc(body, name=name, in_specs=[vm] * 4, out_specs=[vm] * 4, out_shape=[SDS((rows, LANES), F32)] * 4)(w, m, v, g)


def _layer_fwd(l, x, h1, p_l, wts, gains, cos, sin, sinks, lb_logits, g_next, target):
    n = f"l{l}_"
    proj = mm_col(n + "in_proj", h1, wts["w_in"])
    qp, kp, vp = rope_qkv(n + "rope_qkv", proj, cos, sin)
    attn = attn_fwd(n + "attn_fwd", qp, kp, vp, sinks)
    o, states = hgrn_fwd(n + "hgrn_fwd", proj, lb_logits, l)
    cat = mix_out_fwd(n + "mix_out_fwd", attn, o, proj, gains["attn_out_gain"], gains["hgrn_out_gain"])
    m = mm_row(n + "out_proj", cat, wts["w_out"])
    x1, h2 = post_pre_norm(n + "post_mix", m, gains["post_mix_gain"], x, gains["pre_ffn_gain"])
    g = mm_col(n + "ffn_gate", h2, wts["w_ffn_gate"])
    u = mm_col(n + "ffn_up", h2, wts["w_ffn_up"])
    a = swiglu_fwd(n + "swiglu_fwd", g, u)
    f = mm_row(n + "ffn_down", a, wts["w_ffn_down"])
    x2, h3 = post_pre_norm(n + "post_ffn", f, gains["post_ffn_gain"], x1, gains["ple_gain"])
    z = mm_row(n + "ple_gate", h3, wts["w_ple_gate"])
    pp = mm_col(n + "ple_proj", p_l, wts["w_ple_proj"])
    if target is None:
        out = ple_fwd_mid(n + "ple_fwd", z, pp, x2, g_next)
    else:
        out = ple_fwd_loss(n + "ple_loss", z, pp, x2, target)
    saved = dict(x=x, h1=h1, proj=proj, qp=qp, kp=kp, vp=vp, attn=attn, o=o, states=states, cat=cat, m=m, x1=x1, h2=h2,
                 g=g, u=u, a=a, f=f, x2=x2, h3=h3, z=z, pp=pp, p=p_l)
    return out, saved


def _layer_bwd(l, dx3, sv, wts, gains, cos, sin, sinks, lb_logits):
    n = f"l{l}_"
    dpp, dz = ple_bwd(n + "ple_bwd", dx3, sv["z"], sv["pp"])
    dh3 = mm_row_t(n + "ple_gate_dx", dz, wts["w_ple_gate"])
    dx2, d_ple_gain = norm_bwd(n + "ple_norm_bwd", sv["x2"], gains["ple_gain"], dh3, dx3)
    df, d_post_ffn = norm_bwd(n + "post_ffn_bwd", sv["f"], gains["post_ffn_gain"], dx2, None, BF16)
    da = mm_row_t(n + "ffn_down_dx", df, wts["w_ffn_down"])
    dg, du = swiglu_bwd(n + "swiglu_bwd", sv["g"], sv["u"], da)
    dh2 = mm_col_t(n + "ffn_gate_dx", dg, wts["w_ffn_gate"])
    dh2 = mm_col_t(n + "ffn_up_dx", du, wts["w_ffn_up"], add=dh2)
    dx1, d_pre_ffn = norm_bwd(n + "pre_ffn_bwd", sv["x1"], gains["pre_ffn_gain"], dh2, dx2)
    dm, d_post_mix = norm_bwd(n + "post_mix_bwd", sv["m"], gains["post_mix_gain"], dx1, None, BF16)
    dcat = mm_row_t(n + "out_proj_dx", dm, wts["w_out"])
    dattn, do, dhg, d_attn_gain, d_hgrn_gain = mix_out_bwd(n + "mix_out_bwd", dcat, sv["attn"], sv["o"], sv["proj"],
                                                            gains["attn_out_gain"], gains["hgrn_out_gain"])
    dqp, dkc, dkp, dvc, dvp, dsinks = attn_bwd(n + "attn_bwd", sv["qp"], sv["kp"], sv["vp"], sinks, dattn)
    dqkv = rope_bwd(n + "rope_bwd", dqp, dkc, dkp, dvc, dvp, cos, sin)
    dhq, dhf, dhi, dlb = hgrn_bwd(n + "hgrn_bwd", sv["proj"], lb_logits, l, sv["states"], do)
    dproj = jnp.concatenate([dqkv, dhq, dhf, dhi, dhg], axis=1)
    dh1 = mm_col_t(n + "in_proj_dx", dproj, wts["w_in"])
    dx, d_pre_mix = norm_bwd(n + "pre_mix_bwd", sv["x"], gains["pre_mix_gain"], dh1, dx1)
    big = dict(
        w_in=mm_wg_col(n + "in_proj_dw", sv["h1"], dproj),
        w_out=mm_wg_row(n + "out_proj_dw", sv["cat"], dm),
        w_ffn_gate=mm_wg_col(n + "ffn_gate_dw", sv["h2"], dg),
        w_ffn_up=mm_wg_col(n + "ffn_up_dw", sv["h2"], du),
        w_ffn_down=mm_wg_row(n + "ffn_down_dw", sv["a"], df),
        w_ple_gate=mm_wg_row(n + "ple_gate_dw", sv["h3"], dz),
        w_ple_proj=mm_wg_col(n + "ple_proj_dw", sv["p"], dpp),
    )
    small = dict(pre_mix_gain=d_pre_mix, post_mix_gain=d_post_mix, pre_ffn_gain=d_pre_ffn, post_ffn_gain=d_post_ffn,
                 ple_gain=d_ple_gain, attn_out_gain=d_attn_gain, hgrn_out_gain=d_hgrn_gain, hgrn_lb_logits=dlb,
                 attn_sinks=dsinks)
    return dx, big, small


def _reduce_big(l, core, big):
    n = f"l{l}_"
    names = [nm for nm, *_ in BIG]
    got = reduce_to_sibling(n + "reduce_to_sibling", [big[nm] for nm in names])
    parts = [add_halves(n + "add_" + nm, core, big[nm], gt) for nm, gt in zip(names, got)]
    slots = reduce_over_chips(n + "reduce_over_chips", parts)
    halves = [sum_chips(n + "sum_" + nm, sl) for nm, sl in zip(names, slots)]
    return dict(zip(names, share_with_sibling(n + "share_with_sibling", halves)))


def kernel(x, p, positions, w_in, attn_sinks, hgrn_lb_logits, attn_out_gain, hgrn_out_gain, w_out, pre_mix_gain, post_mix_gain, pre_ffn_gain, post_ffn_gain, w_ffn_gate, w_ffn_up, w_ffn_down, ple_gain, w_ple_gate, w_ple_proj, loss_target, m_w_in, m_attn_sinks, m_hgrn_lb_logits, m_attn_out_gain, m_hgrn_out_gain, m_w_out, m_pre_mix_gain, m_post_mix_gain, m_pre_ffn_gain, m_post_ffn_gain, m_w_ffn_gate, m_w_ffn_up, m_w_ffn_down, m_ple_gain, m_w_ple_gate, m_w_ple_proj, v_w_in, v_attn_sinks, v_hgrn_lb_logits, v_attn_out_gain, v_hgrn_out_gain, v_w_out, v_pre_mix_gain, v_post_mix_gain, v_pre_ffn_gain, v_post_ffn_gain, v_w_ffn_gate, v_w_ffn_up, v_w_ffn_down, v_ple_gain, v_w_ple_gate, v_w_ple_proj):
    given = dict(locals())
    depth = 2
    core = lax.axis_index("c").astype(jnp.int32).reshape(1)
    xs = x[0]
    tgt = loss_target[0]
    pos_col = positions.reshape(-1, 1)
    half = 32
    inv_freq = ROPE_THETA ** (-jnp.arange(half, dtype=F32) / half)
    inv_freq = jnp.tile(inv_freq, 4).reshape(1, LANES)
    sinks_pad = jnp.pad(attn_sinks, ((0, 0), (0, LANES - attn_sinks.shape[1])))

    shards = [cast_bf16("cast_" + nm, given[nm]) for nm, *_ in BIG]
    gathered = [gather_weights(f"l{l}_gather_weights", shards, l) for l in range(depth)]
    wts = [dict(zip([nm for nm, *_ in BIG], gw)) for gw in gathered]
    gains = [{nm: given[nm][l:l + 1] for nm, _ in SMALL[:7]} for l in range(depth)]

    cos, sin = rope_tables("rope_tables", pos_col, inv_freq)
    h1 = pre_norm("l0_pre_mix", xs, gains[0]["pre_mix_gain"])
    (x_mid, h1_next), sv0 = _layer_fwd(0, xs, h1, p[0, 0], wts[0], gains[0], cos, sin, attn_sinks[0], hgrn_lb_logits,
                                       gains[1]["pre_mix_gain"], None)
    (dy, loss_part), sv1 = _layer_fwd(1, x_mid, h1_next, p[1, 0], wts[1], gains[1], cos, sin, attn_sinks[1], hgrn_lb_logits,
                                      None, tgt)
    dx_mid, big1, small1 = _layer_bwd(1, dy, sv1, wts[1], gains[1], cos, sin, attn_sinks[1], hgrn_lb_logits)
    red1 = _reduce_big(1, core, big1)
    dx0, big0, small0 = _layer_bwd(0, dx_mid, sv0, wts[0], gains[0], cos, sin, attn_sinks[0], hgrn_lb_logits)
    red0 = _reduce_big(0, core, big0)

    loss = lax.psum(loss_part[0, 0], ("x", "y", "c"))
    grad_x = dx0[None]

    out_big = {}
    for nm, _, _, axis in BIG:
        out_big[nm] = adamw_big("adamw_" + nm, given[nm], given["m_" + nm], given["v_" + nm], red0[nm], red1[nm], axis)

    widths = {nm: given[nm].shape[1] for nm, _ in SMALL}
    small_g = {nm: jnp.concatenate([small0[nm][:, :widths[nm]] if nm != "attn_sinks" else small0[nm][:, :LANES],
                                    small1[nm][:, :widths[nm]] if nm != "attn_sinks" else small1[nm][:, :LANES]], axis=0)
               for nm, _ in SMALL}
    g_sum = allreduce_small("allreduce_small", _pack_small(small_g))
    sm = adamw_small("adamw_small", _pack_small({nm: given[nm] for nm, _ in SMALL}),
                     _pack_small({nm: given["m_" + nm] for nm, _ in SMALL}),
                     _pack_small({nm: given["v_" + nm] for nm, _ in SMALL}), g_sum)
    out_small = [_unpack_small(a, widths) for a in sm]

    order = ["w_in", "attn_sinks", "hgrn_lb_logits", "attn_out_gain", "hgrn_out_gain", "w_out", "pre_mix_gain", "post_mix_gain",
             "pre_ffn_gain", "post_ffn_gain", "w_ffn_gate", "w_ffn_up", "w_ffn_down", "ple_gain", "w_ple_gate", "w_ple_proj"]
    res = [loss, grad_x]
    for k in range(4):
        for nm in order:
            res.append(out_big[nm][k] if nm in out_big else out_small[k][nm])
    return tuple(res)
```

```python
import functools

import jax
import jax.numpy as jnp
from jax import lax
from jax.experimental import pallas as pl
from jax.experimental.pallas import tpu as pltpu

F32, BF16 = jnp.float32, jnp.bfloat16
SDS = jax.ShapeDtypeStruct
MESH = pl.DeviceIdType.MESH

D_MODEL = 2048
ATTN_WIDTH = 1024
HGRN_WIDTH = 1024
KV_WIDTH = 256
N_Q_HEADS = 16
N_KV_HEADS = 4
Q_PER_KV = 4
WINDOW = 128
MASK_VALUE = -1e30
ROPE_THETA = 10000.0
HGRN_HEADS = 8
HGRN_CHUNK = 16
D_FF = 5632
D_PLE = 256
RMS_EPS = 1e-6
LANES = 128
N_CHIPS = 4
COL_Q, COL_K, COL_V, COL_HQ, COL_HF, COL_HI, COL_HG = 0, 8, 10, 12, 20, 28, 36

ADAM_LR, ADAM_B1, ADAM_B2, ADAM_EPS, ADAM_WD, ADAM_STEP = 0.001, 0.9, 0.999, 1e-08, 0.01, 10

VMEM_LIMIT = 56 * 1024 * 1024
ROW_TILE = 256

_NN = (((1,), (0,)), ((), ()))
_NT = (((1,), (1,)), ((), ()))
_TN = (((0,), (0,)), ((), ()))


def _pc(body, *, name, out_shape, in_specs, out_specs, grid=(), scratch=(), sem=None, grid_spec=None, **kw):
    params = dict(vmem_limit_bytes=VMEM_LIMIT)
    if sem is not None:
        params["dimension_semantics"] = sem
    if grid_spec is not None:
        return pl.pallas_call(body, name=name, out_shape=out_shape, grid_spec=grid_spec,
                              compiler_params=pltpu.CompilerParams(**params), **kw)
    return pl.pallas_call(body, name=name, out_shape=out_shape, grid=grid, in_specs=in_specs, out_specs=out_specs,
                          scratch_shapes=list(scratch), compiler_params=pltpu.CompilerParams(**params), **kw)


def _sigmoid(x):
    return 1.0 / (1.0 + jnp.exp(-x))


def _rstd(x):
    return lax.rsqrt(jnp.mean(x * x, axis=-1, keepdims=True) + RMS_EPS)


def _rows(t, w, col=0):
    return pl.BlockSpec((t, w), lambda i, col=col: (i, col))


def _fixed(shape):
    return pl.BlockSpec(shape, lambda *_: (0,) * len(shape))


def _mm(name, a, b, *, dims, grid, a_spec, b_spec, o_spec, out_shape, acc_shape=None, kaxis=None, add=None,
        add_spec=None, sem=None):
    nk = grid[kaxis] if kaxis is not None else 1

    def body(*refs):
        if add is None:
            a_ref, b_ref, o_ref, *scr = refs
            add_ref = None
        else:
            a_ref, b_ref, add_ref, o_ref, *scr = refs
        r = lax.dot_general(a_ref[...].astype(BF16), b_ref[...].astype(BF16), dims, preferred_element_type=F32)
        if nk == 1:
            if add_ref is not None:
                r = r + add_ref[...]
            o_ref[...] = r.astype(o_ref.dtype)
            return
        acc = scr[0]
        k = pl.program_id(kaxis)

        @pl.when(k == 0)
        def _():
            acc[...] = r

        @pl.when(k > 0)
        def _():
            acc[...] += r

        @pl.when(k == nk - 1)
        def _():
            t = acc[...]
            if add_ref is not None:
                t = t + add_ref[...]
            o_ref[...] = t.astype(o_ref.dtype)

    ins = [a, b] + ([] if add is None else [add])
    specs = [a_spec, b_spec] + ([] if add is None else [add_spec])
    scratch = [] if nk == 1 else [pltpu.VMEM(acc_shape, F32)]
    return _pc(body, name=name, grid=grid, in_specs=specs, out_specs=o_spec, out_shape=out_shape, scratch=scratch,
               sem=sem)(*ins)


def _tile(n, t):
    if n <= t:
        return n
    while n % t:
        t //= 2
    assert t % 8 == 0
    return t


def mm_col(name, a, wg, out_dtype=F32):
    s, k = a.shape
    _, _, n = wg.shape
    tm = _tile(s, 512)
    return _mm(name, a, wg, dims=_NN, grid=(N_CHIPS, s // tm),
               a_spec=pl.BlockSpec((tm, k), lambda j, i: (i, 0)),
               b_spec=pl.BlockSpec((None, k, n), lambda j, i: (j, 0, 0)),
               o_spec=pl.BlockSpec((tm, n), lambda j, i: (i, j)),
               out_shape=SDS((s, N_CHIPS * n), out_dtype), sem=("parallel", "parallel"))


def mm_row(name, a, wg, out_dtype=F32):
    s, _ = a.shape
    _, r, n = wg.shape
    tm = _tile(s, 512)
    return _mm(name, a, wg, dims=_NN, grid=(s // tm, N_CHIPS), kaxis=1,
               a_spec=pl.BlockSpec((tm, r), lambda i, k: (i, k)),
               b_spec=pl.BlockSpec((None, r, n), lambda i, k: (k, 0, 0)),
               o_spec=pl.BlockSpec((tm, n), lambda i, k: (i, 0)),
               out_shape=SDS((s, n), out_dtype), acc_shape=(tm, n), sem=("parallel", "arbitrary"))


def mm_col_t(name, dy, wg, add=None, out_dtype=F32):
    s, _ = dy.shape
    _, k, n = wg.shape
    tm = _tile(s, 512)
    return _mm(name, dy, wg, dims=_NT, grid=(s // tm, N_CHIPS), kaxis=1,
               a_spec=pl.BlockSpec((tm, n), lambda i, kk: (i, kk)),
               b_spec=pl.BlockSpec((None, k, n), lambda i, kk: (kk, 0, 0)),
               o_spec=pl.BlockSpec((tm, k), lambda i, kk: (i, 0)),
               add=add, add_spec=pl.BlockSpec((tm, k), lambda i, kk: (i, 0)),
               out_shape=SDS((s, k), out_dtype), acc_shape=(tm, k), sem=("parallel", "arbitrary"))


def mm_row_t(name, dy, wg, out_dtype=F32):
    s, n = dy.shape
    _, r, _ = wg.shape
    tm = _tile(s, 512)
    return _mm(name, dy, wg, dims=_NT, grid=(N_CHIPS, s // tm),
               a_spec=pl.BlockSpec((tm, n), lambda j, i: (i, 0)),
               b_spec=pl.BlockSpec((None, r, n), lambda j, i: (j, 0, 0)),
               o_spec=pl.BlockSpec((tm, r), lambda j, i: (i, j)),
               out_shape=SDS((s, N_CHIPS * r), out_dtype), sem=("parallel", "parallel"))


def mm_wg_col(name, a, dy):
    s, k = a.shape
    n = dy.shape[1] // N_CHIPS
    tm = _tile(k // 2, 1024)
    ts = _tile(s, 1024)
    hb = (k // 2) // tm
    return _mm(name, a, dy, dims=_TN, grid=(N_CHIPS, k // tm, s // ts), kaxis=2,
               a_spec=pl.BlockSpec((ts, tm), lambda j, i, t: (t, i)),
               b_spec=pl.BlockSpec((ts, n), lambda j, i, t: (t, j)),
               o_spec=pl.BlockSpec((None, None, tm, n), lambda j, i, t: (i // hb, j, i % hb, 0)),
               out_shape=SDS((2, N_CHIPS, k // 2, n), F32), acc_shape=(tm, n),
               sem=("parallel", "parallel", "arbitrary"))


def mm_wg_row(name, a, dy):
    s, n = dy.shape
    r = a.shape[1] // N_CHIPS
    tn = _tile(n // 2, 1024)
    ts = _tile(s, 1024)
    nb = (n // 2) // tn
    return _mm(name, a, dy, dims=_TN, grid=(N_CHIPS, n // tn, s // ts), kaxis=2,
               a_spec=pl.BlockSpec((ts, r), lambda j, i, t: (t, j)),
               b_spec=pl.BlockSpec((ts, tn), lambda j, i, t: (t, i)),
               o_spec=pl.BlockSpec((None, None, r, tn), lambda j, i, t: (i // nb, j, 0, i % nb)),
               out_shape=SDS((2, N_CHIPS, r, n // 2), F32), acc_shape=(r, tn),
               sem=("parallel", "parallel", "arbitrary"))


def pre_norm(name, x, gain):
    s, d = x.shape
    t = _tile(s, ROW_TILE)

    def body(x_ref, g_ref, o_ref):
        xv = x_ref[...]
        o_ref[...] = ((xv * _rstd(xv)) * g_ref[...]).astype(BF16)

    return _pc(body, name=name, grid=(s // t,), in_specs=[_rows(t, d), _fixed((1, d))], out_specs=_rows(t, d),
               out_shape=SDS((s, d), BF16), sem=("parallel",))(x, gain)


def post_pre_norm(name, m, g_post, res, g_pre):
    s, d = m.shape
    t = _tile(s, ROW_TILE)

    def body(m_ref, gp_ref, r_ref, gn_ref, x_ref, h_ref):
        mv = m_ref[...]
        xn = r_ref[...] + (mv * _rstd(mv)) * gp_ref[...]
        x_ref[...] = xn
        h_ref[...] = ((xn * _rstd(xn)) * gn_ref[...]).astype(BF16)

    return _pc(body, name=name, grid=(s // t,),
               in_specs=[_rows(t, d), _fixed((1, d)), _rows(t, d), _fixed((1, d))],
               out_specs=[_rows(t, d), _rows(t, d)], out_shape=[SDS((s, d), F32), SDS((s, d), BF16)],
               sem=("parallel",))(m, g_post, res, g_pre)


def ple_fwd_mid(name, z, pp, x2, g_next):
    s, d = z.shape
    t = _tile(s, ROW_TILE)

    def body(z_ref, p_ref, x_ref, g_ref, xo_ref, h_ref):
        xn = x_ref[...] + p_ref[...] * _sigmoid(z_ref[...])
        xo_ref[...] = xn
        h_ref[...] = ((xn * _rstd(xn)) * g_ref[...]).astype(BF16)

    return _pc(body, name=name, grid=(s // t,),
               in_specs=[_rows(t, d), _rows(t, d), _rows(t, d), _fixed((1, d))],
               out_specs=[_rows(t, d), _rows(t, d)], out_shape=[SDS((s, d), F32), SDS((s, d), BF16)],
               sem=("parallel",))(z, pp, x2, g_next)


def ple_fwd_loss(name, z, pp, x2, target):
    s, d = z.shape
    t = _tile(s, ROW_TILE)

    def body(z_ref, p_ref, x_ref, t_ref, dy_ref, l_ref):
        err = (x_ref[...] + p_ref[...] * _sigmoid(z_ref[...])) - t_ref[...]
        dy_ref[...] = err * (1.0 / d)
        part = jnp.sum(jnp.sum(err * err, axis=-1, keepdims=True), axis=0, keepdims=True) * (0.5 / d)

        @pl.when(pl.program_id(0) == 0)
        def _():
            l_ref[...] = part

        @pl.when(pl.program_id(0) > 0)
        def _():
            l_ref[...] += part

    return _pc(body, name=name, grid=(s // t,),
               in_specs=[_rows(t, d), _rows(t, d), _rows(t, d), _rows(t, d)],
               out_specs=[_rows(t, d), _fixed((1, 1))], out_shape=[SDS((s, d), F32), SDS((1, 1), F32)],
               sem=("arbitrary",))(z, pp, x2, target)


def ple_bwd(name, dx3, z, pp):
    s, d = z.shape
    t = _tile(s, ROW_TILE)

    def body(d_ref, z_ref, p_ref, dpp_ref, dz_ref):
        gate = _sigmoid(z_ref[...])
        dv = d_ref[...]
        dpp_ref[...] = (dv * gate).astype(BF16)
        dz_ref[...] = ((dv * p_ref[...]) * (gate * (1.0 - gate))).astype(BF16)

    return _pc(body, name=name, grid=(s // t,), in_specs=[_rows(t, d)] * 3, out_specs=[_rows(t, d)] * 2,
               out_shape=[SDS((s, d), BF16)] * 2, sem=("parallel",))(dx3, z, pp)


def norm_bwd(name, xin, gain, dy, dres=None, out_dtype=F32):
    s, d = xin.shape
    t = _tile(s, ROW_TILE)

    def body(*refs):
        if dres is None:
            x_ref, g_ref, dy_ref, dx_ref, dg_ref = refs
            r_ref = None
        else:
            x_ref, g_ref, dy_ref, r_ref, dx_ref, dg_ref = refs
        xv = x_ref[...]
        r = _rstd(xv)
        xh = xv * r
        dyv = dy_ref[...].astype(F32)
        dyg = dyv * g_ref[...]
        c = jnp.mean(dyg * xh, axis=-1, keepdims=True)
        dx = r * (dyg - xh * c)
        if r_ref is not None:
            dx = dx + r_ref[...]
        dx_ref[...] = dx.astype(out_dtype)
        part = jnp.sum(dyv * xh, axis=0, keepdims=True)

        @pl.when(pl.program_id(0) == 0)
        def _():
            dg_ref[...] = part

        @pl.when(pl.program_id(0) > 0)
        def _():
            dg_ref[...] += part

    ins = [xin, gain, dy] + ([] if dres is None else [dres])
    specs = [_rows(t, d), _fixed((1, d)), _rows(t, d)] + ([] if dres is None else [_rows(t, d)])
    return _pc(body, name=name, grid=(s // t,), in_specs=specs, out_specs=[_rows(t, d), _fixed((1, d))],
               out_shape=[SDS((s, d), out_dtype), SDS((1, d), F32)], sem=("arbitrary",))(*ins)


def swiglu_fwd(name, g, u):
    s, f = g.shape
    t = _tile(s, ROW_TILE)
    w = f // N_CHIPS
    spec = pl.BlockSpec((t, w), lambda i, j: (i, j))

    def body(g_ref, u_ref, a_ref):
        gv = g_ref[...]
        a_ref[...] = ((gv * _sigmoid(gv)) * u_ref[...]).astype(BF16)

    return _pc(body, name=name, grid=(s // t, N_CHIPS), in_specs=[spec, spec], out_specs=spec,
               out_shape=SDS((s, f), BF16), sem=("parallel", "parallel"))(g, u)


def swiglu_bwd(name, g, u, da):
    s, f = g.shape
    t = _tile(s, ROW_TILE)
    w = f // N_CHIPS
    spec = pl.BlockSpec((t, w), lambda i, j: (i, j))

    def body(g_ref, u_ref, da_ref, dg_ref, du_ref):
        gv = g_ref[...]
        sg = _sigmoid(gv)
        dav = da_ref[...]
        du_ref[...] = (dav * (gv * sg)).astype(BF16)
        dg_ref[...] = ((dav * u_ref[...]) * (sg * (1.0 + gv * (1.0 - sg)))).astype(BF16)

    return _pc(body, name=name, grid=(s // t, N_CHIPS), in_specs=[spec] * 3, out_specs=[spec] * 2,
               out_shape=[SDS((s, f), BF16)] * 2, sem=("parallel", "parallel"))(g, u, da)


def _lane(shape):
    return lax.broadcasted_iota(jnp.int32, shape, 1)


def _swap_halves(x):
    lo = (_lane(x.shape) % 64) < 32
    return jnp.where(lo, pltpu.roll(x, 96, 1), pltpu.roll(x, 32, 1))


def rope_tables(name, pos_col, inv_freq):
    s = pos_col.shape[0]
    t = _tile(s, ROW_TILE)

    def body(p_ref, f_ref, c_ref, s_ref):
        ang = p_ref[...].astype(F32) * f_ref[...]
        lo = (_lane(ang.shape) % 64) < 32
        c_ref[...] = jnp.cos(ang)
        sn = jnp.sin(ang)
        s_ref[...] = jnp.where(lo, -sn, sn)

    return _pc(body, name=name, grid=(s // t,), in_specs=[_rows(t, 1), _fixed((1, LANES))],
               out_specs=[_rows(t, LANES)] * 2, out_shape=[SDS((s, LANES), F32)] * 2, sem=("parallel",))(pos_col, inv_freq)


def _pad_heads(chunk, lo_mask):
    zero = jnp.zeros_like(chunk)
    return jnp.where(lo_mask, chunk, zero), jnp.where(lo_mask, pltpu.roll(chunk, 64, 1), zero)


def rope_qkv(name, proj, cos, sin):
    s = proj.shape[0]
    t = _tile(s, ROW_TILE)

    def body(q_ref, kv_ref, c_ref, s_ref, qp_ref, kp_ref, vp_ref):
        cs, sn = c_ref[...], s_ref[...]
        lo_mask = _lane(cs.shape) < 64

        def rot(x):
            return x * cs + _swap_halves(x) * sn

        for j in range(ATTN_WIDTH // LANES):
            a, b = _pad_heads(rot(q_ref[:, j * LANES:(j + 1) * LANES]), lo_mask)
            qp_ref[:, (2 * j) * LANES:(2 * j + 1) * LANES] = a.astype(BF16)
            qp_ref[:, (2 * j + 1) * LANES:(2 * j + 2) * LANES] = b.astype(BF16)
        for j in range(KV_WIDTH // LANES):
            a, b = _pad_heads(rot(kv_ref[:, j * LANES:(j + 1) * LANES]), lo_mask)
            kp_ref[:, (2 * j) * LANES:(2 * j + 1) * LANES] = a.astype(BF16)
            kp_ref[:, (2 * j + 1) * LANES:(2 * j + 2) * LANES] = b.astype(BF16)
            a, b = _pad_heads(kv_ref[:, KV_WIDTH + j * LANES:KV_WIDTH + (j + 1) * LANES], lo_mask)
            vp_ref[:, (2 * j) * LANES:(2 * j + 1) * LANES] = a.astype(BF16)
            vp_ref[:, (2 * j + 1) * LANES:(2 * j + 2) * LANES] = b.astype(BF16)

    return _pc(body, name=name, grid=(s // t,),
               in_specs=[_rows(t, ATTN_WIDTH, 0), _rows(t, 2 * KV_WIDTH, 2), _rows(t, LANES), _rows(t, LANES)],
               out_specs=[_rows(t, N_Q_HEADS * LANES), _rows(t, N_KV_HEADS * LANES), _rows(t, N_KV_HEADS * LANES)],
               out_shape=[SDS((s, N_Q_HEADS * LANES), BF16), SDS((s, N_KV_HEADS * LANES), BF16),
                          SDS((s, N_KV_HEADS * LANES), BF16)],
               sem=("parallel",))(proj, proj, cos, sin)


def _attn_mask(n):
    L = WINDOW
    qi = lax.broadcasted_iota(jnp.int32, (L, 2 * L), 0) + L
    ki = lax.broadcasted_iota(jnp.int32, (L, 2 * L), 1)
    rel = qi - ki
    return (rel >= 0) & (rel < WINDOW) & ((n > 0) | (ki >= L))


def _attn_probs(qh, kk, valid, sink):
    sc = lax.dot_general(qh, kk, _NT, preferred_element_type=F32) * 0.125
    sc = jnp.where(valid, sc, MASK_VALUE)
    m = jnp.maximum(jnp.max(sc, axis=-1, keepdims=True), sink)
    e = jnp.exp(sc - m)
    es = jnp.exp(sink - m)
    den = jnp.sum(e, axis=-1, keepdims=True) + es
    return e / den, es / den


def _attn_specs(s):
    L = WINDOW
    cur = lambda n: (n, 0)
    prev = lambda n: (jnp.maximum(n - 1, 0), 0)
    kvw = N_KV_HEADS * LANES
    return [pl.BlockSpec((L, N_Q_HEADS * LANES), cur), pl.BlockSpec((L, kvw), cur), pl.BlockSpec((L, kvw), prev),
            pl.BlockSpec((L, kvw), cur), pl.BlockSpec((L, kvw), prev), pl.BlockSpec(memory_space=pltpu.SMEM)]


def attn_fwd(name, qp, kp, vp, sinks):
    s = qp.shape[0]
    L = WINDOW

    def body(q_ref, kc_ref, kp_ref, vc_ref, vp_ref, sk_ref, o_ref):
        valid = _attn_mask(pl.program_id(0))
        outs = []
        for kvh in range(N_KV_HEADS):
            cols = slice(kvh * LANES, (kvh + 1) * LANES)
            kk = jnp.concatenate([kp_ref[:, cols], kc_ref[:, cols]], axis=0)
            vv = jnp.concatenate([vp_ref[:, cols], vc_ref[:, cols]], axis=0)
            for g in range(Q_PER_KV):
                h = kvh * Q_PER_KV + g
                p, _ = _attn_probs(q_ref[:, h * LANES:(h + 1) * LANES], kk, valid, sk_ref[h])
                outs.append(jnp.dot(p.astype(BF16), vv, preferred_element_type=F32))
        for j in range(ATTN_WIDTH // LANES):
            o_ref[:, j * LANES:(j + 1) * LANES] = outs[2 * j] + pltpu.roll(outs[2 * j + 1], 64, 1)

    return _pc(body, name=name, grid=(s // L,), in_specs=_attn_specs(s),
               out_specs=pl.BlockSpec((L, ATTN_WIDTH), lambda n: (n, 0)),
               out_shape=SDS((s, ATTN_WIDTH), F32), sem=("parallel",))(qp, kp, kp, vp, vp, sinks)


def attn_bwd(name, qp, kp, vp, sinks, dattn):
    s = qp.shape[0]
    L = WINDOW
    kvw = N_KV_HEADS * LANES

    def body(q_ref, kc_ref, kp_ref, vc_ref, vp_ref, sk_ref, do_ref, dq_ref, dkc_ref, dkp_ref, dvc_ref, dvp_ref, ds_ref):
        n = pl.program_id(0)
        valid = _attn_mask(n)
        lo_mask = _lane((L, LANES)) < 64
        lane1 = _lane((1, LANES))
        dsink = jnp.zeros((1, LANES), F32)
        for kvh in range(N_KV_HEADS):
            cols = slice(kvh * LANES, (kvh + 1) * LANES)
            kk = jnp.concatenate([kp_ref[:, cols], kc_ref[:, cols]], axis=0)
            vv = jnp.concatenate([vp_ref[:, cols], vc_ref[:, cols]], axis=0)
            dkk = jnp.zeros((2 * L, LANES), F32)
            dvv = jnp.zeros((2 * L, LANES), F32)
            for g in range(Q_PER_KV):
                h = kvh * Q_PER_KV + g
                qh = q_ref[:, h * LANES:(h + 1) * LANES]
                p, ps = _attn_probs(qh, kk, valid, sk_ref[h])
                chunk = do_ref[:, (h // 2) * LANES:(h // 2 + 1) * LANES]
                if h % 2:
                    chunk = pltpu.roll(chunk, 64, 1)
                doh = jnp.where(lo_mask, chunk, 0.0).astype(BF16)
                dp = lax.dot_general(doh, vv, _NT, preferred_element_type=F32)
                delta = jnp.sum(p * dp, axis=-1, keepdims=True)
                dsb = ((p * (dp - delta)) * 0.125).astype(BF16)
                dsink = dsink + jnp.where(lane1 == h, -jnp.sum(ps * delta, axis=0, keepdims=True), 0.0)
                dq_ref[:, h * LANES:(h + 1) * LANES] = jnp.dot(dsb, kk, preferred_element_type=F32)
                dkk = dkk + lax.dot_general(dsb, qh, _TN, preferred_element_type=F32)
                dvv = dvv + lax.dot_general(p.astype(BF16), doh, _TN, preferred_element_type=F32)
            dkp_ref[:, cols] = dkk[:L]
            dkc_ref[:, cols] = dkk[L:]
            dvp_ref[:, cols] = dvv[:L]
            dvc_ref[:, cols] = dvv[L:]

        @pl.when(n == 0)
        def _():
            ds_ref[...] = dsink

        @pl.when(n > 0)
        def _():
            ds_ref[...] += dsink

    blk = lambda w: pl.BlockSpec((L, w), lambda n: (n, 0))
    return _pc(body, name=name, grid=(s // L,), in_specs=_attn_specs(s) + [blk(ATTN_WIDTH)],
               out_specs=[blk(N_Q_HEADS * LANES), blk(kvw), blk(kvw), blk(kvw), blk(kvw), _fixed((1, LANES))],
               out_shape=[SDS((s, N_Q_HEADS * LANES), F32)] + [SDS((s, kvw), F32)] * 4 + [SDS((1, LANES), F32)],
               sem=("arbitrary",))(qp, kp, kp, vp, vp, sinks, dattn)


def rope_bwd(name, dqp, dkc, dkp, dvc, dvp, cos, sin):
    s = dqp.shape[0]
    L = WINDOW
    nb = s // L
    kvw = N_KV_HEADS * LANES

    def body(dq_ref, dkc_ref, dkp_ref, dvc_ref, dvp_ref, c_ref, s_ref, o_ref):
        cs, sn = c_ref[...], s_ref[...]
        more = (pl.program_id(0) < nb - 1).astype(F32)

        def unrot(x):
            return x * cs - _swap_halves(x) * sn

        def compact(ref, j, nxt=None):
            a = ref[:, (2 * j) * LANES:(2 * j + 1) * LANES]
            b = ref[:, (2 * j + 1) * LANES:(2 * j + 2) * LANES]
            if nxt is not None:
                a = a + more * nxt[:, (2 * j) * LANES:(2 * j + 1) * LANES]
                b = b + more * nxt[:, (2 * j + 1) * LANES:(2 * j + 2) * LANES]
            return a + pltpu.roll(b, 64, 1)

        for j in range(ATTN_WIDTH // LANES):
            o_ref[:, j * LANES:(j + 1) * LANES] = unrot(compact(dq_ref, j)).astype(BF16)
        for j in range(KV_WIDTH // LANES):
            o_ref[:, (COL_K + j) * LANES:(COL_K + j + 1) * LANES] = unrot(compact(dkc_ref, j, dkp_ref)).astype(BF16)
            o_ref[:, (COL_V + j) * LANES:(COL_V + j + 1) * LANES] = compact(dvc_ref, j, dvp_ref).astype(BF16)

    cur = lambda n: (n, 0)
    nxt = lambda n: (jnp.minimum(n + 1, nb - 1), 0)
    return _pc(body, name=name, grid=(nb,),
               in_specs=[pl.BlockSpec((L, N_Q_HEADS * LANES), cur), pl.BlockSpec((L, kvw), cur), pl.BlockSpec((L, kvw), nxt),
                         pl.BlockSpec((L, kvw), cur), pl.BlockSpec((L, kvw), nxt), pl.BlockSpec((L, LANES), cur),
                         pl.BlockSpec((L, LANES), cur)],
               out_specs=pl.BlockSpec((L, COL_HQ * LANES), cur), out_shape=SDS((s, COL_HQ * LANES), BF16),
               sem=("parallel",))(dqp, dkc, dkp, dvc, dvp, cos, sin)


def _split3(x):
    a = x.astype(BF16)
    r = x - a.astype(F32)
    b = r.astype(BF16)
    c = (r - b.astype(F32)).astype(BF16)
    return a, b, c


def _chunk_sum(x, upper):
    t = x.shape[0]
    ri = lax.broadcasted_iota(jnp.int32, (t, t), 0)
    ci = lax.broadcasted_iota(jnp.int32, (t, t), 1)
    same = (ri // HGRN_CHUNK) == (ci // HGRN_CHUNK)
    tri = (ci >= ri) if upper else (ci <= ri)
    m = jnp.where(same & tri, 1.0, 0.0).astype(BF16)
    out = None
    for part in _split3(x):
        y = jnp.dot(m, part, preferred_element_type=F32)
        out = y if out is None else out + y
    return out


def _lower_bound(l_ref, layer):
    lv = l_ref[...]
    e = jnp.exp(lv - jnp.max(lv, axis=0, keepdims=True))
    sm = e / jnp.sum(e, axis=0, keepdims=True)
    s0 = sm[0:1]
    return (s0 - s0) if layer == 0 else ((s0 + sm[1:2]) - s0)


def _hgrn_gates(hq_ref, hf_ref, lb):
    z = hf_ref[...]
    sg = _sigmoid(z)
    f = lb + (1.0 - lb) * sg
    kin = (1.0 - lb) * _sigmoid(-z)
    hq = hq_ref[...]
    sq = _sigmoid(hq)
    return sg, f, kin, hq, sq


def _shift_down(x, d):
    return x if d == 0 else pltpu.roll(x, d, 0)


def _shift_up(x, d):
    return x if d == 0 else pltpu.roll(x, x.shape[0] - d, 0)


def _hgrn_specs(t, rev, nt):
    row = (lambda h, i: nt - 1 - i) if rev else (lambda h, i: i)
    col = lambda base: pl.BlockSpec((t, LANES), lambda h, i, base=base: (row(h, i), base + h))
    return col, row


def hgrn_fwd(name, proj, lb_logits, layer):
    s = proj.shape[0]
    t = _tile(s, ROW_TILE)
    nt = s // t
    nc = t // HGRN_CHUNK
    col, row = _hgrn_specs(t, False, nt)

    def body(hq_ref, hf_ref, hi_ref, l_ref, o_ref, st_ref, state, b_s, q_s, k_s, oi_s):
        @pl.when(pl.program_id(1) == 0)
        def _():
            state[...] = jnp.zeros_like(state)

        lb = _lower_bound(l_ref, layer)
        sg, f, kin, hq, sq = _hgrn_gates(hq_ref, hf_ref, lb)
        q = hq * sq
        v = hi_ref[...]
        b = _chunk_sum(jnp.log(f), False)
        tmod = lax.broadcasted_iota(jnp.int32, (t, LANES), 0) % HGRN_CHUNK
        o = jnp.sum(q * kin, axis=-1, keepdims=True) * v
        for d in range(1, HGRN_CHUNK):
            e = jnp.exp(jnp.where(tmod >= d, b - _shift_down(b, d), MASK_VALUE))
            a = jnp.sum((q * _shift_down(kin, d)) * e, axis=-1, keepdims=True)
            o = o + a * _shift_down(v, d)
        b_s[...] = b
        q_s[...] = q
        k_s[...] = kin

        def chunk(c, carry):
            r = pl.ds(pl.multiple_of(c * HGRN_CHUNK, HGRN_CHUNK), HGRN_CHUNK)
            bc = b_s[r, :]
            bl = bc[HGRN_CHUNK - 1:HGRN_CHUNK, :]
            qe = (q_s[r, :] * jnp.exp(bc)).astype(BF16)
            ke = (k_s[r, :] * jnp.exp(bl - bc)).astype(BF16)
            st = state[...]
            st_ref[c] = st
            oi_s[r, :] = lax.dot_general(qe, st.astype(BF16), _NT, preferred_element_type=F32)
            state[...] = st * jnp.exp(bl) + lax.dot_general(hi_ref[r, :].astype(BF16), ke, _TN, preferred_element_type=F32)
            return carry

        lax.fori_loop(0, nc, chunk, 0)
        o_ref[...] = o + oi_s[...]

    tile = pltpu.VMEM((t, LANES), F32)
    return _pc(body, name=name, grid=(HGRN_HEADS, nt),
               in_specs=[col(COL_HQ), col(COL_HF), col(COL_HI), pl.BlockSpec((2, LANES), lambda h, i: (0, h))],
               out_specs=[pl.BlockSpec((t, LANES), lambda h, i: (i, h)),
                          pl.BlockSpec((nc, None, LANES, LANES), lambda h, i: (i, h, 0, 0))],
               out_shape=[SDS((s, HGRN_WIDTH), F32), SDS((s // HGRN_CHUNK, HGRN_HEADS, LANES, LANES), F32)],
               scratch=[pltpu.VMEM((LANES, LANES), F32), tile, tile, tile, tile],
               sem=("parallel", "arbitrary"))(proj, proj, proj, lb_logits)


def hgrn_bwd(name, proj, lb_logits, layer, states, do):
    s = proj.shape[0]
    t = _tile(s, ROW_TILE)
    nt = s // t
    nc = t // HGRN_CHUNK
    col, row = _hgrn_specs(t, True, nt)

    def body(hq_ref, hf_ref, hi_ref, l_ref, st_ref, do_ref, dhq_ref, dhf_ref, dhi_ref, dlb_ref,
             dstate, b_s, q_s, k_s, dq_s, dk_s, dv_s, db_s):
        first = pl.program_id(1) == 0

        @pl.when(first)
        def _():
            dstate[...] = jnp.zeros_like(dstate)

        lb = _lower_bound(l_ref, layer)
        sg, f, kin, hq, sq = _hgrn_gates(hq_ref, hf_ref, lb)
        q = hq * sq
        v = hi_ref[...]
        b = _chunk_sum(jnp.log(f), False)
        dov = do_ref[...]
        b_s[...] = b
        q_s[...] = q
        k_s[...] = kin
        last_row = lax.broadcasted_iota(jnp.int32, (HGRN_CHUNK, LANES), 0) == HGRN_CHUNK - 1

        def chunk(cc, carry):
            c = nc - 1 - cc
            r = pl.ds(pl.multiple_of(c * HGRN_CHUNK, HGRN_CHUNK), HGRN_CHUNK)
            bc = b_s[r, :]
            bl = bc[HGRN_CHUNK - 1:HGRN_CHUNK, :]
            ebl = jnp.exp(bl)
            eb = jnp.exp(bc)
            ekb = jnp.exp(bl - bc)
            qe = q_s[r, :] * eb
            ke = k_s[r, :] * ekb
            doc = do_ref[r, :].astype(BF16)
            vc = hi_ref[r, :].astype(BF16)
            st = st_ref[c]
            dn = dstate[...]
            dnb = dn.astype(BF16)
            dqe = jnp.dot(doc, st.astype(BF16), preferred_element_type=F32)
            dke = jnp.dot(vc, dnb, preferred_element_type=F32)
            dv_s[r, :] = lax.dot_general(ke.astype(BF16), dnb, _NT, preferred_element_type=F32)
            dbl = jnp.sum(dn * st, axis=0, keepdims=True) * ebl + jnp.sum(dke * ke, axis=0, keepdims=True)
            dq_s[r, :] = dqe * eb
            dk_s[r, :] = dke * ekb
            db_s[r, :] = (dqe * qe - dke * ke) + jnp.where(last_row, dbl, 0.0)
            dstate[...] = dn * ebl + lax.dot_general(doc, qe.astype(BF16), _TN, preferred_element_type=F32)
            return carry

        lax.fori_loop(0, nc, chunk, 0)

        tmod = lax.broadcasted_iota(jnp.int32, (t, LANES), 0) % HGRN_CHUNK
        da = jnp.sum(dov * v, axis=-1, keepdims=True)
        a = jnp.sum(q * kin, axis=-1, keepdims=True)
        dq = dq_s[...] + da * kin
        dk = dk_s[...] + da * q
        dv = dv_s[...] + a * dov
        db = db_s[...]
        for d in range(1, HGRN_CHUNK):
            ks = _shift_down(kin, d)
            e = jnp.exp(jnp.where(tmod >= d, b - _shift_down(b, d), MASK_VALUE))
            qe = q * e
            w = qe * ks
            a = jnp.sum(w, axis=-1, keepdims=True)
            da = jnp.sum(dov * _shift_down(v, d), axis=-1, keepdims=True)
            dq = dq + da * (ks * e)
            daw = da * w
            db = db + daw - _shift_up(daw, d)
            dk = dk + _shift_up(da * qe, d)
            dv = dv + _shift_up(a * dov, d)
        dg = _chunk_sum(db, True)
        dhq_ref[...] = (dq * (sq * (1.0 + hq * (1.0 - sq)))).astype(BF16)
        dhi_ref[...] = dv.astype(BF16)
        dfk = dg / f - dk
        dhf_ref[...] = ((dfk * (1.0 - lb)) * (sg * (1.0 - sg))).astype(BF16)
        part = jnp.sum(dfk * (1.0 - sg), axis=0, keepdims=True)

        @pl.when(first)
        def _():
            dlb_ref[...] = part

        @pl.when(jnp.logical_not(first))
        def _():
            dlb_ref[...] += part

    tile = pltpu.VMEM((t, LANES), F32)
    out_col = pl.BlockSpec((t, LANES), lambda h, i: (nt - 1 - i, h))
    return _pc(body, name=name, grid=(HGRN_HEADS, nt),
               in_specs=[col(COL_HQ), col(COL_HF), col(COL_HI), pl.BlockSpec((2, LANES), lambda h, i: (0, h)),
                         pl.BlockSpec((nc, None, LANES, LANES), lambda h, i: (nt - 1 - i, h, 0, 0)), out_col],
               out_specs=[out_col, out_col, out_col, pl.BlockSpec((1, LANES), lambda h, i: (0, h))],
               out_shape=[SDS((s, HGRN_WIDTH), BF16)] * 3 + [SDS((1, HGRN_WIDTH), F32)],
               scratch=[pltpu.VMEM((LANES, LANES), F32)] + [tile] * 7,
               sem=("parallel", "arbitrary"))(proj, proj, proj, lb_logits, states, do)


def mix_out_fwd(name, attn, o, proj, g_attn, g_hgrn):
    s = attn.shape[0]
    t = _tile(s, ROW_TILE)
    half = HGRN_WIDTH // 2

    def body(a_ref, o_ref, hg0_ref, hg1_ref, ga_ref, gh_ref, c_ref):
        av = a_ref[...]
        c_ref[:, :ATTN_WIDTH] = ((av * _rstd(av)) * ga_ref[...]).astype(BF16)
        for j in range(HGRN_HEADS):
            cols = slice(j * LANES, (j + 1) * LANES)
            ov = o_ref[:, cols]
            hg_ref, hcols = (hg0_ref, cols) if j < 4 else (hg1_ref, slice((j - 4) * LANES, (j - 3) * LANES))
            hg = hg_ref[:, hcols]
            on = (ov * _rstd(ov)) * gh_ref[:, cols]
            c_ref[:, ATTN_WIDTH + j * LANES:ATTN_WIDTH + (j + 1) * LANES] = (on * (hg * _sigmoid(hg))).astype(BF16)

    return _pc(body, name=name, grid=(s // t,),
               in_specs=[_rows(t, ATTN_WIDTH), _rows(t, HGRN_WIDTH), _rows(t, half, COL_HG // 4), _rows(t, half, COL_HG // 4 + 1),
                         _fixed((1, ATTN_WIDTH)), _fixed((1, HGRN_WIDTH))],
               out_specs=_rows(t, D_MODEL), out_shape=SDS((s, D_MODEL), BF16), sem=("parallel",))(attn, o, proj, proj, g_attn, g_hgrn)


def mix_out_bwd(name, dcat, attn, o, proj, g_attn, g_hgrn):
    s = attn.shape[0]
    t = _tile(s, ROW_TILE)
    half = HGRN_WIDTH // 2

    def body(dc_ref, a_ref, o_ref, hg0_ref, hg1_ref, ga_ref, gh_ref, da_ref, do_ref, dhg_ref, dga_ref, dgh_ref, pa_s, ph_s):
        av = a_ref[...]
        r = _rstd(av)
        xh = av * r
        dyv = dc_ref[:, :ATTN_WIDTH]
        dyg = dyv * ga_ref[...]
        da_ref[...] = r * (dyg - xh * jnp.mean(dyg * xh, axis=-1, keepdims=True))
        pa_s[...] = jnp.sum(dyv * xh, axis=0, keepdims=True)
        for j in range(HGRN_HEADS):
            cols = slice(j * LANES, (j + 1) * LANES)
            ov = o_ref[:, cols]
            hg_ref, hcols = (hg0_ref, cols) if j < 4 else (hg1_ref, slice((j - 4) * LANES, (j - 3) * LANES))
            hg = hg_ref[:, hcols]
            sg = _sigmoid(hg)
            r = _rstd(ov)
            xh = ov * r
            gain = gh_ref[:, cols]
            dh = dc_ref[:, ATTN_WIDTH + j * LANES:ATTN_WIDTH + (j + 1) * LANES]
            dhg_ref[:, cols] = ((dh * (xh * gain)) * (sg * (1.0 + hg * (1.0 - sg)))).astype(BF16)
            dyv = dh * (hg * sg)
            dyg = dyv * gain
            do_ref[:, cols] = r * (dyg - xh * jnp.mean(dyg * xh, axis=-1, keepdims=True))
            ph_s[:, cols] = jnp.sum(dyv * xh, axis=0, keepdims=True)

        @pl.when(pl.program_id(0) == 0)
        def _():
            dga_ref[...] = pa_s[...]
            dgh_ref[...] = ph_s[...]

        @pl.when(pl.program_id(0) > 0)
        def _():
            dga_ref[...] += pa_s[...]
            dgh_ref[...] += ph_s[...]

    return _pc(body, name=name, grid=(s // t,),
               in_specs=[_rows(t, D_MODEL), _rows(t, ATTN_WIDTH), _rows(t, HGRN_WIDTH), _rows(t, half, COL_HG // 4),
                         _rows(t, half, COL_HG // 4 + 1), _fixed((1, ATTN_WIDTH)), _fixed((1, HGRN_WIDTH))],
               out_specs=[_rows(t, ATTN_WIDTH), _rows(t, HGRN_WIDTH), _rows(t, HGRN_WIDTH), _fixed((1, ATTN_WIDTH)),
                          _fixed((1, HGRN_WIDTH))],
               out_shape=[SDS((s, ATTN_WIDTH), F32), SDS((s, HGRN_WIDTH), F32), SDS((s, HGRN_WIDTH), BF16),
                          SDS((1, ATTN_WIDTH), F32), SDS((1, HGRN_WIDTH), F32)],
               scratch=[pltpu.VMEM((1, ATTN_WIDTH), F32), pltpu.VMEM((1, HGRN_WIDTH), F32)],
               sem=("arbitrary",))(dcat, attn, o, proj, proj, g_attn, g_hgrn)


BIG = (("w_in", 2048, 1408, 0), ("w_out", 512, 2048, 1), ("w_ffn_gate", 2048, 1408, 0), ("w_ffn_up", 2048, 1408, 0),
       ("w_ffn_down", 1408, 2048, 1), ("w_ple_gate", 512, 2048, 1), ("w_ple_proj", 256, 512, 0))
BIG_BY_NAME = {spec[0]: spec for spec in BIG}
HBM_SPEC = pl.BlockSpec(memory_space=pltpu.HBM)
SEM_SPEC = pl.BlockSpec(memory_space=pltpu.SEMAPHORE)
TOKEN_SHAPE = (8, LANES)


def _split_call(body, *, name, in_specs, out_specs, out_shape, aliases):
    return pl.pallas_call(body, name=name, in_specs=in_specs, out_specs=out_specs, out_shape=out_shape,
                          input_output_aliases=aliases,
                          compiler_params=pltpu.CompilerParams(has_side_effects=pltpu.SideEffectType.DATAFLOW_SIDE_EFFECTING))


def _in_hbm(arrays):
    return [pltpu.with_memory_space_constraint(a, pltpu.HBM) for a in arrays]


def cast_to_slot(name, place, w, layer):
    _, r, c = w.shape
    tr = _tile(r, 512)

    def body(place_ref, w_ref, o_ref):
        o_ref[...] = w_ref[...].astype(BF16)

    gs = pltpu.PrefetchScalarGridSpec(
        num_scalar_prefetch=1, grid=(r // tr,),
        in_specs=[pl.BlockSpec((None, tr, c), lambda i, pr: (layer, i, 0))],
        out_specs=pl.BlockSpec((None, tr, c), lambda i, pr: (pr[1], i, 0)))
    return _pc(body, name=name, grid_spec=gs, in_specs=None, out_specs=None, out_shape=SDS((N_CHIPS, r, c), BF16),
               sem=("parallel",))(place, w)


def _place():
    x, y, c = lax.axis_index("x"), lax.axis_index("y"), lax.axis_index("c")
    chips = [(1 - x, y), (x, 1 - y), (1 - x, 1 - y)]
    return x, y, c, chips


def _half(ref, axis, c, rows, cols):
    if axis == 0:
        return ref.at[pl.ds(pl.multiple_of(c * (rows // 2), 16), rows // 2), :]
    return ref.at[:, pl.ds(pl.multiple_of(c * (cols // 2), LANES), cols // 2)]


def _gather_copies(specs, bufs, send, recv):
    x, y, c, chips = _place()
    cps = []
    for t, (_, rows, cols, axis) in enumerate(specs):
        mine = _half(bufs[t].at[2 * x + y], axis, c, rows, cols)
        for k, (cx, cy) in enumerate(chips):
            cps.append(pltpu.make_async_remote_copy(src_ref=mine, dst_ref=mine, send_sem=send.at[3 * t + k],
                                                    recv_sem=recv.at[3 * t + k], device_id=(cx, cy, c), device_id_type=MESH))
    return cps


def gather_start(name, specs, bufs):
    nt = len(bufs)
    n = 3 * nt

    def body(*refs):
        send, recv, token = refs[nt], refs[nt + 1], refs[-1]
        for cp in _gather_copies(specs, refs[:nt], send, recv):
            cp.start()
        token[...] = jnp.zeros(TOKEN_SHAPE, F32)

    out = _split_call(
        body, name=name, in_specs=[HBM_SPEC] * nt,
        out_specs=(SEM_SPEC, SEM_SPEC) + (HBM_SPEC,) * nt + (pl.BlockSpec(memory_space=pltpu.VMEM),),
        out_shape=(pltpu.SemaphoreType.DMA((n,)), pltpu.SemaphoreType.DMA((n,)))
        + tuple(pltpu.HBM(b.shape, b.dtype) for b in bufs) + (SDS(TOKEN_SHAPE, F32),),
        aliases={t: 2 + t for t in range(nt)})(*_in_hbm(bufs))
    return out[0], out[1], list(out[2:2 + nt]), out[-1]


def gather_wait(name, specs, send, recv, bufs, after):
    nt = len(bufs)

    def body(*refs):
        for cp in _gather_copies(specs, refs[:nt], refs[nt], refs[nt + 1]):
            cp.wait_send()
            cp.wait_recv()

    out = _split_call(
        body, name=name, in_specs=[HBM_SPEC] * nt + [SEM_SPEC, SEM_SPEC, pl.BlockSpec(memory_space=pl.ANY)],
        out_specs=(HBM_SPEC,) * nt, out_shape=tuple(pltpu.HBM(b.shape, b.dtype) for b in bufs),
        aliases={t: t for t in range(nt)})(*bufs, send, recv, after)
    return list(out)


def gather_pass(name, specs, bufs):
    nt = len(bufs)

    def body(*refs):
        ins, outs = refs[:nt], refs[nt:2 * nt]
        send, recv = refs[2 * nt:]
        x, y, c, chips = _place()
        cps = []
        for t, (_, rows, cols, axis) in enumerate(specs):
            for k, (cx, cy) in enumerate(chips):
                cp = pltpu.make_async_remote_copy(
                    src_ref=_half(ins[t].at[2 * cx + cy], axis, c, rows, cols),
                    dst_ref=_half(outs[t].at[2 * cx + cy], axis, c, rows, cols),
                    send_sem=send.at[3 * t + k], recv_sem=recv.at[3 * t + k], device_id=(x, y, 1 - c), device_id_type=MESH)
                cp.start()
                cps.append(cp)
        for t, (_, rows, cols, axis) in enumerate(specs):
            for k, (cx, cy) in enumerate(chips):
                theirs = _half(outs[t].at[2 * cx + cy], axis, 1 - c, rows, cols)
                pltpu.make_async_remote_copy(src_ref=theirs, dst_ref=theirs, send_sem=send.at[3 * t + k],
                                             recv_sem=recv.at[3 * t + k], device_id=(x, y, 1 - c), device_id_type=MESH).wait_recv()
        for cp in cps:
            cp.wait_send()

    return _pc(body, name=name, in_specs=[HBM_SPEC] * nt, out_specs=[HBM_SPEC] * nt,
               out_shape=[SDS(b.shape, b.dtype) for b in bufs], scratch=[pltpu.SemaphoreType.DMA((3 * nt,))] * 2,
               input_output_aliases={t: t for t in range(nt)})(*bufs)


def reduce_to_sibling(name, grads):
    nt = len(grads)

    def body(*refs):
        srcs, dsts = refs[:nt], refs[nt:2 * nt]
        send, recv = refs[2 * nt:]
        x, y, c, _ = _place()
        cps = []
        for t in range(nt):
            cp = pltpu.make_async_remote_copy(src_ref=srcs[t].at[1 - c], dst_ref=dsts[t], send_sem=send.at[t],
                                              recv_sem=recv.at[t], device_id=(x, y, 1 - c), device_id_type=MESH)
            cp.start()
            cps.append(cp)
        for cp in cps:
            cp.wait()

    return _pc(body, name=name, in_specs=[HBM_SPEC] * nt, out_specs=[HBM_SPEC] * nt,
               out_shape=[SDS(g.shape[1:], F32) for g in grads],
               scratch=[pltpu.SemaphoreType.DMA((nt,))] * 2)(*grads)


def add_halves(name, place, grad, got):
    _, n, r, c = grad.shape
    tr = _tile(r, 256)

    def body(place_ref, g_ref, o_ref, part_ref, slot_ref):
        val = (g_ref[...] + o_ref[...]).astype(BF16)
        part_ref[...] = val

        @pl.when(pl.program_id(1) == place_ref[1])
        def _():
            slot_ref[...] = val

    gs = pltpu.PrefetchScalarGridSpec(
        num_scalar_prefetch=1, grid=(r // tr, n),
        in_specs=[pl.BlockSpec((None, None, tr, c), lambda i, j, pr: (pr[0], j, i, 0)),
                  pl.BlockSpec((None, tr, c), lambda i, j, pr: (j, i, 0))],
        out_specs=[pl.BlockSpec((None, tr, c), lambda i, j, pr: (j, i, 0)),
                   pl.BlockSpec((None, tr, c), lambda i, j, pr: (pr[1], i, 0))])
    return _pc(body, name=name, grid_spec=gs, in_specs=None, out_specs=None, out_shape=[SDS((n, r, c), BF16)] * 2,
               sem=("parallel", "arbitrary"))(place, grad, got)


def _chips_copies(parts, slots, send, recv):
    x, y, c, chips = _place()
    cps = []
    for t in range(len(parts)):
        for k, (cx, cy) in enumerate(chips):
            cps.append(pltpu.make_async_remote_copy(src_ref=parts[t].at[2 * cx + cy], dst_ref=slots[t].at[2 * x + y],
                                                    send_sem=send.at[3 * t + k], recv_sem=recv.at[3 * t + k],
                                                    device_id=(cx, cy, c), device_id_type=MESH))
    return cps


def chips_start(name, parts, slots):
    nt = len(parts)
    n = 3 * nt

    def body(*refs):
        send, recv, token = refs[2 * nt], refs[2 * nt + 1], refs[-1]
        for cp in _chips_copies(refs[:nt], refs[nt:2 * nt], send, recv):
            cp.start()
        token[...] = jnp.zeros(TOKEN_SHAPE, F32)

    both = list(parts) + list(slots)
    out = _split_call(
        body, name=name, in_specs=[HBM_SPEC] * (2 * nt),
        out_specs=(SEM_SPEC, SEM_SPEC) + (HBM_SPEC,) * (2 * nt) + (pl.BlockSpec(memory_space=pltpu.VMEM),),
        out_shape=(pltpu.SemaphoreType.DMA((n,)), pltpu.SemaphoreType.DMA((n,)))
        + tuple(pltpu.HBM(b.shape, b.dtype) for b in both) + (SDS(TOKEN_SHAPE, F32),),
        aliases={t: 2 + t for t in range(2 * nt)})(*_in_hbm(both))
    return out[0], out[1], list(out[2:2 + nt]), list(out[2 + nt:2 + 2 * nt]), out[-1]


def chips_wait(name, send, recv, parts, slots, after):
    nt = len(parts)

    def body(*refs):
        for cp in _chips_copies(refs[:nt], refs[nt:2 * nt], refs[2 * nt], refs[2 * nt + 1]):
            cp.wait_send()
            cp.wait_recv()

    both = list(parts) + list(slots)
    out = _split_call(
        body, name=name, in_specs=[HBM_SPEC] * (2 * nt) + [SEM_SPEC, SEM_SPEC, pl.BlockSpec(memory_space=pl.ANY)],
        out_specs=(HBM_SPEC,) * (2 * nt), out_shape=tuple(pltpu.HBM(b.shape, b.dtype) for b in both),
        aliases={t: t for t in range(2 * nt)})(*both, send, recv, after)
    return list(out[nt:])


def sum_chips(name, place, slots):
    n, r, c = slots.shape
    tr = _tile(r, 256)

    def body(place_ref, s_ref, o_ref):
        acc = s_ref[0].astype(F32)
        for k in range(1, n):
            acc = acc + s_ref[k].astype(F32)
        o_ref[...] = acc

    gs = pltpu.PrefetchScalarGridSpec(
        num_scalar_prefetch=1, grid=(r // tr,),
        in_specs=[pl.BlockSpec((n, tr, c), lambda i, pr: (0, i, 0))],
        out_specs=pl.BlockSpec((None, tr, c), lambda i, pr: (pr[0], i, 0)))
    return _pc(body, name=name, grid_spec=gs, in_specs=None, out_specs=None, out_shape=SDS((2, r, c), F32),
               sem=("parallel",))(place, slots)


def share_with_sibling(name, bufs):
    nt = len(bufs)

    def body(*refs):
        ins, outs = refs[:nt], refs[nt:2 * nt]
        send, recv = refs[2 * nt:]
        x, y, c, _ = _place()
        cps = []
        for t in range(nt):
            cp = pltpu.make_async_remote_copy(src_ref=ins[t].at[c], dst_ref=outs[t].at[c], send_sem=send.at[t], recv_sem=recv.at[t],
                                              device_id=(x, y, 1 - c), device_id_type=MESH)
            cp.start()
            cps.append(cp)
        for t in range(nt):
            theirs = outs[t].at[1 - c]
            pltpu.make_async_remote_copy(src_ref=theirs, dst_ref=theirs, send_sem=send.at[t], recv_sem=recv.at[t],
                                         device_id=(x, y, 1 - c), device_id_type=MESH).wait_recv()
        for cp in cps:
            cp.wait_send()

    return _pc(body, name=name, in_specs=[HBM_SPEC] * nt, out_specs=[HBM_SPEC] * nt,
               out_shape=[SDS(b.shape, F32) for b in bufs], scratch=[pltpu.SemaphoreType.DMA((nt,))] * 2,
               input_output_aliases={t: t for t in range(nt)})(*bufs)


def _adamw(w, g, m, v):
    m = ADAM_B1 * m + (1.0 - ADAM_B1) * g
    v = ADAM_B2 * v + (1.0 - ADAM_B2) * (g * g)
    m_hat = m / (1.0 - ADAM_B1 ** ADAM_STEP)
    v_hat = v / (1.0 - ADAM_B2 ** ADAM_STEP)
    delta = -ADAM_LR * (m_hat / (jnp.sqrt(v_hat) + ADAM_EPS) + ADAM_WD * w)
    return delta, m, v


def adamw_big(name, w, m, v, g0, g1, axis):
    _, r, c = w.shape
    _, rh, ch = g0.shape
    tr = _tile(rh, 256)
    nb = rh // tr
    if axis == 0:
        wspec = pl.BlockSpec((None, tr, ch), lambda l, h, i: (l, h * nb + i, 0))
    else:
        wspec = pl.BlockSpec((None, tr, ch), lambda l, h, i: (l, i, h))
    g0spec = pl.BlockSpec((None, tr, ch), lambda l, h, i: (h * (1 - l), i * (1 - l), 0))
    g1spec = pl.BlockSpec((None, tr, ch), lambda l, h, i: (h * l, i * l, 0))

    def body(w_ref, m_ref, v_ref, g0_ref, g1_ref, go_ref, d_ref, mo_ref, vo_ref):
        def run(g_ref):
            g = g_ref[...]
            delta, mn, vn = _adamw(w_ref[...], g, m_ref[...], v_ref[...])
            go_ref[...] = g
            d_ref[...] = delta
            mo_ref[...] = mn
            vo_ref[...] = vn

        @pl.when(pl.program_id(0) == 0)
        def _():
            run(g0_ref)

        @pl.when(pl.program_id(0) == 1)
        def _():
            run(g1_ref)

    return _pc(body, name=name, grid=(2, 2, nb), in_specs=[wspec, wspec, wspec, g0spec, g1spec], out_specs=[wspec] * 4,
               out_shape=[SDS(w.shape, F32)] * 4, sem=("parallel", "parallel", "parallel"))(w, m, v, g0, g1)


SMALL = (("pre_mix_gain", 2048), ("post_mix_gain", 2048), ("pre_ffn_gain", 2048), ("post_ffn_gain", 2048), ("ple_gain", 2048),
         ("attn_out_gain", 1024), ("hgrn_out_gain", 1024), ("hgrn_lb_logits", 1024), ("attn_sinks", 128))
SMALL_ROWS = sum(2 * w // LANES for _, w in SMALL)
SMALL_PAD = -(-SMALL_ROWS // 8) * 8
LB_ROW = sum(2 * w // LANES for _, w in SMALL[:7])


def _pack_small(parts):
    rows = []
    for nm, w in SMALL:
        a = parts[nm].astype(F32)
        if a.shape[1] != w:
            a = jnp.pad(a, ((0, 0), (0, w - a.shape[1])))
        rows.append(a.reshape(2 * w // LANES, LANES))
    rows.append(jnp.zeros((SMALL_PAD - SMALL_ROWS, LANES), F32))
    return jnp.concatenate(rows, axis=0)


def _unpack_small(packed, widths):
    out, r = {}, 0
    for nm, w in SMALL:
        n = 2 * w // LANES
        out[nm] = packed[r:r + n].reshape(2, w)[:, :widths[nm]]
        r += n
    return out


def allreduce_small(name, packed):
    rows = packed.shape[0]

    def body(x_ref, o_ref, buf, send, recv, own_sem):
        x, y, c, _ = _place()
        me = 4 * x + 2 * y + c
        own = pltpu.make_async_copy(x_ref, buf.at[me], own_sem)
        own.start()
        cps = []
        for k in range(1, 8):
            px, py, pc = x ^ (k >> 2), y ^ ((k >> 1) & 1), c ^ (k & 1)
            cp = pltpu.make_async_remote_copy(src_ref=x_ref, dst_ref=buf.at[me], send_sem=send.at[k - 1], recv_sem=recv.at[k - 1],
                                              device_id=(px, py, pc), device_id_type=MESH)
            cp.start()
            cps.append(cp)
        for k in range(1, 8):
            px, py, pc = x ^ (k >> 2), y ^ ((k >> 1) & 1), c ^ (k & 1)
            slot = buf.at[4 * px + 2 * py + pc]
            pltpu.make_async_remote_copy(src_ref=slot, dst_ref=slot, send_sem=send.at[k - 1], recv_sem=recv.at[k - 1],
                                         device_id=(px, py, pc), device_id_type=MESH).wait_recv()
        for cp in cps:
            cp.wait_send()
        own.wait()
        acc = buf[0]
        for k in range(1, 8):
            acc = acc + buf[k]
        o_ref[...] = acc

    vm = pl.BlockSpec(memory_space=pltpu.VMEM)
    return _pc(body, name=name, in_specs=[vm], out_specs=vm, out_shape=SDS((rows, LANES), F32),
               scratch=[pltpu.VMEM((8, rows, LANES), F32), pltpu.SemaphoreType.DMA((7,)), pltpu.SemaphoreType.DMA((7,)),
                        pltpu.SemaphoreType.DMA])(packed)


def adamw_small(name, w, m, v, g):
    rows = w.shape[0]
    n = HGRN_WIDTH // LANES

    def body(w_ref, m_ref, v_ref, g_ref, go_ref, d_ref, mo_ref, vo_ref):
        go_ref[...] = g_ref[...]
        l0 = w_ref[LB_ROW:LB_ROW + n, :]
        l1 = w_ref[LB_ROW + n:LB_ROW + 2 * n, :]
        mx = jnp.maximum(l0, l1)
        e0, e1 = jnp.exp(l0 - mx), jnp.exp(l1 - mx)
        s0, s1 = e0 / (e0 + e1), e1 / (e0 + e1)
        dlb1 = g_ref[LB_ROW + n:LB_ROW + 2 * n, :]
        inner = s1 * dlb1
        go_ref[LB_ROW:LB_ROW + n, :] = s0 * (0.0 - inner)
        go_ref[LB_ROW + n:LB_ROW + 2 * n, :] = s1 * (dlb1 - inner)
        delta, mn, vn = _adamw(w_ref[...], go_ref[...], m_ref[...], v_ref[...])
        d_ref[...] = delta
        mo_ref[...] = mn
        vo_ref[...] = vn

    vm = pl.BlockSpec(memory_space=pltpu.VMEM)
    return _pc(body, name=name, in_specs=[vm] * 4, out_specs=[vm] * 4, out_shape=[SDS((rows, LANES), F32)] * 4)(w, m, v, g)


def _layer_fwd(l, x, h1, p_l, w_in_g, rest_of_weights, gains, cos, sin, sinks, lb_logits, g_next, target):
    n = f"l{l}_"
    proj = mm_col(n + "in_proj", h1, w_in_g)
    qp, kp, vp = rope_qkv(n + "rope_qkv", proj, cos, sin)
    attn = attn_fwd(n + "attn_fwd", qp, kp, vp, sinks)
    o, states = hgrn_fwd(n + "hgrn_fwd", proj, lb_logits, l)
    cat = mix_out_fwd(n + "mix_out_fwd", attn, o, proj, gains["attn_out_gain"], gains["hgrn_out_gain"])
    wts = dict(rest_of_weights(cat), w_in=w_in_g)
    m = mm_row(n + "out_proj", cat, wts["w_out"])
    x1, h2 = post_pre_norm(n + "post_mix", m, gains["post_mix_gain"], x, gains["pre_ffn_gain"])
    g = mm_col(n + "ffn_gate", h2, wts["w_ffn_gate"])
    u = mm_col(n + "ffn_up", h2, wts["w_ffn_up"])
    a = swiglu_fwd(n + "swiglu_fwd", g, u)
    f = mm_row(n + "ffn_down", a, wts["w_ffn_down"])
    x2, h3 = post_pre_norm(n + "post_ffn", f, gains["post_ffn_gain"], x1, gains["ple_gain"])
    z = mm_row(n + "ple_gate", h3, wts["w_ple_gate"])
    pp = mm_col(n + "ple_proj", p_l, wts["w_ple_proj"])
    if target is None:
        out = ple_fwd_mid(n + "ple_fwd", z, pp, x2, g_next)
    else:
        out = ple_fwd_loss(n + "ple_loss", z, pp, x2, target)
    saved = dict(x=x, h1=h1, proj=proj, qp=qp, kp=kp, vp=vp, attn=attn, o=o, states=states, cat=cat, m=m, x1=x1, h2=h2,
                 g=g, u=u, a=a, f=f, x2=x2, h3=h3, z=z, pp=pp, p=p_l)
    return out, saved, wts


EARLY = ("w_ple_gate", "w_ple_proj", "w_ffn_down", "w_ffn_gate", "w_ffn_up")
LATE = ("w_out", "w_in")


def _layer_bwd_ffn(l, dx3, sv, wts, gains):
    n = f"l{l}_"
    dpp, dz = ple_bwd(n + "ple_bwd", dx3, sv["z"], sv["pp"])
    dh3 = mm_row_t(n + "ple_gate_dx", dz, wts["w_ple_gate"])
    dx2, d_ple_gain = norm_bwd(n + "ple_norm_bwd", sv["x2"], gains["ple_gain"], dh3, dx3)
    df, d_post_ffn = norm_bwd(n + "post_ffn_bwd", sv["f"], gains["post_ffn_gain"], dx2, None, BF16)
    da = mm_row_t(n + "ffn_down_dx", df, wts["w_ffn_down"])
    dg, du = swiglu_bwd(n + "swiglu_bwd", sv["g"], sv["u"], da)
    big = dict(
        w_ple_gate=mm_wg_row(n + "ple_gate_dw", sv["h3"], dz),
        w_ple_proj=mm_wg_col(n + "ple_proj_dw", sv["p"], dpp),
        w_ffn_down=mm_wg_row(n + "ffn_down_dw", sv["a"], df),
        w_ffn_gate=mm_wg_col(n + "ffn_gate_dw", sv["h2"], dg),
        w_ffn_up=mm_wg_col(n + "ffn_up_dw", sv["h2"], du),
    )
    return dict(dx2=dx2, dg=dg, du=du), big, dict(ple_gain=d_ple_gain, post_ffn_gain=d_post_ffn)


def _layer_bwd_mix(l, st, sv, wts, gains, cos, sin, sinks, lb_logits):
    n = f"l{l}_"
    dh2 = mm_col_t(n + "ffn_gate_dx", st["dg"], wts["w_ffn_gate"])
    dh2 = mm_col_t(n + "ffn_up_dx", st["du"], wts["w_ffn_up"], add=dh2)
    dx1, d_pre_ffn = norm_bwd(n + "pre_ffn_bwd", sv["x1"], gains["pre_ffn_gain"], dh2, st["dx2"])
    dm, d_post_mix = norm_bwd(n + "post_mix_bwd", sv["m"], gains["post_mix_gain"], dx1, None, BF16)
    dcat = mm_row_t(n + "out_proj_dx", dm, wts["w_out"])
    dattn, do, dhg, d_attn_gain, d_hgrn_gain = mix_out_bwd(n + "mix_out_bwd", dcat, sv["attn"], sv["o"], sv["proj"],
                                                            gains["attn_out_gain"], gains["hgrn_out_gain"])
    dqp, dkc, dkp, dvc, dvp, dsinks = attn_bwd(n + "attn_bwd", sv["qp"], sv["kp"], sv["vp"], sinks, dattn)
    dqkv = rope_bwd(n + "rope_bwd", dqp, dkc, dkp, dvc, dvp, cos, sin)
    dhq, dhf, dhi, dlb = hgrn_bwd(n + "hgrn_bwd", sv["proj"], lb_logits, l, sv["states"], do)
    dproj = jnp.concatenate([dqkv, dhq, dhf, dhi, dhg], axis=1)
    dh1 = mm_col_t(n + "in_proj_dx", dproj, wts["w_in"])
    dx, d_pre_mix = norm_bwd(n + "pre_mix_bwd", sv["x"], gains["pre_mix_gain"], dh1, dx1)
    big = dict(w_out=mm_wg_row(n + "out_proj_dw", sv["cat"], dm), w_in=mm_wg_col(n + "in_proj_dw", sv["h1"], dproj))
    small = dict(pre_mix_gain=d_pre_mix, post_mix_gain=d_post_mix, pre_ffn_gain=d_pre_ffn, attn_out_gain=d_attn_gain,
                 hgrn_out_gain=d_hgrn_gain, hgrn_lb_logits=dlb, attn_sinks=dsinks)
    return dx, big, small


def _layer_bwd(l, dx3, sv, wts, gains, cos, sin, sinks, lb_logits):
    st, early, small_a = _layer_bwd_ffn(l, dx3, sv, wts, gains)
    dx, late, small_b = _layer_bwd_mix(l, st, sv, wts, gains, cos, sin, sinks, lb_logits)
    return dx, {**early, **late}, {**small_a, **small_b}


def _reduce_start(tag, names, big, place):
    got = reduce_to_sibling(tag + "_reduce_to_sibling", [big[nm] for nm in names])
    pairs = [add_halves(f"{tag}_add_{nm}", place, big[nm], gt) for nm, gt in zip(names, got)]
    return chips_start(tag + "_chips_start", [pr[0] for pr in pairs], [pr[1] for pr in pairs])


def _reduce_finish(tag, names, started, place, after):
    send, recv, parts, slots, _ = started
    slots = chips_wait(tag + "_chips_wait", send, recv, parts, slots, after)
    bufs = [sum_chips(f"{tag}_sum_{nm}", place, sl) for nm, sl in zip(names, slots)]
    return dict(zip(names, share_with_sibling(tag + "_share_with_sibling", bufs)))


def _with_token(gains, name, token):
    out = dict(gains)
    out[name] = gains[name] + token[0, 0]
    return out


def kernel(x, p, positions, w_in, attn_sinks, hgrn_lb_logits, attn_out_gain, hgrn_out_gain, w_out, pre_mix_gain, post_mix_gain, pre_ffn_gain, post_ffn_gain, w_ffn_gate, w_ffn_up, w_ffn_down, ple_gain, w_ple_gate, w_ple_proj, loss_target, m_w_in, m_attn_sinks, m_hgrn_lb_logits, m_attn_out_gain, m_hgrn_out_gain, m_w_out, m_pre_mix_gain, m_post_mix_gain, m_pre_ffn_gain, m_post_ffn_gain, m_w_ffn_gate, m_w_ffn_up, m_w_ffn_down, m_ple_gain, m_w_ple_gate, m_w_ple_proj, v_w_in, v_attn_sinks, v_hgrn_lb_logits, v_attn_out_gain, v_hgrn_out_gain, v_w_out, v_pre_mix_gain, v_post_mix_gain, v_pre_ffn_gain, v_post_ffn_gain, v_w_ffn_gate, v_w_ffn_up, v_w_ffn_down, v_ple_gain, v_w_ple_gate, v_w_ple_proj):
    given = dict(locals())
    depth = 2
    place = jnp.stack([lax.axis_index("c"), 2 * lax.axis_index("x") + lax.axis_index("y")]).astype(jnp.int32)
    xs = x[0]
    tgt = loss_target[0]
    pos_col = positions.reshape(-1, 1)
    half = 32
    inv_freq = ROPE_THETA ** (-jnp.arange(half, dtype=F32) / half)
    inv_freq = jnp.tile(inv_freq, 4).reshape(1, LANES)
    gains = [{nm: given[nm][l:l + 1] for nm, _ in SMALL[:7]} for l in range(depth)]
    names = [nm for nm, *_ in BIG]
    first, others = names[:1], names[1:]

    def specs(nms):
        return [BIG_BY_NAME[nm] for nm in nms]

    def start_gather(tag, l, nms):
        return gather_start(tag + "_gather_start", specs(nms), [cast_to_slot(f"{tag}_cast_{nm}", place, given[nm], l) for nm in nms])

    def finish_gather(tag, nms, started, after):
        bufs = gather_wait(tag + "_gather_wait", specs(nms), started[0], started[1], started[2], after)
        return dict(zip(nms, gather_pass(tag + "_gather_pass", specs(nms), bufs)))

    g0a = start_gather("l0a", 0, first)
    g0b = start_gather("l0b", 0, others)
    g1 = start_gather("l1", 1, names)

    cos, sin = rope_tables("rope_tables", pos_col, inv_freq)
    h1 = pre_norm("l0_pre_mix", xs, gains[0]["pre_mix_gain"])
    w_in0 = finish_gather("l0a", first, g0a, g1[3])["w_in"]
    (x_mid, h1_next), sv0, wts0 = _layer_fwd(0, xs, h1, p[0, 0], w_in0, lambda after: finish_gather("l0b", others, g0b, after),
                                             gains[0], cos, sin, attn_sinks[0], hgrn_lb_logits, gains[1]["pre_mix_gain"], None)
    wts1 = finish_gather("l1", names, g1, x_mid)
    (dy, loss_part), sv1, _ = _layer_fwd(1, x_mid, h1_next, p[1, 0], wts1["w_in"], lambda after: wts1, gains[1], cos, sin,
                                         attn_sinks[1], hgrn_lb_logits, None, tgt)

    dx_mid, big1, small1 = _layer_bwd(1, dy, sv1, wts1, gains[1], cos, sin, attn_sinks[1], hgrn_lb_logits)
    r1 = _reduce_start("l1", names, big1, place)
    st0, early0, small0 = _layer_bwd_ffn(0, dx_mid, sv0, wts0, _with_token(gains[0], "ple_gain", r1[4]))
    red1 = _reduce_finish("l1", names, r1, place, st0["du"])
    r0e = _reduce_start("l0e", EARLY, early0, place)
    dx0, late0, small0b = _layer_bwd_mix(0, st0, sv0, wts0, _with_token(gains[0], "pre_ffn_gain", r0e[4]), cos, sin,
                                         attn_sinks[0], hgrn_lb_logits)
    small0 = {**small0, **small0b}
    r0l = _reduce_start("l0l", LATE, late0, place)
    red0 = _reduce_finish("l0e", EARLY, r0e, place, r0l[4])

    loss = lax.psum(loss_part[0, 0], ("x", "y", "c"))
    grad_x = dx0[None]

    out_big = {}
    for nm in EARLY:
        out_big[nm] = adamw_big("adamw_" + nm, given[nm], given["m_" + nm], given["v_" + nm], red0[nm], red1[nm], BIG_BY_NAME[nm][3])
    red0.update(_reduce_finish("l0l", LATE, r0l, place, out_big[EARLY[-1]][3]))
    for nm in LATE:
        out_big[nm] = adamw_big("adamw_" + nm, given[nm], given["m_" + nm], given["v_" + nm], red0[nm], red1[nm], BIG_BY_NAME[nm][3])

    widths = {nm: given[nm].shape[1] for nm, _ in SMALL}
    small_g = {nm: jnp.concatenate([small0[nm][:, :widths[nm]] if nm != "attn_sinks" else small0[nm][:, :LANES],
                                    small1[nm][:, :widths[nm]] if nm != "attn_sinks" else small1[nm][:, :LANES]], axis=0)
               for nm, _ in SMALL}
    g_sum = allreduce_small("allreduce_small", _pack_small(small_g))
    sm = adamw_small("adamw_small", _pack_small({nm: given[nm] for nm, _ in SMALL}),
                     _pack_small({nm: given["m_" + nm] for nm, _ in SMALL}),
                     _pack_small({nm: given["v_" + nm] for nm, _ in SMALL}), g_sum)
    out_small = [_unpack_small(a, widths) for a in sm]

    order = ["w_in", "attn_sinks", "hgrn_lb_logits", "attn_out_gain", "hgrn_out_gain", "w_out", "pre_mix_gain", "post_mix_gain",
             "pre_ffn_gain", "post_ffn_gain", "w_ffn_gate", "w_ffn_up", "w_ffn_down", "ple_gain", "w_ple_gate", "w_ple_proj"]
    res = [loss, grad_x]
    for k in range(4):
        for nm in order:
            res.append(out_big[nm][k] if nm in out_big else out_small[k][nm])
    return tuple(res)
```

```python
import functools

import jax
import jax.numpy as jnp
from jax import lax
from jax.experimental import pallas as pl
from jax.experimental.pallas import tpu as pltpu

F32, BF16 = jnp.float32, jnp.bfloat16
SDS = jax.ShapeDtypeStruct
MESH = pl.DeviceIdType.MESH

D_MODEL = 2048
ATTN_WIDTH = 1024
HGRN_WIDTH = 1024
KV_WIDTH = 256
N_Q_HEADS = 16
N_KV_HEADS = 4
Q_PER_KV = 4
WINDOW = 128
MASK_VALUE = -1e30
ROPE_THETA = 10000.0
HGRN_HEADS = 8
HGRN_CHUNK = 16
D_FF = 5632
D_PLE = 256
RMS_EPS = 1e-6
LANES = 128
N_CHIPS = 4
COL_Q, COL_K, COL_V, COL_HQ, COL_HF, COL_HI, COL_HG = 0, 8, 10, 12, 20, 28, 36

ADAM_LR, ADAM_B1, ADAM_B2, ADAM_EPS, ADAM_WD, ADAM_STEP = 0.001, 0.9, 0.999, 1e-08, 0.01, 10

VMEM_LIMIT = 56 * 1024 * 1024
ROW_TILE = 256

_NN = (((1,), (0,)), ((), ()))
_NT = (((1,), (1,)), ((), ()))
_TN = (((0,), (0,)), ((), ()))


def _pc(body, *, name, out_shape, in_specs, out_specs, grid=(), scratch=(), sem=None, grid_spec=None, **kw):
    params = dict(vmem_limit_bytes=VMEM_LIMIT)
    if sem is not None:
        params["dimension_semantics"] = sem
    if grid_spec is not None:
        return pl.pallas_call(body, name=name, out_shape=out_shape, grid_spec=grid_spec,
                              compiler_params=pltpu.CompilerParams(**params), **kw)
    return pl.pallas_call(body, name=name, out_shape=out_shape, grid=grid, in_specs=in_specs, out_specs=out_specs,
                          scratch_shapes=list(scratch), compiler_params=pltpu.CompilerParams(**params), **kw)


def _sigmoid(x):
    return 1.0 / (1.0 + jnp.exp(-x))


def _rstd(x):
    return lax.rsqrt(jnp.mean(x * x, axis=-1, keepdims=True) + RMS_EPS)


def _rows(t, w, col=0):
    return pl.BlockSpec((t, w), lambda i, col=col: (i, col))


def _fixed(shape):
    return pl.BlockSpec(shape, lambda *_: (0,) * len(shape))


def _mm(name, a, b, *, dims, grid, a_spec, b_spec, o_spec, out_shape, acc_shape=None, kaxis=None, add=None,
        add_spec=None, sem=None):
    nk = grid[kaxis] if kaxis is not None else 1

    def body(*refs):
        if add is None:
            a_ref, b_ref, o_ref, *scr = refs
            add_ref = None
        else:
            a_ref, b_ref, add_ref, o_ref, *scr = refs
        r = lax.dot_general(a_ref[...].astype(BF16), b_ref[...].astype(BF16), dims, preferred_element_type=F32)
        if nk == 1:
            if add_ref is not None:
                r = r + add_ref[...]
            o_ref[...] = r.astype(o_ref.dtype)
            return
        acc = scr[0]
        k = pl.program_id(kaxis)

        @pl.when(k == 0)
        def _():
            acc[...] = r

        @pl.when(k > 0)
        def _():
            acc[...] += r

        @pl.when(k == nk - 1)
        def _():
            t = acc[...]
            if add_ref is not None:
                t = t + add_ref[...]
            o_ref[...] = t.astype(o_ref.dtype)

    ins = [a, b] + ([] if add is None else [add])
    specs = [a_spec, b_spec] + ([] if add is None else [add_spec])
    scratch = [] if nk == 1 else [pltpu.VMEM(acc_shape, F32)]
    return _pc(body, name=name, grid=grid, in_specs=specs, out_specs=o_spec, out_shape=out_shape, scratch=scratch,
               sem=sem)(*ins)


def _tile(n, t):
    if n <= t:
        return n
    while n % t:
        t //= 2
    assert t % 8 == 0
    return t


def mm_col(name, a, wg, out_dtype=F32):
    s, k = a.shape
    _, _, n = wg.shape
    tm = _tile(s, 512)
    return _mm(name, a, wg, dims=_NN, grid=(N_CHIPS, s // tm),
               a_spec=pl.BlockSpec((tm, k), lambda j, i: (i, 0)),
               b_spec=pl.BlockSpec((None, k, n), lambda j, i: (j, 0, 0)),
               o_spec=pl.BlockSpec((tm, n), lambda j, i: (i, j)),
               out_shape=SDS((s, N_CHIPS * n), out_dtype), sem=("parallel", "parallel"))


def mm_row(name, a, wg, out_dtype=F32):
    s, _ = a.shape
    _, r, n = wg.shape
    tm = _tile(s, 512)
    return _mm(name, a, wg, dims=_NN, grid=(s // tm, N_CHIPS), kaxis=1,
               a_spec=pl.BlockSpec((tm, r), lambda i, k: (i, k)),
               b_spec=pl.BlockSpec((None, r, n), lambda i, k: (k, 0, 0)),
               o_spec=pl.BlockSpec((tm, n), lambda i, k: (i, 0)),
               out_shape=SDS((s, n), out_dtype), acc_shape=(tm, n), sem=("parallel", "arbitrary"))


def mm_col_t(name, dy, wg, add=None, out_dtype=F32):
    s, _ = dy.shape
    _, k, n = wg.shape
    tm = _tile(s, 512)
    return _mm(name, dy, wg, dims=_NT, grid=(s // tm, N_CHIPS), kaxis=1,
               a_spec=pl.BlockSpec((tm, n), lambda i, kk: (i, kk)),
               b_spec=pl.BlockSpec((None, k, n), lambda i, kk: (kk, 0, 0)),
               o_spec=pl.BlockSpec((tm, k), lambda i, kk: (i, 0)),
               add=add, add_spec=pl.BlockSpec((tm, k), lambda i, kk: (i, 0)),
               out_shape=SDS((s, k), out_dtype), acc_shape=(tm, k), sem=("parallel", "arbitrary"))


def mm_row_t(name, dy, wg, out_dtype=F32):
    s, n = dy.shape
    _, r, _ = wg.shape
    tm = _tile(s, 512)
    return _mm(name, dy, wg, dims=_NT, grid=(N_CHIPS, s // tm),
               a_spec=pl.BlockSpec((tm, n), lambda j, i: (i, 0)),
               b_spec=pl.BlockSpec((None, r, n), lambda j, i: (j, 0, 0)),
               o_spec=pl.BlockSpec((tm, r), lambda j, i: (i, j)),
               out_shape=SDS((s, N_CHIPS * r), out_dtype), sem=("parallel", "parallel"))


def mm_wg_col(name, a, dy):
    s, k = a.shape
    n = dy.shape[1] // N_CHIPS
    tm = _tile(k // 2, 1024)
    ts = _tile(s, 1024)
    hb = (k // 2) // tm
    return _mm(name, a, dy, dims=_TN, grid=(N_CHIPS, k // tm, s // ts), kaxis=2,
               a_spec=pl.BlockSpec((ts, tm), lambda j, i, t: (t, i)),
               b_spec=pl.BlockSpec((ts, n), lambda j, i, t: (t, j)),
               o_spec=pl.BlockSpec((None, None, tm, n), lambda j, i, t: (i // hb, j, i % hb, 0)),
               out_shape=SDS((2, N_CHIPS, k // 2, n), BF16), acc_shape=(tm, n),
               sem=("parallel", "parallel", "arbitrary"))


def mm_wg_row(name, a, dy):
    s, n = dy.shape
    r = a.shape[1] // N_CHIPS
    tn = _tile(n // 2, 1024)
    ts = _tile(s, 1024)
    nb = (n // 2) // tn
    return _mm(name, a, dy, dims=_TN, grid=(N_CHIPS, n // tn, s // ts), kaxis=2,
               a_spec=pl.BlockSpec((ts, r), lambda j, i, t: (t, j)),
               b_spec=pl.BlockSpec((ts, tn), lambda j, i, t: (t, i)),
               o_spec=pl.BlockSpec((None, None, r, tn), lambda j, i, t: (i // nb, j, 0, i % nb)),
               out_shape=SDS((2, N_CHIPS, r, n // 2), BF16), acc_shape=(r, tn),
               sem=("parallel", "parallel", "arbitrary"))


def pre_norm(name, x, gain):
    s, d = x.shape
    t = _tile(s, ROW_TILE)

    def body(x_ref, g_ref, o_ref):
        xv = x_ref[...]
        o_ref[...] = ((xv * _rstd(xv)) * g_ref[...]).astype(BF16)

    return _pc(body, name=name, grid=(s // t,), in_specs=[_rows(t, d), _fixed((1, d))], out_specs=_rows(t, d),
               out_shape=SDS((s, d), BF16), sem=("parallel",))(x, gain)


def post_pre_norm(name, m, g_post, res, g_pre):
    s, d = m.shape
    t = _tile(s, ROW_TILE)

    def body(m_ref, gp_ref, r_ref, gn_ref, x_ref, h_ref):
        mv = m_ref[...]
        xn = r_ref[...] + (mv * _rstd(mv)) * gp_ref[...]
        x_ref[...] = xn
        h_ref[...] = ((xn * _rstd(xn)) * gn_ref[...]).astype(BF16)

    return _pc(body, name=name, grid=(s // t,),
               in_specs=[_rows(t, d), _fixed((1, d)), _rows(t, d), _fixed((1, d))],
               out_specs=[_rows(t, d), _rows(t, d)], out_shape=[SDS((s, d), F32), SDS((s, d), BF16)],
               sem=("parallel",))(m, g_post, res, g_pre)


def ple_fwd_mid(name, z, pp, x2, g_next):
    s, d = z.shape
    t = _tile(s, ROW_TILE)

    def body(z_ref, p_ref, x_ref, g_ref, xo_ref, h_ref):
        xn = x_ref[...] + p_ref[...] * _sigmoid(z_ref[...])
        xo_ref[...] = xn
        h_ref[...] = ((xn * _rstd(xn)) * g_ref[...]).astype(BF16)

    return _pc(body, name=name, grid=(s // t,),
               in_specs=[_rows(t, d), _rows(t, d), _rows(t, d), _fixed((1, d))],
               out_specs=[_rows(t, d), _rows(t, d)], out_shape=[SDS((s, d), F32), SDS((s, d), BF16)],
               sem=("parallel",))(z, pp, x2, g_next)


def ple_fwd_loss(name, z, pp, x2, target):
    s, d = z.shape
    t = _tile(s, ROW_TILE)

    def body(z_ref, p_ref, x_ref, t_ref, dy_ref, l_ref):
        err = (x_ref[...] + p_ref[...] * _sigmoid(z_ref[...])) - t_ref[...]
        dy_ref[...] = err * (1.0 / d)
        part = jnp.sum(jnp.sum(err * err, axis=-1, keepdims=True), axis=0, keepdims=True) * (0.5 / d)

        @pl.when(pl.program_id(0) == 0)
        def _():
            l_ref[...] = part

        @pl.when(pl.program_id(0) > 0)
        def _():
            l_ref[...] += part

    return _pc(body, name=name, grid=(s // t,),
               in_specs=[_rows(t, d), _rows(t, d), _rows(t, d), _rows(t, d)],
               out_specs=[_rows(t, d), _fixed((1, 1))], out_shape=[SDS((s, d), F32), SDS((1, 1), F32)],
               sem=("arbitrary",))(z, pp, x2, target)


def ple_bwd(name, dx3, z, pp):
    s, d = z.shape
    t = _tile(s, ROW_TILE)

    def body(d_ref, z_ref, p_ref, dpp_ref, dz_ref):
        gate = _sigmoid(z_ref[...])
        dv = d_ref[...]
        dpp_ref[...] = (dv * gate).astype(BF16)
        dz_ref[...] = ((dv * p_ref[...]) * (gate * (1.0 - gate))).astype(BF16)

    return _pc(body, name=name, grid=(s // t,), in_specs=[_rows(t, d)] * 3, out_specs=[_rows(t, d)] * 2,
               out_shape=[SDS((s, d), BF16)] * 2, sem=("parallel",))(dx3, z, pp)


def norm_bwd(name, xin, gain, dy, dres=None, out_dtype=F32):
    s, d = xin.shape
    t = _tile(s, ROW_TILE)

    def body(*refs):
        if dres is None:
            x_ref, g_ref, dy_ref, dx_ref, dg_ref = refs
            r_ref = None
        else:
            x_ref, g_ref, dy_ref, r_ref, dx_ref, dg_ref = refs
        xv = x_ref[...]
        r = _rstd(xv)
        xh = xv * r
        dyv = dy_ref[...].astype(F32)
        dyg = dyv * g_ref[...]
        c = jnp.mean(dyg * xh, axis=-1, keepdims=True)
        dx = r * (dyg - xh * c)
        if r_ref is not None:
            dx = dx + r_ref[...]
        dx_ref[...] = dx.astype(out_dtype)
        part = jnp.sum(dyv * xh, axis=0, keepdims=True)

        @pl.when(pl.program_id(0) == 0)
        def _():
            dg_ref[...] = part

        @pl.when(pl.program_id(0) > 0)
        def _():
            dg_ref[...] += part

    ins = [xin, gain, dy] + ([] if dres is None else [dres])
    specs = [_rows(t, d), _fixed((1, d)), _rows(t, d)] + ([] if dres is None else [_rows(t, d)])
    return _pc(body, name=name, grid=(s // t,), in_specs=specs, out_specs=[_rows(t, d), _fixed((1, d))],
               out_shape=[SDS((s, d), out_dtype), SDS((1, d), F32)], sem=("arbitrary",))(*ins)


def swiglu_fwd(name, g, u):
    s, f = g.shape
    t = _tile(s, ROW_TILE)
    w = f // N_CHIPS
    spec = pl.BlockSpec((t, w), lambda i, j: (i, j))

    def body(g_ref, u_ref, a_ref):
        gv = g_ref[...]
        a_ref[...] = ((gv * _sigmoid(gv)) * u_ref[...]).astype(BF16)

    return _pc(body, name=name, grid=(s // t, N_CHIPS), in_specs=[spec, spec], out_specs=spec,
               out_shape=SDS((s, f), BF16), sem=("parallel", "parallel"))(g, u)


def swiglu_bwd(name, g, u, da):
    s, f = g.shape
    t = _tile(s, ROW_TILE)
    w = f // N_CHIPS
    spec = pl.BlockSpec((t, w), lambda i, j: (i, j))

    def body(g_ref, u_ref, da_ref, dg_ref, du_ref):
        gv = g_ref[...]
        sg = _sigmoid(gv)
        dav = da_ref[...]
        du_ref[...] = (dav * (gv * sg)).astype(BF16)
        dg_ref[...] = ((dav * u_ref[...]) * (sg * (1.0 + gv * (1.0 - sg)))).astype(BF16)

    return _pc(body, name=name, grid=(s // t, N_CHIPS), in_specs=[spec] * 3, out_specs=[spec] * 2,
               out_shape=[SDS((s, f), BF16)] * 2, sem=("parallel", "parallel"))(g, u, da)


def _lane(shape):
    return lax.broadcasted_iota(jnp.int32, shape, 1)


def _swap_halves(x):
    lo = (_lane(x.shape) % 64) < 32
    return jnp.where(lo, pltpu.roll(x, 96, 1), pltpu.roll(x, 32, 1))


def rope_tables(name, pos_col, inv_freq):
    s = pos_col.shape[0]
    t = _tile(s, ROW_TILE)

    def body(p_ref, f_ref, c_ref, s_ref):
        ang = p_ref[...].astype(F32) * f_ref[...]
        lo = (_lane(ang.shape) % 64) < 32
        c_ref[...] = jnp.cos(ang)
        sn = jnp.sin(ang)
        s_ref[...] = jnp.where(lo, -sn, sn)

    return _pc(body, name=name, grid=(s // t,), in_specs=[_rows(t, 1), _fixed((1, LANES))],
               out_specs=[_rows(t, LANES)] * 2, out_shape=[SDS((s, LANES), F32)] * 2, sem=("parallel",))(pos_col, inv_freq)


def _pad_heads(chunk, lo_mask):
    zero = jnp.zeros_like(chunk)
    return jnp.where(lo_mask, chunk, zero), jnp.where(lo_mask, pltpu.roll(chunk, 64, 1), zero)


def rope_qkv(name, proj, cos, sin):
    s = proj.shape[0]
    t = _tile(s, ROW_TILE)

    def body(q_ref, kv_ref, c_ref, s_ref, qp_ref, kp_ref, vp_ref):
        cs, sn = c_ref[...], s_ref[...]
        lo_mask = _lane(cs.shape) < 64

        def rot(x):
            return x * cs + _swap_halves(x) * sn

        for j in range(ATTN_WIDTH // LANES):
            a, b = _pad_heads(rot(q_ref[:, j * LANES:(j + 1) * LANES]), lo_mask)
            qp_ref[:, (2 * j) * LANES:(2 * j + 1) * LANES] = a.astype(BF16)
            qp_ref[:, (2 * j + 1) * LANES:(2 * j + 2) * LANES] = b.astype(BF16)
        for j in range(KV_WIDTH // LANES):
            a, b = _pad_heads(rot(kv_ref[:, j * LANES:(j + 1) * LANES]), lo_mask)
            kp_ref[:, (2 * j) * LANES:(2 * j + 1) * LANES] = a.astype(BF16)
            kp_ref[:, (2 * j + 1) * LANES:(2 * j + 2) * LANES] = b.astype(BF16)
            a, b = _pad_heads(kv_ref[:, KV_WIDTH + j * LANES:KV_WIDTH + (j + 1) * LANES], lo_mask)
            vp_ref[:, (2 * j) * LANES:(2 * j + 1) * LANES] = a.astype(BF16)
            vp_ref[:, (2 * j + 1) * LANES:(2 * j + 2) * LANES] = b.astype(BF16)

    return _pc(body, name=name, grid=(s // t,),
               in_specs=[_rows(t, ATTN_WIDTH, 0), _rows(t, 2 * KV_WIDTH, 2), _rows(t, LANES), _rows(t, LANES)],
               out_specs=[_rows(t, N_Q_HEADS * LANES), _rows(t, N_KV_HEADS * LANES), _rows(t, N_KV_HEADS * LANES)],
               out_shape=[SDS((s, N_Q_HEADS * LANES), BF16), SDS((s, N_KV_HEADS * LANES), BF16),
                          SDS((s, N_KV_HEADS * LANES), BF16)],
               sem=("parallel",))(proj, proj, cos, sin)


def _attn_mask(n):
    L = WINDOW
    qi = lax.broadcasted_iota(jnp.int32, (L, 2 * L), 0) + L
    ki = lax.broadcasted_iota(jnp.int32, (L, 2 * L), 1)
    rel = qi - ki
    return (rel >= 0) & (rel < WINDOW) & ((n > 0) | (ki >= L))


def _attn_probs(qh, kk, valid, sink):
    sc = lax.dot_general(qh, kk, _NT, preferred_element_type=F32) * 0.125
    sc = jnp.where(valid, sc, MASK_VALUE)
    m = jnp.maximum(jnp.max(sc, axis=-1, keepdims=True), sink)
    e = jnp.exp(sc - m)
    es = jnp.exp(sink - m)
    den = jnp.sum(e, axis=-1, keepdims=True) + es
    return e / den, es / den


def _attn_specs(s):
    L = WINDOW
    cur = lambda n: (n, 0)
    prev = lambda n: (jnp.maximum(n - 1, 0), 0)
    kvw = N_KV_HEADS * LANES
    return [pl.BlockSpec((L, N_Q_HEADS * LANES), cur), pl.BlockSpec((L, kvw), cur), pl.BlockSpec((L, kvw), prev),
            pl.BlockSpec((L, kvw), cur), pl.BlockSpec((L, kvw), prev), pl.BlockSpec(memory_space=pltpu.SMEM)]


def attn_fwd(name, qp, kp, vp, sinks):
    s = qp.shape[0]
    L = WINDOW

    def body(q_ref, kc_ref, kp_ref, vc_ref, vp_ref, sk_ref, o_ref):
        valid = _attn_mask(pl.program_id(0))
        outs = []
        for kvh in range(N_KV_HEADS):
            cols = slice(kvh * LANES, (kvh + 1) * LANES)
            kk = jnp.concatenate([kp_ref[:, cols], kc_ref[:, cols]], axis=0)
            vv = jnp.concatenate([vp_ref[:, cols], vc_ref[:, cols]], axis=0)
            for g in range(Q_PER_KV):
                h = kvh * Q_PER_KV + g
                p, _ = _attn_probs(q_ref[:, h * LANES:(h + 1) * LANES], kk, valid, sk_ref[h])
                outs.append(jnp.dot(p.astype(BF16), vv, preferred_element_type=F32))
        for j in range(ATTN_WIDTH // LANES):
            o_ref[:, j * LANES:(j + 1) * LANES] = outs[2 * j] + pltpu.roll(outs[2 * j + 1], 64, 1)

    return _pc(body, name=name, grid=(s // L,), in_specs=_attn_specs(s),
               out_specs=pl.BlockSpec((L, ATTN_WIDTH), lambda n: (n, 0)),
               out_shape=SDS((s, ATTN_WIDTH), F32), sem=("parallel",))(qp, kp, kp, vp, vp, sinks)


def attn_bwd(name, qp, kp, vp, sinks, dattn):
    s = qp.shape[0]
    L = WINDOW
    kvw = N_KV_HEADS * LANES

    def body(q_ref, kc_ref, kp_ref, vc_ref, vp_ref, sk_ref, do_ref, dq_ref, dkc_ref, dkp_ref, dvc_ref, dvp_ref, ds_ref):
        n = pl.program_id(0)
        valid = _attn_mask(n)
        lo_mask = _lane((L, LANES)) < 64
        lane1 = _lane((1, LANES))
        dsink = jnp.zeros((1, LANES), F32)
        for kvh in range(N_KV_HEADS):
            cols = slice(kvh * LANES, (kvh + 1) * LANES)
            kk = jnp.concatenate([kp_ref[:, cols], kc_ref[:, cols]], axis=0)
            vv = jnp.concatenate([vp_ref[:, cols], vc_ref[:, cols]], axis=0)
            dkk = jnp.zeros((2 * L, LANES), F32)
            dvv = jnp.zeros((2 * L, LANES), F32)
            for g in range(Q_PER_KV):
                h = kvh * Q_PER_KV + g
                qh = q_ref[:, h * LANES:(h + 1) * LANES]
                p, ps = _attn_probs(qh, kk, valid, sk_ref[h])
                chunk = do_ref[:, (h // 2) * LANES:(h // 2 + 1) * LANES]
                if h % 2:
                    chunk = pltpu.roll(chunk, 64, 1)
                doh = jnp.where(lo_mask, chunk, 0.0).astype(BF16)
                dp = lax.dot_general(doh, vv, _NT, preferred_element_type=F32)
                delta = jnp.sum(p * dp, axis=-1, keepdims=True)
                dsb = ((p * (dp - delta)) * 0.125).astype(BF16)
                dsink = dsink + jnp.where(lane1 == h, -jnp.sum(ps * delta, axis=0, keepdims=True), 0.0)
                dq_ref[:, h * LANES:(h + 1) * LANES] = jnp.dot(dsb, kk, preferred_element_type=F32)
                dkk = dkk + lax.dot_general(dsb, qh, _TN, preferred_element_type=F32)
                dvv = dvv + lax.dot_general(p.astype(BF16), doh, _TN, preferred_element_type=F32)
            dkp_ref[:, cols] = dkk[:L]
            dkc_ref[:, cols] = dkk[L:]
            dvp_ref[:, cols] = dvv[:L]
            dvc_ref[:, cols] = dvv[L:]

        @pl.when(n == 0)
        def _():
            ds_ref[...] = dsink

        @pl.when(n > 0)
        def _():
            ds_ref[...] += dsink

    blk = lambda w: pl.BlockSpec((L, w), lambda n: (n, 0))
    return _pc(body, name=name, grid=(s // L,), in_specs=_attn_specs(s) + [blk(ATTN_WIDTH)],
               out_specs=[blk(N_Q_HEADS * LANES), blk(kvw), blk(kvw), blk(kvw), blk(kvw), _fixed((1, LANES))],
               out_shape=[SDS((s, N_Q_HEADS * LANES), F32)] + [SDS((s, kvw), F32)] * 4 + [SDS((1, LANES), F32)],
               sem=("arbitrary",))(qp, kp, kp, vp, vp, sinks, dattn)


def rope_bwd(name, dqp, dkc, dkp, dvc, dvp, cos, sin):
    s = dqp.shape[0]
    L = WINDOW
    nb = s // L
    kvw = N_KV_HEADS * LANES

    def body(dq_ref, dkc_ref, dkp_ref, dvc_ref, dvp_ref, c_ref, s_ref, o_ref):
        cs, sn = c_ref[...], s_ref[...]
        more = (pl.program_id(0) < nb - 1).astype(F32)

        def unrot(x):
            return x * cs - _swap_halves(x) * sn

        def compact(ref, j, nxt=None):
            a = ref[:, (2 * j) * LANES:(2 * j + 1) * LANES]
            b = ref[:, (2 * j + 1) * LANES:(2 * j + 2) * LANES]
            if nxt is not None:
                a = a + more * nxt[:, (2 * j) * LANES:(2 * j + 1) * LANES]
                b = b + more * nxt[:, (2 * j + 1) * LANES:(2 * j + 2) * LANES]
            return a + pltpu.roll(b, 64, 1)

        for j in range(ATTN_WIDTH // LANES):
            o_ref[:, j * LANES:(j + 1) * LANES] = unrot(compact(dq_ref, j)).astype(BF16)
        for j in range(KV_WIDTH // LANES):
            o_ref[:, (COL_K + j) * LANES:(COL_K + j + 1) * LANES] = unrot(compact(dkc_ref, j, dkp_ref)).astype(BF16)
            o_ref[:, (COL_V + j) * LANES:(COL_V + j + 1) * LANES] = compact(dvc_ref, j, dvp_ref).astype(BF16)

    cur = lambda n: (n, 0)
    nxt = lambda n: (jnp.minimum(n + 1, nb - 1), 0)
    return _pc(body, name=name, grid=(nb,),
               in_specs=[pl.BlockSpec((L, N_Q_HEADS * LANES), cur), pl.BlockSpec((L, kvw), cur), pl.BlockSpec((L, kvw), nxt),
                         pl.BlockSpec((L, kvw), cur), pl.BlockSpec((L, kvw), nxt), pl.BlockSpec((L, LANES), cur),
                         pl.BlockSpec((L, LANES), cur)],
               out_specs=pl.BlockSpec((L, COL_HQ * LANES), cur), out_shape=SDS((s, COL_HQ * LANES), BF16),
               sem=("parallel",))(dqp, dkc, dkp, dvc, dvp, cos, sin)


def _split3(x):
    a = x.astype(BF16)
    r = x - a.astype(F32)
    b = r.astype(BF16)
    c = (r - b.astype(F32)).astype(BF16)
    return a, b, c


def _chunk_sum(x, upper):
    t = x.shape[0]
    ri = lax.broadcasted_iota(jnp.int32, (t, t), 0)
    ci = lax.broadcasted_iota(jnp.int32, (t, t), 1)
    same = (ri // HGRN_CHUNK) == (ci // HGRN_CHUNK)
    tri = (ci >= ri) if upper else (ci <= ri)
    m = jnp.where(same & tri, 1.0, 0.0).astype(BF16)
    out = None
    for part in _split3(x):
        y = jnp.dot(m, part, preferred_element_type=F32)
        out = y if out is None else out + y
    return out


def _lower_bound(l_ref, layer):
    lv = l_ref[...]
    e = jnp.exp(lv - jnp.max(lv, axis=0, keepdims=True))
    sm = e / jnp.sum(e, axis=0, keepdims=True)
    s0 = sm[0:1]
    return (s0 - s0) if layer == 0 else ((s0 + sm[1:2]) - s0)


def _hgrn_gates(hq_ref, hf_ref, lb):
    z = hf_ref[...]
    sg = _sigmoid(z)
    f = lb + (1.0 - lb) * sg
    kin = (1.0 - lb) * _sigmoid(-z)
    hq = hq_ref[...]
    sq = _sigmoid(hq)
    return sg, f, kin, hq, sq


def _shift_down(x, d):
    return x if d == 0 else pltpu.roll(x, d, 0)


def _shift_up(x, d):
    return x if d == 0 else pltpu.roll(x, x.shape[0] - d, 0)


def _intra_iotas(t):
    rows = lax.broadcasted_iota(jnp.int32, (t, LANES), 0)
    return rows % LANES, lax.broadcasted_iota(jnp.int32, (t, LANES), 1), rows % HGRN_CHUNK


def _hgrn_specs(t, rev, nt):
    row = (lambda h, i: nt - 1 - i) if rev else (lambda h, i: i)
    col = lambda base: pl.BlockSpec((t, LANES), lambda h, i, base=base: (row(h, i), base + h))
    return col, row


def hgrn_fwd(name, proj, lb_logits, layer):
    s = proj.shape[0]
    t = _tile(s, ROW_TILE)
    nt = s // t
    nc = t // HGRN_CHUNK
    col, row = _hgrn_specs(t, False, nt)

    def body(hq_ref, hf_ref, hi_ref, l_ref, o_ref, st_ref, state):
        @pl.when(pl.program_id(1) == 0)
        def _():
            state[...] = jnp.zeros_like(state)

        lb = _lower_bound(l_ref, layer)
        sg, f, kin, hq, sq = _hgrn_gates(hq_ref, hf_ref, lb)
        q = hq * sq
        vb = hi_ref[...].astype(BF16)
        b = _chunk_sum(jnp.log(f), False)
        row, lane, tmod = _intra_iotas(t)
        amat = jnp.where(lane == row, jnp.sum(q * kin, axis=-1, keepdims=True), 0.0)
        for d in range(1, HGRN_CHUNK):
            e = jnp.exp(jnp.where(tmod >= d, b - _shift_down(b, d), MASK_VALUE))
            a = jnp.sum((q * _shift_down(kin, d)) * e, axis=-1, keepdims=True)
            amat = jnp.where(lane == row - d, a, amat)
        ab = amat.astype(BF16)
        o_intra = jnp.concatenate([jnp.dot(ab[j * LANES:(j + 1) * LANES], vb[j * LANES:(j + 1) * LANES], preferred_element_type=F32)
                                   for j in range(t // LANES)], axis=0)
        qe = (q * jnp.exp(b)).astype(BF16)
        st = state[...]
        for c in range(nc):
            rs = slice(c * HGRN_CHUNK, (c + 1) * HGRN_CHUNK)
            bc = b[rs]
            bl = bc[HGRN_CHUNK - 1:HGRN_CHUNK, :]
            ke = (kin[rs] * jnp.exp(bl - bc)).astype(BF16)
            st_ref[c] = st
            o_ref[rs, :] = o_intra[rs] + lax.dot_general(qe[rs], st.astype(BF16), _NT, preferred_element_type=F32)
            st = st * jnp.exp(bl) + lax.dot_general(vb[rs], ke, _TN, preferred_element_type=F32)
        state[...] = st

    return _pc(body, name=name, grid=(HGRN_HEADS, nt),
               in_specs=[col(COL_HQ), col(COL_HF), col(COL_HI), pl.BlockSpec((2, LANES), lambda h, i: (0, h))],
               out_specs=[pl.BlockSpec((t, LANES), lambda h, i: (i, h)),
                          pl.BlockSpec((nc, None, LANES, LANES), lambda h, i: (i, h, 0, 0))],
               out_shape=[SDS((s, HGRN_WIDTH), F32), SDS((s // HGRN_CHUNK, HGRN_HEADS, LANES, LANES), F32)],
               scratch=[pltpu.VMEM((LANES, LANES), F32)],
               sem=("parallel", "arbitrary"))(proj, proj, proj, lb_logits)


def hgrn_bwd(name, proj, lb_logits, layer, states, do):
    s = proj.shape[0]
    t = _tile(s, ROW_TILE)
    nt = s // t
    nc = t // HGRN_CHUNK
    col, row = _hgrn_specs(t, True, nt)

    def body(hq_ref, hf_ref, hi_ref, l_ref, st_ref, do_ref, dhq_ref, dhf_ref, dhi_ref, dlb_ref, dstate):
        first = pl.program_id(1) == 0

        @pl.when(first)
        def _():
            dstate[...] = jnp.zeros_like(dstate)

        lb = _lower_bound(l_ref, layer)
        sg, f, kin, hq, sq = _hgrn_gates(hq_ref, hf_ref, lb)
        q = hq * sq
        vb = hi_ref[...].astype(BF16)
        b = _chunk_sum(jnp.log(f), False)
        dob = do_ref[...].astype(BF16)
        eb = jnp.exp(b)
        qe = q * eb
        qeb = qe.astype(BF16)
        last_row = lax.broadcasted_iota(jnp.int32, (HGRN_CHUNK, LANES), 0) == HGRN_CHUNK - 1

        dn = dstate[...]
        dq_c, dk_c, dv_c, db_c = [None] * nc, [None] * nc, [None] * nc, [None] * nc
        for c in reversed(range(nc)):
            rs = slice(c * HGRN_CHUNK, (c + 1) * HGRN_CHUNK)
            bc = b[rs]
            bl = bc[HGRN_CHUNK - 1:HGRN_CHUNK, :]
            ebl = jnp.exp(bl)
            ekb = jnp.exp(bl - bc)
            ke = kin[rs] * ekb
            st = st_ref[c]
            dnb = dn.astype(BF16)
            dqe = jnp.dot(dob[rs], st.astype(BF16), preferred_element_type=F32)
            dke = jnp.dot(vb[rs], dnb, preferred_element_type=F32)
            dv_c[c] = lax.dot_general(ke.astype(BF16), dnb, _NT, preferred_element_type=F32)
            dbl = jnp.sum(dn * st, axis=0, keepdims=True) * ebl + jnp.sum(dke * ke, axis=0, keepdims=True)
            dq_c[c] = dqe * eb[rs]
            dk_c[c] = dke * ekb
            db_c[c] = (dqe * qe[rs] - dke * ke) + jnp.where(last_row, dbl, 0.0)
            dn = dn * ebl + lax.dot_general(dob[rs], qeb[rs], _TN, preferred_element_type=F32)
        dstate[...] = dn

        row, lane, tmod = _intra_iotas(t)
        blocks = [slice(j * LANES, (j + 1) * LANES) for j in range(t // LANES)]
        damat = jnp.concatenate([lax.dot_general(dob[bs], vb[bs], _NT, preferred_element_type=F32) for bs in blocks], axis=0)
        on = lane == row
        da = jnp.sum(jnp.where(on, damat, 0.0), axis=-1, keepdims=True)
        amat = jnp.where(on, jnp.sum(q * kin, axis=-1, keepdims=True), 0.0)
        dq = jnp.concatenate(dq_c, axis=0) + da * kin
        dk = jnp.concatenate(dk_c, axis=0) + da * q
        db = jnp.concatenate(db_c, axis=0)
        for d in range(1, HGRN_CHUNK):
            on = lane == row - d
            ks = _shift_down(kin, d)
            e = jnp.exp(jnp.where(tmod >= d, b - _shift_down(b, d), MASK_VALUE))
            qd = q * e
            w = qd * ks
            amat = jnp.where(on, jnp.sum(w, axis=-1, keepdims=True), amat)
            da = jnp.sum(jnp.where(on, damat, 0.0), axis=-1, keepdims=True)
            dq = dq + da * (ks * e)
            daw = da * w
            db = db + daw - _shift_up(daw, d)
            dk = dk + _shift_up(da * qd, d)
        ab = amat.astype(BF16)
        dv = jnp.concatenate(dv_c, axis=0) + jnp.concatenate(
            [lax.dot_general(ab[bs], dob[bs], _TN, preferred_element_type=F32) for bs in blocks], axis=0)
        dg = _chunk_sum(db, True)
        dhq_ref[...] = (dq * (sq * (1.0 + hq * (1.0 - sq)))).astype(BF16)
        dhi_ref[...] = dv.astype(BF16)
        dfk = dg / f - dk
        dhf_ref[...] = ((dfk * (1.0 - lb)) * (sg * (1.0 - sg))).astype(BF16)
        part = jnp.sum(dfk * (1.0 - sg), axis=0, keepdims=True)

        @pl.when(first)
        def _():
            dlb_ref[...] = part

        @pl.when(jnp.logical_not(first))
        def _():
            dlb_ref[...] += part

    out_col = pl.BlockSpec((t, LANES), lambda h, i: (nt - 1 - i, h))
    return _pc(body, name=name, grid=(HGRN_HEADS, nt),
               in_specs=[col(COL_HQ), col(COL_HF), col(COL_HI), pl.BlockSpec((2, LANES), lambda h, i: (0, h)),
                         pl.BlockSpec((nc, None, LANES, LANES), lambda h, i: (nt - 1 - i, h, 0, 0)), out_col],
               out_specs=[out_col, out_col, out_col, pl.BlockSpec((1, LANES), lambda h, i: (0, h))],
               out_shape=[SDS((s, HGRN_WIDTH), BF16)] * 3 + [SDS((1, HGRN_WIDTH), F32)],
               scratch=[pltpu.VMEM((LANES, LANES), F32)],
               sem=("parallel", "arbitrary"))(proj, proj, proj, lb_logits, states, do)


def mix_out_fwd(name, attn, o, proj, g_attn, g_hgrn):
    s = attn.shape[0]
    t = _tile(s, ROW_TILE)
    half = HGRN_WIDTH // 2

    def body(a_ref, o_ref, hg0_ref, hg1_ref, ga_ref, gh_ref, c_ref):
        av = a_ref[...]
        c_ref[:, :ATTN_WIDTH] = ((av * _rstd(av)) * ga_ref[...]).astype(BF16)
        for j in range(HGRN_HEADS):
            cols = slice(j * LANES, (j + 1) * LANES)
            ov = o_ref[:, cols]
            hg_ref, hcols = (hg0_ref, cols) if j < 4 else (hg1_ref, slice((j - 4) * LANES, (j - 3) * LANES))
            hg = hg_ref[:, hcols]
            on = (ov * _rstd(ov)) * gh_ref[:, cols]
            c_ref[:, ATTN_WIDTH + j * LANES:ATTN_WIDTH + (j + 1) * LANES] = (on * (hg * _sigmoid(hg))).astype(BF16)

    return _pc(body, name=name, grid=(s // t,),
               in_specs=[_rows(t, ATTN_WIDTH), _rows(t, HGRN_WIDTH), _rows(t, half, COL_HG // 4), _rows(t, half, COL_HG // 4 + 1),
                         _fixed((1, ATTN_WIDTH)), _fixed((1, HGRN_WIDTH))],
               out_specs=_rows(t, D_MODEL), out_shape=SDS((s, D_MODEL), BF16), sem=("parallel",))(attn, o, proj, proj, g_attn, g_hgrn)


def mix_out_bwd(name, dcat, attn, o, proj, g_attn, g_hgrn):
    s = attn.shape[0]
    t = _tile(s, ROW_TILE)
    half = HGRN_WIDTH // 2

    def body(dc_ref, a_ref, o_ref, hg0_ref, hg1_ref, ga_ref, gh_ref, da_ref, do_ref, dhg_ref, dga_ref, dgh_ref, pa_s, ph_s):
        av = a_ref[...]
        r = _rstd(av)
        xh = av * r
        dyv = dc_ref[:, :ATTN_WIDTH]
        dyg = dyv * ga_ref[...]
        da_ref[...] = r * (dyg - xh * jnp.mean(dyg * xh, axis=-1, keepdims=True))
        pa_s[...] = jnp.sum(dyv * xh, axis=0, keepdims=True)
        for j in range(HGRN_HEADS):
            cols = slice(j * LANES, (j + 1) * LANES)
            ov = o_ref[:, cols]
            hg_ref, hcols = (hg0_ref, cols) if j < 4 else (hg1_ref, slice((j - 4) * LANES, (j - 3) * LANES))
            hg = hg_ref[:, hcols]
            sg = _sigmoid(hg)
            r = _rstd(ov)
            xh = ov * r
            gain = gh_ref[:, cols]
            dh = dc_ref[:, ATTN_WIDTH + j * LANES:ATTN_WIDTH + (j + 1) * LANES]
            dhg_ref[:, cols] = ((dh * (xh * gain)) * (sg * (1.0 + hg * (1.0 - sg)))).astype(BF16)
            dyv = dh * (hg * sg)
            dyg = dyv * gain
            do_ref[:, cols] = r * (dyg - xh * jnp.mean(dyg * xh, axis=-1, keepdims=True))
            ph_s[:, cols] = jnp.sum(dyv * xh, axis=0, keepdims=True)

        @pl.when(pl.program_id(0) == 0)
        def _():
            dga_ref[...] = pa_s[...]
            dgh_ref[...] = ph_s[...]

        @pl.when(pl.program_id(0) > 0)
        def _():
            dga_ref[...] += pa_s[...]
            dgh_ref[...] += ph_s[...]

    return _pc(body, name=name, grid=(s // t,),
               in_specs=[_rows(t, D_MODEL), _rows(t, ATTN_WIDTH), _rows(t, HGRN_WIDTH), _rows(t, half, COL_HG // 4),
                         _rows(t, half, COL_HG // 4 + 1), _fixed((1, ATTN_WIDTH)), _fixed((1, HGRN_WIDTH))],
               out_specs=[_rows(t, ATTN_WIDTH), _rows(t, HGRN_WIDTH), _rows(t, HGRN_WIDTH), _fixed((1, ATTN_WIDTH)),
                          _fixed((1, HGRN_WIDTH))],
               out_shape=[SDS((s, ATTN_WIDTH), F32), SDS((s, HGRN_WIDTH), F32), SDS((s, HGRN_WIDTH), BF16),
                          SDS((1, ATTN_WIDTH), F32), SDS((1, HGRN_WIDTH), F32)],
               scratch=[pltpu.VMEM((1, ATTN_WIDTH), F32), pltpu.VMEM((1, HGRN_WIDTH), F32)],
               sem=("arbitrary",))(dcat, attn, o, proj, proj, g_attn, g_hgrn)


BIG = (("w_in", 2048, 1408, 0), ("w_out", 512, 2048, 1), ("w_ffn_gate", 2048, 1408, 0), ("w_ffn_up", 2048, 1408, 0),
       ("w_ffn_down", 1408, 2048, 1), ("w_ple_gate", 512, 2048, 1), ("w_ple_proj", 256, 512, 0))
BIG_BY_NAME = {spec[0]: spec for spec in BIG}
HBM_SPEC = pl.BlockSpec(memory_space=pltpu.HBM)
SEM_SPEC = pl.BlockSpec(memory_space=pltpu.SEMAPHORE)
TOKEN_SHAPE = (8, LANES)


def _split_call(body, *, name, in_specs, out_specs, out_shape, aliases):
    return pl.pallas_call(body, name=name, in_specs=in_specs, out_specs=out_specs, out_shape=out_shape,
                          input_output_aliases=aliases,
                          compiler_params=pltpu.CompilerParams(has_side_effects=pltpu.SideEffectType.DATAFLOW_SIDE_EFFECTING))


def _in_hbm(arrays):
    return [pltpu.with_memory_space_constraint(a, pltpu.HBM) for a in arrays]


def cast_to_slot(name, place, w, layer, after):
    _, r, c = w.shape
    tr = _tile(r, 512)

    def body(place_ref, w_ref, after_ref, o_ref):
        o_ref[...] = w_ref[...].astype(BF16)

    gs = pltpu.PrefetchScalarGridSpec(
        num_scalar_prefetch=1, grid=(r // tr,),
        in_specs=[pl.BlockSpec((None, tr, c), lambda i, pr: (layer, i, 0)), pl.BlockSpec(memory_space=pl.ANY)],
        out_specs=pl.BlockSpec((None, tr, c), lambda i, pr: (pr[1], i, 0)))
    return _pc(body, name=name, grid_spec=gs, in_specs=None, out_specs=None, out_shape=SDS((N_CHIPS, r, c), BF16),
               sem=("parallel",))(place, w, after)


def _place():
    x, y, c = lax.axis_index("x"), lax.axis_index("y"), lax.axis_index("c")
    chips = [(1 - x, y), (x, 1 - y), (1 - x, 1 - y)]
    return x, y, c, chips


def _half(ref, axis, c, rows, cols):
    if axis == 0:
        return ref.at[pl.ds(pl.multiple_of(c * (rows // 2), 16), rows // 2), :]
    return ref.at[:, pl.ds(pl.multiple_of(c * (cols // 2), LANES), cols // 2)]


def _gather_copies(specs, bufs, send, recv):
    x, y, c, chips = _place()
    cps = []
    for t, (_, rows, cols, axis) in enumerate(specs):
        mine = _half(bufs[t].at[2 * x + y], axis, c, rows, cols)
        for k, (cx, cy) in enumerate(chips):
            cps.append(pltpu.make_async_remote_copy(src_ref=mine, dst_ref=mine, send_sem=send.at[3 * t + k],
                                                    recv_sem=recv.at[3 * t + k], device_id=(cx, cy, c), device_id_type=MESH))
    return cps


def gather_start(name, specs, bufs):
    nt = len(bufs)
    n = 3 * nt

    def body(*refs):
        send, recv, token = refs[nt], refs[nt + 1], refs[-1]
        for cp in _gather_copies(specs, refs[:nt], send, recv):
            cp.start()
        token[...] = jnp.zeros(TOKEN_SHAPE, F32)

    out = _split_call(
        body, name=name, in_specs=[HBM_SPEC] * nt,
        out_specs=(SEM_SPEC, SEM_SPEC) + (HBM_SPEC,) * nt + (pl.BlockSpec(memory_space=pltpu.VMEM),),
        out_shape=(pltpu.SemaphoreType.DMA((n,)), pltpu.SemaphoreType.DMA((n,)))
        + tuple(pltpu.HBM(b.shape, b.dtype) for b in bufs) + (SDS(TOKEN_SHAPE, F32),),
        aliases={t: 2 + t for t in range(nt)})(*_in_hbm(bufs))
    return out[0], out[1], list(out[2:2 + nt]), out[-1]


def gather_wait(name, specs, send, recv, bufs, after):
    nt = len(bufs)

    def body(*refs):
        for cp in _gather_copies(specs, refs[:nt], refs[nt], refs[nt + 1]):
            cp.wait_send()
            cp.wait_recv()

    out = _split_call(
        body, name=name, in_specs=[HBM_SPEC] * nt + [SEM_SPEC, SEM_SPEC, pl.BlockSpec(memory_space=pl.ANY)],
        out_specs=(HBM_SPEC,) * nt, out_shape=tuple(pltpu.HBM(b.shape, b.dtype) for b in bufs),
        aliases={t: t for t in range(nt)})(*bufs, send, recv, after)
    return list(out)


def gather_pass(name, specs, bufs):
    nt = len(bufs)

    def body(*refs):
        ins, outs = refs[:nt], refs[nt:2 * nt]
        send, recv = refs[2 * nt:]
        x, y, c, chips = _place()
        cps = []
        for t, (_, rows, cols, axis) in enumerate(specs):
            for k, (cx, cy) in enumerate(chips):
                cp = pltpu.make_async_remote_copy(
                    src_ref=_half(ins[t].at[2 * cx + cy], axis, c, rows, cols),
                    dst_ref=_half(outs[t].at[2 * cx + cy], axis, c, rows, cols),
                    send_sem=send.at[3 * t + k], recv_sem=recv.at[3 * t + k], device_id=(x, y, 1 - c), device_id_type=MESH)
                cp.start()
                cps.append(cp)
        for t, (_, rows, cols, axis) in enumerate(specs):
            for k, (cx, cy) in enumerate(chips):
                theirs = _half(outs[t].at[2 * cx + cy], axis, 1 - c, rows, cols)
                pltpu.make_async_remote_copy(src_ref=theirs, dst_ref=theirs, send_sem=send.at[3 * t + k],
                                             recv_sem=recv.at[3 * t + k], device_id=(x, y, 1 - c), device_id_type=MESH).wait_recv()
        for cp in cps:
            cp.wait_send()

    return _pc(body, name=name, in_specs=[HBM_SPEC] * nt, out_specs=[HBM_SPEC] * nt,
               out_shape=[SDS(b.shape, b.dtype) for b in bufs], scratch=[pltpu.SemaphoreType.DMA((3 * nt,))] * 2,
               input_output_aliases={t: t for t in range(nt)})(*bufs)


def reduce_to_sibling(name, grads):
    nt = len(grads)

    def body(*refs):
        srcs, dsts = refs[:nt], refs[nt:2 * nt]
        send, recv = refs[2 * nt:]
        x, y, c, _ = _place()
        cps = []
        for t in range(nt):
            cp = pltpu.make_async_remote_copy(src_ref=srcs[t].at[1 - c], dst_ref=dsts[t], send_sem=send.at[t],
                                              recv_sem=recv.at[t], device_id=(x, y, 1 - c), device_id_type=MESH)
            cp.start()
            cps.append(cp)
        for cp in cps:
            cp.wait()

    return _pc(body, name=name, in_specs=[HBM_SPEC] * nt, out_specs=[HBM_SPEC] * nt,
               out_shape=[SDS(g.shape[1:], g.dtype) for g in grads],
               scratch=[pltpu.SemaphoreType.DMA((nt,))] * 2)(*grads)


def add_halves(name, place, grad, got):
    _, n, r, c = grad.shape
    tr = _tile(r, 256)

    def body(place_ref, g_ref, o_ref, part_ref, slot_ref):
        val = (g_ref[...].astype(F32) + o_ref[...].astype(F32)).astype(BF16)
        part_ref[...] = val

        @pl.when(pl.program_id(1) == place_ref[1])
        def _():
            slot_ref[...] = val

    gs = pltpu.PrefetchScalarGridSpec(
        num_scalar_prefetch=1, grid=(r // tr, n),
        in_specs=[pl.BlockSpec((None, None, tr, c), lambda i, j, pr: (pr[0], j, i, 0)),
                  pl.BlockSpec((None, tr, c), lambda i, j, pr: (j, i, 0))],
        out_specs=[pl.BlockSpec((None, tr, c), lambda i, j, pr: (j, i, 0)),
                   pl.BlockSpec((None, tr, c), lambda i, j, pr: (pr[1], i, 0))])
    return _pc(body, name=name, grid_spec=gs, in_specs=None, out_specs=None, out_shape=[SDS((n, r, c), BF16)] * 2,
               sem=("parallel", "arbitrary"))(place, grad, got)


def _chips_copies(parts, slots, send, recv):
    x, y, c, chips = _place()
    cps = []
    for t in range(len(parts)):
        for k, (cx, cy) in enumerate(chips):
            cps.append(pltpu.make_async_remote_copy(src_ref=parts[t].at[2 * cx + cy], dst_ref=slots[t].at[2 * x + y],
                                                    send_sem=send.at[3 * t + k], recv_sem=recv.at[3 * t + k],
                                                    device_id=(cx, cy, c), device_id_type=MESH))
    return cps


def chips_start(name, parts, slots):
    nt = len(parts)
    n = 3 * nt

    def body(*refs):
        send, recv, token = refs[2 * nt], refs[2 * nt + 1], refs[-1]
        for cp in _chips_copies(refs[:nt], refs[nt:2 * nt], send, recv):
            cp.start()
        token[...] = jnp.zeros(TOKEN_SHAPE, F32)

    both = list(parts) + list(slots)
    out = _split_call(
        body, name=name, in_specs=[HBM_SPEC] * (2 * nt),
        out_specs=(SEM_SPEC, SEM_SPEC) + (HBM_SPEC,) * (2 * nt) + (pl.BlockSpec(memory_space=pltpu.VMEM),),
        out_shape=(pltpu.SemaphoreType.DMA((n,)), pltpu.SemaphoreType.DMA((n,)))
        + tuple(pltpu.HBM(b.shape, b.dtype) for b in both) + (SDS(TOKEN_SHAPE, F32),),
        aliases={t: 2 + t for t in range(2 * nt)})(*_in_hbm(both))
    return out[0], out[1], list(out[2:2 + nt]), list(out[2 + nt:2 + 2 * nt]), out[-1]


def chips_wait(name, send, recv, parts, slots, after):
    nt = len(parts)

    def body(*refs):
        for cp in _chips_copies(refs[:nt], refs[nt:2 * nt], refs[2 * nt], refs[2 * nt + 1]):
            cp.wait_send()
            cp.wait_recv()

    both = list(parts) + list(slots)
    out = _split_call(
        body, name=name, in_specs=[HBM_SPEC] * (2 * nt) + [SEM_SPEC, SEM_SPEC, pl.BlockSpec(memory_space=pl.ANY)],
        out_specs=(HBM_SPEC,) * (2 * nt), out_shape=tuple(pltpu.HBM(b.shape, b.dtype) for b in both),
        aliases={t: t for t in range(2 * nt)})(*both, send, recv, after)
    return list(out[nt:])


def sum_chips(name, place, slots):
    n, r, c = slots.shape
    tr = _tile(r, 256)

    def body(place_ref, s_ref, o_ref):
        acc = s_ref[0].astype(F32)
        for k in range(1, n):
            acc = acc + s_ref[k].astype(F32)
        o_ref[...] = acc

    gs = pltpu.PrefetchScalarGridSpec(
        num_scalar_prefetch=1, grid=(r // tr,),
        in_specs=[pl.BlockSpec((n, tr, c), lambda i, pr: (0, i, 0))],
        out_specs=pl.BlockSpec((None, tr, c), lambda i, pr: (pr[0], i, 0)))
    return _pc(body, name=name, grid_spec=gs, in_specs=None, out_specs=None, out_shape=SDS((2, r, c), F32),
               sem=("parallel",))(place, slots)


def share_with_sibling(name, bufs):
    nt = len(bufs)

    def body(*refs):
        ins, outs = refs[:nt], refs[nt:2 * nt]
        send, recv = refs[2 * nt:]
        x, y, c, _ = _place()
        cps = []
        for t in range(nt):
            cp = pltpu.make_async_remote_copy(src_ref=ins[t].at[c], dst_ref=outs[t].at[c], send_sem=send.at[t], recv_sem=recv.at[t],
                                              device_id=(x, y, 1 - c), device_id_type=MESH)
            cp.start()
            cps.append(cp)
        for t in range(nt):
            theirs = outs[t].at[1 - c]
            pltpu.make_async_remote_copy(src_ref=theirs, dst_ref=theirs, send_sem=send.at[t], recv_sem=recv.at[t],
                                         device_id=(x, y, 1 - c), device_id_type=MESH).wait_recv()
        for cp in cps:
            cp.wait_send()

    return _pc(body, name=name, in_specs=[HBM_SPEC] * nt, out_specs=[HBM_SPEC] * nt,
               out_shape=[SDS(b.shape, F32) for b in bufs], scratch=[pltpu.SemaphoreType.DMA((nt,))] * 2,
               input_output_aliases={t: t for t in range(nt)})(*bufs)


def _adamw(w, g, m, v):
    m = ADAM_B1 * m + (1.0 - ADAM_B1) * g
    v = ADAM_B2 * v + (1.0 - ADAM_B2) * (g * g)
    m_hat = m / (1.0 - ADAM_B1 ** ADAM_STEP)
    v_hat = v / (1.0 - ADAM_B2 ** ADAM_STEP)
    delta = -ADAM_LR * (m_hat / (jnp.sqrt(v_hat) + ADAM_EPS) + ADAM_WD * w)
    return delta, m, v


def adamw_big(name, w, m, v, g0, g1, axis):
    _, r, c = w.shape
    _, rh, ch = g0.shape
    tr = _tile(rh, 256)
    nb = rh // tr
    if axis == 0:
        wspec = pl.BlockSpec((None, tr, ch), lambda l, h, i: (l, h * nb + i, 0))
    else:
        wspec = pl.BlockSpec((None, tr, ch), lambda l, h, i: (l, i, h))
    g0spec = pl.BlockSpec((None, tr, ch), lambda l, h, i: (h * (1 - l), i * (1 - l), 0))
    g1spec = pl.BlockSpec((None, tr, ch), lambda l, h, i: (h * l, i * l, 0))

    def body(w_ref, m_ref, v_ref, g0_ref, g1_ref, go_ref, d_ref, mo_ref, vo_ref):
        def run(g_ref):
            g = g_ref[...]
            delta, mn, vn = _adamw(w_ref[...], g, m_ref[...], v_ref[...])
            go_ref[...] = g
            d_ref[...] = delta
            mo_ref[...] = mn
            vo_ref[...] = vn

        @pl.when(pl.program_id(0) == 0)
        def _():
            run(g0_ref)

        @pl.when(pl.program_id(0) == 1)
        def _():
            run(g1_ref)

    return _pc(body, name=name, grid=(2, 2, nb), in_specs=[wspec, wspec, wspec, g0spec, g1spec], out_specs=[wspec] * 4,
               out_shape=[SDS(w.shape, F32)] * 4, sem=("parallel", "parallel", "parallel"))(w, m, v, g0, g1)


SMALL = (("pre_mix_gain", 2048), ("post_mix_gain", 2048), ("pre_ffn_gain", 2048), ("post_ffn_gain", 2048), ("ple_gain", 2048),
         ("attn_out_gain", 1024), ("hgrn_out_gain", 1024), ("hgrn_lb_logits", 1024), ("attn_sinks", 128))
SMALL_ROWS = sum(2 * w // LANES for _, w in SMALL)
SMALL_PAD = -(-SMALL_ROWS // 8) * 8
LB_ROW = sum(2 * w // LANES for _, w in SMALL[:7])


def _pack_small(parts):
    rows = []
    for nm, w in SMALL:
        a = parts[nm].astype(F32)
        if a.shape[1] != w:
            a = jnp.pad(a, ((0, 0), (0, w - a.shape[1])))
        rows.append(a.reshape(2 * w // LANES, LANES))
    rows.append(jnp.zeros((SMALL_PAD - SMALL_ROWS, LANES), F32))
    return jnp.concatenate(rows, axis=0)


def _unpack_small(packed, widths):
    out, r = {}, 0
    for nm, w in SMALL:
        n = 2 * w // LANES
        out[nm] = packed[r:r + n].reshape(2, w)[:, :widths[nm]]
        r += n
    return out


def allreduce_small(name, packed):
    rows = packed.shape[0]

    def body(x_ref, o_ref, buf, send, recv, own_sem):
        x, y, c, _ = _place()
        me = 4 * x + 2 * y + c
        own = pltpu.make_async_copy(x_ref, buf.at[me], own_sem)
        own.start()
        cps = []
        for k in range(1, 8):
            px, py, pc = x ^ (k >> 2), y ^ ((k >> 1) & 1), c ^ (k & 1)
            cp = pltpu.make_async_remote_copy(src_ref=x_ref, dst_ref=buf.at[me], send_sem=send.at[k - 1], recv_sem=recv.at[k - 1],
                                              device_id=(px, py, pc), device_id_type=MESH)
            cp.start()
            cps.append(cp)
        for k in range(1, 8):
            px, py, pc = x ^ (k >> 2), y ^ ((k >> 1) & 1), c ^ (k & 1)
            slot = buf.at[4 * px + 2 * py + pc]
            pltpu.make_async_remote_copy(src_ref=slot, dst_ref=slot, send_sem=send.at[k - 1], recv_sem=recv.at[k - 1],
                                         device_id=(px, py, pc), device_id_type=MESH).wait_recv()
        for cp in cps:
            cp.wait_send()
        own.wait()
        acc = buf[0]
        for k in range(1, 8):
            acc = acc + buf[k]
        o_ref[...] = acc

    vm = pl.BlockSpec(memory_space=pltpu.VMEM)
    return _pc(body, name=name, in_specs=[vm], out_specs=vm, out_shape=SDS((rows, LANES), F32),
               scratch=[pltpu.VMEM((8, rows, LANES), F32), pltpu.SemaphoreType.DMA((7,)), pltpu.SemaphoreType.DMA((7,)),
                        pltpu.SemaphoreType.DMA])(packed)


def adamw_small(name, w, m, v, g):
    rows = w.shape[0]
    n = HGRN_WIDTH // LANES

    def body(w_ref, m_ref, v_ref, g_ref, go_ref, d_ref, mo_ref, vo_ref):
        go_ref[...] = g_ref[...]
        l0 = w_ref[LB_ROW:LB_ROW + n, :]
        l1 = w_ref[LB_ROW + n:LB_ROW + 2 * n, :]
        mx = jnp.maximum(l0, l1)
        e0, e1 = jnp.exp(l0 - mx), jnp.exp(l1 - mx)
        s0, s1 = e0 / (e0 + e1), e1 / (e0 + e1)
        dlb1 = g_ref[LB_ROW + n:LB_ROW + 2 * n, :]
        inner = s1 * dlb1
        go_ref[LB_ROW:LB_ROW + n, :] = s0 * (0.0 - inner)
        go_ref[LB_ROW + n:LB_ROW + 2 * n, :] = s1 * (dlb1 - inner)
        delta, mn, vn = _adamw(w_ref[...], go_ref[...], m_ref[...], v_ref[...])
        d_ref[...] = delta
        mo_ref[...] = mn
        vo_ref[...] = vn

    vm = pl.BlockSpec(memory_space=pltpu.VMEM)
    return _pc(body, name=name, in_specs=[vm] * 4, out_specs=[vm] * 4, out_shape=[SDS((rows, LANES), F32)] * 4)(w, m, v, g)


def _layer_fwd(l, x, h1, p_l, w_in_g, rest_of_weights, gains, cos, sin, sinks, lb_logits, g_next, target):
    n = f"l{l}_"
    proj = mm_col(n + "in_proj", h1, w_in_g)
    qp, kp, vp = rope_qkv(n + "rope_qkv", proj, cos, sin)
    attn = attn_fwd(n + "attn_fwd", qp, kp, vp, sinks)
    o, states = hgrn_fwd(n + "hgrn_fwd", proj, lb_logits, l)
    cat = mix_out_fwd(n + "mix_out_fwd", attn, o, proj, gains["attn_out_gain"], gains["hgrn_out_gain"])
    wts = dict(rest_of_weights(cat), w_in=w_in_g)
    m = mm_row(n + "out_proj", cat, wts["w_out"])
    x1, h2 = post_pre_norm(n + "post_mix", m, gains["post_mix_gain"], x, gains["pre_ffn_gain"])
    g = mm_col(n + "ffn_gate", h2, wts["w_ffn_gate"])
    u = mm_col(n + "ffn_up", h2, wts["w_ffn_up"])
    a = swiglu_fwd(n + "swiglu_fwd", g, u)
    f = mm_row(n + "ffn_down", a, wts["w_ffn_down"])
    x2, h3 = post_pre_norm(n + "post_ffn", f, gains["post_ffn_gain"], x1, gains["ple_gain"])
    z = mm_row(n + "ple_gate", h3, wts["w_ple_gate"])
    pp = mm_col(n + "ple_proj", p_l, wts["w_ple_proj"])
    if target is None:
        out = ple_fwd_mid(n + "ple_fwd", z, pp, x2, g_next)
    else:
        out = ple_fwd_loss(n + "ple_loss", z, pp, x2, target)
    saved = dict(x=x, h1=h1, proj=proj, qp=qp, kp=kp, vp=vp, attn=attn, o=o, states=states, cat=cat, m=m, x1=x1, h2=h2,
                 g=g, u=u, a=a, f=f, x2=x2, h3=h3, z=z, pp=pp, p=p_l)
    return out, saved, wts


EARLY = ("w_ple_gate", "w_ple_proj", "w_ffn_down", "w_ffn_gate", "w_ffn_up")
LATE = ("w_out", "w_in")


def _layer_bwd_ffn(l, dx3, sv, wts, gains):
    n = f"l{l}_"
    dpp, dz = ple_bwd(n + "ple_bwd", dx3, sv["z"], sv["pp"])
    dh3 = mm_row_t(n + "ple_gate_dx", dz, wts["w_ple_gate"])
    dx2, d_ple_gain = norm_bwd(n + "ple_norm_bwd", sv["x2"], gains["ple_gain"], dh3, dx3)
    df, d_post_ffn = norm_bwd(n + "post_ffn_bwd", sv["f"], gains["post_ffn_gain"], dx2, None, BF16)
    da = mm_row_t(n + "ffn_down_dx", df, wts["w_ffn_down"])
    dg, du = swiglu_bwd(n + "swiglu_bwd", sv["g"], sv["u"], da)
    big = dict(
        w_ple_gate=mm_wg_row(n + "ple_gate_dw", sv["h3"], dz),
        w_ple_proj=mm_wg_col(n + "ple_proj_dw", sv["p"], dpp),
        w_ffn_down=mm_wg_row(n + "ffn_down_dw", sv["a"], df),
        w_ffn_gate=mm_wg_col(n + "ffn_gate_dw", sv["h2"], dg),
        w_ffn_up=mm_wg_col(n + "ffn_up_dw", sv["h2"], du),
    )
    return dict(dx2=dx2, dg=dg, du=du), big, dict(ple_gain=d_ple_gain, post_ffn_gain=d_post_ffn)


def _layer_bwd_mix(l, st, sv, wts, gains, cos, sin, sinks, lb_logits):
    n = f"l{l}_"
    dh2 = mm_col_t(n + "ffn_gate_dx", st["dg"], wts["w_ffn_gate"])
    dh2 = mm_col_t(n + "ffn_up_dx", st["du"], wts["w_ffn_up"], add=dh2)
    dx1, d_pre_ffn = norm_bwd(n + "pre_ffn_bwd", sv["x1"], gains["pre_ffn_gain"], dh2, st["dx2"])
    dm, d_post_mix = norm_bwd(n + "post_mix_bwd", sv["m"], gains["post_mix_gain"], dx1, None, BF16)
    dcat = mm_row_t(n + "out_proj_dx", dm, wts["w_out"])
    dattn, do, dhg, d_attn_gain, d_hgrn_gain = mix_out_bwd(n + "mix_out_bwd", dcat, sv["attn"], sv["o"], sv["proj"],
                                                            gains["attn_out_gain"], gains["hgrn_out_gain"])
    dqp, dkc, dkp, dvc, dvp, dsinks = attn_bwd(n + "attn_bwd", sv["qp"], sv["kp"], sv["vp"], sinks, dattn)
    dqkv = rope_bwd(n + "rope_bwd", dqp, dkc, dkp, dvc, dvp, cos, sin)
    dhq, dhf, dhi, dlb = hgrn_bwd(n + "hgrn_bwd", sv["proj"], lb_logits, l, sv["states"], do)
    dproj = jnp.concatenate([dqkv, dhq, dhf, dhi, dhg], axis=1)
    dh1 = mm_col_t(n + "in_proj_dx", dproj, wts["w_in"])
    dx, d_pre_mix = norm_bwd(n + "pre_mix_bwd", sv["x"], gains["pre_mix_gain"], dh1, dx1)
    big = dict(w_out=mm_wg_row(n + "out_proj_dw", sv["cat"], dm), w_in=mm_wg_col(n + "in_proj_dw", sv["h1"], dproj))
    small = dict(pre_mix_gain=d_pre_mix, post_mix_gain=d_post_mix, pre_ffn_gain=d_pre_ffn, attn_out_gain=d_attn_gain,
                 hgrn_out_gain=d_hgrn_gain, hgrn_lb_logits=dlb, attn_sinks=dsinks)
    return dx, big, small


def _layer_bwd(l, dx3, sv, wts, gains, cos, sin, sinks, lb_logits):
    st, early, small_a = _layer_bwd_ffn(l, dx3, sv, wts, gains)
    dx, late, small_b = _layer_bwd_mix(l, st, sv, wts, gains, cos, sin, sinks, lb_logits)
    return dx, {**early, **late}, {**small_a, **small_b}


def _reduce_start(tag, names, big, place):
    got = reduce_to_sibling(tag + "_reduce_to_sibling", [big[nm] for nm in names])
    pairs = [add_halves(f"{tag}_add_{nm}", place, big[nm], gt) for nm, gt in zip(names, got)]
    return chips_start(tag + "_chips_start", [pr[0] for pr in pairs], [pr[1] for pr in pairs])


def _reduce_finish(tag, names, started, place, after):
    send, recv, parts, slots, _ = started
    slots = chips_wait(tag + "_chips_wait", send, recv, parts, slots, after)
    bufs = [sum_chips(f"{tag}_sum_{nm}", place, sl) for nm, sl in zip(names, slots)]
    return dict(zip(names, share_with_sibling(tag + "_share_with_sibling", bufs)))


def _with_token(gains, name, token):
    out = dict(gains)
    out[name] = gains[name] + token[0, 0]
    return out


def kernel(x, p, positions, w_in, attn_sinks, hgrn_lb_logits, attn_out_gain, hgrn_out_gain, w_out, pre_mix_gain, post_mix_gain, pre_ffn_gain, post_ffn_gain, w_ffn_gate, w_ffn_up, w_ffn_down, ple_gain, w_ple_gate, w_ple_proj, loss_target, m_w_in, m_attn_sinks, m_hgrn_lb_logits, m_attn_out_gain, m_hgrn_out_gain, m_w_out, m_pre_mix_gain, m_post_mix_gain, m_pre_ffn_gain, m_post_ffn_gain, m_w_ffn_gate, m_w_ffn_up, m_w_ffn_down, m_ple_gain, m_w_ple_gate, m_w_ple_proj, v_w_in, v_attn_sinks, v_hgrn_lb_logits, v_attn_out_gain, v_hgrn_out_gain, v_w_out, v_pre_mix_gain, v_post_mix_gain, v_pre_ffn_gain, v_post_ffn_gain, v_w_ffn_gate, v_w_ffn_up, v_w_ffn_down, v_ple_gain, v_w_ple_gate, v_w_ple_proj):
    given = dict(locals())
    depth = 2
    place = jnp.stack([lax.axis_index("c"), 2 * lax.axis_index("x") + lax.axis_index("y")]).astype(jnp.int32)
    xs = x[0]
    tgt = loss_target[0]
    pos_col = positions.reshape(-1, 1)
    half = 32
    inv_freq = ROPE_THETA ** (-jnp.arange(half, dtype=F32) / half)
    inv_freq = jnp.tile(inv_freq, 4).reshape(1, LANES)
    gains = [{nm: given[nm][l:l + 1] for nm, _ in SMALL[:7]} for l in range(depth)]
    names = [nm for nm, *_ in BIG]
    first, others = names[:1], names[1:]

    def specs(nms):
        return [BIG_BY_NAME[nm] for nm in nms]

    def start_gather(tag, l, nms, after):
        return gather_start(tag + "_gather_start", specs(nms),
                            [cast_to_slot(f"{tag}_cast_{nm}", place, given[nm], l, after) for nm in nms])

    def finish_gather(tag, nms, started, after):
        bufs = gather_wait(tag + "_gather_wait", specs(nms), started[0], started[1], started[2], after)
        return dict(zip(nms, gather_pass(tag + "_gather_pass", specs(nms), bufs)))

    g0a = start_gather("l0a", 0, first, place)
    g0b = start_gather("l0b", 0, others, g0a[3])
    started = {}

    def rest_of_layer0(after):
        got = finish_gather("l0b", others, g0b, after)
        started["l1"] = start_gather("l1", 1, names, got["w_out"])
        return got

    cos, sin = rope_tables("rope_tables", pos_col, inv_freq)
    h1 = pre_norm("l0_pre_mix", xs, gains[0]["pre_mix_gain"])
    w_in0 = finish_gather("l0a", first, g0a, g0b[3])["w_in"]
    (x_mid, h1_next), sv0, wts0 = _layer_fwd(0, xs, h1, p[0, 0], w_in0, rest_of_layer0, gains[0], cos, sin, attn_sinks[0],
                                             hgrn_lb_logits, gains[1]["pre_mix_gain"], None)
    wts1 = finish_gather("l1", names, started["l1"], x_mid)
    (dy, loss_part), sv1, _ = _layer_fwd(1, x_mid, h1_next, p[1, 0], wts1["w_in"], lambda after: wts1, gains[1], cos, sin,
                                         attn_sinks[1], hgrn_lb_logits, None, tgt)

    dx_mid, big1, small1 = _layer_bwd(1, dy, sv1, wts1, gains[1], cos, sin, attn_sinks[1], hgrn_lb_logits)
    r1 = _reduce_start("l1", names, big1, place)
    st0, early0, small0 = _layer_bwd_ffn(0, dx_mid, sv0, wts0, _with_token(gains[0], "ple_gain", r1[4]))
    r0e = _reduce_start("l0e", EARLY, early0, place)
    dx0, late0, small0b = _layer_bwd_mix(0, st0, sv0, wts0, _with_token(gains[0], "pre_ffn_gain", r0e[4]), cos, sin,
                                         attn_sinks[0], hgrn_lb_logits)
    small0 = {**small0, **small0b}
    r0l = _reduce_start("l0l", LATE, late0, place)
    red1 = _reduce_finish("l1", names, r1, place, r0l[4])
    red0 = _reduce_finish("l0e", EARLY, r0e, place, red1[names[-1]])

    loss = lax.psum(loss_part[0, 0], ("x", "y", "c"))
    grad_x = dx0[None]

    out_big = {}
    for nm in EARLY:
        out_big[nm] = adamw_big("adamw_" + nm, given[nm], given["m_" + nm], given["v_" + nm], red0[nm], red1[nm], BIG_BY_NAME[nm][3])
    red0.update(_reduce_finish("l0l", LATE, r0l, place, out_big[EARLY[-1]][3]))
    for nm in LATE:
        out_big[nm] = adamw_big("adamw_" + nm, given[nm], given["m_" + nm], given["v_" + nm], red0[nm], red1[nm], BIG_BY_NAME[nm][3])

    widths = {nm: given[nm].shape[1] for nm, _ in SMALL}
    small_g = {nm: jnp.concatenate([small0[nm][:, :widths[nm]] if nm != "attn_sinks" else small0[nm][:, :LANES],
                                    small1[nm][:, :widths[nm]] if nm != "attn_sinks" else small1[nm][:, :LANES]], axis=0)
               for nm, _ in SMALL}
    g_sum = allreduce_small("allreduce_small", _pack_small(small_g))
    sm = adamw_small("adamw_small", _pack_small({nm: given[nm] for nm, _ in SMALL}),
                     _pack_small({nm: given["m_" + nm] for nm, _ in SMALL}),
                     _pack_small({nm: given["v_" + nm] for nm, _ in SMALL}), g_sum)
    out_small = [_unpack_small(a, widths) for a in sm]

    order = ["w_in", "attn_sinks", "hgrn_lb_logits", "attn_out_gain", "hgrn_out_gain", "w_out", "pre_mix_gain", "post_mix_gain",
             "pre_ffn_gain", "post_ffn_gain", "w_ffn_gate", "w_ffn_up", "w_ffn_down", "ple_gain", "w_ple_gate", "w_ple_proj"]
    res = [loss, grad_x]
    for k in range(4):
        for nm in order:
            res.append(out_big[nm][k] if nm in out_big else out_small[k][nm])
    return tuple(res)
```

```python
import functools

import jax
import jax.numpy as jnp
from jax import lax
from jax.experimental import pallas as pl
from jax.experimental.pallas import tpu as pltpu

F32, BF16 = jnp.float32, jnp.bfloat16
SDS = jax.ShapeDtypeStruct
MESH = pl.DeviceIdType.MESH

D_MODEL = 2048
ATTN_WIDTH = 1024
HGRN_WIDTH = 1024
KV_WIDTH = 256
N_Q_HEADS = 16
N_KV_HEADS = 4
Q_PER_KV = 4
WINDOW = 128
MASK_VALUE = -1e30
ROPE_THETA = 10000.0
HGRN_HEADS = 8
HGRN_CHUNK = 16
D_FF = 5632
D_PLE = 256
RMS_EPS = 1e-6
LANES = 128
N_CHIPS = 4
COL_Q, COL_K, COL_V, COL_HQ, COL_HF, COL_HI, COL_HG = 0, 8, 10, 12, 20, 28, 36

ADAM_LR, ADAM_B1, ADAM_B2, ADAM_EPS, ADAM_WD, ADAM_STEP = 0.001, 0.9, 0.999, 1e-08, 0.01, 10

VMEM_LIMIT = 56 * 1024 * 1024
ROW_TILE = 256

_NN = (((1,), (0,)), ((), ()))
_NT = (((1,), (1,)), ((), ()))
_TN = (((0,), (0,)), ((), ()))


def _pc(body, *, name, out_shape, in_specs, out_specs, grid=(), scratch=(), sem=None, grid_spec=None, **kw):
    params = dict(vmem_limit_bytes=VMEM_LIMIT)
    if sem is not None:
        params["dimension_semantics"] = sem
    if grid_spec is not None:
        return pl.pallas_call(body, name=name, out_shape=out_shape, grid_spec=grid_spec,
                              compiler_params=pltpu.CompilerParams(**params), **kw)
    return pl.pallas_call(body, name=name, out_shape=out_shape, grid=grid, in_specs=in_specs, out_specs=out_specs,
                          scratch_shapes=list(scratch), compiler_params=pltpu.CompilerParams(**params), **kw)


def _sigmoid(x):
    return 1.0 / (1.0 + jnp.exp(-x))


def _rstd(x):
    return lax.rsqrt(jnp.mean(x * x, axis=-1, keepdims=True) + RMS_EPS)


def _rows(t, w, col=0):
    return pl.BlockSpec((t, w), lambda i, col=col: (i, col))


def _fixed(shape):
    return pl.BlockSpec(shape, lambda *_: (0,) * len(shape))


def _mm(name, a, b, *, dims, grid, a_spec, b_spec, o_spec, out_shape, parts=1, add=None, add_spec=None):
    def body(*refs):
        a_ref, b_ref, o_ref = refs[0], refs[1], refs[-1]
        if parts == 1:
            r = lax.dot_general(a_ref[...].astype(BF16), b_ref[...].astype(BF16), dims, preferred_element_type=F32)
        else:
            w = a_ref.shape[1] // parts
            r = None
            for j in range(parts):
                t = lax.dot_general(a_ref[:, j * w:(j + 1) * w].astype(BF16), b_ref[j].astype(BF16), dims,
                                    preferred_element_type=F32)
                r = t if r is None else r + t
        if add is not None:
            r = r + refs[2][...]
        o_ref[...] = r.astype(o_ref.dtype)

    ins = [a, b] + ([] if add is None else [add])
    specs = [a_spec, b_spec] + ([] if add is None else [add_spec])
    return _pc(body, name=name, grid=grid, in_specs=specs, out_specs=o_spec, out_shape=out_shape,
               sem=("parallel",) * len(grid))(*ins)


def _tile(n, t):
    if n <= t:
        return n
    while n % t:
        t //= 2
    assert t % 8 == 0
    return t


def mm_col(name, a, wg, out_dtype=F32):
    s, k = a.shape
    _, _, n = wg.shape
    tm = _tile(s, 512)
    return _mm(name, a, wg, dims=_NN, grid=(N_CHIPS, s // tm),
               a_spec=pl.BlockSpec((tm, k), lambda j, i: (i, 0)),
               b_spec=pl.BlockSpec((None, k, n), lambda j, i: (j, 0, 0)),
               o_spec=pl.BlockSpec((tm, n), lambda j, i: (i, j)),
               out_shape=SDS((s, N_CHIPS * n), out_dtype))


def mm_row(name, a, wg, out_dtype=F32):
    s, _ = a.shape
    _, r, n = wg.shape
    tm = _tile(s, 512)
    tn = _tile(n, 1024 if r > 512 else 2048)
    return _mm(name, a, wg, dims=_NN, grid=(n // tn, s // tm), parts=N_CHIPS,
               a_spec=pl.BlockSpec((tm, N_CHIPS * r), lambda j, i: (i, 0)),
               b_spec=pl.BlockSpec((N_CHIPS, r, tn), lambda j, i: (0, 0, j)),
               o_spec=pl.BlockSpec((tm, tn), lambda j, i: (i, j)),
               out_shape=SDS((s, n), out_dtype))


def mm_col_t(name, dy, wg, add=None, out_dtype=F32):
    s, _ = dy.shape
    _, k, n = wg.shape
    tm = _tile(s, 512)
    tk = _tile(k, 1024)
    return _mm(name, dy, wg, dims=_NT, grid=(k // tk, s // tm), parts=N_CHIPS,
               a_spec=pl.BlockSpec((tm, N_CHIPS * n), lambda j, i: (i, 0)),
               b_spec=pl.BlockSpec((N_CHIPS, tk, n), lambda j, i: (0, j, 0)),
               o_spec=pl.BlockSpec((tm, tk), lambda j, i: (i, j)),
               add=add, add_spec=pl.BlockSpec((tm, tk), lambda j, i: (i, j)),
               out_shape=SDS((s, k), out_dtype))


def mm_row_t(name, dy, wg, out_dtype=F32):
    s, n = dy.shape
    _, r, _ = wg.shape
    tm = _tile(s, 512)
    return _mm(name, dy, wg, dims=_NT, grid=(N_CHIPS, s // tm),
               a_spec=pl.BlockSpec((tm, n), lambda j, i: (i, 0)),
               b_spec=pl.BlockSpec((None, r, n), lambda j, i: (j, 0, 0)),
               o_spec=pl.BlockSpec((tm, r), lambda j, i: (i, j)),
               out_shape=SDS((s, N_CHIPS * r), out_dtype))


def ffn_gate_up(name, h, wg_gate, wg_up):
    s, k = h.shape
    _, _, n = wg_gate.shape
    tm = _tile(s, 512)

    def body(h_ref, wg_ref, wu_ref, g_ref, u_ref, a_ref):
        hv = h_ref[...]
        g = jnp.dot(hv, wg_ref[...], preferred_element_type=F32)
        u = jnp.dot(hv, wu_ref[...], preferred_element_type=F32)
        g_ref[...] = g
        u_ref[...] = u
        a_ref[...] = ((g * _sigmoid(g)) * u).astype(BF16)

    wspec = pl.BlockSpec((None, k, n), lambda j, i: (j, 0, 0))
    ospec = pl.BlockSpec((tm, n), lambda j, i: (i, j))
    return _pc(body, name=name, grid=(N_CHIPS, s // tm), in_specs=[pl.BlockSpec((tm, k), lambda j, i: (i, 0)), wspec, wspec],
               out_specs=[ospec] * 3, out_shape=[SDS((s, N_CHIPS * n), F32)] * 2 + [SDS((s, N_CHIPS * n), BF16)],
               sem=("parallel", "parallel"))(h, wg_gate, wg_up)


def ffn_down_bwd(name, df, wg_down, g, u):
    s, n = df.shape
    _, r, _ = wg_down.shape
    tm = _tile(s, 512)

    def body(df_ref, w_ref, g_ref, u_ref, dg_ref, du_ref):
        da = lax.dot_general(df_ref[...], w_ref[...], _NT, preferred_element_type=F32)
        gv = g_ref[...]
        sg = _sigmoid(gv)
        du_ref[...] = (da * (gv * sg)).astype(BF16)
        dg_ref[...] = ((da * u_ref[...]) * (sg * (1.0 + gv * (1.0 - sg)))).astype(BF16)

    cspec = pl.BlockSpec((tm, r), lambda j, i: (i, j))
    return _pc(body, name=name, grid=(N_CHIPS, s // tm),
               in_specs=[pl.BlockSpec((tm, n), lambda j, i: (i, 0)), pl.BlockSpec((None, r, n), lambda j, i: (j, 0, 0)), cspec, cspec],
               out_specs=[cspec] * 2, out_shape=[SDS((s, N_CHIPS * r), BF16)] * 2,
               sem=("parallel", "parallel"))(df, wg_down, g, u)


def mm_wg_col(name, a, dy):
    s, k = a.shape
    n = dy.shape[1] // N_CHIPS
    tm = _tile(k // 2, 512)
    hb = (k // 2) // tm
    return _mm(name, a, dy, dims=_TN, grid=(N_CHIPS, k // tm),
               a_spec=pl.BlockSpec((s, tm), lambda j, i: (0, i)),
               b_spec=pl.BlockSpec((s, n), lambda j, i: (0, j)),
               o_spec=pl.BlockSpec((None, None, tm, n), lambda j, i: (i // hb, j, i % hb, 0)),
               out_shape=SDS((2, N_CHIPS, k // 2, n), BF16))


def mm_wg_row(name, a, dy):
    s, n = dy.shape
    r = a.shape[1] // N_CHIPS
    tn = _tile(n // 2, 512)
    nb = (n // 2) // tn
    return _mm(name, a, dy, dims=_TN, grid=(N_CHIPS, n // tn),
               a_spec=pl.BlockSpec((s, r), lambda j, i: (0, j)),
               b_spec=pl.BlockSpec((s, tn), lambda j, i: (0, i)),
               o_spec=pl.BlockSpec((None, None, r, tn), lambda j, i: (i // nb, j, 0, i % nb)),
               out_shape=SDS((2, N_CHIPS, r, n // 2), BF16))


def pre_norm(name, x, gain):
    s, d = x.shape
    t = _tile(s, ROW_TILE)

    def body(x_ref, g_ref, o_ref):
        xv = x_ref[...]
        o_ref[...] = ((xv * _rstd(xv)) * g_ref[...]).astype(BF16)

    return _pc(body, name=name, grid=(s // t,), in_specs=[_rows(t, d), _fixed((1, d))], out_specs=_rows(t, d),
               out_shape=SDS((s, d), BF16), sem=("parallel",))(x, gain)


def post_pre_norm(name, m, g_post, res, g_pre):
    s, d = m.shape
    t = _tile(s, ROW_TILE)

    def body(m_ref, gp_ref, r_ref, gn_ref, x_ref, h_ref):
        mv = m_ref[...]
        xn = r_ref[...] + (mv * _rstd(mv)) * gp_ref[...]
        x_ref[...] = xn
        h_ref[...] = ((xn * _rstd(xn)) * gn_ref[...]).astype(BF16)

    return _pc(body, name=name, grid=(s // t,),
               in_specs=[_rows(t, d), _fixed((1, d)), _rows(t, d), _fixed((1, d))],
               out_specs=[_rows(t, d), _rows(t, d)], out_shape=[SDS((s, d), F32), SDS((s, d), BF16)],
               sem=("parallel",))(m, g_post, res, g_pre)


def ple_fwd_mid(name, z, pp, x2, g_next):
    s, d = z.shape
    t = _tile(s, ROW_TILE)

    def body(z_ref, p_ref, x_ref, g_ref, xo_ref, h_ref):
        xn = x_ref[...] + p_ref[...] * _sigmoid(z_ref[...])
        xo_ref[...] = xn
        h_ref[...] = ((xn * _rstd(xn)) * g_ref[...]).astype(BF16)

    return _pc(body, name=name, grid=(s // t,),
               in_specs=[_rows(t, d), _rows(t, d), _rows(t, d), _fixed((1, d))],
               out_specs=[_rows(t, d), _rows(t, d)], out_shape=[SDS((s, d), F32), SDS((s, d), BF16)],
               sem=("parallel",))(z, pp, x2, g_next)


def ple_fwd_loss(name, z, pp, x2, target):
    s, d = z.shape
    t = _tile(s, ROW_TILE)

    def body(z_ref, p_ref, x_ref, t_ref, dy_ref, l_ref):
        err = (x_ref[...] + p_ref[...] * _sigmoid(z_ref[...])) - t_ref[...]
        dy_ref[...] = err * (1.0 / d)
        part = jnp.sum(jnp.sum(err * err, axis=-1, keepdims=True), axis=0, keepdims=True) * (0.5 / d)

        @pl.when(pl.program_id(0) == 0)
        def _():
            l_ref[...] = part

        @pl.when(pl.program_id(0) > 0)
        def _():
            l_ref[...] += part

    return _pc(body, name=name, grid=(s // t,),
               in_specs=[_rows(t, d), _rows(t, d), _rows(t, d), _rows(t, d)],
               out_specs=[_rows(t, d), _fixed((1, 1))], out_shape=[SDS((s, d), F32), SDS((1, 1), F32)],
               sem=("arbitrary",))(z, pp, x2, target)


def ple_bwd(name, dx3, z, pp):
    s, d = z.shape
    t = _tile(s, ROW_TILE)

    def body(d_ref, z_ref, p_ref, dpp_ref, dz_ref):
        gate = _sigmoid(z_ref[...])
        dv = d_ref[...]
        dpp_ref[...] = (dv * gate).astype(BF16)
        dz_ref[...] = ((dv * p_ref[...]) * (gate * (1.0 - gate))).astype(BF16)

    return _pc(body, name=name, grid=(s // t,), in_specs=[_rows(t, d)] * 3, out_specs=[_rows(t, d)] * 2,
               out_shape=[SDS((s, d), BF16)] * 2, sem=("parallel",))(dx3, z, pp)


def norm_bwd(name, xin, gain, dy, dres=None, out_dtype=F32):
    s, d = xin.shape
    t = _tile(s, ROW_TILE)

    def body(*refs):
        if dres is None:
            x_ref, g_ref, dy_ref, dx_ref, dg_ref = refs
            r_ref = None
        else:
            x_ref, g_ref, dy_ref, r_ref, dx_ref, dg_ref = refs
        xv = x_ref[...]
        r = _rstd(xv)
        xh = xv * r
        dyv = dy_ref[...].astype(F32)
        dyg = dyv * g_ref[...]
        c = jnp.mean(dyg * xh, axis=-1, keepdims=True)
        dx = r * (dyg - xh * c)
        if r_ref is not None:
            dx = dx + r_ref[...]
        dx_ref[...] = dx.astype(out_dtype)
        part = jnp.sum(dyv * xh, axis=0, keepdims=True)

        @pl.when(pl.program_id(0) == 0)
        def _():
            dg_ref[...] = part

        @pl.when(pl.program_id(0) > 0)
        def _():
            dg_ref[...] += part

    ins = [xin, gain, dy] + ([] if dres is None else [dres])
    specs = [_rows(t, d), _fixed((1, d)), _rows(t, d)] + ([] if dres is None else [_rows(t, d)])
    return _pc(body, name=name, grid=(s // t,), in_specs=specs, out_specs=[_rows(t, d), _fixed((1, d))],
               out_shape=[SDS((s, d), out_dtype), SDS((1, d), F32)], sem=("arbitrary",))(*ins)


def _lane(shape):
    return lax.broadcasted_iota(jnp.int32, shape, 1)


def _swap_halves(x):
    lo = (_lane(x.shape) % 64) < 32
    return jnp.where(lo, pltpu.roll(x, 96, 1), pltpu.roll(x, 32, 1))


def rope_tables(name, pos_col, inv_freq):
    s = pos_col.shape[0]
    t = _tile(s, ROW_TILE)

    def body(p_ref, f_ref, c_ref, s_ref):
        ang = p_ref[...].astype(F32) * f_ref[...]
        lo = (_lane(ang.shape) % 64) < 32
        c_ref[...] = jnp.cos(ang)
        sn = jnp.sin(ang)
        s_ref[...] = jnp.where(lo, -sn, sn)

    return _pc(body, name=name, grid=(s // t,), in_specs=[_rows(t, 1), _fixed((1, LANES))],
               out_specs=[_rows(t, LANES)] * 2, out_shape=[SDS((s, LANES), F32)] * 2, sem=("parallel",))(pos_col, inv_freq)


def _pad_heads(chunk, lo_mask):
    zero = jnp.zeros_like(chunk)
    return jnp.where(lo_mask, chunk, zero), jnp.where(lo_mask, pltpu.roll(chunk, 64, 1), zero)


def rope_qkv(name, proj, cos, sin):
    s = proj.shape[0]
    t = _tile(s, ROW_TILE)

    def body(q_ref, kv_ref, c_ref, s_ref, qp_ref, kp_ref, vp_ref):
        cs, sn = c_ref[...], s_ref[...]
        lo_mask = _lane(cs.shape) < 64

        def rot(x):
            return x * cs + _swap_halves(x) * sn

        for j in range(ATTN_WIDTH // LANES):
            a, b = _pad_heads(rot(q_ref[:, j * LANES:(j + 1) * LANES]), lo_mask)
            qp_ref[:, (2 * j) * LANES:(2 * j + 1) * LANES] = a.astype(BF16)
            qp_ref[:, (2 * j + 1) * LANES:(2 * j + 2) * LANES] = b.astype(BF16)
        for j in range(KV_WIDTH // LANES):
            a, b = _pad_heads(rot(kv_ref[:, j * LANES:(j + 1) * LANES]), lo_mask)
            kp_ref[:, (2 * j) * LANES:(2 * j + 1) * LANES] = a.astype(BF16)
            kp_ref[:, (2 * j + 1) * LANES:(2 * j + 2) * LANES] = b.astype(BF16)
            a, b = _pad_heads(kv_ref[:, KV_WIDTH + j * LANES:KV_WIDTH + (j + 1) * LANES], lo_mask)
            vp_ref[:, (2 * j) * LANES:(2 * j + 1) * LANES] = a.astype(BF16)
            vp_ref[:, (2 * j + 1) * LANES:(2 * j + 2) * LANES] = b.astype(BF16)

    return _pc(body, name=name, grid=(s // t,),
               in_specs=[_rows(t, ATTN_WIDTH, 0), _rows(t, 2 * KV_WIDTH, 2), _rows(t, LANES), _rows(t, LANES)],
               out_specs=[_rows(t, N_Q_HEADS * LANES), _rows(t, N_KV_HEADS * LANES), _rows(t, N_KV_HEADS * LANES)],
               out_shape=[SDS((s, N_Q_HEADS * LANES), BF16), SDS((s, N_KV_HEADS * LANES), BF16),
                          SDS((s, N_KV_HEADS * LANES), BF16)],
               sem=("parallel",))(proj, proj, cos, sin)


def _attn_mask(n):
    L = WINDOW
    qi = lax.broadcasted_iota(jnp.int32, (L, 2 * L), 0) + L
    ki = lax.broadcasted_iota(jnp.int32, (L, 2 * L), 1)
    rel = qi - ki
    return (rel >= 0) & (rel < WINDOW) & ((n > 0) | (ki >= L))


def _attn_probs(qh, kk, valid, sink):
    sc = lax.dot_general(qh, kk, _NT, preferred_element_type=F32) * 0.125
    sc = jnp.where(valid, sc, MASK_VALUE)
    m = jnp.maximum(jnp.max(sc, axis=-1, keepdims=True), sink)
    e = jnp.exp(sc - m)
    es = jnp.exp(sink - m)
    den = jnp.sum(e, axis=-1, keepdims=True) + es
    return e / den, es / den


def _attn_specs(s):
    L = WINDOW
    cur = lambda n: (n, 0)
    prev = lambda n: (jnp.maximum(n - 1, 0), 0)
    kvw = N_KV_HEADS * LANES
    return [pl.BlockSpec((L, N_Q_HEADS * LANES), cur), pl.BlockSpec((L, kvw), cur), pl.BlockSpec((L, kvw), prev),
            pl.BlockSpec((L, kvw), cur), pl.BlockSpec((L, kvw), prev), pl.BlockSpec(memory_space=pltpu.SMEM)]


def attn_fwd(name, qp, kp, vp, sinks):
    s = qp.shape[0]
    L = WINDOW

    def body(q_ref, kc_ref, kp_ref, vc_ref, vp_ref, sk_ref, o_ref):
        valid = _attn_mask(pl.program_id(0))
        outs = []
        for kvh in range(N_KV_HEADS):
            cols = slice(kvh * LANES, (kvh + 1) * LANES)
            kk = jnp.concatenate([kp_ref[:, cols], kc_ref[:, cols]], axis=0)
            vv = jnp.concatenate([vp_ref[:, cols], vc_ref[:, cols]], axis=0)
            for g in range(Q_PER_KV):
                h = kvh * Q_PER_KV + g
                p, _ = _attn_probs(q_ref[:, h * LANES:(h + 1) * LANES], kk, valid, sk_ref[h])
                outs.append(jnp.dot(p.astype(BF16), vv, preferred_element_type=F32))
        for j in range(ATTN_WIDTH // LANES):
            o_ref[:, j * LANES:(j + 1) * LANES] = outs[2 * j] + pltpu.roll(outs[2 * j + 1], 64, 1)

    return _pc(body, name=name, grid=(s // L,), in_specs=_attn_specs(s),
               out_specs=pl.BlockSpec((L, ATTN_WIDTH), lambda n: (n, 0)),
               out_shape=SDS((s, ATTN_WIDTH), F32), sem=("parallel",))(qp, kp, kp, vp, vp, sinks)


def attn_bwd(name, qp, kp, vp, sinks, dattn):
    s = qp.shape[0]
    L = WINDOW
    kvw = N_KV_HEADS * LANES

    def body(q_ref, kc_ref, kp_ref, vc_ref, vp_ref, sk_ref, do_ref, dq_ref, dkc_ref, dkp_ref, dvc_ref, dvp_ref, ds_ref):
        n = pl.program_id(0)
        valid = _attn_mask(n)
        lo_mask = _lane((L, LANES)) < 64
        lane1 = _lane((1, LANES))
        dsink = jnp.zeros((1, LANES), F32)
        for kvh in range(N_KV_HEADS):
            cols = slice(kvh * LANES, (kvh + 1) * LANES)
            kk = jnp.concatenate([kp_ref[:, cols], kc_ref[:, cols]], axis=0)
            vv = jnp.concatenate([vp_ref[:, cols], vc_ref[:, cols]], axis=0)
            dkk = jnp.zeros((2 * L, LANES), F32)
            dvv = jnp.zeros((2 * L, LANES), F32)
            for g in range(Q_PER_KV):
                h = kvh * Q_PER_KV + g
                qh = q_ref[:, h * LANES:(h + 1) * LANES]
                p, ps = _attn_probs(qh, kk, valid, sk_ref[h])
                chunk = do_ref[:, (h // 2) * LANES:(h // 2 + 1) * LANES]
                if h % 2:
                    chunk = pltpu.roll(chunk, 64, 1)
                doh = jnp.where(lo_mask, chunk, 0.0).astype(BF16)
                dp = lax.dot_general(doh, vv, _NT, preferred_element_type=F32)
                delta = jnp.sum(p * dp, axis=-1, keepdims=True)
                dsb = ((p * (dp - delta)) * 0.125).astype(BF16)
                dsink = dsink + jnp.where(lane1 == h, -jnp.sum(ps * delta, axis=0, keepdims=True), 0.0)
                dq_ref[:, h * LANES:(h + 1) * LANES] = jnp.dot(dsb, kk, preferred_element_type=F32)
                dkk = dkk + lax.dot_general(dsb, qh, _TN, preferred_element_type=F32)
                dvv = dvv + lax.dot_general(p.astype(BF16), doh, _TN, preferred_element_type=F32)
            dkp_ref[:, cols] = dkk[:L]
            dkc_ref[:, cols] = dkk[L:]
            dvp_ref[:, cols] = dvv[:L]
            dvc_ref[:, cols] = dvv[L:]

        @pl.when(n == 0)
        def _():
            ds_ref[...] = dsink

        @pl.when(n > 0)
        def _():
            ds_ref[...] += dsink

    blk = lambda w: pl.BlockSpec((L, w), lambda n: (n, 0))
    return _pc(body, name=name, grid=(s // L,), in_specs=_attn_specs(s) + [blk(ATTN_WIDTH)],
               out_specs=[blk(N_Q_HEADS * LANES), blk(kvw), blk(kvw), blk(kvw), blk(kvw), _fixed((1, LANES))],
               out_shape=[SDS((s, N_Q_HEADS * LANES), F32)] + [SDS((s, kvw), F32)] * 4 + [SDS((1, LANES), F32)],
               sem=("arbitrary",))(qp, kp, kp, vp, vp, sinks, dattn)


def rope_bwd(name, dqp, dkc, dkp, dvc, dvp, cos, sin):
    s = dqp.shape[0]
    L = WINDOW
    nb = s // L
    kvw = N_KV_HEADS * LANES

    def body(dq_ref, dkc_ref, dkp_ref, dvc_ref, dvp_ref, c_ref, s_ref, o_ref):
        cs, sn = c_ref[...], s_ref[...]
        more = (pl.program_id(0) < nb - 1).astype(F32)

        def unrot(x):
            return x * cs - _swap_halves(x) * sn

        def compact(ref, j, nxt=None):
            a = ref[:, (2 * j) * LANES:(2 * j + 1) * LANES]
            b = ref[:, (2 * j + 1) * LANES:(2 * j + 2) * LANES]
            if nxt is not None:
                a = a + more * nxt[:, (2 * j) * LANES:(2 * j + 1) * LANES]
                b = b + more * nxt[:, (2 * j + 1) * LANES:(2 * j + 2) * LANES]
            return a + pltpu.roll(b, 64, 1)

        for j in range(ATTN_WIDTH // LANES):
            o_ref[:, j * LANES:(j + 1) * LANES] = unrot(compact(dq_ref, j)).astype(BF16)
        for j in range(KV_WIDTH // LANES):
            o_ref[:, (COL_K + j) * LANES:(COL_K + j + 1) * LANES] = unrot(compact(dkc_ref, j, dkp_ref)).astype(BF16)
            o_ref[:, (COL_V + j) * LANES:(COL_V + j + 1) * LANES] = compact(dvc_ref, j, dvp_ref).astype(BF16)

    cur = lambda n: (n, 0)
    nxt = lambda n: (jnp.minimum(n + 1, nb - 1), 0)
    return _pc(body, name=name, grid=(nb,),
               in_specs=[pl.BlockSpec((L, N_Q_HEADS * LANES), cur), pl.BlockSpec((L, kvw), cur), pl.BlockSpec((L, kvw), nxt),
                         pl.BlockSpec((L, kvw), cur), pl.BlockSpec((L, kvw), nxt), pl.BlockSpec((L, LANES), cur),
                         pl.BlockSpec((L, LANES), cur)],
               out_specs=pl.BlockSpec((L, COL_HQ * LANES), cur), out_shape=SDS((s, COL_HQ * LANES), BF16),
               sem=("parallel",))(dqp, dkc, dkp, dvc, dvp, cos, sin)


def _split3(x):
    a = x.astype(BF16)
    r = x - a.astype(F32)
    b = r.astype(BF16)
    c = (r - b.astype(F32)).astype(BF16)
    return a, b, c


def _chunk_sum(x, upper):
    t = x.shape[0]
    ri = lax.broadcasted_iota(jnp.int32, (t, t), 0)
    ci = lax.broadcasted_iota(jnp.int32, (t, t), 1)
    same = (ri // HGRN_CHUNK) == (ci // HGRN_CHUNK)
    tri = (ci >= ri) if upper else (ci <= ri)
    m = jnp.where(same & tri, 1.0, 0.0).astype(BF16)
    out = None
    for part in _split3(x):
        y = jnp.dot(m, part, preferred_element_type=F32)
        out = y if out is None else out + y
    return out


def _lower_bound(l_ref, layer):
    lv = l_ref[...]
    e = jnp.exp(lv - jnp.max(lv, axis=0, keepdims=True))
    sm = e / jnp.sum(e, axis=0, keepdims=True)
    s0 = sm[0:1]
    return (s0 - s0) if layer == 0 else ((s0 + sm[1:2]) - s0)


def _hgrn_gates(hq_ref, hf_ref, lb):
    z = hf_ref[...]
    sg = _sigmoid(z)
    f = lb + (1.0 - lb) * sg
    kin = (1.0 - lb) * _sigmoid(-z)
    hq = hq_ref[...]
    sq = _sigmoid(hq)
    return sg, f, kin, hq, sq


def _shift_down(x, d):
    return x if d == 0 else pltpu.roll(x, d, 0)


def _shift_up(x, d):
    return x if d == 0 else pltpu.roll(x, x.shape[0] - d, 0)


def _intra_iotas(t):
    rows = lax.broadcasted_iota(jnp.int32, (t, LANES), 0)
    return rows % LANES, lax.broadcasted_iota(jnp.int32, (t, LANES), 1), rows % HGRN_CHUNK


def _hgrn_specs(t, rev, nt):
    row = (lambda h, i: nt - 1 - i) if rev else (lambda h, i: i)
    col = lambda base: pl.BlockSpec((t, LANES), lambda h, i, base=base: (row(h, i), base + h))
    return col, row


def hgrn_fwd(name, proj, lb_logits, layer):
    s = proj.shape[0]
    t = _tile(s, ROW_TILE)
    nt = s // t
    nc = t // HGRN_CHUNK
    col, row = _hgrn_specs(t, False, nt)

    def body(hq_ref, hf_ref, hi_ref, l_ref, o_ref, st_ref, state):
        @pl.when(pl.program_id(1) == 0)
        def _():
            state[...] = jnp.zeros_like(state)

        lb = _lower_bound(l_ref, layer)
        sg, f, kin, hq, sq = _hgrn_gates(hq_ref, hf_ref, lb)
        q = hq * sq
        vb = hi_ref[...].astype(BF16)
        b = _chunk_sum(jnp.log(f), False)
        row, lane, tmod = _intra_iotas(t)
        amat = jnp.where(lane == row, jnp.sum(q * kin, axis=-1, keepdims=True), 0.0)
        for d in range(1, HGRN_CHUNK):
            e = jnp.exp(jnp.where(tmod >= d, b - _shift_down(b, d), MASK_VALUE))
            a = jnp.sum((q * _shift_down(kin, d)) * e, axis=-1, keepdims=True)
            amat = jnp.where(lane == row - d, a, amat)
        ab = amat.astype(BF16)
        o_intra = jnp.concatenate([jnp.dot(ab[j * LANES:(j + 1) * LANES], vb[j * LANES:(j + 1) * LANES], preferred_element_type=F32)
                                   for j in range(t // LANES)], axis=0)
        qe = (q * jnp.exp(b)).astype(BF16)
        st = state[...]
        for c in range(nc):
            rs = slice(c * HGRN_CHUNK, (c + 1) * HGRN_CHUNK)
            bc = b[rs]
            bl = bc[HGRN_CHUNK - 1:HGRN_CHUNK, :]
            ke = (kin[rs] * jnp.exp(bl - bc)).astype(BF16)
            st_ref[c] = st
            o_ref[rs, :] = o_intra[rs] + lax.dot_general(qe[rs], st.astype(BF16), _NT, preferred_element_type=F32)
            st = st * jnp.exp(bl) + lax.dot_general(vb[rs], ke, _TN, preferred_element_type=F32)
        state[...] = st

    return _pc(body, name=name, grid=(HGRN_HEADS, nt),
               in_specs=[col(COL_HQ), col(COL_HF), col(COL_HI), pl.BlockSpec((2, LANES), lambda h, i: (0, h))],
               out_specs=[pl.BlockSpec((t, LANES), lambda h, i: (i, h)),
                          pl.BlockSpec((nc, None, LANES, LANES), lambda h, i: (i, h, 0, 0))],
               out_shape=[SDS((s, HGRN_WIDTH), F32), SDS((s // HGRN_CHUNK, HGRN_HEADS, LANES, LANES), F32)],
               scratch=[pltpu.VMEM((LANES, LANES), F32)],
               sem=("parallel", "arbitrary"))(proj, proj, proj, lb_logits)


def hgrn_bwd(name, proj, lb_logits, layer, states, do):
    s = proj.shape[0]
    t = _tile(s, ROW_TILE)
    nt = s // t
    nc = t // HGRN_CHUNK
    col, row = _hgrn_specs(t, True, nt)

    def body(hq_ref, hf_ref, hi_ref, l_ref, st_ref, do_ref, dhq_ref, dhf_ref, dhi_ref, dlb_ref, dstate):
        first = pl.program_id(1) == 0

        @pl.when(first)
        def _():
            dstate[...] = jnp.zeros_like(dstate)

        lb = _lower_bound(l_ref, layer)
        sg, f, kin, hq, sq = _hgrn_gates(hq_ref, hf_ref, lb)
        q = hq * sq
        vb = hi_ref[...].astype(BF16)
        b = _chunk_sum(jnp.log(f), False)
        dob = do_ref[...].astype(BF16)
        eb = jnp.exp(b)
        qe = q * eb
        qeb = qe.astype(BF16)
        last_row = lax.broadcasted_iota(jnp.int32, (HGRN_CHUNK, LANES), 0) == HGRN_CHUNK - 1

        dn = dstate[...]
        dq_c, dk_c, dv_c, db_c = [None] * nc, [None] * nc, [None] * nc, [None] * nc
        for c in reversed(range(nc)):
            rs = slice(c * HGRN_CHUNK, (c + 1) * HGRN_CHUNK)
            bc = b[rs]
            bl = bc[HGRN_CHUNK - 1:HGRN_CHUNK, :]
            ebl = jnp.exp(bl)
            ekb = jnp.exp(bl - bc)
            ke = kin[rs] * ekb
            st = st_ref[c]
            dnb = dn.astype(BF16)
            dqe = jnp.dot(dob[rs], st.astype(BF16), preferred_element_type=F32)
            dke = jnp.dot(vb[rs], dnb, preferred_element_type=F32)
            dv_c[c] = lax.dot_general(ke.astype(BF16), dnb, _NT, preferred_element_type=F32)
            dbl = jnp.sum(dn * st, axis=0, keepdims=True) * ebl + jnp.sum(dke * ke, axis=0, keepdims=True)
            dq_c[c] = dqe * eb[rs]
            dk_c[c] = dke * ekb
            db_c[c] = (dqe * qe[rs] - dke * ke) + jnp.where(last_row, dbl, 0.0)
            dn = dn * ebl + lax.dot_general(dob[rs], qeb[rs], _TN, preferred_element_type=F32)
        dstate[...] = dn

        row, lane, tmod = _intra_iotas(t)
        blocks = [slice(j * LANES, (j + 1) * LANES) for j in range(t // LANES)]
        damat = jnp.concatenate([lax.dot_general(dob[bs], vb[bs], _NT, preferred_element_type=F32) for bs in blocks], axis=0)
        on = lane == row
        da = jnp.sum(jnp.where(on, damat, 0.0), axis=-1, keepdims=True)
        amat = jnp.where(on, jnp.sum(q * kin, axis=-1, keepdims=True), 0.0)
        dq = jnp.concatenate(dq_c, axis=0) + da * kin
        dk = jnp.concatenate(dk_c, axis=0) + da * q
        db = jnp.concatenate(db_c, axis=0)
        for d in range(1, HGRN_CHUNK):
            on = lane == row - d
            ks = _shift_down(kin, d)
            e = jnp.exp(jnp.where(tmod >= d, b - _shift_down(b, d), MASK_VALUE))
            qd = q * e
            w = qd * ks
            amat = jnp.where(on, jnp.sum(w, axis=-1, keepdims=True), amat)
            da = jnp.sum(jnp.where(on, damat, 0.0), axis=-1, keepdims=True)
            dq = dq + da * (ks * e)
            daw = da * w
            db = db + daw - _shift_up(daw, d)
            dk = dk + _shift_up(da * qd, d)
        ab = amat.astype(BF16)
        dv = jnp.concatenate(dv_c, axis=0) + jnp.concatenate(
            [lax.dot_general(ab[bs], dob[bs], _TN, preferred_element_type=F32) for bs in blocks], axis=0)
        dg = _chunk_sum(db, True)
        dhq_ref[...] = (dq * (sq * (1.0 + hq * (1.0 - sq)))).astype(BF16)
        dhi_ref[...] = dv.astype(BF16)
        dfk = dg / f - dk
        dhf_ref[...] = ((dfk * (1.0 - lb)) * (sg * (1.0 - sg))).astype(BF16)
        part = jnp.sum(dfk * (1.0 - sg), axis=0, keepdims=True)

        @pl.when(first)
        def _():
            dlb_ref[...] = part

        @pl.when(jnp.logical_not(first))
        def _():
            dlb_ref[...] += part

    out_col = pl.BlockSpec((t, LANES), lambda h, i: (nt - 1 - i, h))
    return _pc(body, name=name, grid=(HGRN_HEADS, nt),
               in_specs=[col(COL_HQ), col(COL_HF), col(COL_HI), pl.BlockSpec((2, LANES), lambda h, i: (0, h)),
                         pl.BlockSpec((nc, None, LANES, LANES), lambda h, i: (nt - 1 - i, h, 0, 0)), out_col],
               out_specs=[out_col, out_col, out_col, pl.BlockSpec((1, LANES), lambda h, i: (0, h))],
               out_shape=[SDS((s, HGRN_WIDTH), BF16)] * 3 + [SDS((1, HGRN_WIDTH), F32)],
               scratch=[pltpu.VMEM((LANES, LANES), F32)],
               sem=("parallel", "arbitrary"))(proj, proj, proj, lb_logits, states, do)


def mix_out_fwd(name, attn, o, proj, g_attn, g_hgrn):
    s = attn.shape[0]
    t = _tile(s, ROW_TILE)
    half = HGRN_WIDTH // 2

    def body(a_ref, o_ref, hg0_ref, hg1_ref, ga_ref, gh_ref, c_ref):
        av = a_ref[...]
        c_ref[:, :ATTN_WIDTH] = ((av * _rstd(av)) * ga_ref[...]).astype(BF16)
        for j in range(HGRN_HEADS):
            cols = slice(j * LANES, (j + 1) * LANES)
            ov = o_ref[:, cols]
            hg_ref, hcols = (hg0_ref, cols) if j < 4 else (hg1_ref, slice((j - 4) * LANES, (j - 3) * LANES))
            hg = hg_ref[:, hcols]
            on = (ov * _rstd(ov)) * gh_ref[:, cols]
            c_ref[:, ATTN_WIDTH + j * LANES:ATTN_WIDTH + (j + 1) * LANES] = (on * (hg * _sigmoid(hg))).astype(BF16)

    return _pc(body, name=name, grid=(s // t,),
               in_specs=[_rows(t, ATTN_WIDTH), _rows(t, HGRN_WIDTH), _rows(t, half, COL_HG // 4), _rows(t, half, COL_HG // 4 + 1),
                         _fixed((1, ATTN_WIDTH)), _fixed((1, HGRN_WIDTH))],
               out_specs=_rows(t, D_MODEL), out_shape=SDS((s, D_MODEL), BF16), sem=("parallel",))(attn, o, proj, proj, g_attn, g_hgrn)


def mix_out_bwd(name, dcat, attn, o, proj, g_attn, g_hgrn):
    s = attn.shape[0]
    t = _tile(s, ROW_TILE)
    half = HGRN_WIDTH // 2

    def body(dc_ref, a_ref, o_ref, hg0_ref, hg1_ref, ga_ref, gh_ref, da_ref, do_ref, dhg_ref, dga_ref, dgh_ref, pa_s, ph_s):
        av = a_ref[...]
        r = _rstd(av)
        xh = av * r
        dyv = dc_ref[:, :ATTN_WIDTH]
        dyg = dyv * ga_ref[...]
        da_ref[...] = r * (dyg - xh * jnp.mean(dyg * xh, axis=-1, keepdims=True))
        pa_s[...] = jnp.sum(dyv * xh, axis=0, keepdims=True)
        for j in range(HGRN_HEADS):
            cols = slice(j * LANES, (j + 1) * LANES)
            ov = o_ref[:, cols]
            hg_ref, hcols = (hg0_ref, cols) if j < 4 else (hg1_ref, slice((j - 4) * LANES, (j - 3) * LANES))
            hg = hg_ref[:, hcols]
            sg = _sigmoid(hg)
            r = _rstd(ov)
            xh = ov * r
            gain = gh_ref[:, cols]
            dh = dc_ref[:, ATTN_WIDTH + j * LANES:ATTN_WIDTH + (j + 1) * LANES]
            dhg_ref[:, cols] = ((dh * (xh * gain)) * (sg * (1.0 + hg * (1.0 - sg)))).astype(BF16)
            dyv = dh * (hg * sg)
            dyg = dyv * gain
            do_ref[:, cols] = r * (dyg - xh * jnp.mean(dyg * xh, axis=-1, keepdims=True))
            ph_s[:, cols] = jnp.sum(dyv * xh, axis=0, keepdims=True)

        @pl.when(pl.program_id(0) == 0)
        def _():
            dga_ref[...] = pa_s[...]
            dgh_ref[...] = ph_s[...]

        @pl.when(pl.program_id(0) > 0)
        def _():
            dga_ref[...] += pa_s[...]
            dgh_ref[...] += ph_s[...]

    return _pc(body, name=name, grid=(s // t,),
               in_specs=[_rows(t, D_MODEL), _rows(t, ATTN_WIDTH), _rows(t, HGRN_WIDTH), _rows(t, half, COL_HG // 4),
                         _rows(t, half, COL_HG // 4 + 1), _fixed((1, ATTN_WIDTH)), _fixed((1, HGRN_WIDTH))],
               out_specs=[_rows(t, ATTN_WIDTH), _rows(t, HGRN_WIDTH), _rows(t, HGRN_WIDTH), _fixed((1, ATTN_WIDTH)),
                          _fixed((1, HGRN_WIDTH))],
               out_shape=[SDS((s, ATTN_WIDTH), F32), SDS((s, HGRN_WIDTH), F32), SDS((s, HGRN_WIDTH), BF16),
                          SDS((1, ATTN_WIDTH), F32), SDS((1, HGRN_WIDTH), F32)],
               scratch=[pltpu.VMEM((1, ATTN_WIDTH), F32), pltpu.VMEM((1, HGRN_WIDTH), F32)],
               sem=("arbitrary",))(dcat, attn, o, proj, proj, g_attn, g_hgrn)


BIG = (("w_in", 2048, 1408, 0), ("w_out", 512, 2048, 1), ("w_ffn_gate", 2048, 1408, 0), ("w_ffn_up", 2048, 1408, 0),
       ("w_ffn_down", 1408, 2048, 1), ("w_ple_gate", 512, 2048, 1), ("w_ple_proj", 256, 512, 0))
BIG_BY_NAME = {spec[0]: spec for spec in BIG}
HBM_SPEC = pl.BlockSpec(memory_space=pltpu.HBM)
SEM_SPEC = pl.BlockSpec(memory_space=pltpu.SEMAPHORE)
TOKEN_SHAPE = (8, LANES)


def _split_call(body, *, name, in_specs, out_specs, out_shape, aliases):
    return pl.pallas_call(body, name=name, in_specs=in_specs, out_specs=out_specs, out_shape=out_shape,
                          input_output_aliases=aliases,
                          compiler_params=pltpu.CompilerParams(has_side_effects=pltpu.SideEffectType.DATAFLOW_SIDE_EFFECTING))


def _in_hbm(arrays):
    return [pltpu.with_memory_space_constraint(a, pltpu.HBM) for a in arrays]


def cast_to_slot(name, place, w, layer, after):
    _, r, c = w.shape
    tr = _tile(r, 512)

    def body(place_ref, w_ref, after_ref, o_ref):
        o_ref[...] = w_ref[...].astype(BF16)

    gs = pltpu.PrefetchScalarGridSpec(
        num_scalar_prefetch=1, grid=(r // tr,),
        in_specs=[pl.BlockSpec((None, tr, c), lambda i, pr: (layer, i, 0)), pl.BlockSpec(memory_space=pl.ANY)],
        out_specs=pl.BlockSpec((None, tr, c), lambda i, pr: (pr[1], i, 0)))
    return _pc(body, name=name, grid_spec=gs, in_specs=None, out_specs=None, out_shape=SDS((N_CHIPS, r, c), BF16),
               sem=("parallel",))(place, w, after)


def _place():
    x, y, c = lax.axis_index("x"), lax.axis_index("y"), lax.axis_index("c")
    chips = [(1 - x, y), (x, 1 - y), (1 - x, 1 - y)]
    return x, y, c, chips


def _half(ref, axis, c, rows, cols):
    if axis == 0:
        return ref.at[pl.ds(pl.multiple_of(c * (rows // 2), 16), rows // 2), :]
    return ref.at[:, pl.ds(pl.multiple_of(c * (cols // 2), LANES), cols // 2)]


def _gather_copies(specs, bufs, send, recv):
    x, y, c, chips = _place()
    cps = []
    for t, (_, rows, cols, axis) in enumerate(specs):
        mine = _half(bufs[t].at[2 * x + y], axis, c, rows, cols)
        for k, (cx, cy) in enumerate(chips):
            cps.append(pltpu.make_async_remote_copy(src_ref=mine, dst_ref=mine, send_sem=send.at[3 * t + k],
                                                    recv_sem=recv.at[3 * t + k], device_id=(cx, cy, c), device_id_type=MESH))
    return cps


def gather_start(name, specs, bufs):
    nt = len(bufs)
    n = 3 * nt

    def body(*refs):
        send, recv, token = refs[nt], refs[nt + 1], refs[-1]
        for cp in _gather_copies(specs, refs[:nt], send, recv):
            cp.start()
        token[...] = jnp.zeros(TOKEN_SHAPE, F32)

    out = _split_call(
        body, name=name, in_specs=[HBM_SPEC] * nt,
        out_specs=(SEM_SPEC, SEM_SPEC) + (HBM_SPEC,) * nt + (pl.BlockSpec(memory_space=pltpu.VMEM),),
        out_shape=(pltpu.SemaphoreType.DMA((n,)), pltpu.SemaphoreType.DMA((n,)))
        + tuple(pltpu.HBM(b.shape, b.dtype) for b in bufs) + (SDS(TOKEN_SHAPE, F32),),
        aliases={t: 2 + t for t in range(nt)})(*_in_hbm(bufs))
    return out[0], out[1], list(out[2:2 + nt]), out[-1]


def gather_wait(name, specs, send, recv, bufs, after):
    nt = len(bufs)

    def body(*refs):
        for cp in _gather_copies(specs, refs[:nt], refs[nt], refs[nt + 1]):
            cp.wait_send()
            cp.wait_recv()

    out = _split_call(
        body, name=name, in_specs=[HBM_SPEC] * nt + [SEM_SPEC, SEM_SPEC, pl.BlockSpec(memory_space=pl.ANY)],
        out_specs=(HBM_SPEC,) * nt, out_shape=tuple(pltpu.HBM(b.shape, b.dtype) for b in bufs),
        aliases={t: t for t in range(nt)})(*bufs, send, recv, after)
    return list(out)


def gather_pass(name, specs, bufs):
    nt = len(bufs)

    def body(*refs):
        ins, outs = refs[:nt], refs[nt:2 * nt]
        send, recv = refs[2 * nt:]
        x, y, c, chips = _place()
        cps = []
        for t, (_, rows, cols, axis) in enumerate(specs):
            for k, (cx, cy) in enumerate(chips):
                cp = pltpu.make_async_remote_copy(
                    src_ref=_half(ins[t].at[2 * cx + cy], axis, c, rows, cols),
                    dst_ref=_half(outs[t].at[2 * cx + cy], axis, c, rows, cols),
                    send_sem=send.at[3 * t + k], recv_sem=recv.at[3 * t + k], device_id=(x, y, 1 - c), device_id_type=MESH)
                cp.start()
                cps.append(cp)
        for t, (_, rows, cols, axis) in enumerate(specs):
            for k, (cx, cy) in enumerate(chips):
                theirs = _half(outs[t].at[2 * cx + cy], axis, 1 - c, rows, cols)
                pltpu.make_async_remote_copy(src_ref=theirs, dst_ref=theirs, send_sem=send.at[3 * t + k],
                                             recv_sem=recv.at[3 * t + k], device_id=(x, y, 1 - c), device_id_type=MESH).wait_recv()
        for cp in cps:
            cp.wait_send()

    return _pc(body, name=name, in_specs=[HBM_SPEC] * nt, out_specs=[HBM_SPEC] * nt,
               out_shape=[SDS(b.shape, b.dtype) for b in bufs], scratch=[pltpu.SemaphoreType.DMA((3 * nt,))] * 2,
               input_output_aliases={t: t for t in range(nt)})(*bufs)


def reduce_to_sibling(name, grads):
    nt = len(grads)

    def body(*refs):
        srcs, dsts = refs[:nt], refs[nt:2 * nt]
        send, recv = refs[2 * nt:]
        x, y, c, _ = _place()
        cps = []
        for t in range(nt):
            cp = pltpu.make_async_remote_copy(src_ref=srcs[t].at[1 - c], dst_ref=dsts[t], send_sem=send.at[t],
                                              recv_sem=recv.at[t], device_id=(x, y, 1 - c), device_id_type=MESH)
            cp.start()
            cps.append(cp)
        for cp in cps:
            cp.wait()

    return _pc(body, name=name, in_specs=[HBM_SPEC] * nt, out_specs=[HBM_SPEC] * nt,
               out_shape=[SDS(g.shape[1:], g.dtype) for g in grads],
               scratch=[pltpu.SemaphoreType.DMA((nt,))] * 2)(*grads)


def add_halves(name, place, grad, got):
    _, n, r, c = grad.shape
    tr = _tile(r, 512)

    def body(place_ref, g_ref, o_ref, part_ref, slot_ref):
        val = (g_ref[...].astype(F32) + o_ref[...].astype(F32)).astype(BF16)
        part_ref[...] = val

        @pl.when(pl.program_id(1) == place_ref[1])
        def _():
            slot_ref[...] = val

    gs = pltpu.PrefetchScalarGridSpec(
        num_scalar_prefetch=1, grid=(r // tr, n),
        in_specs=[pl.BlockSpec((None, None, tr, c), lambda i, j, pr: (pr[0], j, i, 0)),
                  pl.BlockSpec((None, tr, c), lambda i, j, pr: (j, i, 0))],
        out_specs=[pl.BlockSpec((None, tr, c), lambda i, j, pr: (j, i, 0)),
                   pl.BlockSpec((None, tr, c), lambda i, j, pr: (pr[1], i, 0))])
    return _pc(body, name=name, grid_spec=gs, in_specs=None, out_specs=None, out_shape=[SDS((n, r, c), BF16)] * 2,
               sem=("parallel", "arbitrary"))(place, grad, got)


def _chips_copies(parts, slots, send, recv):
    x, y, c, chips = _place()
    cps = []
    for t in range(len(parts)):
        for k, (cx, cy) in enumerate(chips):
            cps.append(pltpu.make_async_remote_copy(src_ref=parts[t].at[2 * cx + cy], dst_ref=slots[t].at[2 * x + y],
                                                    send_sem=send.at[3 * t + k], recv_sem=recv.at[3 * t + k],
                                                    device_id=(cx, cy, c), device_id_type=MESH))
    return cps


def chips_start(name, parts, slots):
    nt = len(parts)
    n = 3 * nt

    def body(*refs):
        send, recv, token = refs[2 * nt], refs[2 * nt + 1], refs[-1]
        for cp in _chips_copies(refs[:nt], refs[nt:2 * nt], send, recv):
            cp.start()
        token[...] = jnp.zeros(TOKEN_SHAPE, F32)

    both = list(parts) + list(slots)
    out = _split_call(
        body, name=name, in_specs=[HBM_SPEC] * (2 * nt),
        out_specs=(SEM_SPEC, SEM_SPEC) + (HBM_SPEC,) * (2 * nt) + (pl.BlockSpec(memory_space=pltpu.VMEM),),
        out_shape=(pltpu.SemaphoreType.DMA((n,)), pltpu.SemaphoreType.DMA((n,)))
        + tuple(pltpu.HBM(b.shape, b.dtype) for b in both) + (SDS(TOKEN_SHAPE, F32),),
        aliases={t: 2 + t for t in range(2 * nt)})(*_in_hbm(both))
    return out[0], out[1], list(out[2:2 + nt]), list(out[2 + nt:2 + 2 * nt]), out[-1]


def chips_wait(name, send, recv, parts, slots, after):
    nt = len(parts)

    def body(*refs):
        for cp in _chips_copies(refs[:nt], refs[nt:2 * nt], refs[2 * nt], refs[2 * nt + 1]):
            cp.wait_send()
            cp.wait_recv()

    both = list(parts) + list(slots)
    out = _split_call(
        body, name=name, in_specs=[HBM_SPEC] * (2 * nt) + [SEM_SPEC, SEM_SPEC, pl.BlockSpec(memory_space=pl.ANY)],
        out_specs=(HBM_SPEC,) * (2 * nt), out_shape=tuple(pltpu.HBM(b.shape, b.dtype) for b in both),
        aliases={t: t for t in range(2 * nt)})(*both, send, recv, after)
    return list(out[nt:])


def sum_chips(name, place, slots):
    n, r, c = slots.shape
    tr = _tile(r, 512)

    def body(place_ref, s_ref, o_ref):
        acc = s_ref[0].astype(F32)
        for k in range(1, n):
            acc = acc + s_ref[k].astype(F32)
        o_ref[...] = acc

    gs = pltpu.PrefetchScalarGridSpec(
        num_scalar_prefetch=1, grid=(r // tr,),
        in_specs=[pl.BlockSpec((n, tr, c), lambda i, pr: (0, i, 0))],
        out_specs=pl.BlockSpec((None, tr, c), lambda i, pr: (pr[0], i, 0)))
    return _pc(body, name=name, grid_spec=gs, in_specs=None, out_specs=None, out_shape=SDS((2, r, c), F32),
               sem=("parallel",))(place, slots)


def share_with_sibling(name, bufs):
    nt = len(bufs)

    def body(*refs):
        ins, outs = refs[:nt], refs[nt:2 * nt]
        send, recv = refs[2 * nt:]
        x, y, c, _ = _place()
        cps = []
        for t in range(nt):
            cp = pltpu.make_async_remote_copy(src_ref=ins[t].at[c], dst_ref=outs[t].at[c], send_sem=send.at[t], recv_sem=recv.at[t],
                                              device_id=(x, y, 1 - c), device_id_type=MESH)
            cp.start()
            cps.append(cp)
        for t in range(nt):
            theirs = outs[t].at[1 - c]
            pltpu.make_async_remote_copy(src_ref=theirs, dst_ref=theirs, send_sem=send.at[t], recv_sem=recv.at[t],
                                         device_id=(x, y, 1 - c), device_id_type=MESH).wait_recv()
        for cp in cps:
            cp.wait_send()

    return _pc(body, name=name, in_specs=[HBM_SPEC] * nt, out_specs=[HBM_SPEC] * nt,
               out_shape=[SDS(b.shape, F32) for b in bufs], scratch=[pltpu.SemaphoreType.DMA((nt,))] * 2,
               input_output_aliases={t: t for t in range(nt)})(*bufs)


def _adamw(w, g, m, v):
    m = ADAM_B1 * m + (1.0 - ADAM_B1) * g
    v = ADAM_B2 * v + (1.0 - ADAM_B2) * (g * g)
    m_hat = m / (1.0 - ADAM_B1 ** ADAM_STEP)
    v_hat = v / (1.0 - ADAM_B2 ** ADAM_STEP)
    delta = -ADAM_LR * (m_hat / (jnp.sqrt(v_hat) + ADAM_EPS) + ADAM_WD * w)
    return delta, m, v


def adamw_big(name, w, m, v, g0, g1, axis):
    _, r, c = w.shape
    _, rh, ch = g0.shape
    tr = _tile(rh, 256)
    nb = rh // tr
    if axis == 0:
        wspec = pl.BlockSpec((None, tr, ch), lambda l, h, i: (l, h * nb + i, 0))
    else:
        wspec = pl.BlockSpec((None, tr, ch), lambda l, h, i: (l, i, h))
    g0spec = pl.BlockSpec((None, tr, ch), lambda l, h, i: (h * (1 - l), i * (1 - l), 0))
    g1spec = pl.BlockSpec((None, tr, ch), lambda l, h, i: (h * l, i * l, 0))

    def body(w_ref, m_ref, v_ref, g0_ref, g1_ref, go_ref, d_ref, mo_ref, vo_ref):
        def run(g_ref):
            g = g_ref[...]
            delta, mn, vn = _adamw(w_ref[...], g, m_ref[...], v_ref[...])
            go_ref[...] = g
            d_ref[...] = delta
            mo_ref[...] = mn
            vo_ref[...] = vn

        @pl.when(pl.program_id(0) == 0)
        def _():
            run(g0_ref)

        @pl.when(pl.program_id(0) == 1)
        def _():
            run(g1_ref)

    return _pc(body, name=name, grid=(2, 2, nb), in_specs=[wspec, wspec, wspec, g0spec, g1spec], out_specs=[wspec] * 4,
               out_shape=[SDS(w.shape, F32)] * 4, sem=("parallel", "parallel", "parallel"))(w, m, v, g0, g1)


SMALL = (("pre_mix_gain", 2048), ("post_mix_gain", 2048), ("pre_ffn_gain", 2048), ("post_ffn_gain", 2048), ("ple_gain", 2048),
         ("attn_out_gain", 1024), ("hgrn_out_gain", 1024), ("hgrn_lb_logits", 1024), ("attn_sinks", 128))
SMALL_ROWS = sum(2 * w // LANES for _, w in SMALL)
SMALL_PAD = -(-SMALL_ROWS // 8) * 8
LB_ROW = sum(2 * w // LANES for _, w in SMALL[:7])


def _pack_small(parts):
    rows = []
    for nm, w in SMALL:
        a = parts[nm].astype(F32)
        if a.shape[1] != w:
            a = jnp.pad(a, ((0, 0), (0, w - a.shape[1])))
        rows.append(a.reshape(2 * w // LANES, LANES))
    rows.append(jnp.zeros((SMALL_PAD - SMALL_ROWS, LANES), F32))
    return jnp.concatenate(rows, axis=0)


def _unpack_small(packed, widths):
    out, r = {}, 0
    for nm, w in SMALL:
        n = 2 * w // LANES
        out[nm] = packed[r:r + n].reshape(2, w)[:, :widths[nm]]
        r += n
    return out


def allreduce_small(name, packed):
    rows = packed.shape[0]

    def body(x_ref, o_ref, buf, send, recv, own_sem):
        x, y, c, _ = _place()
        me = 4 * x + 2 * y + c
        own = pltpu.make_async_copy(x_ref, buf.at[me], own_sem)
        own.start()
        cps = []
        for k in range(1, 8):
            px, py, pc = x ^ (k >> 2), y ^ ((k >> 1) & 1), c ^ (k & 1)
            cp = pltpu.make_async_remote_copy(src_ref=x_ref, dst_ref=buf.at[me], send_sem=send.at[k - 1], recv_sem=recv.at[k - 1],
                                              device_id=(px, py, pc), device_id_type=MESH)
            cp.start()
            cps.append(cp)
        for k in range(1, 8):
            px, py, pc = x ^ (k >> 2), y ^ ((k >> 1) & 1), c ^ (k & 1)
            slot = buf.at[4 * px + 2 * py + pc]
            pltpu.make_async_remote_copy(src_ref=slot, dst_ref=slot, send_sem=send.at[k - 1], recv_sem=recv.at[k - 1],
                                         device_id=(px, py, pc), device_id_type=MESH).wait_recv()
        for cp in cps:
            cp.wait_send()
        own.wait()
        acc = buf[0]
        for k in range(1, 8):
            acc = acc + buf[k]
        o_ref[...] = acc

    vm = pl.BlockSpec(memory_space=pltpu.VMEM)
    return _pc(body, name=name, in_specs=[vm], out_specs=vm, out_shape=SDS((rows, LANES), F32),
               scratch=[pltpu.VMEM((8, rows, LANES), F32), pltpu.SemaphoreType.DMA((7,)), pltpu.SemaphoreType.DMA((7,)),
                        pltpu.SemaphoreType.DMA])(packed)


def adamw_small(name, w, m, v, g):
    rows = w.shape[0]
    n = HGRN_WIDTH // LANES

    def body(w_ref, m_ref, v_ref, g_ref, go_ref, d_ref, mo_ref, vo_ref):
        go_ref[...] = g_ref[...]
        l0 = w_ref[LB_ROW:LB_ROW + n, :]
        l1 = w_ref[LB_ROW + n:LB_ROW + 2 * n, :]
        mx = jnp.maximum(l0, l1)
        e0, e1 = jnp.exp(l0 - mx), jnp.exp(l1 - mx)
        s0, s1 = e0 / (e0 + e1), e1 / (e0 + e1)
        dlb1 = g_ref[LB_ROW + n:LB_ROW + 2 * n, :]
        inner = s1 * dlb1
        go_ref[LB_ROW:LB_ROW + n, :] = s0 * (0.0 - inner)
        go_ref[LB_ROW + n:LB_ROW + 2 * n, :] = s1 * (dlb1 - inner)
        delta, mn, vn = _adamw(w_ref[...], go_ref[...], m_ref[...], v_ref[...])
        d_ref[...] = delta
        mo_ref[...] = mn
        vo_ref[...] = vn

    vm = pl.BlockSpec(memory_space=pltpu.VMEM)
    return _pc(body, name=name, in_specs=[vm] * 4, out_specs=[vm] * 4, out_shape=[SDS((rows, LANES), F32)] * 4)(w, m, v, g)


def _layer_fwd(l, x, h1, p_l, w_in_g, rest_of_weights, gains, cos, sin, sinks, lb_logits, g_next, target):
    n = f"l{l}_"
    proj = mm_col(n + "in_proj", h1, w_in_g)
    qp, kp, vp = rope_qkv(n + "rope_qkv", proj, cos, sin)
    attn = attn_fwd(n + "attn_fwd", qp, kp, vp, sinks)
    o, states = hgrn_fwd(n + "hgrn_fwd", proj, lb_logits, l)
    cat = mix_out_fwd(n + "mix_out_fwd", attn, o, proj, gains["attn_out_gain"], gains["hgrn_out_gain"])
    wts = dict(rest_of_weights(cat), w_in=w_in_g)
    m = mm_row(n + "out_proj", cat, wts["w_out"])
    x1, h2 = post_pre_norm(n + "post_mix", m, gains["post_mix_gain"], x, gains["pre_ffn_gain"])
    g, u, a = ffn_gate_up(n + "ffn_gate_up", h2, wts["w_ffn_gate"], wts["w_ffn_up"])
    f = mm_row(n + "ffn_down", a, wts["w_ffn_down"])
    x2, h3 = post_pre_norm(n + "post_ffn", f, gains["post_ffn_gain"], x1, gains["ple_gain"])
    z = mm_row(n + "ple_gate", h3, wts["w_ple_gate"])
    pp = mm_col(n + "ple_proj", p_l, wts["w_ple_proj"])
    if target is None:
        out = ple_fwd_mid(n + "ple_fwd", z, pp, x2, g_next)
    else:
        out = ple_fwd_loss(n + "ple_loss", z, pp, x2, target)
    saved = dict(x=x, h1=h1, proj=proj, qp=qp, kp=kp, vp=vp, attn=attn, o=o, states=states, cat=cat, m=m, x1=x1, h2=h2,
                 g=g, u=u, a=a, f=f, x2=x2, h3=h3, z=z, pp=pp, p=p_l)
    return out, saved, wts


EARLY = ("w_ple_gate", "w_ple_proj", "w_ffn_down", "w_ffn_gate", "w_ffn_up")
LATE = ("w_out", "w_in")


def _layer_bwd_ffn(l, dx3, sv, wts, gains):
    n = f"l{l}_"
    dpp, dz = ple_bwd(n + "ple_bwd", dx3, sv["z"], sv["pp"])
    dh3 = mm_row_t(n + "ple_gate_dx", dz, wts["w_ple_gate"])
    dx2, d_ple_gain = norm_bwd(n + "ple_norm_bwd", sv["x2"], gains["ple_gain"], dh3, dx3)
    df, d_post_ffn = norm_bwd(n + "post_ffn_bwd", sv["f"], gains["post_ffn_gain"], dx2, None, BF16)
    dg, du = ffn_down_bwd(n + "ffn_down_bwd", df, wts["w_ffn_down"], sv["g"], sv["u"])
    big = dict(
        w_ple_gate=mm_wg_row(n + "ple_gate_dw", sv["h3"], dz),
        w_ple_proj=mm_wg_col(n + "ple_proj_dw", sv["p"], dpp),
        w_ffn_down=mm_wg_row(n + "ffn_down_dw", sv["a"], df),
        w_ffn_gate=mm_wg_col(n + "ffn_gate_dw", sv["h2"], dg),
        w_ffn_up=mm_wg_col(n + "ffn_up_dw", sv["h2"], du),
    )
    return dict(dx2=dx2, dg=dg, du=du), big, dict(ple_gain=d_ple_gain, post_ffn_gain=d_post_ffn)


def _layer_bwd_mix(l, st, sv, wts, gains, cos, sin, sinks, lb_logits):
    n = f"l{l}_"
    dh2 = mm_col_t(n + "ffn_gate_dx", st["dg"], wts["w_ffn_gate"])
    dh2 = mm_col_t(n + "ffn_up_dx", st["du"], wts["w_ffn_up"], add=dh2)
    dx1, d_pre_ffn = norm_bwd(n + "pre_ffn_bwd", sv["x1"], gains["pre_ffn_gain"], dh2, st["dx2"])
    dm, d_post_mix = norm_bwd(n + "post_mix_bwd", sv["m"], gains["post_mix_gain"], dx1, None, BF16)
    dcat = mm_row_t(n + "out_proj_dx", dm, wts["w_out"])
    dattn, do, dhg, d_attn_gain, d_hgrn_gain = mix_out_bwd(n + "mix_out_bwd", dcat, sv["attn"], sv["o"], sv["proj"],
                                                            gains["attn_out_gain"], gains["hgrn_out_gain"])
    dqp, dkc, dkp, dvc, dvp, dsinks = attn_bwd(n + "attn_bwd", sv["qp"], sv["kp"], sv["vp"], sinks, dattn)
    dqkv = rope_bwd(n + "rope_bwd", dqp, dkc, dkp, dvc, dvp, cos, sin)
    dhq, dhf, dhi, dlb = hgrn_bwd(n + "hgrn_bwd", sv["proj"], lb_logits, l, sv["states"], do)
    dproj = jnp.concatenate([dqkv, dhq, dhf, dhi, dhg], axis=1)
    dh1 = mm_col_t(n + "in_proj_dx", dproj, wts["w_in"])
    dx, d_pre_mix = norm_bwd(n + "pre_mix_bwd", sv["x"], gains["pre_mix_gain"], dh1, dx1)
    big = dict(w_out=mm_wg_row(n + "out_proj_dw", sv["cat"], dm), w_in=mm_wg_col(n + "in_proj_dw", sv["h1"], dproj))
    small = dict(pre_mix_gain=d_pre_mix, post_mix_gain=d_post_mix, pre_ffn_gain=d_pre_ffn, attn_out_gain=d_attn_gain,
                 hgrn_out_gain=d_hgrn_gain, hgrn_lb_logits=dlb, attn_sinks=dsinks)
    return dx, big, small


def _layer_bwd(l, dx3, sv, wts, gains, cos, sin, sinks, lb_logits):
    st, early, small_a = _layer_bwd_ffn(l, dx3, sv, wts, gains)
    dx, late, small_b = _layer_bwd_mix(l, st, sv, wts, gains, cos, sin, sinks, lb_logits)
    return dx, {**early, **late}, {**small_a, **small_b}


def _reduce_start(tag, names, big, place):
    got = reduce_to_sibling(tag + "_reduce_to_sibling", [big[nm] for nm in names])
    pairs = [add_halves(f"{tag}_add_{nm}", place, big[nm], gt) for nm, gt in zip(names, got)]
    return chips_start(tag + "_chips_start", [pr[0] for pr in pairs], [pr[1] for pr in pairs])


def _reduce_finish(tag, names, started, place, after):
    send, recv, parts, slots, _ = started
    slots = chips_wait(tag + "_chips_wait", send, recv, parts, slots, after)
    bufs = [sum_chips(f"{tag}_sum_{nm}", place, sl) for nm, sl in zip(names, slots)]
    return dict(zip(names, share_with_sibling(tag + "_share_with_sibling", bufs)))


def _with_token(gains, name, token):
    out = dict(gains)
    out[name] = gains[name] + token[0, 0]
    return out


def kernel(x, p, positions, w_in, attn_sinks, hgrn_lb_logits, attn_out_gain, hgrn_out_gain, w_out, pre_mix_gain, post_mix_gain, pre_ffn_gain, post_ffn_gain, w_ffn_gate, w_ffn_up, w_ffn_down, ple_gain, w_ple_gate, w_ple_proj, loss_target, m_w_in, m_attn_sinks, m_hgrn_lb_logits, m_attn_out_gain, m_hgrn_out_gain, m_w_out, m_pre_mix_gain, m_post_mix_gain, m_pre_ffn_gain, m_post_ffn_gain, m_w_ffn_gate, m_w_ffn_up, m_w_ffn_down, m_ple_gain, m_w_ple_gate, m_w_ple_proj, v_w_in, v_attn_sinks, v_hgrn_lb_logits, v_attn_out_gain, v_hgrn_out_gain, v_w_out, v_pre_mix_gain, v_post_mix_gain, v_pre_ffn_gain, v_post_ffn_gain, v_w_ffn_gate, v_w_ffn_up, v_w_ffn_down, v_ple_gain, v_w_ple_gate, v_w_ple_proj):
    given = dict(locals())
    depth = 2
    place = jnp.stack([lax.axis_index("c"), 2 * lax.axis_index("x") + lax.axis_index("y")]).astype(jnp.int32)
    xs = x[0]
    tgt = loss_target[0]
    pos_col = positions.reshape(-1, 1)
    half = 32
    inv_freq = ROPE_THETA ** (-jnp.arange(half, dtype=F32) / half)
    inv_freq = jnp.tile(inv_freq, 4).reshape(1, LANES)
    gains = [{nm: given[nm][l:l + 1] for nm, _ in SMALL[:7]} for l in range(depth)]
    names = [nm for nm, *_ in BIG]
    first, others = names[:1], names[1:]

    def specs(nms):
        return [BIG_BY_NAME[nm] for nm in nms]

    def start_gather(tag, l, nms, after):
        return gather_start(tag + "_gather_start", specs(nms),
                            [cast_to_slot(f"{tag}_cast_{nm}", place, given[nm], l, after) for nm in nms])

    def finish_gather(tag, nms, started, after):
        bufs = gather_wait(tag + "_gather_wait", specs(nms), started[0], started[1], started[2], after)
        return dict(zip(nms, gather_pass(tag + "_gather_pass", specs(nms), bufs)))

    g0a = start_gather("l0a", 0, first, place)
    g0b = start_gather("l0b", 0, others, g0a[3])
    started = {}

    def rest_of_layer0(after):
        got = finish_gather("l0b", others, g0b, after)
        started["l1"] = start_gather("l1", 1, names, got["w_out"])
        return got

    cos, sin = rope_tables("rope_tables", pos_col, inv_freq)
    h1 = pre_norm("l0_pre_mix", xs, gains[0]["pre_mix_gain"])
    w_in0 = finish_gather("l0a", first, g0a, g0b[3])["w_in"]
    (x_mid, h1_next), sv0, wts0 = _layer_fwd(0, xs, h1, p[0, 0], w_in0, rest_of_layer0, gains[0], cos, sin, attn_sinks[0],
                                             hgrn_lb_logits, gains[1]["pre_mix_gain"], None)
    wts1 = finish_gather("l1", names, started["l1"], x_mid)
    (dy, loss_part), sv1, _ = _layer_fwd(1, x_mid, h1_next, p[1, 0], wts1["w_in"], lambda after: wts1, gains[1], cos, sin,
                                         attn_sinks[1], hgrn_lb_logits, None, tgt)

    dx_mid, big1, small1 = _layer_bwd(1, dy, sv1, wts1, gains[1], cos, sin, attn_sinks[1], hgrn_lb_logits)
    r1 = _reduce_start("l1", names, big1, place)
    st0, early0, small0 = _layer_bwd_ffn(0, dx_mid, sv0, wts0, _with_token(gains[0], "ple_gain", r1[4]))
    r0e = _reduce_start("l0e", EARLY, early0, place)
    dx0, late0, small0b = _layer_bwd_mix(0, st0, sv0, wts0, _with_token(gains[0], "pre_ffn_gain", r0e[4]), cos, sin,
                                         attn_sinks[0], hgrn_lb_logits)
    small0 = {**small0, **small0b}
    r0l = _reduce_start("l0l", LATE, late0, place)
    red1 = _reduce_finish("l1", names, r1, place, r0l[4])
    red0 = _reduce_finish("l0e", EARLY, r0e, place, red1[names[-1]])

    loss = lax.psum(loss_part[0, 0], ("x", "y", "c"))
    grad_x = dx0[None]

    out_big = {}
    for nm in EARLY:
        out_big[nm] = adamw_big("adamw_" + nm, given[nm], given["m_" + nm], given["v_" + nm], red0[nm], red1[nm], BIG_BY_NAME[nm][3])
    red0.update(_reduce_finish("l0l", LATE, r0l, place, out_big[EARLY[-1]][3]))
    for nm in LATE:
        out_big[nm] = adamw_big("adamw_" + nm, given[nm], given["m_" + nm], given["v_" + nm], red0[nm], red1[nm], BIG_BY_NAME[nm][3])

    widths = {nm: given[nm].shape[1] for nm, _ in SMALL}
    small_g = {nm: jnp.concatenate([small0[nm][:, :widths[nm]] if nm != "attn_sinks" else small0[nm][:, :LANES],
                                    small1[nm][:, :widths[nm]] if nm != "attn_sinks" else small1[nm][:, :LANES]], axis=0)
               for nm, _ in SMALL}
    g_sum = allreduce_small("allreduce_small", _pack_small(small_g))
    sm = adamw_small("adamw_small", _pack_small({nm: given[nm] for nm, _ in SMALL}),
                     _pack_small({nm: given["m_" + nm] for nm, _ in SMALL}),
                     _pack_small({nm: given["v_" + nm] for nm, _ in SMALL}), g_sum)
    out_small = [_unpack_small(a, widths) for a in sm]

    order = ["w_in", "attn_sinks", "hgrn_lb_logits", "attn_out_gain", "hgrn_out_gain", "w_out", "pre_mix_gain", "post_mix_gain",
             "pre_ffn_gain", "post_ffn_gain", "w_ffn_gate", "w_ffn_up", "w_ffn_down", "ple_gain", "w_ple_gate", "w_ple_proj"]
    res = [loss, grad_x]
    for k in range(4):
        for nm in order:
            res.append(out_big[nm][k] if nm in out_big else out_small[k][nm])
    return tuple(res)
```

```python
import functools

import jax
import jax.numpy as jnp
from jax import lax
from jax.experimental import pallas as pl
from jax.experimental.pallas import tpu as pltpu

F32, BF16 = jnp.float32, jnp.bfloat16
SDS = jax.ShapeDtypeStruct
MESH = pl.DeviceIdType.MESH

D_MODEL = 2048
ATTN_WIDTH = 1024
HGRN_WIDTH = 1024
KV_WIDTH = 256
N_Q_HEADS = 16
N_KV_HEADS = 4
Q_PER_KV = 4
WINDOW = 128
MASK_VALUE = -1e30
ROPE_THETA = 10000.0
HGRN_HEADS = 8
HGRN_CHUNK = 16
D_FF = 5632
D_PLE = 256
RMS_EPS = 1e-6
LANES = 128
N_CHIPS = 4
COL_Q, COL_K, COL_V, COL_HQ, COL_HF, COL_HI, COL_HG = 0, 8, 10, 12, 20, 28, 36

ADAM_LR, ADAM_B1, ADAM_B2, ADAM_EPS, ADAM_WD, ADAM_STEP = 0.001, 0.9, 0.999, 1e-08, 0.01, 10

VMEM_LIMIT = 56 * 1024 * 1024
ROW_TILE = 256

_NN = (((1,), (0,)), ((), ()))
_NT = (((1,), (1,)), ((), ()))
_TN = (((0,), (0,)), ((), ()))


def _pc(body, *, name, out_shape, in_specs, out_specs, grid=(), scratch=(), sem=None, grid_spec=None, **kw):
    params = dict(vmem_limit_bytes=VMEM_LIMIT)
    if sem is not None:
        params["dimension_semantics"] = sem
    if grid_spec is not None:
        return pl.pallas_call(body, name=name, out_shape=out_shape, grid_spec=grid_spec,
                              compiler_params=pltpu.CompilerParams(**params), **kw)
    return pl.pallas_call(body, name=name, out_shape=out_shape, grid=grid, in_specs=in_specs, out_specs=out_specs,
                          scratch_shapes=list(scratch), compiler_params=pltpu.CompilerParams(**params), **kw)


def _sigmoid(x):
    return 1.0 / (1.0 + jnp.exp(-x))


def _rstd(x):
    return lax.rsqrt(jnp.mean(x * x, axis=-1, keepdims=True) + RMS_EPS)


def _rows(t, w, col=0):
    return pl.BlockSpec((t, w), lambda i, col=col: (i, col))


def _fixed(shape):
    return pl.BlockSpec(shape, lambda *_: (0,) * len(shape))


def _mm(name, a, b, *, dims, grid, a_spec, b_spec, o_spec, out_shape, parts=1, add=None, add_spec=None):
    def body(*refs):
        a_ref, b_ref, o_ref = refs[0], refs[1], refs[-1]
        if parts == 1:
            r = lax.dot_general(a_ref[...].astype(BF16), b_ref[...].astype(BF16), dims, preferred_element_type=F32)
        else:
            w = a_ref.shape[1] // parts
            r = None
            for j in range(parts):
                t = lax.dot_general(a_ref[:, j * w:(j + 1) * w].astype(BF16), b_ref[j].astype(BF16), dims,
                                    preferred_element_type=F32)
                r = t if r is None else r + t
        if add is not None:
            r = r + refs[2][...]
        o_ref[...] = r.astype(o_ref.dtype)

    ins = [a, b] + ([] if add is None else [add])
    specs = [a_spec, b_spec] + ([] if add is None else [add_spec])
    return _pc(body, name=name, grid=grid, in_specs=specs, out_specs=o_spec, out_shape=out_shape,
               sem=("parallel",) * len(grid))(*ins)


def _tile(n, t):
    if n <= t:
        return n
    while n % t:
        t //= 2
    assert t % 8 == 0
    return t


def mm_col(name, a, wg, out_dtype=F32):
    s, k = a.shape
    _, _, n = wg.shape
    tm = _tile(s, 512)
    return _mm(name, a, wg, dims=_NN, grid=(N_CHIPS, s // tm),
               a_spec=pl.BlockSpec((tm, k), lambda j, i: (i, 0)),
               b_spec=pl.BlockSpec((None, k, n), lambda j, i: (j, 0, 0)),
               o_spec=pl.BlockSpec((tm, n), lambda j, i: (i, j)),
               out_shape=SDS((s, N_CHIPS * n), out_dtype))


def mm_row(name, a, wg, out_dtype=F32):
    s, _ = a.shape
    _, r, n = wg.shape
    tm = _tile(s, 512)
    tn = _tile(n, 1024 if r > 512 else 2048)
    return _mm(name, a, wg, dims=_NN, grid=(n // tn, s // tm), parts=N_CHIPS,
               a_spec=pl.BlockSpec((tm, N_CHIPS * r), lambda j, i: (i, 0)),
               b_spec=pl.BlockSpec((N_CHIPS, r, tn), lambda j, i: (0, 0, j)),
               o_spec=pl.BlockSpec((tm, tn), lambda j, i: (i, j)),
               out_shape=SDS((s, n), out_dtype))


def mm_col_t(name, dy, wg, add=None, out_dtype=F32):
    s, _ = dy.shape
    _, k, n = wg.shape
    tm = _tile(s, 512)
    tk = _tile(k, 1024)
    return _mm(name, dy, wg, dims=_NT, grid=(k // tk, s // tm), parts=N_CHIPS,
               a_spec=pl.BlockSpec((tm, N_CHIPS * n), lambda j, i: (i, 0)),
               b_spec=pl.BlockSpec((N_CHIPS, tk, n), lambda j, i: (0, j, 0)),
               o_spec=pl.BlockSpec((tm, tk), lambda j, i: (i, j)),
               add=add, add_spec=pl.BlockSpec((tm, tk), lambda j, i: (i, j)),
               out_shape=SDS((s, k), out_dtype))


def mm_row_t(name, dy, wg, out_dtype=F32):
    s, n = dy.shape
    _, r, _ = wg.shape
    tm = _tile(s, 512)
    return _mm(name, dy, wg, dims=_NT, grid=(N_CHIPS, s // tm),
               a_spec=pl.BlockSpec((tm, n), lambda j, i: (i, 0)),
               b_spec=pl.BlockSpec((None, r, n), lambda j, i: (j, 0, 0)),
               o_spec=pl.BlockSpec((tm, r), lambda j, i: (i, j)),
               out_shape=SDS((s, N_CHIPS * r), out_dtype))


def norm_bwd_pair(name, x_a, gain_a, dy, dres, x_b, gain_b):
    s, d = x_a.shape
    t = _tile(s, ROW_TILE)

    def one(xv, g, dyv):
        r = _rstd(xv)
        xh = xv * r
        dyg = dyv * g
        return r * (dyg - xh * jnp.mean(dyg * xh, axis=-1, keepdims=True)), jnp.sum(dyv * xh, axis=0, keepdims=True)

    def body(xa_ref, ga_ref, dy_ref, r_ref, xb_ref, gb_ref, dx_ref, db_ref, dga_ref, dgb_ref):
        dx, pa = one(xa_ref[...], ga_ref[...], dy_ref[...])
        dx = dx + r_ref[...]
        dx_ref[...] = dx
        db, pb = one(xb_ref[...], gb_ref[...], dx)
        db_ref[...] = db.astype(BF16)

        @pl.when(pl.program_id(0) == 0)
        def _():
            dga_ref[...] = pa
            dgb_ref[...] = pb

        @pl.when(pl.program_id(0) > 0)
        def _():
            dga_ref[...] += pa
            dgb_ref[...] += pb

    row, gain = _rows(t, d), _fixed((1, d))
    return _pc(body, name=name, grid=(s // t,), in_specs=[row, gain, row, row, row, gain], out_specs=[row, row, gain, gain],
               out_shape=[SDS((s, d), F32), SDS((s, d), BF16), SDS((1, d), F32), SDS((1, d), F32)],
               sem=("arbitrary",))(x_a, gain_a, dy, dres, x_b, gain_b)


def ffn_gate_up(name, h, wg_gate, wg_up):
    s, k = h.shape
    _, _, n = wg_gate.shape
    tm = _tile(s, 512)

    def body(h_ref, wg_ref, wu_ref, g_ref, u_ref, a_ref):
        hv = h_ref[...]
        g = jnp.dot(hv, wg_ref[...], preferred_element_type=F32)
        u = jnp.dot(hv, wu_ref[...], preferred_element_type=F32)
        g_ref[...] = g
        u_ref[...] = u
        a_ref[...] = ((g * _sigmoid(g)) * u).astype(BF16)

    wspec = pl.BlockSpec((None, k, n), lambda j, i: (j, 0, 0))
    ospec = pl.BlockSpec((tm, n), lambda j, i: (i, j))
    return _pc(body, name=name, grid=(N_CHIPS, s // tm), in_specs=[pl.BlockSpec((tm, k), lambda j, i: (i, 0)), wspec, wspec],
               out_specs=[ospec] * 3, out_shape=[SDS((s, N_CHIPS * n), F32)] * 2 + [SDS((s, N_CHIPS * n), BF16)],
               sem=("parallel", "parallel"))(h, wg_gate, wg_up)


def ffn_down_bwd(name, df, wg_down, g, u):
    s, n = df.shape
    _, r, _ = wg_down.shape
    tm = _tile(s, 512)

    def body(df_ref, w_ref, g_ref, u_ref, dg_ref, du_ref):
        da = lax.dot_general(df_ref[...], w_ref[...], _NT, preferred_element_type=F32)
        gv = g_ref[...]
        sg = _sigmoid(gv)
        du_ref[...] = (da * (gv * sg)).astype(BF16)
        dg_ref[...] = ((da * u_ref[...]) * (sg * (1.0 + gv * (1.0 - sg)))).astype(BF16)

    cspec = pl.BlockSpec((tm, r), lambda j, i: (i, j))
    return _pc(body, name=name, grid=(N_CHIPS, s // tm),
               in_specs=[pl.BlockSpec((tm, n), lambda j, i: (i, 0)), pl.BlockSpec((None, r, n), lambda j, i: (j, 0, 0)), cspec, cspec],
               out_specs=[cspec] * 2, out_shape=[SDS((s, N_CHIPS * r), BF16)] * 2,
               sem=("parallel", "parallel"))(df, wg_down, g, u)


def mm_wg_col(name, a, dy):
    s, k = a.shape
    n = dy.shape[1] // N_CHIPS
    tm = _tile(k // 2, 512)
    hb = (k // 2) // tm
    return _mm(name, a, dy, dims=_TN, grid=(N_CHIPS, k // tm),
               a_spec=pl.BlockSpec((s, tm), lambda j, i: (0, i)),
               b_spec=pl.BlockSpec((s, n), lambda j, i: (0, j)),
               o_spec=pl.BlockSpec((None, None, tm, n), lambda j, i: (i // hb, j, i % hb, 0)),
               out_shape=SDS((2, N_CHIPS, k // 2, n), BF16))


def mm_wg_row(name, a, dy):
    s, n = dy.shape
    r = a.shape[1] // N_CHIPS
    tn = _tile(n // 2, 512)
    nb = (n // 2) // tn
    return _mm(name, a, dy, dims=_TN, grid=(N_CHIPS, n // tn),
               a_spec=pl.BlockSpec((s, r), lambda j, i: (0, j)),
               b_spec=pl.BlockSpec((s, tn), lambda j, i: (0, i)),
               o_spec=pl.BlockSpec((None, None, r, tn), lambda j, i: (i // nb, j, 0, i % nb)),
               out_shape=SDS((2, N_CHIPS, r, n // 2), BF16))


def pre_norm(name, x, gain):
    s, d = x.shape
    t = _tile(s, ROW_TILE)

    def body(x_ref, g_ref, o_ref):
        xv = x_ref[...]
        o_ref[...] = ((xv * _rstd(xv)) * g_ref[...]).astype(BF16)

    return _pc(body, name=name, grid=(s // t,), in_specs=[_rows(t, d), _fixed((1, d))], out_specs=_rows(t, d),
               out_shape=SDS((s, d), BF16), sem=("parallel",))(x, gain)


def post_pre_norm(name, m, g_post, res, g_pre):
    s, d = m.shape
    t = _tile(s, ROW_TILE)

    def body(m_ref, gp_ref, r_ref, gn_ref, x_ref, h_ref):
        mv = m_ref[...]
        xn = r_ref[...] + (mv * _rstd(mv)) * gp_ref[...]
        x_ref[...] = xn
        h_ref[...] = ((xn * _rstd(xn)) * gn_ref[...]).astype(BF16)

    return _pc(body, name=name, grid=(s // t,),
               in_specs=[_rows(t, d), _fixed((1, d)), _rows(t, d), _fixed((1, d))],
               out_specs=[_rows(t, d), _rows(t, d)], out_shape=[SDS((s, d), F32), SDS((s, d), BF16)],
               sem=("parallel",))(m, g_post, res, g_pre)


def ple_fwd_mid(name, z, pp, x2, g_next):
    s, d = z.shape
    t = _tile(s, ROW_TILE)

    def body(z_ref, p_ref, x_ref, g_ref, xo_ref, h_ref):
        xn = x_ref[...] + p_ref[...] * _sigmoid(z_ref[...])
        xo_ref[...] = xn
        h_ref[...] = ((xn * _rstd(xn)) * g_ref[...]).astype(BF16)

    return _pc(body, name=name, grid=(s // t,),
               in_specs=[_rows(t, d), _rows(t, d), _rows(t, d), _fixed((1, d))],
               out_specs=[_rows(t, d), _rows(t, d)], out_shape=[SDS((s, d), F32), SDS((s, d), BF16)],
               sem=("parallel",))(z, pp, x2, g_next)


def ple_fwd_loss(name, z, pp, x2, target):
    s, d = z.shape
    t = _tile(s, ROW_TILE)

    def body(z_ref, p_ref, x_ref, t_ref, dy_ref, l_ref):
        err = (x_ref[...] + p_ref[...] * _sigmoid(z_ref[...])) - t_ref[...]
        dy_ref[...] = err * (1.0 / d)
        part = jnp.sum(jnp.sum(err * err, axis=-1, keepdims=True), axis=0, keepdims=True) * (0.5 / d)

        @pl.when(pl.program_id(0) == 0)
        def _():
            l_ref[...] = part

        @pl.when(pl.program_id(0) > 0)
        def _():
            l_ref[...] += part

    return _pc(body, name=name, grid=(s // t,),
               in_specs=[_rows(t, d), _rows(t, d), _rows(t, d), _rows(t, d)],
               out_specs=[_rows(t, d), _fixed((1, 1))], out_shape=[SDS((s, d), F32), SDS((1, 1), F32)],
               sem=("arbitrary",))(z, pp, x2, target)


def ple_bwd(name, dx3, z, pp):
    s, d = z.shape
    t = _tile(s, ROW_TILE)

    def body(d_ref, z_ref, p_ref, dpp_ref, dz_ref):
        gate = _sigmoid(z_ref[...])
        dv = d_ref[...]
        dpp_ref[...] = (dv * gate).astype(BF16)
        dz_ref[...] = ((dv * p_ref[...]) * (gate * (1.0 - gate))).astype(BF16)

    return _pc(body, name=name, grid=(s // t,), in_specs=[_rows(t, d)] * 3, out_specs=[_rows(t, d)] * 2,
               out_shape=[SDS((s, d), BF16)] * 2, sem=("parallel",))(dx3, z, pp)


def norm_bwd(name, xin, gain, dy, dres=None, out_dtype=F32):
    s, d = xin.shape
    t = _tile(s, ROW_TILE)

    def body(*refs):
        if dres is None:
            x_ref, g_ref, dy_ref, dx_ref, dg_ref = refs
            r_ref = None
        else:
            x_ref, g_ref, dy_ref, r_ref, dx_ref, dg_ref = refs
        xv = x_ref[...]
        r = _rstd(xv)
        xh = xv * r
        dyv = dy_ref[...].astype(F32)
        dyg = dyv * g_ref[...]
        c = jnp.mean(dyg * xh, axis=-1, keepdims=True)
        dx = r * (dyg - xh * c)
        if r_ref is not None:
            dx = dx + r_ref[...]
        dx_ref[...] = dx.astype(out_dtype)
        part = jnp.sum(dyv * xh, axis=0, keepdims=True)

        @pl.when(pl.program_id(0) == 0)
        def _():
            dg_ref[...] = part

        @pl.when(pl.program_id(0) > 0)
        def _():
            dg_ref[...] += part

    ins = [xin, gain, dy] + ([] if dres is None else [dres])
    specs = [_rows(t, d), _fixed((1, d)), _rows(t, d)] + ([] if dres is None else [_rows(t, d)])
    return _pc(body, name=name, grid=(s // t,), in_specs=specs, out_specs=[_rows(t, d), _fixed((1, d))],
               out_shape=[SDS((s, d), out_dtype), SDS((1, d), F32)], sem=("arbitrary",))(*ins)


def _lane(shape):
    return lax.broadcasted_iota(jnp.int32, shape, 1)


def _swap_halves(x):
    lo = (_lane(x.shape) % 64) < 32
    return jnp.where(lo, pltpu.roll(x, 96, 1), pltpu.roll(x, 32, 1))


def rope_tables(name, pos_col, inv_freq):
    s = pos_col.shape[0]
    t = _tile(s, ROW_TILE)

    def body(p_ref, f_ref, c_ref, s_ref):
        ang = p_ref[...].astype(F32) * f_ref[...]
        lo = (_lane(ang.shape) % 64) < 32
        c_ref[...] = jnp.cos(ang)
        sn = jnp.sin(ang)
        s_ref[...] = jnp.where(lo, -sn, sn)

    return _pc(body, name=name, grid=(s // t,), in_specs=[_rows(t, 1), _fixed((1, LANES))],
               out_specs=[_rows(t, LANES)] * 2, out_shape=[SDS((s, LANES), F32)] * 2, sem=("parallel",))(pos_col, inv_freq)


def _pad_heads(chunk, lo_mask):
    zero = jnp.zeros_like(chunk)
    return jnp.where(lo_mask, chunk, zero), jnp.where(lo_mask, pltpu.roll(chunk, 64, 1), zero)


def rope_qkv(name, proj, cos, sin):
    s = proj.shape[0]
    t = _tile(s, ROW_TILE)

    def body(q_ref, kv_ref, c_ref, s_ref, qp_ref, kp_ref, vp_ref):
        cs, sn = c_ref[...], s_ref[...]
        lo_mask = _lane(cs.shape) < 64

        def rot(x):
            return x * cs + _swap_halves(x) * sn

        for j in range(ATTN_WIDTH // LANES):
            a, b = _pad_heads(rot(q_ref[:, j * LANES:(j + 1) * LANES]), lo_mask)
            qp_ref[:, (2 * j) * LANES:(2 * j + 1) * LANES] = a.astype(BF16)
            qp_ref[:, (2 * j + 1) * LANES:(2 * j + 2) * LANES] = b.astype(BF16)
        for j in range(KV_WIDTH // LANES):
            a, b = _pad_heads(rot(kv_ref[:, j * LANES:(j + 1) * LANES]), lo_mask)
            kp_ref[:, (2 * j) * LANES:(2 * j + 1) * LANES] = a.astype(BF16)
            kp_ref[:, (2 * j + 1) * LANES:(2 * j + 2) * LANES] = b.astype(BF16)
            a, b = _pad_heads(kv_ref[:, KV_WIDTH + j * LANES:KV_WIDTH + (j + 1) * LANES], lo_mask)
            vp_ref[:, (2 * j) * LANES:(2 * j + 1) * LANES] = a.astype(BF16)
            vp_ref[:, (2 * j + 1) * LANES:(2 * j + 2) * LANES] = b.astype(BF16)

    return _pc(body, name=name, grid=(s // t,),
               in_specs=[_rows(t, ATTN_WIDTH, 0), _rows(t, 2 * KV_WIDTH, 2), _rows(t, LANES), _rows(t, LANES)],
               out_specs=[_rows(t, N_Q_HEADS * LANES), _rows(t, N_KV_HEADS * LANES), _rows(t, N_KV_HEADS * LANES)],
               out_shape=[SDS((s, N_Q_HEADS * LANES), BF16), SDS((s, N_KV_HEADS * LANES), BF16),
                          SDS((s, N_KV_HEADS * LANES), BF16)],
               sem=("parallel",))(proj, proj, cos, sin)


def _attn_mask(n):
    L = WINDOW
    qi = lax.broadcasted_iota(jnp.int32, (L, 2 * L), 0) + L
    ki = lax.broadcasted_iota(jnp.int32, (L, 2 * L), 1)
    rel = qi - ki
    return (rel >= 0) & (rel < WINDOW) & ((n > 0) | (ki >= L))


def _attn_probs(qh, kk, valid, sink):
    sc = lax.dot_general(qh, kk, _NT, preferred_element_type=F32) * 0.125
    sc = jnp.where(valid, sc, MASK_VALUE)
    m = jnp.maximum(jnp.max(sc, axis=-1, keepdims=True), sink)
    e = jnp.exp(sc - m)
    es = jnp.exp(sink - m)
    den = jnp.sum(e, axis=-1, keepdims=True) + es
    return e / den, es / den


def _attn_specs(s):
    L = WINDOW
    cur = lambda n: (n, 0)
    prev = lambda n: (jnp.maximum(n - 1, 0), 0)
    kvw = N_KV_HEADS * LANES
    return [pl.BlockSpec((L, N_Q_HEADS * LANES), cur), pl.BlockSpec((L, kvw), cur), pl.BlockSpec((L, kvw), prev),
            pl.BlockSpec((L, kvw), cur), pl.BlockSpec((L, kvw), prev), pl.BlockSpec(memory_space=pltpu.SMEM)]


def attn_fwd(name, qp, kp, vp, sinks):
    s = qp.shape[0]
    L = WINDOW

    def body(q_ref, kc_ref, kp_ref, vc_ref, vp_ref, sk_ref, o_ref):
        valid = _attn_mask(pl.program_id(0))
        outs = []
        for kvh in range(N_KV_HEADS):
            cols = slice(kvh * LANES, (kvh + 1) * LANES)
            kk = jnp.concatenate([kp_ref[:, cols], kc_ref[:, cols]], axis=0)
            vv = jnp.concatenate([vp_ref[:, cols], vc_ref[:, cols]], axis=0)
            for g in range(Q_PER_KV):
                h = kvh * Q_PER_KV + g
                p, _ = _attn_probs(q_ref[:, h * LANES:(h + 1) * LANES], kk, valid, sk_ref[h])
                outs.append(jnp.dot(p.astype(BF16), vv, preferred_element_type=F32))
        for j in range(ATTN_WIDTH // LANES):
            o_ref[:, j * LANES:(j + 1) * LANES] = outs[2 * j] + pltpu.roll(outs[2 * j + 1], 64, 1)

    return _pc(body, name=name, grid=(s // L,), in_specs=_attn_specs(s),
               out_specs=pl.BlockSpec((L, ATTN_WIDTH), lambda n: (n, 0)),
               out_shape=SDS((s, ATTN_WIDTH), F32), sem=("parallel",))(qp, kp, kp, vp, vp, sinks)


def attn_bwd(name, qp, kp, vp, sinks, dattn):
    s = qp.shape[0]
    L = WINDOW
    kvw = N_KV_HEADS * LANES

    def body(q_ref, kc_ref, kp_ref, vc_ref, vp_ref, sk_ref, do_ref, dq_ref, dkc_ref, dkp_ref, dvc_ref, dvp_ref, ds_ref):
        n = pl.program_id(0)
        valid = _attn_mask(n)
        lo_mask = _lane((L, LANES)) < 64
        lane1 = _lane((1, LANES))
        dsink = jnp.zeros((1, LANES), F32)
        for kvh in range(N_KV_HEADS):
            cols = slice(kvh * LANES, (kvh + 1) * LANES)
            kk = jnp.concatenate([kp_ref[:, cols], kc_ref[:, cols]], axis=0)
            vv = jnp.concatenate([vp_ref[:, cols], vc_ref[:, cols]], axis=0)
            dkk = jnp.zeros((2 * L, LANES), F32)
            dvv = jnp.zeros((2 * L, LANES), F32)
            for g in range(Q_PER_KV):
                h = kvh * Q_PER_KV + g
                qh = q_ref[:, h * LANES:(h + 1) * LANES]
                p, ps = _attn_probs(qh, kk, valid, sk_ref[h])
                chunk = do_ref[:, (h // 2) * LANES:(h // 2 + 1) * LANES]
                if h % 2:
                    chunk = pltpu.roll(chunk, 64, 1)
                doh = jnp.where(lo_mask, chunk, 0.0).astype(BF16)
                dp = lax.dot_general(doh, vv, _NT, preferred_element_type=F32)
                delta = jnp.sum(p * dp, axis=-1, keepdims=True)
                dsb = ((p * (dp - delta)) * 0.125).astype(BF16)
                dsink = dsink + jnp.where(lane1 == h, -jnp.sum(ps * delta, axis=0, keepdims=True), 0.0)
                dq_ref[:, h * LANES:(h + 1) * LANES] = jnp.dot(dsb, kk, preferred_element_type=F32)
                dkk = dkk + lax.dot_general(dsb, qh, _TN, preferred_element_type=F32)
                dvv = dvv + lax.dot_general(p.astype(BF16), doh, _TN, preferred_element_type=F32)
            dkp_ref[:, cols] = dkk[:L]
            dkc_ref[:, cols] = dkk[L:]
            dvp_ref[:, cols] = dvv[:L]
            dvc_ref[:, cols] = dvv[L:]

        @pl.when(n == 0)
        def _():
            ds_ref[...] = dsink

        @pl.when(n > 0)
        def _():
            ds_ref[...] += dsink

    blk = lambda w: pl.BlockSpec((L, w), lambda n: (n, 0))
    return _pc(body, name=name, grid=(s // L,), in_specs=_attn_specs(s) + [blk(ATTN_WIDTH)],
               out_specs=[blk(N_Q_HEADS * LANES), blk(kvw), blk(kvw), blk(kvw), blk(kvw), _fixed((1, LANES))],
               out_shape=[SDS((s, N_Q_HEADS * LANES), F32)] + [SDS((s, kvw), F32)] * 4 + [SDS((1, LANES), F32)],
               sem=("arbitrary",))(qp, kp, kp, vp, vp, sinks, dattn)


def rope_bwd(name, dqp, dkc, dkp, dvc, dvp, cos, sin):
    s = dqp.shape[0]
    L = WINDOW
    nb = s // L
    kvw = N_KV_HEADS * LANES

    def body(dq_ref, dkc_ref, dkp_ref, dvc_ref, dvp_ref, c_ref, s_ref, o_ref):
        cs, sn = c_ref[...], s_ref[...]
        more = (pl.program_id(0) < nb - 1).astype(F32)

        def unrot(x):
            return x * cs - _swap_halves(x) * sn

        def compact(ref, j, nxt=None):
            a = ref[:, (2 * j) * LANES:(2 * j + 1) * LANES]
            b = ref[:, (2 * j + 1) * LANES:(2 * j + 2) * LANES]
            if nxt is not None:
                a = a + more * nxt[:, (2 * j) * LANES:(2 * j + 1) * LANES]
                b = b + more * nxt[:, (2 * j + 1) * LANES:(2 * j + 2) * LANES]
            return a + pltpu.roll(b, 64, 1)

        for j in range(ATTN_WIDTH // LANES):
            o_ref[:, j * LANES:(j + 1) * LANES] = unrot(compact(dq_ref, j)).astype(BF16)
        for j in range(KV_WIDTH // LANES):
            o_ref[:, (COL_K + j) * LANES:(COL_K + j + 1) * LANES] = unrot(compact(dkc_ref, j, dkp_ref)).astype(BF16)
            o_ref[:, (COL_V + j) * LANES:(COL_V + j + 1) * LANES] = compact(dvc_ref, j, dvp_ref).astype(BF16)

    cur = lambda n: (n, 0)
    nxt = lambda n: (jnp.minimum(n + 1, nb - 1), 0)
    return _pc(body, name=name, grid=(nb,),
               in_specs=[pl.BlockSpec((L, N_Q_HEADS * LANES), cur), pl.BlockSpec((L, kvw), cur), pl.BlockSpec((L, kvw), nxt),
                         pl.BlockSpec((L, kvw), cur), pl.BlockSpec((L, kvw), nxt), pl.BlockSpec((L, LANES), cur),
                         pl.BlockSpec((L, LANES), cur)],
               out_specs=pl.BlockSpec((L, COL_HQ * LANES), cur), out_shape=SDS((s, COL_HQ * LANES), BF16),
               sem=("parallel",))(dqp, dkc, dkp, dvc, dvp, cos, sin)


def _split3(x):
    a = x.astype(BF16)
    r = x - a.astype(F32)
    b = r.astype(BF16)
    c = (r - b.astype(F32)).astype(BF16)
    return a, b, c


def _chunk_sum(x, upper):
    t = x.shape[0]
    ri = lax.broadcasted_iota(jnp.int32, (t, t), 0)
    ci = lax.broadcasted_iota(jnp.int32, (t, t), 1)
    same = (ri // HGRN_CHUNK) == (ci // HGRN_CHUNK)
    tri = (ci >= ri) if upper else (ci <= ri)
    m = jnp.where(same & tri, 1.0, 0.0).astype(BF16)
    out = None
    for part in _split3(x):
        y = jnp.dot(m, part, preferred_element_type=F32)
        out = y if out is None else out + y
    return out


def _lower_bound(l_ref, layer):
    lv = l_ref[...]
    e = jnp.exp(lv - jnp.max(lv, axis=0, keepdims=True))
    sm = e / jnp.sum(e, axis=0, keepdims=True)
    s0 = sm[0:1]
    return (s0 - s0) if layer == 0 else ((s0 + sm[1:2]) - s0)


def _hgrn_gates(hq_ref, hf_ref, lb):
    z = hf_ref[...]
    sg = _sigmoid(z)
    f = lb + (1.0 - lb) * sg
    kin = (1.0 - lb) * _sigmoid(-z)
    hq = hq_ref[...]
    sq = _sigmoid(hq)
    return sg, f, kin, hq, sq


def _shift_down(x, d):
    return x if d == 0 else pltpu.roll(x, d, 0)


def _shift_up(x, d):
    return x if d == 0 else pltpu.roll(x, x.shape[0] - d, 0)


CHUNKS_PER_BLOCK = LANES // HGRN_CHUNK


def _chunk_iotas():
    shape = (HGRN_CHUNK, LANES)
    return lax.broadcasted_iota(jnp.int32, shape, 0), lax.broadcasted_iota(jnp.int32, shape, 1)


def _chunk_rows(block, chunk):
    start = block * LANES + chunk * HGRN_CHUNK
    return slice(start, start + HGRN_CHUNK)


HGRN_HEADS_PER_STEP = 2
HGRN_STEP_WIDTH = HGRN_HEADS_PER_STEP * LANES


def _hgrn_specs(t, rev, nt):
    row = (lambda h, i: nt - 1 - i) if rev else (lambda h, i: i)
    col = lambda base: pl.BlockSpec((t, HGRN_STEP_WIDTH),
                                    lambda h, i, base=base: (row(h, i), base // HGRN_HEADS_PER_STEP + h))
    return col, row


def _head_views(refs, hh):
    return [r.at[:, pl.ds(hh * LANES, LANES)] for r in refs]


def hgrn_fwd(name, proj, lb_logits, layer):
    s = proj.shape[0]
    t = _tile(s, ROW_TILE)
    nt = s // t
    nc = t // HGRN_CHUNK
    col, row = _hgrn_specs(t, False, nt)

    def body(hq_ref, hf_ref, hi_ref, l_ref, o_ref, st_ref, state):
        @pl.when(pl.program_id(1) == 0)
        def _():
            state[...] = jnp.zeros_like(state)

        for hh in range(HGRN_HEADS_PER_STEP):
            head(*_head_views((hq_ref, hf_ref, hi_ref, l_ref, o_ref), hh), st_ref.at[:, hh], state.at[hh])

    def head(hq_ref, hf_ref, hi_ref, l_ref, o_ref, st_ref, state):
        lb = _lower_bound(l_ref, layer)
        sg, f, kin, hq, sq = _hgrn_gates(hq_ref, hf_ref, lb)
        q = hq * sq
        vb = hi_ref[...].astype(BF16)
        b = _chunk_sum(jnp.log(f), False)
        qe = (q * jnp.exp(b)).astype(BF16)
        trow, lane = _chunk_iotas()
        st = state[...]
        for j in range(t // LANES):
            blk = slice(j * LANES, (j + 1) * LANES)
            rows = []
            for cc in range(CHUNKS_PER_BLOCK):
                rs = _chunk_rows(j, cc)
                bc, qc, kc = b[rs], q[rs], kin[rs]
                here = trow + cc * HGRN_CHUNK
                am = jnp.where(lane == here, jnp.sum(qc * kc, axis=-1, keepdims=True), 0.0)
                for d in range(1, HGRN_CHUNK):
                    e = jnp.exp(jnp.where(trow >= d, bc - _shift_down(bc, d), MASK_VALUE))
                    a = jnp.sum((qc * _shift_down(kc, d)) * e, axis=-1, keepdims=True)
                    am = jnp.where(lane == here - d, a, am)
                rows.append(am)
            o_intra = jnp.dot(jnp.concatenate(rows, axis=0).astype(BF16), vb[blk], preferred_element_type=F32)
            for cc in range(CHUNKS_PER_BLOCK):
                rs = _chunk_rows(j, cc)
                bc = b[rs]
                bl = bc[HGRN_CHUNK - 1:HGRN_CHUNK, :]
                ke = (kin[rs] * jnp.exp(bl - bc)).astype(BF16)
                st_ref[j * CHUNKS_PER_BLOCK + cc] = st
                o_ref[rs, :] = (o_intra[cc * HGRN_CHUNK:(cc + 1) * HGRN_CHUNK]
                                + lax.dot_general(qe[rs], st.astype(BF16), _NT, preferred_element_type=F32))
                st = st * jnp.exp(bl) + lax.dot_general(vb[rs], ke, _TN, preferred_element_type=F32)
        state[...] = st

    hp = HGRN_HEADS_PER_STEP
    return _pc(body, name=name, grid=(HGRN_HEADS // hp, nt),
               in_specs=[col(COL_HQ), col(COL_HF), col(COL_HI), pl.BlockSpec((2, HGRN_STEP_WIDTH), lambda h, i: (0, h))],
               out_specs=[pl.BlockSpec((t, HGRN_STEP_WIDTH), lambda h, i: (i, h)),
                          pl.BlockSpec((nc, hp, LANES, LANES), lambda h, i: (i, h, 0, 0))],
               out_shape=[SDS((s, HGRN_WIDTH), F32), SDS((s // HGRN_CHUNK, HGRN_HEADS, LANES, LANES), F32)],
               scratch=[pltpu.VMEM((hp, LANES, LANES), F32)],
               sem=("parallel", "arbitrary"))(proj, proj, proj, lb_logits)


def hgrn_bwd(name, proj, lb_logits, layer, states, do):
    s = proj.shape[0]
    t = _tile(s, ROW_TILE)
    nt = s // t
    nc = t // HGRN_CHUNK
    col, row = _hgrn_specs(t, True, nt)

    def body(hq_ref, hf_ref, hi_ref, l_ref, st_ref, do_ref, dhq_ref, dhf_ref, dhi_ref, dlb_ref, dstate):
        @pl.when(pl.program_id(1) == 0)
        def _():
            dstate[...] = jnp.zeros_like(dstate)

        for hh in range(HGRN_HEADS_PER_STEP):
            hq_v, hf_v, hi_v, l_v, do_v, dhq_v, dhf_v, dhi_v, dlb_v = _head_views(
                (hq_ref, hf_ref, hi_ref, l_ref, do_ref, dhq_ref, dhf_ref, dhi_ref, dlb_ref), hh)
            head(hq_v, hf_v, hi_v, l_v, st_ref.at[:, hh], do_v, dhq_v, dhf_v, dhi_v, dlb_v, dstate.at[hh])

    def head(hq_ref, hf_ref, hi_ref, l_ref, st_ref, do_ref, dhq_ref, dhf_ref, dhi_ref, dlb_ref, dstate):
        first = pl.program_id(1) == 0
        lb = _lower_bound(l_ref, layer)
        sg, f, kin, hq, sq = _hgrn_gates(hq_ref, hf_ref, lb)
        q = hq * sq
        vb = hi_ref[...].astype(BF16)
        b = _chunk_sum(jnp.log(f), False)
        dob = do_ref[...].astype(BF16)
        eb = jnp.exp(b)
        qe = q * eb
        qeb = qe.astype(BF16)
        trow, lane = _chunk_iotas()
        last_row = trow == HGRN_CHUNK - 1

        dn = dstate[...]
        dq_c, dk_c, dv_c, db_c = [None] * nc, [None] * nc, [None] * nc, [None] * nc
        for j in reversed(range(t // LANES)):
            blk = slice(j * LANES, (j + 1) * LANES)
            damat = lax.dot_general(dob[blk], vb[blk], _NT, preferred_element_type=F32)
            rows = [None] * CHUNKS_PER_BLOCK
            for cc in reversed(range(CHUNKS_PER_BLOCK)):
                c = j * CHUNKS_PER_BLOCK + cc
                rs = _chunk_rows(j, cc)
                bc, qc, kc = b[rs], q[rs], kin[rs]
                bl = bc[HGRN_CHUNK - 1:HGRN_CHUNK, :]
                ebl = jnp.exp(bl)
                ekb = jnp.exp(bl - bc)
                ke = kc * ekb
                st = st_ref[c]
                dnb = dn.astype(BF16)
                dqe = jnp.dot(dob[rs], st.astype(BF16), preferred_element_type=F32)
                dke = jnp.dot(vb[rs], dnb, preferred_element_type=F32)
                dv_c[c] = lax.dot_general(ke.astype(BF16), dnb, _NT, preferred_element_type=F32)
                dbl = jnp.sum(dn * st, axis=0, keepdims=True) * ebl + jnp.sum(dke * ke, axis=0, keepdims=True)
                dn = dn * ebl + lax.dot_general(dob[rs], qeb[rs], _TN, preferred_element_type=F32)

                dam = damat[cc * HGRN_CHUNK:(cc + 1) * HGRN_CHUNK]
                here = trow + cc * HGRN_CHUNK
                on = lane == here
                da = jnp.sum(jnp.where(on, dam, 0.0), axis=-1, keepdims=True)
                am = jnp.where(on, jnp.sum(qc * kc, axis=-1, keepdims=True), 0.0)
                dq = dqe * eb[rs] + da * kc
                dk = dke * ekb + da * qc
                db = (dqe * qe[rs] - dke * ke) + jnp.where(last_row, dbl, 0.0)
                for d in range(1, HGRN_CHUNK):
                    on = lane == here - d
                    ks = _shift_down(kc, d)
                    e = jnp.exp(jnp.where(trow >= d, bc - _shift_down(bc, d), MASK_VALUE))
                    qd = qc * e
                    w = qd * ks
                    am = jnp.where(on, jnp.sum(w, axis=-1, keepdims=True), am)
                    da = jnp.sum(jnp.where(on, dam, 0.0), axis=-1, keepdims=True)
                    dq = dq + da * (ks * e)
                    daw = da * w
                    db = db + daw - _shift_up(daw, d)
                    dk = dk + _shift_up(da * qd, d)
                rows[cc] = am
                dq_c[c], dk_c[c], db_c[c] = dq, dk, db
            dv_blk = lax.dot_general(jnp.concatenate(rows, axis=0).astype(BF16), dob[blk], _TN, preferred_element_type=F32)
            for cc in range(CHUNKS_PER_BLOCK):
                c = j * CHUNKS_PER_BLOCK + cc
                dv_c[c] = dv_c[c] + dv_blk[cc * HGRN_CHUNK:(cc + 1) * HGRN_CHUNK]
        dstate[...] = dn
        dq = jnp.concatenate(dq_c, axis=0)
        dk = jnp.concatenate(dk_c, axis=0)
        dv = jnp.concatenate(dv_c, axis=0)
        db = jnp.concatenate(db_c, axis=0)
        dg = _chunk_sum(db, True)
        dhq_ref[...] = (dq * (sq * (1.0 + hq * (1.0 - sq)))).astype(BF16)
        dhi_ref[...] = dv.astype(BF16)
        dfk = dg / f - dk
        dhf_ref[...] = ((dfk * (1.0 - lb)) * (sg * (1.0 - sg))).astype(BF16)
        part = jnp.sum(dfk * (1.0 - sg), axis=0, keepdims=True)

        @pl.when(first)
        def _():
            dlb_ref[...] = part

        @pl.when(jnp.logical_not(first))
        def _():
            dlb_ref[...] += part

    hp = HGRN_HEADS_PER_STEP
    out_col = pl.BlockSpec((t, HGRN_STEP_WIDTH), lambda h, i: (nt - 1 - i, h))
    return _pc(body, name=name, grid=(HGRN_HEADS // hp, nt),
               in_specs=[col(COL_HQ), col(COL_HF), col(COL_HI), pl.BlockSpec((2, HGRN_STEP_WIDTH), lambda h, i: (0, h)),
                         pl.BlockSpec((nc, hp, LANES, LANES), lambda h, i: (nt - 1 - i, h, 0, 0)), out_col],
               out_specs=[out_col, out_col, out_col, pl.BlockSpec((1, HGRN_STEP_WIDTH), lambda h, i: (0, h))],
               out_shape=[SDS((s, HGRN_WIDTH), BF16)] * 3 + [SDS((1, HGRN_WIDTH), F32)],
               scratch=[pltpu.VMEM((hp, LANES, LANES), F32)],
               sem=("parallel", "arbitrary"))(proj, proj, proj, lb_logits, states, do)


def mix_out_fwd(name, attn, o, proj, g_attn, g_hgrn):
    s = attn.shape[0]
    t = _tile(s, ROW_TILE)
    half = HGRN_WIDTH // 2

    def body(a_ref, o_ref, hg0_ref, hg1_ref, ga_ref, gh_ref, c_ref):
        av = a_ref[...]
        c_ref[:, :ATTN_WIDTH] = ((av * _rstd(av)) * ga_ref[...]).astype(BF16)
        for j in range(HGRN_HEADS):
            cols = slice(j * LANES, (j + 1) * LANES)
            ov = o_ref[:, cols]
            hg_ref, hcols = (hg0_ref, cols) if j < 4 else (hg1_ref, slice((j - 4) * LANES, (j - 3) * LANES))
            hg = hg_ref[:, hcols]
            on = (ov * _rstd(ov)) * gh_ref[:, cols]
            c_ref[:, ATTN_WIDTH + j * LANES:ATTN_WIDTH + (j + 1) * LANES] = (on * (hg * _sigmoid(hg))).astype(BF16)

    return _pc(body, name=name, grid=(s // t,),
               in_specs=[_rows(t, ATTN_WIDTH), _rows(t, HGRN_WIDTH), _rows(t, half, COL_HG // 4), _rows(t, half, COL_HG // 4 + 1),
                         _fixed((1, ATTN_WIDTH)), _fixed((1, HGRN_WIDTH))],
               out_specs=_rows(t, D_MODEL), out_shape=SDS((s, D_MODEL), BF16), sem=("parallel",))(attn, o, proj, proj, g_attn, g_hgrn)


def mix_out_bwd(name, dcat, attn, o, proj, g_attn, g_hgrn):
    s = attn.shape[0]
    t = _tile(s, ROW_TILE)
    half = HGRN_WIDTH // 2

    def body(dc_ref, a_ref, o_ref, hg0_ref, hg1_ref, ga_ref, gh_ref, da_ref, do_ref, dhg_ref, dga_ref, dgh_ref, pa_s, ph_s):
        av = a_ref[...]
        r = _rstd(av)
        xh = av * r
        dyv = dc_ref[:, :ATTN_WIDTH]
        dyg = dyv * ga_ref[...]
        da_ref[...] = r * (dyg - xh * jnp.mean(dyg * xh, axis=-1, keepdims=True))
        pa_s[...] = jnp.sum(dyv * xh, axis=0, keepdims=True)
        for j in range(HGRN_HEADS):
            cols = slice(j * LANES, (j + 1) * LANES)
            ov = o_ref[:, cols]
            hg_ref, hcols = (hg0_ref, cols) if j < 4 else (hg1_ref, slice((j - 4) * LANES, (j - 3) * LANES))
            hg = hg_ref[:, hcols]
            sg = _sigmoid(hg)
            r = _rstd(ov)
            xh = ov * r
            gain = gh_ref[:, cols]
            dh = dc_ref[:, ATTN_WIDTH + j * LANES:ATTN_WIDTH + (j + 1) * LANES]
            dhg_ref[:, cols] = ((dh * (xh * gain)) * (sg * (1.0 + hg * (1.0 - sg)))).astype(BF16)
            dyv = dh * (hg * sg)
            dyg = dyv * gain
            do_ref[:, cols] = r * (dyg - xh * jnp.mean(dyg * xh, axis=-1, keepdims=True))
            ph_s[:, cols] = jnp.sum(dyv * xh, axis=0, keepdims=True)

        @pl.when(pl.program_id(0) == 0)
        def _():
            dga_ref[...] = pa_s[...]
            dgh_ref[...] = ph_s[...]

        @pl.when(pl.program_id(0) > 0)
        def _():
            dga_ref[...] += pa_s[...]
            dgh_ref[...] += ph_s[...]

    return _pc(body, name=name, grid=(s // t,),
               in_specs=[_rows(t, D_MODEL), _rows(t, ATTN_WIDTH), _rows(t, HGRN_WIDTH), _rows(t, half, COL_HG // 4),
                         _rows(t, half, COL_HG // 4 + 1), _fixed((1, ATTN_WIDTH)), _fixed((1, HGRN_WIDTH))],
               out_specs=[_rows(t, ATTN_WIDTH), _rows(t, HGRN_WIDTH), _rows(t, HGRN_WIDTH), _fixed((1, ATTN_WIDTH)),
                          _fixed((1, HGRN_WIDTH))],
               out_shape=[SDS((s, ATTN_WIDTH), F32), SDS((s, HGRN_WIDTH), F32), SDS((s, HGRN_WIDTH), BF16),
                          SDS((1, ATTN_WIDTH), F32), SDS((1, HGRN_WIDTH), F32)],
               scratch=[pltpu.VMEM((1, ATTN_WIDTH), F32), pltpu.VMEM((1, HGRN_WIDTH), F32)],
               sem=("arbitrary",))(dcat, attn, o, proj, proj, g_attn, g_hgrn)


BIG = (("w_in", 2048, 1408, 0), ("w_out", 512, 2048, 1), ("w_ffn_gate", 2048, 1408, 0), ("w_ffn_up", 2048, 1408, 0),
       ("w_ffn_down", 1408, 2048, 1), ("w_ple_gate", 512, 2048, 1), ("w_ple_proj", 256, 512, 0))
BIG_BY_NAME = {spec[0]: spec for spec in BIG}
HBM_SPEC = pl.BlockSpec(memory_space=pltpu.HBM)
SEM_SPEC = pl.BlockSpec(memory_space=pltpu.SEMAPHORE)
TOKEN_SHAPE = (8, LANES)


def _split_call(body, *, name, in_specs, out_specs, out_shape, aliases):
    return pl.pallas_call(body, name=name, in_specs=in_specs, out_specs=out_specs, out_shape=out_shape,
                          input_output_aliases=aliases,
                          compiler_params=pltpu.CompilerParams(has_side_effects=pltpu.SideEffectType.DATAFLOW_SIDE_EFFECTING))


def _in_hbm(arrays):
    return [pltpu.with_memory_space_constraint(a, pltpu.HBM) for a in arrays]


def cast_to_slot(name, place, w, layer, after):
    _, r, c = w.shape
    tr = _tile(r, 512)

    def body(place_ref, w_ref, after_ref, o_ref):
        o_ref[...] = w_ref[...].astype(BF16)

    gs = pltpu.PrefetchScalarGridSpec(
        num_scalar_prefetch=1, grid=(r // tr,),
        in_specs=[pl.BlockSpec((None, tr, c), lambda i, pr: (layer, i, 0)), pl.BlockSpec(memory_space=pl.ANY)],
        out_specs=pl.BlockSpec((None, tr, c), lambda i, pr: (pr[1], i, 0)))
    return _pc(body, name=name, grid_spec=gs, in_specs=None, out_specs=None, out_shape=SDS((N_CHIPS, r, c), BF16),
               sem=("parallel",))(place, w, after)


def _place():
    x, y, c = lax.axis_index("x"), lax.axis_index("y"), lax.axis_index("c")
    chips = [(1 - x, y), (x, 1 - y), (1 - x, 1 - y)]
    return x, y, c, chips


def _half(ref, axis, c, rows, cols):
    if axis == 0:
        return ref.at[pl.ds(pl.multiple_of(c * (rows // 2), 16), rows // 2), :]
    return ref.at[:, pl.ds(pl.multiple_of(c * (cols // 2), LANES), cols // 2)]


def _gather_copies(specs, bufs, send, recv):
    x, y, c, chips = _place()
    cps = []
    for t, (_, rows, cols, axis) in enumerate(specs):
        mine = _half(bufs[t].at[2 * x + y], axis, c, rows, cols)
        for k, (cx, cy) in enumerate(chips):
            cps.append(pltpu.make_async_remote_copy(src_ref=mine, dst_ref=mine, send_sem=send.at[3 * t + k],
                                                    recv_sem=recv.at[3 * t + k], device_id=(cx, cy, c), device_id_type=MESH))
    return cps


def gather_start(name, specs, bufs):
    nt = len(bufs)
    n = 3 * nt

    def body(*refs):
        send, recv, token = refs[nt], refs[nt + 1], refs[-1]
        for cp in _gather_copies(specs, refs[:nt], send, recv):
            cp.start()
        token[...] = jnp.zeros(TOKEN_SHAPE, F32)

    out = _split_call(
        body, name=name, in_specs=[HBM_SPEC] * nt,
        out_specs=(SEM_SPEC, SEM_SPEC) + (HBM_SPEC,) * nt + (pl.BlockSpec(memory_space=pltpu.VMEM),),
        out_shape=(pltpu.SemaphoreType.DMA((n,)), pltpu.SemaphoreType.DMA((n,)))
        + tuple(pltpu.HBM(b.shape, b.dtype) for b in bufs) + (SDS(TOKEN_SHAPE, F32),),
        aliases={t: 2 + t for t in range(nt)})(*_in_hbm(bufs))
    return out[0], out[1], list(out[2:2 + nt]), out[-1]


def gather_wait(name, specs, send, recv, bufs, after):
    nt = len(bufs)

    def body(*refs):
        for cp in _gather_copies(specs, refs[:nt], refs[nt], refs[nt + 1]):
            cp.wait_send()
            cp.wait_recv()

    out = _split_call(
        body, name=name, in_specs=[HBM_SPEC] * nt + [SEM_SPEC, SEM_SPEC, pl.BlockSpec(memory_space=pl.ANY)],
        out_specs=(HBM_SPEC,) * nt, out_shape=tuple(pltpu.HBM(b.shape, b.dtype) for b in bufs),
        aliases={t: t for t in range(nt)})(*bufs, send, recv, after)
    return list(out)


def gather_pass(name, specs, bufs):
    nt = len(bufs)

    def body(*refs):
        ins, outs = refs[:nt], refs[nt:2 * nt]
        send, recv = refs[2 * nt:]
        x, y, c, chips = _place()
        cps = []
        for t, (_, rows, cols, axis) in enumerate(specs):
            for k, (cx, cy) in enumerate(chips):
                cp = pltpu.make_async_remote_copy(
                    src_ref=_half(ins[t].at[2 * cx + cy], axis, c, rows, cols),
                    dst_ref=_half(outs[t].at[2 * cx + cy], axis, c, rows, cols),
                    send_sem=send.at[3 * t + k], recv_sem=recv.at[3 * t + k], device_id=(x, y, 1 - c), device_id_type=MESH)
                cp.start()
                cps.append(cp)
        for t, (_, rows, cols, axis) in enumerate(specs):
            for k, (cx, cy) in enumerate(chips):
                theirs = _half(outs[t].at[2 * cx + cy], axis, 1 - c, rows, cols)
                pltpu.make_async_remote_copy(src_ref=theirs, dst_ref=theirs, send_sem=send.at[3 * t + k],
                                             recv_sem=recv.at[3 * t + k], device_id=(x, y, 1 - c), device_id_type=MESH).wait_recv()
        for cp in cps:
            cp.wait_send()

    return _pc(body, name=name, in_specs=[HBM_SPEC] * nt, out_specs=[HBM_SPEC] * nt,
               out_shape=[SDS(b.shape, b.dtype) for b in bufs], scratch=[pltpu.SemaphoreType.DMA((3 * nt,))] * 2,
               input_output_aliases={t: t for t in range(nt)})(*bufs)


def reduce_to_sibling(name, grads):
    nt = len(grads)

    def body(*refs):
        srcs, dsts = refs[:nt], refs[nt:2 * nt]
        send, recv = refs[2 * nt:]
        x, y, c, _ = _place()
        cps = []
        for t in range(nt):
            cp = pltpu.make_async_remote_copy(src_ref=srcs[t].at[1 - c], dst_ref=dsts[t], send_sem=send.at[t],
                                              recv_sem=recv.at[t], device_id=(x, y, 1 - c), device_id_type=MESH)
            cp.start()
            cps.append(cp)
        for cp in cps:
            cp.wait()

    return _pc(body, name=name, in_specs=[HBM_SPEC] * nt, out_specs=[HBM_SPEC] * nt,
               out_shape=[SDS(g.shape[1:], g.dtype) for g in grads],
               scratch=[pltpu.SemaphoreType.DMA((nt,))] * 2)(*grads)


def add_halves(name, place, grad, got):
    _, n, r, c = grad.shape
    tr = _tile(r, 512)

    def body(place_ref, g_ref, o_ref, part_ref, slot_ref):
        val = (g_ref[...].astype(F32) + o_ref[...].astype(F32)).astype(BF16)
        part_ref[...] = val

        @pl.when(pl.program_id(1) == place_ref[1])
        def _():
            slot_ref[...] = val

    gs = pltpu.PrefetchScalarGridSpec(
        num_scalar_prefetch=1, grid=(r // tr, n),
        in_specs=[pl.BlockSpec((None, None, tr, c), lambda i, j, pr: (pr[0], j, i, 0)),
                  pl.BlockSpec((None, tr, c), lambda i, j, pr: (j, i, 0))],
        out_specs=[pl.BlockSpec((None, tr, c), lambda i, j, pr: (j, i, 0)),
                   pl.BlockSpec((None, tr, c), lambda i, j, pr: (pr[1], i, 0))])
    return _pc(body, name=name, grid_spec=gs, in_specs=None, out_specs=None, out_shape=[SDS((n, r, c), BF16)] * 2,
               sem=("parallel", "arbitrary"))(place, grad, got)


def _chips_copies(parts, slots, send, recv):
    x, y, c, chips = _place()
    cps = []
    for t in range(len(parts)):
        for k, (cx, cy) in enumerate(chips):
            cps.append(pltpu.make_async_remote_copy(src_ref=parts[t].at[2 * cx + cy], dst_ref=slots[t].at[2 * x + y],
                                                    send_sem=send.at[3 * t + k], recv_sem=recv.at[3 * t + k],
                                                    device_id=(cx, cy, c), device_id_type=MESH))
    return cps


def chips_start(name, parts, slots):
    nt = len(parts)
    n = 3 * nt

    def body(*refs):
        send, recv, token = refs[2 * nt], refs[2 * nt + 1], refs[-1]
        for cp in _chips_copies(refs[:nt], refs[nt:2 * nt], send, recv):
            cp.start()
        token[...] = jnp.zeros(TOKEN_SHAPE, F32)

    both = list(parts) + list(slots)
    out = _split_call(
        body, name=name, in_specs=[HBM_SPEC] * (2 * nt),
        out_specs=(SEM_SPEC, SEM_SPEC) + (HBM_SPEC,) * (2 * nt) + (pl.BlockSpec(memory_space=pltpu.VMEM),),
        out_shape=(pltpu.SemaphoreType.DMA((n,)), pltpu.SemaphoreType.DMA((n,)))
        + tuple(pltpu.HBM(b.shape, b.dtype) for b in both) + (SDS(TOKEN_SHAPE, F32),),
        aliases={t: 2 + t for t in range(2 * nt)})(*_in_hbm(both))
    return out[0], out[1], list(out[2:2 + nt]), list(out[2 + nt:2 + 2 * nt]), out[-1]


def chips_wait(name, send, recv, parts, slots, after):
    nt = len(parts)

    def body(*refs):
        for cp in _chips_copies(refs[:nt], refs[nt:2 * nt], refs[2 * nt], refs[2 * nt + 1]):
            cp.wait_send()
            cp.wait_recv()

    both = list(parts) + list(slots)
    out = _split_call(
        body, name=name, in_specs=[HBM_SPEC] * (2 * nt) + [SEM_SPEC, SEM_SPEC, pl.BlockSpec(memory_space=pl.ANY)],
        out_specs=(HBM_SPEC,) * (2 * nt), out_shape=tuple(pltpu.HBM(b.shape, b.dtype) for b in both),
        aliases={t: t for t in range(2 * nt)})(*both, send, recv, after)
    return list(out[nt:])


def sum_chips(name, place, slots):
    n, r, c = slots.shape
    tr = _tile(r, 512)

    def body(place_ref, s_ref, o_ref):
        acc = s_ref[0].astype(F32)
        for k in range(1, n):
            acc = acc + s_ref[k].astype(F32)
        o_ref[...] = acc

    gs = pltpu.PrefetchScalarGridSpec(
        num_scalar_prefetch=1, grid=(r // tr,),
        in_specs=[pl.BlockSpec((n, tr, c), lambda i, pr: (0, i, 0))],
        out_specs=pl.BlockSpec((None, tr, c), lambda i, pr: (pr[0], i, 0)))
    return _pc(body, name=name, grid_spec=gs, in_specs=None, out_specs=None, out_shape=SDS((2, r, c), F32),
               sem=("parallel",))(place, slots)


def share_with_sibling(name, bufs):
    nt = len(bufs)

    def body(*refs):
        ins, outs = refs[:nt], refs[nt:2 * nt]
        send, recv = refs[2 * nt:]
        x, y, c, _ = _place()
        cps = []
        for t in range(nt):
            cp = pltpu.make_async_remote_copy(src_ref=ins[t].at[c], dst_ref=outs[t].at[c], send_sem=send.at[t], recv_sem=recv.at[t],
                                              device_id=(x, y, 1 - c), device_id_type=MESH)
            cp.start()
            cps.append(cp)
        for t in range(nt):
            theirs = outs[t].at[1 - c]
            pltpu.make_async_remote_copy(src_ref=theirs, dst_ref=theirs, send_sem=send.at[t], recv_sem=recv.at[t],
                                         device_id=(x, y, 1 - c), device_id_type=MESH).wait_recv()
        for cp in cps:
            cp.wait_send()

    return _pc(body, name=name, in_specs=[HBM_SPEC] * nt, out_specs=[HBM_SPEC] * nt,
               out_shape=[SDS(b.shape, F32) for b in bufs], scratch=[pltpu.SemaphoreType.DMA((nt,))] * 2,
               input_output_aliases={t: t for t in range(nt)})(*bufs)


def _adamw(w, g, m, v):
    m = ADAM_B1 * m + (1.0 - ADAM_B1) * g
    v = ADAM_B2 * v + (1.0 - ADAM_B2) * (g * g)
    m_hat = m / (1.0 - ADAM_B1 ** ADAM_STEP)
    v_hat = v / (1.0 - ADAM_B2 ** ADAM_STEP)
    delta = -ADAM_LR * (m_hat / (jnp.sqrt(v_hat) + ADAM_EPS) + ADAM_WD * w)
    return delta, m, v


def adamw_big(name, w, m, v, g0, g1, axis):
    _, r, c = w.shape
    _, rh, ch = g0.shape
    tr = _tile(rh, 256)
    nb = rh // tr
    if axis == 0:
        wspec = pl.BlockSpec((None, tr, ch), lambda l, h, i: (l, h * nb + i, 0))
    else:
        wspec = pl.BlockSpec((None, tr, ch), lambda l, h, i: (l, i, h))
    g0spec = pl.BlockSpec((None, tr, ch), lambda l, h, i: (h * (1 - l), i * (1 - l), 0))
    g1spec = pl.BlockSpec((None, tr, ch), lambda l, h, i: (h * l, i * l, 0))

    def body(w_ref, m_ref, v_ref, g0_ref, g1_ref, go_ref, d_ref, mo_ref, vo_ref):
        def run(g_ref):
            g = g_ref[...]
            delta, mn, vn = _adamw(w_ref[...], g, m_ref[...], v_ref[...])
            go_ref[...] = g
            d_ref[...] = delta
            mo_ref[...] = mn
            vo_ref[...] = vn

        @pl.when(pl.program_id(0) == 0)
        def _():
            run(g0_ref)

        @pl.when(pl.program_id(0) == 1)
        def _():
            run(g1_ref)

    return _pc(body, name=name, grid=(2, 2, nb), in_specs=[wspec, wspec, wspec, g0spec, g1spec], out_specs=[wspec] * 4,
               out_shape=[SDS(w.shape, F32)] * 4, sem=("parallel", "parallel", "parallel"))(w, m, v, g0, g1)


SMALL = (("pre_mix_gain", 2048), ("post_mix_gain", 2048), ("pre_ffn_gain", 2048), ("post_ffn_gain", 2048), ("ple_gain", 2048),
         ("attn_out_gain", 1024), ("hgrn_out_gain", 1024), ("hgrn_lb_logits", 1024), ("attn_sinks", 128))
SMALL_ROWS = sum(2 * w // LANES for _, w in SMALL)
SMALL_PAD = -(-SMALL_ROWS // 8) * 8
LB_ROW = sum(2 * w // LANES for _, w in SMALL[:7])


def _pack_small(parts):
    rows = []
    for nm, w in SMALL:
        a = parts[nm].astype(F32)
        if a.shape[1] != w:
            a = jnp.pad(a, ((0, 0), (0, w - a.shape[1])))
        rows.append(a.reshape(2 * w // LANES, LANES))
    rows.append(jnp.zeros((SMALL_PAD - SMALL_ROWS, LANES), F32))
    return jnp.concatenate(rows, axis=0)


def _unpack_small(packed, widths):
    out, r = {}, 0
    for nm, w in SMALL:
        n = 2 * w // LANES
        out[nm] = packed[r:r + n].reshape(2, w)[:, :widths[nm]]
        r += n
    return out


def allreduce_small(name, packed):
    rows = packed.shape[0]

    def body(x_ref, o_ref, buf, send, recv, own_sem):
        x, y, c, _ = _place()
        me = 4 * x + 2 * y + c
        own = pltpu.make_async_copy(x_ref, buf.at[me], own_sem)
        own.start()
        cps = []
        for k in range(1, 8):
            px, py, pc = x ^ (k >> 2), y ^ ((k >> 1) & 1), c ^ (k & 1)
            cp = pltpu.make_async_remote_copy(src_ref=x_ref, dst_ref=buf.at[me], send_sem=send.at[k - 1], recv_sem=recv.at[k - 1],
                                              device_id=(px, py, pc), device_id_type=MESH)
            cp.start()
            cps.append(cp)
        for k in range(1, 8):
            px, py, pc = x ^ (k >> 2), y ^ ((k >> 1) & 1), c ^ (k & 1)
            slot = buf.at[4 * px + 2 * py + pc]
            pltpu.make_async_remote_copy(src_ref=slot, dst_ref=slot, send_sem=send.at[k - 1], recv_sem=recv.at[k - 1],
                                         device_id=(px, py, pc), device_id_type=MESH).wait_recv()
        for cp in cps:
            cp.wait_send()
        own.wait()
        acc = buf[0]
        for k in range(1, 8):
            acc = acc + buf[k]
        o_ref[...] = acc

    vm = pl.BlockSpec(memory_space=pltpu.VMEM)
    return _pc(body, name=name, in_specs=[vm], out_specs=vm, out_shape=SDS((rows, LANES), F32),
               scratch=[pltpu.VMEM((8, rows, LANES), F32), pltpu.SemaphoreType.DMA((7,)), pltpu.SemaphoreType.DMA((7,)),
                        pltpu.SemaphoreType.DMA])(packed)


def adamw_small(name, w, m, v, g):
    rows = w.shape[0]
    n = HGRN_WIDTH // LANES

    def body(w_ref, m_ref, v_ref, g_ref, go_ref, d_ref, mo_ref, vo_ref):
        go_ref[...] = g_ref[...]
        l0 = w_ref[LB_ROW:LB_ROW + n, :]
        l1 = w_ref[LB_ROW + n:LB_ROW + 2 * n, :]
        mx = jnp.maximum(l0, l1)
        e0, e1 = jnp.exp(l0 - mx), jnp.exp(l1 - mx)
        s0, s1 = e0 / (e0 + e1), e1 / (e0 + e1)
        dlb1 = g_ref[LB_ROW + n:LB_ROW + 2 * n, :]
        inner = s1 * dlb1
        go_ref[LB_ROW:LB_ROW + n, :] = s0 * (0.0 - inner)
        go_ref[LB_ROW + n:LB_ROW + 2 * n, :] = s1 * (dlb1 - inner)
        delta, mn, vn = _adamw(w_ref[...], go_ref[...], m_ref[...], v_ref[...])
        d_ref[...] = delta
        mo_ref[...] = mn
        vo_ref[...] = vn

    vm = pl.BlockSpec(memory_space=pltpu.VMEM)
    return _pc(body, name=name, in_specs=[vm] * 4, out_specs=[vm] * 4, out_shape=[SDS((rows, LANES), F32)] * 4)(w, m, v, g)


def _layer_fwd(l, x, h1, p_l, w_in_g, rest_of_weights, gains, cos, sin, sinks, lb_logits, g_next, target):
    n = f"l{l}_"
    proj = mm_col(n + "in_proj", h1, w_in_g)
    qp, kp, vp = rope_qkv(n + "rope_qkv", proj, cos, sin)
    attn = attn_fwd(n + "attn_fwd", qp, kp, vp, sinks)
    o, states = hgrn_fwd(n + "hgrn_fwd", proj, lb_logits, l)
    cat = mix_out_fwd(n + "mix_out_fwd", attn, o, proj, gains["attn_out_gain"], gains["hgrn_out_gain"])
    rest, token = rest_of_weights(cat)
    wts = dict(rest, w_in=w_in_g)
    if token is not None:
        gains = _with_token(gains, "post_mix_gain", token)
    m = mm_row(n + "out_proj", cat, wts["w_out"])
    x1, h2 = post_pre_norm(n + "post_mix", m, gains["post_mix_gain"], x, gains["pre_ffn_gain"])
    g, u, a = ffn_gate_up(n + "ffn_gate_up", h2, wts["w_ffn_gate"], wts["w_ffn_up"])
    f = mm_row(n + "ffn_down", a, wts["w_ffn_down"])
    x2, h3 = post_pre_norm(n + "post_ffn", f, gains["post_ffn_gain"], x1, gains["ple_gain"])
    z = mm_row(n + "ple_gate", h3, wts["w_ple_gate"])
    pp = mm_col(n + "ple_proj", p_l, wts["w_ple_proj"])
    if target is None:
        out = ple_fwd_mid(n + "ple_fwd", z, pp, x2, g_next)
    else:
        out = ple_fwd_loss(n + "ple_loss", z, pp, x2, target)
    saved = dict(x=x, h1=h1, proj=proj, qp=qp, kp=kp, vp=vp, attn=attn, o=o, states=states, cat=cat, m=m, x1=x1, h2=h2,
                 g=g, u=u, a=a, f=f, x2=x2, h3=h3, z=z, pp=pp, p=p_l)
    return out, saved, wts


EARLY = ("w_ple_gate", "w_ple_proj", "w_ffn_down", "w_ffn_gate", "w_ffn_up")
LATE = ("w_out", "w_in")


def _layer_bwd_ffn(l, dx3, sv, wts, gains):
    n = f"l{l}_"
    dpp, dz = ple_bwd(n + "ple_bwd", dx3, sv["z"], sv["pp"])
    dh3 = mm_row_t(n + "ple_gate_dx", dz, wts["w_ple_gate"])
    dx2, df, d_ple_gain, d_post_ffn = norm_bwd_pair(n + "ple_post_ffn_bwd", sv["x2"], gains["ple_gain"], dh3, dx3, sv["f"],
                                                    gains["post_ffn_gain"])
    dg, du = ffn_down_bwd(n + "ffn_down_bwd", df, wts["w_ffn_down"], sv["g"], sv["u"])
    big = dict(
        w_ple_gate=mm_wg_row(n + "ple_gate_dw", sv["h3"], dz),
        w_ple_proj=mm_wg_col(n + "ple_proj_dw", sv["p"], dpp),
        w_ffn_down=mm_wg_row(n + "ffn_down_dw", sv["a"], df),
        w_ffn_gate=mm_wg_col(n + "ffn_gate_dw", sv["h2"], dg),
        w_ffn_up=mm_wg_col(n + "ffn_up_dw", sv["h2"], du),
    )
    return dict(dx2=dx2, dg=dg, du=du), big, dict(ple_gain=d_ple_gain, post_ffn_gain=d_post_ffn)


def _layer_bwd_mix(l, st, sv, wts, gains, cos, sin, sinks, lb_logits):
    n = f"l{l}_"
    dh2 = mm_col_t(n + "ffn_gate_dx", st["dg"], wts["w_ffn_gate"])
    dh2 = mm_col_t(n + "ffn_up_dx", st["du"], wts["w_ffn_up"], add=dh2)
    dx1, dm, d_pre_ffn, d_post_mix = norm_bwd_pair(n + "pre_ffn_post_mix_bwd", sv["x1"], gains["pre_ffn_gain"], dh2, st["dx2"],
                                                   sv["m"], gains["post_mix_gain"])
    dcat = mm_row_t(n + "out_proj_dx", dm, wts["w_out"])
    dattn, do, dhg, d_attn_gain, d_hgrn_gain = mix_out_bwd(n + "mix_out_bwd", dcat, sv["attn"], sv["o"], sv["proj"],
                                                            gains["attn_out_gain"], gains["hgrn_out_gain"])
    dqp, dkc, dkp, dvc, dvp, dsinks = attn_bwd(n + "attn_bwd", sv["qp"], sv["kp"], sv["vp"], sinks, dattn)
    dqkv = rope_bwd(n + "rope_bwd", dqp, dkc, dkp, dvc, dvp, cos, sin)
    dhq, dhf, dhi, dlb = hgrn_bwd(n + "hgrn_bwd", sv["proj"], lb_logits, l, sv["states"], do)
    dproj = jnp.concatenate([dqkv, dhq, dhf, dhi, dhg], axis=1)
    dh1 = mm_col_t(n + "in_proj_dx", dproj, wts["w_in"])
    dx, d_pre_mix = norm_bwd(n + "pre_mix_bwd", sv["x"], gains["pre_mix_gain"], dh1, dx1)
    big = dict(w_out=mm_wg_row(n + "out_proj_dw", sv["cat"], dm), w_in=mm_wg_col(n + "in_proj_dw", sv["h1"], dproj))
    small = dict(pre_mix_gain=d_pre_mix, post_mix_gain=d_post_mix, pre_ffn_gain=d_pre_ffn, attn_out_gain=d_attn_gain,
                 hgrn_out_gain=d_hgrn_gain, hgrn_lb_logits=dlb, attn_sinks=dsinks)
    return dx, big, small


def _layer_bwd(l, dx3, sv, wts, gains, cos, sin, sinks, lb_logits):
    st, early, small_a = _layer_bwd_ffn(l, dx3, sv, wts, gains)
    dx, late, small_b = _layer_bwd_mix(l, st, sv, wts, gains, cos, sin, sinks, lb_logits)
    return dx, {**early, **late}, {**small_a, **small_b}


def _reduce_start(tag, names, big, place):
    got = reduce_to_sibling(tag + "_reduce_to_sibling", [big[nm] for nm in names])
    pairs = [add_halves(f"{tag}_add_{nm}", place, big[nm], gt) for nm, gt in zip(names, got)]
    return chips_start(tag + "_chips_start", [pr[0] for pr in pairs], [pr[1] for pr in pairs])


def _reduce_finish(tag, names, started, place, after):
    send, recv, parts, slots, _ = started
    slots = chips_wait(tag + "_chips_wait", send, recv, parts, slots, after)
    bufs = [sum_chips(f"{tag}_sum_{nm}", place, sl) for nm, sl in zip(names, slots)]
    return dict(zip(names, share_with_sibling(tag + "_share_with_sibling", bufs)))


def _with_token(gains, name, token):
    out = dict(gains)
    out[name] = gains[name] + token[0, 0]
    return out


def kernel(x, p, positions, w_in, attn_sinks, hgrn_lb_logits, attn_out_gain, hgrn_out_gain, w_out, pre_mix_gain, post_mix_gain, pre_ffn_gain, post_ffn_gain, w_ffn_gate, w_ffn_up, w_ffn_down, ple_gain, w_ple_gate, w_ple_proj, loss_target, m_w_in, m_attn_sinks, m_hgrn_lb_logits, m_attn_out_gain, m_hgrn_out_gain, m_w_out, m_pre_mix_gain, m_post_mix_gain, m_pre_ffn_gain, m_post_ffn_gain, m_w_ffn_gate, m_w_ffn_up, m_w_ffn_down, m_ple_gain, m_w_ple_gate, m_w_ple_proj, v_w_in, v_attn_sinks, v_hgrn_lb_logits, v_attn_out_gain, v_hgrn_out_gain, v_w_out, v_pre_mix_gain, v_post_mix_gain, v_pre_ffn_gain, v_post_ffn_gain, v_w_ffn_gate, v_w_ffn_up, v_w_ffn_down, v_ple_gain, v_w_ple_gate, v_w_ple_proj):
    given = dict(locals())
    depth = 2
    place = jnp.stack([lax.axis_index("c"), 2 * lax.axis_index("x") + lax.axis_index("y")]).astype(jnp.int32)
    xs = x[0]
    tgt = loss_target[0]
    pos_col = positions.reshape(-1, 1)
    half = 32
    inv_freq = ROPE_THETA ** (-jnp.arange(half, dtype=F32) / half)
    inv_freq = jnp.tile(inv_freq, 4).reshape(1, LANES)
    gains = [{nm: given[nm][l:l + 1] for nm, _ in SMALL[:7]} for l in range(depth)]
    names = [nm for nm, *_ in BIG]
    first, others = names[:1], names[1:]

    def specs(nms):
        return [BIG_BY_NAME[nm] for nm in nms]

    def start_gather(tag, l, nms, after):
        return gather_start(tag + "_gather_start", specs(nms),
                            [cast_to_slot(f"{tag}_cast_{nm}", place, given[nm], l, after) for nm in nms])

    def finish_gather(tag, nms, started, after):
        bufs = gather_wait(tag + "_gather_wait", specs(nms), started[0], started[1], started[2], after)
        return dict(zip(nms, gather_pass(tag + "_gather_pass", specs(nms), bufs)))

    g0a = start_gather("l0a", 0, first, place)
    g0b = start_gather("l0b", 0, others, g0a[3])
    started = {}

    def rest_of_layer0(after):
        got = finish_gather("l0b", others, g0b, after)
        started["l1"] = start_gather("l1", 1, names, got["w_out"])
        return got, started["l1"][3]

    cos, sin = rope_tables("rope_tables", pos_col, inv_freq)
    h1 = pre_norm("l0_pre_mix", xs, gains[0]["pre_mix_gain"])
    w_in0 = finish_gather("l0a", first, g0a, g0b[3])["w_in"]
    (x_mid, h1_next), sv0, wts0 = _layer_fwd(0, xs, h1, p[0, 0], w_in0, rest_of_layer0, gains[0], cos, sin, attn_sinks[0],
                                             hgrn_lb_logits, gains[1]["pre_mix_gain"], None)
    wts1 = finish_gather("l1", names, started["l1"], x_mid)
    (dy, loss_part), sv1, _ = _layer_fwd(1, x_mid, h1_next, p[1, 0], wts1["w_in"], lambda after: (wts1, None), gains[1], cos, sin,
                                         attn_sinks[1], hgrn_lb_logits, None, tgt)

    dx_mid, big1, small1 = _layer_bwd(1, dy, sv1, wts1, gains[1], cos, sin, attn_sinks[1], hgrn_lb_logits)
    r1 = _reduce_start("l1", names, big1, place)
    st0, early0, small0 = _layer_bwd_ffn(0, dx_mid, sv0, wts0, _with_token(gains[0], "ple_gain", r1[4]))
    r0e = _reduce_start("l0e", EARLY, early0, place)
    dx0, late0, small0b = _layer_bwd_mix(0, st0, sv0, wts0, _with_token(gains[0], "pre_ffn_gain", r0e[4]), cos, sin,
                                         attn_sinks[0], hgrn_lb_logits)
    small0 = {**small0, **small0b}
    r0l = _reduce_start("l0l", LATE, late0, place)
    red1 = _reduce_finish("l1", names, r1, place, r0l[4])
    red0 = _reduce_finish("l0e", EARLY, r0e, place, red1[names[-1]])

    loss = lax.psum(loss_part[0, 0], ("x", "y", "c"))
    grad_x = dx0[None]

    out_big = {}
    for nm in EARLY:
        out_big[nm] = adamw_big("adamw_" + nm, given[nm], given["m_" + nm], given["v_" + nm], red0[nm], red1[nm], BIG_BY_NAME[nm][3])
    red0.update(_reduce_finish("l0l", LATE, r0l, place, out_big[EARLY[-1]][3]))
    for nm in LATE:
        out_big[nm] = adamw_big("adamw_" + nm, given[nm], given["m_" + nm], given["v_" + nm], red0[nm], red1[nm], BIG_BY_NAME[nm][3])

    widths = {nm: given[nm].shape[1] for nm, _ in SMALL}
    small_g = {nm: jnp.concatenate([small0[nm][:, :widths[nm]] if nm != "attn_sinks" else small0[nm][:, :LANES],
                                    small1[nm][:, :widths[nm]] if nm != "attn_sinks" else small1[nm][:, :LANES]], axis=0)
               for nm, _ in SMALL}
    g_sum = allreduce_small("allreduce_small", _pack_small(small_g))
    sm = adamw_small("adamw_small", _pack_small({nm: given[nm] for nm, _ in SMALL}),
                     _pack_small({nm: given["m_" + nm] for nm, _ in SMALL}),
                     _pack_small({nm: given["v_" + nm] for nm, _ in SMALL}), g_sum)
    out_small = [_unpack_small(a, widths) for a in sm]

    order = ["w_in", "attn_sinks", "hgrn_lb_logits", "attn_out_gain", "hgrn_out_gain", "w_out", "pre_mix_gain", "post_mix_gain",
             "pre_ffn_gain", "post_ffn_gain", "w_ffn_gate", "w_ffn_up", "w_ffn_down", "ple_gain", "w_ple_gate", "w_ple_proj"]
    res = [loss, grad_x]
    for k in range(4):
        for nm in order:
            res.append(out_big[nm][k] if nm in out_big else out_small[k][nm])
    return tuple(res)
```

```python
import functools

import jax
import jax.numpy as jnp
from jax import lax
from jax.experimental import pallas as pl
from jax.experimental.pallas import tpu as pltpu

F32, BF16 = jnp.float32, jnp.bfloat16
SDS = jax.ShapeDtypeStruct
MESH = pl.DeviceIdType.MESH

D_MODEL = 2048
ATTN_WIDTH = 1024
HGRN_WIDTH = 1024
KV_WIDTH = 256
N_Q_HEADS = 16
N_KV_HEADS = 4
Q_PER_KV = 4
WINDOW = 128
MASK_VALUE = -1e30
ROPE_THETA = 10000.0
HGRN_HEADS = 8
HGRN_CHUNK = 16
D_FF = 5632
D_PLE = 256
RMS_EPS = 1e-6
LANES = 128
N_CHIPS = 4
COL_Q, COL_K, COL_V, COL_HQ, COL_HF, COL_HI, COL_HG = 0, 8, 10, 12, 20, 28, 36

ADAM_LR, ADAM_B1, ADAM_B2, ADAM_EPS, ADAM_WD, ADAM_STEP = 0.001, 0.9, 0.999, 1e-08, 0.01, 10

VMEM_LIMIT = 56 * 1024 * 1024
ROW_TILE = 256

_NN = (((1,), (0,)), ((), ()))
_NT = (((1,), (1,)), ((), ()))
_TN = (((0,), (0,)), ((), ()))


def _pc(body, *, name, out_shape, in_specs, out_specs, grid=(), scratch=(), sem=None, grid_spec=None, **kw):
    params = dict(vmem_limit_bytes=VMEM_LIMIT)
    if sem is not None:
        params["dimension_semantics"] = sem
    if grid_spec is not None:
        return pl.pallas_call(body, name=name, out_shape=out_shape, grid_spec=grid_spec,
                              compiler_params=pltpu.CompilerParams(**params), **kw)
    return pl.pallas_call(body, name=name, out_shape=out_shape, grid=grid, in_specs=in_specs, out_specs=out_specs,
                          scratch_shapes=list(scratch), compiler_params=pltpu.CompilerParams(**params), **kw)


def _sigmoid(x):
    return 1.0 / (1.0 + jnp.exp(-x))


def _rstd(x):
    return lax.rsqrt(jnp.mean(x * x, axis=-1, keepdims=True) + RMS_EPS)


def _rows(t, w, col=0):
    return pl.BlockSpec((t, w), lambda i, col=col: (i, col))


def _fixed(shape):
    return pl.BlockSpec(shape, lambda *_: (0,) * len(shape))


def _mm(name, a, b, *, dims, grid, a_spec, b_spec, o_spec, out_shape, parts=1, add=None, add_spec=None):
    def body(*refs):
        a_ref, b_ref, o_ref = refs[0], refs[1], refs[-1]
        if parts == 1:
            r = lax.dot_general(a_ref[...].astype(BF16), b_ref[...].astype(BF16), dims, preferred_element_type=F32)
        else:
            w = a_ref.shape[1] // parts
            r = None
            for j in range(parts):
                t = lax.dot_general(a_ref[:, j * w:(j + 1) * w].astype(BF16), b_ref[j].astype(BF16), dims,
                                    preferred_element_type=F32)
                r = t if r is None else r + t
        if add is not None:
            r = r + refs[2][...]
        o_ref[...] = r.astype(o_ref.dtype)

    ins = [a, b] + ([] if add is None else [add])
    specs = [a_spec, b_spec] + ([] if add is None else [add_spec])
    return _pc(body, name=name, grid=grid, in_specs=specs, out_specs=o_spec, out_shape=out_shape,
               sem=("parallel",) * len(grid))(*ins)


def _tile(n, t):
    if n <= t:
        return n
    while n % t:
        t //= 2
    assert t % 8 == 0
    return t


def mm_col(name, a, wg, out_dtype=F32):
    s, k = a.shape
    _, _, n = wg.shape
    tm = _tile(s, 512)
    return _mm(name, a, wg, dims=_NN, grid=(N_CHIPS, s // tm),
               a_spec=pl.BlockSpec((tm, k), lambda j, i: (i, 0)),
               b_spec=pl.BlockSpec((None, k, n), lambda j, i: (j, 0, 0)),
               o_spec=pl.BlockSpec((tm, n), lambda j, i: (i, j)),
               out_shape=SDS((s, N_CHIPS * n), out_dtype))


def mm_row(name, a, wg, out_dtype=F32):
    s, _ = a.shape
    _, r, n = wg.shape
    tm = _tile(s, 512)
    tn = _tile(n, 1024 if r > 512 else 2048)
    return _mm(name, a, wg, dims=_NN, grid=(n // tn, s // tm), parts=N_CHIPS,
               a_spec=pl.BlockSpec((tm, N_CHIPS * r), lambda j, i: (i, 0)),
               b_spec=pl.BlockSpec((N_CHIPS, r, tn), lambda j, i: (0, 0, j)),
               o_spec=pl.BlockSpec((tm, tn), lambda j, i: (i, j)),
               out_shape=SDS((s, n), out_dtype))


def mm_col_t(name, dy, wg, add=None, out_dtype=F32):
    s, _ = dy.shape
    _, k, n = wg.shape
    tm = _tile(s, 512)
    tk = _tile(k, 1024)
    return _mm(name, dy, wg, dims=_NT, grid=(k // tk, s // tm), parts=N_CHIPS,
               a_spec=pl.BlockSpec((tm, N_CHIPS * n), lambda j, i: (i, 0)),
               b_spec=pl.BlockSpec((N_CHIPS, tk, n), lambda j, i: (0, j, 0)),
               o_spec=pl.BlockSpec((tm, tk), lambda j, i: (i, j)),
               add=add, add_spec=pl.BlockSpec((tm, tk), lambda j, i: (i, j)),
               out_shape=SDS((s, k), out_dtype))


def mm_row_t(name, dy, wg, out_dtype=F32):
    s, n = dy.shape
    _, r, _ = wg.shape
    tm = _tile(s, 512)
    return _mm(name, dy, wg, dims=_NT, grid=(N_CHIPS, s // tm),
               a_spec=pl.BlockSpec((tm, n), lambda j, i: (i, 0)),
               b_spec=pl.BlockSpec((None, r, n), lambda j, i: (j, 0, 0)),
               o_spec=pl.BlockSpec((tm, r), lambda j, i: (i, j)),
               out_shape=SDS((s, N_CHIPS * r), out_dtype))


def norm_bwd_pair(name, x_a, gain_a, dy, dres, x_b, gain_b):
    s, d = x_a.shape
    t = _tile(s, ROW_TILE)

    def one(xv, g, dyv):
        r = _rstd(xv)
        xh = xv * r
        dyg = dyv * g
        return r * (dyg - xh * jnp.mean(dyg * xh, axis=-1, keepdims=True)), jnp.sum(dyv * xh, axis=0, keepdims=True)

    def body(xa_ref, ga_ref, dy_ref, r_ref, xb_ref, gb_ref, dx_ref, db_ref, dga_ref, dgb_ref):
        dx, pa = one(xa_ref[...], ga_ref[...], dy_ref[...])
        dx = dx + r_ref[...]
        dx_ref[...] = dx
        db, pb = one(xb_ref[...], gb_ref[...], dx)
        db_ref[...] = db.astype(BF16)

        @pl.when(pl.program_id(0) == 0)
        def _():
            dga_ref[...] = pa
            dgb_ref[...] = pb

        @pl.when(pl.program_id(0) > 0)
        def _():
            dga_ref[...] += pa
            dgb_ref[...] += pb

    row, gain = _rows(t, d), _fixed((1, d))
    return _pc(body, name=name, grid=(s // t,), in_specs=[row, gain, row, row, row, gain], out_specs=[row, row, gain, gain],
               out_shape=[SDS((s, d), F32), SDS((s, d), BF16), SDS((1, d), F32), SDS((1, d), F32)],
               sem=("arbitrary",))(x_a, gain_a, dy, dres, x_b, gain_b)


def ffn_gate_up(name, h, wg_gate, wg_up):
    s, k = h.shape
    _, _, n = wg_gate.shape
    tm = _tile(s, 512)

    def body(h_ref, wg_ref, wu_ref, g_ref, u_ref, a_ref):
        hv = h_ref[...]
        g = jnp.dot(hv, wg_ref[...], preferred_element_type=F32)
        u = jnp.dot(hv, wu_ref[...], preferred_element_type=F32)
        g_ref[...] = g
        u_ref[...] = u
        a_ref[...] = ((g * _sigmoid(g)) * u).astype(BF16)

    wspec = pl.BlockSpec((None, k, n), lambda j, i: (j, 0, 0))
    ospec = pl.BlockSpec((tm, n), lambda j, i: (i, j))
    return _pc(body, name=name, grid=(N_CHIPS, s // tm), in_specs=[pl.BlockSpec((tm, k), lambda j, i: (i, 0)), wspec, wspec],
               out_specs=[ospec] * 3, out_shape=[SDS((s, N_CHIPS * n), F32)] * 2 + [SDS((s, N_CHIPS * n), BF16)],
               sem=("parallel", "parallel"))(h, wg_gate, wg_up)


def ffn_down_bwd(name, df, wg_down, g, u):
    s, n = df.shape
    _, r, _ = wg_down.shape
    tm = _tile(s, 512)

    def body(df_ref, w_ref, g_ref, u_ref, dg_ref, du_ref):
        da = lax.dot_general(df_ref[...], w_ref[...], _NT, preferred_element_type=F32)
        gv = g_ref[...]
        sg = _sigmoid(gv)
        du_ref[...] = (da * (gv * sg)).astype(BF16)
        dg_ref[...] = ((da * u_ref[...]) * (sg * (1.0 + gv * (1.0 - sg)))).astype(BF16)

    cspec = pl.BlockSpec((tm, r), lambda j, i: (i, j))
    return _pc(body, name=name, grid=(N_CHIPS, s // tm),
               in_specs=[pl.BlockSpec((tm, n), lambda j, i: (i, 0)), pl.BlockSpec((None, r, n), lambda j, i: (j, 0, 0)), cspec, cspec],
               out_specs=[cspec] * 2, out_shape=[SDS((s, N_CHIPS * r), BF16)] * 2,
               sem=("parallel", "parallel"))(df, wg_down, g, u)


def mm_wg_col(name, a, dy):
    s, k = a.shape
    n = dy.shape[1] // N_CHIPS
    tm = _tile(k // 2, 512)
    hb = (k // 2) // tm
    return _mm(name, a, dy, dims=_TN, grid=(N_CHIPS, k // tm),
               a_spec=pl.BlockSpec((s, tm), lambda j, i: (0, i)),
               b_spec=pl.BlockSpec((s, n), lambda j, i: (0, j)),
               o_spec=pl.BlockSpec((None, None, tm, n), lambda j, i: (i // hb, j, i % hb, 0)),
               out_shape=SDS((2, N_CHIPS, k // 2, n), BF16))


def mm_wg_row(name, a, dy):
    s, n = dy.shape
    r = a.shape[1] // N_CHIPS
    tn = _tile(n // 2, 512)
    nb = (n // 2) // tn
    return _mm(name, a, dy, dims=_TN, grid=(N_CHIPS, n // tn),
               a_spec=pl.BlockSpec((s, r), lambda j, i: (0, j)),
               b_spec=pl.BlockSpec((s, tn), lambda j, i: (0, i)),
               o_spec=pl.BlockSpec((None, None, r, tn), lambda j, i: (i // nb, j, 0, i % nb)),
               out_shape=SDS((2, N_CHIPS, r, n // 2), BF16))


def pre_norm(name, x, gain):
    s, d = x.shape
    t = _tile(s, ROW_TILE)

    def body(x_ref, g_ref, o_ref):
        xv = x_ref[...]
        o_ref[...] = ((xv * _rstd(xv)) * g_ref[...]).astype(BF16)

    return _pc(body, name=name, grid=(s // t,), in_specs=[_rows(t, d), _fixed((1, d))], out_specs=_rows(t, d),
               out_shape=SDS((s, d), BF16), sem=("parallel",))(x, gain)


def post_pre_norm(name, m, g_post, res, g_pre):
    s, d = m.shape
    t = _tile(s, ROW_TILE)

    def body(m_ref, gp_ref, r_ref, gn_ref, x_ref, h_ref):
        mv = m_ref[...]
        xn = r_ref[...] + (mv * _rstd(mv)) * gp_ref[...]
        x_ref[...] = xn
        h_ref[...] = ((xn * _rstd(xn)) * gn_ref[...]).astype(BF16)

    return _pc(body, name=name, grid=(s // t,),
               in_specs=[_rows(t, d), _fixed((1, d)), _rows(t, d), _fixed((1, d))],
               out_specs=[_rows(t, d), _rows(t, d)], out_shape=[SDS((s, d), F32), SDS((s, d), BF16)],
               sem=("parallel",))(m, g_post, res, g_pre)


def ple_fwd_mid(name, z, pp, x2, g_next):
    s, d = z.shape
    t = _tile(s, ROW_TILE)

    def body(z_ref, p_ref, x_ref, g_ref, xo_ref, h_ref):
        xn = x_ref[...] + p_ref[...] * _sigmoid(z_ref[...])
        xo_ref[...] = xn
        h_ref[...] = ((xn * _rstd(xn)) * g_ref[...]).astype(BF16)

    return _pc(body, name=name, grid=(s // t,),
               in_specs=[_rows(t, d), _rows(t, d), _rows(t, d), _fixed((1, d))],
               out_specs=[_rows(t, d), _rows(t, d)], out_shape=[SDS((s, d), F32), SDS((s, d), BF16)],
               sem=("parallel",))(z, pp, x2, g_next)


def ple_fwd_loss(name, z, pp, x2, target):
    s, d = z.shape
    t = _tile(s, ROW_TILE)

    def body(z_ref, p_ref, x_ref, t_ref, dy_ref, l_ref):
        err = (x_ref[...] + p_ref[...] * _sigmoid(z_ref[...])) - t_ref[...]
        dy_ref[...] = err * (1.0 / d)
        part = jnp.sum(jnp.sum(err * err, axis=-1, keepdims=True), axis=0, keepdims=True) * (0.5 / d)

        @pl.when(pl.program_id(0) == 0)
        def _():
            l_ref[...] = part

        @pl.when(pl.program_id(0) > 0)
        def _():
            l_ref[...] += part

    return _pc(body, name=name, grid=(s // t,),
               in_specs=[_rows(t, d), _rows(t, d), _rows(t, d), _rows(t, d)],
               out_specs=[_rows(t, d), _fixed((1, 1))], out_shape=[SDS((s, d), F32), SDS((1, 1), F32)],
               sem=("arbitrary",))(z, pp, x2, target)


def ple_bwd(name, dx3, z, pp):
    s, d = z.shape
    t = _tile(s, ROW_TILE)

    def body(d_ref, z_ref, p_ref, dpp_ref, dz_ref):
        gate = _sigmoid(z_ref[...])
        dv = d_ref[...]
        dpp_ref[...] = (dv * gate).astype(BF16)
        dz_ref[...] = ((dv * p_ref[...]) * (gate * (1.0 - gate))).astype(BF16)

    return _pc(body, name=name, grid=(s // t,), in_specs=[_rows(t, d)] * 3, out_specs=[_rows(t, d)] * 2,
               out_shape=[SDS((s, d), BF16)] * 2, sem=("parallel",))(dx3, z, pp)


def norm_bwd(name, xin, gain, dy, dres=None, out_dtype=F32):
    s, d = xin.shape
    t = _tile(s, ROW_TILE)

    def body(*refs):
        if dres is None:
            x_ref, g_ref, dy_ref, dx_ref, dg_ref = refs
            r_ref = None
        else:
            x_ref, g_ref, dy_ref, r_ref, dx_ref, dg_ref = refs
        xv = x_ref[...]
        r = _rstd(xv)
        xh = xv * r
        dyv = dy_ref[...].astype(F32)
        dyg = dyv * g_ref[...]
        c = jnp.mean(dyg * xh, axis=-1, keepdims=True)
        dx = r * (dyg - xh * c)
        if r_ref is not None:
            dx = dx + r_ref[...]
        dx_ref[...] = dx.astype(out_dtype)
        part = jnp.sum(dyv * xh, axis=0, keepdims=True)

        @pl.when(pl.program_id(0) == 0)
        def _():
            dg_ref[...] = part

        @pl.when(pl.program_id(0) > 0)
        def _():
            dg_ref[...] += part

    ins = [xin, gain, dy] + ([] if dres is None else [dres])
    specs = [_rows(t, d), _fixed((1, d)), _rows(t, d)] + ([] if dres is None else [_rows(t, d)])
    return _pc(body, name=name, grid=(s // t,), in_specs=specs, out_specs=[_rows(t, d), _fixed((1, d))],
               out_shape=[SDS((s, d), out_dtype), SDS((1, d), F32)], sem=("arbitrary",))(*ins)


def _lane(shape):
    return lax.broadcasted_iota(jnp.int32, shape, 1)


def _swap_halves(x):
    lo = (_lane(x.shape) % 64) < 32
    return jnp.where(lo, pltpu.roll(x, 96, 1), pltpu.roll(x, 32, 1))


def rope_tables(name, pos_col, inv_freq):
    s = pos_col.shape[0]
    t = _tile(s, ROW_TILE)

    def body(p_ref, f_ref, c_ref, s_ref):
        ang = p_ref[...].astype(F32) * f_ref[...]
        lo = (_lane(ang.shape) % 64) < 32
        c_ref[...] = jnp.cos(ang)
        sn = jnp.sin(ang)
        s_ref[...] = jnp.where(lo, -sn, sn)

    return _pc(body, name=name, grid=(s // t,), in_specs=[_rows(t, 1), _fixed((1, LANES))],
               out_specs=[_rows(t, LANES)] * 2, out_shape=[SDS((s, LANES), F32)] * 2, sem=("parallel",))(pos_col, inv_freq)


def _pad_heads(chunk, lo_mask):
    zero = jnp.zeros_like(chunk)
    return jnp.where(lo_mask, chunk, zero), jnp.where(lo_mask, pltpu.roll(chunk, 64, 1), zero)


def rope_qkv(name, proj, cos, sin):
    s = proj.shape[0]
    t = _tile(s, ROW_TILE)

    def body(q_ref, kv_ref, c_ref, s_ref, qp_ref, kp_ref, vp_ref):
        cs, sn = c_ref[...], s_ref[...]
        lo_mask = _lane(cs.shape) < 64

        def rot(x):
            return x * cs + _swap_halves(x) * sn

        for j in range(ATTN_WIDTH // LANES):
            a, b = _pad_heads(rot(q_ref[:, j * LANES:(j + 1) * LANES]), lo_mask)
            qp_ref[:, (2 * j) * LANES:(2 * j + 1) * LANES] = a.astype(BF16)
            qp_ref[:, (2 * j + 1) * LANES:(2 * j + 2) * LANES] = b.astype(BF16)
        for j in range(KV_WIDTH // LANES):
            a, b = _pad_heads(rot(kv_ref[:, j * LANES:(j + 1) * LANES]), lo_mask)
            kp_ref[:, (2 * j) * LANES:(2 * j + 1) * LANES] = a.astype(BF16)
            kp_ref[:, (2 * j + 1) * LANES:(2 * j + 2) * LANES] = b.astype(BF16)
            a, b = _pad_heads(kv_ref[:, KV_WIDTH + j * LANES:KV_WIDTH + (j + 1) * LANES], lo_mask)
            vp_ref[:, (2 * j) * LANES:(2 * j + 1) * LANES] = a.astype(BF16)
            vp_ref[:, (2 * j + 1) * LANES:(2 * j + 2) * LANES] = b.astype(BF16)

    return _pc(body, name=name, grid=(s // t,),
               in_specs=[_rows(t, ATTN_WIDTH, 0), _rows(t, 2 * KV_WIDTH, 2), _rows(t, LANES), _rows(t, LANES)],
               out_specs=[_rows(t, N_Q_HEADS * LANES), _rows(t, N_KV_HEADS * LANES), _rows(t, N_KV_HEADS * LANES)],
               out_shape=[SDS((s, N_Q_HEADS * LANES), BF16), SDS((s, N_KV_HEADS * LANES), BF16),
                          SDS((s, N_KV_HEADS * LANES), BF16)],
               sem=("parallel",))(proj, proj, cos, sin)


def _attn_mask(n):
    L = WINDOW
    qi = lax.broadcasted_iota(jnp.int32, (L, 2 * L), 0) + L
    ki = lax.broadcasted_iota(jnp.int32, (L, 2 * L), 1)
    rel = qi - ki
    return (rel >= 0) & (rel < WINDOW) & ((n > 0) | (ki >= L))


def _attn_scores(qh, kk, valid):
    sc = lax.dot_general(qh, kk, _NT, preferred_element_type=F32) * 0.125
    return jnp.where(valid, sc, MASK_VALUE)


def _attn_softmax(sc, sink):
    m = jnp.maximum(jnp.max(sc, axis=-1, keepdims=True), sink)
    e = jnp.exp(sc - m)
    es = jnp.exp(sink - m)
    den = jnp.sum(e, axis=-1, keepdims=True) + es
    return e / den, es / den


def _attn_specs(s):
    L = WINDOW
    cur = lambda n: (n, 0)
    prev = lambda n: (jnp.maximum(n - 1, 0), 0)
    kvw = N_KV_HEADS * LANES
    return [pl.BlockSpec((L, N_Q_HEADS * LANES), cur), pl.BlockSpec((L, kvw), cur), pl.BlockSpec((L, kvw), prev),
            pl.BlockSpec((L, kvw), cur), pl.BlockSpec((L, kvw), prev), pl.BlockSpec(memory_space=pltpu.SMEM)]


def attn_fwd(name, qp, kp, vp, sinks):
    s = qp.shape[0]
    L = WINDOW

    def body(q_ref, kc_ref, kp_ref, vc_ref, vp_ref, sk_ref, o_ref):
        valid = _attn_mask(pl.program_id(0))
        kks, vvs = [], []
        for kvh in range(N_KV_HEADS):
            cols = slice(kvh * LANES, (kvh + 1) * LANES)
            kks.append(jnp.concatenate([kp_ref[:, cols], kc_ref[:, cols]], axis=0))
            vvs.append(jnp.concatenate([vp_ref[:, cols], vc_ref[:, cols]], axis=0))
        scs = [_attn_scores(q_ref[:, h * LANES:(h + 1) * LANES], kks[h // Q_PER_KV], valid) for h in range(N_Q_HEADS)]
        pbs = [_attn_softmax(scs[h], sk_ref[h])[0].astype(BF16) for h in range(N_Q_HEADS)]
        outs = [jnp.dot(pbs[h], vvs[h // Q_PER_KV], preferred_element_type=F32) for h in range(N_Q_HEADS)]
        for j in range(ATTN_WIDTH // LANES):
            o_ref[:, j * LANES:(j + 1) * LANES] = outs[2 * j] + pltpu.roll(outs[2 * j + 1], 64, 1)

    return _pc(body, name=name, grid=(s // L,), in_specs=_attn_specs(s),
               out_specs=pl.BlockSpec((L, ATTN_WIDTH), lambda n: (n, 0)),
               out_shape=SDS((s, ATTN_WIDTH), F32), sem=("parallel",))(qp, kp, kp, vp, vp, sinks)


def attn_bwd(name, qp, kp, vp, sinks, dattn):
    s = qp.shape[0]
    L = WINDOW
    kvw = N_KV_HEADS * LANES

    def body(q_ref, kc_ref, kp_ref, vc_ref, vp_ref, sk_ref, do_ref, dq_ref, dkc_ref, dkp_ref, dvc_ref, dvp_ref, ds_ref):
        n = pl.program_id(0)
        valid = _attn_mask(n)
        lo_mask = _lane((L, LANES)) < 64
        lane1 = _lane((1, LANES))
        dsink = jnp.zeros((1, LANES), F32)
        heads = range(N_Q_HEADS)
        kks, vvs = [], []
        for kvh in range(N_KV_HEADS):
            cols = slice(kvh * LANES, (kvh + 1) * LANES)
            kks.append(jnp.concatenate([kp_ref[:, cols], kc_ref[:, cols]], axis=0))
            vvs.append(jnp.concatenate([vp_ref[:, cols], vc_ref[:, cols]], axis=0))
        qs, dos, scs, dps = [], [], [], []
        for h in heads:
            qs.append(q_ref[:, h * LANES:(h + 1) * LANES])
            chunk = do_ref[:, (h // 2) * LANES:(h // 2 + 1) * LANES]
            if h % 2:
                chunk = pltpu.roll(chunk, 64, 1)
            dos.append(jnp.where(lo_mask, chunk, 0.0).astype(BF16))
            scs.append(_attn_scores(qs[h], kks[h // Q_PER_KV], valid))
            dps.append(lax.dot_general(dos[h], vvs[h // Q_PER_KV], _NT, preferred_element_type=F32))
        pbs, dsbs = [], []
        for h in heads:
            p, ps = _attn_softmax(scs[h], sk_ref[h])
            delta = jnp.sum(p * dps[h], axis=-1, keepdims=True)
            dsbs.append(((p * (dps[h] - delta)) * 0.125).astype(BF16))
            pbs.append(p.astype(BF16))
            dsink = dsink + jnp.where(lane1 == h, -jnp.sum(ps * delta, axis=0, keepdims=True), 0.0)
        for kvh in range(N_KV_HEADS):
            cols = slice(kvh * LANES, (kvh + 1) * LANES)
            dkk = jnp.zeros((2 * L, LANES), F32)
            dvv = jnp.zeros((2 * L, LANES), F32)
            for h in range(kvh * Q_PER_KV, (kvh + 1) * Q_PER_KV):
                dq_ref[:, h * LANES:(h + 1) * LANES] = jnp.dot(dsbs[h], kks[kvh], preferred_element_type=F32)
                dkk = dkk + lax.dot_general(dsbs[h], qs[h], _TN, preferred_element_type=F32)
                dvv = dvv + lax.dot_general(pbs[h], dos[h], _TN, preferred_element_type=F32)
            dkp_ref[:, cols] = dkk[:L]
            dkc_ref[:, cols] = dkk[L:]
            dvp_ref[:, cols] = dvv[:L]
            dvc_ref[:, cols] = dvv[L:]

        @pl.when(n == 0)
        def _():
            ds_ref[...] = dsink

        @pl.when(n > 0)
        def _():
            ds_ref[...] += dsink

    blk = lambda w: pl.BlockSpec((L, w), lambda n: (n, 0))
    return _pc(body, name=name, grid=(s // L,), in_specs=_attn_specs(s) + [blk(ATTN_WIDTH)],
               out_specs=[blk(N_Q_HEADS * LANES), blk(kvw), blk(kvw), blk(kvw), blk(kvw), _fixed((1, LANES))],
               out_shape=[SDS((s, N_Q_HEADS * LANES), F32)] + [SDS((s, kvw), F32)] * 4 + [SDS((1, LANES), F32)],
               sem=("arbitrary",))(qp, kp, kp, vp, vp, sinks, dattn)


def rope_bwd(name, dqp, dkc, dkp, dvc, dvp, cos, sin):
    s = dqp.shape[0]
    L = WINDOW
    nb = s // L
    kvw = N_KV_HEADS * LANES

    def body(dq_ref, dkc_ref, dkp_ref, dvc_ref, dvp_ref, c_ref, s_ref, o_ref):
        cs, sn = c_ref[...], s_ref[...]
        more = (pl.program_id(0) < nb - 1).astype(F32)

        def unrot(x):
            return x * cs - _swap_halves(x) * sn

        def compact(ref, j, nxt=None):
            a = ref[:, (2 * j) * LANES:(2 * j + 1) * LANES]
            b = ref[:, (2 * j + 1) * LANES:(2 * j + 2) * LANES]
            if nxt is not None:
                a = a + more * nxt[:, (2 * j) * LANES:(2 * j + 1) * LANES]
                b = b + more * nxt[:, (2 * j + 1) * LANES:(2 * j + 2) * LANES]
            return a + pltpu.roll(b, 64, 1)

        for j in range(ATTN_WIDTH // LANES):
            o_ref[:, j * LANES:(j + 1) * LANES] = unrot(compact(dq_ref, j)).astype(BF16)
        for j in range(KV_WIDTH // LANES):
            o_ref[:, (COL_K + j) * LANES:(COL_K + j + 1) * LANES] = unrot(compact(dkc_ref, j, dkp_ref)).astype(BF16)
            o_ref[:, (COL_V + j) * LANES:(COL_V + j + 1) * LANES] = compact(dvc_ref, j, dvp_ref).astype(BF16)

    cur = lambda n: (n, 0)
    nxt = lambda n: (jnp.minimum(n + 1, nb - 1), 0)
    return _pc(body, name=name, grid=(nb,),
               in_specs=[pl.BlockSpec((L, N_Q_HEADS * LANES), cur), pl.BlockSpec((L, kvw), cur), pl.BlockSpec((L, kvw), nxt),
                         pl.BlockSpec((L, kvw), cur), pl.BlockSpec((L, kvw), nxt), pl.BlockSpec((L, LANES), cur),
                         pl.BlockSpec((L, LANES), cur)],
               out_specs=pl.BlockSpec((L, COL_HQ * LANES), cur), out_shape=SDS((s, COL_HQ * LANES), BF16),
               sem=("parallel",))(dqp, dkc, dkp, dvc, dvp, cos, sin)


def _split3(x):
    a = x.astype(BF16)
    r = x - a.astype(F32)
    b = r.astype(BF16)
    c = (r - b.astype(F32)).astype(BF16)
    return a, b, c


def _chunk_sum(x, upper):
    t = x.shape[0]
    ri = lax.broadcasted_iota(jnp.int32, (t, t), 0)
    ci = lax.broadcasted_iota(jnp.int32, (t, t), 1)
    same = (ri // HGRN_CHUNK) == (ci // HGRN_CHUNK)
    tri = (ci >= ri) if upper else (ci <= ri)
    m = jnp.where(same & tri, 1.0, 0.0).astype(BF16)
    out = None
    for part in _split3(x):
        y = jnp.dot(m, part, preferred_element_type=F32)
        out = y if out is None else out + y
    return out


def _lower_bound(l_ref, layer):
    lv = l_ref[...]
    e = jnp.exp(lv - jnp.max(lv, axis=0, keepdims=True))
    sm = e / jnp.sum(e, axis=0, keepdims=True)
    s0 = sm[0:1]
    return (s0 - s0) if layer == 0 else ((s0 + sm[1:2]) - s0)


def _hgrn_gates(hq_ref, hf_ref, lb):
    z = hf_ref[...]
    sg = _sigmoid(z)
    f = lb + (1.0 - lb) * sg
    kin = (1.0 - lb) * _sigmoid(-z)
    hq = hq_ref[...]
    sq = _sigmoid(hq)
    return sg, f, kin, hq, sq


def _shift_down(x, d):
    return x if d == 0 else pltpu.roll(x, d, 0)


def _shift_up(x, d):
    return x if d == 0 else pltpu.roll(x, x.shape[0] - d, 0)


CHUNKS_PER_BLOCK = LANES // HGRN_CHUNK


def _chunk_iotas():
    shape = (HGRN_CHUNK, LANES)
    return lax.broadcasted_iota(jnp.int32, shape, 0), lax.broadcasted_iota(jnp.int32, shape, 1)


def _chunk_rows(block, chunk):
    start = block * LANES + chunk * HGRN_CHUNK
    return slice(start, start + HGRN_CHUNK)


HGRN_HEADS_PER_STEP = 2
HGRN_STEP_WIDTH = HGRN_HEADS_PER_STEP * LANES


def _hgrn_specs(t, rev, nt):
    row = (lambda h, i: nt - 1 - i) if rev else (lambda h, i: i)
    col = lambda base: pl.BlockSpec((t, HGRN_STEP_WIDTH),
                                    lambda h, i, base=base: (row(h, i), base // HGRN_HEADS_PER_STEP + h))
    return col, row


def _head_views(refs, hh):
    return [r.at[:, pl.ds(hh * LANES, LANES)] for r in refs]


def hgrn_fwd(name, proj, lb_logits, layer):
    s = proj.shape[0]
    t = _tile(s, ROW_TILE)
    nt = s // t
    nc = t // HGRN_CHUNK
    col, row = _hgrn_specs(t, False, nt)

    def body(hq_ref, hf_ref, hi_ref, l_ref, o_ref, st_ref, state):
        @pl.when(pl.program_id(1) == 0)
        def _():
            state[...] = jnp.zeros_like(state)

        for hh in range(HGRN_HEADS_PER_STEP):
            head(*_head_views((hq_ref, hf_ref, hi_ref, l_ref, o_ref), hh), st_ref.at[:, hh], state.at[hh])

    def head(hq_ref, hf_ref, hi_ref, l_ref, o_ref, st_ref, state):
        lb = _lower_bound(l_ref, layer)
        sg, f, kin, hq, sq = _hgrn_gates(hq_ref, hf_ref, lb)
        q = hq * sq
        vb = hi_ref[...].astype(BF16)
        b = _chunk_sum(jnp.log(f), False)
        qe = (q * jnp.exp(b)).astype(BF16)
        trow, lane = _chunk_iotas()
        chunks = [(j, cc) for j in range(t // LANES) for cc in range(CHUNKS_PER_BLOCK)]
        decay, update = [], []
        for j, cc in chunks:
            rs = _chunk_rows(j, cc)
            bc = b[rs]
            bl = bc[HGRN_CHUNK - 1:HGRN_CHUNK, :]
            ke = (kin[rs] * jnp.exp(bl - bc)).astype(BF16)
            decay.append(jnp.exp(bl))
            update.append(lax.dot_general(vb[rs], ke, _TN, preferred_element_type=F32))
        st = state[...]
        for c in range(nc):
            st_ref[c] = st
            st = st * decay[c] + update[c]
        state[...] = st
        for j in range(t // LANES):
            blk = slice(j * LANES, (j + 1) * LANES)
            rows = []
            for cc in range(CHUNKS_PER_BLOCK):
                rs = _chunk_rows(j, cc)
                bc, qc, kc = b[rs], q[rs], kin[rs]
                here = trow + cc * HGRN_CHUNK
                am = jnp.where(lane == here, jnp.sum(qc * kc, axis=-1, keepdims=True), 0.0)
                for d in range(1, HGRN_CHUNK):
                    e = jnp.exp(jnp.where(trow >= d, bc - _shift_down(bc, d), MASK_VALUE))
                    a = jnp.sum((qc * _shift_down(kc, d)) * e, axis=-1, keepdims=True)
                    am = jnp.where(lane == here - d, a, am)
                rows.append(am)
            o_intra = jnp.dot(jnp.concatenate(rows, axis=0).astype(BF16), vb[blk], preferred_element_type=F32)
            for cc in range(CHUNKS_PER_BLOCK):
                rs = _chunk_rows(j, cc)
                o_ref[rs, :] = (o_intra[cc * HGRN_CHUNK:(cc + 1) * HGRN_CHUNK]
                                + lax.dot_general(qe[rs], st_ref[j * CHUNKS_PER_BLOCK + cc].astype(BF16), _NT,
                                                  preferred_element_type=F32))

    hp = HGRN_HEADS_PER_STEP
    return _pc(body, name=name, grid=(HGRN_HEADS // hp, nt),
               in_specs=[col(COL_HQ), col(COL_HF), col(COL_HI), pl.BlockSpec((2, HGRN_STEP_WIDTH), lambda h, i: (0, h))],
               out_specs=[pl.BlockSpec((t, HGRN_STEP_WIDTH), lambda h, i: (i, h)),
                          pl.BlockSpec((nc, hp, LANES, LANES), lambda h, i: (i, h, 0, 0))],
               out_shape=[SDS((s, HGRN_WIDTH), F32), SDS((s // HGRN_CHUNK, HGRN_HEADS, LANES, LANES), F32)],
               scratch=[pltpu.VMEM((hp, LANES, LANES), F32)],
               sem=("parallel", "arbitrary"))(proj, proj, proj, lb_logits)


def hgrn_bwd(name, proj, lb_logits, layer, states, do):
    s = proj.shape[0]
    t = _tile(s, ROW_TILE)
    nt = s // t
    nc = t // HGRN_CHUNK
    col, row = _hgrn_specs(t, True, nt)

    def body(hq_ref, hf_ref, hi_ref, l_ref, st_ref, do_ref, dhq_ref, dhf_ref, dhi_ref, dlb_ref, dstate):
        @pl.when(pl.program_id(1) == 0)
        def _():
            dstate[...] = jnp.zeros_like(dstate)

        for hh in range(HGRN_HEADS_PER_STEP):
            hq_v, hf_v, hi_v, l_v, do_v, dhq_v, dhf_v, dhi_v, dlb_v = _head_views(
                (hq_ref, hf_ref, hi_ref, l_ref, do_ref, dhq_ref, dhf_ref, dhi_ref, dlb_ref), hh)
            head(hq_v, hf_v, hi_v, l_v, st_ref.at[:, hh], do_v, dhq_v, dhf_v, dhi_v, dlb_v, dstate.at[hh])

    def head(hq_ref, hf_ref, hi_ref, l_ref, st_ref, do_ref, dhq_ref, dhf_ref, dhi_ref, dlb_ref, dstate):
        first = pl.program_id(1) == 0
        lb = _lower_bound(l_ref, layer)
        sg, f, kin, hq, sq = _hgrn_gates(hq_ref, hf_ref, lb)
        q = hq * sq
        vb = hi_ref[...].astype(BF16)
        b = _chunk_sum(jnp.log(f), False)
        dob = do_ref[...].astype(BF16)
        eb = jnp.exp(b)
        qe = q * eb
        qeb = qe.astype(BF16)
        trow, lane = _chunk_iotas()
        last_row = trow == HGRN_CHUNK - 1

        decay, update = [None] * nc, [None] * nc
        for c in range(nc):
            rs = slice(c * HGRN_CHUNK, (c + 1) * HGRN_CHUNK)
            decay[c] = jnp.exp(b[(c + 1) * HGRN_CHUNK - 1:(c + 1) * HGRN_CHUNK, :])
            update[c] = lax.dot_general(dob[rs], qeb[rs], _TN, preferred_element_type=F32)
        dn_in = [None] * nc
        dn = dstate[...]
        for c in reversed(range(nc)):
            dn_in[c] = dn
            dn = dn * decay[c] + update[c]
        dstate[...] = dn

        dq_c, dk_c, dv_c, db_c = [None] * nc, [None] * nc, [None] * nc, [None] * nc
        for j in reversed(range(t // LANES)):
            blk = slice(j * LANES, (j + 1) * LANES)
            damat = lax.dot_general(dob[blk], vb[blk], _NT, preferred_element_type=F32)
            rows = [None] * CHUNKS_PER_BLOCK
            for cc in reversed(range(CHUNKS_PER_BLOCK)):
                c = j * CHUNKS_PER_BLOCK + cc
                rs = _chunk_rows(j, cc)
                bc, qc, kc = b[rs], q[rs], kin[rs]
                bl = bc[HGRN_CHUNK - 1:HGRN_CHUNK, :]
                ekb = jnp.exp(bl - bc)
                ke = kc * ekb
                st = st_ref[c]
                dn = dn_in[c]
                dnb = dn.astype(BF16)
                dqe = jnp.dot(dob[rs], st.astype(BF16), preferred_element_type=F32)
                dke = jnp.dot(vb[rs], dnb, preferred_element_type=F32)
                dv_c[c] = lax.dot_general(ke.astype(BF16), dnb, _NT, preferred_element_type=F32)
                dbl = jnp.sum(dn * st, axis=0, keepdims=True) * decay[c] + jnp.sum(dke * ke, axis=0, keepdims=True)

                dam = damat[cc * HGRN_CHUNK:(cc + 1) * HGRN_CHUNK]
                here = trow + cc * HGRN_CHUNK
                on = lane == here
                da = jnp.sum(jnp.where(on, dam, 0.0), axis=-1, keepdims=True)
                am = jnp.where(on, jnp.sum(qc * kc, axis=-1, keepdims=True), 0.0)
                dq = dqe * eb[rs] + da * kc
                dk = dke * ekb + da * qc
                db = (dqe * qe[rs] - dke * ke) + jnp.where(last_row, dbl, 0.0)
                for d in range(1, HGRN_CHUNK):
                    on = lane == here - d
                    ks = _shift_down(kc, d)
                    e = jnp.exp(jnp.where(trow >= d, bc - _shift_down(bc, d), MASK_VALUE))
                    qd = qc * e
                    w = qd * ks
                    am = jnp.where(on, jnp.sum(w, axis=-1, keepdims=True), am)
                    da = jnp.sum(jnp.where(on, dam, 0.0), axis=-1, keepdims=True)
                    dq = dq + da * (ks * e)
                    daw = da * w
                    db = db + daw - _shift_up(daw, d)
                    dk = dk + _shift_up(da * qd, d)
                rows[cc] = am
                dq_c[c], dk_c[c], db_c[c] = dq, dk, db
            dv_blk = lax.dot_general(jnp.concatenate(rows, axis=0).astype(BF16), dob[blk], _TN, preferred_element_type=F32)
            for cc in range(CHUNKS_PER_BLOCK):
                c = j * CHUNKS_PER_BLOCK + cc
                dv_c[c] = dv_c[c] + dv_blk[cc * HGRN_CHUNK:(cc + 1) * HGRN_CHUNK]
        dq = jnp.concatenate(dq_c, axis=0)
        dk = jnp.concatenate(dk_c, axis=0)
        dv = jnp.concatenate(dv_c, axis=0)
        db = jnp.concatenate(db_c, axis=0)
        dg = _chunk_sum(db, True)
        dhq_ref[...] = (dq * (sq * (1.0 + hq * (1.0 - sq)))).astype(BF16)
        dhi_ref[...] = dv.astype(BF16)
        dfk = dg / f - dk
        dhf_ref[...] = ((dfk * (1.0 - lb)) * (sg * (1.0 - sg))).astype(BF16)
        part = jnp.sum(dfk * (1.0 - sg), axis=0, keepdims=True)

        @pl.when(first)
        def _():
            dlb_ref[...] = part

        @pl.when(jnp.logical_not(first))
        def _():
            dlb_ref[...] += part

    hp = HGRN_HEADS_PER_STEP
    out_col = pl.BlockSpec((t, HGRN_STEP_WIDTH), lambda h, i: (nt - 1 - i, h))
    return _pc(body, name=name, grid=(HGRN_HEADS // hp, nt),
               in_specs=[col(COL_HQ), col(COL_HF), col(COL_HI), pl.BlockSpec((2, HGRN_STEP_WIDTH), lambda h, i: (0, h)),
                         pl.BlockSpec((nc, hp, LANES, LANES), lambda h, i: (nt - 1 - i, h, 0, 0)), out_col],
               out_specs=[out_col, out_col, out_col, pl.BlockSpec((1, HGRN_STEP_WIDTH), lambda h, i: (0, h))],
               out_shape=[SDS((s, HGRN_WIDTH), BF16)] * 3 + [SDS((1, HGRN_WIDTH), F32)],
               scratch=[pltpu.VMEM((hp, LANES, LANES), F32)],
               sem=("parallel", "arbitrary"))(proj, proj, proj, lb_logits, states, do)


def mix_out_fwd(name, attn, o, proj, g_attn, g_hgrn):
    s = attn.shape[0]
    t = _tile(s, ROW_TILE)
    half = HGRN_WIDTH // 2

    def body(a_ref, o_ref, hg0_ref, hg1_ref, ga_ref, gh_ref, c_ref):
        av = a_ref[...]
        c_ref[:, :ATTN_WIDTH] = ((av * _rstd(av)) * ga_ref[...]).astype(BF16)
        for j in range(HGRN_HEADS):
            cols = slice(j * LANES, (j + 1) * LANES)
            ov = o_ref[:, cols]
            hg_ref, hcols = (hg0_ref, cols) if j < 4 else (hg1_ref, slice((j - 4) * LANES, (j - 3) * LANES))
            hg = hg_ref[:, hcols]
            on = (ov * _rstd(ov)) * gh_ref[:, cols]
            c_ref[:, ATTN_WIDTH + j * LANES:ATTN_WIDTH + (j + 1) * LANES] = (on * (hg * _sigmoid(hg))).astype(BF16)

    return _pc(body, name=name, grid=(s // t,),
               in_specs=[_rows(t, ATTN_WIDTH), _rows(t, HGRN_WIDTH), _rows(t, half, COL_HG // 4), _rows(t, half, COL_HG // 4 + 1),
                         _fixed((1, ATTN_WIDTH)), _fixed((1, HGRN_WIDTH))],
               out_specs=_rows(t, D_MODEL), out_shape=SDS((s, D_MODEL), BF16), sem=("parallel",))(attn, o, proj, proj, g_attn, g_hgrn)


def mix_out_bwd(name, dcat, attn, o, proj, g_attn, g_hgrn):
    s = attn.shape[0]
    t = _tile(s, ROW_TILE)
    half = HGRN_WIDTH // 2

    def body(dc_ref, a_ref, o_ref, hg0_ref, hg1_ref, ga_ref, gh_ref, da_ref, do_ref, dhg_ref, dga_ref, dgh_ref, pa_s, ph_s):
        av = a_ref[...]
        r = _rstd(av)
        xh = av * r
        dyv = dc_ref[:, :ATTN_WIDTH]
        dyg = dyv * ga_ref[...]
        da_ref[...] = r * (dyg - xh * jnp.mean(dyg * xh, axis=-1, keepdims=True))
        pa_s[...] = jnp.sum(dyv * xh, axis=0, keepdims=True)
        for j in range(HGRN_HEADS):
            cols = slice(j * LANES, (j + 1) * LANES)
            ov = o_ref[:, cols]
            hg_ref, hcols = (hg0_ref, cols) if j < 4 else (hg1_ref, slice((j - 4) * LANES, (j - 3) * LANES))
            hg = hg_ref[:, hcols]
            sg = _sigmoid(hg)
            r = _rstd(ov)
            xh = ov * r
            gain = gh_ref[:, cols]
            dh = dc_ref[:, ATTN_WIDTH + j * LANES:ATTN_WIDTH + (j + 1) * LANES]
            dhg_ref[:, cols] = ((dh * (xh * gain)) * (sg * (1.0 + hg * (1.0 - sg)))).astype(BF16)
            dyv = dh * (hg * sg)
            dyg = dyv * gain
            do_ref[:, cols] = r * (dyg - xh * jnp.mean(dyg * xh, axis=-1, keepdims=True))
            ph_s[:, cols] = jnp.sum(dyv * xh, axis=0, keepdims=True)

        @pl.when(pl.program_id(0) == 0)
        def _():
            dga_ref[...] = pa_s[...]
            dgh_ref[...] = ph_s[...]

        @pl.when(pl.program_id(0) > 0)
        def _():
            dga_ref[...] += pa_s[...]
            dgh_ref[...] += ph_s[...]

    return _pc(body, name=name, grid=(s // t,),
               in_specs=[_rows(t, D_MODEL), _rows(t, ATTN_WIDTH), _rows(t, HGRN_WIDTH), _rows(t, half, COL_HG // 4),
                         _rows(t, half, COL_HG // 4 + 1), _fixed((1, ATTN_WIDTH)), _fixed((1, HGRN_WIDTH))],
               out_specs=[_rows(t, ATTN_WIDTH), _rows(t, HGRN_WIDTH), _rows(t, HGRN_WIDTH), _fixed((1, ATTN_WIDTH)),
                          _fixed((1, HGRN_WIDTH))],
               out_shape=[SDS((s, ATTN_WIDTH), F32), SDS((s, HGRN_WIDTH), F32), SDS((s, HGRN_WIDTH), BF16),
                          SDS((1, ATTN_WIDTH), F32), SDS((1, HGRN_WIDTH), F32)],
               scratch=[pltpu.VMEM((1, ATTN_WIDTH), F32), pltpu.VMEM((1, HGRN_WIDTH), F32)],
               sem=("arbitrary",))(dcat, attn, o, proj, proj, g_attn, g_hgrn)


BIG = (("w_in", 2048, 1408, 0), ("w_out", 512, 2048, 1), ("w_ffn_gate", 2048, 1408, 0), ("w_ffn_up", 2048, 1408, 0),
       ("w_ffn_down", 1408, 2048, 1), ("w_ple_gate", 512, 2048, 1), ("w_ple_proj", 256, 512, 0))
BIG_BY_NAME = {spec[0]: spec for spec in BIG}
HBM_SPEC = pl.BlockSpec(memory_space=pltpu.HBM)
SEM_SPEC = pl.BlockSpec(memory_space=pltpu.SEMAPHORE)
TOKEN_SHAPE = (8, LANES)


def _split_call(body, *, name, in_specs, out_specs, out_shape, aliases):
    return pl.pallas_call(body, name=name, in_specs=in_specs, out_specs=out_specs, out_shape=out_shape,
                          input_output_aliases=aliases,
                          compiler_params=pltpu.CompilerParams(has_side_effects=pltpu.SideEffectType.DATAFLOW_SIDE_EFFECTING))


def _in_hbm(arrays):
    return [pltpu.with_memory_space_constraint(a, pltpu.HBM) for a in arrays]


def cast_to_slot(name, place, w, layer, after):
    _, r, c = w.shape
    tr = _tile(r, 512)

    def body(place_ref, w_ref, after_ref, o_ref):
        o_ref[...] = w_ref[...].astype(BF16)

    gs = pltpu.PrefetchScalarGridSpec(
        num_scalar_prefetch=1, grid=(r // tr,),
        in_specs=[pl.BlockSpec((None, tr, c), lambda i, pr: (layer, i, 0)), pl.BlockSpec(memory_space=pl.ANY)],
        out_specs=pl.BlockSpec((None, tr, c), lambda i, pr: (pr[1], i, 0)))
    return _pc(body, name=name, grid_spec=gs, in_specs=None, out_specs=None, out_shape=SDS((N_CHIPS, r, c), BF16),
               sem=("parallel",))(place, w, after)


def _place():
    x, y, c = lax.axis_index("x"), lax.axis_index("y"), lax.axis_index("c")
    chips = [(1 - x, y), (x, 1 - y), (1 - x, 1 - y)]
    return x, y, c, chips


def _half(ref, axis, c, rows, cols):
    if axis == 0:
        return ref.at[pl.ds(pl.multiple_of(c * (rows // 2), 16), rows // 2), :]
    return ref.at[:, pl.ds(pl.multiple_of(c * (cols // 2), LANES), cols // 2)]


def _gather_copies(specs, bufs, send, recv):
    x, y, c, chips = _place()
    cps = []
    for t, (_, rows, cols, axis) in enumerate(specs):
        mine = _half(bufs[t].at[2 * x + y], axis, c, rows, cols)
        for k, (cx, cy) in enumerate(chips):
            cps.append(pltpu.make_async_remote_copy(src_ref=mine, dst_ref=mine, send_sem=send.at[3 * t + k],
                                                    recv_sem=recv.at[3 * t + k], device_id=(cx, cy, c), device_id_type=MESH))
    return cps


def gather_start(name, specs, bufs):
    nt = len(bufs)
    n = 3 * nt

    def body(*refs):
        send, recv, token = refs[nt], refs[nt + 1], refs[-1]
        for cp in _gather_copies(specs, refs[:nt], send, recv):
            cp.start()
        token[...] = jnp.zeros(TOKEN_SHAPE, F32)

    out = _split_call(
        body, name=name, in_specs=[HBM_SPEC] * nt,
        out_specs=(SEM_SPEC, SEM_SPEC) + (HBM_SPEC,) * nt + (pl.BlockSpec(memory_space=pltpu.VMEM),),
        out_shape=(pltpu.SemaphoreType.DMA((n,)), pltpu.SemaphoreType.DMA((n,)))
        + tuple(pltpu.HBM(b.shape, b.dtype) for b in bufs) + (SDS(TOKEN_SHAPE, F32),),
        aliases={t: 2 + t for t in range(nt)})(*_in_hbm(bufs))
    return out[0], out[1], list(out[2:2 + nt]), out[-1]


def gather_wait(name, specs, send, recv, bufs, after):
    nt = len(bufs)

    def body(*refs):
        for cp in _gather_copies(specs, refs[:nt], refs[nt], refs[nt + 1]):
            cp.wait_send()
            cp.wait_recv()

    out = _split_call(
        body, name=name, in_specs=[HBM_SPEC] * nt + [SEM_SPEC, SEM_SPEC, pl.BlockSpec(memory_space=pl.ANY)],
        out_specs=(HBM_SPEC,) * nt, out_shape=tuple(pltpu.HBM(b.shape, b.dtype) for b in bufs),
        aliases={t: t for t in range(nt)})(*bufs, send, recv, after)
    return list(out)


def gather_pass(name, specs, bufs):
    nt = len(bufs)

    def body(*refs):
        ins, outs = refs[:nt], refs[nt:2 * nt]
        send, recv = refs[2 * nt:]
        x, y, c, chips = _place()
        cps = []
        for t, (_, rows, cols, axis) in enumerate(specs):
            for k, (cx, cy) in enumerate(chips):
                cp = pltpu.make_async_remote_copy(
                    src_ref=_half(ins[t].at[2 * cx + cy], axis, c, rows, cols),
                    dst_ref=_half(outs[t].at[2 * cx + cy], axis, c, rows, cols),
                    send_sem=send.at[3 * t + k], recv_sem=recv.at[3 * t + k], device_id=(x, y, 1 - c), device_id_type=MESH)
                cp.start()
                cps.append(cp)
        for t, (_, rows, cols, axis) in enumerate(specs):
            for k, (cx, cy) in enumerate(chips):
                theirs = _half(outs[t].at[2 * cx + cy], axis, 1 - c, rows, cols)
                pltpu.make_async_remote_copy(src_ref=theirs, dst_ref=theirs, send_sem=send.at[3 * t + k],
                                             recv_sem=recv.at[3 * t + k], device_id=(x, y, 1 - c), device_id_type=MESH).wait_recv()
        for cp in cps:
            cp.wait_send()

    return _pc(body, name=name, in_specs=[HBM_SPEC] * nt, out_specs=[HBM_SPEC] * nt,
               out_shape=[SDS(b.shape, b.dtype) for b in bufs], scratch=[pltpu.SemaphoreType.DMA((3 * nt,))] * 2,
               input_output_aliases={t: t for t in range(nt)})(*bufs)


def reduce_to_sibling(name, grads):
    nt = len(grads)

    def body(*refs):
        srcs, dsts = refs[:nt], refs[nt:2 * nt]
        send, recv = refs[2 * nt:]
        x, y, c, _ = _place()
        cps = []
        for t in range(nt):
            cp = pltpu.make_async_remote_copy(src_ref=srcs[t].at[1 - c], dst_ref=dsts[t], send_sem=send.at[t],
                                              recv_sem=recv.at[t], device_id=(x, y, 1 - c), device_id_type=MESH)
            cp.start()
            cps.append(cp)
        for cp in cps:
            cp.wait()

    return _pc(body, name=name, in_specs=[HBM_SPEC] * nt, out_specs=[HBM_SPEC] * nt,
               out_shape=[SDS(g.shape[1:], g.dtype) for g in grads],
               scratch=[pltpu.SemaphoreType.DMA((nt,))] * 2)(*grads)


def add_halves(name, place, grad, got):
    _, n, r, c = grad.shape
    tr = _tile(r, 512)

    def body(place_ref, g_ref, o_ref, part_ref, slot_ref):
        val = (g_ref[...].astype(F32) + o_ref[...].astype(F32)).astype(BF16)
        part_ref[...] = val

        @pl.when(pl.program_id(1) == place_ref[1])
        def _():
            slot_ref[...] = val

    gs = pltpu.PrefetchScalarGridSpec(
        num_scalar_prefetch=1, grid=(r // tr, n),
        in_specs=[pl.BlockSpec((None, None, tr, c), lambda i, j, pr: (pr[0], j, i, 0)),
                  pl.BlockSpec((None, tr, c), lambda i, j, pr: (j, i, 0))],
        out_specs=[pl.BlockSpec((None, tr, c), lambda i, j, pr: (j, i, 0)),
                   pl.BlockSpec((None, tr, c), lambda i, j, pr: (pr[1], i, 0))])
    return _pc(body, name=name, grid_spec=gs, in_specs=None, out_specs=None, out_shape=[SDS((n, r, c), BF16)] * 2,
               sem=("parallel", "arbitrary"))(place, grad, got)


def _chips_copies(parts, slots, send, recv):
    x, y, c, chips = _place()
    cps = []
    for t in range(len(parts)):
        for k, (cx, cy) in enumerate(chips):
            cps.append(pltpu.make_async_remote_copy(src_ref=parts[t].at[2 * cx + cy], dst_ref=slots[t].at[2 * x + y],
                                                    send_sem=send.at[3 * t + k], recv_sem=recv.at[3 * t + k],
                                                    device_id=(cx, cy, c), device_id_type=MESH))
    return cps


def chips_start(name, parts, slots):
    nt = len(parts)
    n = 3 * nt

    def body(*refs):
        send, recv, token = refs[2 * nt], refs[2 * nt + 1], refs[-1]
        for cp in _chips_copies(refs[:nt], refs[nt:2 * nt], send, recv):
            cp.start()
        token[...] = jnp.zeros(TOKEN_SHAPE, F32)

    both = list(parts) + list(slots)
    out = _split_call(
        body, name=name, in_specs=[HBM_SPEC] * (2 * nt),
        out_specs=(SEM_SPEC, SEM_SPEC) + (HBM_SPEC,) * (2 * nt) + (pl.BlockSpec(memory_space=pltpu.VMEM),),
        out_shape=(pltpu.SemaphoreType.DMA((n,)), pltpu.SemaphoreType.DMA((n,)))
        + tuple(pltpu.HBM(b.shape, b.dtype) for b in both) + (SDS(TOKEN_SHAPE, F32),),
        aliases={t: 2 + t for t in range(2 * nt)})(*_in_hbm(both))
    return out[0], out[1], list(out[2:2 + nt]), list(out[2 + nt:2 + 2 * nt]), out[-1]


def chips_wait(name, send, recv, parts, slots, after):
    nt = len(parts)

    def body(*refs):
        for cp in _chips_copies(refs[:nt], refs[nt:2 * nt], refs[2 * nt], refs[2 * nt + 1]):
            cp.wait_send()
            cp.wait_recv()

    both = list(parts) + list(slots)
    out = _split_call(
        body, name=name, in_specs=[HBM_SPEC] * (2 * nt) + [SEM_SPEC, SEM_SPEC, pl.BlockSpec(memory_space=pl.ANY)],
        out_specs=(HBM_SPEC,) * (2 * nt), out_shape=tuple(pltpu.HBM(b.shape, b.dtype) for b in both),
        aliases={t: t for t in range(2 * nt)})(*both, send, recv, after)
    return list(out[nt:])


def sum_chips(name, place, slots):
    n, r, c = slots.shape
    tr = _tile(r, 512)

    def body(place_ref, s_ref, o_ref):
        acc = s_ref[0].astype(F32)
        for k in range(1, n):
            acc = acc + s_ref[k].astype(F32)
        o_ref[...] = acc

    gs = pltpu.PrefetchScalarGridSpec(
        num_scalar_prefetch=1, grid=(r // tr,),
        in_specs=[pl.BlockSpec((n, tr, c), lambda i, pr: (0, i, 0))],
        out_specs=pl.BlockSpec((None, tr, c), lambda i, pr: (pr[0], i, 0)))
    return _pc(body, name=name, grid_spec=gs, in_specs=None, out_specs=None, out_shape=SDS((2, r, c), F32),
               sem=("parallel",))(place, slots)


def share_with_sibling(name, bufs):
    nt = len(bufs)

    def body(*refs):
        ins, outs = refs[:nt], refs[nt:2 * nt]
        send, recv = refs[2 * nt:]
        x, y, c, _ = _place()
        cps = []
        for t in range(nt):
            cp = pltpu.make_async_remote_copy(src_ref=ins[t].at[c], dst_ref=outs[t].at[c], send_sem=send.at[t], recv_sem=recv.at[t],
                                              device_id=(x, y, 1 - c), device_id_type=MESH)
            cp.start()
            cps.append(cp)
        for t in range(nt):
            theirs = outs[t].at[1 - c]
            pltpu.make_async_remote_copy(src_ref=theirs, dst_ref=theirs, send_sem=send.at[t], recv_sem=recv.at[t],
                                         device_id=(x, y, 1 - c), device_id_type=MESH).wait_recv()
        for cp in cps:
            cp.wait_send()

    return _pc(body, name=name, in_specs=[HBM_SPEC] * nt, out_specs=[HBM_SPEC] * nt,
               out_shape=[SDS(b.shape, F32) for b in bufs], scratch=[pltpu.SemaphoreType.DMA((nt,))] * 2,
               input_output_aliases={t: t for t in range(nt)})(*bufs)


def _adamw(w, g, m, v):
    m = ADAM_B1 * m + (1.0 - ADAM_B1) * g
    v = ADAM_B2 * v + (1.0 - ADAM_B2) * (g * g)
    m_hat = m / (1.0 - ADAM_B1 ** ADAM_STEP)
    v_hat = v / (1.0 - ADAM_B2 ** ADAM_STEP)
    delta = -ADAM_LR * (m_hat / (jnp.sqrt(v_hat) + ADAM_EPS) + ADAM_WD * w)
    return delta, m, v


def adamw_big(name, w, m, v, g0, g1, axis):
    _, r, c = w.shape
    _, rh, ch = g0.shape
    tr = _tile(rh, 256)
    nb = rh // tr
    if axis == 0:
        wspec = pl.BlockSpec((None, tr, ch), lambda l, h, i: (l, h * nb + i, 0))
    else:
        wspec = pl.BlockSpec((None, tr, ch), lambda l, h, i: (l, i, h))
    g0spec = pl.BlockSpec((None, tr, ch), lambda l, h, i: (h * (1 - l), i * (1 - l), 0))
    g1spec = pl.BlockSpec((None, tr, ch), lambda l, h, i: (h * l, i * l, 0))

    def body(w_ref, m_ref, v_ref, g0_ref, g1_ref, go_ref, d_ref, mo_ref, vo_ref):
        def run(g_ref):
            g = g_ref[...]
            delta, mn, vn = _adamw(w_ref[...], g, m_ref[...], v_ref[...])
            go_ref[...] = g
            d_ref[...] = delta
            mo_ref[...] = mn
            vo_ref[...] = vn

        @pl.when(pl.program_id(0) == 0)
        def _():
            run(g0_ref)

        @pl.when(pl.program_id(0) == 1)
        def _():
            run(g1_ref)

    return _pc(body, name=name, grid=(2, 2, nb), in_specs=[wspec, wspec, wspec, g0spec, g1spec], out_specs=[wspec] * 4,
               out_shape=[SDS(w.shape, F32)] * 4, sem=("parallel", "parallel", "parallel"))(w, m, v, g0, g1)


SMALL = (("pre_mix_gain", 2048), ("post_mix_gain", 2048), ("pre_ffn_gain", 2048), ("post_ffn_gain", 2048), ("ple_gain", 2048),
         ("attn_out_gain", 1024), ("hgrn_out_gain", 1024), ("hgrn_lb_logits", 1024), ("attn_sinks", 128))
SMALL_ROWS = sum(2 * w // LANES for _, w in SMALL)
SMALL_PAD = -(-SMALL_ROWS // 8) * 8
LB_ROW = sum(2 * w // LANES for _, w in SMALL[:7])


def _pack_small(parts):
    rows = []
    for nm, w in SMALL:
        a = parts[nm].astype(F32)
        if a.shape[1] != w:
            a = jnp.pad(a, ((0, 0), (0, w - a.shape[1])))
        rows.append(a.reshape(2 * w // LANES, LANES))
    rows.append(jnp.zeros((SMALL_PAD - SMALL_ROWS, LANES), F32))
    return jnp.concatenate(rows, axis=0)


def _unpack_small(packed, widths):
    out, r = {}, 0
    for nm, w in SMALL:
        n = 2 * w // LANES
        out[nm] = packed[r:r + n].reshape(2, w)[:, :widths[nm]]
        r += n
    return out


def allreduce_small(name, packed):
    rows = packed.shape[0]

    def body(x_ref, o_ref, buf, send, recv, own_sem):
        x, y, c, _ = _place()
        me = 4 * x + 2 * y + c
        own = pltpu.make_async_copy(x_ref, buf.at[me], own_sem)
        own.start()
        cps = []
        for k in range(1, 8):
            px, py, pc = x ^ (k >> 2), y ^ ((k >> 1) & 1), c ^ (k & 1)
            cp = pltpu.make_async_remote_copy(src_ref=x_ref, dst_ref=buf.at[me], send_sem=send.at[k - 1], recv_sem=recv.at[k - 1],
                                              device_id=(px, py, pc), device_id_type=MESH)
            cp.start()
            cps.append(cp)
        for k in range(1, 8):
            px, py, pc = x ^ (k >> 2), y ^ ((k >> 1) & 1), c ^ (k & 1)
            slot = buf.at[4 * px + 2 * py + pc]
            pltpu.make_async_remote_copy(src_ref=slot, dst_ref=slot, send_sem=send.at[k - 1], recv_sem=recv.at[k - 1],
                                         device_id=(px, py, pc), device_id_type=MESH).wait_recv()
        for cp in cps:
            cp.wait_send()
        own.wait()
        acc = buf[0]
        for k in range(1, 8):
            acc = acc + buf[k]
        o_ref[...] = acc

    vm = pl.BlockSpec(memory_space=pltpu.VMEM)
    return _pc(body, name=name, in_specs=[vm], out_specs=vm, out_shape=SDS((rows, LANES), F32),
               scratch=[pltpu.VMEM((8, rows, LANES), F32), pltpu.SemaphoreType.DMA((7,)), pltpu.SemaphoreType.DMA((7,)),
                        pltpu.SemaphoreType.DMA])(packed)


def adamw_small(name, w, m, v, g):
    rows = w.shape[0]
    n = HGRN_WIDTH // LANES

    def body(w_ref, m_ref, v_ref, g_ref, go_ref, d_ref, mo_ref, vo_ref):
        go_ref[...] = g_ref[...]
        l0 = w_ref[LB_ROW:LB_ROW + n, :]
        l1 = w_ref[LB_ROW + n:LB_ROW + 2 * n, :]
        mx = jnp.maximum(l0, l1)
        e0, e1 = jnp.exp(l0 - mx), jnp.exp(l1 - mx)
        s0, s1 = e0 / (e0 + e1), e1 / (e0 + e1)
        dlb1 = g_ref[LB_ROW + n:LB_ROW + 2 * n, :]
        inner = s1 * dlb1
        go_ref[LB_ROW:LB_ROW + n, :] = s0 * (0.0 - inner)
        go_ref[LB_ROW + n:LB_ROW + 2 * n, :] = s1 * (dlb1 - inner)
        delta, mn, vn = _adamw(w_ref[...], go_ref[...], m_ref[...], v_ref[...])
        d_ref[...] = delta
        mo_ref[...] = mn
        vo_ref[...] = vn

    vm = pl.BlockSpec(memory_space=pltpu.VMEM)
    return _pc(body, name=name, in_specs=[vm] * 4, out_specs=[vm] * 4, out_shape=[SDS((rows, LANES), F32)] * 4)(w, m, v, g)


def _layer_fwd(l, x, h1, p_l, w_in_g, rest_of_weights, gains, cos, sin, sinks, lb_logits, g_next, target):
    n = f"l{l}_"
    proj = mm_col(n + "in_proj", h1, w_in_g)
    qp, kp, vp = rope_qkv(n + "rope_qkv", proj, cos, sin)
    attn = attn_fwd(n + "attn_fwd", qp, kp, vp, sinks)
    o, states = hgrn_fwd(n + "hgrn_fwd", proj, lb_logits, l)
    cat = mix_out_fwd(n + "mix_out_fwd", attn, o, proj, gains["attn_out_gain"], gains["hgrn_out_gain"])
    rest, token = rest_of_weights(cat)
    wts = dict(rest, w_in=w_in_g)
    if token is not None:
        gains = _with_token(gains, "post_mix_gain", token)
    m = mm_row(n + "out_proj", cat, wts["w_out"])
    x1, h2 = post_pre_norm(n + "post_mix", m, gains["post_mix_gain"], x, gains["pre_ffn_gain"])
    g, u, a = ffn_gate_up(n + "ffn_gate_up", h2, wts["w_ffn_gate"], wts["w_ffn_up"])
    f = mm_row(n + "ffn_down", a, wts["w_ffn_down"])
    x2, h3 = post_pre_norm(n + "post_ffn", f, gains["post_ffn_gain"], x1, gains["ple_gain"])
    z = mm_row(n + "ple_gate", h3, wts["w_ple_gate"])
    pp = mm_col(n + "ple_proj", p_l, wts["w_ple_proj"])
    if target is None:
        out = ple_fwd_mid(n + "ple_fwd", z, pp, x2, g_next)
    else:
        out = ple_fwd_loss(n + "ple_loss", z, pp, x2, target)
    saved = dict(x=x, h1=h1, proj=proj, qp=qp, kp=kp, vp=vp, attn=attn, o=o, states=states, cat=cat, m=m, x1=x1, h2=h2,
                 g=g, u=u, a=a, f=f, x2=x2, h3=h3, z=z, pp=pp, p=p_l)
    return out, saved, wts


EARLY = ("w_ple_gate", "w_ple_proj", "w_ffn_down", "w_ffn_gate", "w_ffn_up")
LATE = ("w_out", "w_in")


def _layer_bwd_ffn(l, dx3, sv, wts, gains):
    n = f"l{l}_"
    dpp, dz = ple_bwd(n + "ple_bwd", dx3, sv["z"], sv["pp"])
    dh3 = mm_row_t(n + "ple_gate_dx", dz, wts["w_ple_gate"])
    dx2, df, d_ple_gain, d_post_ffn = norm_bwd_pair(n + "ple_post_ffn_bwd", sv["x2"], gains["ple_gain"], dh3, dx3, sv["f"],
                                                    gains["post_ffn_gain"])
    dg, du = ffn_down_bwd(n + "ffn_down_bwd", df, wts["w_ffn_down"], sv["g"], sv["u"])
    big = dict(
        w_ple_gate=mm_wg_row(n + "ple_gate_dw", sv["h3"], dz),
        w_ple_proj=mm_wg_col(n + "ple_proj_dw", sv["p"], dpp),
        w_ffn_down=mm_wg_row(n + "ffn_down_dw", sv["a"], df),
        w_ffn_gate=mm_wg_col(n + "ffn_gate_dw", sv["h2"], dg),
        w_ffn_up=mm_wg_col(n + "ffn_up_dw", sv["h2"], du),
    )
    return dict(dx2=dx2, dg=dg, du=du), big, dict(ple_gain=d_ple_gain, post_ffn_gain=d_post_ffn)


def _layer_bwd_mix(l, st, sv, wts, gains, cos, sin, sinks, lb_logits):
    n = f"l{l}_"
    dh2 = mm_col_t(n + "ffn_gate_dx", st["dg"], wts["w_ffn_gate"])
    dh2 = mm_col_t(n + "ffn_up_dx", st["du"], wts["w_ffn_up"], add=dh2)
    dx1, dm, d_pre_ffn, d_post_mix = norm_bwd_pair(n + "pre_ffn_post_mix_bwd", sv["x1"], gains["pre_ffn_gain"], dh2, st["dx2"],
                                                   sv["m"], gains["post_mix_gain"])
    dcat = mm_row_t(n + "out_proj_dx", dm, wts["w_out"])
    dattn, do, dhg, d_attn_gain, d_hgrn_gain = mix_out_bwd(n + "mix_out_bwd", dcat, sv["attn"], sv["o"], sv["proj"],
                                                            gains["attn_out_gain"], gains["hgrn_out_gain"])
    dqp, dkc, dkp, dvc, dvp, dsinks = attn_bwd(n + "attn_bwd", sv["qp"], sv["kp"], sv["vp"], sinks, dattn)
    dqkv = rope_bwd(n + "rope_bwd", dqp, dkc, dkp, dvc, dvp, cos, sin)
    dhq, dhf, dhi, dlb = hgrn_bwd(n + "hgrn_bwd", sv["proj"], lb_logits, l, sv["states"], do)
    dproj = jnp.concatenate([dqkv, dhq, dhf, dhi, dhg], axis=1)
    dh1 = mm_col_t(n + "in_proj_dx", dproj, wts["w_in"])
    dx, d_pre_mix = norm_bwd(n + "pre_mix_bwd", sv["x"], gains["pre_mix_gain"], dh1, dx1)
    big = dict(w_out=mm_wg_row(n + "out_proj_dw", sv["cat"], dm), w_in=mm_wg_col(n + "in_proj_dw", sv["h1"], dproj))
    small = dict(pre_mix_gain=d_pre_mix, post_mix_gain=d_post_mix, pre_ffn_gain=d_pre_ffn, attn_out_gain=d_attn_gain,
                 hgrn_out_gain=d_hgrn_gain, hgrn_lb_logits=dlb, attn_sinks=dsinks)
    return dx, big, small


def _layer_bwd(l, dx3, sv, wts, gains, cos, sin, sinks, lb_logits):
    st, early, small_a = _layer_bwd_ffn(l, dx3, sv, wts, gains)
    dx, late, small_b = _layer_bwd_mix(l, st, sv, wts, gains, cos, sin, sinks, lb_logits)
    return dx, {**early, **late}, {**small_a, **small_b}


def _reduce_start(tag, names, big, place):
    got = reduce_to_sibling(tag + "_reduce_to_sibling", [big[nm] for nm in names])
    pairs = [add_halves(f"{tag}_add_{nm}", place, big[nm], gt) for nm, gt in zip(names, got)]
    return chips_start(tag + "_chips_start", [pr[0] for pr in pairs], [pr[1] for pr in pairs])


def _reduce_finish(tag, names, started, place, after):
    send, recv, parts, slots, _ = started
    slots = chips_wait(tag + "_chips_wait", send, recv, parts, slots, after)
    bufs = [sum_chips(f"{tag}_sum_{nm}", place, sl) for nm, sl in zip(names, slots)]
    return dict(zip(names, share_with_sibling(tag + "_share_with_sibling", bufs)))


def _with_token(gains, name, token):
    out = dict(gains)
    out[name] = gains[name] + token[0, 0]
    return out


def kernel(x, p, positions, w_in, attn_sinks, hgrn_lb_logits, attn_out_gain, hgrn_out_gain, w_out, pre_mix_gain, post_mix_gain, pre_ffn_gain, post_ffn_gain, w_ffn_gate, w_ffn_up, w_ffn_down, ple_gain, w_ple_gate, w_ple_proj, loss_target, m_w_in, m_attn_sinks, m_hgrn_lb_logits, m_attn_out_gain, m_hgrn_out_gain, m_w_out, m_pre_mix_gain, m_post_mix_gain, m_pre_ffn_gain, m_post_ffn_gain, m_w_ffn_gate, m_w_ffn_up, m_w_ffn_down, m_ple_gain, m_w_ple_gate, m_w_ple_proj, v_w_in, v_attn_sinks, v_hgrn_lb_logits, v_attn_out_gain, v_hgrn_out_gain, v_w_out, v_pre_mix_gain, v_post_mix_gain, v_pre_ffn_gain, v_post_ffn_gain, v_w_ffn_gate, v_w_ffn_up, v_w_ffn_down, v_ple_gain, v_w_ple_gate, v_w_ple_proj):
    given = dict(locals())
    depth = 2
    place = jnp.stack([lax.axis_index("c"), 2 * lax.axis_index("x") + lax.axis_index("y")]).astype(jnp.int32)
    xs = x[0]
    tgt = loss_target[0]
    pos_col = positions.reshape(-1, 1)
    half = 32
    inv_freq = ROPE_THETA ** (-jnp.arange(half, dtype=F32) / half)
    inv_freq = jnp.tile(inv_freq, 4).reshape(1, LANES)
    gains = [{nm: given[nm][l:l + 1] for nm, _ in SMALL[:7]} for l in range(depth)]
    names = [nm for nm, *_ in BIG]
    first, others = names[:1], names[1:]

    def specs(nms):
        return [BIG_BY_NAME[nm] for nm in nms]

    def start_gather(tag, l, nms, after):
        return gather_start(tag + "_gather_start", specs(nms),
                            [cast_to_slot(f"{tag}_cast_{nm}", place, given[nm], l, after) for nm in nms])

    def finish_gather(tag, nms, started, after):
        bufs = gather_wait(tag + "_gather_wait", specs(nms), started[0], started[1], started[2], after)
        return dict(zip(nms, gather_pass(tag + "_gather_pass", specs(nms), bufs)))

    g0a = start_gather("l0a", 0, first, place)
    g0b = start_gather("l0b", 0, others, g0a[3])
    started = {}

    def rest_of_layer0(after):
        got = finish_gather("l0b", others, g0b, after)
        started["l1"] = start_gather("l1", 1, names, got["w_out"])
        return got, started["l1"][3]

    cos, sin = rope_tables("rope_tables", pos_col, inv_freq)
    h1 = pre_norm("l0_pre_mix", xs, gains[0]["pre_mix_gain"])
    w_in0 = finish_gather("l0a", first, g0a, g0b[3])["w_in"]
    (x_mid, h1_next), sv0, wts0 = _layer_fwd(0, xs, h1, p[0, 0], w_in0, rest_of_layer0, gains[0], cos, sin, attn_sinks[0],
                                             hgrn_lb_logits, gains[1]["pre_mix_gain"], None)
    wts1 = finish_gather("l1", names, started["l1"], x_mid)
    (dy, loss_part), sv1, _ = _layer_fwd(1, x_mid, h1_next, p[1, 0], wts1["w_in"], lambda after: (wts1, None), gains[1], cos, sin,
                                         attn_sinks[1], hgrn_lb_logits, None, tgt)

    dx_mid, big1, small1 = _layer_bwd(1, dy, sv1, wts1, gains[1], cos, sin, attn_sinks[1], hgrn_lb_logits)
    r1 = _reduce_start("l1", names, big1, place)
    st0, early0, small0 = _layer_bwd_ffn(0, dx_mid, sv0, wts0, _with_token(gains[0], "ple_gain", r1[4]))
    r0e = _reduce_start("l0e", EARLY, early0, place)
    dx0, late0, small0b = _layer_bwd_mix(0, st0, sv0, wts0, _with_token(gains[0], "pre_ffn_gain", r0e[4]), cos, sin,
                                         attn_sinks[0], hgrn_lb_logits)
    small0 = {**small0, **small0b}
    r0l = _reduce_start("l0l", LATE, late0, place)
    red1 = _reduce_finish("l1", names, r1, place, r0l[4])
    red0 = _reduce_finish("l0e", EARLY, r0e, place, red1[names[-1]])

    loss = lax.psum(loss_part[0, 0], ("x", "y", "c"))
    grad_x = dx0[None]

    out_big = {}
    for nm in EARLY:
        out_big[nm] = adamw_big("adamw_" + nm, given[nm], given["m_" + nm], given["v_" + nm], red0[nm], red1[nm], BIG_BY_NAME[nm][3])
    red0.update(_reduce_finish("l0l", LATE, r0l, place, out_big[EARLY[-1]][3]))
    for nm in LATE:
        out_big[nm] = adamw_big("adamw_" + nm, given[nm], given["m_" + nm], given["v_" + nm], red0[nm], red1[nm], BIG_BY_NAME[nm][3])

    widths = {nm: given[nm].shape[1] for nm, _ in SMALL}
    small_g = {nm: jnp.concatenate([small0[nm][:, :widths[nm]] if nm != "attn_sinks" else small0[nm][:, :LANES],
                                    small1[nm][:, :widths[nm]] if nm != "attn_sinks" else small1[nm][:, :LANES]], axis=0)
               for nm, _ in SMALL}
    g_sum = allreduce_small("allreduce_small", _pack_small(small_g))
    sm = adamw_small("adamw_small", _pack_small({nm: given[nm] for nm, _ in SMALL}),
                     _pack_small({nm: given["m_" + nm] for nm, _ in SMALL}),
                     _pack_small({nm: given["v_" + nm] for nm, _ in SMALL}), g_sum)
    out_small = [_unpack_small(a, widths) for a in sm]

    order = ["w_in", "attn_sinks", "hgrn_lb_logits", "attn_out_gain", "hgrn_out_gain", "w_out", "pre_mix_gain", "post_mix_gain",
             "pre_ffn_gain", "post_ffn_gain", "w_ffn_gate", "w_ffn_up", "w_ffn_down", "ple_gain", "w_ple_gate", "w_ple_proj"]
    res = [loss, grad_x]
    for k in range(4):
        for nm in order:
            res.append(out_big[nm][k] if nm in out_big else out_small[k][nm])
    return tuple(res)
```

```python
import functools

import jax
import jax.numpy as jnp
from jax import lax
from jax.experimental import pallas as pl
from jax.experimental.pallas import tpu as pltpu

F32, BF16 = jnp.float32, jnp.bfloat16
SDS = jax.ShapeDtypeStruct
MESH = pl.DeviceIdType.MESH

D_MODEL = 2048
ATTN_WIDTH = 1024
HGRN_WIDTH = 1024
KV_WIDTH = 256
N_Q_HEADS = 16
N_KV_HEADS = 4
Q_PER_KV = 4
WINDOW = 128
MASK_VALUE = -1e30
ROPE_THETA = 10000.0
HGRN_HEADS = 8
HGRN_CHUNK = 16
D_FF = 5632
D_PLE = 256
RMS_EPS = 1e-6
LANES = 128
N_CHIPS = 4
COL_Q, COL_K, COL_V, COL_HQ, COL_HF, COL_HI, COL_HG = 0, 8, 10, 12, 20, 28, 36

ADAM_LR, ADAM_B1, ADAM_B2, ADAM_EPS, ADAM_WD, ADAM_STEP = 0.001, 0.9, 0.999, 1e-08, 0.01, 10

VMEM_LIMIT = 56 * 1024 * 1024
ROW_TILE = 256

_NN = (((1,), (0,)), ((), ()))
_NT = (((1,), (1,)), ((), ()))
_TN = (((0,), (0,)), ((), ()))


def _pc(body, *, name, out_shape, in_specs, out_specs, grid=(), scratch=(), sem=None, grid_spec=None, **kw):
    params = dict(vmem_limit_bytes=VMEM_LIMIT)
    if sem is not None:
        params["dimension_semantics"] = sem
    if grid_spec is not None:
        return pl.pallas_call(body, name=name, out_shape=out_shape, grid_spec=grid_spec,
                              compiler_params=pltpu.CompilerParams(**params), **kw)
    return pl.pallas_call(body, name=name, out_shape=out_shape, grid=grid, in_specs=in_specs, out_specs=out_specs,
                          scratch_shapes=list(scratch), compiler_params=pltpu.CompilerParams(**params), **kw)


def _sigmoid(x):
    return 1.0 / (1.0 + jnp.exp(-x))


def _rstd(x):
    return lax.rsqrt(jnp.mean(x * x, axis=-1, keepdims=True) + RMS_EPS)


def _rows(t, w, col=0):
    return pl.BlockSpec((t, w), lambda i, col=col: (i, col))


def _fixed(shape):
    return pl.BlockSpec(shape, lambda *_: (0,) * len(shape))


def _mm(name, a, b, *, dims, grid, a_spec, b_spec, o_spec, out_shape, parts=1, add=None, add_spec=None):
    def body(*refs):
        a_ref, b_ref, o_ref = refs[0], refs[1], refs[-1]
        if parts == 1:
            r = lax.dot_general(a_ref[...].astype(BF16), b_ref[...].astype(BF16), dims, preferred_element_type=F32)
        else:
            w = a_ref.shape[1] // parts
            r = None
            for j in range(parts):
                t = lax.dot_general(a_ref[:, j * w:(j + 1) * w].astype(BF16), b_ref[j].astype(BF16), dims,
                                    preferred_element_type=F32)
                r = t if r is None else r + t
        if add is not None:
            r = r + refs[2][...]
        o_ref[...] = r.astype(o_ref.dtype)

    ins = [a, b] + ([] if add is None else [add])
    specs = [a_spec, b_spec] + ([] if add is None else [add_spec])
    return _pc(body, name=name, grid=grid, in_specs=specs, out_specs=o_spec, out_shape=out_shape,
               sem=("parallel",) * len(grid))(*ins)


def _tile(n, t):
    if n <= t:
        return n
    while n % t:
        t //= 2
    assert t % 8 == 0
    return t


def mm_col(name, a, wg, out_dtype=F32):
    s, k = a.shape
    _, _, n = wg.shape
    tm = _tile(s, 512)
    return _mm(name, a, wg, dims=_NN, grid=(N_CHIPS, s // tm),
               a_spec=pl.BlockSpec((tm, k), lambda j, i: (i, 0)),
               b_spec=pl.BlockSpec((None, k, n), lambda j, i: (j, 0, 0)),
               o_spec=pl.BlockSpec((tm, n), lambda j, i: (i, j)),
               out_shape=SDS((s, N_CHIPS * n), out_dtype))


def mm_row(name, a, wg, out_dtype=F32):
    s, _ = a.shape
    _, r, n = wg.shape
    tm = _tile(s, 512)
    tn = _tile(n, 1024 if r > 512 else 2048)
    return _mm(name, a, wg, dims=_NN, grid=(n // tn, s // tm), parts=N_CHIPS,
               a_spec=pl.BlockSpec((tm, N_CHIPS * r), lambda j, i: (i, 0)),
               b_spec=pl.BlockSpec((N_CHIPS, r, tn), lambda j, i: (0, 0, j)),
               o_spec=pl.BlockSpec((tm, tn), lambda j, i: (i, j)),
               out_shape=SDS((s, n), out_dtype))


def mm_col_t(name, dy, wg, add=None, out_dtype=F32):
    s, _ = dy.shape
    _, k, n = wg.shape
    tm = _tile(s, 512)
    tk = _tile(k, 1024)
    return _mm(name, dy, wg, dims=_NT, grid=(k // tk, s // tm), parts=N_CHIPS,
               a_spec=pl.BlockSpec((tm, N_CHIPS * n), lambda j, i: (i, 0)),
               b_spec=pl.BlockSpec((N_CHIPS, tk, n), lambda j, i: (0, j, 0)),
               o_spec=pl.BlockSpec((tm, tk), lambda j, i: (i, j)),
               add=add, add_spec=pl.BlockSpec((tm, tk), lambda j, i: (i, j)),
               out_shape=SDS((s, k), out_dtype))


def mm_row_t(name, dy, wg, out_dtype=F32):
    s, n = dy.shape
    _, r, _ = wg.shape
    tm = _tile(s, 512)
    return _mm(name, dy, wg, dims=_NT, grid=(N_CHIPS, s // tm),
               a_spec=pl.BlockSpec((tm, n), lambda j, i: (i, 0)),
               b_spec=pl.BlockSpec((None, r, n), lambda j, i: (j, 0, 0)),
               o_spec=pl.BlockSpec((tm, r), lambda j, i: (i, j)),
               out_shape=SDS((s, N_CHIPS * r), out_dtype))


def norm_bwd_pair(name, x_a, gain_a, dy, dres, x_b, gain_b):
    s, d = x_a.shape
    t = _tile(s, ROW_TILE)

    def one(xv, g, dyv):
        r = _rstd(xv)
        xh = xv * r
        dyg = dyv * g
        return r * (dyg - xh * jnp.mean(dyg * xh, axis=-1, keepdims=True)), jnp.sum(dyv * xh, axis=0, keepdims=True)

    def body(xa_ref, ga_ref, dy_ref, r_ref, xb_ref, gb_ref, dx_ref, db_ref, dga_ref, dgb_ref):
        dx, pa = one(xa_ref[...], ga_ref[...], dy_ref[...])
        dx = dx + r_ref[...]
        dx_ref[...] = dx
        db, pb = one(xb_ref[...], gb_ref[...], dx)
        db_ref[...] = db.astype(BF16)

        @pl.when(pl.program_id(0) == 0)
        def _():
            dga_ref[...] = pa
            dgb_ref[...] = pb

        @pl.when(pl.program_id(0) > 0)
        def _():
            dga_ref[...] += pa
            dgb_ref[...] += pb

    row, gain = _rows(t, d), _fixed((1, d))
    return _pc(body, name=name, grid=(s // t,), in_specs=[row, gain, row, row, row, gain], out_specs=[row, row, gain, gain],
               out_shape=[SDS((s, d), F32), SDS((s, d), BF16), SDS((1, d), F32), SDS((1, d), F32)],
               sem=("arbitrary",))(x_a, gain_a, dy, dres, x_b, gain_b)


def ffn_gate_up(name, h, wg_gate, wg_up):
    s, k = h.shape
    _, _, n = wg_gate.shape
    tm = _tile(s, 512)

    def body(h_ref, wg_ref, wu_ref, g_ref, u_ref, a_ref):
        hv = h_ref[...]
        g = jnp.dot(hv, wg_ref[...], preferred_element_type=F32)
        u = jnp.dot(hv, wu_ref[...], preferred_element_type=F32)
        g_ref[...] = g
        u_ref[...] = u
        a_ref[...] = ((g * _sigmoid(g)) * u).astype(BF16)

    wspec = pl.BlockSpec((None, k, n), lambda j, i: (j, 0, 0))
    ospec = pl.BlockSpec((tm, n), lambda j, i: (i, j))
    return _pc(body, name=name, grid=(N_CHIPS, s // tm), in_specs=[pl.BlockSpec((tm, k), lambda j, i: (i, 0)), wspec, wspec],
               out_specs=[ospec] * 3, out_shape=[SDS((s, N_CHIPS * n), F32)] * 2 + [SDS((s, N_CHIPS * n), BF16)],
               sem=("parallel", "parallel"))(h, wg_gate, wg_up)


def ffn_down_bwd(name, df, wg_down, g, u):
    s, n = df.shape
    _, r, _ = wg_down.shape
    tm = _tile(s, 512)

    def body(df_ref, w_ref, g_ref, u_ref, dg_ref, du_ref):
        da = lax.dot_general(df_ref[...], w_ref[...], _NT, preferred_element_type=F32)
        gv = g_ref[...]
        sg = _sigmoid(gv)
        du_ref[...] = (da * (gv * sg)).astype(BF16)
        dg_ref[...] = ((da * u_ref[...]) * (sg * (1.0 + gv * (1.0 - sg)))).astype(BF16)

    cspec = pl.BlockSpec((tm, r), lambda j, i: (i, j))
    return _pc(body, name=name, grid=(N_CHIPS, s // tm),
               in_specs=[pl.BlockSpec((tm, n), lambda j, i: (i, 0)), pl.BlockSpec((None, r, n), lambda j, i: (j, 0, 0)), cspec, cspec],
               out_specs=[cspec] * 2, out_shape=[SDS((s, N_CHIPS * r), BF16)] * 2,
               sem=("parallel", "parallel"))(df, wg_down, g, u)


def mm_wg_col(name, a, dy):
    s, k = a.shape
    n = dy.shape[1] // N_CHIPS
    tm = _tile(k // 2, 512)
    hb = (k // 2) // tm
    return _mm(name, a, dy, dims=_TN, grid=(N_CHIPS, k // tm),
               a_spec=pl.BlockSpec((s, tm), lambda j, i: (0, i)),
               b_spec=pl.BlockSpec((s, n), lambda j, i: (0, j)),
               o_spec=pl.BlockSpec((None, None, tm, n), lambda j, i: (i // hb, j, i % hb, 0)),
               out_shape=SDS((2, N_CHIPS, k // 2, n), BF16))


def mm_wg_row(name, a, dy):
    s, n = dy.shape
    r = a.shape[1] // N_CHIPS
    tn = _tile(n // 2, 512)
    nb = (n // 2) // tn
    return _mm(name, a, dy, dims=_TN, grid=(N_CHIPS, n // tn),
               a_spec=pl.BlockSpec((s, r), lambda j, i: (0, j)),
               b_spec=pl.BlockSpec((s, tn), lambda j, i: (0, i)),
               o_spec=pl.BlockSpec((None, None, r, tn), lambda j, i: (i // nb, j, 0, i % nb)),
               out_shape=SDS((2, N_CHIPS, r, n // 2), BF16))


def pre_norm(name, x, gain):
    s, d = x.shape
    t = _tile(s, ROW_TILE)

    def body(x_ref, g_ref, o_ref):
        xv = x_ref[...]
        o_ref[...] = ((xv * _rstd(xv)) * g_ref[...]).astype(BF16)

    return _pc(body, name=name, grid=(s // t,), in_specs=[_rows(t, d), _fixed((1, d))], out_specs=_rows(t, d),
               out_shape=SDS((s, d), BF16), sem=("parallel",))(x, gain)


def post_pre_norm(name, m, g_post, res, g_pre):
    s, d = m.shape
    t = _tile(s, ROW_TILE)

    def body(m_ref, gp_ref, r_ref, gn_ref, x_ref, h_ref):
        mv = m_ref[...]
        xn = r_ref[...] + (mv * _rstd(mv)) * gp_ref[...]
        x_ref[...] = xn
        h_ref[...] = ((xn * _rstd(xn)) * gn_ref[...]).astype(BF16)

    return _pc(body, name=name, grid=(s // t,),
               in_specs=[_rows(t, d), _fixed((1, d)), _rows(t, d), _fixed((1, d))],
               out_specs=[_rows(t, d), _rows(t, d)], out_shape=[SDS((s, d), F32), SDS((s, d), BF16)],
               sem=("parallel",))(m, g_post, res, g_pre)


def _row_dot(a_ref, w_ref):
    r = w_ref.shape[1]
    out = None
    for j in range(N_CHIPS):
        part = jnp.dot(a_ref[:, j * r:(j + 1) * r], w_ref[j], preferred_element_type=F32)
        out = part if out is None else out + part
    return out


def _row_dot_specs(t, a, wg):
    return [_rows(t, a.shape[1]), _fixed(wg.shape)]


def out_proj_post_mix(name, a, wg, g_post, res, g_pre):
    s, d = res.shape
    t = _tile(s, ROW_TILE)

    def body(a_ref, w_ref, gp_ref, r_ref, gn_ref, m_ref, x_ref, h_ref):
        mv = _row_dot(a_ref, w_ref)
        m_ref[...] = mv
        xn = r_ref[...] + (mv * _rstd(mv)) * gp_ref[...]
        x_ref[...] = xn
        h_ref[...] = ((xn * _rstd(xn)) * gn_ref[...]).astype(BF16)

    return _pc(body, name=name, grid=(s // t,),
               in_specs=_row_dot_specs(t, a, wg) + [_fixed((1, d)), _rows(t, d), _fixed((1, d))],
               out_specs=[_rows(t, d)] * 3, out_shape=[SDS((s, d), F32), SDS((s, d), F32), SDS((s, d), BF16)],
               sem=("parallel",))(a, wg, g_post, res, g_pre)


def ple_gate_fwd_mid(name, a, wg, pp, x2, g_next):
    s, d = x2.shape
    t = _tile(s, ROW_TILE)

    def body(a_ref, w_ref, p_ref, x_ref, g_ref, z_ref, xo_ref, h_ref):
        z = _row_dot(a_ref, w_ref)
        z_ref[...] = z
        xn = x_ref[...] + p_ref[...] * _sigmoid(z)
        xo_ref[...] = xn
        h_ref[...] = ((xn * _rstd(xn)) * g_ref[...]).astype(BF16)

    return _pc(body, name=name, grid=(s // t,),
               in_specs=_row_dot_specs(t, a, wg) + [_rows(t, d), _rows(t, d), _fixed((1, d))],
               out_specs=[_rows(t, d)] * 3, out_shape=[SDS((s, d), F32), SDS((s, d), F32), SDS((s, d), BF16)],
               sem=("parallel",))(a, wg, pp, x2, g_next)


def ple_gate_fwd_loss(name, a, wg, pp, x2, target):
    s, d = x2.shape
    t = _tile(s, ROW_TILE)

    def body(a_ref, w_ref, p_ref, x_ref, t_ref, z_ref, dy_ref, l_ref):
        z = _row_dot(a_ref, w_ref)
        z_ref[...] = z
        err = (x_ref[...] + p_ref[...] * _sigmoid(z)) - t_ref[...]
        dy_ref[...] = err * (1.0 / d)
        part = jnp.sum(jnp.sum(err * err, axis=-1, keepdims=True), axis=0, keepdims=True) * (0.5 / d)

        @pl.when(pl.program_id(0) == 0)
        def _():
            l_ref[...] = part

        @pl.when(pl.program_id(0) > 0)
        def _():
            l_ref[...] += part

    return _pc(body, name=name, grid=(s // t,),
               in_specs=_row_dot_specs(t, a, wg) + [_rows(t, d), _rows(t, d), _rows(t, d)],
               out_specs=[_rows(t, d), _rows(t, d), _fixed((1, 1))],
               out_shape=[SDS((s, d), F32), SDS((s, d), F32), SDS((1, 1), F32)],
               sem=("arbitrary",))(a, wg, pp, x2, target)


def ple_bwd(name, dx3, z, pp):
    s, d = z.shape
    t = _tile(s, ROW_TILE)

    def body(d_ref, z_ref, p_ref, dpp_ref, dz_ref):
        gate = _sigmoid(z_ref[...])
        dv = d_ref[...]
        dpp_ref[...] = (dv * gate).astype(BF16)
        dz_ref[...] = ((dv * p_ref[...]) * (gate * (1.0 - gate))).astype(BF16)

    return _pc(body, name=name, grid=(s // t,), in_specs=[_rows(t, d)] * 3, out_specs=[_rows(t, d)] * 2,
               out_shape=[SDS((s, d), BF16)] * 2, sem=("parallel",))(dx3, z, pp)


def norm_bwd(name, xin, gain, dy, dres=None, out_dtype=F32):
    s, d = xin.shape
    t = _tile(s, ROW_TILE)

    def body(*refs):
        if dres is None:
            x_ref, g_ref, dy_ref, dx_ref, dg_ref = refs
            r_ref = None
        else:
            x_ref, g_ref, dy_ref, r_ref, dx_ref, dg_ref = refs
        xv = x_ref[...]
        r = _rstd(xv)
        xh = xv * r
        dyv = dy_ref[...].astype(F32)
        dyg = dyv * g_ref[...]
        c = jnp.mean(dyg * xh, axis=-1, keepdims=True)
        dx = r * (dyg - xh * c)
        if r_ref is not None:
            dx = dx + r_ref[...]
        dx_ref[...] = dx.astype(out_dtype)
        part = jnp.sum(dyv * xh, axis=0, keepdims=True)

        @pl.when(pl.program_id(0) == 0)
        def _():
            dg_ref[...] = part

        @pl.when(pl.program_id(0) > 0)
        def _():
            dg_ref[...] += part

    ins = [xin, gain, dy] + ([] if dres is None else [dres])
    specs = [_rows(t, d), _fixed((1, d)), _rows(t, d)] + ([] if dres is None else [_rows(t, d)])
    return _pc(body, name=name, grid=(s // t,), in_specs=specs, out_specs=[_rows(t, d), _fixed((1, d))],
               out_shape=[SDS((s, d), out_dtype), SDS((1, d), F32)], sem=("arbitrary",))(*ins)


def _lane(shape):
    return lax.broadcasted_iota(jnp.int32, shape, 1)


def _swap_halves(x):
    lo = (_lane(x.shape) % 64) < 32
    return jnp.where(lo, pltpu.roll(x, 96, 1), pltpu.roll(x, 32, 1))


def rope_tables(name, pos_col, inv_freq):
    s = pos_col.shape[0]
    t = _tile(s, ROW_TILE)

    def body(p_ref, f_ref, c_ref, s_ref):
        ang = p_ref[...].astype(F32) * f_ref[...]
        lo = (_lane(ang.shape) % 64) < 32
        c_ref[...] = jnp.cos(ang)
        sn = jnp.sin(ang)
        s_ref[...] = jnp.where(lo, -sn, sn)

    return _pc(body, name=name, grid=(s // t,), in_specs=[_rows(t, 1), _fixed((1, LANES))],
               out_specs=[_rows(t, LANES)] * 2, out_shape=[SDS((s, LANES), F32)] * 2, sem=("parallel",))(pos_col, inv_freq)


def _pad_heads(chunk, lo_mask):
    zero = jnp.zeros_like(chunk)
    return jnp.where(lo_mask, chunk, zero), jnp.where(lo_mask, pltpu.roll(chunk, 64, 1), zero)


def rope_qkv(name, proj, cos, sin):
    s = proj.shape[0]
    t = _tile(s, ROW_TILE)

    def body(q_ref, kv_ref, c_ref, s_ref, qp_ref, kp_ref, vp_ref):
        cs, sn = c_ref[...], s_ref[...]
        lo_mask = _lane(cs.shape) < 64

        def rot(x):
            return x * cs + _swap_halves(x) * sn

        for j in range(ATTN_WIDTH // LANES):
            a, b = _pad_heads(rot(q_ref[:, j * LANES:(j + 1) * LANES]), lo_mask)
            qp_ref[:, (2 * j) * LANES:(2 * j + 1) * LANES] = a.astype(BF16)
            qp_ref[:, (2 * j + 1) * LANES:(2 * j + 2) * LANES] = b.astype(BF16)
        for j in range(KV_WIDTH // LANES):
            a, b = _pad_heads(rot(kv_ref[:, j * LANES:(j + 1) * LANES]), lo_mask)
            kp_ref[:, (2 * j) * LANES:(2 * j + 1) * LANES] = a.astype(BF16)
            kp_ref[:, (2 * j + 1) * LANES:(2 * j + 2) * LANES] = b.astype(BF16)
            a, b = _pad_heads(kv_ref[:, KV_WIDTH + j * LANES:KV_WIDTH + (j + 1) * LANES], lo_mask)
            vp_ref[:, (2 * j) * LANES:(2 * j + 1) * LANES] = a.astype(BF16)
            vp_ref[:, (2 * j + 1) * LANES:(2 * j + 2) * LANES] = b.astype(BF16)

    return _pc(body, name=name, grid=(s // t,),
               in_specs=[_rows(t, ATTN_WIDTH, 0), _rows(t, 2 * KV_WIDTH, 2), _rows(t, LANES), _rows(t, LANES)],
               out_specs=[_rows(t, N_Q_HEADS * LANES), _rows(t, N_KV_HEADS * LANES), _rows(t, N_KV_HEADS * LANES)],
               out_shape=[SDS((s, N_Q_HEADS * LANES), BF16), SDS((s, N_KV_HEADS * LANES), BF16),
                          SDS((s, N_KV_HEADS * LANES), BF16)],
               sem=("parallel",))(proj, proj, cos, sin)


def _attn_mask(n):
    L = WINDOW
    qi = lax.broadcasted_iota(jnp.int32, (L, 2 * L), 0) + L
    ki = lax.broadcasted_iota(jnp.int32, (L, 2 * L), 1)
    rel = qi - ki
    return (rel >= 0) & (rel < WINDOW) & ((n > 0) | (ki >= L))


def _attn_scores(qh, kk, valid):
    sc = lax.dot_general(qh, kk, _NT, preferred_element_type=F32) * 0.125
    return jnp.where(valid, sc, MASK_VALUE)


def _attn_softmax(sc, sink):
    m = jnp.maximum(jnp.max(sc, axis=-1, keepdims=True), sink)
    e = jnp.exp(sc - m)
    es = jnp.exp(sink - m)
    den = jnp.sum(e, axis=-1, keepdims=True) + es
    return e / den, es / den


def _attn_specs(s):
    L = WINDOW
    cur = lambda n: (n, 0)
    prev = lambda n: (jnp.maximum(n - 1, 0), 0)
    kvw = N_KV_HEADS * LANES
    return [pl.BlockSpec((L, N_Q_HEADS * LANES), cur), pl.BlockSpec((L, kvw), cur), pl.BlockSpec((L, kvw), prev),
            pl.BlockSpec((L, kvw), cur), pl.BlockSpec((L, kvw), prev), pl.BlockSpec(memory_space=pltpu.SMEM)]


def attn_fwd(name, qp, kp, vp, sinks):
    s = qp.shape[0]
    L = WINDOW

    def body(q_ref, kc_ref, kp_ref, vc_ref, vp_ref, sk_ref, o_ref):
        valid = _attn_mask(pl.program_id(0))
        kks, vvs = [], []
        for kvh in range(N_KV_HEADS):
            cols = slice(kvh * LANES, (kvh + 1) * LANES)
            kks.append(jnp.concatenate([kp_ref[:, cols], kc_ref[:, cols]], axis=0))
            vvs.append(jnp.concatenate([vp_ref[:, cols], vc_ref[:, cols]], axis=0))
        scs = [_attn_scores(q_ref[:, h * LANES:(h + 1) * LANES], kks[h // Q_PER_KV], valid) for h in range(N_Q_HEADS)]
        pbs = [_attn_softmax(scs[h], sk_ref[h])[0].astype(BF16) for h in range(N_Q_HEADS)]
        outs = [jnp.dot(pbs[h], vvs[h // Q_PER_KV], preferred_element_type=F32) for h in range(N_Q_HEADS)]
        for j in range(ATTN_WIDTH // LANES):
            o_ref[:, j * LANES:(j + 1) * LANES] = outs[2 * j] + pltpu.roll(outs[2 * j + 1], 64, 1)

    return _pc(body, name=name, grid=(s // L,), in_specs=_attn_specs(s),
               out_specs=pl.BlockSpec((L, ATTN_WIDTH), lambda n: (n, 0)),
               out_shape=SDS((s, ATTN_WIDTH), F32), sem=("parallel",))(qp, kp, kp, vp, vp, sinks)


def attn_bwd(name, qp, kp, vp, sinks, dattn):
    s = qp.shape[0]
    L = WINDOW
    kvw = N_KV_HEADS * LANES

    def body(q_ref, kc_ref, kp_ref, vc_ref, vp_ref, sk_ref, do_ref, dq_ref, dkc_ref, dkp_ref, dvc_ref, dvp_ref, ds_ref):
        n = pl.program_id(0)
        valid = _attn_mask(n)
        lo_mask = _lane((L, LANES)) < 64
        lane1 = _lane((1, LANES))
        dsink = jnp.zeros((1, LANES), F32)
        heads = range(N_Q_HEADS)
        kks, vvs = [], []
        for kvh in range(N_KV_HEADS):
            cols = slice(kvh * LANES, (kvh + 1) * LANES)
            kks.append(jnp.concatenate([kp_ref[:, cols], kc_ref[:, cols]], axis=0))
            vvs.append(jnp.concatenate([vp_ref[:, cols], vc_ref[:, cols]], axis=0))
        qs, dos, scs, dps = [], [], [], []
        for h in heads:
            qs.append(q_ref[:, h * LANES:(h + 1) * LANES])
            chunk = do_ref[:, (h // 2) * LANES:(h // 2 + 1) * LANES]
            if h % 2:
                chunk = pltpu.roll(chunk, 64, 1)
            dos.append(jnp.where(lo_mask, chunk, 0.0).astype(BF16))
            scs.append(_attn_scores(qs[h], kks[h // Q_PER_KV], valid))
            dps.append(lax.dot_general(dos[h], vvs[h // Q_PER_KV], _NT, preferred_element_type=F32))
        pbs, dsbs = [], []
        for h in heads:
            p, ps = _attn_softmax(scs[h], sk_ref[h])
            delta = jnp.sum(p * dps[h], axis=-1, keepdims=True)
            dsbs.append(((p * (dps[h] - delta)) * 0.125).astype(BF16))
            pbs.append(p.astype(BF16))
            dsink = dsink + jnp.where(lane1 == h, -jnp.sum(ps * delta, axis=0, keepdims=True), 0.0)
        for kvh in range(N_KV_HEADS):
            cols = slice(kvh * LANES, (kvh + 1) * LANES)
            dkk = jnp.zeros((2 * L, LANES), F32)
            dvv = jnp.zeros((2 * L, LANES), F32)
            for h in range(kvh * Q_PER_KV, (kvh + 1) * Q_PER_KV):
                dq_ref[:, h * LANES:(h + 1) * LANES] = jnp.dot(dsbs[h], kks[kvh], preferred_element_type=F32)
                dkk = dkk + lax.dot_general(dsbs[h], qs[h], _TN, preferred_element_type=F32)
                dvv = dvv + lax.dot_general(pbs[h], dos[h], _TN, preferred_element_type=F32)
            dkp_ref[:, cols] = dkk[:L]
            dkc_ref[:, cols] = dkk[L:]
            dvp_ref[:, cols] = dvv[:L]
            dvc_ref[:, cols] = dvv[L:]

        @pl.when(n == 0)
        def _():
            ds_ref[...] = dsink

        @pl.when(n > 0)
        def _():
            ds_ref[...] += dsink

    blk = lambda w: pl.BlockSpec((L, w), lambda n: (n, 0))
    return _pc(body, name=name, grid=(s // L,), in_specs=_attn_specs(s) + [blk(ATTN_WIDTH)],
               out_specs=[blk(N_Q_HEADS * LANES), blk(kvw), blk(kvw), blk(kvw), blk(kvw), _fixed((1, LANES))],
               out_shape=[SDS((s, N_Q_HEADS * LANES), F32)] + [SDS((s, kvw), F32)] * 4 + [SDS((1, LANES), F32)],
               sem=("arbitrary",))(qp, kp, kp, vp, vp, sinks, dattn)


def rope_bwd(name, dqp, dkc, dkp, dvc, dvp, cos, sin):
    s = dqp.shape[0]
    L = WINDOW
    nb = s // L
    kvw = N_KV_HEADS * LANES

    def body(dq_ref, dkc_ref, dkp_ref, dvc_ref, dvp_ref, c_ref, s_ref, o_ref):
        cs, sn = c_ref[...], s_ref[...]
        more = (pl.program_id(0) < nb - 1).astype(F32)

        def unrot(x):
            return x * cs - _swap_halves(x) * sn

        def compact(ref, j, nxt=None):
            a = ref[:, (2 * j) * LANES:(2 * j + 1) * LANES]
            b = ref[:, (2 * j + 1) * LANES:(2 * j + 2) * LANES]
            if nxt is not None:
                a = a + more * nxt[:, (2 * j) * LANES:(2 * j + 1) * LANES]
                b = b + more * nxt[:, (2 * j + 1) * LANES:(2 * j + 2) * LANES]
            return a + pltpu.roll(b, 64, 1)

        for j in range(ATTN_WIDTH // LANES):
            o_ref[:, j * LANES:(j + 1) * LANES] = unrot(compact(dq_ref, j)).astype(BF16)
        for j in range(KV_WIDTH // LANES):
            o_ref[:, (COL_K + j) * LANES:(COL_K + j + 1) * LANES] = unrot(compact(dkc_ref, j, dkp_ref)).astype(BF16)
            o_ref[:, (COL_V + j) * LANES:(COL_V + j + 1) * LANES] = compact(dvc_ref, j, dvp_ref).astype(BF16)

    cur = lambda n: (n, 0)
    nxt = lambda n: (jnp.minimum(n + 1, nb - 1), 0)
    return _pc(body, name=name, grid=(nb,),
               in_specs=[pl.BlockSpec((L, N_Q_HEADS * LANES), cur), pl.BlockSpec((L, kvw), cur), pl.BlockSpec((L, kvw), nxt),
                         pl.BlockSpec((L, kvw), cur), pl.BlockSpec((L, kvw), nxt), pl.BlockSpec((L, LANES), cur),
                         pl.BlockSpec((L, LANES), cur)],
               out_specs=pl.BlockSpec((L, COL_HQ * LANES), cur), out_shape=SDS((s, COL_HQ * LANES), BF16),
               sem=("parallel",))(dqp, dkc, dkp, dvc, dvp, cos, sin)


def _split3(x):
    a = x.astype(BF16)
    r = x - a.astype(F32)
    b = r.astype(BF16)
    c = (r - b.astype(F32)).astype(BF16)
    return a, b, c


def _chunk_sum(x, upper):
    t = x.shape[0]
    ri = lax.broadcasted_iota(jnp.int32, (t, t), 0)
    ci = lax.broadcasted_iota(jnp.int32, (t, t), 1)
    same = (ri // HGRN_CHUNK) == (ci // HGRN_CHUNK)
    tri = (ci >= ri) if upper else (ci <= ri)
    m = jnp.where(same & tri, 1.0, 0.0).astype(BF16)
    out = None
    for part in _split3(x):
        y = jnp.dot(m, part, preferred_element_type=F32)
        out = y if out is None else out + y
    return out


def _lower_bound(l_ref, layer):
    lv = l_ref[...]
    e = jnp.exp(lv - jnp.max(lv, axis=0, keepdims=True))
    sm = e / jnp.sum(e, axis=0, keepdims=True)
    s0 = sm[0:1]
    return (s0 - s0) if layer == 0 else ((s0 + sm[1:2]) - s0)


def _hgrn_gates(hq_ref, hf_ref, lb):
    z = hf_ref[...]
    sg = _sigmoid(z)
    f = lb + (1.0 - lb) * sg
    kin = (1.0 - lb) * _sigmoid(-z)
    hq = hq_ref[...]
    sq = _sigmoid(hq)
    return sg, f, kin, hq, sq


def _shift_down(x, d):
    return x if d == 0 else pltpu.roll(x, d, 0)


def _shift_up(x, d):
    return x if d == 0 else pltpu.roll(x, x.shape[0] - d, 0)


CHUNKS_PER_BLOCK = LANES // HGRN_CHUNK


def _chunk_iotas():
    shape = (HGRN_CHUNK, LANES)
    return lax.broadcasted_iota(jnp.int32, shape, 0), lax.broadcasted_iota(jnp.int32, shape, 1)


def _chunk_rows(block, chunk):
    start = block * LANES + chunk * HGRN_CHUNK
    return slice(start, start + HGRN_CHUNK)


HGRN_HEADS_PER_STEP = 2
HGRN_STEP_WIDTH = HGRN_HEADS_PER_STEP * LANES


def _hgrn_specs(t, rev, nt):
    row = (lambda h, i: nt - 1 - i) if rev else (lambda h, i: i)
    col = lambda base: pl.BlockSpec((t, HGRN_STEP_WIDTH),
                                    lambda h, i, base=base: (row(h, i), base // HGRN_HEADS_PER_STEP + h))
    return col, row


def _head_views(refs, hh):
    return [r.at[:, pl.ds(hh * LANES, LANES)] for r in refs]


def hgrn_fwd(name, proj, lb_logits, layer):
    s = proj.shape[0]
    t = _tile(s, ROW_TILE)
    nt = s // t
    nc = t // HGRN_CHUNK
    col, row = _hgrn_specs(t, False, nt)

    def body(hq_ref, hf_ref, hi_ref, l_ref, o_ref, st_ref, state):
        @pl.when(pl.program_id(1) == 0)
        def _():
            state[...] = jnp.zeros_like(state)

        for hh in range(HGRN_HEADS_PER_STEP):
            head(*_head_views((hq_ref, hf_ref, hi_ref, l_ref, o_ref), hh), st_ref.at[:, hh], state.at[hh])

    def head(hq_ref, hf_ref, hi_ref, l_ref, o_ref, st_ref, state):
        lb = _lower_bound(l_ref, layer)
        sg, f, kin, hq, sq = _hgrn_gates(hq_ref, hf_ref, lb)
        q = hq * sq
        vb = hi_ref[...].astype(BF16)
        b = _chunk_sum(jnp.log(f), False)
        qe = (q * jnp.exp(b)).astype(BF16)
        trow, lane = _chunk_iotas()
        chunks = [(j, cc) for j in range(t // LANES) for cc in range(CHUNKS_PER_BLOCK)]
        decay, update = [], []
        for j, cc in chunks:
            rs = _chunk_rows(j, cc)
            bc = b[rs]
            bl = bc[HGRN_CHUNK - 1:HGRN_CHUNK, :]
            ke = (kin[rs] * jnp.exp(bl - bc)).astype(BF16)
            decay.append(jnp.exp(bl))
            update.append(lax.dot_general(vb[rs], ke, _TN, preferred_element_type=F32))
        st = state[...]
        for c in range(nc):
            st_ref[c] = st
            st = st * decay[c] + update[c]
        state[...] = st
        o_inter = [lax.dot_general(qe[c * HGRN_CHUNK:(c + 1) * HGRN_CHUNK], st_ref[c].astype(BF16), _NT,
                                   preferred_element_type=F32) for c in range(nc)]
        for j in range(t // LANES):
            blk = slice(j * LANES, (j + 1) * LANES)
            rows = []
            for cc in range(CHUNKS_PER_BLOCK):
                rs = _chunk_rows(j, cc)
                bc, qc, kc = b[rs], q[rs], kin[rs]
                here = trow + cc * HGRN_CHUNK
                am = jnp.where(lane == here, jnp.sum(qc * kc, axis=-1, keepdims=True), 0.0)
                for d in range(1, HGRN_CHUNK):
                    e = jnp.exp(jnp.where(trow >= d, bc - _shift_down(bc, d), MASK_VALUE))
                    a = jnp.sum((qc * _shift_down(kc, d)) * e, axis=-1, keepdims=True)
                    am = jnp.where(lane == here - d, a, am)
                rows.append(am)
            o_intra = jnp.dot(jnp.concatenate(rows, axis=0).astype(BF16), vb[blk], preferred_element_type=F32)
            for cc in range(CHUNKS_PER_BLOCK):
                rs = _chunk_rows(j, cc)
                o_ref[rs, :] = o_intra[cc * HGRN_CHUNK:(cc + 1) * HGRN_CHUNK] + o_inter[j * CHUNKS_PER_BLOCK + cc]

    hp = HGRN_HEADS_PER_STEP
    return _pc(body, name=name, grid=(HGRN_HEADS // hp, nt),
               in_specs=[col(COL_HQ), col(COL_HF), col(COL_HI), pl.BlockSpec((2, HGRN_STEP_WIDTH), lambda h, i: (0, h))],
               out_specs=[pl.BlockSpec((t, HGRN_STEP_WIDTH), lambda h, i: (i, h)),
                          pl.BlockSpec((nc, hp, LANES, LANES), lambda h, i: (i, h, 0, 0))],
               out_shape=[SDS((s, HGRN_WIDTH), F32), SDS((s // HGRN_CHUNK, HGRN_HEADS, LANES, LANES), F32)],
               scratch=[pltpu.VMEM((hp, LANES, LANES), F32)],
               sem=("parallel", "arbitrary"))(proj, proj, proj, lb_logits)


def hgrn_bwd(name, proj, lb_logits, layer, states, do):
    s = proj.shape[0]
    t = _tile(s, ROW_TILE)
    nt = s // t
    nc = t // HGRN_CHUNK
    col, row = _hgrn_specs(t, True, nt)

    def body(hq_ref, hf_ref, hi_ref, l_ref, st_ref, do_ref, dhq_ref, dhf_ref, dhi_ref, dlb_ref, dstate):
        @pl.when(pl.program_id(1) == 0)
        def _():
            dstate[...] = jnp.zeros_like(dstate)

        for hh in range(HGRN_HEADS_PER_STEP):
            hq_v, hf_v, hi_v, l_v, do_v, dhq_v, dhf_v, dhi_v, dlb_v = _head_views(
                (hq_ref, hf_ref, hi_ref, l_ref, do_ref, dhq_ref, dhf_ref, dhi_ref, dlb_ref), hh)
            head(hq_v, hf_v, hi_v, l_v, st_ref.at[:, hh], do_v, dhq_v, dhf_v, dhi_v, dlb_v, dstate.at[hh])

    def head(hq_ref, hf_ref, hi_ref, l_ref, st_ref, do_ref, dhq_ref, dhf_ref, dhi_ref, dlb_ref, dstate):
        first = pl.program_id(1) == 0
        lb = _lower_bound(l_ref, layer)
        sg, f, kin, hq, sq = _hgrn_gates(hq_ref, hf_ref, lb)
        q = hq * sq
        vb = hi_ref[...].astype(BF16)
        b = _chunk_sum(jnp.log(f), False)
        dob = do_ref[...].astype(BF16)
        eb = jnp.exp(b)
        qe = q * eb
        qeb = qe.astype(BF16)
        trow, lane = _chunk_iotas()
        last_row = trow == HGRN_CHUNK - 1

        decay, update = [None] * nc, [None] * nc
        for c in range(nc):
            rs = slice(c * HGRN_CHUNK, (c + 1) * HGRN_CHUNK)
            decay[c] = jnp.exp(b[(c + 1) * HGRN_CHUNK - 1:(c + 1) * HGRN_CHUNK, :])
            update[c] = lax.dot_general(dob[rs], qeb[rs], _TN, preferred_element_type=F32)
        dn_in = [None] * nc
        dn = dstate[...]
        for c in reversed(range(nc)):
            dn_in[c] = dn
            dn = dn * decay[c] + update[c]
        dstate[...] = dn

        dq_c, dk_c, dv_c, dbl_c = [None] * nc, [None] * nc, [None] * nc, [None] * nc
        for c in range(nc):
            rs = slice(c * HGRN_CHUNK, (c + 1) * HGRN_CHUNK)
            bc = b[rs]
            ekb = jnp.exp(bc[HGRN_CHUNK - 1:HGRN_CHUNK, :] - bc)
            ke = kin[rs] * ekb
            st = st_ref[c]
            dnb = dn_in[c].astype(BF16)
            dke = jnp.dot(vb[rs], dnb, preferred_element_type=F32)
            dq_c[c] = jnp.dot(dob[rs], st.astype(BF16), preferred_element_type=F32) * eb[rs]
            dk_c[c] = dke * ekb
            dv_c[c] = lax.dot_general(ke.astype(BF16), dnb, _NT, preferred_element_type=F32)
            dbl_c[c] = jnp.sum(dn_in[c] * st, axis=0, keepdims=True) * decay[c] + jnp.sum(dke * ke, axis=0, keepdims=True)

        db_c = [None] * nc
        for j in range(t // LANES):
            blk = slice(j * LANES, (j + 1) * LANES)
            damat = lax.dot_general(dob[blk], vb[blk], _NT, preferred_element_type=F32)
            rows = [None] * CHUNKS_PER_BLOCK
            for cc in range(CHUNKS_PER_BLOCK):
                c = j * CHUNKS_PER_BLOCK + cc
                rs = _chunk_rows(j, cc)
                bc, qc, kc = b[rs], q[rs], kin[rs]
                dam = damat[cc * HGRN_CHUNK:(cc + 1) * HGRN_CHUNK]
                here = trow + cc * HGRN_CHUNK
                on = lane == here
                da = jnp.sum(jnp.where(on, dam, 0.0), axis=-1, keepdims=True)
                am = jnp.where(on, jnp.sum(qc * kc, axis=-1, keepdims=True), 0.0)
                dq = dq_c[c] + da * kc
                dk = dk_c[c] + da * qc
                for d in range(1, HGRN_CHUNK):
                    on = lane == here - d
                    e = jnp.exp(jnp.where(trow >= d, bc - _shift_down(bc, d), MASK_VALUE))
                    kse = _shift_down(kc, d) * e
                    am = jnp.where(on, jnp.sum(qc * kse, axis=-1, keepdims=True), am)
                    da = jnp.sum(jnp.where(on, dam, 0.0), axis=-1, keepdims=True)
                    dq = dq + da * kse
                    dk = dk + _shift_up(da * (qc * e), d)
                rows[cc] = am
                dq_c[c], dk_c[c] = dq, dk
                db_c[c] = (qc * dq - kc * dk) + jnp.where(last_row, dbl_c[c], 0.0)
            dv_blk = lax.dot_general(jnp.concatenate(rows, axis=0).astype(BF16), dob[blk], _TN, preferred_element_type=F32)
            for cc in range(CHUNKS_PER_BLOCK):
                c = j * CHUNKS_PER_BLOCK + cc
                dv_c[c] = dv_c[c] + dv_blk[cc * HGRN_CHUNK:(cc + 1) * HGRN_CHUNK]
        dq = jnp.concatenate(dq_c, axis=0)
        dk = jnp.concatenate(dk_c, axis=0)
        dv = jnp.concatenate(dv_c, axis=0)
        db = jnp.concatenate(db_c, axis=0)
        dg = _chunk_sum(db, True)
        dhq_ref[...] = (dq * (sq * (1.0 + hq * (1.0 - sq)))).astype(BF16)
        dhi_ref[...] = dv.astype(BF16)
        dfk = dg / f - dk
        dhf_ref[...] = ((dfk * (1.0 - lb)) * (sg * (1.0 - sg))).astype(BF16)
        part = jnp.sum(dfk * (1.0 - sg), axis=0, keepdims=True)

        @pl.when(first)
        def _():
            dlb_ref[...] = part

        @pl.when(jnp.logical_not(first))
        def _():
            dlb_ref[...] += part

    hp = HGRN_HEADS_PER_STEP
    out_col = pl.BlockSpec((t, HGRN_STEP_WIDTH), lambda h, i: (nt - 1 - i, h))
    return _pc(body, name=name, grid=(HGRN_HEADS // hp, nt),
               in_specs=[col(COL_HQ), col(COL_HF), col(COL_HI), pl.BlockSpec((2, HGRN_STEP_WIDTH), lambda h, i: (0, h)),
                         pl.BlockSpec((nc, hp, LANES, LANES), lambda h, i: (nt - 1 - i, h, 0, 0)), out_col],
               out_specs=[out_col, out_col, out_col, pl.BlockSpec((1, HGRN_STEP_WIDTH), lambda h, i: (0, h))],
               out_shape=[SDS((s, HGRN_WIDTH), BF16)] * 3 + [SDS((1, HGRN_WIDTH), F32)],
               scratch=[pltpu.VMEM((hp, LANES, LANES), F32)],
               sem=("parallel", "arbitrary"))(proj, proj, proj, lb_logits, states, do)


def mix_out_fwd(name, attn, o, proj, g_attn, g_hgrn):
    s = attn.shape[0]
    t = _tile(s, ROW_TILE)
    half = HGRN_WIDTH // 2

    def body(a_ref, o_ref, hg0_ref, hg1_ref, ga_ref, gh_ref, c_ref):
        av = a_ref[...]
        c_ref[:, :ATTN_WIDTH] = ((av * _rstd(av)) * ga_ref[...]).astype(BF16)
        for j in range(HGRN_HEADS):
            cols = slice(j * LANES, (j + 1) * LANES)
            ov = o_ref[:, cols]
            hg_ref, hcols = (hg0_ref, cols) if j < 4 else (hg1_ref, slice((j - 4) * LANES, (j - 3) * LANES))
            hg = hg_ref[:, hcols]
            on = (ov * _rstd(ov)) * gh_ref[:, cols]
            c_ref[:, ATTN_WIDTH + j * LANES:ATTN_WIDTH + (j + 1) * LANES] = (on * (hg * _sigmoid(hg))).astype(BF16)

    return _pc(body, name=name, grid=(s // t,),
               in_specs=[_rows(t, ATTN_WIDTH), _rows(t, HGRN_WIDTH), _rows(t, half, COL_HG // 4), _rows(t, half, COL_HG // 4 + 1),
                         _fixed((1, ATTN_WIDTH)), _fixed((1, HGRN_WIDTH))],
               out_specs=_rows(t, D_MODEL), out_shape=SDS((s, D_MODEL), BF16), sem=("parallel",))(attn, o, proj, proj, g_attn, g_hgrn)


def mix_out_bwd(name, dcat, attn, o, proj, g_attn, g_hgrn):
    s = attn.shape[0]
    t = _tile(s, ROW_TILE)
    half = HGRN_WIDTH // 2

    def body(dc_ref, a_ref, o_ref, hg0_ref, hg1_ref, ga_ref, gh_ref, da_ref, do_ref, dhg_ref, dga_ref, dgh_ref, pa_s, ph_s):
        av = a_ref[...]
        r = _rstd(av)
        xh = av * r
        dyv = dc_ref[:, :ATTN_WIDTH]
        dyg = dyv * ga_ref[...]
        da_ref[...] = r * (dyg - xh * jnp.mean(dyg * xh, axis=-1, keepdims=True))
        pa_s[...] = jnp.sum(dyv * xh, axis=0, keepdims=True)
        for j in range(HGRN_HEADS):
            cols = slice(j * LANES, (j + 1) * LANES)
            ov = o_ref[:, cols]
            hg_ref, hcols = (hg0_ref, cols) if j < 4 else (hg1_ref, slice((j - 4) * LANES, (j - 3) * LANES))
            hg = hg_ref[:, hcols]
            sg = _sigmoid(hg)
            r = _rstd(ov)
            xh = ov * r
            gain = gh_ref[:, cols]
            dh = dc_ref[:, ATTN_WIDTH + j * LANES:ATTN_WIDTH + (j + 1) * LANES]
            dhg_ref[:, cols] = ((dh * (xh * gain)) * (sg * (1.0 + hg * (1.0 - sg)))).astype(BF16)
            dyv = dh * (hg * sg)
            dyg = dyv * gain
            do_ref[:, cols] = r * (dyg - xh * jnp.mean(dyg * xh, axis=-1, keepdims=True))
            ph_s[:, cols] = jnp.sum(dyv * xh, axis=0, keepdims=True)

        @pl.when(pl.program_id(0) == 0)
        def _():
            dga_ref[...] = pa_s[...]
            dgh_ref[...] = ph_s[...]

        @pl.when(pl.program_id(0) > 0)
        def _():
            dga_ref[...] += pa_s[...]
            dgh_ref[...] += ph_s[...]

    return _pc(body, name=name, grid=(s // t,),
               in_specs=[_rows(t, D_MODEL), _rows(t, ATTN_WIDTH), _rows(t, HGRN_WIDTH), _rows(t, half, COL_HG // 4),
                         _rows(t, half, COL_HG // 4 + 1), _fixed((1, ATTN_WIDTH)), _fixed((1, HGRN_WIDTH))],
               out_specs=[_rows(t, ATTN_WIDTH), _rows(t, HGRN_WIDTH), _rows(t, HGRN_WIDTH), _fixed((1, ATTN_WIDTH)),
                          _fixed((1, HGRN_WIDTH))],
               out_shape=[SDS((s, ATTN_WIDTH), F32), SDS((s, HGRN_WIDTH), F32), SDS((s, HGRN_WIDTH), BF16),
                          SDS((1, ATTN_WIDTH), F32), SDS((1, HGRN_WIDTH), F32)],
               scratch=[pltpu.VMEM((1, ATTN_WIDTH), F32), pltpu.VMEM((1, HGRN_WIDTH), F32)],
               sem=("arbitrary",))(dcat, attn, o, proj, proj, g_attn, g_hgrn)


BIG = (("w_in", 2048, 1408, 0), ("w_out", 512, 2048, 1), ("w_ffn_gate", 2048, 1408, 0), ("w_ffn_up", 2048, 1408, 0),
       ("w_ffn_down", 1408, 2048, 1), ("w_ple_gate", 512, 2048, 1), ("w_ple_proj", 256, 512, 0))
BIG_BY_NAME = {spec[0]: spec for spec in BIG}
HBM_SPEC = pl.BlockSpec(memory_space=pltpu.HBM)
SEM_SPEC = pl.BlockSpec(memory_space=pltpu.SEMAPHORE)
TOKEN_SHAPE = (8, LANES)


def _split_call(body, *, name, in_specs, out_specs, out_shape, aliases):
    return pl.pallas_call(body, name=name, in_specs=in_specs, out_specs=out_specs, out_shape=out_shape,
                          input_output_aliases=aliases,
                          compiler_params=pltpu.CompilerParams(has_side_effects=pltpu.SideEffectType.DATAFLOW_SIDE_EFFECTING))


def _in_hbm(arrays):
    return [pltpu.with_memory_space_constraint(a, pltpu.HBM) for a in arrays]


def cast_to_slot(name, place, w, layer, after):
    _, r, c = w.shape
    tr = _tile(r, 1024)

    def body(place_ref, w_ref, after_ref, o_ref):
        o_ref[...] = w_ref[...].astype(BF16)

    gs = pltpu.PrefetchScalarGridSpec(
        num_scalar_prefetch=1, grid=(r // tr,),
        in_specs=[pl.BlockSpec((None, tr, c), lambda i, pr: (layer, i, 0)), pl.BlockSpec(memory_space=pl.ANY)],
        out_specs=pl.BlockSpec((None, tr, c), lambda i, pr: (pr[1], i, 0)))
    return _pc(body, name=name, grid_spec=gs, in_specs=None, out_specs=None, out_shape=SDS((N_CHIPS, r, c), BF16),
               sem=("parallel",))(place, w, after)


def _place():
    x, y, c = lax.axis_index("x"), lax.axis_index("y"), lax.axis_index("c")
    chips = [(1 - x, y), (x, 1 - y), (1 - x, 1 - y)]
    return x, y, c, chips


def _half(ref, axis, c, rows, cols):
    if axis == 0:
        return ref.at[pl.ds(pl.multiple_of(c * (rows // 2), 16), rows // 2), :]
    return ref.at[:, pl.ds(pl.multiple_of(c * (cols // 2), LANES), cols // 2)]


def _gather_copies(specs, bufs, send, recv):
    x, y, c, chips = _place()
    cps = []
    for t, (_, rows, cols, axis) in enumerate(specs):
        mine = _half(bufs[t].at[2 * x + y], axis, c, rows, cols)
        for k, (cx, cy) in enumerate(chips):
            cps.append(pltpu.make_async_remote_copy(src_ref=mine, dst_ref=mine, send_sem=send.at[3 * t + k],
                                                    recv_sem=recv.at[3 * t + k], device_id=(cx, cy, c), device_id_type=MESH))
    return cps


def gather_start(name, specs, bufs):
    nt = len(bufs)
    n = 3 * nt

    def body(*refs):
        send, recv, token = refs[nt], refs[nt + 1], refs[-1]
        for cp in _gather_copies(specs, refs[:nt], send, recv):
            cp.start()
        token[...] = jnp.zeros(TOKEN_SHAPE, F32)

    out = _split_call(
        body, name=name, in_specs=[HBM_SPEC] * nt,
        out_specs=(SEM_SPEC, SEM_SPEC) + (HBM_SPEC,) * nt + (pl.BlockSpec(memory_space=pltpu.VMEM),),
        out_shape=(pltpu.SemaphoreType.DMA((n,)), pltpu.SemaphoreType.DMA((n,)))
        + tuple(pltpu.HBM(b.shape, b.dtype) for b in bufs) + (SDS(TOKEN_SHAPE, F32),),
        aliases={t: 2 + t for t in range(nt)})(*_in_hbm(bufs))
    return out[0], out[1], list(out[2:2 + nt]), out[-1]


def gather_wait(name, specs, send, recv, bufs, after):
    nt = len(bufs)

    def body(*refs):
        for cp in _gather_copies(specs, refs[:nt], refs[nt], refs[nt + 1]):
            cp.wait_send()
            cp.wait_recv()

    out = _split_call(
        body, name=name, in_specs=[HBM_SPEC] * nt + [SEM_SPEC, SEM_SPEC, pl.BlockSpec(memory_space=pl.ANY)],
        out_specs=(HBM_SPEC,) * nt, out_shape=tuple(pltpu.HBM(b.shape, b.dtype) for b in bufs),
        aliases={t: t for t in range(nt)})(*bufs, send, recv, after)
    return list(out)


def gather_pass(name, specs, bufs):
    nt = len(bufs)

    def body(*refs):
        ins, outs = refs[:nt], refs[nt:2 * nt]
        send, recv = refs[2 * nt:]
        x, y, c, chips = _place()
        cps = []
        for t, (_, rows, cols, axis) in enumerate(specs):
            for k, (cx, cy) in enumerate(chips):
                cp = pltpu.make_async_remote_copy(
                    src_ref=_half(ins[t].at[2 * cx + cy], axis, c, rows, cols),
                    dst_ref=_half(outs[t].at[2 * cx + cy], axis, c, rows, cols),
                    send_sem=send.at[3 * t + k], recv_sem=recv.at[3 * t + k], device_id=(x, y, 1 - c), device_id_type=MESH)
                cp.start()
                cps.append(cp)
        for t, (_, rows, cols, axis) in enumerate(specs):
            for k, (cx, cy) in enumerate(chips):
                theirs = _half(outs[t].at[2 * cx + cy], axis, 1 - c, rows, cols)
                pltpu.make_async_remote_copy(src_ref=theirs, dst_ref=theirs, send_sem=send.at[3 * t + k],
                                             recv_sem=recv.at[3 * t + k], device_id=(x, y, 1 - c), device_id_type=MESH).wait_recv()
        for cp in cps:
            cp.wait_send()

    return _pc(body, name=name, in_specs=[HBM_SPEC] * nt, out_specs=[HBM_SPEC] * nt,
               out_shape=[SDS(b.shape, b.dtype) for b in bufs], scratch=[pltpu.SemaphoreType.DMA((3 * nt,))] * 2,
               input_output_aliases={t: t for t in range(nt)})(*bufs)


def reduce_to_sibling(name, grads):
    nt = len(grads)

    def body(*refs):
        srcs, dsts = refs[:nt], refs[nt:2 * nt]
        send, recv = refs[2 * nt:]
        x, y, c, _ = _place()
        cps = []
        for t in range(nt):
            cp = pltpu.make_async_remote_copy(src_ref=srcs[t].at[1 - c], dst_ref=dsts[t], send_sem=send.at[t],
                                              recv_sem=recv.at[t], device_id=(x, y, 1 - c), device_id_type=MESH)
            cp.start()
            cps.append(cp)
        for cp in cps:
            cp.wait()

    return _pc(body, name=name, in_specs=[HBM_SPEC] * nt, out_specs=[HBM_SPEC] * nt,
               out_shape=[SDS(g.shape[1:], g.dtype) for g in grads],
               scratch=[pltpu.SemaphoreType.DMA((nt,))] * 2)(*grads)


def add_halves(name, place, grad, got):
    _, n, r, c = grad.shape
    tr = _tile(r, 1024)

    def body(place_ref, g_ref, o_ref, part_ref, slot_ref):
        val = (g_ref[...].astype(F32) + o_ref[...].astype(F32)).astype(BF16)
        part_ref[...] = val

        @pl.when(pl.program_id(1) == place_ref[1])
        def _():
            slot_ref[...] = val

    gs = pltpu.PrefetchScalarGridSpec(
        num_scalar_prefetch=1, grid=(r // tr, n),
        in_specs=[pl.BlockSpec((None, None, tr, c), lambda i, j, pr: (pr[0], j, i, 0)),
                  pl.BlockSpec((None, tr, c), lambda i, j, pr: (j, i, 0))],
        out_specs=[pl.BlockSpec((None, tr, c), lambda i, j, pr: (j, i, 0)),
                   pl.BlockSpec((None, tr, c), lambda i, j, pr: (pr[1], i, 0))])
    return _pc(body, name=name, grid_spec=gs, in_specs=None, out_specs=None, out_shape=[SDS((n, r, c), BF16)] * 2,
               sem=("parallel", "arbitrary"))(place, grad, got)


def _chips_copies(parts, slots, send, recv):
    x, y, c, chips = _place()
    cps = []
    for t in range(len(parts)):
        for k, (cx, cy) in enumerate(chips):
            cps.append(pltpu.make_async_remote_copy(src_ref=parts[t].at[2 * cx + cy], dst_ref=slots[t].at[2 * x + y],
                                                    send_sem=send.at[3 * t + k], recv_sem=recv.at[3 * t + k],
                                                    device_id=(cx, cy, c), device_id_type=MESH))
    return cps


def chips_start(name, parts, slots):
    nt = len(parts)
    n = 3 * nt

    def body(*refs):
        send, recv, token = refs[2 * nt], refs[2 * nt + 1], refs[-1]
        for cp in _chips_copies(refs[:nt], refs[nt:2 * nt], send, recv):
            cp.start()
        token[...] = jnp.zeros(TOKEN_SHAPE, F32)

    both = list(parts) + list(slots)
    out = _split_call(
        body, name=name, in_specs=[HBM_SPEC] * (2 * nt),
        out_specs=(SEM_SPEC, SEM_SPEC) + (HBM_SPEC,) * (2 * nt) + (pl.BlockSpec(memory_space=pltpu.VMEM),),
        out_shape=(pltpu.SemaphoreType.DMA((n,)), pltpu.SemaphoreType.DMA((n,)))
        + tuple(pltpu.HBM(b.shape, b.dtype) for b in both) + (SDS(TOKEN_SHAPE, F32),),
        aliases={t: 2 + t for t in range(2 * nt)})(*_in_hbm(both))
    return out[0], out[1], list(out[2:2 + nt]), list(out[2 + nt:2 + 2 * nt]), out[-1]


def chips_wait(name, send, recv, parts, slots, after):
    nt = len(parts)

    def body(*refs):
        for cp in _chips_copies(refs[:nt], refs[nt:2 * nt], refs[2 * nt], refs[2 * nt + 1]):
            cp.wait_send()
            cp.wait_recv()

    both = list(parts) + list(slots)
    out = _split_call(
        body, name=name, in_specs=[HBM_SPEC] * (2 * nt) + [SEM_SPEC, SEM_SPEC, pl.BlockSpec(memory_space=pl.ANY)],
        out_specs=(HBM_SPEC,) * (2 * nt), out_shape=tuple(pltpu.HBM(b.shape, b.dtype) for b in both),
        aliases={t: t for t in range(2 * nt)})(*both, send, recv, after)
    return list(out[nt:])


def sum_chips(name, place, slots):
    n, r, c = slots.shape
    tr = _tile(r, 1024)

    def body(place_ref, s_ref, o_ref):
        acc = s_ref[0].astype(F32)
        for k in range(1, n):
            acc = acc + s_ref[k].astype(F32)
        o_ref[...] = acc

    gs = pltpu.PrefetchScalarGridSpec(
        num_scalar_prefetch=1, grid=(r // tr,),
        in_specs=[pl.BlockSpec((n, tr, c), lambda i, pr: (0, i, 0))],
        out_specs=pl.BlockSpec((None, tr, c), lambda i, pr: (pr[0], i, 0)))
    return _pc(body, name=name, grid_spec=gs, in_specs=None, out_specs=None, out_shape=SDS((2, r, c), F32),
               sem=("parallel",))(place, slots)


def share_with_sibling(name, bufs):
    nt = len(bufs)

    def body(*refs):
        ins, outs = refs[:nt], refs[nt:2 * nt]
        send, recv = refs[2 * nt:]
        x, y, c, _ = _place()
        cps = []
        for t in range(nt):
            cp = pltpu.make_async_remote_copy(src_ref=ins[t].at[c], dst_ref=outs[t].at[c], send_sem=send.at[t], recv_sem=recv.at[t],
                                              device_id=(x, y, 1 - c), device_id_type=MESH)
            cp.start()
            cps.append(cp)
        for t in range(nt):
            theirs = outs[t].at[1 - c]
            pltpu.make_async_remote_copy(src_ref=theirs, dst_ref=theirs, send_sem=send.at[t], recv_sem=recv.at[t],
                                         device_id=(x, y, 1 - c), device_id_type=MESH).wait_recv()
        for cp in cps:
            cp.wait_send()

    return _pc(body, name=name, in_specs=[HBM_SPEC] * nt, out_specs=[HBM_SPEC] * nt,
               out_shape=[SDS(b.shape, F32) for b in bufs], scratch=[pltpu.SemaphoreType.DMA((nt,))] * 2,
               input_output_aliases={t: t for t in range(nt)})(*bufs)


def _adamw(w, g, m, v):
    m = ADAM_B1 * m + (1.0 - ADAM_B1) * g
    v = ADAM_B2 * v + (1.0 - ADAM_B2) * (g * g)
    m_hat = m / (1.0 - ADAM_B1 ** ADAM_STEP)
    v_hat = v / (1.0 - ADAM_B2 ** ADAM_STEP)
    delta = -ADAM_LR * (m_hat / (jnp.sqrt(v_hat) + ADAM_EPS) + ADAM_WD * w)
    return delta, m, v


def adamw_big(name, w, m, v, g0, g1, axis):
    _, r, c = w.shape
    _, rh, ch = g0.shape
    tr = _tile(rh, 256)
    nb = rh // tr
    if axis == 0:
        wspec = pl.BlockSpec((None, tr, ch), lambda l, h, i: (l, h * nb + i, 0))
    else:
        wspec = pl.BlockSpec((None, tr, ch), lambda l, h, i: (l, i, h))
    g0spec = pl.BlockSpec((None, tr, ch), lambda l, h, i: (h * (1 - l), i * (1 - l), 0))
    g1spec = pl.BlockSpec((None, tr, ch), lambda l, h, i: (h * l, i * l, 0))

    def body(w_ref, m_ref, v_ref, g0_ref, g1_ref, go_ref, d_ref, mo_ref, vo_ref):
        def run(g_ref):
            g = g_ref[...]
            delta, mn, vn = _adamw(w_ref[...], g, m_ref[...], v_ref[...])
            go_ref[...] = g
            d_ref[...] = delta
            mo_ref[...] = mn
            vo_ref[...] = vn

        @pl.when(pl.program_id(0) == 0)
        def _():
            run(g0_ref)

        @pl.when(pl.program_id(0) == 1)
        def _():
            run(g1_ref)

    return _pc(body, name=name, grid=(2, 2, nb), in_specs=[wspec, wspec, wspec, g0spec, g1spec], out_specs=[wspec] * 4,
               out_shape=[SDS(w.shape, F32)] * 4, sem=("parallel", "parallel", "parallel"))(w, m, v, g0, g1)


SMALL = (("pre_mix_gain", 2048), ("post_mix_gain", 2048), ("pre_ffn_gain", 2048), ("post_ffn_gain", 2048), ("ple_gain", 2048),
         ("attn_out_gain", 1024), ("hgrn_out_gain", 1024), ("hgrn_lb_logits", 1024), ("attn_sinks", 128))
SMALL_ROWS = sum(2 * w // LANES for _, w in SMALL)
SMALL_PAD = -(-SMALL_ROWS // 8) * 8
LB_ROW = sum(2 * w // LANES for _, w in SMALL[:7])


def _pack_small(parts):
    rows = []
    for nm, w in SMALL:
        a = parts[nm].astype(F32)
        if a.shape[1] != w:
            a = jnp.pad(a, ((0, 0), (0, w - a.shape[1])))
        rows.append(a.reshape(2 * w // LANES, LANES))
    rows.append(jnp.zeros((SMALL_PAD - SMALL_ROWS, LANES), F32))
    return jnp.concatenate(rows, axis=0)


def _unpack_small(packed, widths):
    out, r = {}, 0
    for nm, w in SMALL:
        n = 2 * w // LANES
        out[nm] = packed[r:r + n].reshape(2, w)[:, :widths[nm]]
        r += n
    return out


def allreduce_small(name, packed):
    rows = packed.shape[0]

    def body(x_ref, o_ref, buf, send, recv, own_sem):
        x, y, c, _ = _place()
        me = 4 * x + 2 * y + c
        own = pltpu.make_async_copy(x_ref, buf.at[me], own_sem)
        own.start()
        cps = []
        for k in range(1, 8):
            px, py, pc = x ^ (k >> 2), y ^ ((k >> 1) & 1), c ^ (k & 1)
            cp = pltpu.make_async_remote_copy(src_ref=x_ref, dst_ref=buf.at[me], send_sem=send.at[k - 1], recv_sem=recv.at[k - 1],
                                              device_id=(px, py, pc), device_id_type=MESH)
            cp.start()
            cps.append(cp)
        for k in range(1, 8):
            px, py, pc = x ^ (k >> 2), y ^ ((k >> 1) & 1), c ^ (k & 1)
            slot = buf.at[4 * px + 2 * py + pc]
            pltpu.make_async_remote_copy(src_ref=slot, dst_ref=slot, send_sem=send.at[k - 1], recv_sem=recv.at[k - 1],
                                         device_id=(px, py, pc), device_id_type=MESH).wait_recv()
        for cp in cps:
            cp.wait_send()
        own.wait()
        acc = buf[0]
        for k in range(1, 8):
            acc = acc + buf[k]
        o_ref[...] = acc

    vm = pl.BlockSpec(memory_space=pltpu.VMEM)
    return _pc(body, name=name, in_specs=[vm], out_specs=vm, out_shape=SDS((rows, LANES), F32),
               scratch=[pltpu.VMEM((8, rows, LANES), F32), pltpu.SemaphoreType.DMA((7,)), pltpu.SemaphoreType.DMA((7,)),
                        pltpu.SemaphoreType.DMA])(packed)


def adamw_small(name, w, m, v, g):
    rows = w.shape[0]
    n = HGRN_WIDTH // LANES

    def body(w_ref, m_ref, v_ref, g_ref, go_ref, d_ref, mo_ref, vo_ref):
        go_ref[...] = g_ref[...]
        l0 = w_ref[LB_ROW:LB_ROW + n, :]
        l1 = w_ref[LB_ROW + n:LB_ROW + 2 * n, :]
        mx = jnp.maximum(l0, l1)
        e0, e1 = jnp.exp(l0 - mx), jnp.exp(l1 - mx)
        s0, s1 = e0 / (e0 + e1), e1 / (e0 + e1)
        dlb1 = g_ref[LB_ROW + n:LB_ROW + 2 * n, :]
        inner = s1 * dlb1
        go_ref[LB_ROW:LB_ROW + n, :] = s0 * (0.0 - inner)
        go_ref[LB_ROW + n:LB_ROW + 2 * n, :] = s1 * (dlb1 - inner)
        delta, mn, vn = _adamw(w_ref[...], go_ref[...], m_ref[...], v_ref[...])
        d_ref[...] = delta
        mo_ref[...] = mn
        vo_ref[...] = vn

    vm = pl.BlockSpec(memory_space=pltpu.VMEM)
    return _pc(body, name=name, in_specs=[vm] * 4, out_specs=[vm] * 4, out_shape=[SDS((rows, LANES), F32)] * 4)(w, m, v, g)


def _layer_fwd(l, x, h1, p_l, w_in_g, rest_of_weights, gains, cos, sin, sinks, lb_logits, g_next, target):
    n = f"l{l}_"
    proj = mm_col(n + "in_proj", h1, w_in_g)
    qp, kp, vp = rope_qkv(n + "rope_qkv", proj, cos, sin)
    attn = attn_fwd(n + "attn_fwd", qp, kp, vp, sinks)
    o, states = hgrn_fwd(n + "hgrn_fwd", proj, lb_logits, l)
    cat = mix_out_fwd(n + "mix_out_fwd", attn, o, proj, gains["attn_out_gain"], gains["hgrn_out_gain"])
    rest, token = rest_of_weights(cat)
    wts = dict(rest, w_in=w_in_g)
    if token is not None:
        gains = _with_token(gains, "post_mix_gain", token)
    m, x1, h2 = out_proj_post_mix(n + "out_proj_post_mix", cat, wts["w_out"], gains["post_mix_gain"], x, gains["pre_ffn_gain"])
    g, u, a = ffn_gate_up(n + "ffn_gate_up", h2, wts["w_ffn_gate"], wts["w_ffn_up"])
    f = mm_row(n + "ffn_down", a, wts["w_ffn_down"])
    x2, h3 = post_pre_norm(n + "post_ffn", f, gains["post_ffn_gain"], x1, gains["ple_gain"])
    pp = mm_col(n + "ple_proj", p_l, wts["w_ple_proj"])
    if target is None:
        z, *out = ple_gate_fwd_mid(n + "ple_gate_fwd", h3, wts["w_ple_gate"], pp, x2, g_next)
    else:
        z, *out = ple_gate_fwd_loss(n + "ple_gate_loss", h3, wts["w_ple_gate"], pp, x2, target)
    saved = dict(x=x, h1=h1, proj=proj, qp=qp, kp=kp, vp=vp, attn=attn, o=o, states=states, cat=cat, m=m, x1=x1, h2=h2,
                 g=g, u=u, a=a, f=f, x2=x2, h3=h3, z=z, pp=pp, p=p_l)
    return out, saved, wts


EARLY = ("w_ple_gate", "w_ple_proj", "w_ffn_down", "w_ffn_gate", "w_ffn_up")
LATE = ("w_out", "w_in")


def _layer_bwd_ffn(l, dx3, sv, wts, gains):
    n = f"l{l}_"
    dpp, dz = ple_bwd(n + "ple_bwd", dx3, sv["z"], sv["pp"])
    dh3 = mm_row_t(n + "ple_gate_dx", dz, wts["w_ple_gate"])
    dx2, df, d_ple_gain, d_post_ffn = norm_bwd_pair(n + "ple_post_ffn_bwd", sv["x2"], gains["ple_gain"], dh3, dx3, sv["f"],
                                                    gains["post_ffn_gain"])
    dg, du = ffn_down_bwd(n + "ffn_down_bwd", df, wts["w_ffn_down"], sv["g"], sv["u"])
    big = dict(
        w_ple_gate=mm_wg_row(n + "ple_gate_dw", sv["h3"], dz),
        w_ple_proj=mm_wg_col(n + "ple_proj_dw", sv["p"], dpp),
        w_ffn_down=mm_wg_row(n + "ffn_down_dw", sv["a"], df),
        w_ffn_gate=mm_wg_col(n + "ffn_gate_dw", sv["h2"], dg),
        w_ffn_up=mm_wg_col(n + "ffn_up_dw", sv["h2"], du),
    )
    return dict(dx2=dx2, dg=dg, du=du), big, dict(ple_gain=d_ple_gain, post_ffn_gain=d_post_ffn)


def _layer_bwd_mix(l, st, sv, wts, gains, cos, sin, sinks, lb_logits):
    n = f"l{l}_"
    dh2 = mm_col_t(n + "ffn_gate_dx", st["dg"], wts["w_ffn_gate"])
    dh2 = mm_col_t(n + "ffn_up_dx", st["du"], wts["w_ffn_up"], add=dh2)
    dx1, dm, d_pre_ffn, d_post_mix = norm_bwd_pair(n + "pre_ffn_post_mix_bwd", sv["x1"], gains["pre_ffn_gain"], dh2, st["dx2"],
                                                   sv["m"], gains["post_mix_gain"])
    dcat = mm_row_t(n + "out_proj_dx", dm, wts["w_out"])
    dattn, do, dhg, d_attn_gain, d_hgrn_gain = mix_out_bwd(n + "mix_out_bwd", dcat, sv["attn"], sv["o"], sv["proj"],
                                                            gains["attn_out_gain"], gains["hgrn_out_gain"])
    dqp, dkc, dkp, dvc, dvp, dsinks = attn_bwd(n + "attn_bwd", sv["qp"], sv["kp"], sv["vp"], sinks, dattn)
    dqkv = rope_bwd(n + "rope_bwd", dqp, dkc, dkp, dvc, dvp, cos, sin)
    dhq, dhf, dhi, dlb = hgrn_bwd(n + "hgrn_bwd", sv["proj"], lb_logits, l, sv["states"], do)
    dproj = jnp.concatenate([dqkv, dhq, dhf, dhi, dhg], axis=1)
    dh1 = mm_col_t(n + "in_proj_dx", dproj, wts["w_in"])
    dx, d_pre_mix = norm_bwd(n + "pre_mix_bwd", sv["x"], gains["pre_mix_gain"], dh1, dx1)
    big = dict(w_out=mm_wg_row(n + "out_proj_dw", sv["cat"], dm), w_in=mm_wg_col(n + "in_proj_dw", sv["h1"], dproj))
    small = dict(pre_mix_gain=d_pre_mix, post_mix_gain=d_post_mix, pre_ffn_gain=d_pre_ffn, attn_out_gain=d_attn_gain,
                 hgrn_out_gain=d_hgrn_gain, hgrn_lb_logits=dlb, attn_sinks=dsinks)
    return dx, big, small


def _layer_bwd(l, dx3, sv, wts, gains, cos, sin, sinks, lb_logits):
    st, early, small_a = _layer_bwd_ffn(l, dx3, sv, wts, gains)
    dx, late, small_b = _layer_bwd_mix(l, st, sv, wts, gains, cos, sin, sinks, lb_logits)
    return dx, {**early, **late}, {**small_a, **small_b}


def _reduce_start(tag, names, big, place):
    got = reduce_to_sibling(tag + "_reduce_to_sibling", [big[nm] for nm in names])
    pairs = [add_halves(f"{tag}_add_{nm}", place, big[nm], gt) for nm, gt in zip(names, got)]
    return chips_start(tag + "_chips_start", [pr[0] for pr in pairs], [pr[1] for pr in pairs])


def _reduce_finish(tag, names, started, place, after):
    send, recv, parts, slots, _ = started
    slots = chips_wait(tag + "_chips_wait", send, recv, parts, slots, after)
    bufs = [sum_chips(f"{tag}_sum_{nm}", place, sl) for nm, sl in zip(names, slots)]
    return dict(zip(names, share_with_sibling(tag + "_share_with_sibling", bufs)))


def _with_token(gains, name, token):
    out = dict(gains)
    out[name] = gains[name] + token[0, 0]
    return out


def kernel(x, p, positions, w_in, attn_sinks, hgrn_lb_logits, attn_out_gain, hgrn_out_gain, w_out, pre_mix_gain, post_mix_gain, pre_ffn_gain, post_ffn_gain, w_ffn_gate, w_ffn_up, w_ffn_down, ple_gain, w_ple_gate, w_ple_proj, loss_target, m_w_in, m_attn_sinks, m_hgrn_lb_logits, m_attn_out_gain, m_hgrn_out_gain, m_w_out, m_pre_mix_gain, m_post_mix_gain, m_pre_ffn_gain, m_post_ffn_gain, m_w_ffn_gate, m_w_ffn_up, m_w_ffn_down, m_ple_gain, m_w_ple_gate, m_w_ple_proj, v_w_in, v_attn_sinks, v_hgrn_lb_logits, v_attn_out_gain, v_hgrn_out_gain, v_w_out, v_pre_mix_gain, v_post_mix_gain, v_pre_ffn_gain, v_post_ffn_gain, v_w_ffn_gate, v_w_ffn_up, v_w_ffn_down, v_ple_gain, v_w_ple_gate, v_w_ple_proj):
    given = dict(locals())
    depth = 2
    place = jnp.stack([lax.axis_index("c"), 2 * lax.axis_index("x") + lax.axis_index("y")]).astype(jnp.int32)
    xs = x[0]
    tgt = loss_target[0]
    pos_col = positions.reshape(-1, 1)
    half = 32
    inv_freq = ROPE_THETA ** (-jnp.arange(half, dtype=F32) / half)
    inv_freq = jnp.tile(inv_freq, 4).reshape(1, LANES)
    gains = [{nm: given[nm][l:l + 1] for nm, _ in SMALL[:7]} for l in range(depth)]
    names = [nm for nm, *_ in BIG]
    first, others = names[:1], names[1:]

    def specs(nms):
        return [BIG_BY_NAME[nm] for nm in nms]

    def start_gather(tag, l, nms, after):
        return gather_start(tag + "_gather_start", specs(nms),
                            [cast_to_slot(f"{tag}_cast_{nm}", place, given[nm], l, after) for nm in nms])

    def finish_gather(tag, nms, started, after):
        bufs = gather_wait(tag + "_gather_wait", specs(nms), started[0], started[1], started[2], after)
        return dict(zip(nms, gather_pass(tag + "_gather_pass", specs(nms), bufs)))

    g0a = start_gather("l0a", 0, first, place)
    g0b = start_gather("l0b", 0, others, g0a[3])
    started = {}

    def rest_of_layer0(after):
        got = finish_gather("l0b", others, g0b, after)
        started["l1"] = start_gather("l1", 1, names, got["w_out"])
        return got, started["l1"][3]

    cos, sin = rope_tables("rope_tables", pos_col, inv_freq)
    h1 = pre_norm("l0_pre_mix", xs, gains[0]["pre_mix_gain"])
    w_in0 = finish_gather("l0a", first, g0a, g0b[3])["w_in"]
    (x_mid, h1_next), sv0, wts0 = _layer_fwd(0, xs, h1, p[0, 0], w_in0, rest_of_layer0, gains[0], cos, sin, attn_sinks[0],
                                             hgrn_lb_logits, gains[1]["pre_mix_gain"], None)
    wts1 = finish_gather("l1", names, started["l1"], x_mid)
    (dy, loss_part), sv1, _ = _layer_fwd(1, x_mid, h1_next, p[1, 0], wts1["w_in"], lambda after: (wts1, None), gains[1], cos, sin,
                                         attn_sinks[1], hgrn_lb_logits, None, tgt)

    dx_mid, big1, small1 = _layer_bwd(1, dy, sv1, wts1, gains[1], cos, sin, attn_sinks[1], hgrn_lb_logits)
    r1 = _reduce_start("l1", names, big1, place)
    st0, early0, small0 = _layer_bwd_ffn(0, dx_mid, sv0, wts0, _with_token(gains[0], "ple_gain", r1[4]))
    r0e = _reduce_start("l0e", EARLY, early0, place)
    dx0, late0, small0b = _layer_bwd_mix(0, st0, sv0, wts0, _with_token(gains[0], "pre_ffn_gain", r0e[4]), cos, sin,
                                         attn_sinks[0], hgrn_lb_logits)
    small0 = {**small0, **small0b}
    r0l = _reduce_start("l0l", LATE, late0, place)
    red1 = _reduce_finish("l1", names, r1, place, r0l[4])
    red0 = _reduce_finish("l0e", EARLY, r0e, place, red1[names[-1]])

    loss = lax.psum(loss_part[0, 0], ("x", "y", "c"))
    grad_x = dx0[None]

    out_big = {}
    for nm in EARLY:
        out_big[nm] = adamw_big("adamw_" + nm, given[nm], given["m_" + nm], given["v_" + nm], red0[nm], red1[nm], BIG_BY_NAME[nm][3])
    red0.update(_reduce_finish("l0l", LATE, r0l, place, out_big[EARLY[-1]][3]))
    for nm in LATE:
        out_big[nm] = adamw_big("adamw_" + nm, given[nm], given["m_" + nm], given["v_" + nm], red0[nm], red1[nm], BIG_BY_NAME[nm][3])

    widths = {nm: given[nm].shape[1] for nm, _ in SMALL}
    small_g = {nm: jnp.concatenate([small0[nm][:, :widths[nm]] if nm != "attn_sinks" else small0[nm][:, :LANES],
                                    small1[nm][:, :widths[nm]] if nm != "attn_sinks" else small1[nm][:, :LANES]], axis=0)
               for nm, _ in SMALL}
    g_sum = allreduce_small("allreduce_small", _pack_small(small_g))
    sm = adamw_small("adamw_small", _pack_small({nm: given[nm] for nm, _ in SMALL}),
                     _pack_small({nm: given["m_" + nm] for nm, _ in SMALL}),
                     _pack_small({nm: given["v_" + nm] for nm, _ in SMALL}), g_sum)
    out_small = [_unpack_small(a, widths) for a in sm]

    order = ["w_in", "attn_sinks", "hgrn_lb_logits", "attn_out_gain", "hgrn_out_gain", "w_out", "pre_mix_gain", "post_mix_gain",
             "pre_ffn_gain", "post_ffn_gain", "w_ffn_gate", "w_ffn_up", "w_ffn_down", "ple_gain", "w_ple_gate", "w_ple_proj"]
    res = [loss, grad_x]
    for k in range(4):
        for nm in order:
            res.append(out_big[nm][k] if nm in out_big else out_small[k][nm])
    return tuple(res)
```

```python
import functools

import jax
import jax.numpy as jnp
from jax import lax
from jax.experimental import pallas as pl
from jax.experimental.pallas import tpu as pltpu

F32, BF16 = jnp.float32, jnp.bfloat16
SDS = jax.ShapeDtypeStruct
MESH = pl.DeviceIdType.MESH

D_MODEL = 2048
ATTN_WIDTH = 1024
HGRN_WIDTH = 1024
KV_WIDTH = 256
N_Q_HEADS = 16
N_KV_HEADS = 4
Q_PER_KV = 4
WINDOW = 128
MASK_VALUE = -1e30
ROPE_THETA = 10000.0
HGRN_HEADS = 8
HGRN_CHUNK = 16
D_FF = 5632
D_PLE = 256
RMS_EPS = 1e-6
LANES = 128
N_CHIPS = 4
COL_Q, COL_K, COL_V, COL_HQ, COL_HF, COL_HI, COL_HG = 0, 8, 10, 12, 20, 28, 36

ADAM_LR, ADAM_B1, ADAM_B2, ADAM_EPS, ADAM_WD, ADAM_STEP = 0.001, 0.9, 0.999, 1e-08, 0.01, 10

VMEM_LIMIT = 56 * 1024 * 1024
ROW_TILE = 256

_NN = (((1,), (0,)), ((), ()))
_NT = (((1,), (1,)), ((), ()))
_TN = (((0,), (0,)), ((), ()))


def _pc(body, *, name, out_shape, in_specs, out_specs, grid=(), scratch=(), sem=None, grid_spec=None, **kw):
    params = dict(vmem_limit_bytes=VMEM_LIMIT)
    if sem is not None:
        params["dimension_semantics"] = sem
    if grid_spec is not None:
        return pl.pallas_call(body, name=name, out_shape=out_shape, grid_spec=grid_spec,
                              compiler_params=pltpu.CompilerParams(**params), **kw)
    return pl.pallas_call(body, name=name, out_shape=out_shape, grid=grid, in_specs=in_specs, out_specs=out_specs,
                          scratch_shapes=list(scratch), compiler_params=pltpu.CompilerParams(**params), **kw)


def _sigmoid(x):
    return 1.0 / (1.0 + jnp.exp(-x))


def _rstd(x):
    return lax.rsqrt(jnp.mean(x * x, axis=-1, keepdims=True) + RMS_EPS)


def _rows(t, w, col=0):
    return pl.BlockSpec((t, w), lambda i, col=col: (i, col))


def _fixed(shape):
    return pl.BlockSpec(shape, lambda *_: (0,) * len(shape))


def _mm(name, a, b, *, dims, grid, a_spec, b_spec, o_spec, out_shape, parts=1, add=None, add_spec=None, after=None):
    def body(*refs):
        a_ref, b_ref, o_ref = refs[0], refs[1], refs[-1]
        if parts == 1:
            r = lax.dot_general(a_ref[...].astype(BF16), b_ref[...].astype(BF16), dims, preferred_element_type=F32)
        else:
            w = a_ref.shape[1] // parts
            r = None
            for j in range(parts):
                t = lax.dot_general(a_ref[:, j * w:(j + 1) * w].astype(BF16), b_ref[j].astype(BF16), dims,
                                    preferred_element_type=F32)
                r = t if r is None else r + t
        if add is not None:
            r = r + refs[2][...]
        o_ref[...] = r.astype(o_ref.dtype)

    ins = [a, b] + ([] if add is None else [add]) + ([] if after is None else [after])
    specs = [a_spec, b_spec] + ([] if add is None else [add_spec]) + ([] if after is None else [pl.BlockSpec(memory_space=pl.ANY)])
    return _pc(body, name=name, grid=grid, in_specs=specs, out_specs=o_spec, out_shape=out_shape,
               sem=("parallel",) * len(grid))(*ins)


def _tile(n, t):
    if n <= t:
        return n
    while n % t:
        t //= 2
    assert t % 8 == 0
    return t


def mm_col(name, a, wg, out_dtype=F32):
    s, k = a.shape
    _, _, n = wg.shape
    tm = _tile(s, 512)
    return _mm(name, a, wg, dims=_NN, grid=(N_CHIPS, s // tm),
               a_spec=pl.BlockSpec((tm, k), lambda j, i: (i, 0)),
               b_spec=pl.BlockSpec((None, k, n), lambda j, i: (j, 0, 0)),
               o_spec=pl.BlockSpec((tm, n), lambda j, i: (i, j)),
               out_shape=SDS((s, N_CHIPS * n), out_dtype))


def mm_row(name, a, wg, out_dtype=F32):
    s, _ = a.shape
    _, r, n = wg.shape
    tm = _tile(s, 512)
    tn = _tile(n, 1024 if r > 512 else 2048)
    return _mm(name, a, wg, dims=_NN, grid=(n // tn, s // tm), parts=N_CHIPS,
               a_spec=pl.BlockSpec((tm, N_CHIPS * r), lambda j, i: (i, 0)),
               b_spec=pl.BlockSpec((N_CHIPS, r, tn), lambda j, i: (0, 0, j)),
               o_spec=pl.BlockSpec((tm, tn), lambda j, i: (i, j)),
               out_shape=SDS((s, n), out_dtype))


def mm_col_t(name, dy, wg, add=None, out_dtype=F32, after=None):
    s, _ = dy.shape
    _, k, n = wg.shape
    tm = _tile(s, 512)
    tk = _tile(k, 1024)
    return _mm(name, dy, wg, dims=_NT, grid=(k // tk, s // tm), parts=N_CHIPS,
               a_spec=pl.BlockSpec((tm, N_CHIPS * n), lambda j, i: (i, 0)),
               b_spec=pl.BlockSpec((N_CHIPS, tk, n), lambda j, i: (0, j, 0)),
               o_spec=pl.BlockSpec((tm, tk), lambda j, i: (i, j)),
               add=add, add_spec=pl.BlockSpec((tm, tk), lambda j, i: (i, j)),
               out_shape=SDS((s, k), out_dtype), after=after)


def mm_row_t(name, dy, wg, out_dtype=F32):
    s, n = dy.shape
    _, r, _ = wg.shape
    tm = _tile(s, 512)
    return _mm(name, dy, wg, dims=_NT, grid=(N_CHIPS, s // tm),
               a_spec=pl.BlockSpec((tm, n), lambda j, i: (i, 0)),
               b_spec=pl.BlockSpec((None, r, n), lambda j, i: (j, 0, 0)),
               o_spec=pl.BlockSpec((tm, r), lambda j, i: (i, j)),
               out_shape=SDS((s, N_CHIPS * r), out_dtype))


def norm_bwd_pair(name, x_a, gain_a, dy, dres, x_b, gain_b):
    s, d = x_a.shape
    t = _tile(s, ROW_TILE)

    def one(xv, g, dyv):
        r = _rstd(xv)
        xh = xv * r
        dyg = dyv * g
        return r * (dyg - xh * jnp.mean(dyg * xh, axis=-1, keepdims=True)), jnp.sum(dyv * xh, axis=0, keepdims=True)

    def body(xa_ref, ga_ref, dy_ref, r_ref, xb_ref, gb_ref, dx_ref, db_ref, dga_ref, dgb_ref):
        dx, pa = one(xa_ref[...], ga_ref[...], dy_ref[...])
        dx = dx + r_ref[...]
        dx_ref[...] = dx
        db, pb = one(xb_ref[...], gb_ref[...], dx)
        db_ref[...] = db.astype(BF16)

        @pl.when(pl.program_id(0) == 0)
        def _():
            dga_ref[...] = pa
            dgb_ref[...] = pb

        @pl.when(pl.program_id(0) > 0)
        def _():
            dga_ref[...] += pa
            dgb_ref[...] += pb

    row, gain = _rows(t, d), _fixed((1, d))
    return _pc(body, name=name, grid=(s // t,), in_specs=[row, gain, row, row, row, gain], out_specs=[row, row, gain, gain],
               out_shape=[SDS((s, d), F32), SDS((s, d), BF16), SDS((1, d), F32), SDS((1, d), F32)],
               sem=("arbitrary",))(x_a, gain_a, dy, dres, x_b, gain_b)


def ffn_gate_up(name, h, wg_gate, wg_up):
    s, k = h.shape
    _, _, n = wg_gate.shape
    tm = _tile(s, 512)

    def body(h_ref, wg_ref, wu_ref, g_ref, u_ref, a_ref):
        hv = h_ref[...]
        g = jnp.dot(hv, wg_ref[...], preferred_element_type=F32)
        u = jnp.dot(hv, wu_ref[...], preferred_element_type=F32)
        g_ref[...] = g
        u_ref[...] = u
        a_ref[...] = ((g * _sigmoid(g)) * u).astype(BF16)

    wspec = pl.BlockSpec((None, k, n), lambda j, i: (j, 0, 0))
    ospec = pl.BlockSpec((tm, n), lambda j, i: (i, j))
    return _pc(body, name=name, grid=(N_CHIPS, s // tm), in_specs=[pl.BlockSpec((tm, k), lambda j, i: (i, 0)), wspec, wspec],
               out_specs=[ospec] * 3, out_shape=[SDS((s, N_CHIPS * n), F32)] * 2 + [SDS((s, N_CHIPS * n), BF16)],
               sem=("parallel", "parallel"))(h, wg_gate, wg_up)


def ffn_down_bwd(name, df, wg_down, g, u):
    s, n = df.shape
    _, r, _ = wg_down.shape
    tm = _tile(s, 512)

    def body(df_ref, w_ref, g_ref, u_ref, dg_ref, du_ref):
        da = lax.dot_general(df_ref[...], w_ref[...], _NT, preferred_element_type=F32)
        gv = g_ref[...]
        sg = _sigmoid(gv)
        du_ref[...] = (da * (gv * sg)).astype(BF16)
        dg_ref[...] = ((da * u_ref[...]) * (sg * (1.0 + gv * (1.0 - sg)))).astype(BF16)

    cspec = pl.BlockSpec((tm, r), lambda j, i: (i, j))
    return _pc(body, name=name, grid=(N_CHIPS, s // tm),
               in_specs=[pl.BlockSpec((tm, n), lambda j, i: (i, 0)), pl.BlockSpec((None, r, n), lambda j, i: (j, 0, 0)), cspec, cspec],
               out_specs=[cspec] * 2, out_shape=[SDS((s, N_CHIPS * r), BF16)] * 2,
               sem=("parallel", "parallel"))(df, wg_down, g, u)


def mm_wg_col(name, a, dy):
    s, k = a.shape
    n = dy.shape[1] // N_CHIPS
    tm = _tile(k // 2, 512)
    hb = (k // 2) // tm
    return _mm(name, a, dy, dims=_TN, grid=(N_CHIPS, k // tm),
               a_spec=pl.BlockSpec((s, tm), lambda j, i: (0, i)),
               b_spec=pl.BlockSpec((s, n), lambda j, i: (0, j)),
               o_spec=pl.BlockSpec((None, None, tm, n), lambda j, i: (i // hb, j, i % hb, 0)),
               out_shape=SDS((2, N_CHIPS, k // 2, n), BF16))


def mm_wg_row(name, a, dy):
    s, n = dy.shape
    r = a.shape[1] // N_CHIPS
    tn = _tile(n // 2, 512)
    nb = (n // 2) // tn
    return _mm(name, a, dy, dims=_TN, grid=(N_CHIPS, n // tn),
               a_spec=pl.BlockSpec((s, r), lambda j, i: (0, j)),
               b_spec=pl.BlockSpec((s, tn), lambda j, i: (0, i)),
               o_spec=pl.BlockSpec((None, None, r, tn), lambda j, i: (i // nb, j, 0, i % nb)),
               out_shape=SDS((2, N_CHIPS, r, n // 2), BF16))


def pre_norm(name, x, gain):
    s, d = x.shape
    t = _tile(s, ROW_TILE)

    def body(x_ref, g_ref, o_ref):
        xv = x_ref[...]
        o_ref[...] = ((xv * _rstd(xv)) * g_ref[...]).astype(BF16)

    return _pc(body, name=name, grid=(s // t,), in_specs=[_rows(t, d), _fixed((1, d))], out_specs=_rows(t, d),
               out_shape=SDS((s, d), BF16), sem=("parallel",))(x, gain)


def post_pre_norm(name, m, g_post, res, g_pre):
    s, d = m.shape
    t = _tile(s, ROW_TILE)

    def body(m_ref, gp_ref, r_ref, gn_ref, x_ref, h_ref):
        mv = m_ref[...]
        xn = r_ref[...] + (mv * _rstd(mv)) * gp_ref[...]
        x_ref[...] = xn
        h_ref[...] = ((xn * _rstd(xn)) * gn_ref[...]).astype(BF16)

    return _pc(body, name=name, grid=(s // t,),
               in_specs=[_rows(t, d), _fixed((1, d)), _rows(t, d), _fixed((1, d))],
               out_specs=[_rows(t, d), _rows(t, d)], out_shape=[SDS((s, d), F32), SDS((s, d), BF16)],
               sem=("parallel",))(m, g_post, res, g_pre)


def _row_dot(a_ref, w_ref):
    r = w_ref.shape[1]
    out = None
    for j in range(N_CHIPS):
        part = jnp.dot(a_ref[:, j * r:(j + 1) * r], w_ref[j], preferred_element_type=F32)
        out = part if out is None else out + part
    return out


def _row_dot_specs(t, a, wg):
    return [_rows(t, a.shape[1]), _fixed(wg.shape)]


def out_proj_post_mix(name, a, wg, g_post, res, g_pre):
    s, d = res.shape
    t = _tile(s, ROW_TILE)

    def body(a_ref, w_ref, gp_ref, r_ref, gn_ref, m_ref, x_ref, h_ref):
        mv = _row_dot(a_ref, w_ref)
        m_ref[...] = mv
        xn = r_ref[...] + (mv * _rstd(mv)) * gp_ref[...]
        x_ref[...] = xn
        h_ref[...] = ((xn * _rstd(xn)) * gn_ref[...]).astype(BF16)

    return _pc(body, name=name, grid=(s // t,),
               in_specs=_row_dot_specs(t, a, wg) + [_fixed((1, d)), _rows(t, d), _fixed((1, d))],
               out_specs=[_rows(t, d)] * 3, out_shape=[SDS((s, d), F32), SDS((s, d), F32), SDS((s, d), BF16)],
               sem=("parallel",))(a, wg, g_post, res, g_pre)


def ple_gate_fwd_mid(name, a, wg, pp, x2, g_next):
    s, d = x2.shape
    t = _tile(s, ROW_TILE)

    def body(a_ref, w_ref, p_ref, x_ref, g_ref, z_ref, xo_ref, h_ref):
        z = _row_dot(a_ref, w_ref)
        z_ref[...] = z
        xn = x_ref[...] + p_ref[...] * _sigmoid(z)
        xo_ref[...] = xn
        h_ref[...] = ((xn * _rstd(xn)) * g_ref[...]).astype(BF16)

    return _pc(body, name=name, grid=(s // t,),
               in_specs=_row_dot_specs(t, a, wg) + [_rows(t, d), _rows(t, d), _fixed((1, d))],
               out_specs=[_rows(t, d)] * 3, out_shape=[SDS((s, d), F32), SDS((s, d), F32), SDS((s, d), BF16)],
               sem=("parallel",))(a, wg, pp, x2, g_next)


def ple_gate_fwd_loss(name, a, wg, pp, x2, target):
    s, d = x2.shape
    t = _tile(s, ROW_TILE)

    def body(a_ref, w_ref, p_ref, x_ref, t_ref, dy_ref, dpp_ref, dz_ref, l_ref):
        gate = _sigmoid(_row_dot(a_ref, w_ref))
        pv = p_ref[...]
        err = (x_ref[...] + pv * gate) - t_ref[...]
        dy = err * (1.0 / d)
        dy_ref[...] = dy
        dpp_ref[...] = (dy * gate).astype(BF16)
        dz_ref[...] = ((dy * pv) * (gate * (1.0 - gate))).astype(BF16)
        part = jnp.sum(jnp.sum(err * err, axis=-1, keepdims=True), axis=0, keepdims=True) * (0.5 / d)

        @pl.when(pl.program_id(0) == 0)
        def _():
            l_ref[...] = part

        @pl.when(pl.program_id(0) > 0)
        def _():
            l_ref[...] += part

    return _pc(body, name=name, grid=(s // t,),
               in_specs=_row_dot_specs(t, a, wg) + [_rows(t, d), _rows(t, d), _rows(t, d)],
               out_specs=[_rows(t, d), _rows(t, d), _rows(t, d), _fixed((1, 1))],
               out_shape=[SDS((s, d), F32), SDS((s, d), BF16), SDS((s, d), BF16), SDS((1, 1), F32)],
               sem=("arbitrary",))(a, wg, pp, x2, target)


ANY_SPEC = pl.BlockSpec(memory_space=pl.ANY)


def ple_bwd(name, dx3, z, pp, after):
    s, d = z.shape
    t = _tile(s, ROW_TILE)

    def body(d_ref, z_ref, p_ref, after_ref, dpp_ref, dz_ref):
        gate = _sigmoid(z_ref[...])
        dv = d_ref[...]
        dpp_ref[...] = (dv * gate).astype(BF16)
        dz_ref[...] = ((dv * p_ref[...]) * (gate * (1.0 - gate))).astype(BF16)

    return _pc(body, name=name, grid=(s // t,), in_specs=[_rows(t, d)] * 3 + [ANY_SPEC], out_specs=[_rows(t, d)] * 2,
               out_shape=[SDS((s, d), BF16)] * 2, sem=("parallel",))(dx3, z, pp, after)


def norm_bwd(name, xin, gain, dy, dres=None, out_dtype=F32):
    s, d = xin.shape
    t = _tile(s, ROW_TILE)

    def body(*refs):
        if dres is None:
            x_ref, g_ref, dy_ref, dx_ref, dg_ref = refs
            r_ref = None
        else:
            x_ref, g_ref, dy_ref, r_ref, dx_ref, dg_ref = refs
        xv = x_ref[...]
        r = _rstd(xv)
        xh = xv * r
        dyv = dy_ref[...].astype(F32)
        dyg = dyv * g_ref[...]
        c = jnp.mean(dyg * xh, axis=-1, keepdims=True)
        dx = r * (dyg - xh * c)
        if r_ref is not None:
            dx = dx + r_ref[...]
        dx_ref[...] = dx.astype(out_dtype)
        part = jnp.sum(dyv * xh, axis=0, keepdims=True)

        @pl.when(pl.program_id(0) == 0)
        def _():
            dg_ref[...] = part

        @pl.when(pl.program_id(0) > 0)
        def _():
            dg_ref[...] += part

    ins = [xin, gain, dy] + ([] if dres is None else [dres])
    specs = [_rows(t, d), _fixed((1, d)), _rows(t, d)] + ([] if dres is None else [_rows(t, d)])
    return _pc(body, name=name, grid=(s // t,), in_specs=specs, out_specs=[_rows(t, d), _fixed((1, d))],
               out_shape=[SDS((s, d), out_dtype), SDS((1, d), F32)], sem=("arbitrary",))(*ins)


def _lane(shape):
    return lax.broadcasted_iota(jnp.int32, shape, 1)


def _swap_halves(x):
    lo = (_lane(x.shape) % 64) < 32
    return jnp.where(lo, pltpu.roll(x, 96, 1), pltpu.roll(x, 32, 1))


def rope_tables(name, pos_col, inv_freq):
    s = pos_col.shape[0]
    t = _tile(s, ROW_TILE)

    def body(p_ref, f_ref, c_ref, s_ref):
        ang = p_ref[...].astype(F32) * f_ref[...]
        lo = (_lane(ang.shape) % 64) < 32
        c_ref[...] = jnp.cos(ang)
        sn = jnp.sin(ang)
        s_ref[...] = jnp.where(lo, -sn, sn)

    return _pc(body, name=name, grid=(s // t,), in_specs=[_rows(t, 1), _fixed((1, LANES))],
               out_specs=[_rows(t, LANES)] * 2, out_shape=[SDS((s, LANES), F32)] * 2, sem=("parallel",))(pos_col, inv_freq)


def _pad_heads(chunk, lo_mask):
    zero = jnp.zeros_like(chunk)
    return jnp.where(lo_mask, chunk, zero), jnp.where(lo_mask, pltpu.roll(chunk, 64, 1), zero)


def rope_qkv(name, proj, cos, sin):
    s = proj.shape[0]
    t = _tile(s, ROW_TILE)

    def body(q_ref, kv_ref, c_ref, s_ref, qp_ref, kp_ref, vp_ref):
        cs, sn = c_ref[...], s_ref[...]
        lo_mask = _lane(cs.shape) < 64

        def rot(x):
            return x * cs + _swap_halves(x) * sn

        for j in range(ATTN_WIDTH // LANES):
            a, b = _pad_heads(rot(q_ref[:, j * LANES:(j + 1) * LANES]), lo_mask)
            qp_ref[:, (2 * j) * LANES:(2 * j + 1) * LANES] = a.astype(BF16)
            qp_ref[:, (2 * j + 1) * LANES:(2 * j + 2) * LANES] = b.astype(BF16)
        for j in range(KV_WIDTH // LANES):
            a, b = _pad_heads(rot(kv_ref[:, j * LANES:(j + 1) * LANES]), lo_mask)
            kp_ref[:, (2 * j) * LANES:(2 * j + 1) * LANES] = a.astype(BF16)
            kp_ref[:, (2 * j + 1) * LANES:(2 * j + 2) * LANES] = b.astype(BF16)
            a, b = _pad_heads(kv_ref[:, KV_WIDTH + j * LANES:KV_WIDTH + (j + 1) * LANES], lo_mask)
            vp_ref[:, (2 * j) * LANES:(2 * j + 1) * LANES] = a.astype(BF16)
            vp_ref[:, (2 * j + 1) * LANES:(2 * j + 2) * LANES] = b.astype(BF16)

    return _pc(body, name=name, grid=(s // t,),
               in_specs=[_rows(t, ATTN_WIDTH, 0), _rows(t, 2 * KV_WIDTH, 2), _rows(t, LANES), _rows(t, LANES)],
               out_specs=[_rows(t, N_Q_HEADS * LANES), _rows(t, N_KV_HEADS * LANES), _rows(t, N_KV_HEADS * LANES)],
               out_shape=[SDS((s, N_Q_HEADS * LANES), BF16), SDS((s, N_KV_HEADS * LANES), BF16),
                          SDS((s, N_KV_HEADS * LANES), BF16)],
               sem=("parallel",))(proj, proj, cos, sin)


def _attn_mask(n):
    L = WINDOW
    qi = lax.broadcasted_iota(jnp.int32, (L, 2 * L), 0) + L
    ki = lax.broadcasted_iota(jnp.int32, (L, 2 * L), 1)
    rel = qi - ki
    return (rel >= 0) & (rel < WINDOW) & ((n > 0) | (ki >= L))


def _attn_scores(qh, kk, valid):
    sc = lax.dot_general(qh, kk, _NT, preferred_element_type=F32) * 0.125
    return jnp.where(valid, sc, MASK_VALUE)


def _attn_softmax(sc, sink):
    m = jnp.maximum(jnp.max(sc, axis=-1, keepdims=True), sink)
    e = jnp.exp(sc - m)
    es = jnp.exp(sink - m)
    den = jnp.sum(e, axis=-1, keepdims=True) + es
    return e / den, es / den


def _attn_specs(s):
    L = WINDOW
    cur = lambda n: (n, 0)
    prev = lambda n: (jnp.maximum(n - 1, 0), 0)
    kvw = N_KV_HEADS * LANES
    return [pl.BlockSpec((L, N_Q_HEADS * LANES), cur), pl.BlockSpec((L, kvw), cur), pl.BlockSpec((L, kvw), prev),
            pl.BlockSpec((L, kvw), cur), pl.BlockSpec((L, kvw), prev), pl.BlockSpec(memory_space=pltpu.SMEM)]


def attn_fwd(name, qp, kp, vp, sinks):
    s = qp.shape[0]
    L = WINDOW

    def body(q_ref, kc_ref, kp_ref, vc_ref, vp_ref, sk_ref, o_ref):
        valid = _attn_mask(pl.program_id(0))
        kks, vvs = [], []
        for kvh in range(N_KV_HEADS):
            cols = slice(kvh * LANES, (kvh + 1) * LANES)
            kks.append(jnp.concatenate([kp_ref[:, cols], kc_ref[:, cols]], axis=0))
            vvs.append(jnp.concatenate([vp_ref[:, cols], vc_ref[:, cols]], axis=0))
        scs = [_attn_scores(q_ref[:, h * LANES:(h + 1) * LANES], kks[h // Q_PER_KV], valid) for h in range(N_Q_HEADS)]
        pbs = [_attn_softmax(scs[h], sk_ref[h])[0].astype(BF16) for h in range(N_Q_HEADS)]
        outs = [jnp.dot(pbs[h], vvs[h // Q_PER_KV], preferred_element_type=F32) for h in range(N_Q_HEADS)]
        for j in range(ATTN_WIDTH // LANES):
            o_ref[:, j * LANES:(j + 1) * LANES] = outs[2 * j] + pltpu.roll(outs[2 * j + 1], 64, 1)

    return _pc(body, name=name, grid=(s // L,), in_specs=_attn_specs(s),
               out_specs=pl.BlockSpec((L, ATTN_WIDTH), lambda n: (n, 0)),
               out_shape=SDS((s, ATTN_WIDTH), F32), sem=("parallel",))(qp, kp, kp, vp, vp, sinks)


def attn_bwd(name, qp, kp, vp, sinks, dattn):
    s = qp.shape[0]
    L = WINDOW
    kvw = N_KV_HEADS * LANES

    def body(q_ref, kc_ref, kp_ref, vc_ref, vp_ref, sk_ref, do_ref, dq_ref, dkc_ref, dkp_ref, dvc_ref, dvp_ref, ds_ref):
        n = pl.program_id(0)
        valid = _attn_mask(n)
        lo_mask = _lane((L, LANES)) < 64
        lane1 = _lane((1, LANES))
        dsink = jnp.zeros((1, LANES), F32)
        heads = range(N_Q_HEADS)
        kks, vvs = [], []
        for kvh in range(N_KV_HEADS):
            cols = slice(kvh * LANES, (kvh + 1) * LANES)
            kks.append(jnp.concatenate([kp_ref[:, cols], kc_ref[:, cols]], axis=0))
            vvs.append(jnp.concatenate([vp_ref[:, cols], vc_ref[:, cols]], axis=0))
        qs, dos, scs, dps = [], [], [], []
        for h in heads:
            qs.append(q_ref[:, h * LANES:(h + 1) * LANES])
            chunk = do_ref[:, (h // 2) * LANES:(h // 2 + 1) * LANES]
            if h % 2:
                chunk = pltpu.roll(chunk, 64, 1)
            dos.append(jnp.where(lo_mask, chunk, 0.0).astype(BF16))
            scs.append(_attn_scores(qs[h], kks[h // Q_PER_KV], valid))
            dps.append(lax.dot_general(dos[h], vvs[h // Q_PER_KV], _NT, preferred_element_type=F32))
        pbs, dsbs = [], []
        for h in heads:
            p, ps = _attn_softmax(scs[h], sk_ref[h])
            delta = jnp.sum(p * dps[h], axis=-1, keepdims=True)
            dsbs.append(((p * (dps[h] - delta)) * 0.125).astype(BF16))
            pbs.append(p.astype(BF16))
            dsink = dsink + jnp.where(lane1 == h, -jnp.sum(ps * delta, axis=0, keepdims=True), 0.0)
        for kvh in range(N_KV_HEADS):
            cols = slice(kvh * LANES, (kvh + 1) * LANES)
            dkk = jnp.zeros((2 * L, LANES), F32)
            dvv = jnp.zeros((2 * L, LANES), F32)
            for h in range(kvh * Q_PER_KV, (kvh + 1) * Q_PER_KV):
                dq_ref[:, h * LANES:(h + 1) * LANES] = jnp.dot(dsbs[h], kks[kvh], preferred_element_type=F32)
                dkk = dkk + lax.dot_general(dsbs[h], qs[h], _TN, preferred_element_type=F32)
                dvv = dvv + lax.dot_general(pbs[h], dos[h], _TN, preferred_element_type=F32)
            dkp_ref[:, cols] = dkk[:L]
            dkc_ref[:, cols] = dkk[L:]
            dvp_ref[:, cols] = dvv[:L]
            dvc_ref[:, cols] = dvv[L:]

        @pl.when(n == 0)
        def _():
            ds_ref[...] = dsink

        @pl.when(n > 0)
        def _():
            ds_ref[...] += dsink

    blk = lambda w: pl.BlockSpec((L, w), lambda n: (n, 0))
    return _pc(body, name=name, grid=(s // L,), in_specs=_attn_specs(s) + [blk(ATTN_WIDTH)],
               out_specs=[blk(N_Q_HEADS * LANES), blk(kvw), blk(kvw), blk(kvw), blk(kvw), _fixed((1, LANES))],
               out_shape=[SDS((s, N_Q_HEADS * LANES), F32)] + [SDS((s, kvw), F32)] * 4 + [SDS((1, LANES), F32)],
               sem=("arbitrary",))(qp, kp, kp, vp, vp, sinks, dattn)


def rope_bwd(name, dqp, dkc, dkp, dvc, dvp, cos, sin):
    s = dqp.shape[0]
    L = WINDOW
    nb = s // L
    kvw = N_KV_HEADS * LANES

    def body(dq_ref, dkc_ref, dkp_ref, dvc_ref, dvp_ref, c_ref, s_ref, o_ref):
        cs, sn = c_ref[...], s_ref[...]
        more = (pl.program_id(0) < nb - 1).astype(F32)

        def unrot(x):
            return x * cs - _swap_halves(x) * sn

        def compact(ref, j, nxt=None):
            a = ref[:, (2 * j) * LANES:(2 * j + 1) * LANES]
            b = ref[:, (2 * j + 1) * LANES:(2 * j + 2) * LANES]
            if nxt is not None:
                a = a + more * nxt[:, (2 * j) * LANES:(2 * j + 1) * LANES]
                b = b + more * nxt[:, (2 * j + 1) * LANES:(2 * j + 2) * LANES]
            return a + pltpu.roll(b, 64, 1)

        for j in range(ATTN_WIDTH // LANES):
            o_ref[:, j * LANES:(j + 1) * LANES] = unrot(compact(dq_ref, j)).astype(BF16)
        for j in range(KV_WIDTH // LANES):
            o_ref[:, (COL_K + j) * LANES:(COL_K + j + 1) * LANES] = unrot(compact(dkc_ref, j, dkp_ref)).astype(BF16)
            o_ref[:, (COL_V + j) * LANES:(COL_V + j + 1) * LANES] = compact(dvc_ref, j, dvp_ref).astype(BF16)

    cur = lambda n: (n, 0)
    nxt = lambda n: (jnp.minimum(n + 1, nb - 1), 0)
    return _pc(body, name=name, grid=(nb,),
               in_specs=[pl.BlockSpec((L, N_Q_HEADS * LANES), cur), pl.BlockSpec((L, kvw), cur), pl.BlockSpec((L, kvw), nxt),
                         pl.BlockSpec((L, kvw), cur), pl.BlockSpec((L, kvw), nxt), pl.BlockSpec((L, LANES), cur),
                         pl.BlockSpec((L, LANES), cur)],
               out_specs=pl.BlockSpec((L, COL_HQ * LANES), cur), out_shape=SDS((s, COL_HQ * LANES), BF16),
               sem=("parallel",))(dqp, dkc, dkp, dvc, dvp, cos, sin)


def _split3(x):
    a = x.astype(BF16)
    r = x - a.astype(F32)
    b = r.astype(BF16)
    c = (r - b.astype(F32)).astype(BF16)
    return a, b, c


def _chunk_sum(x, upper):
    t = x.shape[0]
    ri = lax.broadcasted_iota(jnp.int32, (t, t), 0)
    ci = lax.broadcasted_iota(jnp.int32, (t, t), 1)
    same = (ri // HGRN_CHUNK) == (ci // HGRN_CHUNK)
    tri = (ci >= ri) if upper else (ci <= ri)
    m = jnp.where(same & tri, 1.0, 0.0).astype(BF16)
    out = None
    for part in _split3(x):
        y = jnp.dot(m, part, preferred_element_type=F32)
        out = y if out is None else out + y
    return out


def _lower_bound(l_ref, layer):
    lv = l_ref[...]
    e = jnp.exp(lv - jnp.max(lv, axis=0, keepdims=True))
    sm = e / jnp.sum(e, axis=0, keepdims=True)
    s0 = sm[0:1]
    return (s0 - s0) if layer == 0 else ((s0 + sm[1:2]) - s0)


def _hgrn_gates(hq_ref, hf_ref, lb):
    z = hf_ref[...]
    sg = _sigmoid(z)
    f = lb + (1.0 - lb) * sg
    kin = (1.0 - lb) * _sigmoid(-z)
    hq = hq_ref[...]
    sq = _sigmoid(hq)
    return sg, f, kin, hq, sq


def _shift_down(x, d):
    return x if d == 0 else pltpu.roll(x, d, 0)


def _shift_up(x, d):
    return x if d == 0 else pltpu.roll(x, x.shape[0] - d, 0)


CHUNKS_PER_BLOCK = LANES // HGRN_CHUNK


def _chunk_iotas():
    shape = (HGRN_CHUNK, LANES)
    return lax.broadcasted_iota(jnp.int32, shape, 0), lax.broadcasted_iota(jnp.int32, shape, 1)


def _chunk_rows(block, chunk):
    start = block * LANES + chunk * HGRN_CHUNK
    return slice(start, start + HGRN_CHUNK)


HGRN_HEADS_PER_STEP = 2
HGRN_STEP_WIDTH = HGRN_HEADS_PER_STEP * LANES


def _hgrn_specs(t, rev, nt):
    row = (lambda h, i: nt - 1 - i) if rev else (lambda h, i: i)
    col = lambda base: pl.BlockSpec((t, HGRN_STEP_WIDTH),
                                    lambda h, i, base=base: (row(h, i), base // HGRN_HEADS_PER_STEP + h))
    return col, row


def _head_views(refs, hh):
    return [r.at[:, pl.ds(hh * LANES, LANES)] for r in refs]


def hgrn_fwd(name, proj, lb_logits, layer):
    s = proj.shape[0]
    t = _tile(s, ROW_TILE)
    nt = s // t
    nc = t // HGRN_CHUNK
    col, row = _hgrn_specs(t, False, nt)

    def body(hq_ref, hf_ref, hi_ref, l_ref, o_ref, st_ref, state):
        @pl.when(pl.program_id(1) == 0)
        def _():
            state[...] = jnp.zeros_like(state)

        for hh in range(HGRN_HEADS_PER_STEP):
            head(*_head_views((hq_ref, hf_ref, hi_ref, l_ref, o_ref), hh), st_ref.at[:, hh], state.at[hh])

    def head(hq_ref, hf_ref, hi_ref, l_ref, o_ref, st_ref, state):
        lb = _lower_bound(l_ref, layer)
        sg, f, kin, hq, sq = _hgrn_gates(hq_ref, hf_ref, lb)
        q = hq * sq
        vb = hi_ref[...].astype(BF16)
        b = _chunk_sum(jnp.log(f), False)
        qe = (q * jnp.exp(b)).astype(BF16)
        trow, lane = _chunk_iotas()
        chunks = [(j, cc) for j in range(t // LANES) for cc in range(CHUNKS_PER_BLOCK)]
        decay, update = [], []
        for j, cc in chunks:
            rs = _chunk_rows(j, cc)
            bc = b[rs]
            bl = bc[HGRN_CHUNK - 1:HGRN_CHUNK, :]
            ke = (kin[rs] * jnp.exp(bl - bc)).astype(BF16)
            decay.append(jnp.exp(bl))
            update.append(lax.dot_general(vb[rs], ke, _TN, preferred_element_type=F32))
        st = state[...]
        for c in range(nc):
            st_ref[c] = st
            st = st * decay[c] + update[c]
        state[...] = st
        o_inter = [lax.dot_general(qe[c * HGRN_CHUNK:(c + 1) * HGRN_CHUNK], st_ref[c].astype(BF16), _NT,
                                   preferred_element_type=F32) for c in range(nc)]
        for j in range(t // LANES):
            blk = slice(j * LANES, (j + 1) * LANES)
            rows = []
            for cc in range(CHUNKS_PER_BLOCK):
                rs = _chunk_rows(j, cc)
                bc, qc, kc = b[rs], q[rs], kin[rs]
                here = trow + cc * HGRN_CHUNK
                am = jnp.where(lane == here, jnp.sum(qc * kc, axis=-1, keepdims=True), 0.0)
                for d in range(1, HGRN_CHUNK):
                    e = jnp.exp(jnp.where(trow >= d, bc - _shift_down(bc, d), MASK_VALUE))
                    a = jnp.sum((qc * _shift_down(kc, d)) * e, axis=-1, keepdims=True)
                    am = jnp.where(lane == here - d, a, am)
                rows.append(am)
            o_intra = jnp.dot(jnp.concatenate(rows, axis=0).astype(BF16), vb[blk], preferred_element_type=F32)
            for cc in range(CHUNKS_PER_BLOCK):
                rs = _chunk_rows(j, cc)
                o_ref[rs, :] = o_intra[cc * HGRN_CHUNK:(cc + 1) * HGRN_CHUNK] + o_inter[j * CHUNKS_PER_BLOCK + cc]

    hp = HGRN_HEADS_PER_STEP
    return _pc(body, name=name, grid=(HGRN_HEADS // hp, nt),
               in_specs=[col(COL_HQ), col(COL_HF), col(COL_HI), pl.BlockSpec((2, HGRN_STEP_WIDTH), lambda h, i: (0, h))],
               out_specs=[pl.BlockSpec((t, HGRN_STEP_WIDTH), lambda h, i: (i, h)),
                          pl.BlockSpec((nc, hp, LANES, LANES), lambda h, i: (i, h, 0, 0))],
               out_shape=[SDS((s, HGRN_WIDTH), F32), SDS((s // HGRN_CHUNK, HGRN_HEADS, LANES, LANES), F32)],
               scratch=[pltpu.VMEM((hp, LANES, LANES), F32)],
               sem=("parallel", "arbitrary"))(proj, proj, proj, lb_logits)


def hgrn_bwd(name, proj, lb_logits, layer, states, do):
    s = proj.shape[0]
    t = _tile(s, ROW_TILE)
    nt = s // t
    nc = t // HGRN_CHUNK
    col, row = _hgrn_specs(t, True, nt)

    def body(hq_ref, hf_ref, hi_ref, l_ref, st_ref, do_ref, dhq_ref, dhf_ref, dhi_ref, dlb_ref, dstate):
        @pl.when(pl.program_id(1) == 0)
        def _():
            dstate[...] = jnp.zeros_like(dstate)

        for hh in range(HGRN_HEADS_PER_STEP):
            hq_v, hf_v, hi_v, l_v, do_v, dhq_v, dhf_v, dhi_v, dlb_v = _head_views(
                (hq_ref, hf_ref, hi_ref, l_ref, do_ref, dhq_ref, dhf_ref, dhi_ref, dlb_ref), hh)
            head(hq_v, hf_v, hi_v, l_v, st_ref.at[:, hh], do_v, dhq_v, dhf_v, dhi_v, dlb_v, dstate.at[hh])

    def head(hq_ref, hf_ref, hi_ref, l_ref, st_ref, do_ref, dhq_ref, dhf_ref, dhi_ref, dlb_ref, dstate):
        first = pl.program_id(1) == 0
        lb = _lower_bound(l_ref, layer)
        sg, f, kin, hq, sq = _hgrn_gates(hq_ref, hf_ref, lb)
        q = hq * sq
        vb = hi_ref[...].astype(BF16)
        b = _chunk_sum(jnp.log(f), False)
        dob = do_ref[...].astype(BF16)
        eb = jnp.exp(b)
        qe = q * eb
        qeb = qe.astype(BF16)
        trow, lane = _chunk_iotas()
        last_row = trow == HGRN_CHUNK - 1

        decay, update = [None] * nc, [None] * nc
        for c in range(nc):
            rs = slice(c * HGRN_CHUNK, (c + 1) * HGRN_CHUNK)
            decay[c] = jnp.exp(b[(c + 1) * HGRN_CHUNK - 1:(c + 1) * HGRN_CHUNK, :])
            update[c] = lax.dot_general(dob[rs], qeb[rs], _TN, preferred_element_type=F32)
        dn_in = [None] * nc
        dn = dstate[...]
        for c in reversed(range(nc)):
            dn_in[c] = dn
            dn = dn * decay[c] + update[c]
        dstate[...] = dn

        dq_c, dk_c, dv_c, dbl_c = [None] * nc, [None] * nc, [None] * nc, [None] * nc
        for c in range(nc):
            rs = slice(c * HGRN_CHUNK, (c + 1) * HGRN_CHUNK)
            bc = b[rs]
            ekb = jnp.exp(bc[HGRN_CHUNK - 1:HGRN_CHUNK, :] - bc)
            ke = kin[rs] * ekb
            st = st_ref[c]
            dnb = dn_in[c].astype(BF16)
            dke = jnp.dot(vb[rs], dnb, preferred_element_type=F32)
            dq_c[c] = jnp.dot(dob[rs], st.astype(BF16), preferred_element_type=F32) * eb[rs]
            dk_c[c] = dke * ekb
            dv_c[c] = lax.dot_general(ke.astype(BF16), dnb, _NT, preferred_element_type=F32)
            dbl_c[c] = jnp.sum(dn_in[c] * st, axis=0, keepdims=True) * decay[c] + jnp.sum(dke * ke, axis=0, keepdims=True)

        db_c = [None] * nc
        for j in range(t // LANES):
            blk = slice(j * LANES, (j + 1) * LANES)
            damat = lax.dot_general(dob[blk], vb[blk], _NT, preferred_element_type=F32)
            rows = [None] * CHUNKS_PER_BLOCK
            for cc in range(CHUNKS_PER_BLOCK):
                c = j * CHUNKS_PER_BLOCK + cc
                rs = _chunk_rows(j, cc)
                bc, qc, kc = b[rs], q[rs], kin[rs]
                dam = damat[cc * HGRN_CHUNK:(cc + 1) * HGRN_CHUNK]
                here = trow + cc * HGRN_CHUNK
                on = lane == here
                da = jnp.sum(jnp.where(on, dam, 0.0), axis=-1, keepdims=True)
                am = jnp.where(on, jnp.sum(qc * kc, axis=-1, keepdims=True), 0.0)
                dq = dq_c[c] + da * kc
                dk = dk_c[c] + da * qc
                for d in range(1, HGRN_CHUNK):
                    on = lane == here - d
                    e = jnp.exp(jnp.where(trow >= d, bc - _shift_down(bc, d), MASK_VALUE))
                    kse = _shift_down(kc, d) * e
                    am = jnp.where(on, jnp.sum(qc * kse, axis=-1, keepdims=True), am)
                    da = jnp.sum(jnp.where(on, dam, 0.0), axis=-1, keepdims=True)
                    dq = dq + da * kse
                    dk = dk + _shift_up(da * (qc * e), d)
                rows[cc] = am
                dq_c[c], dk_c[c] = dq, dk
                db_c[c] = (qc * dq - kc * dk) + jnp.where(last_row, dbl_c[c], 0.0)
            dv_blk = lax.dot_general(jnp.concatenate(rows, axis=0).astype(BF16), dob[blk], _TN, preferred_element_type=F32)
            for cc in range(CHUNKS_PER_BLOCK):
                c = j * CHUNKS_PER_BLOCK + cc
                dv_c[c] = dv_c[c] + dv_blk[cc * HGRN_CHUNK:(cc + 1) * HGRN_CHUNK]
        dq = jnp.concatenate(dq_c, axis=0)
        dk = jnp.concatenate(dk_c, axis=0)
        dv = jnp.concatenate(dv_c, axis=0)
        db = jnp.concatenate(db_c, axis=0)
        dg = _chunk_sum(db, True)
        dhq_ref[...] = (dq * (sq * (1.0 + hq * (1.0 - sq)))).astype(BF16)
        dhi_ref[...] = dv.astype(BF16)
        dfk = dg / f - dk
        dhf_ref[...] = ((dfk * (1.0 - lb)) * (sg * (1.0 - sg))).astype(BF16)
        part = jnp.sum(dfk * (1.0 - sg), axis=0, keepdims=True)

        @pl.when(first)
        def _():
            dlb_ref[...] = part

        @pl.when(jnp.logical_not(first))
        def _():
            dlb_ref[...] += part

    hp = HGRN_HEADS_PER_STEP
    out_col = pl.BlockSpec((t, HGRN_STEP_WIDTH), lambda h, i: (nt - 1 - i, h))
    return _pc(body, name=name, grid=(HGRN_HEADS // hp, nt),
               in_specs=[col(COL_HQ), col(COL_HF), col(COL_HI), pl.BlockSpec((2, HGRN_STEP_WIDTH), lambda h, i: (0, h)),
                         pl.BlockSpec((nc, hp, LANES, LANES), lambda h, i: (nt - 1 - i, h, 0, 0)), out_col],
               out_specs=[out_col, out_col, out_col, pl.BlockSpec((1, HGRN_STEP_WIDTH), lambda h, i: (0, h))],
               out_shape=[SDS((s, HGRN_WIDTH), BF16)] * 3 + [SDS((1, HGRN_WIDTH), F32)],
               scratch=[pltpu.VMEM((hp, LANES, LANES), F32)],
               sem=("parallel", "arbitrary"))(proj, proj, proj, lb_logits, states, do)


def mix_out_fwd(name, attn, o, proj, g_attn, g_hgrn):
    s = attn.shape[0]
    t = _tile(s, ROW_TILE)
    half = HGRN_WIDTH // 2

    def body(a_ref, o_ref, hg0_ref, hg1_ref, ga_ref, gh_ref, c_ref):
        av = a_ref[...]
        c_ref[:, :ATTN_WIDTH] = ((av * _rstd(av)) * ga_ref[...]).astype(BF16)
        for j in range(HGRN_HEADS):
            cols = slice(j * LANES, (j + 1) * LANES)
            ov = o_ref[:, cols]
            hg_ref, hcols = (hg0_ref, cols) if j < 4 else (hg1_ref, slice((j - 4) * LANES, (j - 3) * LANES))
            hg = hg_ref[:, hcols]
            on = (ov * _rstd(ov)) * gh_ref[:, cols]
            c_ref[:, ATTN_WIDTH + j * LANES:ATTN_WIDTH + (j + 1) * LANES] = (on * (hg * _sigmoid(hg))).astype(BF16)

    return _pc(body, name=name, grid=(s // t,),
               in_specs=[_rows(t, ATTN_WIDTH), _rows(t, HGRN_WIDTH), _rows(t, half, COL_HG // 4), _rows(t, half, COL_HG // 4 + 1),
                         _fixed((1, ATTN_WIDTH)), _fixed((1, HGRN_WIDTH))],
               out_specs=_rows(t, D_MODEL), out_shape=SDS((s, D_MODEL), BF16), sem=("parallel",))(attn, o, proj, proj, g_attn, g_hgrn)


def mix_out_bwd(name, dcat, attn, o, proj, g_attn, g_hgrn):
    s = attn.shape[0]
    t = _tile(s, ROW_TILE)
    half = HGRN_WIDTH // 2

    def body(dc_ref, a_ref, o_ref, hg0_ref, hg1_ref, ga_ref, gh_ref, da_ref, do_ref, dhg_ref, dga_ref, dgh_ref, pa_s, ph_s):
        av = a_ref[...]
        r = _rstd(av)
        xh = av * r
        dyv = dc_ref[:, :ATTN_WIDTH]
        dyg = dyv * ga_ref[...]
        da_ref[...] = r * (dyg - xh * jnp.mean(dyg * xh, axis=-1, keepdims=True))
        pa_s[...] = jnp.sum(dyv * xh, axis=0, keepdims=True)
        for j in range(HGRN_HEADS):
            cols = slice(j * LANES, (j + 1) * LANES)
            ov = o_ref[:, cols]
            hg_ref, hcols = (hg0_ref, cols) if j < 4 else (hg1_ref, slice((j - 4) * LANES, (j - 3) * LANES))
            hg = hg_ref[:, hcols]
            sg = _sigmoid(hg)
            r = _rstd(ov)
            xh = ov * r
            gain = gh_ref[:, cols]
            dh = dc_ref[:, ATTN_WIDTH + j * LANES:ATTN_WIDTH + (j + 1) * LANES]
            dhg_ref[:, cols] = ((dh * (xh * gain)) * (sg * (1.0 + hg * (1.0 - sg)))).astype(BF16)
            dyv = dh * (hg * sg)
            dyg = dyv * gain
            do_ref[:, cols] = r * (dyg - xh * jnp.mean(dyg * xh, axis=-1, keepdims=True))
            ph_s[:, cols] = jnp.sum(dyv * xh, axis=0, keepdims=True)

        @pl.when(pl.program_id(0) == 0)
        def _():
            dga_ref[...] = pa_s[...]
            dgh_ref[...] = ph_s[...]

        @pl.when(pl.program_id(0) > 0)
        def _():
            dga_ref[...] += pa_s[...]
            dgh_ref[...] += ph_s[...]

    return _pc(body, name=name, grid=(s // t,),
               in_specs=[_rows(t, D_MODEL), _rows(t, ATTN_WIDTH), _rows(t, HGRN_WIDTH), _rows(t, half, COL_HG // 4),
                         _rows(t, half, COL_HG // 4 + 1), _fixed((1, ATTN_WIDTH)), _fixed((1, HGRN_WIDTH))],
               out_specs=[_rows(t, ATTN_WIDTH), _rows(t, HGRN_WIDTH), _rows(t, HGRN_WIDTH), _fixed((1, ATTN_WIDTH)),
                          _fixed((1, HGRN_WIDTH))],
               out_shape=[SDS((s, ATTN_WIDTH), F32), SDS((s, HGRN_WIDTH), F32), SDS((s, HGRN_WIDTH), BF16),
                          SDS((1, ATTN_WIDTH), F32), SDS((1, HGRN_WIDTH), F32)],
               scratch=[pltpu.VMEM((1, ATTN_WIDTH), F32), pltpu.VMEM((1, HGRN_WIDTH), F32)],
               sem=("arbitrary",))(dcat, attn, o, proj, proj, g_attn, g_hgrn)


BIG = (("w_in", 2048, 1408, 0), ("w_out", 512, 2048, 1), ("w_ffn_gate", 2048, 1408, 0), ("w_ffn_up", 2048, 1408, 0),
       ("w_ffn_down", 1408, 2048, 1), ("w_ple_gate", 512, 2048, 1), ("w_ple_proj", 256, 512, 0))
BIG_BY_NAME = {spec[0]: spec for spec in BIG}
HBM_SPEC = pl.BlockSpec(memory_space=pltpu.HBM)
SEM_SPEC = pl.BlockSpec(memory_space=pltpu.SEMAPHORE)
TOKEN_SHAPE = (8, LANES)


def _split_call(body, *, name, in_specs, out_specs, out_shape, aliases):
    return pl.pallas_call(body, name=name, in_specs=in_specs, out_specs=out_specs, out_shape=out_shape,
                          input_output_aliases=aliases,
                          compiler_params=pltpu.CompilerParams(has_side_effects=pltpu.SideEffectType.DATAFLOW_SIDE_EFFECTING))


def _in_hbm(arrays):
    return [pltpu.with_memory_space_constraint(a, pltpu.HBM) for a in arrays]


def cast_to_slot(name, place, w, layer, after):
    _, r, c = w.shape
    tr = _tile(r, 1024)

    def body(place_ref, w_ref, after_ref, o_ref):
        o_ref[...] = w_ref[...].astype(BF16)

    gs = pltpu.PrefetchScalarGridSpec(
        num_scalar_prefetch=1, grid=(r // tr,),
        in_specs=[pl.BlockSpec((None, tr, c), lambda i, pr: (layer, i, 0)), pl.BlockSpec(memory_space=pl.ANY)],
        out_specs=pl.BlockSpec((None, tr, c), lambda i, pr: (pr[1], i, 0)))
    return _pc(body, name=name, grid_spec=gs, in_specs=None, out_specs=None, out_shape=SDS((N_CHIPS, r, c), BF16),
               sem=("parallel",))(place, w, after)


def _place():
    x, y, c = lax.axis_index("x"), lax.axis_index("y"), lax.axis_index("c")
    chips = [(1 - x, y), (x, 1 - y), (1 - x, 1 - y)]
    return x, y, c, chips


def _half(ref, axis, c, rows, cols):
    if axis == 0:
        return ref.at[pl.ds(pl.multiple_of(c * (rows // 2), 16), rows // 2), :]
    return ref.at[:, pl.ds(pl.multiple_of(c * (cols // 2), LANES), cols // 2)]


def _gather_copies(specs, bufs, send, recv):
    x, y, c, chips = _place()
    cps = []
    for t, (_, rows, cols, axis) in enumerate(specs):
        mine = _half(bufs[t].at[2 * x + y], axis, c, rows, cols)
        for k, (cx, cy) in enumerate(chips):
            cps.append(pltpu.make_async_remote_copy(src_ref=mine, dst_ref=mine, send_sem=send.at[3 * t + k],
                                                    recv_sem=recv.at[3 * t + k], device_id=(cx, cy, c), device_id_type=MESH))
    return cps


def gather_start(name, specs, bufs):
    nt = len(bufs)
    n = 3 * nt

    def body(*refs):
        send, recv, token = refs[nt], refs[nt + 1], refs[-1]
        for cp in _gather_copies(specs, refs[:nt], send, recv):
            cp.start()
        token[...] = jnp.zeros(TOKEN_SHAPE, F32)

    out = _split_call(
        body, name=name, in_specs=[HBM_SPEC] * nt,
        out_specs=(SEM_SPEC, SEM_SPEC) + (HBM_SPEC,) * nt + (pl.BlockSpec(memory_space=pltpu.VMEM),),
        out_shape=(pltpu.SemaphoreType.DMA((n,)), pltpu.SemaphoreType.DMA((n,)))
        + tuple(pltpu.HBM(b.shape, b.dtype) for b in bufs) + (SDS(TOKEN_SHAPE, F32),),
        aliases={t: 2 + t for t in range(nt)})(*_in_hbm(bufs))
    return out[0], out[1], list(out[2:2 + nt]), out[-1]


def gather_wait(name, specs, send, recv, bufs, after):
    nt = len(bufs)

    def body(*refs):
        for cp in _gather_copies(specs, refs[:nt], refs[nt], refs[nt + 1]):
            cp.wait_send()
            cp.wait_recv()

    out = _split_call(
        body, name=name, in_specs=[HBM_SPEC] * nt + [SEM_SPEC, SEM_SPEC, pl.BlockSpec(memory_space=pl.ANY)],
        out_specs=(HBM_SPEC,) * nt, out_shape=tuple(pltpu.HBM(b.shape, b.dtype) for b in bufs),
        aliases={t: t for t in range(nt)})(*bufs, send, recv, after)
    return list(out)


def gather_pass(name, specs, bufs):
    nt = len(bufs)

    def body(*refs):
        ins, outs = refs[:nt], refs[nt:2 * nt]
        send, recv = refs[2 * nt:]
        x, y, c, chips = _place()
        cps = []
        for t, (_, rows, cols, axis) in enumerate(specs):
            for k, (cx, cy) in enumerate(chips):
                cp = pltpu.make_async_remote_copy(
                    src_ref=_half(ins[t].at[2 * cx + cy], axis, c, rows, cols),
                    dst_ref=_half(outs[t].at[2 * cx + cy], axis, c, rows, cols),
                    send_sem=send.at[3 * t + k], recv_sem=recv.at[3 * t + k], device_id=(x, y, 1 - c), device_id_type=MESH)
                cp.start()
                cps.append(cp)
        for t, (_, rows, cols, axis) in enumerate(specs):
            for k, (cx, cy) in enumerate(chips):
                theirs = _half(outs[t].at[2 * cx + cy], axis, 1 - c, rows, cols)
                pltpu.make_async_remote_copy(src_ref=theirs, dst_ref=theirs, send_sem=send.at[3 * t + k],
                                             recv_sem=recv.at[3 * t + k], device_id=(x, y, 1 - c), device_id_type=MESH).wait_recv()
        for cp in cps:
            cp.wait_send()

    return _pc(body, name=name, in_specs=[HBM_SPEC] * nt, out_specs=[HBM_SPEC] * nt,
               out_shape=[SDS(b.shape, b.dtype) for b in bufs], scratch=[pltpu.SemaphoreType.DMA((3 * nt,))] * 2,
               input_output_aliases={t: t for t in range(nt)})(*bufs)


def reduce_to_sibling(name, grads):
    nt = len(grads)

    def body(*refs):
        srcs, dsts = refs[:nt], refs[nt:2 * nt]
        send, recv = refs[2 * nt:]
        x, y, c, _ = _place()
        cps = []
        for t in range(nt):
            cp = pltpu.make_async_remote_copy(src_ref=srcs[t].at[1 - c], dst_ref=dsts[t], send_sem=send.at[t],
                                              recv_sem=recv.at[t], device_id=(x, y, 1 - c), device_id_type=MESH)
            cp.start()
            cps.append(cp)
        for cp in cps:
            cp.wait()

    return _pc(body, name=name, in_specs=[HBM_SPEC] * nt, out_specs=[HBM_SPEC] * nt,
               out_shape=[SDS(g.shape[1:], g.dtype) for g in grads],
               scratch=[pltpu.SemaphoreType.DMA((nt,))] * 2)(*grads)


def _sibling_copies(grads, lands, send, recv):
    x, y, c, _ = _place()
    return [pltpu.make_async_remote_copy(src_ref=grads[t].at[1 - c], dst_ref=lands[t], send_sem=send.at[t], recv_sem=recv.at[t],
                                         device_id=(x, y, 1 - c), device_id_type=MESH) for t in range(len(grads))]


def sibling_start(name, grads):
    nt = len(grads)
    lands = [lax.empty(g.shape[1:], g.dtype) for g in grads]

    def body(*refs):
        send, recv, token = refs[2 * nt], refs[2 * nt + 1], refs[-1]
        for cp in _sibling_copies(refs[:nt], refs[nt:2 * nt], send, recv):
            cp.start()
        token[...] = jnp.zeros(TOKEN_SHAPE, F32)

    both = list(grads) + lands
    out = _split_call(
        body, name=name, in_specs=[HBM_SPEC] * (2 * nt),
        out_specs=(SEM_SPEC, SEM_SPEC) + (HBM_SPEC,) * (2 * nt) + (pl.BlockSpec(memory_space=pltpu.VMEM),),
        out_shape=(pltpu.SemaphoreType.DMA((nt,)), pltpu.SemaphoreType.DMA((nt,)))
        + tuple(pltpu.HBM(b.shape, b.dtype) for b in both) + (SDS(TOKEN_SHAPE, F32),),
        aliases={t: 2 + t for t in range(2 * nt)})(*_in_hbm(both))
    return out[0], out[1], list(out[2:2 + nt]), list(out[2 + nt:2 + 2 * nt]), out[-1]


def sibling_wait(name, send, recv, grads, lands, after):
    nt = len(grads)

    def body(*refs):
        for cp in _sibling_copies(refs[:nt], refs[nt:2 * nt], refs[2 * nt], refs[2 * nt + 1]):
            cp.wait_send()
            cp.wait_recv()

    both = list(grads) + list(lands)
    out = _split_call(
        body, name=name, in_specs=[HBM_SPEC] * (2 * nt) + [SEM_SPEC, SEM_SPEC, pl.BlockSpec(memory_space=pl.ANY)],
        out_specs=(HBM_SPEC,) * (2 * nt), out_shape=tuple(pltpu.HBM(b.shape, b.dtype) for b in both),
        aliases={t: t for t in range(2 * nt)})(*both, send, recv, after)
    return list(out[:nt]), list(out[nt:])


def add_halves(name, place, grad, got):
    _, n, r, c = grad.shape
    tr = _tile(r, 1024)

    def body(place_ref, g_ref, o_ref, part_ref, slot_ref):
        val = (g_ref[...].astype(F32) + o_ref[...].astype(F32)).astype(BF16)
        part_ref[...] = val

        @pl.when(pl.program_id(1) == place_ref[1])
        def _():
            slot_ref[...] = val

    gs = pltpu.PrefetchScalarGridSpec(
        num_scalar_prefetch=1, grid=(r // tr, n),
        in_specs=[pl.BlockSpec((None, None, tr, c), lambda i, j, pr: (pr[0], j, i, 0)),
                  pl.BlockSpec((None, tr, c), lambda i, j, pr: (j, i, 0))],
        out_specs=[pl.BlockSpec((None, tr, c), lambda i, j, pr: (j, i, 0)),
                   pl.BlockSpec((None, tr, c), lambda i, j, pr: (pr[1], i, 0))])
    return _pc(body, name=name, grid_spec=gs, in_specs=None, out_specs=None, out_shape=[SDS((n, r, c), BF16)] * 2,
               sem=("parallel", "arbitrary"))(place, grad, got)


def _chips_copies(parts, slots, send, recv):
    x, y, c, chips = _place()
    cps = []
    for t in range(len(parts)):
        for k, (cx, cy) in enumerate(chips):
            cps.append(pltpu.make_async_remote_copy(src_ref=parts[t].at[2 * cx + cy], dst_ref=slots[t].at[2 * x + y],
                                                    send_sem=send.at[3 * t + k], recv_sem=recv.at[3 * t + k],
                                                    device_id=(cx, cy, c), device_id_type=MESH))
    return cps


def chips_start(name, parts, slots):
    nt = len(parts)
    n = 3 * nt

    def body(*refs):
        send, recv, token = refs[2 * nt], refs[2 * nt + 1], refs[-1]
        for cp in _chips_copies(refs[:nt], refs[nt:2 * nt], send, recv):
            cp.start()
        token[...] = jnp.zeros(TOKEN_SHAPE, F32)

    both = list(parts) + list(slots)
    out = _split_call(
        body, name=name, in_specs=[HBM_SPEC] * (2 * nt),
        out_specs=(SEM_SPEC, SEM_SPEC) + (HBM_SPEC,) * (2 * nt) + (pl.BlockSpec(memory_space=pltpu.VMEM),),
        out_shape=(pltpu.SemaphoreType.DMA((n,)), pltpu.SemaphoreType.DMA((n,)))
        + tuple(pltpu.HBM(b.shape, b.dtype) for b in both) + (SDS(TOKEN_SHAPE, F32),),
        aliases={t: 2 + t for t in range(2 * nt)})(*_in_hbm(both))
    return out[0], out[1], list(out[2:2 + nt]), list(out[2 + nt:2 + 2 * nt]), out[-1]


def chips_wait(name, send, recv, parts, slots, after):
    nt = len(parts)

    def body(*refs):
        for cp in _chips_copies(refs[:nt], refs[nt:2 * nt], refs[2 * nt], refs[2 * nt + 1]):
            cp.wait_send()
            cp.wait_recv()

    both = list(parts) + list(slots)
    out = _split_call(
        body, name=name, in_specs=[HBM_SPEC] * (2 * nt) + [SEM_SPEC, SEM_SPEC, pl.BlockSpec(memory_space=pl.ANY)],
        out_specs=(HBM_SPEC,) * (2 * nt), out_shape=tuple(pltpu.HBM(b.shape, b.dtype) for b in both),
        aliases={t: t for t in range(2 * nt)})(*both, send, recv, after)
    return list(out[nt:])


def sum_chips(name, place, slots):
    n, r, c = slots.shape
    tr = _tile(r, 1024)

    def body(place_ref, s_ref, o_ref):
        acc = s_ref[0].astype(F32)
        for k in range(1, n):
            acc = acc + s_ref[k].astype(F32)
        o_ref[...] = acc

    gs = pltpu.PrefetchScalarGridSpec(
        num_scalar_prefetch=1, grid=(r // tr,),
        in_specs=[pl.BlockSpec((n, tr, c), lambda i, pr: (0, i, 0))],
        out_specs=pl.BlockSpec((None, tr, c), lambda i, pr: (pr[0], i, 0)))
    return _pc(body, name=name, grid_spec=gs, in_specs=None, out_specs=None, out_shape=SDS((2, r, c), F32),
               sem=("parallel",))(place, slots)


def share_with_sibling(name, bufs):
    nt = len(bufs)

    def body(*refs):
        ins, outs = refs[:nt], refs[nt:2 * nt]
        send, recv = refs[2 * nt:]
        x, y, c, _ = _place()
        cps = []
        for t in range(nt):
            cp = pltpu.make_async_remote_copy(src_ref=ins[t].at[c], dst_ref=outs[t].at[c], send_sem=send.at[t], recv_sem=recv.at[t],
                                              device_id=(x, y, 1 - c), device_id_type=MESH)
            cp.start()
            cps.append(cp)
        for t in range(nt):
            theirs = outs[t].at[1 - c]
            pltpu.make_async_remote_copy(src_ref=theirs, dst_ref=theirs, send_sem=send.at[t], recv_sem=recv.at[t],
                                         device_id=(x, y, 1 - c), device_id_type=MESH).wait_recv()
        for cp in cps:
            cp.wait_send()

    return _pc(body, name=name, in_specs=[HBM_SPEC] * nt, out_specs=[HBM_SPEC] * nt,
               out_shape=[SDS(b.shape, F32) for b in bufs], scratch=[pltpu.SemaphoreType.DMA((nt,))] * 2,
               input_output_aliases={t: t for t in range(nt)})(*bufs)


def _adamw(w, g, m, v):
    m = ADAM_B1 * m + (1.0 - ADAM_B1) * g
    v = ADAM_B2 * v + (1.0 - ADAM_B2) * (g * g)
    m_hat = m / (1.0 - ADAM_B1 ** ADAM_STEP)
    v_hat = v / (1.0 - ADAM_B2 ** ADAM_STEP)
    delta = -ADAM_LR * (m_hat / (jnp.sqrt(v_hat) + ADAM_EPS) + ADAM_WD * w)
    return delta, m, v


def adamw_big(name, w, m, v, g0, g1, axis):
    _, r, c = w.shape
    _, rh, ch = g0.shape
    tr = _tile(rh, 256)
    nb = rh // tr
    if axis == 0:
        wspec = pl.BlockSpec((None, tr, ch), lambda l, h, i: (l, h * nb + i, 0))
    else:
        wspec = pl.BlockSpec((None, tr, ch), lambda l, h, i: (l, i, h))
    g0spec = pl.BlockSpec((None, tr, ch), lambda l, h, i: (h * (1 - l), i * (1 - l), 0))
    g1spec = pl.BlockSpec((None, tr, ch), lambda l, h, i: (h * l, i * l, 0))

    def body(w_ref, m_ref, v_ref, g0_ref, g1_ref, go_ref, d_ref, mo_ref, vo_ref):
        def run(g_ref):
            g = g_ref[...]
            delta, mn, vn = _adamw(w_ref[...], g, m_ref[...], v_ref[...])
            go_ref[...] = g
            d_ref[...] = delta
            mo_ref[...] = mn
            vo_ref[...] = vn

        @pl.when(pl.program_id(0) == 0)
        def _():
            run(g0_ref)

        @pl.when(pl.program_id(0) == 1)
        def _():
            run(g1_ref)

    return _pc(body, name=name, grid=(2, 2, nb), in_specs=[wspec, wspec, wspec, g0spec, g1spec], out_specs=[wspec] * 4,
               out_shape=[SDS(w.shape, F32)] * 4, sem=("parallel", "parallel", "parallel"))(w, m, v, g0, g1)


SMALL = (("pre_mix_gain", 2048), ("post_mix_gain", 2048), ("pre_ffn_gain", 2048), ("post_ffn_gain", 2048), ("ple_gain", 2048),
         ("attn_out_gain", 1024), ("hgrn_out_gain", 1024), ("hgrn_lb_logits", 1024), ("attn_sinks", 128))
SMALL_ROWS = sum(2 * w // LANES for _, w in SMALL)
SMALL_PAD = -(-SMALL_ROWS // 8) * 8
LB_ROW = sum(2 * w // LANES for _, w in SMALL[:7])


def _pack_small(parts):
    rows = []
    for nm, w in SMALL:
        a = parts[nm].astype(F32)
        if a.shape[1] != w:
            a = jnp.pad(a, ((0, 0), (0, w - a.shape[1])))
        rows.append(a.reshape(2 * w // LANES, LANES))
    rows.append(jnp.zeros((SMALL_PAD - SMALL_ROWS, LANES), F32))
    return jnp.concatenate(rows, axis=0)


def _unpack_small(packed, widths):
    out, r = {}, 0
    for nm, w in SMALL:
        n = 2 * w // LANES
        out[nm] = packed[r:r + n].reshape(2, w)[:, :widths[nm]]
        r += n
    return out


def allreduce_small(name, packed):
    rows = packed.shape[0]

    def body(x_ref, o_ref, buf, send, recv, own_sem):
        x, y, c, _ = _place()
        me = 4 * x + 2 * y + c
        own = pltpu.make_async_copy(x_ref, buf.at[me], own_sem)
        own.start()
        cps = []
        for k in range(1, 8):
            px, py, pc = x ^ (k >> 2), y ^ ((k >> 1) & 1), c ^ (k & 1)
            cp = pltpu.make_async_remote_copy(src_ref=x_ref, dst_ref=buf.at[me], send_sem=send.at[k - 1], recv_sem=recv.at[k - 1],
                                              device_id=(px, py, pc), device_id_type=MESH)
            cp.start()
            cps.append(cp)
        for k in range(1, 8):
            px, py, pc = x ^ (k >> 2), y ^ ((k >> 1) & 1), c ^ (k & 1)
            slot = buf.at[4 * px + 2 * py + pc]
            pltpu.make_async_remote_copy(src_ref=slot, dst_ref=slot, send_sem=send.at[k - 1], recv_sem=recv.at[k - 1],
                                         device_id=(px, py, pc), device_id_type=MESH).wait_recv()
        for cp in cps:
            cp.wait_send()
        own.wait()
        acc = buf[0]
        for k in range(1, 8):
            acc = acc + buf[k]
        o_ref[...] = acc

    vm = pl.BlockSpec(memory_space=pltpu.VMEM)
    return _pc(body, name=name, in_specs=[vm], out_specs=vm, out_shape=SDS((rows, LANES), F32),
               scratch=[pltpu.VMEM((8, rows, LANES), F32), pltpu.SemaphoreType.DMA((7,)), pltpu.SemaphoreType.DMA((7,)),
                        pltpu.SemaphoreType.DMA])(packed)


def adamw_small(name, w, m, v, g):
    rows = w.shape[0]
    n = HGRN_WIDTH // LANES

    def body(w_ref, m_ref, v_ref, g_ref, go_ref, d_ref, mo_ref, vo_ref):
        go_ref[...] = g_ref[...]
        l0 = w_ref[LB_ROW:LB_ROW + n, :]
        l1 = w_ref[LB_ROW + n:LB_ROW + 2 * n, :]
        mx = jnp.maximum(l0, l1)
        e0, e1 = jnp.exp(l0 - mx), jnp.exp(l1 - mx)
        s0, s1 = e0 / (e0 + e1), e1 / (e0 + e1)
        dlb1 = g_ref[LB_ROW + n:LB_ROW + 2 * n, :]
        inner = s1 * dlb1
        go_ref[LB_ROW:LB_ROW + n, :] = s0 * (0.0 - inner)
        go_ref[LB_ROW + n:LB_ROW + 2 * n, :] = s1 * (dlb1 - inner)
        delta, mn, vn = _adamw(w_ref[...], go_ref[...], m_ref[...], v_ref[...])
        d_ref[...] = delta
        mo_ref[...] = mn
        vo_ref[...] = vn

    vm = pl.BlockSpec(memory_space=pltpu.VMEM)
    return _pc(body, name=name, in_specs=[vm] * 4, out_specs=[vm] * 4, out_shape=[SDS((rows, LANES), F32)] * 4)(w, m, v, g)


def _layer_fwd(l, x, h1, p_l, w_in_g, rest_of_weights, gains, cos, sin, sinks, lb_logits, g_next, target):
    n = f"l{l}_"
    proj = mm_col(n + "in_proj", h1, w_in_g)
    qp, kp, vp = rope_qkv(n + "rope_qkv", proj, cos, sin)
    attn = attn_fwd(n + "attn_fwd", qp, kp, vp, sinks)
    o, states = hgrn_fwd(n + "hgrn_fwd", proj, lb_logits, l)
    cat = mix_out_fwd(n + "mix_out_fwd", attn, o, proj, gains["attn_out_gain"], gains["hgrn_out_gain"])
    rest, token = rest_of_weights(cat)
    wts = dict(rest, w_in=w_in_g)
    if token is not None:
        gains = _with_token(gains, "post_mix_gain", token)
    m, x1, h2 = out_proj_post_mix(n + "out_proj_post_mix", cat, wts["w_out"], gains["post_mix_gain"], x, gains["pre_ffn_gain"])
    g, u, a = ffn_gate_up(n + "ffn_gate_up", h2, wts["w_ffn_gate"], wts["w_ffn_up"])
    f = mm_row(n + "ffn_down", a, wts["w_ffn_down"])
    x2, h3 = post_pre_norm(n + "post_ffn", f, gains["post_ffn_gain"], x1, gains["ple_gain"])
    pp = mm_col(n + "ple_proj", p_l, wts["w_ple_proj"])
    saved = dict(x=x, h1=h1, proj=proj, qp=qp, kp=kp, vp=vp, attn=attn, o=o, states=states, cat=cat, m=m, x1=x1, h2=h2,
                 g=g, u=u, a=a, f=f, x2=x2, h3=h3, p=p_l)
    if target is None:
        z, *out = ple_gate_fwd_mid(n + "ple_gate_fwd", h3, wts["w_ple_gate"], pp, x2, g_next)
        saved.update(z=z, pp=pp)
    else:
        dy, dpp, dz, loss = ple_gate_fwd_loss(n + "ple_gate_loss", h3, wts["w_ple_gate"], pp, x2, target)
        out = [dy, loss]
        saved.update(dpp=dpp, dz=dz)
    return out, saved, wts


EARLY = ("w_ple_gate", "w_ple_proj", "w_ffn_down", "w_ffn_gate", "w_ffn_up")
LATE = ("w_out", "w_in")


def _layer_bwd_ffn(l, dx3, sv, wts, gains, after=None, hook=None):
    n = f"l{l}_"
    dpp, dz = (sv["dpp"], sv["dz"]) if "dz" in sv else ple_bwd(n + "ple_bwd", dx3, sv["z"], sv["pp"],
                                                                 dx3 if after is None else after)
    dh3 = mm_row_t(n + "ple_gate_dx", dz, wts["w_ple_gate"])
    if hook is not None:
        gains = _with_token(gains, "ple_gain", hook(dh3))
    dx2, df, d_ple_gain, d_post_ffn = norm_bwd_pair(n + "ple_post_ffn_bwd", sv["x2"], gains["ple_gain"], dh3, dx3, sv["f"],
                                                    gains["post_ffn_gain"])
    dg, du = ffn_down_bwd(n + "ffn_down_bwd", df, wts["w_ffn_down"], sv["g"], sv["u"])
    big = dict(
        w_ple_gate=mm_wg_row(n + "ple_gate_dw", sv["h3"], dz),
        w_ple_proj=mm_wg_col(n + "ple_proj_dw", sv["p"], dpp),
        w_ffn_down=mm_wg_row(n + "ffn_down_dw", sv["a"], df),
        w_ffn_gate=mm_wg_col(n + "ffn_gate_dw", sv["h2"], dg),
        w_ffn_up=mm_wg_col(n + "ffn_up_dw", sv["h2"], du),
    )
    return dict(dx2=dx2, dg=dg, du=du), big, dict(ple_gain=d_ple_gain, post_ffn_gain=d_post_ffn)


def _layer_bwd_mix(l, st, sv, wts, gains, cos, sin, sinks, lb_logits, after=None, hook=None):
    n = f"l{l}_"
    dh2 = mm_col_t(n + "ffn_gate_dx", st["dg"], wts["w_ffn_gate"], after=after)
    dh2 = mm_col_t(n + "ffn_up_dx", st["du"], wts["w_ffn_up"], add=dh2)
    if hook is not None:
        gains = _with_token(gains, "pre_ffn_gain", hook(dh2))
    dx1, dm, d_pre_ffn, d_post_mix = norm_bwd_pair(n + "pre_ffn_post_mix_bwd", sv["x1"], gains["pre_ffn_gain"], dh2, st["dx2"],
                                                   sv["m"], gains["post_mix_gain"])
    dcat = mm_row_t(n + "out_proj_dx", dm, wts["w_out"])
    dattn, do, dhg, d_attn_gain, d_hgrn_gain = mix_out_bwd(n + "mix_out_bwd", dcat, sv["attn"], sv["o"], sv["proj"],
                                                            gains["attn_out_gain"], gains["hgrn_out_gain"])
    dqp, dkc, dkp, dvc, dvp, dsinks = attn_bwd(n + "attn_bwd", sv["qp"], sv["kp"], sv["vp"], sinks, dattn)
    dqkv = rope_bwd(n + "rope_bwd", dqp, dkc, dkp, dvc, dvp, cos, sin)
    dhq, dhf, dhi, dlb = hgrn_bwd(n + "hgrn_bwd", sv["proj"], lb_logits, l, sv["states"], do)
    dproj = jnp.concatenate([dqkv, dhq, dhf, dhi, dhg], axis=1)
    dh1 = mm_col_t(n + "in_proj_dx", dproj, wts["w_in"])
    dx, d_pre_mix = norm_bwd(n + "pre_mix_bwd", sv["x"], gains["pre_mix_gain"], dh1, dx1)
    big = dict(w_out=mm_wg_row(n + "out_proj_dw", sv["cat"], dm), w_in=mm_wg_col(n + "in_proj_dw", sv["h1"], dproj))
    small = dict(pre_mix_gain=d_pre_mix, post_mix_gain=d_post_mix, pre_ffn_gain=d_pre_ffn, attn_out_gain=d_attn_gain,
                 hgrn_out_gain=d_hgrn_gain, hgrn_lb_logits=dlb, attn_sinks=dsinks)
    return dx, big, small


def _layer_bwd(l, dx3, sv, wts, gains, cos, sin, sinks, lb_logits):
    st, early, small_a = _layer_bwd_ffn(l, dx3, sv, wts, gains)
    dx, late, small_b = _layer_bwd_mix(l, st, sv, wts, gains, cos, sin, sinks, lb_logits)
    return dx, {**early, **late}, {**small_a, **small_b}


def _reduce_start(tag, names, big, place):
    got = reduce_to_sibling(tag + "_reduce_to_sibling", [big[nm] for nm in names])
    pairs = [add_halves(f"{tag}_add_{nm}", place, big[nm], gt) for nm, gt in zip(names, got)]
    return chips_start(tag + "_chips_start", [pr[0] for pr in pairs], [pr[1] for pr in pairs])


def _reduce_begin(tag, names, big):
    return sibling_start(tag + "_sibling_start", [big[nm] for nm in names])


def _reduce_chips(tag, names, begun, place, after):
    send, recv, grads, lands, _ = begun
    grads, got = sibling_wait(tag + "_sibling_wait", send, recv, grads, lands, after)
    pairs = [add_halves(f"{tag}_add_{nm}", place, g, gt) for nm, g, gt in zip(names, grads, got)]
    return chips_start(tag + "_chips_start", [pr[0] for pr in pairs], [pr[1] for pr in pairs])


def _reduce_finish(tag, names, started, place, after):
    send, recv, parts, slots, _ = started
    slots = chips_wait(tag + "_chips_wait", send, recv, parts, slots, after)
    bufs = [sum_chips(f"{tag}_sum_{nm}", place, sl) for nm, sl in zip(names, slots)]
    return dict(zip(names, share_with_sibling(tag + "_share_with_sibling", bufs)))


def _with_token(gains, name, token):
    out = dict(gains)
    out[name] = gains[name] + token[0, 0]
    return out


def kernel(x, p, positions, w_in, attn_sinks, hgrn_lb_logits, attn_out_gain, hgrn_out_gain, w_out, pre_mix_gain, post_mix_gain, pre_ffn_gain, post_ffn_gain, w_ffn_gate, w_ffn_up, w_ffn_down, ple_gain, w_ple_gate, w_ple_proj, loss_target, m_w_in, m_attn_sinks, m_hgrn_lb_logits, m_attn_out_gain, m_hgrn_out_gain, m_w_out, m_pre_mix_gain, m_post_mix_gain, m_pre_ffn_gain, m_post_ffn_gain, m_w_ffn_gate, m_w_ffn_up, m_w_ffn_down, m_ple_gain, m_w_ple_gate, m_w_ple_proj, v_w_in, v_attn_sinks, v_hgrn_lb_logits, v_attn_out_gain, v_hgrn_out_gain, v_w_out, v_pre_mix_gain, v_post_mix_gain, v_pre_ffn_gain, v_post_ffn_gain, v_w_ffn_gate, v_w_ffn_up, v_w_ffn_down, v_ple_gain, v_w_ple_gate, v_w_ple_proj):
    given = dict(locals())
    depth = 2
    place = jnp.stack([lax.axis_index("c"), 2 * lax.axis_index("x") + lax.axis_index("y")]).astype(jnp.int32)
    xs = x[0]
    tgt = loss_target[0]
    pos_col = positions.reshape(-1, 1)
    half = 32
    inv_freq = ROPE_THETA ** (-jnp.arange(half, dtype=F32) / half)
    inv_freq = jnp.tile(inv_freq, 4).reshape(1, LANES)
    gains = [{nm: given[nm][l:l + 1] for nm, _ in SMALL[:7]} for l in range(depth)]
    names = [nm for nm, *_ in BIG]
    first, others = names[:1], names[1:]

    def specs(nms):
        return [BIG_BY_NAME[nm] for nm in nms]

    def start_gather(tag, l, nms, after):
        return gather_start(tag + "_gather_start", specs(nms),
                            [cast_to_slot(f"{tag}_cast_{nm}", place, given[nm], l, after) for nm in nms])

    def finish_gather(tag, nms, started, after):
        bufs = gather_wait(tag + "_gather_wait", specs(nms), started[0], started[1], started[2], after)
        return dict(zip(nms, gather_pass(tag + "_gather_pass", specs(nms), bufs)))

    g0a = start_gather("l0a", 0, first, place)
    g0b = start_gather("l0b", 0, others, g0a[3])
    started = {}

    def rest_of_layer0(after):
        got = finish_gather("l0b", others, g0b, after)
        started["l1"] = start_gather("l1", 1, names, got["w_out"])
        return got, started["l1"][3]

    cos, sin = rope_tables("rope_tables", pos_col, inv_freq)
    h1 = pre_norm("l0_pre_mix", xs, gains[0]["pre_mix_gain"])
    w_in0 = finish_gather("l0a", first, g0a, g0b[3])["w_in"]
    (x_mid, h1_next), sv0, wts0 = _layer_fwd(0, xs, h1, p[0, 0], w_in0, rest_of_layer0, gains[0], cos, sin, attn_sinks[0],
                                             hgrn_lb_logits, gains[1]["pre_mix_gain"], None)
    wts1 = finish_gather("l1", names, started["l1"], x_mid)
    (dy, loss_part), sv1, _ = _layer_fwd(1, x_mid, h1_next, p[1, 0], wts1["w_in"], lambda after: (wts1, None), gains[1], cos, sin,
                                         attn_sinks[1], hgrn_lb_logits, None, tgt)

    dx_mid, big1, small1 = _layer_bwd(1, dy, sv1, wts1, gains[1], cos, sin, attn_sinks[1], hgrn_lb_logits)
    chips = {}

    def chips_after(tag, nms, begun):
        def hook(x):
            chips[tag] = _reduce_chips(tag, nms, begun, place, x)
            return chips[tag][4]
        return hook

    b1 = _reduce_begin("l1", names, big1)
    st0, early0, small0 = _layer_bwd_ffn(0, dx_mid, sv0, wts0, gains[0], after=b1[4], hook=chips_after("l1", names, b1))
    b0e = _reduce_begin("l0e", EARLY, early0)
    dx0, late0, small0b = _layer_bwd_mix(0, st0, sv0, wts0, gains[0], cos, sin, attn_sinks[0], hgrn_lb_logits, after=b0e[4],
                                         hook=chips_after("l0e", EARLY, b0e))
    r1, r0e = chips["l1"], chips["l0e"]
    small0 = {**small0, **small0b}
    r0l = _reduce_start("l0l", LATE, late0, place)
    red1 = _reduce_finish("l1", names, r1, place, r0l[4])
    red0 = _reduce_finish("l0e", EARLY, r0e, place, red1[names[-1]])

    loss = lax.psum(loss_part[0, 0], ("x", "y", "c"))
    grad_x = dx0[None]

    out_big = {}
    for nm in EARLY:
        out_big[nm] = adamw_big("adamw_" + nm, given[nm], given["m_" + nm], given["v_" + nm], red0[nm], red1[nm], BIG_BY_NAME[nm][3])
    red0.update(_reduce_finish("l0l", LATE, r0l, place, out_big[EARLY[-1]][3]))
    for nm in LATE:
        out_big[nm] = adamw_big("adamw_" + nm, given[nm], given["m_" + nm], given["v_" + nm], red0[nm], red1[nm], BIG_BY_NAME[nm][3])

    widths = {nm: given[nm].shape[1] for nm, _ in SMALL}
    small_g = {nm: jnp.concatenate([small0[nm][:, :widths[nm]] if nm != "attn_sinks" else small0[nm][:, :LANES],
                                    small1[nm][:, :widths[nm]] if nm != "attn_sinks" else small1[nm][:, :LANES]], axis=0)
               for nm, _ in SMALL}
    g_sum = allreduce_small("allreduce_small", _pack_small(small_g))
    sm = adamw_small("adamw_small", _pack_small({nm: given[nm] for nm, _ in SMALL}),
                     _pack_small({nm: given["m_" + nm] for nm, _ in SMALL}),
                     _pack_small({nm: given["v_" + nm] for nm, _ in SMALL}), g_sum)
    out_small = [_unpack_small(a, widths) for a in sm]

    order = ["w_in", "attn_sinks", "hgrn_lb_logits", "attn_out_gain", "hgrn_out_gain", "w_out", "pre_mix_gain", "post_mix_gain",
             "pre_ffn_gain", "post_ffn_gain", "w_ffn_gate", "w_ffn_up", "w_ffn_down", "ple_gain", "w_ple_gate", "w_ple_proj"]
    res = [loss, grad_x]
    for k in range(4):
        for nm in order:
            res.append(out_big[nm][k] if nm in out_big else out_small[k][nm])
    return tuple(res)
```

```python
import functools

import jax
import jax.numpy as jnp
from jax import lax
from jax.experimental import pallas as pl
from jax.experimental.pallas import tpu as pltpu

F32, BF16 = jnp.float32, jnp.bfloat16
SDS = jax.ShapeDtypeStruct
MESH = pl.DeviceIdType.MESH

D_MODEL = 2048
ATTN_WIDTH = 1024
HGRN_WIDTH = 1024
KV_WIDTH = 256
N_Q_HEADS = 16
N_KV_HEADS = 4
Q_PER_KV = 4
WINDOW = 128
MASK_VALUE = -1e30
ROPE_THETA = 10000.0
HGRN_HEADS = 8
HGRN_CHUNK = 16
D_FF = 5632
D_PLE = 256
RMS_EPS = 1e-6
LANES = 128
N_CHIPS = 4
COL_Q, COL_K, COL_V, COL_HQ, COL_HF, COL_HI, COL_HG = 0, 8, 10, 12, 20, 28, 36

ADAM_LR, ADAM_B1, ADAM_B2, ADAM_EPS, ADAM_WD, ADAM_STEP = 0.001, 0.9, 0.999, 1e-08, 0.01, 10

VMEM_LIMIT = 56 * 1024 * 1024
ROW_TILE = 256

_NN = (((1,), (0,)), ((), ()))
_NT = (((1,), (1,)), ((), ()))
_TN = (((0,), (0,)), ((), ()))


def _pc(body, *, name, out_shape, in_specs, out_specs, grid=(), scratch=(), sem=None, grid_spec=None, **kw):
    params = dict(vmem_limit_bytes=VMEM_LIMIT)
    if sem is not None:
        params["dimension_semantics"] = sem
    if grid_spec is not None:
        return pl.pallas_call(body, name=name, out_shape=out_shape, grid_spec=grid_spec,
                              compiler_params=pltpu.CompilerParams(**params), **kw)
    return pl.pallas_call(body, name=name, out_shape=out_shape, grid=grid, in_specs=in_specs, out_specs=out_specs,
                          scratch_shapes=list(scratch), compiler_params=pltpu.CompilerParams(**params), **kw)


def _sigmoid(x):
    return 1.0 / (1.0 + jnp.exp(-x))


def _rstd(x):
    return lax.rsqrt(jnp.mean(x * x, axis=-1, keepdims=True) + RMS_EPS)


def _rows(t, w, col=0):
    return pl.BlockSpec((t, w), lambda i, col=col: (i, col))


def _fixed(shape):
    return pl.BlockSpec(shape, lambda *_: (0,) * len(shape))


def _mm(name, a, b, *, dims, grid, a_spec, b_spec, o_spec, out_shape, parts=1, add=None, add_spec=None, after=None):
    def body(*refs):
        a_ref, b_ref, o_ref = refs[0], refs[1], refs[-1]
        if parts == 1:
            r = lax.dot_general(a_ref[...].astype(BF16), b_ref[...].astype(BF16), dims, preferred_element_type=F32)
        else:
            w = a_ref.shape[1] // parts
            r = None
            for j in range(parts):
                t = lax.dot_general(a_ref[:, j * w:(j + 1) * w].astype(BF16), b_ref[j].astype(BF16), dims,
                                    preferred_element_type=F32)
                r = t if r is None else r + t
        if add is not None:
            r = r + refs[2][...]
        o_ref[...] = r.astype(o_ref.dtype)

    ins = [a, b] + ([] if add is None else [add]) + ([] if after is None else [after])
    specs = [a_spec, b_spec] + ([] if add is None else [add_spec]) + ([] if after is None else [pl.BlockSpec(memory_space=pl.ANY)])
    return _pc(body, name=name, grid=grid, in_specs=specs, out_specs=o_spec, out_shape=out_shape,
               sem=("parallel",) * len(grid))(*ins)


def _tile(n, t):
    if n <= t:
        return n
    while n % t:
        t //= 2
    assert t % 8 == 0
    return t


def mm_col(name, a, wg, out_dtype=F32):
    s, k = a.shape
    _, _, n = wg.shape
    tm = _tile(s, 512)
    return _mm(name, a, wg, dims=_NN, grid=(N_CHIPS, s // tm),
               a_spec=pl.BlockSpec((tm, k), lambda j, i: (i, 0)),
               b_spec=pl.BlockSpec((None, k, n), lambda j, i: (j, 0, 0)),
               o_spec=pl.BlockSpec((tm, n), lambda j, i: (i, j)),
               out_shape=SDS((s, N_CHIPS * n), out_dtype))


def mm_row(name, a, wg, out_dtype=F32):
    s, _ = a.shape
    _, r, n = wg.shape
    tm = _tile(s, 512)
    tn = _tile(n, 1024 if r > 512 else 2048)
    return _mm(name, a, wg, dims=_NN, grid=(n // tn, s // tm), parts=N_CHIPS,
               a_spec=pl.BlockSpec((tm, N_CHIPS * r), lambda j, i: (i, 0)),
               b_spec=pl.BlockSpec((N_CHIPS, r, tn), lambda j, i: (0, 0, j)),
               o_spec=pl.BlockSpec((tm, tn), lambda j, i: (i, j)),
               out_shape=SDS((s, n), out_dtype))


def mm_col_t(name, dy, wg, add=None, out_dtype=F32, after=None):
    s, _ = dy.shape
    _, k, n = wg.shape
    tm = _tile(s, 512)
    tk = _tile(k, 1024)
    return _mm(name, dy, wg, dims=_NT, grid=(k // tk, s // tm), parts=N_CHIPS,
               a_spec=pl.BlockSpec((tm, N_CHIPS * n), lambda j, i: (i, 0)),
               b_spec=pl.BlockSpec((N_CHIPS, tk, n), lambda j, i: (0, j, 0)),
               o_spec=pl.BlockSpec((tm, tk), lambda j, i: (i, j)),
               add=add, add_spec=pl.BlockSpec((tm, tk), lambda j, i: (i, j)),
               out_shape=SDS((s, k), out_dtype), after=after)


def mm_row_t(name, dy, wg, out_dtype=F32):
    s, n = dy.shape
    _, r, _ = wg.shape
    tm = _tile(s, 512)
    return _mm(name, dy, wg, dims=_NT, grid=(N_CHIPS, s // tm),
               a_spec=pl.BlockSpec((tm, n), lambda j, i: (i, 0)),
               b_spec=pl.BlockSpec((None, r, n), lambda j, i: (j, 0, 0)),
               o_spec=pl.BlockSpec((tm, r), lambda j, i: (i, j)),
               out_shape=SDS((s, N_CHIPS * r), out_dtype))


def norm_bwd_pair(name, x_a, gain_a, dy, dres, x_b, gain_b):
    s, d = x_a.shape
    t = _tile(s, ROW_TILE)

    def one(xv, g, dyv):
        r = _rstd(xv)
        xh = xv * r
        dyg = dyv * g
        return r * (dyg - xh * jnp.mean(dyg * xh, axis=-1, keepdims=True)), jnp.sum(dyv * xh, axis=0, keepdims=True)

    def body(xa_ref, ga_ref, dy_ref, r_ref, xb_ref, gb_ref, dx_ref, db_ref, dga_ref, dgb_ref):
        dx, pa = one(xa_ref[...], ga_ref[...], dy_ref[...])
        dx = dx + r_ref[...]
        dx_ref[...] = dx
        db, pb = one(xb_ref[...], gb_ref[...], dx)
        db_ref[...] = db.astype(BF16)

        @pl.when(pl.program_id(0) == 0)
        def _():
            dga_ref[...] = pa
            dgb_ref[...] = pb

        @pl.when(pl.program_id(0) > 0)
        def _():
            dga_ref[...] += pa
            dgb_ref[...] += pb

    row, gain = _rows(t, d), _fixed((1, d))
    return _pc(body, name=name, grid=(s // t,), in_specs=[row, gain, row, row, row, gain], out_specs=[row, row, gain, gain],
               out_shape=[SDS((s, d), F32), SDS((s, d), BF16), SDS((1, d), F32), SDS((1, d), F32)],
               sem=("arbitrary",))(x_a, gain_a, dy, dres, x_b, gain_b)


def ffn_gate_up(name, h, wg_gate, wg_up):
    s, k = h.shape
    _, _, n = wg_gate.shape
    tm = _tile(s, 512)

    def body(h_ref, wg_ref, wu_ref, g_ref, u_ref, a_ref):
        hv = h_ref[...]
        g = jnp.dot(hv, wg_ref[...], preferred_element_type=F32)
        u = jnp.dot(hv, wu_ref[...], preferred_element_type=F32)
        g_ref[...] = g
        u_ref[...] = u
        a_ref[...] = ((g * _sigmoid(g)) * u).astype(BF16)

    wspec = pl.BlockSpec((None, k, n), lambda j, i: (j, 0, 0))
    ospec = pl.BlockSpec((tm, n), lambda j, i: (i, j))
    return _pc(body, name=name, grid=(N_CHIPS, s // tm), in_specs=[pl.BlockSpec((tm, k), lambda j, i: (i, 0)), wspec, wspec],
               out_specs=[ospec] * 3, out_shape=[SDS((s, N_CHIPS * n), F32)] * 2 + [SDS((s, N_CHIPS * n), BF16)],
               sem=("parallel", "parallel"))(h, wg_gate, wg_up)


def ffn_down_bwd(name, df, wg_down, g, u):
    s, n = df.shape
    _, r, _ = wg_down.shape
    tm = _tile(s, 512)

    def body(df_ref, w_ref, g_ref, u_ref, dg_ref, du_ref):
        da = lax.dot_general(df_ref[...], w_ref[...], _NT, preferred_element_type=F32)
        gv = g_ref[...]
        sg = _sigmoid(gv)
        du_ref[...] = (da * (gv * sg)).astype(BF16)
        dg_ref[...] = ((da * u_ref[...]) * (sg * (1.0 + gv * (1.0 - sg)))).astype(BF16)

    cspec = pl.BlockSpec((tm, r), lambda j, i: (i, j))
    return _pc(body, name=name, grid=(N_CHIPS, s // tm),
               in_specs=[pl.BlockSpec((tm, n), lambda j, i: (i, 0)), pl.BlockSpec((None, r, n), lambda j, i: (j, 0, 0)), cspec, cspec],
               out_specs=[cspec] * 2, out_shape=[SDS((s, N_CHIPS * r), BF16)] * 2,
               sem=("parallel", "parallel"))(df, wg_down, g, u)


def mm_wg_col(name, a, dy):
    s, k = a.shape
    n = dy.shape[1] // N_CHIPS
    tm = _tile(k // 2, 512)
    hb = (k // 2) // tm
    return _mm(name, a, dy, dims=_TN, grid=(N_CHIPS, k // tm),
               a_spec=pl.BlockSpec((s, tm), lambda j, i: (0, i)),
               b_spec=pl.BlockSpec((s, n), lambda j, i: (0, j)),
               o_spec=pl.BlockSpec((None, None, tm, n), lambda j, i: (i // hb, j, i % hb, 0)),
               out_shape=SDS((2, N_CHIPS, k // 2, n), BF16))


def mm_wg_row(name, a, dy):
    s, n = dy.shape
    r = a.shape[1] // N_CHIPS
    tn = _tile(n // 2, 512)
    nb = (n // 2) // tn
    return _mm(name, a, dy, dims=_TN, grid=(N_CHIPS, n // tn),
               a_spec=pl.BlockSpec((s, r), lambda j, i: (0, j)),
               b_spec=pl.BlockSpec((s, tn), lambda j, i: (0, i)),
               o_spec=pl.BlockSpec((None, None, r, tn), lambda j, i: (i // nb, j, 0, i % nb)),
               out_shape=SDS((2, N_CHIPS, r, n // 2), BF16))


def pre_norm(name, x, gain):
    s, d = x.shape
    t = _tile(s, ROW_TILE)

    def body(x_ref, g_ref, o_ref):
        xv = x_ref[...]
        o_ref[...] = ((xv * _rstd(xv)) * g_ref[...]).astype(BF16)

    return _pc(body, name=name, grid=(s // t,), in_specs=[_rows(t, d), _fixed((1, d))], out_specs=_rows(t, d),
               out_shape=SDS((s, d), BF16), sem=("parallel",))(x, gain)


def post_pre_norm(name, m, g_post, res, g_pre):
    s, d = m.shape
    t = _tile(s, ROW_TILE)

    def body(m_ref, gp_ref, r_ref, gn_ref, x_ref, h_ref):
        mv = m_ref[...]
        xn = r_ref[...] + (mv * _rstd(mv)) * gp_ref[...]
        x_ref[...] = xn
        h_ref[...] = ((xn * _rstd(xn)) * gn_ref[...]).astype(BF16)

    return _pc(body, name=name, grid=(s // t,),
               in_specs=[_rows(t, d), _fixed((1, d)), _rows(t, d), _fixed((1, d))],
               out_specs=[_rows(t, d), _rows(t, d)], out_shape=[SDS((s, d), F32), SDS((s, d), BF16)],
               sem=("parallel",))(m, g_post, res, g_pre)


def _row_dot(a_ref, w_ref):
    r = w_ref.shape[1]
    out = None
    for j in range(N_CHIPS):
        part = jnp.dot(a_ref[:, j * r:(j + 1) * r], w_ref[j], preferred_element_type=F32)
        out = part if out is None else out + part
    return out


def _row_dot_specs(t, a, wg):
    return [_rows(t, a.shape[1]), _fixed(wg.shape)]


def out_proj_post_mix(name, a, wg, g_post, res, g_pre):
    s, d = res.shape
    t = _tile(s, ROW_TILE)

    def body(a_ref, w_ref, gp_ref, r_ref, gn_ref, m_ref, x_ref, h_ref):
        mv = _row_dot(a_ref, w_ref)
        m_ref[...] = mv
        xn = r_ref[...] + (mv * _rstd(mv)) * gp_ref[...]
        x_ref[...] = xn
        h_ref[...] = ((xn * _rstd(xn)) * gn_ref[...]).astype(BF16)

    return _pc(body, name=name, grid=(s // t,),
               in_specs=_row_dot_specs(t, a, wg) + [_fixed((1, d)), _rows(t, d), _fixed((1, d))],
               out_specs=[_rows(t, d)] * 3, out_shape=[SDS((s, d), F32), SDS((s, d), F32), SDS((s, d), BF16)],
               sem=("parallel",))(a, wg, g_post, res, g_pre)


def ple_gate_fwd_mid(name, a, wg, pp, x2, g_next):
    s, d = x2.shape
    t = _tile(s, ROW_TILE)

    def body(a_ref, w_ref, p_ref, x_ref, g_ref, z_ref, xo_ref, h_ref):
        z = _row_dot(a_ref, w_ref)
        z_ref[...] = z
        xn = x_ref[...] + p_ref[...] * _sigmoid(z)
        xo_ref[...] = xn
        h_ref[...] = ((xn * _rstd(xn)) * g_ref[...]).astype(BF16)

    return _pc(body, name=name, grid=(s // t,),
               in_specs=_row_dot_specs(t, a, wg) + [_rows(t, d), _rows(t, d), _fixed((1, d))],
               out_specs=[_rows(t, d)] * 3, out_shape=[SDS((s, d), F32), SDS((s, d), F32), SDS((s, d), BF16)],
               sem=("parallel",))(a, wg, pp, x2, g_next)


def ple_gate_fwd_loss(name, a, wg, pp, x2, target):
    s, d = x2.shape
    t = _tile(s, ROW_TILE)

    def body(a_ref, w_ref, p_ref, x_ref, t_ref, dy_ref, dpp_ref, dz_ref, l_ref):
        gate = _sigmoid(_row_dot(a_ref, w_ref))
        pv = p_ref[...]
        err = (x_ref[...] + pv * gate) - t_ref[...]
        dy = err * (1.0 / d)
        dy_ref[...] = dy
        dpp_ref[...] = (dy * gate).astype(BF16)
        dz_ref[...] = ((dy * pv) * (gate * (1.0 - gate))).astype(BF16)
        part = jnp.sum(jnp.sum(err * err, axis=-1, keepdims=True), axis=0, keepdims=True) * (0.5 / d)

        @pl.when(pl.program_id(0) == 0)
        def _():
            l_ref[...] = part

        @pl.when(pl.program_id(0) > 0)
        def _():
            l_ref[...] += part

    return _pc(body, name=name, grid=(s // t,),
               in_specs=_row_dot_specs(t, a, wg) + [_rows(t, d), _rows(t, d), _rows(t, d)],
               out_specs=[_rows(t, d), _rows(t, d), _rows(t, d), _fixed((1, 1))],
               out_shape=[SDS((s, d), F32), SDS((s, d), BF16), SDS((s, d), BF16), SDS((1, 1), F32)],
               sem=("arbitrary",))(a, wg, pp, x2, target)


ANY_SPEC = pl.BlockSpec(memory_space=pl.ANY)


def ple_bwd(name, dx3, z, pp, after):
    s, d = z.shape
    t = _tile(s, ROW_TILE)

    def body(d_ref, z_ref, p_ref, after_ref, dpp_ref, dz_ref):
        gate = _sigmoid(z_ref[...])
        dv = d_ref[...]
        dpp_ref[...] = (dv * gate).astype(BF16)
        dz_ref[...] = ((dv * p_ref[...]) * (gate * (1.0 - gate))).astype(BF16)

    return _pc(body, name=name, grid=(s // t,), in_specs=[_rows(t, d)] * 3 + [ANY_SPEC], out_specs=[_rows(t, d)] * 2,
               out_shape=[SDS((s, d), BF16)] * 2, sem=("parallel",))(dx3, z, pp, after)


def norm_bwd(name, xin, gain, dy, dres=None, out_dtype=F32):
    s, d = xin.shape
    t = _tile(s, ROW_TILE)

    def body(*refs):
        if dres is None:
            x_ref, g_ref, dy_ref, dx_ref, dg_ref = refs
            r_ref = None
        else:
            x_ref, g_ref, dy_ref, r_ref, dx_ref, dg_ref = refs
        xv = x_ref[...]
        r = _rstd(xv)
        xh = xv * r
        dyv = dy_ref[...].astype(F32)
        dyg = dyv * g_ref[...]
        c = jnp.mean(dyg * xh, axis=-1, keepdims=True)
        dx = r * (dyg - xh * c)
        if r_ref is not None:
            dx = dx + r_ref[...]
        dx_ref[...] = dx.astype(out_dtype)
        part = jnp.sum(dyv * xh, axis=0, keepdims=True)

        @pl.when(pl.program_id(0) == 0)
        def _():
            dg_ref[...] = part

        @pl.when(pl.program_id(0) > 0)
        def _():
            dg_ref[...] += part

    ins = [xin, gain, dy] + ([] if dres is None else [dres])
    specs = [_rows(t, d), _fixed((1, d)), _rows(t, d)] + ([] if dres is None else [_rows(t, d)])
    return _pc(body, name=name, grid=(s // t,), in_specs=specs, out_specs=[_rows(t, d), _fixed((1, d))],
               out_shape=[SDS((s, d), out_dtype), SDS((1, d), F32)], sem=("arbitrary",))(*ins)


def _lane(shape):
    return lax.broadcasted_iota(jnp.int32, shape, 1)


def _swap_halves(x):
    lo = (_lane(x.shape) % 64) < 32
    return jnp.where(lo, pltpu.roll(x, 96, 1), pltpu.roll(x, 32, 1))


def rope_tables(name, pos_col, inv_freq):
    s = pos_col.shape[0]
    t = _tile(s, ROW_TILE)

    def body(p_ref, f_ref, c_ref, s_ref):
        ang = p_ref[...].astype(F32) * f_ref[...]
        lo = (_lane(ang.shape) % 64) < 32
        c_ref[...] = jnp.cos(ang)
        sn = jnp.sin(ang)
        s_ref[...] = jnp.where(lo, -sn, sn)

    return _pc(body, name=name, grid=(s // t,), in_specs=[_rows(t, 1), _fixed((1, LANES))],
               out_specs=[_rows(t, LANES)] * 2, out_shape=[SDS((s, LANES), F32)] * 2, sem=("parallel",))(pos_col, inv_freq)


def _pad_heads(chunk, lo_mask):
    zero = jnp.zeros_like(chunk)
    return jnp.where(lo_mask, chunk, zero), jnp.where(lo_mask, pltpu.roll(chunk, 64, 1), zero)


def rope_qkv(name, proj, cos, sin):
    s = proj.shape[0]
    t = _tile(s, ROW_TILE)

    def body(q_ref, kv_ref, c_ref, s_ref, qp_ref, kp_ref, vp_ref):
        cs, sn = c_ref[...], s_ref[...]
        lo_mask = _lane(cs.shape) < 64

        def rot(x):
            return x * cs + _swap_halves(x) * sn

        for j in range(ATTN_WIDTH // LANES):
            a, b = _pad_heads(rot(q_ref[:, j * LANES:(j + 1) * LANES]), lo_mask)
            qp_ref[:, (2 * j) * LANES:(2 * j + 1) * LANES] = a.astype(BF16)
            qp_ref[:, (2 * j + 1) * LANES:(2 * j + 2) * LANES] = b.astype(BF16)
        for j in range(KV_WIDTH // LANES):
            a, b = _pad_heads(rot(kv_ref[:, j * LANES:(j + 1) * LANES]), lo_mask)
            kp_ref[:, (2 * j) * LANES:(2 * j + 1) * LANES] = a.astype(BF16)
            kp_ref[:, (2 * j + 1) * LANES:(2 * j + 2) * LANES] = b.astype(BF16)
            a, b = _pad_heads(kv_ref[:, KV_WIDTH + j * LANES:KV_WIDTH + (j + 1) * LANES], lo_mask)
            vp_ref[:, (2 * j) * LANES:(2 * j + 1) * LANES] = a.astype(BF16)
            vp_ref[:, (2 * j + 1) * LANES:(2 * j + 2) * LANES] = b.astype(BF16)

    return _pc(body, name=name, grid=(s // t,),
               in_specs=[_rows(t, ATTN_WIDTH, 0), _rows(t, 2 * KV_WIDTH, 2), _rows(t, LANES), _rows(t, LANES)],
               out_specs=[_rows(t, N_Q_HEADS * LANES), _rows(t, N_KV_HEADS * LANES), _rows(t, N_KV_HEADS * LANES)],
               out_shape=[SDS((s, N_Q_HEADS * LANES), BF16), SDS((s, N_KV_HEADS * LANES), BF16),
                          SDS((s, N_KV_HEADS * LANES), BF16)],
               sem=("parallel",))(proj, proj, cos, sin)


def _attn_mask(n):
    L = WINDOW
    qi = lax.broadcasted_iota(jnp.int32, (L, 2 * L), 0) + L
    ki = lax.broadcasted_iota(jnp.int32, (L, 2 * L), 1)
    rel = qi - ki
    return (rel >= 0) & (rel < WINDOW) & ((n > 0) | (ki >= L))


def _attn_scores(qh, kk, valid):
    sc = lax.dot_general(qh, kk, _NT, preferred_element_type=F32) * 0.125
    return jnp.where(valid, sc, MASK_VALUE)


def _attn_softmax(sc, sink):
    m = jnp.maximum(jnp.max(sc, axis=-1, keepdims=True), sink)
    e = jnp.exp(sc - m)
    es = jnp.exp(sink - m)
    den = jnp.sum(e, axis=-1, keepdims=True) + es
    return e / den, es / den


def _attn_specs(s):
    L = WINDOW
    cur = lambda n: (n, 0)
    prev = lambda n: (jnp.maximum(n - 1, 0), 0)
    kvw = N_KV_HEADS * LANES
    return [pl.BlockSpec((L, N_Q_HEADS * LANES), cur), pl.BlockSpec((L, kvw), cur), pl.BlockSpec((L, kvw), prev),
            pl.BlockSpec((L, kvw), cur), pl.BlockSpec((L, kvw), prev), pl.BlockSpec(memory_space=pltpu.SMEM)]


def attn_fwd(name, qp, kp, vp, sinks):
    s = qp.shape[0]
    L = WINDOW

    def body(q_ref, kc_ref, kp_ref, vc_ref, vp_ref, sk_ref, o_ref):
        valid = _attn_mask(pl.program_id(0))
        kks, vvs = [], []
        for kvh in range(N_KV_HEADS):
            cols = slice(kvh * LANES, (kvh + 1) * LANES)
            kks.append(jnp.concatenate([kp_ref[:, cols], kc_ref[:, cols]], axis=0))
            vvs.append(jnp.concatenate([vp_ref[:, cols], vc_ref[:, cols]], axis=0))
        scs = [_attn_scores(q_ref[:, h * LANES:(h + 1) * LANES], kks[h // Q_PER_KV], valid) for h in range(N_Q_HEADS)]
        pbs = [_attn_softmax(scs[h], sk_ref[h])[0].astype(BF16) for h in range(N_Q_HEADS)]
        outs = [jnp.dot(pbs[h], vvs[h // Q_PER_KV], preferred_element_type=F32) for h in range(N_Q_HEADS)]
        for j in range(ATTN_WIDTH // LANES):
            o_ref[:, j * LANES:(j + 1) * LANES] = outs[2 * j] + pltpu.roll(outs[2 * j + 1], 64, 1)

    return _pc(body, name=name, grid=(s // L,), in_specs=_attn_specs(s),
               out_specs=pl.BlockSpec((L, ATTN_WIDTH), lambda n: (n, 0)),
               out_shape=SDS((s, ATTN_WIDTH), F32), sem=("parallel",))(qp, kp, kp, vp, vp, sinks)


def attn_bwd(name, qp, kp, vp, sinks, dattn):
    s = qp.shape[0]
    L = WINDOW
    kvw = N_KV_HEADS * LANES

    def body(q_ref, kc_ref, kp_ref, vc_ref, vp_ref, sk_ref, do_ref, dq_ref, dkc_ref, dkp_ref, dvc_ref, dvp_ref, ds_ref):
        n = pl.program_id(0)
        valid = _attn_mask(n)
        lo_mask = _lane((L, LANES)) < 64
        lane1 = _lane((1, LANES))
        dsink = jnp.zeros((1, LANES), F32)
        heads = range(N_Q_HEADS)
        kks, vvs = [], []
        for kvh in range(N_KV_HEADS):
            cols = slice(kvh * LANES, (kvh + 1) * LANES)
            kks.append(jnp.concatenate([kp_ref[:, cols], kc_ref[:, cols]], axis=0))
            vvs.append(jnp.concatenate([vp_ref[:, cols], vc_ref[:, cols]], axis=0))
        qs, dos, scs, dps = [], [], [], []
        for h in heads:
            qs.append(q_ref[:, h * LANES:(h + 1) * LANES])
            chunk = do_ref[:, (h // 2) * LANES:(h // 2 + 1) * LANES]
            if h % 2:
                chunk = pltpu.roll(chunk, 64, 1)
            dos.append(jnp.where(lo_mask, chunk, 0.0).astype(BF16))
            scs.append(_attn_scores(qs[h], kks[h // Q_PER_KV], valid))
            dps.append(lax.dot_general(dos[h], vvs[h // Q_PER_KV], _NT, preferred_element_type=F32))
        pbs, dsbs = [], []
        for h in heads:
            p, ps = _attn_softmax(scs[h], sk_ref[h])
            delta = jnp.sum(p * dps[h], axis=-1, keepdims=True)
            dsbs.append(((p * (dps[h] - delta)) * 0.125).astype(BF16))
            pbs.append(p.astype(BF16))
            dsink = dsink + jnp.where(lane1 == h, -jnp.sum(ps * delta, axis=0, keepdims=True), 0.0)
        for kvh in range(N_KV_HEADS):
            cols = slice(kvh * LANES, (kvh + 1) * LANES)
            dkk = jnp.zeros((2 * L, LANES), F32)
            dvv = jnp.zeros((2 * L, LANES), F32)
            for h in range(kvh * Q_PER_KV, (kvh + 1) * Q_PER_KV):
                dq_ref[:, h * LANES:(h + 1) * LANES] = jnp.dot(dsbs[h], kks[kvh], preferred_element_type=F32)
                dkk = dkk + lax.dot_general(dsbs[h], qs[h], _TN, preferred_element_type=F32)
                dvv = dvv + lax.dot_general(pbs[h], dos[h], _TN, preferred_element_type=F32)
            dkp_ref[:, cols] = dkk[:L]
            dkc_ref[:, cols] = dkk[L:]
            dvp_ref[:, cols] = dvv[:L]
            dvc_ref[:, cols] = dvv[L:]

        @pl.when(n == 0)
        def _():
            ds_ref[...] = dsink

        @pl.when(n > 0)
        def _():
            ds_ref[...] += dsink

    blk = lambda w: pl.BlockSpec((L, w), lambda n: (n, 0))
    return _pc(body, name=name, grid=(s // L,), in_specs=_attn_specs(s) + [blk(ATTN_WIDTH)],
               out_specs=[blk(N_Q_HEADS * LANES), blk(kvw), blk(kvw), blk(kvw), blk(kvw), _fixed((1, LANES))],
               out_shape=[SDS((s, N_Q_HEADS * LANES), F32)] + [SDS((s, kvw), F32)] * 4 + [SDS((1, LANES), F32)],
               sem=("arbitrary",))(qp, kp, kp, vp, vp, sinks, dattn)


def rope_bwd(name, dqp, dkc, dkp, dvc, dvp, cos, sin):
    s = dqp.shape[0]
    L = WINDOW
    nb = s // L
    kvw = N_KV_HEADS * LANES

    def body(dq_ref, dkc_ref, dkp_ref, dvc_ref, dvp_ref, c_ref, s_ref, o_ref):
        cs, sn = c_ref[...], s_ref[...]
        more = (pl.program_id(0) < nb - 1).astype(F32)

        def unrot(x):
            return x * cs - _swap_halves(x) * sn

        def compact(ref, j, nxt=None):
            a = ref[:, (2 * j) * LANES:(2 * j + 1) * LANES]
            b = ref[:, (2 * j + 1) * LANES:(2 * j + 2) * LANES]
            if nxt is not None:
                a = a + more * nxt[:, (2 * j) * LANES:(2 * j + 1) * LANES]
                b = b + more * nxt[:, (2 * j + 1) * LANES:(2 * j + 2) * LANES]
            return a + pltpu.roll(b, 64, 1)

        for j in range(ATTN_WIDTH // LANES):
            o_ref[:, j * LANES:(j + 1) * LANES] = unrot(compact(dq_ref, j)).astype(BF16)
        for j in range(KV_WIDTH // LANES):
            o_ref[:, (COL_K + j) * LANES:(COL_K + j + 1) * LANES] = unrot(compact(dkc_ref, j, dkp_ref)).astype(BF16)
            o_ref[:, (COL_V + j) * LANES:(COL_V + j + 1) * LANES] = compact(dvc_ref, j, dvp_ref).astype(BF16)

    cur = lambda n: (n, 0)
    nxt = lambda n: (jnp.minimum(n + 1, nb - 1), 0)
    return _pc(body, name=name, grid=(nb,),
               in_specs=[pl.BlockSpec((L, N_Q_HEADS * LANES), cur), pl.BlockSpec((L, kvw), cur), pl.BlockSpec((L, kvw), nxt),
                         pl.BlockSpec((L, kvw), cur), pl.BlockSpec((L, kvw), nxt), pl.BlockSpec((L, LANES), cur),
                         pl.BlockSpec((L, LANES), cur)],
               out_specs=pl.BlockSpec((L, COL_HQ * LANES), cur), out_shape=SDS((s, COL_HQ * LANES), BF16),
               sem=("parallel",))(dqp, dkc, dkp, dvc, dvp, cos, sin)


def _split3(x):
    a = x.astype(BF16)
    r = x - a.astype(F32)
    b = r.astype(BF16)
    c = (r - b.astype(F32)).astype(BF16)
    return a, b, c


def _chunk_sum(x, upper):
    t = x.shape[0]
    ri = lax.broadcasted_iota(jnp.int32, (t, t), 0)
    ci = lax.broadcasted_iota(jnp.int32, (t, t), 1)
    same = (ri // HGRN_CHUNK) == (ci // HGRN_CHUNK)
    tri = (ci >= ri) if upper else (ci <= ri)
    m = jnp.where(same & tri, 1.0, 0.0).astype(BF16)
    out = None
    for part in _split3(x):
        y = jnp.dot(m, part, preferred_element_type=F32)
        out = y if out is None else out + y
    return out


def _lower_bound(l_ref, layer):
    lv = l_ref[...]
    e = jnp.exp(lv - jnp.max(lv, axis=0, keepdims=True))
    sm = e / jnp.sum(e, axis=0, keepdims=True)
    s0 = sm[0:1]
    return (s0 - s0) if layer == 0 else ((s0 + sm[1:2]) - s0)


def _hgrn_gates(hq_ref, hf_ref, lb):
    z = hf_ref[...]
    sg = _sigmoid(z)
    f = lb + (1.0 - lb) * sg
    kin = (1.0 - lb) * _sigmoid(-z)
    hq = hq_ref[...]
    sq = _sigmoid(hq)
    return sg, f, kin, hq, sq


def _shift_down(x, d):
    return x if d == 0 else pltpu.roll(x, d, 0)


def _shift_up(x, d):
    return x if d == 0 else pltpu.roll(x, x.shape[0] - d, 0)


CHUNKS_PER_BLOCK = LANES // HGRN_CHUNK


def _chunk_iotas():
    shape = (HGRN_CHUNK, LANES)
    return lax.broadcasted_iota(jnp.int32, shape, 0), lax.broadcasted_iota(jnp.int32, shape, 1)


def _chunk_rows(block, chunk):
    start = block * LANES + chunk * HGRN_CHUNK
    return slice(start, start + HGRN_CHUNK)


HGRN_HEADS_PER_STEP = 2
HGRN_STEP_WIDTH = HGRN_HEADS_PER_STEP * LANES


def _hgrn_specs(t, rev, nt):
    row = (lambda h, i: nt - 1 - i) if rev else (lambda h, i: i)
    col = lambda base: pl.BlockSpec((t, HGRN_STEP_WIDTH),
                                    lambda h, i, base=base: (row(h, i), base // HGRN_HEADS_PER_STEP + h))
    return col, row


def _head_views(refs, hh):
    return [r.at[:, pl.ds(hh * LANES, LANES)] for r in refs]


def hgrn_fwd(name, proj, lb_logits, layer):
    s = proj.shape[0]
    t = _tile(s, ROW_TILE)
    nt = s // t
    nc = t // HGRN_CHUNK
    col, row = _hgrn_specs(t, False, nt)

    def body(hq_ref, hf_ref, hi_ref, l_ref, o_ref, st_ref, state):
        @pl.when(pl.program_id(1) == 0)
        def _():
            state[...] = jnp.zeros_like(state)

        for hh in range(HGRN_HEADS_PER_STEP):
            head(*_head_views((hq_ref, hf_ref, hi_ref, l_ref, o_ref), hh), st_ref.at[:, hh], state.at[hh])

    def head(hq_ref, hf_ref, hi_ref, l_ref, o_ref, st_ref, state):
        lb = _lower_bound(l_ref, layer)
        sg, f, kin, hq, sq = _hgrn_gates(hq_ref, hf_ref, lb)
        q = hq * sq
        vb = hi_ref[...].astype(BF16)
        b = _chunk_sum(jnp.log(f), False)
        qe = (q * jnp.exp(b)).astype(BF16)
        trow, lane = _chunk_iotas()
        chunks = [(j, cc) for j in range(t // LANES) for cc in range(CHUNKS_PER_BLOCK)]
        decay, update = [], []
        for j, cc in chunks:
            rs = _chunk_rows(j, cc)
            bc = b[rs]
            bl = bc[HGRN_CHUNK - 1:HGRN_CHUNK, :]
            ke = (kin[rs] * jnp.exp(bl - bc)).astype(BF16)
            decay.append(jnp.exp(bl))
            update.append(lax.dot_general(vb[rs], ke, _TN, preferred_element_type=F32))
        st = state[...]
        for c in range(nc):
            st_ref[c] = st
            st = st * decay[c] + update[c]
        state[...] = st
        o_inter = [lax.dot_general(qe[c * HGRN_CHUNK:(c + 1) * HGRN_CHUNK], st_ref[c].astype(BF16), _NT,
                                   preferred_element_type=F32) for c in range(nc)]
        for j in range(t // LANES):
            blk = slice(j * LANES, (j + 1) * LANES)
            rows = []
            for cc in range(CHUNKS_PER_BLOCK):
                rs = _chunk_rows(j, cc)
                bc, qc, kc = b[rs], q[rs], kin[rs]
                here = trow + cc * HGRN_CHUNK
                am = jnp.where(lane == here, jnp.sum(qc * kc, axis=-1, keepdims=True), 0.0)
                for d in range(1, HGRN_CHUNK):
                    e = jnp.exp(jnp.where(trow >= d, bc - _shift_down(bc, d), MASK_VALUE))
                    a = jnp.sum((qc * _shift_down(kc, d)) * e, axis=-1, keepdims=True)
                    am = jnp.where(lane == here - d, a, am)
                rows.append(am)
            o_intra = jnp.dot(jnp.concatenate(rows, axis=0).astype(BF16), vb[blk], preferred_element_type=F32)
            for cc in range(CHUNKS_PER_BLOCK):
                rs = _chunk_rows(j, cc)
                o_ref[rs, :] = o_intra[cc * HGRN_CHUNK:(cc + 1) * HGRN_CHUNK] + o_inter[j * CHUNKS_PER_BLOCK + cc]

    hp = HGRN_HEADS_PER_STEP
    return _pc(body, name=name, grid=(HGRN_HEADS // hp, nt),
               in_specs=[col(COL_HQ), col(COL_HF), col(COL_HI), pl.BlockSpec((2, HGRN_STEP_WIDTH), lambda h, i: (0, h))],
               out_specs=[pl.BlockSpec((t, HGRN_STEP_WIDTH), lambda h, i: (i, h)),
                          pl.BlockSpec((nc, hp, LANES, LANES), lambda h, i: (i, h, 0, 0))],
               out_shape=[SDS((s, HGRN_WIDTH), F32), SDS((s // HGRN_CHUNK, HGRN_HEADS, LANES, LANES), F32)],
               scratch=[pltpu.VMEM((hp, LANES, LANES), F32)],
               sem=("parallel", "arbitrary"))(proj, proj, proj, lb_logits)


def hgrn_bwd(name, proj, lb_logits, layer, states, do):
    s = proj.shape[0]
    t = _tile(s, ROW_TILE)
    nt = s // t
    nc = t // HGRN_CHUNK
    col, row = _hgrn_specs(t, True, nt)

    def body(hq_ref, hf_ref, hi_ref, l_ref, st_ref, do_ref, dhq_ref, dhf_ref, dhi_ref, dlb_ref, dstate):
        @pl.when(pl.program_id(1) == 0)
        def _():
            dstate[...] = jnp.zeros_like(dstate)

        for hh in range(HGRN_HEADS_PER_STEP):
            hq_v, hf_v, hi_v, l_v, do_v, dhq_v, dhf_v, dhi_v, dlb_v = _head_views(
                (hq_ref, hf_ref, hi_ref, l_ref, do_ref, dhq_ref, dhf_ref, dhi_ref, dlb_ref), hh)
            head(hq_v, hf_v, hi_v, l_v, st_ref.at[:, hh], do_v, dhq_v, dhf_v, dhi_v, dlb_v, dstate.at[hh])

    def head(hq_ref, hf_ref, hi_ref, l_ref, st_ref, do_ref, dhq_ref, dhf_ref, dhi_ref, dlb_ref, dstate):
        first = pl.program_id(1) == 0
        lb = _lower_bound(l_ref, layer)
        sg, f, kin, hq, sq = _hgrn_gates(hq_ref, hf_ref, lb)
        q = hq * sq
        vb = hi_ref[...].astype(BF16)
        b = _chunk_sum(jnp.log(f), False)
        dob = do_ref[...].astype(BF16)
        eb = jnp.exp(b)
        qe = q * eb
        qeb = qe.astype(BF16)
        trow, lane = _chunk_iotas()
        last_row = trow == HGRN_CHUNK - 1

        decay, update = [None] * nc, [None] * nc
        for c in range(nc):
            rs = slice(c * HGRN_CHUNK, (c + 1) * HGRN_CHUNK)
            decay[c] = jnp.exp(b[(c + 1) * HGRN_CHUNK - 1:(c + 1) * HGRN_CHUNK, :])
            update[c] = lax.dot_general(dob[rs], qeb[rs], _TN, preferred_element_type=F32)
        dn_in = [None] * nc
        dn = dstate[...]
        for c in reversed(range(nc)):
            dn_in[c] = dn
            dn = dn * decay[c] + update[c]
        dstate[...] = dn

        dq_c, dk_c, dv_c, dbl_c = [None] * nc, [None] * nc, [None] * nc, [None] * nc
        for c in range(nc):
            rs = slice(c * HGRN_CHUNK, (c + 1) * HGRN_CHUNK)
            bc = b[rs]
            ekb = jnp.exp(bc[HGRN_CHUNK - 1:HGRN_CHUNK, :] - bc)
            ke = kin[rs] * ekb
            st = st_ref[c]
            dnb = dn_in[c].astype(BF16)
            dke = jnp.dot(vb[rs], dnb, preferred_element_type=F32)
            dq_c[c] = jnp.dot(dob[rs], st.astype(BF16), preferred_element_type=F32) * eb[rs]
            dk_c[c] = dke * ekb
            dv_c[c] = lax.dot_general(ke.astype(BF16), dnb, _NT, preferred_element_type=F32)
            dbl_c[c] = jnp.sum(dn_in[c] * st, axis=0, keepdims=True) * decay[c] + jnp.sum(dke * ke, axis=0, keepdims=True)

        db_c = [None] * nc
        for j in range(t // LANES):
            blk = slice(j * LANES, (j + 1) * LANES)
            damat = lax.dot_general(dob[blk], vb[blk], _NT, preferred_element_type=F32)
            rows = [None] * CHUNKS_PER_BLOCK
            for cc in range(CHUNKS_PER_BLOCK):
                c = j * CHUNKS_PER_BLOCK + cc
                rs = _chunk_rows(j, cc)
                bc, qc, kc = b[rs], q[rs], kin[rs]
                dam = damat[cc * HGRN_CHUNK:(cc + 1) * HGRN_CHUNK]
                here = trow + cc * HGRN_CHUNK
                on = lane == here
                da = jnp.sum(jnp.where(on, dam, 0.0), axis=-1, keepdims=True)
                am = jnp.where(on, jnp.sum(qc * kc, axis=-1, keepdims=True), 0.0)
                dq = dq_c[c] + da * kc
                dk = dk_c[c] + da * qc
                for d in range(1, HGRN_CHUNK):
                    on = lane == here - d
                    e = jnp.exp(jnp.where(trow >= d, bc - _shift_down(bc, d), MASK_VALUE))
                    kse = _shift_down(kc, d) * e
                    am = jnp.where(on, jnp.sum(qc * kse, axis=-1, keepdims=True), am)
                    da = jnp.sum(jnp.where(on, dam, 0.0), axis=-1, keepdims=True)
                    dq = dq + da * kse
                    dk = dk + _shift_up(da * (qc * e), d)
                rows[cc] = am
                dq_c[c], dk_c[c] = dq, dk
                db_c[c] = (qc * dq - kc * dk) + jnp.where(last_row, dbl_c[c], 0.0)
            dv_blk = lax.dot_general(jnp.concatenate(rows, axis=0).astype(BF16), dob[blk], _TN, preferred_element_type=F32)
            for cc in range(CHUNKS_PER_BLOCK):
                c = j * CHUNKS_PER_BLOCK + cc
                dv_c[c] = dv_c[c] + dv_blk[cc * HGRN_CHUNK:(cc + 1) * HGRN_CHUNK]
        dq = jnp.concatenate(dq_c, axis=0)
        dk = jnp.concatenate(dk_c, axis=0)
        dv = jnp.concatenate(dv_c, axis=0)
        db = jnp.concatenate(db_c, axis=0)
        dg = _chunk_sum(db, True)
        dhq_ref[...] = (dq * (sq * (1.0 + hq * (1.0 - sq)))).astype(BF16)
        dhi_ref[...] = dv.astype(BF16)
        dfk = dg / f - dk
        dhf_ref[...] = ((dfk * (1.0 - lb)) * (sg * (1.0 - sg))).astype(BF16)
        part = jnp.sum(dfk * (1.0 - sg), axis=0, keepdims=True)

        @pl.when(first)
        def _():
            dlb_ref[...] = part

        @pl.when(jnp.logical_not(first))
        def _():
            dlb_ref[...] += part

    hp = HGRN_HEADS_PER_STEP
    out_col = pl.BlockSpec((t, HGRN_STEP_WIDTH), lambda h, i: (nt - 1 - i, h))
    return _pc(body, name=name, grid=(HGRN_HEADS // hp, nt),
               in_specs=[col(COL_HQ), col(COL_HF), col(COL_HI), pl.BlockSpec((2, HGRN_STEP_WIDTH), lambda h, i: (0, h)),
                         pl.BlockSpec((nc, hp, LANES, LANES), lambda h, i: (nt - 1 - i, h, 0, 0)), out_col],
               out_specs=[out_col, out_col, out_col, pl.BlockSpec((1, HGRN_STEP_WIDTH), lambda h, i: (0, h))],
               out_shape=[SDS((s, HGRN_WIDTH), BF16)] * 3 + [SDS((1, HGRN_WIDTH), F32)],
               scratch=[pltpu.VMEM((hp, LANES, LANES), F32)],
               sem=("parallel", "arbitrary"))(proj, proj, proj, lb_logits, states, do)


def mix_out_fwd(name, attn, o, proj, g_attn, g_hgrn):
    s = attn.shape[0]
    t = _tile(s, ROW_TILE)
    half = HGRN_WIDTH // 2

    def body(a_ref, o_ref, hg0_ref, hg1_ref, ga_ref, gh_ref, c_ref):
        av = a_ref[...]
        c_ref[:, :ATTN_WIDTH] = ((av * _rstd(av)) * ga_ref[...]).astype(BF16)
        for j in range(HGRN_HEADS):
            cols = slice(j * LANES, (j + 1) * LANES)
            ov = o_ref[:, cols]
            hg_ref, hcols = (hg0_ref, cols) if j < 4 else (hg1_ref, slice((j - 4) * LANES, (j - 3) * LANES))
            hg = hg_ref[:, hcols]
            on = (ov * _rstd(ov)) * gh_ref[:, cols]
            c_ref[:, ATTN_WIDTH + j * LANES:ATTN_WIDTH + (j + 1) * LANES] = (on * (hg * _sigmoid(hg))).astype(BF16)

    return _pc(body, name=name, grid=(s // t,),
               in_specs=[_rows(t, ATTN_WIDTH), _rows(t, HGRN_WIDTH), _rows(t, half, COL_HG // 4), _rows(t, half, COL_HG // 4 + 1),
                         _fixed((1, ATTN_WIDTH)), _fixed((1, HGRN_WIDTH))],
               out_specs=_rows(t, D_MODEL), out_shape=SDS((s, D_MODEL), BF16), sem=("parallel",))(attn, o, proj, proj, g_attn, g_hgrn)


def mix_out_bwd(name, dcat, attn, o, proj, g_attn, g_hgrn):
    s = attn.shape[0]
    t = _tile(s, ROW_TILE)
    half = HGRN_WIDTH // 2

    def body(dc_ref, a_ref, o_ref, hg0_ref, hg1_ref, ga_ref, gh_ref, da_ref, do_ref, dhg_ref, dga_ref, dgh_ref, pa_s, ph_s):
        av = a_ref[...]
        r = _rstd(av)
        xh = av * r
        dyv = dc_ref[:, :ATTN_WIDTH]
        dyg = dyv * ga_ref[...]
        da_ref[...] = r * (dyg - xh * jnp.mean(dyg * xh, axis=-1, keepdims=True))
        pa_s[...] = jnp.sum(dyv * xh, axis=0, keepdims=True)
        for j in range(HGRN_HEADS):
            cols = slice(j * LANES, (j + 1) * LANES)
            ov = o_ref[:, cols]
            hg_ref, hcols = (hg0_ref, cols) if j < 4 else (hg1_ref, slice((j - 4) * LANES, (j - 3) * LANES))
            hg = hg_ref[:, hcols]
            sg = _sigmoid(hg)
            r = _rstd(ov)
            xh = ov * r
            gain = gh_ref[:, cols]
            dh = dc_ref[:, ATTN_WIDTH + j * LANES:ATTN_WIDTH + (j + 1) * LANES]
            dhg_ref[:, cols] = ((dh * (xh * gain)) * (sg * (1.0 + hg * (1.0 - sg)))).astype(BF16)
            dyv = dh * (hg * sg)
            dyg = dyv * gain
            do_ref[:, cols] = r * (dyg - xh * jnp.mean(dyg * xh, axis=-1, keepdims=True))
            ph_s[:, cols] = jnp.sum(dyv * xh, axis=0, keepdims=True)

        @pl.when(pl.program_id(0) == 0)
        def _():
            dga_ref[...] = pa_s[...]
            dgh_ref[...] = ph_s[...]

        @pl.when(pl.program_id(0) > 0)
        def _():
            dga_ref[...] += pa_s[...]
            dgh_ref[...] += ph_s[...]

    return _pc(body, name=name, grid=(s // t,),
               in_specs=[_rows(t, D_MODEL), _rows(t, ATTN_WIDTH), _rows(t, HGRN_WIDTH), _rows(t, half, COL_HG // 4),
                         _rows(t, half, COL_HG // 4 + 1), _fixed((1, ATTN_WIDTH)), _fixed((1, HGRN_WIDTH))],
               out_specs=[_rows(t, ATTN_WIDTH), _rows(t, HGRN_WIDTH), _rows(t, HGRN_WIDTH), _fixed((1, ATTN_WIDTH)),
                          _fixed((1, HGRN_WIDTH))],
               out_shape=[SDS((s, ATTN_WIDTH), F32), SDS((s, HGRN_WIDTH), F32), SDS((s, HGRN_WIDTH), BF16),
                          SDS((1, ATTN_WIDTH), F32), SDS((1, HGRN_WIDTH), F32)],
               scratch=[pltpu.VMEM((1, ATTN_WIDTH), F32), pltpu.VMEM((1, HGRN_WIDTH), F32)],
               sem=("arbitrary",))(dcat, attn, o, proj, proj, g_attn, g_hgrn)


BIG = (("w_in", 2048, 1408, 0), ("w_out", 512, 2048, 1), ("w_ffn_gate", 2048, 1408, 0), ("w_ffn_up", 2048, 1408, 0),
       ("w_ffn_down", 1408, 2048, 1), ("w_ple_gate", 512, 2048, 1), ("w_ple_proj", 256, 512, 0))
BIG_BY_NAME = {spec[0]: spec for spec in BIG}
HBM_SPEC = pl.BlockSpec(memory_space=pltpu.HBM)
SEM_SPEC = pl.BlockSpec(memory_space=pltpu.SEMAPHORE)
TOKEN_SHAPE = (8, LANES)


def _split_call(body, *, name, in_specs, out_specs, out_shape, aliases):
    return pl.pallas_call(body, name=name, in_specs=in_specs, out_specs=out_specs, out_shape=out_shape,
                          input_output_aliases=aliases,
                          compiler_params=pltpu.CompilerParams(has_side_effects=pltpu.SideEffectType.DATAFLOW_SIDE_EFFECTING))


def _in_hbm(arrays):
    return [pltpu.with_memory_space_constraint(a, pltpu.HBM) for a in arrays]


GROUP_STEPS = 8


def _step_rows(rows):
    assert rows % (GROUP_STEPS * 16) == 0
    return rows // GROUP_STEPS


def cast_to_slots(name, place, ws, layer, after):
    nt = len(ws)

    def body(place_ref, *refs):
        for t in range(nt):
            refs[nt + 1 + t][...] = refs[t][...].astype(BF16)

    in_specs, out_specs = [], []
    for w in ws:
        block = (None, _step_rows(w.shape[1]), w.shape[2])
        in_specs.append(pl.BlockSpec(block, lambda i, pr: (layer, i, 0)))
        out_specs.append(pl.BlockSpec(block, lambda i, pr: (pr[1], i, 0)))
    gs = pltpu.PrefetchScalarGridSpec(num_scalar_prefetch=1, grid=(GROUP_STEPS,), in_specs=in_specs + [ANY_SPEC],
                                      out_specs=out_specs)
    return _pc(body, name=name, grid_spec=gs, in_specs=None, out_specs=None,
               out_shape=[SDS((N_CHIPS,) + w.shape[1:], BF16) for w in ws], sem=("parallel",))(place, *ws, after)


def _place():
    x, y, c = lax.axis_index("x"), lax.axis_index("y"), lax.axis_index("c")
    chips = [(1 - x, y), (x, 1 - y), (1 - x, 1 - y)]
    return x, y, c, chips


def _half(ref, axis, c, rows, cols):
    if axis == 0:
        return ref.at[pl.ds(pl.multiple_of(c * (rows // 2), 16), rows // 2), :]
    return ref.at[:, pl.ds(pl.multiple_of(c * (cols // 2), LANES), cols // 2)]


def _gather_copies(specs, bufs, send, recv):
    x, y, c, chips = _place()
    cps = []
    for t, (_, rows, cols, axis) in enumerate(specs):
        mine = _half(bufs[t].at[2 * x + y], axis, c, rows, cols)
        for k, (cx, cy) in enumerate(chips):
            cps.append(pltpu.make_async_remote_copy(src_ref=mine, dst_ref=mine, send_sem=send.at[3 * t + k],
                                                    recv_sem=recv.at[3 * t + k], device_id=(cx, cy, c), device_id_type=MESH))
    return cps


def gather_start(name, specs, bufs):
    nt = len(bufs)
    n = 3 * nt

    def body(*refs):
        send, recv, token = refs[nt], refs[nt + 1], refs[-1]
        for cp in _gather_copies(specs, refs[:nt], send, recv):
            cp.start()
        token[...] = jnp.zeros(TOKEN_SHAPE, F32)

    out = _split_call(
        body, name=name, in_specs=[HBM_SPEC] * nt,
        out_specs=(SEM_SPEC, SEM_SPEC) + (HBM_SPEC,) * nt + (pl.BlockSpec(memory_space=pltpu.VMEM),),
        out_shape=(pltpu.SemaphoreType.DMA((n,)), pltpu.SemaphoreType.DMA((n,)))
        + tuple(pltpu.HBM(b.shape, b.dtype) for b in bufs) + (SDS(TOKEN_SHAPE, F32),),
        aliases={t: 2 + t for t in range(nt)})(*_in_hbm(bufs))
    return out[0], out[1], list(out[2:2 + nt]), out[-1]


def gather_wait(name, specs, send, recv, bufs, after):
    nt = len(bufs)

    def body(*refs):
        for cp in _gather_copies(specs, refs[:nt], refs[nt], refs[nt + 1]):
            cp.wait_send()
            cp.wait_recv()

    out = _split_call(
        body, name=name, in_specs=[HBM_SPEC] * nt + [SEM_SPEC, SEM_SPEC, pl.BlockSpec(memory_space=pl.ANY)],
        out_specs=(HBM_SPEC,) * nt, out_shape=tuple(pltpu.HBM(b.shape, b.dtype) for b in bufs),
        aliases={t: t for t in range(nt)})(*bufs, send, recv, after)
    return list(out)


def gather_pass(name, specs, bufs):
    nt = len(bufs)

    def body(*refs):
        ins, outs = refs[:nt], refs[nt:2 * nt]
        send, recv = refs[2 * nt:]
        x, y, c, chips = _place()
        cps = []
        for t, (_, rows, cols, axis) in enumerate(specs):
            for k, (cx, cy) in enumerate(chips):
                cp = pltpu.make_async_remote_copy(
                    src_ref=_half(ins[t].at[2 * cx + cy], axis, c, rows, cols),
                    dst_ref=_half(outs[t].at[2 * cx + cy], axis, c, rows, cols),
                    send_sem=send.at[3 * t + k], recv_sem=recv.at[3 * t + k], device_id=(x, y, 1 - c), device_id_type=MESH)
                cp.start()
                cps.append(cp)
        for t, (_, rows, cols, axis) in enumerate(specs):
            for k, (cx, cy) in enumerate(chips):
                theirs = _half(outs[t].at[2 * cx + cy], axis, 1 - c, rows, cols)
                pltpu.make_async_remote_copy(src_ref=theirs, dst_ref=theirs, send_sem=send.at[3 * t + k],
                                             recv_sem=recv.at[3 * t + k], device_id=(x, y, 1 - c), device_id_type=MESH).wait_recv()
        for cp in cps:
            cp.wait_send()

    return _pc(body, name=name, in_specs=[HBM_SPEC] * nt, out_specs=[HBM_SPEC] * nt,
               out_shape=[SDS(b.shape, b.dtype) for b in bufs], scratch=[pltpu.SemaphoreType.DMA((3 * nt,))] * 2,
               input_output_aliases={t: t for t in range(nt)})(*bufs)


def reduce_to_sibling(name, grads):
    nt = len(grads)

    def body(*refs):
        srcs, dsts = refs[:nt], refs[nt:2 * nt]
        send, recv = refs[2 * nt:]
        x, y, c, _ = _place()
        cps = []
        for t in range(nt):
            cp = pltpu.make_async_remote_copy(src_ref=srcs[t].at[1 - c], dst_ref=dsts[t], send_sem=send.at[t],
                                              recv_sem=recv.at[t], device_id=(x, y, 1 - c), device_id_type=MESH)
            cp.start()
            cps.append(cp)
        for cp in cps:
            cp.wait()

    return _pc(body, name=name, in_specs=[HBM_SPEC] * nt, out_specs=[HBM_SPEC] * nt,
               out_shape=[SDS(g.shape[1:], g.dtype) for g in grads],
               scratch=[pltpu.SemaphoreType.DMA((nt,))] * 2)(*grads)


def _sibling_copies(grads, lands, send, recv):
    x, y, c, _ = _place()
    return [pltpu.make_async_remote_copy(src_ref=grads[t].at[1 - c], dst_ref=lands[t], send_sem=send.at[t], recv_sem=recv.at[t],
                                         device_id=(x, y, 1 - c), device_id_type=MESH) for t in range(len(grads))]


def sibling_start(name, grads):
    nt = len(grads)
    lands = [lax.empty(g.shape[1:], g.dtype) for g in grads]

    def body(*refs):
        send, recv, token = refs[2 * nt], refs[2 * nt + 1], refs[-1]
        for cp in _sibling_copies(refs[:nt], refs[nt:2 * nt], send, recv):
            cp.start()
        token[...] = jnp.zeros(TOKEN_SHAPE, F32)

    both = list(grads) + lands
    out = _split_call(
        body, name=name, in_specs=[HBM_SPEC] * (2 * nt),
        out_specs=(SEM_SPEC, SEM_SPEC) + (HBM_SPEC,) * (2 * nt) + (pl.BlockSpec(memory_space=pltpu.VMEM),),
        out_shape=(pltpu.SemaphoreType.DMA((nt,)), pltpu.SemaphoreType.DMA((nt,)))
        + tuple(pltpu.HBM(b.shape, b.dtype) for b in both) + (SDS(TOKEN_SHAPE, F32),),
        aliases={t: 2 + t for t in range(2 * nt)})(*_in_hbm(both))
    return out[0], out[1], list(out[2:2 + nt]), list(out[2 + nt:2 + 2 * nt]), out[-1]


def sibling_wait(name, send, recv, grads, lands, after):
    nt = len(grads)

    def body(*refs):
        for cp in _sibling_copies(refs[:nt], refs[nt:2 * nt], refs[2 * nt], refs[2 * nt + 1]):
            cp.wait_send()
            cp.wait_recv()

    both = list(grads) + list(lands)
    out = _split_call(
        body, name=name, in_specs=[HBM_SPEC] * (2 * nt) + [SEM_SPEC, SEM_SPEC, pl.BlockSpec(memory_space=pl.ANY)],
        out_specs=(HBM_SPEC,) * (2 * nt), out_shape=tuple(pltpu.HBM(b.shape, b.dtype) for b in both),
        aliases={t: t for t in range(2 * nt)})(*both, send, recv, after)
    return list(out[:nt]), list(out[nt:])


def add_halves(name, place, grads, gots):
    nt = len(grads)

    def body(place_ref, *refs):
        for t in range(nt):
            val = (refs[t][...].astype(F32) + refs[nt + t][...].astype(F32)).astype(BF16)
            refs[2 * nt + t][...] = val

            @pl.when(pl.program_id(1) == place_ref[1])
            def _():
                refs[3 * nt + t][...] = val

    g_specs, o_specs, part_specs, slot_specs = [], [], [], []
    for g in grads:
        _, n, r, c = g.shape
        tr = _step_rows(r)
        g_specs.append(pl.BlockSpec((None, None, tr, c), lambda i, j, pr: (pr[0], j, i, 0)))
        o_specs.append(pl.BlockSpec((None, tr, c), lambda i, j, pr: (j, i, 0)))
        part_specs.append(pl.BlockSpec((None, tr, c), lambda i, j, pr: (j, i, 0)))
        slot_specs.append(pl.BlockSpec((None, tr, c), lambda i, j, pr: (pr[1], i, 0)))
    gs = pltpu.PrefetchScalarGridSpec(num_scalar_prefetch=1, grid=(GROUP_STEPS, N_CHIPS), in_specs=g_specs + o_specs,
                                      out_specs=part_specs + slot_specs)
    out = _pc(body, name=name, grid_spec=gs, in_specs=None, out_specs=None,
              out_shape=[SDS(g.shape[1:], BF16) for g in grads] * 2, sem=("parallel", "arbitrary"))(place, *grads, *gots)
    return list(out[:nt]), list(out[nt:])


def _chips_copies(parts, slots, send, recv):
    x, y, c, chips = _place()
    cps = []
    for t in range(len(parts)):
        for k, (cx, cy) in enumerate(chips):
            cps.append(pltpu.make_async_remote_copy(src_ref=parts[t].at[2 * cx + cy], dst_ref=slots[t].at[2 * x + y],
                                                    send_sem=send.at[3 * t + k], recv_sem=recv.at[3 * t + k],
                                                    device_id=(cx, cy, c), device_id_type=MESH))
    return cps


def chips_start(name, parts, slots):
    nt = len(parts)
    n = 3 * nt

    def body(*refs):
        send, recv, token = refs[2 * nt], refs[2 * nt + 1], refs[-1]
        for cp in _chips_copies(refs[:nt], refs[nt:2 * nt], send, recv):
            cp.start()
        token[...] = jnp.zeros(TOKEN_SHAPE, F32)

    both = list(parts) + list(slots)
    out = _split_call(
        body, name=name, in_specs=[HBM_SPEC] * (2 * nt),
        out_specs=(SEM_SPEC, SEM_SPEC) + (HBM_SPEC,) * (2 * nt) + (pl.BlockSpec(memory_space=pltpu.VMEM),),
        out_shape=(pltpu.SemaphoreType.DMA((n,)), pltpu.SemaphoreType.DMA((n,)))
        + tuple(pltpu.HBM(b.shape, b.dtype) for b in both) + (SDS(TOKEN_SHAPE, F32),),
        aliases={t: 2 + t for t in range(2 * nt)})(*_in_hbm(both))
    return out[0], out[1], list(out[2:2 + nt]), list(out[2 + nt:2 + 2 * nt]), out[-1]


def chips_wait(name, send, recv, parts, slots, after):
    nt = len(parts)

    def body(*refs):
        for cp in _chips_copies(refs[:nt], refs[nt:2 * nt], refs[2 * nt], refs[2 * nt + 1]):
            cp.wait_send()
            cp.wait_recv()

    both = list(parts) + list(slots)
    out = _split_call(
        body, name=name, in_specs=[HBM_SPEC] * (2 * nt) + [SEM_SPEC, SEM_SPEC, pl.BlockSpec(memory_space=pl.ANY)],
        out_specs=(HBM_SPEC,) * (2 * nt), out_shape=tuple(pltpu.HBM(b.shape, b.dtype) for b in both),
        aliases={t: t for t in range(2 * nt)})(*both, send, recv, after)
    return list(out[nt:])


def sum_chips(name, place, slots):
    nt = len(slots)

    def body(place_ref, *refs):
        for t in range(nt):
            s_ref = refs[t]
            acc = s_ref[0].astype(F32)
            for k in range(1, N_CHIPS):
                acc = acc + s_ref[k].astype(F32)
            refs[nt + t][...] = acc

    in_specs, out_specs = [], []
    for sl in slots:
        n, r, c = sl.shape
        tr = _step_rows(r)
        in_specs.append(pl.BlockSpec((n, tr, c), lambda i, pr: (0, i, 0)))
        out_specs.append(pl.BlockSpec((None, tr, c), lambda i, pr: (pr[0], i, 0)))
    gs = pltpu.PrefetchScalarGridSpec(num_scalar_prefetch=1, grid=(GROUP_STEPS,), in_specs=in_specs, out_specs=out_specs)
    return list(_pc(body, name=name, grid_spec=gs, in_specs=None, out_specs=None,
                    out_shape=[SDS((2,) + sl.shape[1:], F32) for sl in slots], sem=("parallel",))(place, *slots))


def share_with_sibling(name, bufs):
    nt = len(bufs)

    def body(*refs):
        ins, outs = refs[:nt], refs[nt:2 * nt]
        send, recv = refs[2 * nt:]
        x, y, c, _ = _place()
        cps = []
        for t in range(nt):
            cp = pltpu.make_async_remote_copy(src_ref=ins[t].at[c], dst_ref=outs[t].at[c], send_sem=send.at[t], recv_sem=recv.at[t],
                                              device_id=(x, y, 1 - c), device_id_type=MESH)
            cp.start()
            cps.append(cp)
        for t in range(nt):
            theirs = outs[t].at[1 - c]
            pltpu.make_async_remote_copy(src_ref=theirs, dst_ref=theirs, send_sem=send.at[t], recv_sem=recv.at[t],
                                         device_id=(x, y, 1 - c), device_id_type=MESH).wait_recv()
        for cp in cps:
            cp.wait_send()

    return _pc(body, name=name, in_specs=[HBM_SPEC] * nt, out_specs=[HBM_SPEC] * nt,
               out_shape=[SDS(b.shape, F32) for b in bufs], scratch=[pltpu.SemaphoreType.DMA((nt,))] * 2,
               input_output_aliases={t: t for t in range(nt)})(*bufs)


def _adamw(w, g, m, v):
    m = ADAM_B1 * m + (1.0 - ADAM_B1) * g
    v = ADAM_B2 * v + (1.0 - ADAM_B2) * (g * g)
    m_hat = m / (1.0 - ADAM_B1 ** ADAM_STEP)
    v_hat = v / (1.0 - ADAM_B2 ** ADAM_STEP)
    delta = -ADAM_LR * (m_hat / (jnp.sqrt(v_hat) + ADAM_EPS) + ADAM_WD * w)
    return delta, m, v


def adamw_big(name, w, m, v, g0, g1, axis):
    _, r, c = w.shape
    _, rh, ch = g0.shape
    tr = _tile(rh, 256)
    nb = rh // tr
    if axis == 0:
        wspec = pl.BlockSpec((None, tr, ch), lambda l, h, i: (l, h * nb + i, 0))
    else:
        wspec = pl.BlockSpec((None, tr, ch), lambda l, h, i: (l, i, h))
    g0spec = pl.BlockSpec((None, tr, ch), lambda l, h, i: (h * (1 - l), i * (1 - l), 0))
    g1spec = pl.BlockSpec((None, tr, ch), lambda l, h, i: (h * l, i * l, 0))

    def body(w_ref, m_ref, v_ref, g0_ref, g1_ref, go_ref, d_ref, mo_ref, vo_ref):
        def run(g_ref):
            g = g_ref[...]
            delta, mn, vn = _adamw(w_ref[...], g, m_ref[...], v_ref[...])
            go_ref[...] = g
            d_ref[...] = delta
            mo_ref[...] = mn
            vo_ref[...] = vn

        @pl.when(pl.program_id(0) == 0)
        def _():
            run(g0_ref)

        @pl.when(pl.program_id(0) == 1)
        def _():
            run(g1_ref)

    return _pc(body, name=name, grid=(2, 2, nb), in_specs=[wspec, wspec, wspec, g0spec, g1spec], out_specs=[wspec] * 4,
               out_shape=[SDS(w.shape, F32)] * 4, sem=("parallel", "parallel", "parallel"))(w, m, v, g0, g1)


SMALL = (("pre_mix_gain", 2048), ("post_mix_gain", 2048), ("pre_ffn_gain", 2048), ("post_ffn_gain", 2048), ("ple_gain", 2048),
         ("attn_out_gain", 1024), ("hgrn_out_gain", 1024), ("hgrn_lb_logits", 1024), ("attn_sinks", 128))
SMALL_ROWS = sum(2 * w // LANES for _, w in SMALL)
SMALL_PAD = -(-SMALL_ROWS // 8) * 8
LB_ROW = sum(2 * w // LANES for _, w in SMALL[:7])


def _pack_small(parts):
    rows = []
    for nm, w in SMALL:
        a = parts[nm].astype(F32)
        if a.shape[1] != w:
            a = jnp.pad(a, ((0, 0), (0, w - a.shape[1])))
        rows.append(a.reshape(2 * w // LANES, LANES))
    rows.append(jnp.zeros((SMALL_PAD - SMALL_ROWS, LANES), F32))
    return jnp.concatenate(rows, axis=0)


def _unpack_small(packed, widths):
    out, r = {}, 0
    for nm, w in SMALL:
        n = 2 * w // LANES
        out[nm] = packed[r:r + n].reshape(2, w)[:, :widths[nm]]
        r += n
    return out


def allreduce_small(name, packed):
    rows = packed.shape[0]

    def body(x_ref, o_ref, buf, send, recv, own_sem):
        x, y, c, _ = _place()
        me = 4 * x + 2 * y + c
        own = pltpu.make_async_copy(x_ref, buf.at[me], own_sem)
        own.start()
        cps = []
        for k in range(1, 8):
            px, py, pc = x ^ (k >> 2), y ^ ((k >> 1) & 1), c ^ (k & 1)
            cp = pltpu.make_async_remote_copy(src_ref=x_ref, dst_ref=buf.at[me], send_sem=send.at[k - 1], recv_sem=recv.at[k - 1],
                                              device_id=(px, py, pc), device_id_type=MESH)
            cp.start()
            cps.append(cp)
        for k in range(1, 8):
            px, py, pc = x ^ (k >> 2), y ^ ((k >> 1) & 1), c ^ (k & 1)
            slot = buf.at[4 * px + 2 * py + pc]
            pltpu.make_async_remote_copy(src_ref=slot, dst_ref=slot, send_sem=send.at[k - 1], recv_sem=recv.at[k - 1],
                                         device_id=(px, py, pc), device_id_type=MESH).wait_recv()
        for cp in cps:
            cp.wait_send()
        own.wait()
        acc = buf[0]
        for k in range(1, 8):
            acc = acc + buf[k]
        o_ref[...] = acc

    vm = pl.BlockSpec(memory_space=pltpu.VMEM)
    return _pc(body, name=name, in_specs=[vm], out_specs=vm, out_shape=SDS((rows, LANES), F32),
               scratch=[pltpu.VMEM((8, rows, LANES), F32), pltpu.SemaphoreType.DMA((7,)), pltpu.SemaphoreType.DMA((7,)),
                        pltpu.SemaphoreType.DMA])(packed)


def adamw_small(name, w, m, v, g):
    rows = w.shape[0]
    n = HGRN_WIDTH // LANES

    def body(w_ref, m_ref, v_ref, g_ref, go_ref, d_ref, mo_ref, vo_ref):
        go_ref[...] = g_ref[...]
        l0 = w_ref[LB_ROW:LB_ROW + n, :]
        l1 = w_ref[LB_ROW + n:LB_ROW + 2 * n, :]
        mx = jnp.maximum(l0, l1)
        e0, e1 = jnp.exp(l0 - mx), jnp.exp(l1 - mx)
        s0, s1 = e0 / (e0 + e1), e1 / (e0 + e1)
        dlb1 = g_ref[LB_ROW + n:LB_ROW + 2 * n, :]
        inner = s1 * dlb1
        go_ref[LB_ROW:LB_ROW + n, :] = s0 * (0.0 - inner)
        go_ref[LB_ROW + n:LB_ROW + 2 * n, :] = s1 * (dlb1 - inner)
        delta, mn, vn = _adamw(w_ref[...], go_ref[...], m_ref[...], v_ref[...])
        d_ref[...] = delta
        mo_ref[...] = mn
        vo_ref[...] = vn

    vm = pl.BlockSpec(memory_space=pltpu.VMEM)
    return _pc(body, name=name, in_specs=[vm] * 4, out_specs=[vm] * 4, out_shape=[SDS((rows, LANES), F32)] * 4)(w, m, v, g)


def _layer_fwd(l, x, h1, p_l, w_in_g, rest_of_weights, gains, cos, sin, sinks, lb_logits, g_next, target):
    n = f"l{l}_"
    proj = mm_col(n + "in_proj", h1, w_in_g)
    qp, kp, vp = rope_qkv(n + "rope_qkv", proj, cos, sin)
    attn = attn_fwd(n + "attn_fwd", qp, kp, vp, sinks)
    o, states = hgrn_fwd(n + "hgrn_fwd", proj, lb_logits, l)
    cat = mix_out_fwd(n + "mix_out_fwd", attn, o, proj, gains["attn_out_gain"], gains["hgrn_out_gain"])
    rest, token = rest_of_weights(cat)
    wts = dict(rest, w_in=w_in_g)
    if token is not None:
        gains = _with_token(gains, "post_mix_gain", token)
    m, x1, h2 = out_proj_post_mix(n + "out_proj_post_mix", cat, wts["w_out"], gains["post_mix_gain"], x, gains["pre_ffn_gain"])
    g, u, a = ffn_gate_up(n + "ffn_gate_up", h2, wts["w_ffn_gate"], wts["w_ffn_up"])
    f = mm_row(n + "ffn_down", a, wts["w_ffn_down"])
    x2, h3 = post_pre_norm(n + "post_ffn", f, gains["post_ffn_gain"], x1, gains["ple_gain"])
    pp = mm_col(n + "ple_proj", p_l, wts["w_ple_proj"])
    saved = dict(x=x, h1=h1, proj=proj, qp=qp, kp=kp, vp=vp, attn=attn, o=o, states=states, cat=cat, m=m, x1=x1, h2=h2,
                 g=g, u=u, a=a, f=f, x2=x2, h3=h3, p=p_l)
    if target is None:
        z, *out = ple_gate_fwd_mid(n + "ple_gate_fwd", h3, wts["w_ple_gate"], pp, x2, g_next)
        saved.update(z=z, pp=pp)
    else:
        dy, dpp, dz, loss = ple_gate_fwd_loss(n + "ple_gate_loss", h3, wts["w_ple_gate"], pp, x2, target)
        out = [dy, loss]
        saved.update(dpp=dpp, dz=dz)
    return out, saved, wts


EARLY = ("w_ple_gate", "w_ple_proj", "w_ffn_down", "w_ffn_gate", "w_ffn_up")
LATE = ("w_out", "w_in")


def _layer_bwd_ffn(l, dx3, sv, wts, gains, after=None, hook=None):
    n = f"l{l}_"
    dpp, dz = (sv["dpp"], sv["dz"]) if "dz" in sv else ple_bwd(n + "ple_bwd", dx3, sv["z"], sv["pp"],
                                                                 dx3 if after is None else after)
    dh3 = mm_row_t(n + "ple_gate_dx", dz, wts["w_ple_gate"])
    if hook is not None:
        gains = _with_token(gains, "ple_gain", hook(dh3))
    dx2, df, d_ple_gain, d_post_ffn = norm_bwd_pair(n + "ple_post_ffn_bwd", sv["x2"], gains["ple_gain"], dh3, dx3, sv["f"],
                                                    gains["post_ffn_gain"])
    dg, du = ffn_down_bwd(n + "ffn_down_bwd", df, wts["w_ffn_down"], sv["g"], sv["u"])
    big = dict(
        w_ple_gate=mm_wg_row(n + "ple_gate_dw", sv["h3"], dz),
        w_ple_proj=mm_wg_col(n + "ple_proj_dw", sv["p"], dpp),
        w_ffn_down=mm_wg_row(n + "ffn_down_dw", sv["a"], df),
        w_ffn_gate=mm_wg_col(n + "ffn_gate_dw", sv["h2"], dg),
        w_ffn_up=mm_wg_col(n + "ffn_up_dw", sv["h2"], du),
    )
    return dict(dx2=dx2, dg=dg, du=du), big, dict(ple_gain=d_ple_gain, post_ffn_gain=d_post_ffn)


def _layer_bwd_mix(l, st, sv, wts, gains, cos, sin, sinks, lb_logits, after=None, hook=None):
    n = f"l{l}_"
    dh2 = mm_col_t(n + "ffn_gate_dx", st["dg"], wts["w_ffn_gate"], after=after)
    dh2 = mm_col_t(n + "ffn_up_dx", st["du"], wts["w_ffn_up"], add=dh2)
    if hook is not None:
        gains = _with_token(gains, "pre_ffn_gain", hook(dh2))
    dx1, dm, d_pre_ffn, d_post_mix = norm_bwd_pair(n + "pre_ffn_post_mix_bwd", sv["x1"], gains["pre_ffn_gain"], dh2, st["dx2"],
                                                   sv["m"], gains["post_mix_gain"])
    dcat = mm_row_t(n + "out_proj_dx", dm, wts["w_out"])
    dattn, do, dhg, d_attn_gain, d_hgrn_gain = mix_out_bwd(n + "mix_out_bwd", dcat, sv["attn"], sv["o"], sv["proj"],
                                                            gains["attn_out_gain"], gains["hgrn_out_gain"])
    dqp, dkc, dkp, dvc, dvp, dsinks = attn_bwd(n + "attn_bwd", sv["qp"], sv["kp"], sv["vp"], sinks, dattn)
    dqkv = rope_bwd(n + "rope_bwd", dqp, dkc, dkp, dvc, dvp, cos, sin)
    dhq, dhf, dhi, dlb = hgrn_bwd(n + "hgrn_bwd", sv["proj"], lb_logits, l, sv["states"], do)
    dproj = jnp.concatenate([dqkv, dhq, dhf, dhi, dhg], axis=1)
    dh1 = mm_col_t(n + "in_proj_dx", dproj, wts["w_in"])
    dx, d_pre_mix = norm_bwd(n + "pre_mix_bwd", sv["x"], gains["pre_mix_gain"], dh1, dx1)
    big = dict(w_out=mm_wg_row(n + "out_proj_dw", sv["cat"], dm), w_in=mm_wg_col(n + "in_proj_dw", sv["h1"], dproj))
    small = dict(pre_mix_gain=d_pre_mix, post_mix_gain=d_post_mix, pre_ffn_gain=d_pre_ffn, attn_out_gain=d_attn_gain,
                 hgrn_out_gain=d_hgrn_gain, hgrn_lb_logits=dlb, attn_sinks=dsinks)
    return dx, big, small


def _layer_bwd(l, dx3, sv, wts, gains, cos, sin, sinks, lb_logits):
    st, early, small_a = _layer_bwd_ffn(l, dx3, sv, wts, gains)
    dx, late, small_b = _layer_bwd_mix(l, st, sv, wts, gains, cos, sin, sinks, lb_logits)
    return dx, {**early, **late}, {**small_a, **small_b}


def _reduce_start(tag, names, big, place):
    grads = [big[nm] for nm in names]
    got = reduce_to_sibling(tag + "_reduce_to_sibling", grads)
    parts, slots = add_halves(tag + "_add", place, grads, got)
    return chips_start(tag + "_chips_start", parts, slots)


def _reduce_begin(tag, names, big):
    return sibling_start(tag + "_sibling_start", [big[nm] for nm in names])


def _reduce_chips(tag, names, begun, place, after):
    send, recv, grads, lands, _ = begun
    grads, got = sibling_wait(tag + "_sibling_wait", send, recv, grads, lands, after)
    parts, slots = add_halves(tag + "_add", place, grads, got)
    return chips_start(tag + "_chips_start", parts, slots)


def _reduce_finish(tag, names, started, place, after):
    send, recv, parts, slots, _ = started
    slots = chips_wait(tag + "_chips_wait", send, recv, parts, slots, after)
    bufs = sum_chips(tag + "_sum", place, slots)
    return dict(zip(names, share_with_sibling(tag + "_share_with_sibling", bufs)))


def _with_token(gains, name, token):
    out = dict(gains)
    out[name] = gains[name] + token[0, 0]
    return out


def kernel(x, p, positions, w_in, attn_sinks, hgrn_lb_logits, attn_out_gain, hgrn_out_gain, w_out, pre_mix_gain, post_mix_gain, pre_ffn_gain, post_ffn_gain, w_ffn_gate, w_ffn_up, w_ffn_down, ple_gain, w_ple_gate, w_ple_proj, loss_target, m_w_in, m_attn_sinks, m_hgrn_lb_logits, m_attn_out_gain, m_hgrn_out_gain, m_w_out, m_pre_mix_gain, m_post_mix_gain, m_pre_ffn_gain, m_post_ffn_gain, m_w_ffn_gate, m_w_ffn_up, m_w_ffn_down, m_ple_gain, m_w_ple_gate, m_w_ple_proj, v_w_in, v_attn_sinks, v_hgrn_lb_logits, v_attn_out_gain, v_hgrn_out_gain, v_w_out, v_pre_mix_gain, v_post_mix_gain, v_pre_ffn_gain, v_post_ffn_gain, v_w_ffn_gate, v_w_ffn_up, v_w_ffn_down, v_ple_gain, v_w_ple_gate, v_w_ple_proj):
    given = dict(locals())
    depth = 2
    place = jnp.stack([lax.axis_index("c"), 2 * lax.axis_index("x") + lax.axis_index("y")]).astype(jnp.int32)
    xs = x[0]
    tgt = loss_target[0]
    pos_col = positions.reshape(-1, 1)
    half = 32
    inv_freq = ROPE_THETA ** (-jnp.arange(half, dtype=F32) / half)
    inv_freq = jnp.tile(inv_freq, 4).reshape(1, LANES)
    gains = [{nm: given[nm][l:l + 1] for nm, _ in SMALL[:7]} for l in range(depth)]
    names = [nm for nm, *_ in BIG]
    first, others = names[:1], names[1:]

    def specs(nms):
        return [BIG_BY_NAME[nm] for nm in nms]

    def start_gather(tag, l, nms, after):
        return gather_start(tag + "_gather_start", specs(nms),
                            cast_to_slots(tag + "_cast", place, [given[nm] for nm in nms], l, after))

    def finish_gather(tag, nms, started, after):
        bufs = gather_wait(tag + "_gather_wait", specs(nms), started[0], started[1], started[2], after)
        return dict(zip(nms, gather_pass(tag + "_gather_pass", specs(nms), bufs)))

    g0a = start_gather("l0a", 0, first, place)
    g0b = start_gather("l0b", 0, others, g0a[3])
    started = {}

    def rest_of_layer0(after):
        got = finish_gather("l0b", others, g0b, after)
        started["l1"] = start_gather("l1", 1, names, got["w_out"])
        return got, started["l1"][3]

    cos, sin = rope_tables("rope_tables", pos_col, inv_freq)
    h1 = pre_norm("l0_pre_mix", xs, gains[0]["pre_mix_gain"])
    w_in0 = finish_gather("l0a", first, g0a, g0b[3])["w_in"]
    (x_mid, h1_next), sv0, wts0 = _layer_fwd(0, xs, h1, p[0, 0], w_in0, rest_of_layer0, gains[0], cos, sin, attn_sinks[0],
                                             hgrn_lb_logits, gains[1]["pre_mix_gain"], None)
    wts1 = finish_gather("l1", names, started["l1"], x_mid)
    (dy, loss_part), sv1, _ = _layer_fwd(1, x_mid, h1_next, p[1, 0], wts1["w_in"], lambda after: (wts1, None), gains[1], cos, sin,
                                         attn_sinks[1], hgrn_lb_logits, None, tgt)

    dx_mid, big1, small1 = _layer_bwd(1, dy, sv1, wts1, gains[1], cos, sin, attn_sinks[1], hgrn_lb_logits)
    chips = {}

    def chips_after(tag, nms, begun):
        def hook(x):
            chips[tag] = _reduce_chips(tag, nms, begun, place, x)
            return chips[tag][4]
        return hook

    b1 = _reduce_begin("l1", names, big1)
    st0, early0, small0 = _layer_bwd_ffn(0, dx_mid, sv0, wts0, gains[0], after=b1[4], hook=chips_after("l1", names, b1))
    b0e = _reduce_begin("l0e", EARLY, early0)
    dx0, late0, small0b = _layer_bwd_mix(0, st0, sv0, wts0, gains[0], cos, sin, attn_sinks[0], hgrn_lb_logits, after=b0e[4],
                                         hook=chips_after("l0e", EARLY, b0e))
    r1, r0e = chips["l1"], chips["l0e"]
    small0 = {**small0, **small0b}
    r0l = _reduce_start("l0l", LATE, late0, place)
    red1 = _reduce_finish("l1", names, r1, place, r0l[4])
    red0 = _reduce_finish("l0e", EARLY, r0e, place, red1[names[-1]])

    loss = lax.psum(loss_part[0, 0], ("x", "y", "c"))
    grad_x = dx0[None]

    out_big = {}
    for nm in EARLY:
        out_big[nm] = adamw_big("adamw_" + nm, given[nm], given["m_" + nm], given["v_" + nm], red0[nm], red1[nm], BIG_BY_NAME[nm][3])
    red0.update(_reduce_finish("l0l", LATE, r0l, place, out_big[EARLY[-1]][3]))
    for nm in LATE:
        out_big[nm] = adamw_big("adamw_" + nm, given[nm], given["m_" + nm], given["v_" + nm], red0[nm], red1[nm], BIG_BY_NAME[nm][3])

    widths = {nm: given[nm].shape[1] for nm, _ in SMALL}
    small_g = {nm: jnp.concatenate([small0[nm][:, :widths[nm]] if nm != "attn_sinks" else small0[nm][:, :LANES],
                                    small1[nm][:, :widths[nm]] if nm != "attn_sinks" else small1[nm][:, :LANES]], axis=0)
               for nm, _ in SMALL}
    g_sum = allreduce_small("allreduce_small", _pack_small(small_g))
    sm = adamw_small("adamw_small", _pack_small({nm: given[nm] for nm, _ in SMALL}),
                     _pack_small({nm: given["m_" + nm] for nm, _ in SMALL}),
                     _pack_small({nm: given["v_" + nm] for nm, _ in SMALL}), g_sum)
    out_small = [_unpack_small(a, widths) for a in sm]

    order = ["w_in", "attn_sinks", "hgrn_lb_logits", "attn_out_gain", "hgrn_out_gain", "w_out", "pre_mix_gain", "post_mix_gain",
             "pre_ffn_gain", "post_ffn_gain", "w_ffn_gate", "w_ffn_up", "w_ffn_down", "ple_gain", "w_ple_gate", "w_ple_proj"]
    res = [loss, grad_x]
    for k in range(4):
        for nm in order:
            res.append(out_big[nm][k] if nm in out_big else out_small[k][nm])
    return tuple(res)
```

```python
import functools

import jax
import jax.numpy as jnp
from jax import lax
from jax.experimental import pallas as pl
from jax.experimental.pallas import tpu as pltpu

F32, BF16 = jnp.float32, jnp.bfloat16
SDS = jax.ShapeDtypeStruct
MESH = pl.DeviceIdType.MESH

D_MODEL = 2048
ATTN_WIDTH = 1024
HGRN_WIDTH = 1024
KV_WIDTH = 256
N_Q_HEADS = 16
N_KV_HEADS = 4
Q_PER_KV = 4
WINDOW = 128
MASK_VALUE = -1e30
ROPE_THETA = 10000.0
HGRN_HEADS = 8
HGRN_CHUNK = 16
D_FF = 5632
D_PLE = 256
RMS_EPS = 1e-6
LANES = 128
N_CHIPS = 4
COL_Q, COL_K, COL_V, COL_HQ, COL_HF, COL_HI, COL_HG = 0, 8, 10, 12, 20, 28, 36

ADAM_LR, ADAM_B1, ADAM_B2, ADAM_EPS, ADAM_WD, ADAM_STEP = 0.001, 0.9, 0.999, 1e-08, 0.01, 10

VMEM_LIMIT = 56 * 1024 * 1024
ROW_TILE = 256
ROW_TILE_WIDE = 512

_NN = (((1,), (0,)), ((), ()))
_NT = (((1,), (1,)), ((), ()))
_TN = (((0,), (0,)), ((), ()))


def _pc(body, *, name, out_shape, in_specs, out_specs, grid=(), scratch=(), sem=None, grid_spec=None, **kw):
    params = dict(vmem_limit_bytes=VMEM_LIMIT)
    if sem is not None:
        params["dimension_semantics"] = sem
    if grid_spec is not None:
        return pl.pallas_call(body, name=name, out_shape=out_shape, grid_spec=grid_spec,
                              compiler_params=pltpu.CompilerParams(**params), **kw)
    return pl.pallas_call(body, name=name, out_shape=out_shape, grid=grid, in_specs=in_specs, out_specs=out_specs,
                          scratch_shapes=list(scratch), compiler_params=pltpu.CompilerParams(**params), **kw)


def _sigmoid(x):
    return 1.0 / (1.0 + jnp.exp(-x))


def _rstd(x):
    return lax.rsqrt(jnp.mean(x * x, axis=-1, keepdims=True) + RMS_EPS)


def _rows(t, w, col=0):
    return pl.BlockSpec((t, w), lambda i, col=col: (i, col))


def _fixed(shape):
    return pl.BlockSpec(shape, lambda *_: (0,) * len(shape))


def _mm(name, a, b, *, dims, grid, a_spec, b_spec, o_spec, out_shape, parts=1, add=None, add_spec=None, after=None):
    def body(*refs):
        a_ref, b_ref, o_ref = refs[0], refs[1], refs[-1]
        if parts == 1:
            r = lax.dot_general(a_ref[...].astype(BF16), b_ref[...].astype(BF16), dims, preferred_element_type=F32)
        else:
            w = a_ref.shape[1] // parts
            r = None
            for j in range(parts):
                t = lax.dot_general(a_ref[:, j * w:(j + 1) * w].astype(BF16), b_ref[j].astype(BF16), dims,
                                    preferred_element_type=F32)
                r = t if r is None else r + t
        if add is not None:
            r = r + refs[2][...]
        o_ref[...] = r.astype(o_ref.dtype)

    ins = [a, b] + ([] if add is None else [add]) + ([] if after is None else [after])
    specs = [a_spec, b_spec] + ([] if add is None else [add_spec]) + ([] if after is None else [pl.BlockSpec(memory_space=pl.ANY)])
    return _pc(body, name=name, grid=grid, in_specs=specs, out_specs=o_spec, out_shape=out_shape,
               sem=("parallel",) * len(grid))(*ins)


def _tile(n, t):
    if n <= t:
        return n
    while n % t:
        t //= 2
    assert t % 8 == 0
    return t


def mm_col(name, a, wg, out_dtype=F32):
    s, k = a.shape
    _, _, n = wg.shape
    tm = _tile(s, 512)
    return _mm(name, a, wg, dims=_NN, grid=(N_CHIPS, s // tm),
               a_spec=pl.BlockSpec((tm, k), lambda j, i: (i, 0)),
               b_spec=pl.BlockSpec((None, k, n), lambda j, i: (j, 0, 0)),
               o_spec=pl.BlockSpec((tm, n), lambda j, i: (i, j)),
               out_shape=SDS((s, N_CHIPS * n), out_dtype))


def mm_row(name, a, wg, out_dtype=F32):
    s, _ = a.shape
    _, r, n = wg.shape
    tm = _tile(s, 512)
    tn = _tile(n, 1024 if r > 512 else 2048)
    return _mm(name, a, wg, dims=_NN, grid=(n // tn, s // tm), parts=N_CHIPS,
               a_spec=pl.BlockSpec((tm, N_CHIPS * r), lambda j, i: (i, 0)),
               b_spec=pl.BlockSpec((N_CHIPS, r, tn), lambda j, i: (0, 0, j)),
               o_spec=pl.BlockSpec((tm, tn), lambda j, i: (i, j)),
               out_shape=SDS((s, n), out_dtype))


def mm_col_t(name, dy, wg, add=None, out_dtype=F32, after=None):
    s, _ = dy.shape
    _, k, n = wg.shape
    tm = _tile(s, 512)
    tk = _tile(k, 1024)
    return _mm(name, dy, wg, dims=_NT, grid=(k // tk, s // tm), parts=N_CHIPS,
               a_spec=pl.BlockSpec((tm, N_CHIPS * n), lambda j, i: (i, 0)),
               b_spec=pl.BlockSpec((N_CHIPS, tk, n), lambda j, i: (0, j, 0)),
               o_spec=pl.BlockSpec((tm, tk), lambda j, i: (i, j)),
               add=add, add_spec=pl.BlockSpec((tm, tk), lambda j, i: (i, j)),
               out_shape=SDS((s, k), out_dtype), after=after)


def mm_row_t(name, dy, wg, out_dtype=F32):
    s, n = dy.shape
    _, r, _ = wg.shape
    tm = _tile(s, 512)
    return _mm(name, dy, wg, dims=_NT, grid=(N_CHIPS, s // tm),
               a_spec=pl.BlockSpec((tm, n), lambda j, i: (i, 0)),
               b_spec=pl.BlockSpec((None, r, n), lambda j, i: (j, 0, 0)),
               o_spec=pl.BlockSpec((tm, r), lambda j, i: (i, j)),
               out_shape=SDS((s, N_CHIPS * r), out_dtype))


def norm_bwd_pair(name, x_a, gain_a, dy, dres, x_b, gain_b):
    s, d = x_a.shape
    t = _tile(s, ROW_TILE)

    def one(xv, g, dyv):
        r = _rstd(xv)
        xh = xv * r
        dyg = dyv * g
        return r * (dyg - xh * jnp.mean(dyg * xh, axis=-1, keepdims=True)), jnp.sum(dyv * xh, axis=0, keepdims=True)

    def body(xa_ref, ga_ref, dy_ref, r_ref, xb_ref, gb_ref, dx_ref, db_ref, dga_ref, dgb_ref):
        dx, pa = one(xa_ref[...], ga_ref[...], dy_ref[...])
        dx = dx + r_ref[...]
        dx_ref[...] = dx
        db, pb = one(xb_ref[...], gb_ref[...], dx)
        db_ref[...] = db.astype(BF16)

        @pl.when(pl.program_id(0) == 0)
        def _():
            dga_ref[...] = pa
            dgb_ref[...] = pb

        @pl.when(pl.program_id(0) > 0)
        def _():
            dga_ref[...] += pa
            dgb_ref[...] += pb

    row, gain = _rows(t, d), _fixed((1, d))
    return _pc(body, name=name, grid=(s // t,), in_specs=[row, gain, row, row, row, gain], out_specs=[row, row, gain, gain],
               out_shape=[SDS((s, d), F32), SDS((s, d), BF16), SDS((1, d), F32), SDS((1, d), F32)],
               sem=("arbitrary",))(x_a, gain_a, dy, dres, x_b, gain_b)


def ffn_gate_up(name, h, wg_gate, wg_up):
    s, k = h.shape
    _, _, n = wg_gate.shape
    tm = _tile(s, 512)

    def body(h_ref, wg_ref, wu_ref, g_ref, u_ref, a_ref):
        hv = h_ref[...]
        g = jnp.dot(hv, wg_ref[...], preferred_element_type=F32)
        u = jnp.dot(hv, wu_ref[...], preferred_element_type=F32)
        g_ref[...] = g
        u_ref[...] = u
        a_ref[...] = ((g * _sigmoid(g)) * u).astype(BF16)

    wspec = pl.BlockSpec((None, k, n), lambda j, i: (j, 0, 0))
    ospec = pl.BlockSpec((tm, n), lambda j, i: (i, j))
    return _pc(body, name=name, grid=(N_CHIPS, s // tm), in_specs=[pl.BlockSpec((tm, k), lambda j, i: (i, 0)), wspec, wspec],
               out_specs=[ospec] * 3, out_shape=[SDS((s, N_CHIPS * n), F32)] * 2 + [SDS((s, N_CHIPS * n), BF16)],
               sem=("parallel", "parallel"))(h, wg_gate, wg_up)


def ffn_down_bwd(name, df, wg_down, g, u):
    s, n = df.shape
    _, r, _ = wg_down.shape
    tm = _tile(s, 512)

    def body(df_ref, w_ref, g_ref, u_ref, dg_ref, du_ref):
        da = lax.dot_general(df_ref[...], w_ref[...], _NT, preferred_element_type=F32)
        gv = g_ref[...]
        sg = _sigmoid(gv)
        du_ref[...] = (da * (gv * sg)).astype(BF16)
        dg_ref[...] = ((da * u_ref[...]) * (sg * (1.0 + gv * (1.0 - sg)))).astype(BF16)

    cspec = pl.BlockSpec((tm, r), lambda j, i: (i, j))
    return _pc(body, name=name, grid=(N_CHIPS, s // tm),
               in_specs=[pl.BlockSpec((tm, n), lambda j, i: (i, 0)), pl.BlockSpec((None, r, n), lambda j, i: (j, 0, 0)), cspec, cspec],
               out_specs=[cspec] * 2, out_shape=[SDS((s, N_CHIPS * r), BF16)] * 2,
               sem=("parallel", "parallel"))(df, wg_down, g, u)


def mm_wg_col(name, a, dy):
    s, k = a.shape
    n = dy.shape[1] // N_CHIPS
    tm = _tile(k // 2, 512)
    hb = (k // 2) // tm
    return _mm(name, a, dy, dims=_TN, grid=(N_CHIPS, k // tm),
               a_spec=pl.BlockSpec((s, tm), lambda j, i: (0, i)),
               b_spec=pl.BlockSpec((s, n), lambda j, i: (0, j)),
               o_spec=pl.BlockSpec((None, None, tm, n), lambda j, i: (i // hb, j, i % hb, 0)),
               out_shape=SDS((2, N_CHIPS, k // 2, n), BF16))


def mm_wg_row(name, a, dy):
    s, n = dy.shape
    r = a.shape[1] // N_CHIPS
    tn = _tile(n // 2, 512)
    nb = (n // 2) // tn
    return _mm(name, a, dy, dims=_TN, grid=(N_CHIPS, n // tn),
               a_spec=pl.BlockSpec((s, r), lambda j, i: (0, j)),
               b_spec=pl.BlockSpec((s, tn), lambda j, i: (0, i)),
               o_spec=pl.BlockSpec((None, None, r, tn), lambda j, i: (i // nb, j, 0, i % nb)),
               out_shape=SDS((2, N_CHIPS, r, n // 2), BF16))


def pre_norm(name, x, gain):
    s, d = x.shape
    t = _tile(s, ROW_TILE_WIDE)

    def body(x_ref, g_ref, o_ref):
        xv = x_ref[...]
        o_ref[...] = ((xv * _rstd(xv)) * g_ref[...]).astype(BF16)

    return _pc(body, name=name, grid=(s // t,), in_specs=[_rows(t, d), _fixed((1, d))], out_specs=_rows(t, d),
               out_shape=SDS((s, d), BF16), sem=("parallel",))(x, gain)


def post_pre_norm(name, m, g_post, res, g_pre):
    s, d = m.shape
    t = _tile(s, ROW_TILE_WIDE)

    def body(m_ref, gp_ref, r_ref, gn_ref, x_ref, h_ref):
        mv = m_ref[...]
        xn = r_ref[...] + (mv * _rstd(mv)) * gp_ref[...]
        x_ref[...] = xn
        h_ref[...] = ((xn * _rstd(xn)) * gn_ref[...]).astype(BF16)

    return _pc(body, name=name, grid=(s // t,),
               in_specs=[_rows(t, d), _fixed((1, d)), _rows(t, d), _fixed((1, d))],
               out_specs=[_rows(t, d), _rows(t, d)], out_shape=[SDS((s, d), F32), SDS((s, d), BF16)],
               sem=("parallel",))(m, g_post, res, g_pre)


def _row_dot(a_ref, w_ref):
    r = w_ref.shape[1]
    out = None
    for j in range(N_CHIPS):
        part = jnp.dot(a_ref[:, j * r:(j + 1) * r], w_ref[j], preferred_element_type=F32)
        out = part if out is None else out + part
    return out


def _row_dot_specs(t, a, wg):
    return [_rows(t, a.shape[1]), _fixed(wg.shape)]


def out_proj_post_mix(name, a, wg, g_post, res, g_pre):
    s, d = res.shape
    t = _tile(s, ROW_TILE)

    def body(a_ref, w_ref, gp_ref, r_ref, gn_ref, m_ref, x_ref, h_ref):
        mv = _row_dot(a_ref, w_ref)
        m_ref[...] = mv
        xn = r_ref[...] + (mv * _rstd(mv)) * gp_ref[...]
        x_ref[...] = xn
        h_ref[...] = ((xn * _rstd(xn)) * gn_ref[...]).astype(BF16)

    return _pc(body, name=name, grid=(s // t,),
               in_specs=_row_dot_specs(t, a, wg) + [_fixed((1, d)), _rows(t, d), _fixed((1, d))],
               out_specs=[_rows(t, d)] * 3, out_shape=[SDS((s, d), F32), SDS((s, d), F32), SDS((s, d), BF16)],
               sem=("parallel",))(a, wg, g_post, res, g_pre)


def ple_gate_fwd_mid(name, a, wg, pp, x2, g_next):
    s, d = x2.shape
    t = _tile(s, ROW_TILE)

    def body(a_ref, w_ref, p_ref, x_ref, g_ref, z_ref, xo_ref, h_ref):
        z = _row_dot(a_ref, w_ref)
        z_ref[...] = z
        xn = x_ref[...] + p_ref[...] * _sigmoid(z)
        xo_ref[...] = xn
        h_ref[...] = ((xn * _rstd(xn)) * g_ref[...]).astype(BF16)

    return _pc(body, name=name, grid=(s // t,),
               in_specs=_row_dot_specs(t, a, wg) + [_rows(t, d), _rows(t, d), _fixed((1, d))],
               out_specs=[_rows(t, d)] * 3, out_shape=[SDS((s, d), F32), SDS((s, d), F32), SDS((s, d), BF16)],
               sem=("parallel",))(a, wg, pp, x2, g_next)


def ple_gate_fwd_loss(name, a, wg, pp, x2, target):
    s, d = x2.shape
    t = _tile(s, ROW_TILE)

    def body(a_ref, w_ref, p_ref, x_ref, t_ref, dy_ref, dpp_ref, dz_ref, l_ref):
        gate = _sigmoid(_row_dot(a_ref, w_ref))
        pv = p_ref[...]
        err = (x_ref[...] + pv * gate) - t_ref[...]
        dy = err * (1.0 / d)
        dy_ref[...] = dy
        dpp_ref[...] = (dy * gate).astype(BF16)
        dz_ref[...] = ((dy * pv) * (gate * (1.0 - gate))).astype(BF16)
        part = jnp.sum(jnp.sum(err * err, axis=-1, keepdims=True), axis=0, keepdims=True) * (0.5 / d)

        @pl.when(pl.program_id(0) == 0)
        def _():
            l_ref[...] = part

        @pl.when(pl.program_id(0) > 0)
        def _():
            l_ref[...] += part

    return _pc(body, name=name, grid=(s // t,),
               in_specs=_row_dot_specs(t, a, wg) + [_rows(t, d), _rows(t, d), _rows(t, d)],
               out_specs=[_rows(t, d), _rows(t, d), _rows(t, d), _fixed((1, 1))],
               out_shape=[SDS((s, d), F32), SDS((s, d), BF16), SDS((s, d), BF16), SDS((1, 1), F32)],
               sem=("arbitrary",))(a, wg, pp, x2, target)


ANY_SPEC = pl.BlockSpec(memory_space=pl.ANY)


def ple_bwd(name, dx3, z, pp, after):
    s, d = z.shape
    t = _tile(s, ROW_TILE_WIDE)

    def body(d_ref, z_ref, p_ref, after_ref, dpp_ref, dz_ref):
        gate = _sigmoid(z_ref[...])
        dv = d_ref[...]
        dpp_ref[...] = (dv * gate).astype(BF16)
        dz_ref[...] = ((dv * p_ref[...]) * (gate * (1.0 - gate))).astype(BF16)

    return _pc(body, name=name, grid=(s // t,), in_specs=[_rows(t, d)] * 3 + [ANY_SPEC], out_specs=[_rows(t, d)] * 2,
               out_shape=[SDS((s, d), BF16)] * 2, sem=("parallel",))(dx3, z, pp, after)


def norm_bwd(name, xin, gain, dy, dres=None, out_dtype=F32):
    s, d = xin.shape
    t = _tile(s, ROW_TILE_WIDE)

    def body(*refs):
        if dres is None:
            x_ref, g_ref, dy_ref, dx_ref, dg_ref = refs
            r_ref = None
        else:
            x_ref, g_ref, dy_ref, r_ref, dx_ref, dg_ref = refs
        xv = x_ref[...]
        r = _rstd(xv)
        xh = xv * r
        dyv = dy_ref[...].astype(F32)
        dyg = dyv * g_ref[...]
        c = jnp.mean(dyg * xh, axis=-1, keepdims=True)
        dx = r * (dyg - xh * c)
        if r_ref is not None:
            dx = dx + r_ref[...]
        dx_ref[...] = dx.astype(out_dtype)
        part = jnp.sum(dyv * xh, axis=0, keepdims=True)

        @pl.when(pl.program_id(0) == 0)
        def _():
            dg_ref[...] = part

        @pl.when(pl.program_id(0) > 0)
        def _():
            dg_ref[...] += part

    ins = [xin, gain, dy] + ([] if dres is None else [dres])
    specs = [_rows(t, d), _fixed((1, d)), _rows(t, d)] + ([] if dres is None else [_rows(t, d)])
    return _pc(body, name=name, grid=(s // t,), in_specs=specs, out_specs=[_rows(t, d), _fixed((1, d))],
               out_shape=[SDS((s, d), out_dtype), SDS((1, d), F32)], sem=("arbitrary",))(*ins)


def _lane(shape):
    return lax.broadcasted_iota(jnp.int32, shape, 1)


def _swap_halves(x):
    lo = (_lane(x.shape) % 64) < 32
    return jnp.where(lo, pltpu.roll(x, 96, 1), pltpu.roll(x, 32, 1))


def rope_tables(name, pos_col, inv_freq):
    s = pos_col.shape[0]
    t = _tile(s, ROW_TILE_WIDE)

    def body(p_ref, f_ref, c_ref, s_ref):
        ang = p_ref[...].astype(F32) * f_ref[...]
        lo = (_lane(ang.shape) % 64) < 32
        c_ref[...] = jnp.cos(ang)
        sn = jnp.sin(ang)
        s_ref[...] = jnp.where(lo, -sn, sn)

    return _pc(body, name=name, grid=(s // t,), in_specs=[_rows(t, 1), _fixed((1, LANES))],
               out_specs=[_rows(t, LANES)] * 2, out_shape=[SDS((s, LANES), F32)] * 2, sem=("parallel",))(pos_col, inv_freq)


def _pad_heads(chunk, lo_mask):
    zero = jnp.zeros_like(chunk)
    return jnp.where(lo_mask, chunk, zero), jnp.where(lo_mask, pltpu.roll(chunk, 64, 1), zero)


def rope_qkv(name, proj, cos, sin):
    s = proj.shape[0]
    t = _tile(s, ROW_TILE_WIDE)

    def body(q_ref, kv_ref, c_ref, s_ref, qp_ref, kp_ref, vp_ref):
        cs, sn = c_ref[...], s_ref[...]
        lo_mask = _lane(cs.shape) < 64

        def rot(x):
            return x * cs + _swap_halves(x) * sn

        for j in range(ATTN_WIDTH // LANES):
            a, b = _pad_heads(rot(q_ref[:, j * LANES:(j + 1) * LANES]), lo_mask)
            qp_ref[:, (2 * j) * LANES:(2 * j + 1) * LANES] = a.astype(BF16)
            qp_ref[:, (2 * j + 1) * LANES:(2 * j + 2) * LANES] = b.astype(BF16)
        for j in range(KV_WIDTH // LANES):
            a, b = _pad_heads(rot(kv_ref[:, j * LANES:(j + 1) * LANES]), lo_mask)
            kp_ref[:, (2 * j) * LANES:(2 * j + 1) * LANES] = a.astype(BF16)
            kp_ref[:, (2 * j + 1) * LANES:(2 * j + 2) * LANES] = b.astype(BF16)
            a, b = _pad_heads(kv_ref[:, KV_WIDTH + j * LANES:KV_WIDTH + (j + 1) * LANES], lo_mask)
            vp_ref[:, (2 * j) * LANES:(2 * j + 1) * LANES] = a.astype(BF16)
            vp_ref[:, (2 * j + 1) * LANES:(2 * j + 2) * LANES] = b.astype(BF16)

    return _pc(body, name=name, grid=(s // t,),
               in_specs=[_rows(t, ATTN_WIDTH, 0), _rows(t, 2 * KV_WIDTH, 2), _rows(t, LANES), _rows(t, LANES)],
               out_specs=[_rows(t, N_Q_HEADS * LANES), _rows(t, N_KV_HEADS * LANES), _rows(t, N_KV_HEADS * LANES)],
               out_shape=[SDS((s, N_Q_HEADS * LANES), BF16), SDS((s, N_KV_HEADS * LANES), BF16),
                          SDS((s, N_KV_HEADS * LANES), BF16)],
               sem=("parallel",))(proj, proj, cos, sin)


def _attn_mask(n):
    L = WINDOW
    qi = lax.broadcasted_iota(jnp.int32, (L, 2 * L), 0) + L
    ki = lax.broadcasted_iota(jnp.int32, (L, 2 * L), 1)
    rel = qi - ki
    return (rel >= 0) & (rel < WINDOW) & ((n > 0) | (ki >= L))


def _attn_scores(qh, kk, valid):
    sc = lax.dot_general(qh, kk, _NT, preferred_element_type=F32) * 0.125
    return jnp.where(valid, sc, MASK_VALUE)


def _attn_softmax(sc, sink):
    m = jnp.maximum(jnp.max(sc, axis=-1, keepdims=True), sink)
    e = jnp.exp(sc - m)
    es = jnp.exp(sink - m)
    den = jnp.sum(e, axis=-1, keepdims=True) + es
    return e / den, es / den


def _attn_specs(s):
    L = WINDOW
    cur = lambda n: (n, 0)
    prev = lambda n: (jnp.maximum(n - 1, 0), 0)
    kvw = N_KV_HEADS * LANES
    return [pl.BlockSpec((L, N_Q_HEADS * LANES), cur), pl.BlockSpec((L, kvw), cur), pl.BlockSpec((L, kvw), prev),
            pl.BlockSpec((L, kvw), cur), pl.BlockSpec((L, kvw), prev), pl.BlockSpec(memory_space=pltpu.SMEM)]


def attn_fwd(name, qp, kp, vp, sinks):
    s = qp.shape[0]
    L = WINDOW

    def body(q_ref, kc_ref, kp_ref, vc_ref, vp_ref, sk_ref, o_ref):
        valid = _attn_mask(pl.program_id(0))
        kks, vvs = [], []
        for kvh in range(N_KV_HEADS):
            cols = slice(kvh * LANES, (kvh + 1) * LANES)
            kks.append(jnp.concatenate([kp_ref[:, cols], kc_ref[:, cols]], axis=0))
            vvs.append(jnp.concatenate([vp_ref[:, cols], vc_ref[:, cols]], axis=0))
        scs = [_attn_scores(q_ref[:, h * LANES:(h + 1) * LANES], kks[h // Q_PER_KV], valid) for h in range(N_Q_HEADS)]
        pbs = [_attn_softmax(scs[h], sk_ref[h])[0].astype(BF16) for h in range(N_Q_HEADS)]
        outs = [jnp.dot(pbs[h], vvs[h // Q_PER_KV], preferred_element_type=F32) for h in range(N_Q_HEADS)]
        for j in range(ATTN_WIDTH // LANES):
            o_ref[:, j * LANES:(j + 1) * LANES] = outs[2 * j] + pltpu.roll(outs[2 * j + 1], 64, 1)

    return _pc(body, name=name, grid=(s // L,), in_specs=_attn_specs(s),
               out_specs=pl.BlockSpec((L, ATTN_WIDTH), lambda n: (n, 0)),
               out_shape=SDS((s, ATTN_WIDTH), F32), sem=("parallel",))(qp, kp, kp, vp, vp, sinks)


def attn_bwd(name, qp, kp, vp, sinks, dattn):
    s = qp.shape[0]
    L = WINDOW
    kvw = N_KV_HEADS * LANES

    def body(q_ref, kc_ref, kp_ref, vc_ref, vp_ref, sk_ref, do_ref, dq_ref, dkc_ref, dkp_ref, dvc_ref, dvp_ref, ds_ref):
        n = pl.program_id(0)
        valid = _attn_mask(n)
        lo_mask = _lane((L, LANES)) < 64
        lane1 = _lane((1, LANES))
        dsink = jnp.zeros((1, LANES), F32)
        heads = range(N_Q_HEADS)
        kks, vvs = [], []
        for kvh in range(N_KV_HEADS):
            cols = slice(kvh * LANES, (kvh + 1) * LANES)
            kks.append(jnp.concatenate([kp_ref[:, cols], kc_ref[:, cols]], axis=0))
            vvs.append(jnp.concatenate([vp_ref[:, cols], vc_ref[:, cols]], axis=0))
        qs, dos, scs, dps = [], [], [], []
        for h in heads:
            qs.append(q_ref[:, h * LANES:(h + 1) * LANES])
            chunk = do_ref[:, (h // 2) * LANES:(h // 2 + 1) * LANES]
            if h % 2:
                chunk = pltpu.roll(chunk, 64, 1)
            dos.append(jnp.where(lo_mask, chunk, 0.0).astype(BF16))
            scs.append(_attn_scores(qs[h], kks[h // Q_PER_KV], valid))
            dps.append(lax.dot_general(dos[h], vvs[h // Q_PER_KV], _NT, preferred_element_type=F32))
        pbs, dsbs = [], []
        for h in heads:
            p, ps = _attn_softmax(scs[h], sk_ref[h])
            delta = jnp.sum(p * dps[h], axis=-1, keepdims=True)
            dsbs.append(((p * (dps[h] - delta)) * 0.125).astype(BF16))
            pbs.append(p.astype(BF16))
            dsink = dsink + jnp.where(lane1 == h, -jnp.sum(ps * delta, axis=0, keepdims=True), 0.0)
        for kvh in range(N_KV_HEADS):
            cols = slice(kvh * LANES, (kvh + 1) * LANES)
            dkk = jnp.zeros((2 * L, LANES), F32)
            dvv = jnp.zeros((2 * L, LANES), F32)
            for h in range(kvh * Q_PER_KV, (kvh + 1) * Q_PER_KV):
                dq_ref[:, h * LANES:(h + 1) * LANES] = jnp.dot(dsbs[h], kks[kvh], preferred_element_type=F32)
                dkk = dkk + lax.dot_general(dsbs[h], qs[h], _TN, preferred_element_type=F32)
                dvv = dvv + lax.dot_general(pbs[h], dos[h], _TN, preferred_element_type=F32)
            dkp_ref[:, cols] = dkk[:L]
            dkc_ref[:, cols] = dkk[L:]
            dvp_ref[:, cols] = dvv[:L]
            dvc_ref[:, cols] = dvv[L:]

        @pl.when(n == 0)
        def _():
            ds_ref[...] = dsink

        @pl.when(n > 0)
        def _():
            ds_ref[...] += dsink

    blk = lambda w: pl.BlockSpec((L, w), lambda n: (n, 0))
    return _pc(body, name=name, grid=(s // L,), in_specs=_attn_specs(s) + [blk(ATTN_WIDTH)],
               out_specs=[blk(N_Q_HEADS * LANES), blk(kvw), blk(kvw), blk(kvw), blk(kvw), _fixed((1, LANES))],
               out_shape=[SDS((s, N_Q_HEADS * LANES), F32)] + [SDS((s, kvw), F32)] * 4 + [SDS((1, LANES), F32)],
               sem=("arbitrary",))(qp, kp, kp, vp, vp, sinks, dattn)


def rope_bwd(name, dqp, dkc, dkp, dvc, dvp, cos, sin):
    s = dqp.shape[0]
    L = WINDOW
    nb = s // L
    kvw = N_KV_HEADS * LANES

    def body(dq_ref, dkc_ref, dkp_ref, dvc_ref, dvp_ref, c_ref, s_ref, o_ref):
        cs, sn = c_ref[...], s_ref[...]
        more = (pl.program_id(0) < nb - 1).astype(F32)

        def unrot(x):
            return x * cs - _swap_halves(x) * sn

        def compact(ref, j, nxt=None):
            a = ref[:, (2 * j) * LANES:(2 * j + 1) * LANES]
            b = ref[:, (2 * j + 1) * LANES:(2 * j + 2) * LANES]
            if nxt is not None:
                a = a + more * nxt[:, (2 * j) * LANES:(2 * j + 1) * LANES]
                b = b + more * nxt[:, (2 * j + 1) * LANES:(2 * j + 2) * LANES]
            return a + pltpu.roll(b, 64, 1)

        for j in range(ATTN_WIDTH // LANES):
            o_ref[:, j * LANES:(j + 1) * LANES] = unrot(compact(dq_ref, j)).astype(BF16)
        for j in range(KV_WIDTH // LANES):
            o_ref[:, (COL_K + j) * LANES:(COL_K + j + 1) * LANES] = unrot(compact(dkc_ref, j, dkp_ref)).astype(BF16)
            o_ref[:, (COL_V + j) * LANES:(COL_V + j + 1) * LANES] = compact(dvc_ref, j, dvp_ref).astype(BF16)

    cur = lambda n: (n, 0)
    nxt = lambda n: (jnp.minimum(n + 1, nb - 1), 0)
    return _pc(body, name=name, grid=(nb,),
               in_specs=[pl.BlockSpec((L, N_Q_HEADS * LANES), cur), pl.BlockSpec((L, kvw), cur), pl.BlockSpec((L, kvw), nxt),
                         pl.BlockSpec((L, kvw), cur), pl.BlockSpec((L, kvw), nxt), pl.BlockSpec((L, LANES), cur),
                         pl.BlockSpec((L, LANES), cur)],
               out_specs=pl.BlockSpec((L, COL_HQ * LANES), cur), out_shape=SDS((s, COL_HQ * LANES), BF16),
               sem=("parallel",))(dqp, dkc, dkp, dvc, dvp, cos, sin)


def _split3(x):
    a = x.astype(BF16)
    r = x - a.astype(F32)
    b = r.astype(BF16)
    c = (r - b.astype(F32)).astype(BF16)
    return a, b, c


def _chunk_sum(x, upper):
    t = x.shape[0]
    ri = lax.broadcasted_iota(jnp.int32, (t, t), 0)
    ci = lax.broadcasted_iota(jnp.int32, (t, t), 1)
    same = (ri // HGRN_CHUNK) == (ci // HGRN_CHUNK)
    tri = (ci >= ri) if upper else (ci <= ri)
    m = jnp.where(same & tri, 1.0, 0.0).astype(BF16)
    out = None
    for part in _split3(x):
        y = jnp.dot(m, part, preferred_element_type=F32)
        out = y if out is None else out + y
    return out


def _lower_bound(l_ref, layer):
    lv = l_ref[...]
    e = jnp.exp(lv - jnp.max(lv, axis=0, keepdims=True))
    sm = e / jnp.sum(e, axis=0, keepdims=True)
    s0 = sm[0:1]
    return (s0 - s0) if layer == 0 else ((s0 + sm[1:2]) - s0)


def _hgrn_gates(hq_ref, hf_ref, lb):
    z = hf_ref[...]
    sg = _sigmoid(z)
    f = lb + (1.0 - lb) * sg
    kin = (1.0 - lb) * _sigmoid(-z)
    hq = hq_ref[...]
    sq = _sigmoid(hq)
    return sg, f, kin, hq, sq


def _shift_down(x, d):
    return x if d == 0 else pltpu.roll(x, d, 0)


def _shift_up(x, d):
    return x if d == 0 else pltpu.roll(x, x.shape[0] - d, 0)


CHUNKS_PER_BLOCK = LANES // HGRN_CHUNK


def _chunk_iotas():
    shape = (HGRN_CHUNK, LANES)
    return lax.broadcasted_iota(jnp.int32, shape, 0), lax.broadcasted_iota(jnp.int32, shape, 1)


def _chunk_rows(block, chunk):
    start = block * LANES + chunk * HGRN_CHUNK
    return slice(start, start + HGRN_CHUNK)


HGRN_HEADS_PER_STEP = 4
HGRN_STEP_WIDTH = HGRN_HEADS_PER_STEP * LANES


def _hgrn_specs(t, rev, nt):
    row = (lambda h, i: nt - 1 - i) if rev else (lambda h, i: i)
    col = lambda base: pl.BlockSpec((t, HGRN_STEP_WIDTH),
                                    lambda h, i, base=base: (row(h, i), base // HGRN_HEADS_PER_STEP + h))
    return col, row


def _head_views(refs, hh):
    return [r.at[:, pl.ds(hh * LANES, LANES)] for r in refs]


def hgrn_fwd(name, proj, lb_logits, layer):
    s = proj.shape[0]
    t = _tile(s, ROW_TILE)
    nt = s // t
    nc = t // HGRN_CHUNK
    col, row = _hgrn_specs(t, False, nt)

    def body(hq_ref, hf_ref, hi_ref, l_ref, o_ref, st_ref, state):
        @pl.when(pl.program_id(1) == 0)
        def _():
            state[...] = jnp.zeros_like(state)

        for hh in range(HGRN_HEADS_PER_STEP):
            head(*_head_views((hq_ref, hf_ref, hi_ref, l_ref, o_ref), hh), st_ref.at[:, hh], state.at[hh])

    def head(hq_ref, hf_ref, hi_ref, l_ref, o_ref, st_ref, state):
        lb = _lower_bound(l_ref, layer)
        sg, f, kin, hq, sq = _hgrn_gates(hq_ref, hf_ref, lb)
        q = hq * sq
        vb = hi_ref[...].astype(BF16)
        b = _chunk_sum(jnp.log(f), False)
        qe = (q * jnp.exp(b)).astype(BF16)
        trow, lane = _chunk_iotas()
        chunks = [(j, cc) for j in range(t // LANES) for cc in range(CHUNKS_PER_BLOCK)]
        decay, update = [], []
        for j, cc in chunks:
            rs = _chunk_rows(j, cc)
            bc = b[rs]
            bl = bc[HGRN_CHUNK - 1:HGRN_CHUNK, :]
            ke = (kin[rs] * jnp.exp(bl - bc)).astype(BF16)
            decay.append(jnp.exp(bl))
            update.append(lax.dot_general(vb[rs], ke, _TN, preferred_element_type=F32))
        st = state[...]
        for c in range(nc):
            st_ref[c] = st
            st = st * decay[c] + update[c]
        state[...] = st
        o_inter = [lax.dot_general(qe[c * HGRN_CHUNK:(c + 1) * HGRN_CHUNK], st_ref[c].astype(BF16), _NT,
                                   preferred_element_type=F32) for c in range(nc)]
        for j in range(t // LANES):
            blk = slice(j * LANES, (j + 1) * LANES)
            rows = []
            for cc in range(CHUNKS_PER_BLOCK):
                rs = _chunk_rows(j, cc)
                bc, qc, kc = b[rs], q[rs], kin[rs]
                here = trow + cc * HGRN_CHUNK
                am = jnp.where(lane == here, jnp.sum(qc * kc, axis=-1, keepdims=True), 0.0)
                for d in range(1, HGRN_CHUNK):
                    e = jnp.exp(jnp.where(trow >= d, bc - _shift_down(bc, d), MASK_VALUE))
                    a = jnp.sum((qc * _shift_down(kc, d)) * e, axis=-1, keepdims=True)
                    am = jnp.where(lane == here - d, a, am)
                rows.append(am)
            o_intra = jnp.dot(jnp.concatenate(rows, axis=0).astype(BF16), vb[blk], preferred_element_type=F32)
            for cc in range(CHUNKS_PER_BLOCK):
                rs = _chunk_rows(j, cc)
                o_ref[rs, :] = o_intra[cc * HGRN_CHUNK:(cc + 1) * HGRN_CHUNK] + o_inter[j * CHUNKS_PER_BLOCK + cc]

    hp = HGRN_HEADS_PER_STEP
    return _pc(body, name=name, grid=(HGRN_HEADS // hp, nt),
               in_specs=[col(COL_HQ), col(COL_HF), col(COL_HI), pl.BlockSpec((2, HGRN_STEP_WIDTH), lambda h, i: (0, h))],
               out_specs=[pl.BlockSpec((t, HGRN_STEP_WIDTH), lambda h, i: (i, h)),
                          pl.BlockSpec((nc, hp, LANES, LANES), lambda h, i: (i, h, 0, 0))],
               out_shape=[SDS((s, HGRN_WIDTH), F32), SDS((s // HGRN_CHUNK, HGRN_HEADS, LANES, LANES), F32)],
               scratch=[pltpu.VMEM((hp, LANES, LANES), F32)],
               sem=("parallel", "arbitrary"))(proj, proj, proj, lb_logits)


def hgrn_bwd(name, proj, lb_logits, layer, states, do):
    s = proj.shape[0]
    t = _tile(s, ROW_TILE)
    nt = s // t
    nc = t // HGRN_CHUNK
    col, row = _hgrn_specs(t, True, nt)

    def body(hq_ref, hf_ref, hi_ref, l_ref, st_ref, do_ref, dhq_ref, dhf_ref, dhi_ref, dlb_ref, dstate):
        @pl.when(pl.program_id(1) == 0)
        def _():
            dstate[...] = jnp.zeros_like(dstate)

        for hh in range(HGRN_HEADS_PER_STEP):
            hq_v, hf_v, hi_v, l_v, do_v, dhq_v, dhf_v, dhi_v, dlb_v = _head_views(
                (hq_ref, hf_ref, hi_ref, l_ref, do_ref, dhq_ref, dhf_ref, dhi_ref, dlb_ref), hh)
            head(hq_v, hf_v, hi_v, l_v, st_ref.at[:, hh], do_v, dhq_v, dhf_v, dhi_v, dlb_v, dstate.at[hh])

    def head(hq_ref, hf_ref, hi_ref, l_ref, st_ref, do_ref, dhq_ref, dhf_ref, dhi_ref, dlb_ref, dstate):
        first = pl.program_id(1) == 0
        lb = _lower_bound(l_ref, layer)
        sg, f, kin, hq, sq = _hgrn_gates(hq_ref, hf_ref, lb)
        q = hq * sq
        vb = hi_ref[...].astype(BF16)
        b = _chunk_sum(jnp.log(f), False)
        dob = do_ref[...].astype(BF16)
        eb = jnp.exp(b)
        qe = q * eb
        qeb = qe.astype(BF16)
        trow, lane = _chunk_iotas()
        last_row = trow == HGRN_CHUNK - 1

        decay, update = [None] * nc, [None] * nc
        for c in range(nc):
            rs = slice(c * HGRN_CHUNK, (c + 1) * HGRN_CHUNK)
            decay[c] = jnp.exp(b[(c + 1) * HGRN_CHUNK - 1:(c + 1) * HGRN_CHUNK, :])
            update[c] = lax.dot_general(dob[rs], qeb[rs], _TN, preferred_element_type=F32)
        dn_in = [None] * nc
        dn = dstate[...]
        for c in reversed(range(nc)):
            dn_in[c] = dn
            dn = dn * decay[c] + update[c]
        dstate[...] = dn

        dq_c, dk_c, dv_c, dbl_c = [None] * nc, [None] * nc, [None] * nc, [None] * nc
        for c in range(nc):
            rs = slice(c * HGRN_CHUNK, (c + 1) * HGRN_CHUNK)
            bc = b[rs]
            ekb = jnp.exp(bc[HGRN_CHUNK - 1:HGRN_CHUNK, :] - bc)
            ke = kin[rs] * ekb
            st = st_ref[c]
            dnb = dn_in[c].astype(BF16)
            dke = jnp.dot(vb[rs], dnb, preferred_element_type=F32)
            dq_c[c] = jnp.dot(dob[rs], st.astype(BF16), preferred_element_type=F32) * eb[rs]
            dk_c[c] = dke * ekb
            dv_c[c] = lax.dot_general(ke.astype(BF16), dnb, _NT, preferred_element_type=F32)
            dbl_c[c] = jnp.sum(dn_in[c] * st, axis=0, keepdims=True) * decay[c] + jnp.sum(dke * ke, axis=0, keepdims=True)

        db_c = [None] * nc
        for j in range(t // LANES):
            blk = slice(j * LANES, (j + 1) * LANES)
            damat = lax.dot_general(dob[blk], vb[blk], _NT, preferred_element_type=F32)
            rows = [None] * CHUNKS_PER_BLOCK
            for cc in range(CHUNKS_PER_BLOCK):
                c = j * CHUNKS_PER_BLOCK + cc
                rs = _chunk_rows(j, cc)
                bc, qc, kc = b[rs], q[rs], kin[rs]
                dam = damat[cc * HGRN_CHUNK:(cc + 1) * HGRN_CHUNK]
                here = trow + cc * HGRN_CHUNK
                on = lane == here
                da = jnp.sum(jnp.where(on, dam, 0.0), axis=-1, keepdims=True)
                am = jnp.where(on, jnp.sum(qc * kc, axis=-1, keepdims=True), 0.0)
                dq = dq_c[c] + da * kc
                dk = dk_c[c] + da * qc
                for d in range(1, HGRN_CHUNK):
                    on = lane == here - d
                    e = jnp.exp(jnp.where(trow >= d, bc - _shift_down(bc, d), MASK_VALUE))
                    kse = _shift_down(kc, d) * e
                    am = jnp.where(on, jnp.sum(qc * kse, axis=-1, keepdims=True), am)
                    da = jnp.sum(jnp.where(on, dam, 0.0), axis=-1, keepdims=True)
                    dq = dq + da * kse
                    dk = dk + _shift_up(da * (qc * e), d)
                rows[cc] = am
                dq_c[c], dk_c[c] = dq, dk
                db_c[c] = (qc * dq - kc * dk) + jnp.where(last_row, dbl_c[c], 0.0)
            dv_blk = lax.dot_general(jnp.concatenate(rows, axis=0).astype(BF16), dob[blk], _TN, preferred_element_type=F32)
            for cc in range(CHUNKS_PER_BLOCK):
                c = j * CHUNKS_PER_BLOCK + cc
                dv_c[c] = dv_c[c] + dv_blk[cc * HGRN_CHUNK:(cc + 1) * HGRN_CHUNK]
        dq = jnp.concatenate(dq_c, axis=0)
        dk = jnp.concatenate(dk_c, axis=0)
        dv = jnp.concatenate(dv_c, axis=0)
        db = jnp.concatenate(db_c, axis=0)
        dg = _chunk_sum(db, True)
        dhq_ref[...] = (dq * (sq * (1.0 + hq * (1.0 - sq)))).astype(BF16)
        dhi_ref[...] = dv.astype(BF16)
        dfk = dg / f - dk
        dhf_ref[...] = ((dfk * (1.0 - lb)) * (sg * (1.0 - sg))).astype(BF16)
        part = jnp.sum(dfk * (1.0 - sg), axis=0, keepdims=True)

        @pl.when(first)
        def _():
            dlb_ref[...] = part

        @pl.when(jnp.logical_not(first))
        def _():
            dlb_ref[...] += part

    hp = HGRN_HEADS_PER_STEP
    out_col = pl.BlockSpec((t, HGRN_STEP_WIDTH), lambda h, i: (nt - 1 - i, h))
    return _pc(body, name=name, grid=(HGRN_HEADS // hp, nt),
               in_specs=[col(COL_HQ), col(COL_HF), col(COL_HI), pl.BlockSpec((2, HGRN_STEP_WIDTH), lambda h, i: (0, h)),
                         pl.BlockSpec((nc, hp, LANES, LANES), lambda h, i: (nt - 1 - i, h, 0, 0)), out_col],
               out_specs=[out_col, out_col, out_col, pl.BlockSpec((1, HGRN_STEP_WIDTH), lambda h, i: (0, h))],
               out_shape=[SDS((s, HGRN_WIDTH), BF16)] * 3 + [SDS((1, HGRN_WIDTH), F32)],
               scratch=[pltpu.VMEM((hp, LANES, LANES), F32)],
               sem=("parallel", "arbitrary"))(proj, proj, proj, lb_logits, states, do)


def mix_out_fwd(name, attn, o, proj, g_attn, g_hgrn):
    s = attn.shape[0]
    t = _tile(s, ROW_TILE_WIDE)
    half = HGRN_WIDTH // 2

    def body(a_ref, o_ref, hg0_ref, hg1_ref, ga_ref, gh_ref, c_ref):
        av = a_ref[...]
        c_ref[:, :ATTN_WIDTH] = ((av * _rstd(av)) * ga_ref[...]).astype(BF16)
        for j in range(HGRN_HEADS):
            cols = slice(j * LANES, (j + 1) * LANES)
            ov = o_ref[:, cols]
            hg_ref, hcols = (hg0_ref, cols) if j < 4 else (hg1_ref, slice((j - 4) * LANES, (j - 3) * LANES))
            hg = hg_ref[:, hcols]
            on = (ov * _rstd(ov)) * gh_ref[:, cols]
            c_ref[:, ATTN_WIDTH + j * LANES:ATTN_WIDTH + (j + 1) * LANES] = (on * (hg * _sigmoid(hg))).astype(BF16)

    return _pc(body, name=name, grid=(s // t,),
               in_specs=[_rows(t, ATTN_WIDTH), _rows(t, HGRN_WIDTH), _rows(t, half, COL_HG // 4), _rows(t, half, COL_HG // 4 + 1),
                         _fixed((1, ATTN_WIDTH)), _fixed((1, HGRN_WIDTH))],
               out_specs=_rows(t, D_MODEL), out_shape=SDS((s, D_MODEL), BF16), sem=("parallel",))(attn, o, proj, proj, g_attn, g_hgrn)


def mix_out_bwd(name, dcat, attn, o, proj, g_attn, g_hgrn):
    s = attn.shape[0]
    t = _tile(s, ROW_TILE_WIDE)
    half = HGRN_WIDTH // 2

    def body(dc_ref, a_ref, o_ref, hg0_ref, hg1_ref, ga_ref, gh_ref, da_ref, do_ref, dhg_ref, dga_ref, dgh_ref, pa_s, ph_s):
        av = a_ref[...]
        r = _rstd(av)
        xh = av * r
        dyv = dc_ref[:, :ATTN_WIDTH]
        dyg = dyv * ga_ref[...]
        da_ref[...] = r * (dyg - xh * jnp.mean(dyg * xh, axis=-1, keepdims=True))
        pa_s[...] = jnp.sum(dyv * xh, axis=0, keepdims=True)
        for j in range(HGRN_HEADS):
            cols = slice(j * LANES, (j + 1) * LANES)
            ov = o_ref[:, cols]
            hg_ref, hcols = (hg0_ref, cols) if j < 4 else (hg1_ref, slice((j - 4) * LANES, (j - 3) * LANES))
            hg = hg_ref[:, hcols]
            sg = _sigmoid(hg)
            r = _rstd(ov)
            xh = ov * r
            gain = gh_ref[:, cols]
            dh = dc_ref[:, ATTN_WIDTH + j * LANES:ATTN_WIDTH + (j + 1) * LANES]
            dhg_ref[:, cols] = ((dh * (xh * gain)) * (sg * (1.0 + hg * (1.0 - sg)))).astype(BF16)
            dyv = dh * (hg * sg)
            dyg = dyv * gain
            do_ref[:, cols] = r * (dyg - xh * jnp.mean(dyg * xh, axis=-1, keepdims=True))
            ph_s[:, cols] = jnp.sum(dyv * xh, axis=0, keepdims=True)

        @pl.when(pl.program_id(0) == 0)
        def _():
            dga_ref[...] = pa_s[...]
            dgh_ref[...] = ph_s[...]

        @pl.when(pl.program_id(0) > 0)
        def _():
            dga_ref[...] += pa_s[...]
            dgh_ref[...] += ph_s[...]

    return _pc(body, name=name, grid=(s // t,),
               in_specs=[_rows(t, D_MODEL), _rows(t, ATTN_WIDTH), _rows(t, HGRN_WIDTH), _rows(t, half, COL_HG // 4),
                         _rows(t, half, COL_HG // 4 + 1), _fixed((1, ATTN_WIDTH)), _fixed((1, HGRN_WIDTH))],
               out_specs=[_rows(t, ATTN_WIDTH), _rows(t, HGRN_WIDTH), _rows(t, HGRN_WIDTH), _fixed((1, ATTN_WIDTH)),
                          _fixed((1, HGRN_WIDTH))],
               out_shape=[SDS((s, ATTN_WIDTH), F32), SDS((s, HGRN_WIDTH), F32), SDS((s, HGRN_WIDTH), BF16),
                          SDS((1, ATTN_WIDTH), F32), SDS((1, HGRN_WIDTH), F32)],
               scratch=[pltpu.VMEM((1, ATTN_WIDTH), F32), pltpu.VMEM((1, HGRN_WIDTH), F32)],
               sem=("arbitrary",))(dcat, attn, o, proj, proj, g_attn, g_hgrn)


BIG = (("w_in", 2048, 1408, 0), ("w_out", 512, 2048, 1), ("w_ffn_gate", 2048, 1408, 0), ("w_ffn_up", 2048, 1408, 0),
       ("w_ffn_down", 1408, 2048, 1), ("w_ple_gate", 512, 2048, 1), ("w_ple_proj", 256, 512, 0))
BIG_BY_NAME = {spec[0]: spec for spec in BIG}
HBM_SPEC = pl.BlockSpec(memory_space=pltpu.HBM)
SEM_SPEC = pl.BlockSpec(memory_space=pltpu.SEMAPHORE)
TOKEN_SHAPE = (8, LANES)


def _split_call(body, *, name, in_specs, out_specs, out_shape, aliases):
    return pl.pallas_call(body, name=name, in_specs=in_specs, out_specs=out_specs, out_shape=out_shape,
                          input_output_aliases=aliases,
                          compiler_params=pltpu.CompilerParams(has_side_effects=pltpu.SideEffectType.DATAFLOW_SIDE_EFFECTING))


def _in_hbm(arrays):
    return [pltpu.with_memory_space_constraint(a, pltpu.HBM) for a in arrays]


GROUP_STEPS = 8


def _step_rows(rows):
    assert rows % (GROUP_STEPS * 16) == 0
    return rows // GROUP_STEPS


def cast_to_slots(name, place, ws, layer, after):
    nt = len(ws)

    def body(place_ref, *refs):
        for t in range(nt):
            refs[nt + 1 + t][...] = refs[t][...].astype(BF16)

    in_specs, out_specs = [], []
    for w in ws:
        block = (None, _step_rows(w.shape[1]), w.shape[2])
        in_specs.append(pl.BlockSpec(block, lambda i, pr: (layer, i, 0)))
        out_specs.append(pl.BlockSpec(block, lambda i, pr: (pr[1], i, 0)))
    gs = pltpu.PrefetchScalarGridSpec(num_scalar_prefetch=1, grid=(GROUP_STEPS,), in_specs=in_specs + [ANY_SPEC],
                                      out_specs=out_specs)
    return _pc(body, name=name, grid_spec=gs, in_specs=None, out_specs=None,
               out_shape=[SDS((N_CHIPS,) + w.shape[1:], BF16) for w in ws], sem=("parallel",))(place, *ws, after)


def _place():
    x, y, c = lax.axis_index("x"), lax.axis_index("y"), lax.axis_index("c")
    chips = [(1 - x, y), (x, 1 - y), (1 - x, 1 - y)]
    return x, y, c, chips


def _half(ref, axis, c, rows, cols):
    if axis == 0:
        return ref.at[pl.ds(pl.multiple_of(c * (rows // 2), 16), rows // 2), :]
    return ref.at[:, pl.ds(pl.multiple_of(c * (cols // 2), LANES), cols // 2)]


def _gather_copies(specs, bufs, send, recv):
    x, y, c, chips = _place()
    cps = []
    for t, (_, rows, cols, axis) in enumerate(specs):
        mine = _half(bufs[t].at[2 * x + y], axis, c, rows, cols)
        for k, (cx, cy) in enumerate(chips):
            cps.append(pltpu.make_async_remote_copy(src_ref=mine, dst_ref=mine, send_sem=send.at[3 * t + k],
                                                    recv_sem=recv.at[3 * t + k], device_id=(cx, cy, c), device_id_type=MESH))
    return cps


def gather_start(name, specs, bufs, after):
    nt = len(bufs)
    n = 3 * nt

    def body(*refs):
        send, recv, token = refs[nt + 1], refs[nt + 2], refs[-1]
        for cp in _gather_copies(specs, refs[:nt], send, recv):
            cp.start()
        token[...] = jnp.zeros(TOKEN_SHAPE, F32)

    out = _split_call(
        body, name=name, in_specs=[HBM_SPEC] * nt + [ANY_SPEC],
        out_specs=(SEM_SPEC, SEM_SPEC) + (HBM_SPEC,) * nt + (pl.BlockSpec(memory_space=pltpu.VMEM),),
        out_shape=(pltpu.SemaphoreType.DMA((n,)), pltpu.SemaphoreType.DMA((n,)))
        + tuple(pltpu.HBM(b.shape, b.dtype) for b in bufs) + (SDS(TOKEN_SHAPE, F32),),
        aliases={t: 2 + t for t in range(nt)})(*_in_hbm(bufs), after)
    return out[0], out[1], list(out[2:2 + nt]), out[-1]


def gather_wait(name, specs, send, recv, bufs, after):
    nt = len(bufs)

    def body(*refs):
        for cp in _gather_copies(specs, refs[:nt], refs[nt], refs[nt + 1]):
            cp.wait_send()
            cp.wait_recv()

    out = _split_call(
        body, name=name, in_specs=[HBM_SPEC] * nt + [SEM_SPEC, SEM_SPEC, pl.BlockSpec(memory_space=pl.ANY)],
        out_specs=(HBM_SPEC,) * nt, out_shape=tuple(pltpu.HBM(b.shape, b.dtype) for b in bufs),
        aliases={t: t for t in range(nt)})(*bufs, send, recv, after)
    return list(out)


def gather_pass(name, specs, bufs):
    nt = len(bufs)

    def body(*refs):
        ins, outs = refs[:nt], refs[nt:2 * nt]
        send, recv = refs[2 * nt:]
        x, y, c, chips = _place()
        cps = []
        for t, (_, rows, cols, axis) in enumerate(specs):
            for k, (cx, cy) in enumerate(chips):
                cp = pltpu.make_async_remote_copy(
                    src_ref=_half(ins[t].at[2 * cx + cy], axis, c, rows, cols),
                    dst_ref=_half(outs[t].at[2 * cx + cy], axis, c, rows, cols),
                    send_sem=send.at[3 * t + k], recv_sem=recv.at[3 * t + k], device_id=(x, y, 1 - c), device_id_type=MESH)
                cp.start()
                cps.append(cp)
        for t, (_, rows, cols, axis) in enumerate(specs):
            for k, (cx, cy) in enumerate(chips):
                theirs = _half(outs[t].at[2 * cx + cy], axis, 1 - c, rows, cols)
                pltpu.make_async_remote_copy(src_ref=theirs, dst_ref=theirs, send_sem=send.at[3 * t + k],
                                             recv_sem=recv.at[3 * t + k], device_id=(x, y, 1 - c), device_id_type=MESH).wait_recv()
        for cp in cps:
            cp.wait_send()

    return _pc(body, name=name, in_specs=[HBM_SPEC] * nt, out_specs=[HBM_SPEC] * nt,
               out_shape=[SDS(b.shape, b.dtype) for b in bufs], scratch=[pltpu.SemaphoreType.DMA((3 * nt,))] * 2,
               input_output_aliases={t: t for t in range(nt)})(*bufs)


def reduce_to_sibling(name, grads):
    nt = len(grads)

    def body(*refs):
        srcs, dsts = refs[:nt], refs[nt:2 * nt]
        send, recv = refs[2 * nt:]
        x, y, c, _ = _place()
        cps = []
        for t in range(nt):
            cp = pltpu.make_async_remote_copy(src_ref=srcs[t].at[1 - c], dst_ref=dsts[t], send_sem=send.at[t],
                                              recv_sem=recv.at[t], device_id=(x, y, 1 - c), device_id_type=MESH)
            cp.start()
            cps.append(cp)
        for cp in cps:
            cp.wait()

    return _pc(body, name=name, in_specs=[HBM_SPEC] * nt, out_specs=[HBM_SPEC] * nt,
               out_shape=[SDS(g.shape[1:], g.dtype) for g in grads],
               scratch=[pltpu.SemaphoreType.DMA((nt,))] * 2)(*grads)


def _sibling_copies(grads, lands, send, recv):
    x, y, c, _ = _place()
    return [pltpu.make_async_remote_copy(src_ref=grads[t].at[1 - c], dst_ref=lands[t], send_sem=send.at[t], recv_sem=recv.at[t],
                                         device_id=(x, y, 1 - c), device_id_type=MESH) for t in range(len(grads))]


def sibling_start(name, grads):
    nt = len(grads)
    lands = [lax.empty(g.shape[1:], g.dtype) for g in grads]

    def body(*refs):
        send, recv, token = refs[2 * nt], refs[2 * nt + 1], refs[-1]
        for cp in _sibling_copies(refs[:nt], refs[nt:2 * nt], send, recv):
            cp.start()
        token[...] = jnp.zeros(TOKEN_SHAPE, F32)

    both = list(grads) + lands
    out = _split_call(
        body, name=name, in_specs=[HBM_SPEC] * (2 * nt),
        out_specs=(SEM_SPEC, SEM_SPEC) + (HBM_SPEC,) * (2 * nt) + (pl.BlockSpec(memory_space=pltpu.VMEM),),
        out_shape=(pltpu.SemaphoreType.DMA((nt,)), pltpu.SemaphoreType.DMA((nt,)))
        + tuple(pltpu.HBM(b.shape, b.dtype) for b in both) + (SDS(TOKEN_SHAPE, F32),),
        aliases={t: 2 + t for t in range(2 * nt)})(*_in_hbm(both))
    return out[0], out[1], list(out[2:2 + nt]), list(out[2 + nt:2 + 2 * nt]), out[-1]


def sibling_wait(name, send, recv, grads, lands, after):
    nt = len(grads)

    def body(*refs):
        for cp in _sibling_copies(refs[:nt], refs[nt:2 * nt], refs[2 * nt], refs[2 * nt + 1]):
            cp.wait_send()
            cp.wait_recv()

    both = list(grads) + list(lands)
    out = _split_call(
        body, name=name, in_specs=[HBM_SPEC] * (2 * nt) + [SEM_SPEC, SEM_SPEC, pl.BlockSpec(memory_space=pl.ANY)],
        out_specs=(HBM_SPEC,) * (2 * nt), out_shape=tuple(pltpu.HBM(b.shape, b.dtype) for b in both),
        aliases={t: t for t in range(2 * nt)})(*both, send, recv, after)
    return list(out[:nt]), list(out[nt:])


def add_halves(name, place, grads, gots):
    nt = len(grads)

    def body(place_ref, *refs):
        for t in range(nt):
            val = (refs[t][...].astype(F32) + refs[nt + t][...].astype(F32)).astype(BF16)
            refs[2 * nt + t][...] = val

            @pl.when(pl.program_id(1) == place_ref[1])
            def _():
                refs[3 * nt + t][...] = val

    g_specs, o_specs, part_specs, slot_specs = [], [], [], []
    for g in grads:
        _, n, r, c = g.shape
        tr = _step_rows(r)
        g_specs.append(pl.BlockSpec((None, None, tr, c), lambda i, j, pr: (pr[0], j, i, 0)))
        o_specs.append(pl.BlockSpec((None, tr, c), lambda i, j, pr: (j, i, 0)))
        part_specs.append(pl.BlockSpec((None, tr, c), lambda i, j, pr: (j, i, 0)))
        slot_specs.append(pl.BlockSpec((None, tr, c), lambda i, j, pr: (pr[1], i, 0)))
    gs = pltpu.PrefetchScalarGridSpec(num_scalar_prefetch=1, grid=(GROUP_STEPS, N_CHIPS), in_specs=g_specs + o_specs,
                                      out_specs=part_specs + slot_specs)
    out = _pc(body, name=name, grid_spec=gs, in_specs=None, out_specs=None,
              out_shape=[SDS(g.shape[1:], BF16) for g in grads] * 2, sem=("parallel", "arbitrary"))(place, *grads, *gots)
    return list(out[:nt]), list(out[nt:])


def _chips_copies(parts, slots, send, recv):
    x, y, c, chips = _place()
    cps = []
    for t in range(len(parts)):
        for k, (cx, cy) in enumerate(chips):
            cps.append(pltpu.make_async_remote_copy(src_ref=parts[t].at[2 * cx + cy], dst_ref=slots[t].at[2 * x + y],
                                                    send_sem=send.at[3 * t + k], recv_sem=recv.at[3 * t + k],
                                                    device_id=(cx, cy, c), device_id_type=MESH))
    return cps


def chips_start(name, parts, slots):
    nt = len(parts)
    n = 3 * nt

    def body(*refs):
        send, recv, token = refs[2 * nt], refs[2 * nt + 1], refs[-1]
        for cp in _chips_copies(refs[:nt], refs[nt:2 * nt], send, recv):
            cp.start()
        token[...] = jnp.zeros(TOKEN_SHAPE, F32)

    both = list(parts) + list(slots)
    out = _split_call(
        body, name=name, in_specs=[HBM_SPEC] * (2 * nt),
        out_specs=(SEM_SPEC, SEM_SPEC) + (HBM_SPEC,) * (2 * nt) + (pl.BlockSpec(memory_space=pltpu.VMEM),),
        out_shape=(pltpu.SemaphoreType.DMA((n,)), pltpu.SemaphoreType.DMA((n,)))
        + tuple(pltpu.HBM(b.shape, b.dtype) for b in both) + (SDS(TOKEN_SHAPE, F32),),
        aliases={t: 2 + t for t in range(2 * nt)})(*_in_hbm(both))
    return out[0], out[1], list(out[2:2 + nt]), list(out[2 + nt:2 + 2 * nt]), out[-1]


def chips_wait(name, send, recv, parts, slots, after):
    nt = len(parts)

    def body(*refs):
        for cp in _chips_copies(refs[:nt], refs[nt:2 * nt], refs[2 * nt], refs[2 * nt + 1]):
            cp.wait_send()
            cp.wait_recv()

    both = list(parts) + list(slots)
    out = _split_call(
        body, name=name, in_specs=[HBM_SPEC] * (2 * nt) + [SEM_SPEC, SEM_SPEC, pl.BlockSpec(memory_space=pl.ANY)],
        out_specs=(HBM_SPEC,) * (2 * nt), out_shape=tuple(pltpu.HBM(b.shape, b.dtype) for b in both),
        aliases={t: t for t in range(2 * nt)})(*both, send, recv, after)
    return list(out[nt:])


def sum_chips(name, place, slots):
    nt = len(slots)

    def body(place_ref, *refs):
        for t in range(nt):
            s_ref = refs[t]
            acc = s_ref[0].astype(F32)
            for k in range(1, N_CHIPS):
                acc = acc + s_ref[k].astype(F32)
            refs[nt + t][...] = acc

    in_specs, out_specs = [], []
    for sl in slots:
        n, r, c = sl.shape
        tr = _step_rows(r)
        in_specs.append(pl.BlockSpec((n, tr, c), lambda i, pr: (0, i, 0)))
        out_specs.append(pl.BlockSpec((None, tr, c), lambda i, pr: (pr[0], i, 0)))
    gs = pltpu.PrefetchScalarGridSpec(num_scalar_prefetch=1, grid=(GROUP_STEPS,), in_specs=in_specs, out_specs=out_specs)
    return list(_pc(body, name=name, grid_spec=gs, in_specs=None, out_specs=None,
                    out_shape=[SDS((2,) + sl.shape[1:], F32) for sl in slots], sem=("parallel",))(place, *slots))


def share_with_sibling(name, bufs):
    nt = len(bufs)

    def body(*refs):
        ins, outs = refs[:nt], refs[nt:2 * nt]
        send, recv = refs[2 * nt:]
        x, y, c, _ = _place()
        cps = []
        for t in range(nt):
            cp = pltpu.make_async_remote_copy(src_ref=ins[t].at[c], dst_ref=outs[t].at[c], send_sem=send.at[t], recv_sem=recv.at[t],
                                              device_id=(x, y, 1 - c), device_id_type=MESH)
            cp.start()
            cps.append(cp)
        for t in range(nt):
            theirs = outs[t].at[1 - c]
            pltpu.make_async_remote_copy(src_ref=theirs, dst_ref=theirs, send_sem=send.at[t], recv_sem=recv.at[t],
                                         device_id=(x, y, 1 - c), device_id_type=MESH).wait_recv()
        for cp in cps:
            cp.wait_send()

    return _pc(body, name=name, in_specs=[HBM_SPEC] * nt, out_specs=[HBM_SPEC] * nt,
               out_shape=[SDS(b.shape, F32) for b in bufs], scratch=[pltpu.SemaphoreType.DMA((nt,))] * 2,
               input_output_aliases={t: t for t in range(nt)})(*bufs)


def _adamw(w, g, m, v):
    m = ADAM_B1 * m + (1.0 - ADAM_B1) * g
    v = ADAM_B2 * v + (1.0 - ADAM_B2) * (g * g)
    m_hat = m / (1.0 - ADAM_B1 ** ADAM_STEP)
    v_hat = v / (1.0 - ADAM_B2 ** ADAM_STEP)
    delta = -ADAM_LR * (m_hat / (jnp.sqrt(v_hat) + ADAM_EPS) + ADAM_WD * w)
    return delta, m, v


def adamw_big(name, w, m, v, g0, g1, axis):
    _, r, c = w.shape
    _, rh, ch = g0.shape
    tr = _tile(rh, 256)
    nb = rh // tr
    if axis == 0:
        wspec = pl.BlockSpec((None, tr, ch), lambda l, h, i: (l, h * nb + i, 0))
    else:
        wspec = pl.BlockSpec((None, tr, ch), lambda l, h, i: (l, i, h))
    g0spec = pl.BlockSpec((None, tr, ch), lambda l, h, i: (h * (1 - l), i * (1 - l), 0))
    g1spec = pl.BlockSpec((None, tr, ch), lambda l, h, i: (h * l, i * l, 0))

    def body(w_ref, m_ref, v_ref, g0_ref, g1_ref, go_ref, d_ref, mo_ref, vo_ref):
        def run(g_ref):
            g = g_ref[...]
            delta, mn, vn = _adamw(w_ref[...], g, m_ref[...], v_ref[...])
            go_ref[...] = g
            d_ref[...] = delta
            mo_ref[...] = mn
            vo_ref[...] = vn

        @pl.when(pl.program_id(0) == 0)
        def _():
            run(g0_ref)

        @pl.when(pl.program_id(0) == 1)
        def _():
            run(g1_ref)

    return _pc(body, name=name, grid=(2, 2, nb), in_specs=[wspec, wspec, wspec, g0spec, g1spec], out_specs=[wspec] * 4,
               out_shape=[SDS(w.shape, F32)] * 4, sem=("parallel", "parallel", "parallel"))(w, m, v, g0, g1)


SMALL = (("pre_mix_gain", 2048), ("post_mix_gain", 2048), ("pre_ffn_gain", 2048), ("post_ffn_gain", 2048), ("ple_gain", 2048),
         ("attn_out_gain", 1024), ("hgrn_out_gain", 1024), ("hgrn_lb_logits", 1024), ("attn_sinks", 128))
SMALL_ROWS = sum(2 * w // LANES for _, w in SMALL)
SMALL_PAD = -(-SMALL_ROWS // 8) * 8
LB_ROW = sum(2 * w // LANES for _, w in SMALL[:7])


def _pack_small(parts):
    rows = []
    for nm, w in SMALL:
        a = parts[nm].astype(F32)
        if a.shape[1] != w:
            a = jnp.pad(a, ((0, 0), (0, w - a.shape[1])))
        rows.append(a.reshape(2 * w // LANES, LANES))
    rows.append(jnp.zeros((SMALL_PAD - SMALL_ROWS, LANES), F32))
    return jnp.concatenate(rows, axis=0)


def _unpack_small(packed, widths):
    out, r = {}, 0
    for nm, w in SMALL:
        n = 2 * w // LANES
        out[nm] = packed[r:r + n].reshape(2, w)[:, :widths[nm]]
        r += n
    return out


def allreduce_small(name, packed):
    rows = packed.shape[0]

    def body(x_ref, o_ref, buf, send, recv, own_sem):
        x, y, c, _ = _place()
        me = 4 * x + 2 * y + c
        own = pltpu.make_async_copy(x_ref, buf.at[me], own_sem)
        own.start()
        cps = []
        for k in range(1, 8):
            px, py, pc = x ^ (k >> 2), y ^ ((k >> 1) & 1), c ^ (k & 1)
            cp = pltpu.make_async_remote_copy(src_ref=x_ref, dst_ref=buf.at[me], send_sem=send.at[k - 1], recv_sem=recv.at[k - 1],
                                              device_id=(px, py, pc), device_id_type=MESH)
            cp.start()
            cps.append(cp)
        for k in range(1, 8):
            px, py, pc = x ^ (k >> 2), y ^ ((k >> 1) & 1), c ^ (k & 1)
            slot = buf.at[4 * px + 2 * py + pc]
            pltpu.make_async_remote_copy(src_ref=slot, dst_ref=slot, send_sem=send.at[k - 1], recv_sem=recv.at[k - 1],
                                         device_id=(px, py, pc), device_id_type=MESH).wait_recv()
        for cp in cps:
            cp.wait_send()
        own.wait()
        acc = buf[0]
        for k in range(1, 8):
            acc = acc + buf[k]
        o_ref[...] = acc

    vm = pl.BlockSpec(memory_space=pltpu.VMEM)
    return _pc(body, name=name, in_specs=[vm], out_specs=vm, out_shape=SDS((rows, LANES), F32),
               scratch=[pltpu.VMEM((8, rows, LANES), F32), pltpu.SemaphoreType.DMA((7,)), pltpu.SemaphoreType.DMA((7,)),
                        pltpu.SemaphoreType.DMA])(packed)


def adamw_small(name, w, m, v, g):
    rows = w.shape[0]
    n = HGRN_WIDTH // LANES

    def body(w_ref, m_ref, v_ref, g_ref, go_ref, d_ref, mo_ref, vo_ref):
        go_ref[...] = g_ref[...]
        l0 = w_ref[LB_ROW:LB_ROW + n, :]
        l1 = w_ref[LB_ROW + n:LB_ROW + 2 * n, :]
        mx = jnp.maximum(l0, l1)
        e0, e1 = jnp.exp(l0 - mx), jnp.exp(l1 - mx)
        s0, s1 = e0 / (e0 + e1), e1 / (e0 + e1)
        dlb1 = g_ref[LB_ROW + n:LB_ROW + 2 * n, :]
        inner = s1 * dlb1
        go_ref[LB_ROW:LB_ROW + n, :] = s0 * (0.0 - inner)
        go_ref[LB_ROW + n:LB_ROW + 2 * n, :] = s1 * (dlb1 - inner)
        delta, mn, vn = _adamw(w_ref[...], go_ref[...], m_ref[...], v_ref[...])
        d_ref[...] = delta
        mo_ref[...] = mn
        vo_ref[...] = vn

    vm = pl.BlockSpec(memory_space=pltpu.VMEM)
    return _pc(body, name=name, in_specs=[vm] * 4, out_specs=[vm] * 4, out_shape=[SDS((rows, LANES), F32)] * 4)(w, m, v, g)


def _layer_fwd(l, x, h1, p_l, w_in_g, rest_of_weights, gains, cos, sin, sinks, lb_logits, g_next, target):
    n = f"l{l}_"
    proj = mm_col(n + "in_proj", h1, w_in_g)
    qp, kp, vp = rope_qkv(n + "rope_qkv", proj, cos, sin)
    attn = attn_fwd(n + "attn_fwd", qp, kp, vp, sinks)
    o, states = hgrn_fwd(n + "hgrn_fwd", proj, lb_logits, l)
    cat = mix_out_fwd(n + "mix_out_fwd", attn, o, proj, gains["attn_out_gain"], gains["hgrn_out_gain"])
    rest, token = rest_of_weights(cat)
    wts = dict(rest, w_in=w_in_g)
    if token is not None:
        gains = _with_token(gains, "post_mix_gain", token)
    m, x1, h2 = out_proj_post_mix(n + "out_proj_post_mix", cat, wts["w_out"], gains["post_mix_gain"], x, gains["pre_ffn_gain"])
    g, u, a = ffn_gate_up(n + "ffn_gate_up", h2, wts["w_ffn_gate"], wts["w_ffn_up"])
    f = mm_row(n + "ffn_down", a, wts["w_ffn_down"])
    x2, h3 = post_pre_norm(n + "post_ffn", f, gains["post_ffn_gain"], x1, gains["ple_gain"])
    pp = mm_col(n + "ple_proj", p_l, wts["w_ple_proj"])
    saved = dict(x=x, h1=h1, proj=proj, qp=qp, kp=kp, vp=vp, attn=attn, o=o, states=states, cat=cat, m=m, x1=x1, h2=h2,
                 g=g, u=u, a=a, f=f, x2=x2, h3=h3, p=p_l)
    if target is None:
        z, *out = ple_gate_fwd_mid(n + "ple_gate_fwd", h3, wts["w_ple_gate"], pp, x2, g_next)
        saved.update(z=z, pp=pp)
    else:
        dy, dpp, dz, loss = ple_gate_fwd_loss(n + "ple_gate_loss", h3, wts["w_ple_gate"], pp, x2, target)
        out = [dy, loss]
        saved.update(dpp=dpp, dz=dz)
    return out, saved, wts


EARLY = ("w_ple_gate", "w_ple_proj", "w_ffn_down", "w_ffn_gate", "w_ffn_up")
LATE = ("w_out", "w_in")


def _layer_bwd_ffn(l, dx3, sv, wts, gains, after=None, hook=None):
    n = f"l{l}_"
    dpp, dz = (sv["dpp"], sv["dz"]) if "dz" in sv else ple_bwd(n + "ple_bwd", dx3, sv["z"], sv["pp"],
                                                                 dx3 if after is None else after)
    dh3 = mm_row_t(n + "ple_gate_dx", dz, wts["w_ple_gate"])
    if hook is not None:
        gains = _with_token(gains, "ple_gain", hook(dh3))
    dx2, df, d_ple_gain, d_post_ffn = norm_bwd_pair(n + "ple_post_ffn_bwd", sv["x2"], gains["ple_gain"], dh3, dx3, sv["f"],
                                                    gains["post_ffn_gain"])
    dg, du = ffn_down_bwd(n + "ffn_down_bwd", df, wts["w_ffn_down"], sv["g"], sv["u"])
    big = dict(
        w_ple_gate=mm_wg_row(n + "ple_gate_dw", sv["h3"], dz),
        w_ple_proj=mm_wg_col(n + "ple_proj_dw", sv["p"], dpp),
        w_ffn_down=mm_wg_row(n + "ffn_down_dw", sv["a"], df),
        w_ffn_gate=mm_wg_col(n + "ffn_gate_dw", sv["h2"], dg),
        w_ffn_up=mm_wg_col(n + "ffn_up_dw", sv["h2"], du),
    )
    return dict(dx2=dx2, dg=dg, du=du), big, dict(ple_gain=d_ple_gain, post_ffn_gain=d_post_ffn)


def _layer_bwd_mix(l, st, sv, wts, gains, cos, sin, sinks, lb_logits, after=None, hook=None):
    n = f"l{l}_"
    dh2 = mm_col_t(n + "ffn_gate_dx", st["dg"], wts["w_ffn_gate"], after=after)
    dh2 = mm_col_t(n + "ffn_up_dx", st["du"], wts["w_ffn_up"], add=dh2)
    if hook is not None:
        gains = _with_token(gains, "pre_ffn_gain", hook(dh2))
    dx1, dm, d_pre_ffn, d_post_mix = norm_bwd_pair(n + "pre_ffn_post_mix_bwd", sv["x1"], gains["pre_ffn_gain"], dh2, st["dx2"],
                                                   sv["m"], gains["post_mix_gain"])
    dcat = mm_row_t(n + "out_proj_dx", dm, wts["w_out"])
    dattn, do, dhg, d_attn_gain, d_hgrn_gain = mix_out_bwd(n + "mix_out_bwd", dcat, sv["attn"], sv["o"], sv["proj"],
                                                            gains["attn_out_gain"], gains["hgrn_out_gain"])
    dqp, dkc, dkp, dvc, dvp, dsinks = attn_bwd(n + "attn_bwd", sv["qp"], sv["kp"], sv["vp"], sinks, dattn)
    dqkv = rope_bwd(n + "rope_bwd", dqp, dkc, dkp, dvc, dvp, cos, sin)
    dhq, dhf, dhi, dlb = hgrn_bwd(n + "hgrn_bwd", sv["proj"], lb_logits, l, sv["states"], do)
    dproj = jnp.concatenate([dqkv, dhq, dhf, dhi, dhg], axis=1)
    dh1 = mm_col_t(n + "in_proj_dx", dproj, wts["w_in"])
    dx, d_pre_mix = norm_bwd(n + "pre_mix_bwd", sv["x"], gains["pre_mix_gain"], dh1, dx1)
    big = dict(w_out=mm_wg_row(n + "out_proj_dw", sv["cat"], dm), w_in=mm_wg_col(n + "in_proj_dw", sv["h1"], dproj))
    small = dict(pre_mix_gain=d_pre_mix, post_mix_gain=d_post_mix, pre_ffn_gain=d_pre_ffn, attn_out_gain=d_attn_gain,
                 hgrn_out_gain=d_hgrn_gain, hgrn_lb_logits=dlb, attn_sinks=dsinks)
    return dx, big, small


def _layer_bwd(l, dx3, sv, wts, gains, cos, sin, sinks, lb_logits):
    st, early, small_a = _layer_bwd_ffn(l, dx3, sv, wts, gains)
    dx, late, small_b = _layer_bwd_mix(l, st, sv, wts, gains, cos, sin, sinks, lb_logits)
    return dx, {**early, **late}, {**small_a, **small_b}


def _reduce_start(tag, names, big, place):
    grads = [big[nm] for nm in names]
    got = reduce_to_sibling(tag + "_reduce_to_sibling", grads)
    parts, slots = add_halves(tag + "_add", place, grads, got)
    return chips_start(tag + "_chips_start", parts, slots)


def _reduce_begin(tag, names, big):
    return sibling_start(tag + "_sibling_start", [big[nm] for nm in names])


def _reduce_chips(tag, names, begun, place, after):
    send, recv, grads, lands, _ = begun
    grads, got = sibling_wait(tag + "_sibling_wait", send, recv, grads, lands, after)
    parts, slots = add_halves(tag + "_add", place, grads, got)
    return chips_start(tag + "_chips_start", parts, slots)


def _reduce_finish(tag, names, started, place, after):
    send, recv, parts, slots, _ = started
    slots = chips_wait(tag + "_chips_wait", send, recv, parts, slots, after)
    bufs = sum_chips(tag + "_sum", place, slots)
    return dict(zip(names, share_with_sibling(tag + "_share_with_sibling", bufs)))


def _with_token(gains, name, token):
    out = dict(gains)
    out[name] = gains[name] + token[0, 0]
    return out


def kernel(x, p, positions, w_in, attn_sinks, hgrn_lb_logits, attn_out_gain, hgrn_out_gain, w_out, pre_mix_gain, post_mix_gain, pre_ffn_gain, post_ffn_gain, w_ffn_gate, w_ffn_up, w_ffn_down, ple_gain, w_ple_gate, w_ple_proj, loss_target, m_w_in, m_attn_sinks, m_hgrn_lb_logits, m_attn_out_gain, m_hgrn_out_gain, m_w_out, m_pre_mix_gain, m_post_mix_gain, m_pre_ffn_gain, m_post_ffn_gain, m_w_ffn_gate, m_w_ffn_up, m_w_ffn_down, m_ple_gain, m_w_ple_gate, m_w_ple_proj, v_w_in, v_attn_sinks, v_hgrn_lb_logits, v_attn_out_gain, v_hgrn_out_gain, v_w_out, v_pre_mix_gain, v_post_mix_gain, v_pre_ffn_gain, v_post_ffn_gain, v_w_ffn_gate, v_w_ffn_up, v_w_ffn_down, v_ple_gain, v_w_ple_gate, v_w_ple_proj):
    given = dict(locals())
    depth = 2
    place = jnp.stack([lax.axis_index("c"), 2 * lax.axis_index("x") + lax.axis_index("y")]).astype(jnp.int32)
    xs = x[0]
    tgt = loss_target[0]
    pos_col = positions.reshape(-1, 1)
    half = 32
    inv_freq = ROPE_THETA ** (-jnp.arange(half, dtype=F32) / half)
    inv_freq = jnp.tile(inv_freq, 4).reshape(1, LANES)
    gains = [{nm: given[nm][l:l + 1] for nm, _ in SMALL[:7]} for l in range(depth)]
    names = [nm for nm, *_ in BIG]
    first, others = names[:1], names[1:]

    def specs(nms):
        return [BIG_BY_NAME[nm] for nm in nms]

    def cast(tag, l, nms, after):
        return cast_to_slots(tag + "_cast", place, [given[nm] for nm in nms], l, after)

    def finish_gather(tag, nms, started, after):
        bufs = gather_wait(tag + "_gather_wait", specs(nms), started[0], started[1], started[2], after)
        return dict(zip(nms, gather_pass(tag + "_gather_pass", specs(nms), bufs)))

    g0a = gather_start("l0a_gather_start", specs(first), cast("l0a", 0, first, place), place)
    g0b = gather_start("l0b_gather_start", specs(others), cast("l0b", 0, others, g0a[3]), g0a[3])
    started = {}

    def rest_of_layer0(after):
        got = finish_gather("l0b", others, g0b, after)
        started["l1"] = gather_start("l1_gather_start", specs(names), l1_shards, got["w_out"])
        return got, started["l1"][3]

    cos, sin = rope_tables("rope_tables", pos_col, inv_freq)
    h1 = pre_norm("l0_pre_mix", xs, _with_token(gains[0], "pre_mix_gain", g0b[3])["pre_mix_gain"])
    l1_shards = cast("l1", 1, names, h1)
    w_in0 = finish_gather("l0a", first, g0a, l1_shards[0])["w_in"]
    (x_mid, h1_next), sv0, wts0 = _layer_fwd(0, xs, h1, p[0, 0], w_in0, rest_of_layer0, gains[0], cos, sin, attn_sinks[0],
                                             hgrn_lb_logits, gains[1]["pre_mix_gain"], None)
    wts1 = finish_gather("l1", names, started["l1"], x_mid)
    (dy, loss_part), sv1, _ = _layer_fwd(1, x_mid, h1_next, p[1, 0], wts1["w_in"], lambda after: (wts1, None), gains[1], cos, sin,
                                         attn_sinks[1], hgrn_lb_logits, None, tgt)

    dx_mid, big1, small1 = _layer_bwd(1, dy, sv1, wts1, gains[1], cos, sin, attn_sinks[1], hgrn_lb_logits)
    chips = {}

    def chips_after(tag, nms, begun):
        def hook(x):
            chips[tag] = _reduce_chips(tag, nms, begun, place, x)
            return chips[tag][4]
        return hook

    b1 = _reduce_begin("l1", names, big1)
    st0, early0, small0 = _layer_bwd_ffn(0, dx_mid, sv0, wts0, gains[0], after=b1[4], hook=chips_after("l1", names, b1))
    b0e = _reduce_begin("l0e", EARLY, early0)
    dx0, late0, small0b = _layer_bwd_mix(0, st0, sv0, wts0, gains[0], cos, sin, attn_sinks[0], hgrn_lb_logits, after=b0e[4],
                                         hook=chips_after("l0e", EARLY, b0e))
    r1, r0e = chips["l1"], chips["l0e"]
    small0 = {**small0, **small0b}
    r0l = _reduce_start("l0l", LATE, late0, place)
    red1 = _reduce_finish("l1", names, r1, place, r0l[4])
    red0 = _reduce_finish("l0e", EARLY, r0e, place, red1[names[-1]])

    loss = lax.psum(loss_part[0, 0], ("x", "y", "c"))
    grad_x = dx0[None]

    out_big = {}
    for nm in EARLY:
        out_big[nm] = adamw_big("adamw_" + nm, given[nm], given["m_" + nm], given["v_" + nm], red0[nm], red1[nm], BIG_BY_NAME[nm][3])
    red0.update(_reduce_finish("l0l", LATE, r0l, place, out_big[EARLY[-1]][3]))
    for nm in LATE:
        out_big[nm] = adamw_big("adamw_" + nm, given[nm], given["m_" + nm], given["v_" + nm], red0[nm], red1[nm], BIG_BY_NAME[nm][3])

    widths = {nm: given[nm].shape[1] for nm, _ in SMALL}
    small_g = {nm: jnp.concatenate([small0[nm][:, :widths[nm]] if nm != "attn_sinks" else small0[nm][:, :LANES],
                                    small1[nm][:, :widths[nm]] if nm != "attn_sinks" else small1[nm][:, :LANES]], axis=0)
               for nm, _ in SMALL}
    g_sum = allreduce_small("allreduce_small", _pack_small(small_g))
    sm = adamw_small("adamw_small", _pack_small({nm: given[nm] for nm, _ in SMALL}),
                     _pack_small({nm: given["m_" + nm] for nm, _ in SMALL}),
                     _pack_small({nm: given["v_" + nm] for nm, _ in SMALL}), g_sum)
    out_small = [_unpack_small(a, widths) for a in sm]

    order = ["w_in", "attn_sinks", "hgrn_lb_logits", "attn_out_gain", "hgrn_out_gain", "w_out", "pre_mix_gain", "post_mix_gain",
             "pre_ffn_gain", "post_ffn_gain", "w_ffn_gate", "w_ffn_up", "w_ffn_down", "ple_gain", "w_ple_gate", "w_ple_proj"]
    res = [loss, grad_x]
    for k in range(4):
        for nm in order:
            res.append(out_big[nm][k] if nm in out_big else out_small[k][nm])
    return tuple(res)
```

```python
import jax
import jax.numpy as jnp
from jax import lax
from jax.experimental import pallas as pl
from jax.experimental.pallas import tpu as pltpu

F32, BF16 = jnp.float32, jnp.bfloat16
SDS = jax.ShapeDtypeStruct
MESH = pl.DeviceIdType.MESH

D_MODEL = 2048
ATTN_WIDTH = 1024
HGRN_WIDTH = 1024
KV_WIDTH = 256
N_Q_HEADS = 16
N_KV_HEADS = 4
Q_PER_KV = 4
WINDOW = 128
MASK_VALUE = -1e30
ROPE_THETA = 10000.0
HGRN_HEADS = 8
HGRN_CHUNK = 16
D_FF = 5632
D_PLE = 256
RMS_EPS = 1e-6
LANES = 128
N_CHIPS = 4
COL_Q, COL_K, COL_V, COL_HQ, COL_HF, COL_HI, COL_HG = 0, 8, 10, 12, 20, 28, 36

ADAM_LR, ADAM_B1, ADAM_B2, ADAM_EPS, ADAM_WD, ADAM_STEP = 0.001, 0.9, 0.999, 1e-08, 0.01, 10

VMEM_LIMIT = 56 * 1024 * 1024
ROW_TILE = 256
ROW_TILE_WIDE = 512

_NN = (((1,), (0,)), ((), ()))
_NT = (((1,), (1,)), ((), ()))
_TN = (((0,), (0,)), ((), ()))


def _pc(body, *, name, out_shape, in_specs, out_specs, grid=(), scratch=(), sem=None, grid_spec=None, **kw):
    params = dict(vmem_limit_bytes=VMEM_LIMIT)
    if sem is not None:
        params["dimension_semantics"] = sem
    if grid_spec is not None:
        return pl.pallas_call(body, name=name, out_shape=out_shape, grid_spec=grid_spec,
                              compiler_params=pltpu.CompilerParams(**params), **kw)
    return pl.pallas_call(body, name=name, out_shape=out_shape, grid=grid, in_specs=in_specs, out_specs=out_specs,
                          scratch_shapes=list(scratch), compiler_params=pltpu.CompilerParams(**params), **kw)


def _sigmoid(x):
    return 1.0 / (1.0 + jnp.exp(-x))


def _rstd(x):
    return lax.rsqrt(jnp.mean(x * x, axis=-1, keepdims=True) + RMS_EPS)


def _rows(t, w, col=0):
    return pl.BlockSpec((t, w), lambda i, col=col: (i, col))


def _fixed(shape):
    return pl.BlockSpec(shape, lambda *_: (0,) * len(shape))


def _mm(name, a, b, *, dims, grid, a_spec, b_spec, o_spec, out_shape, parts=1, add=None, add_spec=None, after=None):
    def body(*refs):
        a_ref, b_ref, o_ref = refs[0], refs[1], refs[-1]
        if parts == 1:
            r = lax.dot_general(a_ref[...].astype(BF16), b_ref[...].astype(BF16), dims, preferred_element_type=F32)
        else:
            w = a_ref.shape[1] // parts
            r = None
            for j in range(parts):
                t = lax.dot_general(a_ref[:, j * w:(j + 1) * w].astype(BF16), b_ref[j].astype(BF16), dims,
                                    preferred_element_type=F32)
                r = t if r is None else r + t
        if add is not None:
            r = r + refs[2][...]
        o_ref[...] = r.astype(o_ref.dtype)

    ins = [a, b] + ([] if add is None else [add]) + ([] if after is None else [after])
    specs = [a_spec, b_spec] + ([] if add is None else [add_spec]) + ([] if after is None else [pl.BlockSpec(memory_space=pl.ANY)])
    return _pc(body, name=name, grid=grid, in_specs=specs, out_specs=o_spec, out_shape=out_shape,
               sem=("parallel",) * len(grid))(*ins)


def _tile(n, t):
    if n <= t:
        return n
    while n % t:
        t //= 2
    assert t % 8 == 0
    return t


def mm_col(name, a, wg, out_dtype=F32):
    s, k = a.shape
    _, _, n = wg.shape
    tm = _tile(s, 512)
    return _mm(name, a, wg, dims=_NN, grid=(N_CHIPS, s // tm),
               a_spec=pl.BlockSpec((tm, k), lambda j, i: (i, 0)),
               b_spec=pl.BlockSpec((None, k, n), lambda j, i: (j, 0, 0)),
               o_spec=pl.BlockSpec((tm, n), lambda j, i: (i, j)),
               out_shape=SDS((s, N_CHIPS * n), out_dtype))


def mm_row(name, a, wg, out_dtype=F32):
    s, _ = a.shape
    _, r, n = wg.shape
    tm = _tile(s, 512)
    tn = _tile(n, 1024 if r > 512 else 2048)
    return _mm(name, a, wg, dims=_NN, grid=(n // tn, s // tm), parts=N_CHIPS,
               a_spec=pl.BlockSpec((tm, N_CHIPS * r), lambda j, i: (i, 0)),
               b_spec=pl.BlockSpec((N_CHIPS, r, tn), lambda j, i: (0, 0, j)),
               o_spec=pl.BlockSpec((tm, tn), lambda j, i: (i, j)),
               out_shape=SDS((s, n), out_dtype))


def mm_col_t(name, dy, wg, add=None, out_dtype=F32, after=None):
    s, _ = dy.shape
    _, k, n = wg.shape
    tm = _tile(s, 512)
    tk = _tile(k, 1024)
    return _mm(name, dy, wg, dims=_NT, grid=(k // tk, s // tm), parts=N_CHIPS,
               a_spec=pl.BlockSpec((tm, N_CHIPS * n), lambda j, i: (i, 0)),
               b_spec=pl.BlockSpec((N_CHIPS, tk, n), lambda j, i: (0, j, 0)),
               o_spec=pl.BlockSpec((tm, tk), lambda j, i: (i, j)),
               add=add, add_spec=pl.BlockSpec((tm, tk), lambda j, i: (i, j)),
               out_shape=SDS((s, k), out_dtype), after=after)


def mm_row_t(name, dy, wg, out_dtype=F32):
    s, n = dy.shape
    _, r, _ = wg.shape
    tm = _tile(s, 512)

    def body(dy_ref, w_ref, o_ref):
        dyv = dy_ref[...].astype(BF16)
        for j in range(N_CHIPS):
            o_ref[:, j * r:(j + 1) * r] = lax.dot_general(dyv, w_ref[j], _NT, preferred_element_type=F32).astype(out_dtype)

    return _pc(body, name=name, grid=(s // tm,), in_specs=[_rows(tm, n), _fixed(wg.shape)], out_specs=_rows(tm, N_CHIPS * r),
               out_shape=SDS((s, N_CHIPS * r), out_dtype), sem=("parallel",))(dy, wg)


def norm_bwd_pair(name, x_a, gain_a, dy, dres, x_b, gain_b):
    s, d = x_a.shape
    t = _tile(s, ROW_TILE)

    def one(xv, g, dyv):
        r = _rstd(xv)
        xh = xv * r
        dyg = dyv * g
        return r * (dyg - xh * jnp.mean(dyg * xh, axis=-1, keepdims=True)), jnp.sum(dyv * xh, axis=0, keepdims=True)

    def body(xa_ref, ga_ref, dy_ref, r_ref, xb_ref, gb_ref, dx_ref, db_ref, dga_ref, dgb_ref):
        dx, pa = one(xa_ref[...], ga_ref[...], dy_ref[...])
        dx = dx + r_ref[...]
        dx_ref[...] = dx
        db, pb = one(xb_ref[...], gb_ref[...], dx)
        db_ref[...] = db.astype(BF16)

        @pl.when(pl.program_id(0) == 0)
        def _():
            dga_ref[...] = pa
            dgb_ref[...] = pb

        @pl.when(pl.program_id(0) > 0)
        def _():
            dga_ref[...] += pa
            dgb_ref[...] += pb

    row, gain = _rows(t, d), _fixed((1, d))
    return _pc(body, name=name, grid=(s // t,), in_specs=[row, gain, row, row, row, gain], out_specs=[row, row, gain, gain],
               out_shape=[SDS((s, d), F32), SDS((s, d), BF16), SDS((1, d), F32), SDS((1, d), F32)],
               sem=("arbitrary",))(x_a, gain_a, dy, dres, x_b, gain_b)


def ffn_gate_up(name, h, wg_gate, wg_up):
    s, k = h.shape
    _, _, n = wg_gate.shape
    tm = _tile(s, 512)

    def body(h_ref, wg_ref, wu_ref, g_ref, u_ref, a_ref):
        hv = h_ref[...]
        g = jnp.dot(hv, wg_ref[...], preferred_element_type=F32)
        u = jnp.dot(hv, wu_ref[...], preferred_element_type=F32)
        g_ref[...] = g
        u_ref[...] = u
        a_ref[...] = ((g * _sigmoid(g)) * u).astype(BF16)

    wspec = pl.BlockSpec((None, k, n), lambda j, i: (j, 0, 0))
    ospec = pl.BlockSpec((tm, n), lambda j, i: (i, j))
    return _pc(body, name=name, grid=(N_CHIPS, s // tm), in_specs=[pl.BlockSpec((tm, k), lambda j, i: (i, 0)), wspec, wspec],
               out_specs=[ospec] * 3, out_shape=[SDS((s, N_CHIPS * n), F32)] * 2 + [SDS((s, N_CHIPS * n), BF16)],
               sem=("parallel", "parallel"))(h, wg_gate, wg_up)


def ffn_down_bwd(name, df, wg_down, g, u):
    s, n = df.shape
    _, r, _ = wg_down.shape
    tm = _tile(s, 512)

    def body(df_ref, w_ref, g_ref, u_ref, dg_ref, du_ref):
        da = lax.dot_general(df_ref[...], w_ref[...], _NT, preferred_element_type=F32)
        gv = g_ref[...]
        sg = _sigmoid(gv)
        du_ref[...] = (da * (gv * sg)).astype(BF16)
        dg_ref[...] = ((da * u_ref[...]) * (sg * (1.0 + gv * (1.0 - sg)))).astype(BF16)

    cspec = pl.BlockSpec((tm, r), lambda j, i: (i, j))
    return _pc(body, name=name, grid=(N_CHIPS, s // tm),
               in_specs=[pl.BlockSpec((tm, n), lambda j, i: (i, 0)), pl.BlockSpec((None, r, n), lambda j, i: (j, 0, 0)), cspec, cspec],
               out_specs=[cspec] * 2, out_shape=[SDS((s, N_CHIPS * r), BF16)] * 2,
               sem=("parallel", "parallel"))(df, wg_down, g, u)


def mm_wg_col(name, a, dy):
    s, k = a.shape
    n = dy.shape[1] // N_CHIPS
    tm = _tile(k // 2, 512)
    hb = (k // 2) // tm
    return _mm(name, a, dy, dims=_TN, grid=(N_CHIPS, k // tm),
               a_spec=pl.BlockSpec((s, tm), lambda j, i: (0, i)),
               b_spec=pl.BlockSpec((s, n), lambda j, i: (0, j)),
               o_spec=pl.BlockSpec((None, None, tm, n), lambda j, i: (i // hb, j, i % hb, 0)),
               out_shape=SDS((2, N_CHIPS, k // 2, n), BF16))


def mm_wg_row(name, a, dy):
    s, n = dy.shape
    r = a.shape[1] // N_CHIPS
    tn = _tile(n // 2, 512)
    nb = (n // 2) // tn
    return _mm(name, a, dy, dims=_TN, grid=(N_CHIPS, n // tn),
               a_spec=pl.BlockSpec((s, r), lambda j, i: (0, j)),
               b_spec=pl.BlockSpec((s, tn), lambda j, i: (0, i)),
               o_spec=pl.BlockSpec((None, None, r, tn), lambda j, i: (i // nb, j, 0, i % nb)),
               out_shape=SDS((2, N_CHIPS, r, n // 2), BF16))


def pre_norm(name, x, gain):
    s, d = x.shape
    t = _tile(s, ROW_TILE_WIDE)

    def body(x_ref, g_ref, o_ref):
        xv = x_ref[...]
        o_ref[...] = ((xv * _rstd(xv)) * g_ref[...]).astype(BF16)

    return _pc(body, name=name, grid=(s // t,), in_specs=[_rows(t, d), _fixed((1, d))], out_specs=_rows(t, d),
               out_shape=SDS((s, d), BF16), sem=("parallel",))(x, gain)


def post_pre_norm(name, m, g_post, res, g_pre):
    s, d = m.shape
    t = _tile(s, ROW_TILE_WIDE)

    def body(m_ref, gp_ref, r_ref, gn_ref, x_ref, h_ref):
        mv = m_ref[...]
        xn = r_ref[...] + (mv * _rstd(mv)) * gp_ref[...]
        x_ref[...] = xn
        h_ref[...] = ((xn * _rstd(xn)) * gn_ref[...]).astype(BF16)

    return _pc(body, name=name, grid=(s // t,),
               in_specs=[_rows(t, d), _fixed((1, d)), _rows(t, d), _fixed((1, d))],
               out_specs=[_rows(t, d), _rows(t, d)], out_shape=[SDS((s, d), F32), SDS((s, d), BF16)],
               sem=("parallel",))(m, g_post, res, g_pre)


def _row_dot(a_ref, w_ref):
    r = w_ref.shape[1]
    out = None
    for j in range(N_CHIPS):
        part = jnp.dot(a_ref[:, j * r:(j + 1) * r], w_ref[j], preferred_element_type=F32)
        out = part if out is None else out + part
    return out


def _row_dot_specs(t, a, wg):
    return [_rows(t, a.shape[1]), _fixed(wg.shape)]


def out_proj_post_mix(name, a, wg, g_post, res, g_pre):
    s, d = res.shape
    t = _tile(s, ROW_TILE)

    def body(a_ref, w_ref, gp_ref, r_ref, gn_ref, m_ref, x_ref, h_ref):
        mv = _row_dot(a_ref, w_ref)
        m_ref[...] = mv
        xn = r_ref[...] + (mv * _rstd(mv)) * gp_ref[...]
        x_ref[...] = xn
        h_ref[...] = ((xn * _rstd(xn)) * gn_ref[...]).astype(BF16)

    return _pc(body, name=name, grid=(s // t,),
               in_specs=_row_dot_specs(t, a, wg) + [_fixed((1, d)), _rows(t, d), _fixed((1, d))],
               out_specs=[_rows(t, d)] * 3, out_shape=[SDS((s, d), F32), SDS((s, d), F32), SDS((s, d), BF16)],
               sem=("parallel",))(a, wg, g_post, res, g_pre)


def _col_dot(p_ref, w_ref):
    pv = p_ref[...].astype(BF16)
    return jnp.concatenate([jnp.dot(pv, w_ref[j], preferred_element_type=F32) for j in range(N_CHIPS)], axis=1)


def ple_gate_fwd_mid(name, a, wg, p, wg_proj, x2, g_next):
    s, d = x2.shape
    t = _tile(s, ROW_TILE)

    def body(a_ref, w_ref, p_ref, wp_ref, x_ref, g_ref, z_ref, pp_ref, xo_ref, h_ref):
        z = _row_dot(a_ref, w_ref)
        z_ref[...] = z
        pv = _col_dot(p_ref, wp_ref)
        pp_ref[...] = pv
        xn = x_ref[...] + pv * _sigmoid(z)
        xo_ref[...] = xn
        h_ref[...] = ((xn * _rstd(xn)) * g_ref[...]).astype(BF16)

    return _pc(body, name=name, grid=(s // t,),
               in_specs=_row_dot_specs(t, a, wg) + [_rows(t, p.shape[1]), _fixed(wg_proj.shape), _rows(t, d), _fixed((1, d))],
               out_specs=[_rows(t, d)] * 4, out_shape=[SDS((s, d), F32)] * 3 + [SDS((s, d), BF16)],
               sem=("parallel",))(a, wg, p, wg_proj, x2, g_next)


def ple_gate_fwd_loss(name, a, wg, p, wg_proj, x2, target):
    s, d = x2.shape
    t = _tile(s, ROW_TILE)

    def body(a_ref, w_ref, p_ref, wp_ref, x_ref, t_ref, dy_ref, dpp_ref, dz_ref, l_ref):
        gate = _sigmoid(_row_dot(a_ref, w_ref))
        pv = _col_dot(p_ref, wp_ref)
        err = (x_ref[...] + pv * gate) - t_ref[...]
        dy = err * (1.0 / d)
        dy_ref[...] = dy
        dpp_ref[...] = (dy * gate).astype(BF16)
        dz_ref[...] = ((dy * pv) * (gate * (1.0 - gate))).astype(BF16)
        part = jnp.sum(jnp.sum(err * err, axis=-1, keepdims=True), axis=0, keepdims=True) * (0.5 / d)

        @pl.when(pl.program_id(0) == 0)
        def _():
            l_ref[...] = part

        @pl.when(pl.program_id(0) > 0)
        def _():
            l_ref[...] += part

    return _pc(body, name=name, grid=(s // t,),
               in_specs=_row_dot_specs(t, a, wg) + [_rows(t, p.shape[1]), _fixed(wg_proj.shape), _rows(t, d), _rows(t, d)],
               out_specs=[_rows(t, d), _rows(t, d), _rows(t, d), _fixed((1, 1))],
               out_shape=[SDS((s, d), F32), SDS((s, d), BF16), SDS((s, d), BF16), SDS((1, 1), F32)],
               sem=("arbitrary",))(a, wg, p, wg_proj, x2, target)


ANY_SPEC = pl.BlockSpec(memory_space=pl.ANY)


def ple_bwd(name, dx3, z, pp, after):
    s, d = z.shape
    t = _tile(s, ROW_TILE_WIDE)

    def body(d_ref, z_ref, p_ref, after_ref, dpp_ref, dz_ref):
        gate = _sigmoid(z_ref[...])
        dv = d_ref[...]
        dpp_ref[...] = (dv * gate).astype(BF16)
        dz_ref[...] = ((dv * p_ref[...]) * (gate * (1.0 - gate))).astype(BF16)

    return _pc(body, name=name, grid=(s // t,), in_specs=[_rows(t, d)] * 3 + [ANY_SPEC], out_specs=[_rows(t, d)] * 2,
               out_shape=[SDS((s, d), BF16)] * 2, sem=("parallel",))(dx3, z, pp, after)


def norm_bwd(name, xin, gain, dy, dres):
    s, d = xin.shape
    t = _tile(s, ROW_TILE_WIDE)

    def body(x_ref, g_ref, dy_ref, r_ref, dx_ref, dg_ref):
        xv = x_ref[...]
        r = _rstd(xv)
        xh = xv * r
        dyv = dy_ref[...]
        dyg = dyv * g_ref[...]
        c = jnp.mean(dyg * xh, axis=-1, keepdims=True)
        dx_ref[...] = r * (dyg - xh * c) + r_ref[...]
        part = jnp.sum(dyv * xh, axis=0, keepdims=True)

        @pl.when(pl.program_id(0) == 0)
        def _():
            dg_ref[...] = part

        @pl.when(pl.program_id(0) > 0)
        def _():
            dg_ref[...] += part

    return _pc(body, name=name, grid=(s // t,), in_specs=[_rows(t, d), _fixed((1, d)), _rows(t, d), _rows(t, d)],
               out_specs=[_rows(t, d), _fixed((1, d))], out_shape=[SDS((s, d), F32), SDS((1, d), F32)],
               sem=("arbitrary",))(xin, gain, dy, dres)


def _lane(shape):
    return lax.broadcasted_iota(jnp.int32, shape, 1)


def _swap_halves(x):
    lo = (_lane(x.shape) % 64) < 32
    return jnp.where(lo, pltpu.roll(x, 96, 1), pltpu.roll(x, 32, 1))


def rope_tables(name, pos_col, inv_freq):
    s = pos_col.shape[0]
    t = _tile(s, ROW_TILE_WIDE)

    def body(p_ref, f_ref, c_ref, s_ref):
        ang = p_ref[...].astype(F32) * f_ref[...]
        lo = (_lane(ang.shape) % 64) < 32
        c_ref[...] = jnp.cos(ang)
        sn = jnp.sin(ang)
        s_ref[...] = jnp.where(lo, -sn, sn)

    return _pc(body, name=name, grid=(s // t,), in_specs=[_rows(t, 1), _fixed((1, LANES))],
               out_specs=[_rows(t, LANES)] * 2, out_shape=[SDS((s, LANES), F32)] * 2, sem=("parallel",))(pos_col, inv_freq)


def _pad_heads(chunk, lo_mask):
    zero = jnp.zeros_like(chunk)
    return jnp.where(lo_mask, chunk, zero), jnp.where(lo_mask, pltpu.roll(chunk, 64, 1), zero)


def rope_qkv(name, proj, cos, sin):
    s = proj.shape[0]
    t = _tile(s, ROW_TILE_WIDE)

    def body(q_ref, kv_ref, c_ref, s_ref, qp_ref, kp_ref, vp_ref):
        cs, sn = c_ref[...], s_ref[...]
        lo_mask = _lane(cs.shape) < 64

        def rot(x):
            return x * cs + _swap_halves(x) * sn

        for j in range(ATTN_WIDTH // LANES):
            a, b = _pad_heads(rot(q_ref[:, j * LANES:(j + 1) * LANES]), lo_mask)
            qp_ref[:, (2 * j) * LANES:(2 * j + 1) * LANES] = a.astype(BF16)
            qp_ref[:, (2 * j + 1) * LANES:(2 * j + 2) * LANES] = b.astype(BF16)
        for j in range(KV_WIDTH // LANES):
            a, b = _pad_heads(rot(kv_ref[:, j * LANES:(j + 1) * LANES]), lo_mask)
            kp_ref[:, (2 * j) * LANES:(2 * j + 1) * LANES] = a.astype(BF16)
            kp_ref[:, (2 * j + 1) * LANES:(2 * j + 2) * LANES] = b.astype(BF16)
            a, b = _pad_heads(kv_ref[:, KV_WIDTH + j * LANES:KV_WIDTH + (j + 1) * LANES], lo_mask)
            vp_ref[:, (2 * j) * LANES:(2 * j + 1) * LANES] = a.astype(BF16)
            vp_ref[:, (2 * j + 1) * LANES:(2 * j + 2) * LANES] = b.astype(BF16)

    return _pc(body, name=name, grid=(s // t,),
               in_specs=[_rows(t, ATTN_WIDTH, 0), _rows(t, 2 * KV_WIDTH, 2), _rows(t, LANES), _rows(t, LANES)],
               out_specs=[_rows(t, N_Q_HEADS * LANES), _rows(t, N_KV_HEADS * LANES), _rows(t, N_KV_HEADS * LANES)],
               out_shape=[SDS((s, N_Q_HEADS * LANES), BF16), SDS((s, N_KV_HEADS * LANES), BF16),
                          SDS((s, N_KV_HEADS * LANES), BF16)],
               sem=("parallel",))(proj, proj, cos, sin)


def _attn_mask(n):
    L = WINDOW
    qi = lax.broadcasted_iota(jnp.int32, (L, 2 * L), 0) + L
    ki = lax.broadcasted_iota(jnp.int32, (L, 2 * L), 1)
    rel = qi - ki
    return (rel >= 0) & (rel < WINDOW) & ((n > 0) | (ki >= L))


def _attn_scores(qh, kk, valid):
    sc = lax.dot_general(qh, kk, _NT, preferred_element_type=F32) * 0.125
    return jnp.where(valid, sc, MASK_VALUE)


def _attn_softmax(sc, sink):
    m = jnp.maximum(jnp.max(sc, axis=-1, keepdims=True), sink)
    e = jnp.exp(sc - m)
    es = jnp.exp(sink - m)
    den = jnp.sum(e, axis=-1, keepdims=True) + es
    return e / den, es / den


def _attn_specs(s):
    L = WINDOW
    cur = lambda n: (n, 0)
    prev = lambda n: (jnp.maximum(n - 1, 0), 0)
    kvw = N_KV_HEADS * LANES
    return [pl.BlockSpec((L, N_Q_HEADS * LANES), cur), pl.BlockSpec((L, kvw), cur), pl.BlockSpec((L, kvw), prev),
            pl.BlockSpec((L, kvw), cur), pl.BlockSpec((L, kvw), prev), pl.BlockSpec(memory_space=pltpu.SMEM)]


def attn_fwd(name, qp, kp, vp, sinks):
    s = qp.shape[0]
    L = WINDOW

    def body(q_ref, kc_ref, kp_ref, vc_ref, vp_ref, sk_ref, o_ref):
        valid = _attn_mask(pl.program_id(0))
        kks, vvs = [], []
        for kvh in range(N_KV_HEADS):
            cols = slice(kvh * LANES, (kvh + 1) * LANES)
            kks.append(jnp.concatenate([kp_ref[:, cols], kc_ref[:, cols]], axis=0))
            vvs.append(jnp.concatenate([vp_ref[:, cols], vc_ref[:, cols]], axis=0))
        scs = [_attn_scores(q_ref[:, h * LANES:(h + 1) * LANES], kks[h // Q_PER_KV], valid) for h in range(N_Q_HEADS)]
        pbs = [_attn_softmax(scs[h], sk_ref[h])[0].astype(BF16) for h in range(N_Q_HEADS)]
        outs = [jnp.dot(pbs[h], vvs[h // Q_PER_KV], preferred_element_type=F32) for h in range(N_Q_HEADS)]
        for j in range(ATTN_WIDTH // LANES):
            o_ref[:, j * LANES:(j + 1) * LANES] = outs[2 * j] + pltpu.roll(outs[2 * j + 1], 64, 1)

    return _pc(body, name=name, grid=(s // L,), in_specs=_attn_specs(s),
               out_specs=pl.BlockSpec((L, ATTN_WIDTH), lambda n: (n, 0)),
               out_shape=SDS((s, ATTN_WIDTH), F32), sem=("parallel",))(qp, kp, kp, vp, vp, sinks)


def attn_bwd(name, qp, kp, vp, sinks, dattn):
    s = qp.shape[0]
    L = WINDOW
    kvw = N_KV_HEADS * LANES

    def body(q_ref, kc_ref, kp_ref, vc_ref, vp_ref, sk_ref, do_ref, dq_ref, dkc_ref, dkp_ref, dvc_ref, dvp_ref, ds_ref):
        n = pl.program_id(0)
        valid = _attn_mask(n)
        lo_mask = _lane((L, LANES)) < 64
        lane1 = _lane((1, LANES))
        dsink = jnp.zeros((1, LANES), F32)
        heads = range(N_Q_HEADS)
        kks, vvs = [], []
        for kvh in range(N_KV_HEADS):
            cols = slice(kvh * LANES, (kvh + 1) * LANES)
            kks.append(jnp.concatenate([kp_ref[:, cols], kc_ref[:, cols]], axis=0))
            vvs.append(jnp.concatenate([vp_ref[:, cols], vc_ref[:, cols]], axis=0))
        qs, dos, scs, dps = [], [], [], []
        for h in heads:
            qs.append(q_ref[:, h * LANES:(h + 1) * LANES])
            chunk = do_ref[:, (h // 2) * LANES:(h // 2 + 1) * LANES]
            if h % 2:
                chunk = pltpu.roll(chunk, 64, 1)
            dos.append(jnp.where(lo_mask, chunk, 0.0).astype(BF16))
            scs.append(_attn_scores(qs[h], kks[h // Q_PER_KV], valid))
            dps.append(lax.dot_general(dos[h], vvs[h // Q_PER_KV], _NT, preferred_element_type=F32))
        pbs, dsbs = [], []
        for h in heads:
            p, ps = _attn_softmax(scs[h], sk_ref[h])
            delta = jnp.sum(p * dps[h], axis=-1, keepdims=True)
            dsbs.append(((p * (dps[h] - delta)) * 0.125).astype(BF16))
            pbs.append(p.astype(BF16))
            dsink = dsink + jnp.where(lane1 == h, -jnp.sum(ps * delta, axis=0, keepdims=True), 0.0)
        for kvh in range(N_KV_HEADS):
            cols = slice(kvh * LANES, (kvh + 1) * LANES)
            dkk = jnp.zeros((2 * L, LANES), F32)
            dvv = jnp.zeros((2 * L, LANES), F32)
            for h in range(kvh * Q_PER_KV, (kvh + 1) * Q_PER_KV):
                dq_ref[:, h * LANES:(h + 1) * LANES] = jnp.dot(dsbs[h], kks[kvh], preferred_element_type=F32)
                dkk = dkk + lax.dot_general(dsbs[h], qs[h], _TN, preferred_element_type=F32)
                dvv = dvv + lax.dot_general(pbs[h], dos[h], _TN, preferred_element_type=F32)
            dkp_ref[:, cols] = dkk[:L]
            dkc_ref[:, cols] = dkk[L:]
            dvp_ref[:, cols] = dvv[:L]
            dvc_ref[:, cols] = dvv[L:]

        @pl.when(n == 0)
        def _():
            ds_ref[...] = dsink

        @pl.when(n > 0)
        def _():
            ds_ref[...] += dsink

    blk = lambda w: pl.BlockSpec((L, w), lambda n: (n, 0))
    return _pc(body, name=name, grid=(s // L,), in_specs=_attn_specs(s) + [blk(ATTN_WIDTH)],
               out_specs=[blk(N_Q_HEADS * LANES), blk(kvw), blk(kvw), blk(kvw), blk(kvw), _fixed((1, LANES))],
               out_shape=[SDS((s, N_Q_HEADS * LANES), F32)] + [SDS((s, kvw), F32)] * 4 + [SDS((1, LANES), F32)],
               sem=("arbitrary",))(qp, kp, kp, vp, vp, sinks, dattn)


def rope_bwd(name, dqp, dkc, dkp, dvc, dvp, cos, sin):
    s = dqp.shape[0]
    L = WINDOW
    nb = s // L
    kvw = N_KV_HEADS * LANES

    def body(dq_ref, dkc_ref, dkp_ref, dvc_ref, dvp_ref, c_ref, s_ref, o_ref):
        cs, sn = c_ref[...], s_ref[...]
        more = (pl.program_id(0) < nb - 1).astype(F32)

        def unrot(x):
            return x * cs - _swap_halves(x) * sn

        def compact(ref, j, nxt=None):
            a = ref[:, (2 * j) * LANES:(2 * j + 1) * LANES]
            b = ref[:, (2 * j + 1) * LANES:(2 * j + 2) * LANES]
            if nxt is not None:
                a = a + more * nxt[:, (2 * j) * LANES:(2 * j + 1) * LANES]
                b = b + more * nxt[:, (2 * j + 1) * LANES:(2 * j + 2) * LANES]
            return a + pltpu.roll(b, 64, 1)

        for j in range(ATTN_WIDTH // LANES):
            o_ref[:, j * LANES:(j + 1) * LANES] = unrot(compact(dq_ref, j)).astype(BF16)
        for j in range(KV_WIDTH // LANES):
            o_ref[:, (COL_K + j) * LANES:(COL_K + j + 1) * LANES] = unrot(compact(dkc_ref, j, dkp_ref)).astype(BF16)
            o_ref[:, (COL_V + j) * LANES:(COL_V + j + 1) * LANES] = compact(dvc_ref, j, dvp_ref).astype(BF16)

    cur = lambda n: (n, 0)
    nxt = lambda n: (jnp.minimum(n + 1, nb - 1), 0)
    return _pc(body, name=name, grid=(nb,),
               in_specs=[pl.BlockSpec((L, N_Q_HEADS * LANES), cur), pl.BlockSpec((L, kvw), cur), pl.BlockSpec((L, kvw), nxt),
                         pl.BlockSpec((L, kvw), cur), pl.BlockSpec((L, kvw), nxt), pl.BlockSpec((L, LANES), cur),
                         pl.BlockSpec((L, LANES), cur)],
               out_specs=pl.BlockSpec((L, COL_HQ * LANES), cur), out_shape=SDS((s, COL_HQ * LANES), BF16),
               sem=("parallel",))(dqp, dkc, dkp, dvc, dvp, cos, sin)


def _split3(x):
    a = x.astype(BF16)
    r = x - a.astype(F32)
    b = r.astype(BF16)
    c = (r - b.astype(F32)).astype(BF16)
    return a, b, c


def _chunk_sum(x, upper):
    t = x.shape[0]
    ri = lax.broadcasted_iota(jnp.int32, (t, t), 0)
    ci = lax.broadcasted_iota(jnp.int32, (t, t), 1)
    same = (ri // HGRN_CHUNK) == (ci // HGRN_CHUNK)
    tri = (ci >= ri) if upper else (ci <= ri)
    m = jnp.where(same & tri, 1.0, 0.0).astype(BF16)
    out = None
    for part in _split3(x):
        y = jnp.dot(m, part, preferred_element_type=F32)
        out = y if out is None else out + y
    return out


def _lower_bound(l_ref, layer):
    lv = l_ref[...]
    e = jnp.exp(lv - jnp.max(lv, axis=0, keepdims=True))
    sm = e / jnp.sum(e, axis=0, keepdims=True)
    s0 = sm[0:1]
    return (s0 - s0) if layer == 0 else ((s0 + sm[1:2]) - s0)


def _hgrn_gates(hq_ref, hf_ref, lb):
    z = hf_ref[...]
    sg = _sigmoid(z)
    f = lb + (1.0 - lb) * sg
    kin = (1.0 - lb) * _sigmoid(-z)
    hq = hq_ref[...]
    sq = _sigmoid(hq)
    return sg, f, kin, hq, sq


def _shift_down(x, d):
    return x if d == 0 else pltpu.roll(x, d, 0)


def _shift_up(x, d):
    return x if d == 0 else pltpu.roll(x, x.shape[0] - d, 0)


CHUNKS_PER_BLOCK = LANES // HGRN_CHUNK


def _chunk_iotas():
    shape = (HGRN_CHUNK, LANES)
    return lax.broadcasted_iota(jnp.int32, shape, 0), lax.broadcasted_iota(jnp.int32, shape, 1)


def _chunk_rows(block, chunk):
    start = block * LANES + chunk * HGRN_CHUNK
    return slice(start, start + HGRN_CHUNK)


HGRN_HEADS_PER_STEP = 4
HGRN_STEP_WIDTH = HGRN_HEADS_PER_STEP * LANES


def _hgrn_specs(t, rev, nt):
    row = (lambda h, i: nt - 1 - i) if rev else (lambda h, i: i)
    col = lambda base: pl.BlockSpec((t, HGRN_STEP_WIDTH),
                                    lambda h, i, base=base: (row(h, i), base // HGRN_HEADS_PER_STEP + h))
    return col, row


def _head_views(refs, hh):
    return [r.at[:, pl.ds(hh * LANES, LANES)] for r in refs]


def hgrn_fwd(name, proj, lb_logits, layer):
    s = proj.shape[0]
    t = _tile(s, ROW_TILE)
    nt = s // t
    nc = t // HGRN_CHUNK
    col, row = _hgrn_specs(t, False, nt)

    def body(hq_ref, hf_ref, hi_ref, l_ref, o_ref, st_ref, state):
        @pl.when(pl.program_id(1) == 0)
        def _():
            state[...] = jnp.zeros_like(state)

        for hh in range(HGRN_HEADS_PER_STEP):
            head(*_head_views((hq_ref, hf_ref, hi_ref, l_ref, o_ref), hh), st_ref.at[:, hh], state.at[hh])

    def head(hq_ref, hf_ref, hi_ref, l_ref, o_ref, st_ref, state):
        lb = _lower_bound(l_ref, layer)
        sg, f, kin, hq, sq = _hgrn_gates(hq_ref, hf_ref, lb)
        q = hq * sq
        vb = hi_ref[...].astype(BF16)
        b = _chunk_sum(jnp.log(f), False)
        qe = (q * jnp.exp(b)).astype(BF16)
        trow, lane = _chunk_iotas()
        chunks = [(j, cc) for j in range(t // LANES) for cc in range(CHUNKS_PER_BLOCK)]
        decay, update = [], []
        for j, cc in chunks:
            rs = _chunk_rows(j, cc)
            bc = b[rs]
            bl = bc[HGRN_CHUNK - 1:HGRN_CHUNK, :]
            ke = (kin[rs] * jnp.exp(bl - bc)).astype(BF16)
            decay.append(jnp.exp(bl))
            update.append(lax.dot_general(vb[rs], ke, _TN, preferred_element_type=F32))
        st = state[...]
        for c in range(nc):
            st_ref[c] = st
            st = st * decay[c] + update[c]
        state[...] = st
        o_inter = [lax.dot_general(qe[c * HGRN_CHUNK:(c + 1) * HGRN_CHUNK], st_ref[c].astype(BF16), _NT,
                                   preferred_element_type=F32) for c in range(nc)]
        for j in range(t // LANES):
            blk = slice(j * LANES, (j + 1) * LANES)
            rows = []
            for cc in range(CHUNKS_PER_BLOCK):
                rs = _chunk_rows(j, cc)
                bc, qc, kc = b[rs], q[rs], kin[rs]
                here = trow + cc * HGRN_CHUNK
                am = jnp.where(lane == here, jnp.sum(qc * kc, axis=-1, keepdims=True), 0.0)
                for d in range(1, HGRN_CHUNK):
                    e = jnp.exp(jnp.where(trow >= d, bc - _shift_down(bc, d), MASK_VALUE))
                    a = jnp.sum((qc * _shift_down(kc, d)) * e, axis=-1, keepdims=True)
                    am = jnp.where(lane == here - d, a, am)
                rows.append(am)
            o_intra = jnp.dot(jnp.concatenate(rows, axis=0).astype(BF16), vb[blk], preferred_element_type=F32)
            for cc in range(CHUNKS_PER_BLOCK):
                rs = _chunk_rows(j, cc)
                o_ref[rs, :] = o_intra[cc * HGRN_CHUNK:(cc + 1) * HGRN_CHUNK] + o_inter[j * CHUNKS_PER_BLOCK + cc]

    hp = HGRN_HEADS_PER_STEP
    return _pc(body, name=name, grid=(HGRN_HEADS // hp, nt),
               in_specs=[col(COL_HQ), col(COL_HF), col(COL_HI), pl.BlockSpec((2, HGRN_STEP_WIDTH), lambda h, i: (0, h))],
               out_specs=[pl.BlockSpec((t, HGRN_STEP_WIDTH), lambda h, i: (i, h)),
                          pl.BlockSpec((nc, hp, LANES, LANES), lambda h, i: (i, h, 0, 0))],
               out_shape=[SDS((s, HGRN_WIDTH), F32), SDS((s // HGRN_CHUNK, HGRN_HEADS, LANES, LANES), F32)],
               scratch=[pltpu.VMEM((hp, LANES, LANES), F32)],
               sem=("parallel", "arbitrary"))(proj, proj, proj, lb_logits)


def hgrn_bwd(name, proj, lb_logits, layer, states, do):
    s = proj.shape[0]
    t = _tile(s, ROW_TILE)
    nt = s // t
    nc = t // HGRN_CHUNK
    col, row = _hgrn_specs(t, True, nt)

    def body(hq_ref, hf_ref, hi_ref, l_ref, st_ref, do_ref, dhq_ref, dhf_ref, dhi_ref, dlb_ref, dstate):
        @pl.when(pl.program_id(1) == 0)
        def _():
            dstate[...] = jnp.zeros_like(dstate)

        for hh in range(HGRN_HEADS_PER_STEP):
            hq_v, hf_v, hi_v, l_v, do_v, dhq_v, dhf_v, dhi_v, dlb_v = _head_views(
                (hq_ref, hf_ref, hi_ref, l_ref, do_ref, dhq_ref, dhf_ref, dhi_ref, dlb_ref), hh)
            head(hq_v, hf_v, hi_v, l_v, st_ref.at[:, hh], do_v, dhq_v, dhf_v, dhi_v, dlb_v, dstate.at[hh])

    def head(hq_ref, hf_ref, hi_ref, l_ref, st_ref, do_ref, dhq_ref, dhf_ref, dhi_ref, dlb_ref, dstate):
        first = pl.program_id(1) == 0
        lb = _lower_bound(l_ref, layer)
        sg, f, kin, hq, sq = _hgrn_gates(hq_ref, hf_ref, lb)
        q = hq * sq
        vb = hi_ref[...].astype(BF16)
        b = _chunk_sum(jnp.log(f), False)
        dob = do_ref[...].astype(BF16)
        eb = jnp.exp(b)
        qe = q * eb
        qeb = qe.astype(BF16)
        trow, lane = _chunk_iotas()
        last_row = trow == HGRN_CHUNK - 1

        decay, update = [None] * nc, [None] * nc
        for c in range(nc):
            rs = slice(c * HGRN_CHUNK, (c + 1) * HGRN_CHUNK)
            decay[c] = jnp.exp(b[(c + 1) * HGRN_CHUNK - 1:(c + 1) * HGRN_CHUNK, :])
            update[c] = lax.dot_general(dob[rs], qeb[rs], _TN, preferred_element_type=F32)
        dn_in = [None] * nc
        dn = dstate[...]
        for c in reversed(range(nc)):
            dn_in[c] = dn
            dn = dn * decay[c] + update[c]
        dstate[...] = dn

        dq_c, dk_c, dv_c, dbl_c = [None] * nc, [None] * nc, [None] * nc, [None] * nc
        for c in range(nc):
            rs = slice(c * HGRN_CHUNK, (c + 1) * HGRN_CHUNK)
            bc = b[rs]
            ekb = jnp.exp(bc[HGRN_CHUNK - 1:HGRN_CHUNK, :] - bc)
            ke = kin[rs] * ekb
            st = st_ref[c]
            dnb = dn_in[c].astype(BF16)
            dke = jnp.dot(vb[rs], dnb, preferred_element_type=F32)
            dq_c[c] = jnp.dot(dob[rs], st.astype(BF16), preferred_element_type=F32) * eb[rs]
            dk_c[c] = dke * ekb
            dv_c[c] = lax.dot_general(ke.astype(BF16), dnb, _NT, preferred_element_type=F32)
            dbl_c[c] = jnp.sum(dn_in[c] * st, axis=0, keepdims=True) * decay[c] + jnp.sum(dke * ke, axis=0, keepdims=True)

        db_c = [None] * nc
        for j in range(t // LANES):
            blk = slice(j * LANES, (j + 1) * LANES)
            damat = lax.dot_general(dob[blk], vb[blk], _NT, preferred_element_type=F32)
            rows = [None] * CHUNKS_PER_BLOCK
            for cc in range(CHUNKS_PER_BLOCK):
                c = j * CHUNKS_PER_BLOCK + cc
                rs = _chunk_rows(j, cc)
                bc, qc, kc = b[rs], q[rs], kin[rs]
                dam = damat[cc * HGRN_CHUNK:(cc + 1) * HGRN_CHUNK]
                here = trow + cc * HGRN_CHUNK
                on = lane == here
                da = jnp.sum(jnp.where(on, dam, 0.0), axis=-1, keepdims=True)
                am = jnp.where(on, jnp.sum(qc * kc, axis=-1, keepdims=True), 0.0)
                dq = dq_c[c] + da * kc
                dk = dk_c[c] + da * qc
                for d in range(1, HGRN_CHUNK):
                    on = lane == here - d
                    e = jnp.exp(jnp.where(trow >= d, bc - _shift_down(bc, d), MASK_VALUE))
                    kse = _shift_down(kc, d) * e
                    am = jnp.where(on, jnp.sum(qc * kse, axis=-1, keepdims=True), am)
                    da = jnp.sum(jnp.where(on, dam, 0.0), axis=-1, keepdims=True)
                    dq = dq + da * kse
                    dk = dk + _shift_up(da * (qc * e), d)
                rows[cc] = am
                dq_c[c], dk_c[c] = dq, dk
                db_c[c] = (qc * dq - kc * dk) + jnp.where(last_row, dbl_c[c], 0.0)
            dv_blk = lax.dot_general(jnp.concatenate(rows, axis=0).astype(BF16), dob[blk], _TN, preferred_element_type=F32)
            for cc in range(CHUNKS_PER_BLOCK):
                c = j * CHUNKS_PER_BLOCK + cc
                dv_c[c] = dv_c[c] + dv_blk[cc * HGRN_CHUNK:(cc + 1) * HGRN_CHUNK]
        dq = jnp.concatenate(dq_c, axis=0)
        dk = jnp.concatenate(dk_c, axis=0)
        dv = jnp.concatenate(dv_c, axis=0)
        db = jnp.concatenate(db_c, axis=0)
        dg = _chunk_sum(db, True)
        dhq_ref[...] = (dq * (sq * (1.0 + hq * (1.0 - sq)))).astype(BF16)
        dhi_ref[...] = dv.astype(BF16)
        dfk = dg / f - dk
        dhf_ref[...] = ((dfk * (1.0 - lb)) * (sg * (1.0 - sg))).astype(BF16)
        part = jnp.sum(dfk * (1.0 - sg), axis=0, keepdims=True)

        @pl.when(first)
        def _():
            dlb_ref[...] = part

        @pl.when(jnp.logical_not(first))
        def _():
            dlb_ref[...] += part

    hp = HGRN_HEADS_PER_STEP
    out_col = pl.BlockSpec((t, HGRN_STEP_WIDTH), lambda h, i: (nt - 1 - i, h))
    return _pc(body, name=name, grid=(HGRN_HEADS // hp, nt),
               in_specs=[col(COL_HQ), col(COL_HF), col(COL_HI), pl.BlockSpec((2, HGRN_STEP_WIDTH), lambda h, i: (0, h)),
                         pl.BlockSpec((nc, hp, LANES, LANES), lambda h, i: (nt - 1 - i, h, 0, 0)), out_col],
               out_specs=[out_col, out_col, out_col, pl.BlockSpec((1, HGRN_STEP_WIDTH), lambda h, i: (0, h))],
               out_shape=[SDS((s, HGRN_WIDTH), BF16)] * 3 + [SDS((1, HGRN_WIDTH), F32)],
               scratch=[pltpu.VMEM((hp, LANES, LANES), F32)],
               sem=("parallel", "arbitrary"))(proj, proj, proj, lb_logits, states, do)


def mix_out_fwd(name, attn, o, proj, g_attn, g_hgrn):
    s = attn.shape[0]
    t = _tile(s, ROW_TILE_WIDE)
    half = HGRN_WIDTH // 2

    def body(a_ref, o_ref, hg0_ref, hg1_ref, ga_ref, gh_ref, c_ref):
        av = a_ref[...]
        c_ref[:, :ATTN_WIDTH] = ((av * _rstd(av)) * ga_ref[...]).astype(BF16)
        for j in range(HGRN_HEADS):
            cols = slice(j * LANES, (j + 1) * LANES)
            ov = o_ref[:, cols]
            hg_ref, hcols = (hg0_ref, cols) if j < 4 else (hg1_ref, slice((j - 4) * LANES, (j - 3) * LANES))
            hg = hg_ref[:, hcols]
            on = (ov * _rstd(ov)) * gh_ref[:, cols]
            c_ref[:, ATTN_WIDTH + j * LANES:ATTN_WIDTH + (j + 1) * LANES] = (on * (hg * _sigmoid(hg))).astype(BF16)

    return _pc(body, name=name, grid=(s // t,),
               in_specs=[_rows(t, ATTN_WIDTH), _rows(t, HGRN_WIDTH), _rows(t, half, COL_HG // 4), _rows(t, half, COL_HG // 4 + 1),
                         _fixed((1, ATTN_WIDTH)), _fixed((1, HGRN_WIDTH))],
               out_specs=_rows(t, D_MODEL), out_shape=SDS((s, D_MODEL), BF16), sem=("parallel",))(attn, o, proj, proj, g_attn, g_hgrn)


def mix_out_bwd(name, dcat, attn, o, proj, g_attn, g_hgrn):
    s = attn.shape[0]
    t = _tile(s, ROW_TILE_WIDE)
    half = HGRN_WIDTH // 2

    def body(dc_ref, a_ref, o_ref, hg0_ref, hg1_ref, ga_ref, gh_ref, da_ref, do_ref, dhg_ref, dga_ref, dgh_ref, pa_s, ph_s):
        av = a_ref[...]
        r = _rstd(av)
        xh = av * r
        dyv = dc_ref[:, :ATTN_WIDTH]
        dyg = dyv * ga_ref[...]
        da_ref[...] = r * (dyg - xh * jnp.mean(dyg * xh, axis=-1, keepdims=True))
        pa_s[...] = jnp.sum(dyv * xh, axis=0, keepdims=True)
        for j in range(HGRN_HEADS):
            cols = slice(j * LANES, (j + 1) * LANES)
            ov = o_ref[:, cols]
            hg_ref, hcols = (hg0_ref, cols) if j < 4 else (hg1_ref, slice((j - 4) * LANES, (j - 3) * LANES))
            hg = hg_ref[:, hcols]
            sg = _sigmoid(hg)
            r = _rstd(ov)
            xh = ov * r
            gain = gh_ref[:, cols]
            dh = dc_ref[:, ATTN_WIDTH + j * LANES:ATTN_WIDTH + (j + 1) * LANES]
            dhg_ref[:, cols] = ((dh * (xh * gain)) * (sg * (1.0 + hg * (1.0 - sg)))).astype(BF16)
            dyv = dh * (hg * sg)
            dyg = dyv * gain
            do_ref[:, cols] = r * (dyg - xh * jnp.mean(dyg * xh, axis=-1, keepdims=True))
            ph_s[:, cols] = jnp.sum(dyv * xh, axis=0, keepdims=True)

        @pl.when(pl.program_id(0) == 0)
        def _():
            dga_ref[...] = pa_s[...]
            dgh_ref[...] = ph_s[...]

        @pl.when(pl.program_id(0) > 0)
        def _():
            dga_ref[...] += pa_s[...]
            dgh_ref[...] += ph_s[...]

    return _pc(body, name=name, grid=(s // t,),
               in_specs=[_rows(t, D_MODEL), _rows(t, ATTN_WIDTH), _rows(t, HGRN_WIDTH), _rows(t, half, COL_HG // 4),
                         _rows(t, half, COL_HG // 4 + 1), _fixed((1, ATTN_WIDTH)), _fixed((1, HGRN_WIDTH))],
               out_specs=[_rows(t, ATTN_WIDTH), _rows(t, HGRN_WIDTH), _rows(t, HGRN_WIDTH), _fixed((1, ATTN_WIDTH)),
                          _fixed((1, HGRN_WIDTH))],
               out_shape=[SDS((s, ATTN_WIDTH), F32), SDS((s, HGRN_WIDTH), F32), SDS((s, HGRN_WIDTH), BF16),
                          SDS((1, ATTN_WIDTH), F32), SDS((1, HGRN_WIDTH), F32)],
               scratch=[pltpu.VMEM((1, ATTN_WIDTH), F32), pltpu.VMEM((1, HGRN_WIDTH), F32)],
               sem=("arbitrary",))(dcat, attn, o, proj, proj, g_attn, g_hgrn)


BIG = (("w_in", 2048, 1408, 0), ("w_out", 512, 2048, 1), ("w_ffn_gate", 2048, 1408, 0), ("w_ffn_up", 2048, 1408, 0),
       ("w_ffn_down", 1408, 2048, 1), ("w_ple_gate", 512, 2048, 1), ("w_ple_proj", 256, 512, 0))
BIG_BY_NAME = {spec[0]: spec for spec in BIG}
HBM_SPEC = pl.BlockSpec(memory_space=pltpu.HBM)
SEM_SPEC = pl.BlockSpec(memory_space=pltpu.SEMAPHORE)
TOKEN_SHAPE = (8, LANES)


def _split_call(body, *, name, in_specs, out_specs, out_shape, aliases):
    return pl.pallas_call(body, name=name, in_specs=in_specs, out_specs=out_specs, out_shape=out_shape,
                          input_output_aliases=aliases,
                          compiler_params=pltpu.CompilerParams(has_side_effects=pltpu.SideEffectType.DATAFLOW_SIDE_EFFECTING))


def _in_hbm(arrays):
    return [pltpu.with_memory_space_constraint(a, pltpu.HBM) for a in arrays]


GROUP_STEPS = 8


def _step_rows(rows):
    assert rows % (GROUP_STEPS * 16) == 0
    return rows // GROUP_STEPS


def cast_to_slots(name, place, ws, layer, after):
    nt = len(ws)

    def body(place_ref, *refs):
        for t in range(nt):
            refs[nt + 1 + t][...] = refs[t][...].astype(BF16)

    in_specs, out_specs = [], []
    for w in ws:
        block = (None, _step_rows(w.shape[1]), w.shape[2])
        in_specs.append(pl.BlockSpec(block, lambda i, pr: (layer, i, 0)))
        out_specs.append(pl.BlockSpec(block, lambda i, pr: (pr[1], i, 0)))
    gs = pltpu.PrefetchScalarGridSpec(num_scalar_prefetch=1, grid=(GROUP_STEPS,), in_specs=in_specs + [ANY_SPEC],
                                      out_specs=out_specs)
    return _pc(body, name=name, grid_spec=gs, in_specs=None, out_specs=None,
               out_shape=[SDS((N_CHIPS,) + w.shape[1:], BF16) for w in ws], sem=("parallel",))(place, *ws, after)


def _place():
    x, y, c = lax.axis_index("x"), lax.axis_index("y"), lax.axis_index("c")
    chips = [(1 - x, y), (x, 1 - y), (1 - x, 1 - y)]
    return x, y, c, chips


def _half(ref, axis, c, rows, cols):
    if axis == 0:
        return ref.at[pl.ds(pl.multiple_of(c * (rows // 2), 16), rows // 2), :]
    return ref.at[:, pl.ds(pl.multiple_of(c * (cols // 2), LANES), cols // 2)]


def _gather_copies(specs, bufs, send, recv):
    x, y, c, chips = _place()
    cps = []
    for t, (_, rows, cols, axis) in enumerate(specs):
        mine = _half(bufs[t].at[2 * x + y], axis, c, rows, cols)
        for k, (cx, cy) in enumerate(chips):
            cps.append(pltpu.make_async_remote_copy(src_ref=mine, dst_ref=mine, send_sem=send.at[3 * t + k],
                                                    recv_sem=recv.at[3 * t + k], device_id=(cx, cy, c), device_id_type=MESH))
    return cps


def gather_start(name, specs, bufs, after):
    nt = len(bufs)
    n = 3 * nt

    def body(*refs):
        send, recv, token = refs[nt + 1], refs[nt + 2], refs[-1]
        for cp in _gather_copies(specs, refs[:nt], send, recv):
            cp.start()
        token[...] = jnp.zeros(TOKEN_SHAPE, F32)

    out = _split_call(
        body, name=name, in_specs=[HBM_SPEC] * nt + [ANY_SPEC],
        out_specs=(SEM_SPEC, SEM_SPEC) + (HBM_SPEC,) * nt + (pl.BlockSpec(memory_space=pltpu.VMEM),),
        out_shape=(pltpu.SemaphoreType.DMA((n,)), pltpu.SemaphoreType.DMA((n,)))
        + tuple(pltpu.HBM(b.shape, b.dtype) for b in bufs) + (SDS(TOKEN_SHAPE, F32),),
        aliases={t: 2 + t for t in range(nt)})(*_in_hbm(bufs), after)
    return out[0], out[1], list(out[2:2 + nt]), out[-1]


def gather_wait(name, specs, send, recv, bufs, after):
    nt = len(bufs)

    def body(*refs):
        for cp in _gather_copies(specs, refs[:nt], refs[nt], refs[nt + 1]):
            cp.wait_send()
            cp.wait_recv()

    out = _split_call(
        body, name=name, in_specs=[HBM_SPEC] * nt + [SEM_SPEC, SEM_SPEC, pl.BlockSpec(memory_space=pl.ANY)],
        out_specs=(HBM_SPEC,) * nt, out_shape=tuple(pltpu.HBM(b.shape, b.dtype) for b in bufs),
        aliases={t: t for t in range(nt)})(*bufs, send, recv, after)
    return list(out)


def gather_pass(name, specs, bufs):
    nt = len(bufs)

    def body(*refs):
        ins, outs = refs[:nt], refs[nt:2 * nt]
        send, recv = refs[2 * nt:]
        x, y, c, chips = _place()
        cps = []
        for t, (_, rows, cols, axis) in enumerate(specs):
            for k, (cx, cy) in enumerate(chips):
                cp = pltpu.make_async_remote_copy(
                    src_ref=_half(ins[t].at[2 * cx + cy], axis, c, rows, cols),
                    dst_ref=_half(outs[t].at[2 * cx + cy], axis, c, rows, cols),
                    send_sem=send.at[3 * t + k], recv_sem=recv.at[3 * t + k], device_id=(x, y, 1 - c), device_id_type=MESH)
                cp.start()
                cps.append(cp)
        for t, (_, rows, cols, axis) in enumerate(specs):
            for k, (cx, cy) in enumerate(chips):
                theirs = _half(outs[t].at[2 * cx + cy], axis, 1 - c, rows, cols)
                pltpu.make_async_remote_copy(src_ref=theirs, dst_ref=theirs, send_sem=send.at[3 * t + k],
                                             recv_sem=recv.at[3 * t + k], device_id=(x, y, 1 - c), device_id_type=MESH).wait_recv()
        for cp in cps:
            cp.wait_send()

    return _pc(body, name=name, in_specs=[HBM_SPEC] * nt, out_specs=[HBM_SPEC] * nt,
               out_shape=[SDS(b.shape, b.dtype) for b in bufs], scratch=[pltpu.SemaphoreType.DMA((3 * nt,))] * 2,
               input_output_aliases={t: t for t in range(nt)})(*bufs)


def reduce_to_sibling(name, grads):
    nt = len(grads)

    def body(*refs):
        srcs, dsts = refs[:nt], refs[nt:2 * nt]
        send, recv = refs[2 * nt:]
        x, y, c, _ = _place()
        cps = []
        for t in range(nt):
            cp = pltpu.make_async_remote_copy(src_ref=srcs[t].at[1 - c], dst_ref=dsts[t], send_sem=send.at[t],
                                              recv_sem=recv.at[t], device_id=(x, y, 1 - c), device_id_type=MESH)
            cp.start()
            cps.append(cp)
        for cp in cps:
            cp.wait()

    return _pc(body, name=name, in_specs=[HBM_SPEC] * nt, out_specs=[HBM_SPEC] * nt,
               out_shape=[SDS(g.shape[1:], g.dtype) for g in grads],
               scratch=[pltpu.SemaphoreType.DMA((nt,))] * 2)(*grads)


def _sibling_copies(grads, lands, send, recv):
    x, y, c, _ = _place()
    return [pltpu.make_async_remote_copy(src_ref=grads[t].at[1 - c], dst_ref=lands[t], send_sem=send.at[t], recv_sem=recv.at[t],
                                         device_id=(x, y, 1 - c), device_id_type=MESH) for t in range(len(grads))]


def sibling_start(name, grads):
    nt = len(grads)
    lands = [lax.empty(g.shape[1:], g.dtype) for g in grads]

    def body(*refs):
        send, recv, token = refs[2 * nt], refs[2 * nt + 1], refs[-1]
        for cp in _sibling_copies(refs[:nt], refs[nt:2 * nt], send, recv):
            cp.start()
        token[...] = jnp.zeros(TOKEN_SHAPE, F32)

    both = list(grads) + lands
    out = _split_call(
        body, name=name, in_specs=[HBM_SPEC] * (2 * nt),
        out_specs=(SEM_SPEC, SEM_SPEC) + (HBM_SPEC,) * (2 * nt) + (pl.BlockSpec(memory_space=pltpu.VMEM),),
        out_shape=(pltpu.SemaphoreType.DMA((nt,)), pltpu.SemaphoreType.DMA((nt,)))
        + tuple(pltpu.HBM(b.shape, b.dtype) for b in both) + (SDS(TOKEN_SHAPE, F32),),
        aliases={t: 2 + t for t in range(2 * nt)})(*_in_hbm(both))
    return out[0], out[1], list(out[2:2 + nt]), list(out[2 + nt:2 + 2 * nt]), out[-1]


def sibling_wait(name, send, recv, grads, lands, after):
    nt = len(grads)

    def body(*refs):
        for cp in _sibling_copies(refs[:nt], refs[nt:2 * nt], refs[2 * nt], refs[2 * nt + 1]):
            cp.wait_send()
            cp.wait_recv()

    both = list(grads) + list(lands)
    out = _split_call(
        body, name=name, in_specs=[HBM_SPEC] * (2 * nt) + [SEM_SPEC, SEM_SPEC, pl.BlockSpec(memory_space=pl.ANY)],
        out_specs=(HBM_SPEC,) * (2 * nt), out_shape=tuple(pltpu.HBM(b.shape, b.dtype) for b in both),
        aliases={t: t for t in range(2 * nt)})(*both, send, recv, after)
    return list(out[:nt]), list(out[nt:])


def add_halves(name, place, grads, gots):
    nt = len(grads)

    def body(place_ref, *refs):
        for t in range(nt):
            val = (refs[t][...].astype(F32) + refs[nt + t][...].astype(F32)).astype(BF16)
            refs[2 * nt + t][...] = val

            @pl.when(pl.program_id(1) == place_ref[1])
            def _():
                refs[3 * nt + t][...] = val

    g_specs, o_specs, part_specs, slot_specs = [], [], [], []
    for g in grads:
        _, n, r, c = g.shape
        tr = _step_rows(r)
        g_specs.append(pl.BlockSpec((None, None, tr, c), lambda i, j, pr: (pr[0], j, i, 0)))
        o_specs.append(pl.BlockSpec((None, tr, c), lambda i, j, pr: (j, i, 0)))
        part_specs.append(pl.BlockSpec((None, tr, c), lambda i, j, pr: (j, i, 0)))
        slot_specs.append(pl.BlockSpec((None, tr, c), lambda i, j, pr: (pr[1], i, 0)))
    gs = pltpu.PrefetchScalarGridSpec(num_scalar_prefetch=1, grid=(GROUP_STEPS, N_CHIPS), in_specs=g_specs + o_specs,
                                      out_specs=part_specs + slot_specs)
    out = _pc(body, name=name, grid_spec=gs, in_specs=None, out_specs=None,
              out_shape=[SDS(g.shape[1:], BF16) for g in grads] * 2, sem=("parallel", "arbitrary"))(place, *grads, *gots)
    return list(out[:nt]), list(out[nt:])


def _chips_copies(parts, slots, send, recv):
    x, y, c, chips = _place()
    cps = []
    for t in range(len(parts)):
        for k, (cx, cy) in enumerate(chips):
            cps.append(pltpu.make_async_remote_copy(src_ref=parts[t].at[2 * cx + cy], dst_ref=slots[t].at[2 * x + y],
                                                    send_sem=send.at[3 * t + k], recv_sem=recv.at[3 * t + k],
                                                    device_id=(cx, cy, c), device_id_type=MESH))
    return cps


def chips_start(name, parts, slots):
    nt = len(parts)
    n = 3 * nt

    def body(*refs):
        send, recv, token = refs[2 * nt], refs[2 * nt + 1], refs[-1]
        for cp in _chips_copies(refs[:nt], refs[nt:2 * nt], send, recv):
            cp.start()
        token[...] = jnp.zeros(TOKEN_SHAPE, F32)

    both = list(parts) + list(slots)
    out = _split_call(
        body, name=name, in_specs=[HBM_SPEC] * (2 * nt),
        out_specs=(SEM_SPEC, SEM_SPEC) + (HBM_SPEC,) * (2 * nt) + (pl.BlockSpec(memory_space=pltpu.VMEM),),
        out_shape=(pltpu.SemaphoreType.DMA((n,)), pltpu.SemaphoreType.DMA((n,)))
        + tuple(pltpu.HBM(b.shape, b.dtype) for b in both) + (SDS(TOKEN_SHAPE, F32),),
        aliases={t: 2 + t for t in range(2 * nt)})(*_in_hbm(both))
    return out[0], out[1], list(out[2:2 + nt]), list(out[2 + nt:2 + 2 * nt]), out[-1]


def chips_wait(name, send, recv, parts, slots, after):
    nt = len(parts)

    def body(*refs):
        for cp in _chips_copies(refs[:nt], refs[nt:2 * nt], refs[2 * nt], refs[2 * nt + 1]):
            cp.wait_send()
            cp.wait_recv()

    both = list(parts) + list(slots)
    out = _split_call(
        body, name=name, in_specs=[HBM_SPEC] * (2 * nt) + [SEM_SPEC, SEM_SPEC, pl.BlockSpec(memory_space=pl.ANY)],
        out_specs=(HBM_SPEC,) * (2 * nt), out_shape=tuple(pltpu.HBM(b.shape, b.dtype) for b in both),
        aliases={t: t for t in range(2 * nt)})(*both, send, recv, after)
    return list(out[nt:])


def sum_chips(name, place, slots):
    nt = len(slots)

    def body(place_ref, *refs):
        for t in range(nt):
            s_ref = refs[t]
            acc = s_ref[0].astype(F32)
            for k in range(1, N_CHIPS):
                acc = acc + s_ref[k].astype(F32)
            refs[nt + t][...] = acc

    in_specs, out_specs = [], []
    for sl in slots:
        n, r, c = sl.shape
        tr = _step_rows(r)
        in_specs.append(pl.BlockSpec((n, tr, c), lambda i, pr: (0, i, 0)))
        out_specs.append(pl.BlockSpec((None, tr, c), lambda i, pr: (pr[0], i, 0)))
    gs = pltpu.PrefetchScalarGridSpec(num_scalar_prefetch=1, grid=(GROUP_STEPS,), in_specs=in_specs, out_specs=out_specs)
    return list(_pc(body, name=name, grid_spec=gs, in_specs=None, out_specs=None,
                    out_shape=[SDS((2,) + sl.shape[1:], F32) for sl in slots], sem=("parallel",))(place, *slots))


def share_with_sibling(name, bufs):
    nt = len(bufs)

    def body(*refs):
        ins, outs = refs[:nt], refs[nt:2 * nt]
        send, recv = refs[2 * nt:]
        x, y, c, _ = _place()
        cps = []
        for t in range(nt):
            cp = pltpu.make_async_remote_copy(src_ref=ins[t].at[c], dst_ref=outs[t].at[c], send_sem=send.at[t], recv_sem=recv.at[t],
                                              device_id=(x, y, 1 - c), device_id_type=MESH)
            cp.start()
            cps.append(cp)
        for t in range(nt):
            theirs = outs[t].at[1 - c]
            pltpu.make_async_remote_copy(src_ref=theirs, dst_ref=theirs, send_sem=send.at[t], recv_sem=recv.at[t],
                                         device_id=(x, y, 1 - c), device_id_type=MESH).wait_recv()
        for cp in cps:
            cp.wait_send()

    return _pc(body, name=name, in_specs=[HBM_SPEC] * nt, out_specs=[HBM_SPEC] * nt,
               out_shape=[SDS(b.shape, F32) for b in bufs], scratch=[pltpu.SemaphoreType.DMA((nt,))] * 2,
               input_output_aliases={t: t for t in range(nt)})(*bufs)


def _adamw(w, g, m, v):
    m = ADAM_B1 * m + (1.0 - ADAM_B1) * g
    v = ADAM_B2 * v + (1.0 - ADAM_B2) * (g * g)
    m_hat = m / (1.0 - ADAM_B1 ** ADAM_STEP)
    v_hat = v / (1.0 - ADAM_B2 ** ADAM_STEP)
    delta = -ADAM_LR * (m_hat / (jnp.sqrt(v_hat) + ADAM_EPS) + ADAM_WD * w)
    return delta, m, v


def adamw_big(name, w, m, v, g0, g1, axis):
    _, r, c = w.shape
    _, rh, ch = g0.shape
    tr = _tile(rh, 256)
    nb = rh // tr
    if axis == 0:
        wspec = pl.BlockSpec((None, tr, ch), lambda l, h, i: (l, h * nb + i, 0))
    else:
        wspec = pl.BlockSpec((None, tr, ch), lambda l, h, i: (l, i, h))
    g0spec = pl.BlockSpec((None, tr, ch), lambda l, h, i: (h * (1 - l), i * (1 - l), 0))
    g1spec = pl.BlockSpec((None, tr, ch), lambda l, h, i: (h * l, i * l, 0))

    def body(w_ref, m_ref, v_ref, g0_ref, g1_ref, go_ref, d_ref, mo_ref, vo_ref):
        def run(g_ref):
            g = g_ref[...]
            delta, mn, vn = _adamw(w_ref[...], g, m_ref[...], v_ref[...])
            go_ref[...] = g
            d_ref[...] = delta
            mo_ref[...] = mn
            vo_ref[...] = vn

        @pl.when(pl.program_id(0) == 0)
        def _():
            run(g0_ref)

        @pl.when(pl.program_id(0) == 1)
        def _():
            run(g1_ref)

    return _pc(body, name=name, grid=(2, 2, nb), in_specs=[wspec, wspec, wspec, g0spec, g1spec], out_specs=[wspec] * 4,
               out_shape=[SDS(w.shape, F32)] * 4, sem=("parallel", "parallel", "parallel"))(w, m, v, g0, g1)


SMALL = (("pre_mix_gain", 2048), ("post_mix_gain", 2048), ("pre_ffn_gain", 2048), ("post_ffn_gain", 2048), ("ple_gain", 2048),
         ("attn_out_gain", 1024), ("hgrn_out_gain", 1024), ("hgrn_lb_logits", 1024), ("attn_sinks", 128))
SMALL_ROWS = sum(2 * w // LANES for _, w in SMALL)
SMALL_PAD = -(-SMALL_ROWS // 8) * 8
LB_ROW = sum(2 * w // LANES for _, w in SMALL[:7])


def _pack_small(parts):
    rows = []
    for nm, w in SMALL:
        a = parts[nm].astype(F32)
        if a.shape[1] != w:
            a = jnp.pad(a, ((0, 0), (0, w - a.shape[1])))
        rows.append(a.reshape(2 * w // LANES, LANES))
    rows.append(jnp.zeros((SMALL_PAD - SMALL_ROWS, LANES), F32))
    return jnp.concatenate(rows, axis=0)


def _unpack_small(packed, widths):
    out, r = {}, 0
    for nm, w in SMALL:
        n = 2 * w // LANES
        out[nm] = packed[r:r + n].reshape(2, w)[:, :widths[nm]]
        r += n
    return out


def allreduce_small(name, packed):
    rows = packed.shape[0]

    def body(x_ref, o_ref, buf, send, recv, own_sem):
        x, y, c, _ = _place()
        me = 4 * x + 2 * y + c
        own = pltpu.make_async_copy(x_ref, buf.at[me], own_sem)
        own.start()
        cps = []
        for k in range(1, 8):
            px, py, pc = x ^ (k >> 2), y ^ ((k >> 1) & 1), c ^ (k & 1)
            cp = pltpu.make_async_remote_copy(src_ref=x_ref, dst_ref=buf.at[me], send_sem=send.at[k - 1], recv_sem=recv.at[k - 1],
                                              device_id=(px, py, pc), device_id_type=MESH)
            cp.start()
            cps.append(cp)
        for k in range(1, 8):
            px, py, pc = x ^ (k >> 2), y ^ ((k >> 1) & 1), c ^ (k & 1)
            slot = buf.at[4 * px + 2 * py + pc]
            pltpu.make_async_remote_copy(src_ref=slot, dst_ref=slot, send_sem=send.at[k - 1], recv_sem=recv.at[k - 1],
                                         device_id=(px, py, pc), device_id_type=MESH).wait_recv()
        for cp in cps:
            cp.wait_send()
        own.wait()
        acc = buf[0]
        for k in range(1, 8):
            acc = acc + buf[k]
        o_ref[...] = acc

    vm = pl.BlockSpec(memory_space=pltpu.VMEM)
    return _pc(body, name=name, in_specs=[vm], out_specs=vm, out_shape=SDS((rows, LANES), F32),
               scratch=[pltpu.VMEM((8, rows, LANES), F32), pltpu.SemaphoreType.DMA((7,)), pltpu.SemaphoreType.DMA((7,)),
                        pltpu.SemaphoreType.DMA])(packed)


def adamw_small(name, w, m, v, g):
    rows = w.shape[0]
    n = HGRN_WIDTH // LANES

    def body(w_ref, m_ref, v_ref, g_ref, go_ref, d_ref, mo_ref, vo_ref):
        go_ref[...] = g_ref[...]
        l0 = w_ref[LB_ROW:LB_ROW + n, :]
        l1 = w_ref[LB_ROW + n:LB_ROW + 2 * n, :]
        mx = jnp.maximum(l0, l1)
        e0, e1 = jnp.exp(l0 - mx), jnp.exp(l1 - mx)
        s0, s1 = e0 / (e0 + e1), e1 / (e0 + e1)
        dlb1 = g_ref[LB_ROW + n:LB_ROW + 2 * n, :]
        inner = s1 * dlb1
        go_ref[LB_ROW:LB_ROW + n, :] = s0 * (0.0 - inner)
        go_ref[LB_ROW + n:LB_ROW + 2 * n, :] = s1 * (dlb1 - inner)
        delta, mn, vn = _adamw(w_ref[...], go_ref[...], m_ref[...], v_ref[...])
        d_ref[...] = delta
        mo_ref[...] = mn
        vo_ref[...] = vn

    vm = pl.BlockSpec(memory_space=pltpu.VMEM)
    return _pc(body, name=name, in_specs=[vm] * 4, out_specs=[vm] * 4, out_shape=[SDS((rows, LANES), F32)] * 4)(w, m, v, g)


def _layer_fwd(l, x, h1, p_l, w_in_g, rest_of_weights, gains, cos, sin, sinks, lb_logits, g_next, target):
    n = f"l{l}_"
    proj = mm_col(n + "in_proj", h1, w_in_g)
    qp, kp, vp = rope_qkv(n + "rope_qkv", proj, cos, sin)
    attn = attn_fwd(n + "attn_fwd", qp, kp, vp, sinks)
    o, states = hgrn_fwd(n + "hgrn_fwd", proj, lb_logits, l)
    cat = mix_out_fwd(n + "mix_out_fwd", attn, o, proj, gains["attn_out_gain"], gains["hgrn_out_gain"])
    rest, token = rest_of_weights(cat)
    wts = dict(rest, w_in=w_in_g)
    if token is not None:
        gains = _with_token(gains, "post_mix_gain", token)
    m, x1, h2 = out_proj_post_mix(n + "out_proj_post_mix", cat, wts["w_out"], gains["post_mix_gain"], x, gains["pre_ffn_gain"])
    g, u, a = ffn_gate_up(n + "ffn_gate_up", h2, wts["w_ffn_gate"], wts["w_ffn_up"])
    f = mm_row(n + "ffn_down", a, wts["w_ffn_down"])
    x2, h3 = post_pre_norm(n + "post_ffn", f, gains["post_ffn_gain"], x1, gains["ple_gain"])
    saved = dict(x=x, h1=h1, proj=proj, qp=qp, kp=kp, vp=vp, attn=attn, o=o, states=states, cat=cat, m=m, x1=x1, h2=h2,
                 g=g, u=u, a=a, f=f, x2=x2, h3=h3, p=p_l)
    if target is None:
        z, pp, *out = ple_gate_fwd_mid(n + "ple_gate_fwd", h3, wts["w_ple_gate"], p_l, wts["w_ple_proj"], x2, g_next)
        saved.update(z=z, pp=pp)
    else:
        dy, dpp, dz, loss = ple_gate_fwd_loss(n + "ple_gate_loss", h3, wts["w_ple_gate"], p_l, wts["w_ple_proj"], x2, target)
        out = [dy, loss]
        saved.update(dpp=dpp, dz=dz)
    return out, saved, wts


EARLY = ("w_ple_gate", "w_ple_proj", "w_ffn_down", "w_ffn_gate", "w_ffn_up")
LATE = ("w_out", "w_in")


def _layer_bwd_ffn(l, dx3, sv, wts, gains, after=None, hook=None):
    n = f"l{l}_"
    dpp, dz = (sv["dpp"], sv["dz"]) if "dz" in sv else ple_bwd(n + "ple_bwd", dx3, sv["z"], sv["pp"],
                                                                 dx3 if after is None else after)
    dh3 = mm_row_t(n + "ple_gate_dx", dz, wts["w_ple_gate"])
    if hook is not None:
        gains = _with_token(gains, "ple_gain", hook(dh3))
    dx2, df, d_ple_gain, d_post_ffn = norm_bwd_pair(n + "ple_post_ffn_bwd", sv["x2"], gains["ple_gain"], dh3, dx3, sv["f"],
                                                    gains["post_ffn_gain"])
    dg, du = ffn_down_bwd(n + "ffn_down_bwd", df, wts["w_ffn_down"], sv["g"], sv["u"])
    big = dict(
        w_ple_gate=mm_wg_row(n + "ple_gate_dw", sv["h3"], dz),
        w_ple_proj=mm_wg_col(n + "ple_proj_dw", sv["p"], dpp),
        w_ffn_down=mm_wg_row(n + "ffn_down_dw", sv["a"], df),
        w_ffn_gate=mm_wg_col(n + "ffn_gate_dw", sv["h2"], dg),
        w_ffn_up=mm_wg_col(n + "ffn_up_dw", sv["h2"], du),
    )
    return dict(dx2=dx2, dg=dg, du=du), big, dict(ple_gain=d_ple_gain, post_ffn_gain=d_post_ffn)


def _layer_bwd_mix(l, st, sv, wts, gains, cos, sin, sinks, lb_logits, after=None, hook=None):
    n = f"l{l}_"
    dh2 = mm_col_t(n + "ffn_gate_dx", st["dg"], wts["w_ffn_gate"], after=after)
    dh2 = mm_col_t(n + "ffn_up_dx", st["du"], wts["w_ffn_up"], add=dh2)
    if hook is not None:
        gains = _with_token(gains, "pre_ffn_gain", hook(dh2))
    dx1, dm, d_pre_ffn, d_post_mix = norm_bwd_pair(n + "pre_ffn_post_mix_bwd", sv["x1"], gains["pre_ffn_gain"], dh2, st["dx2"],
                                                   sv["m"], gains["post_mix_gain"])
    dcat = mm_row_t(n + "out_proj_dx", dm, wts["w_out"])
    dattn, do, dhg, d_attn_gain, d_hgrn_gain = mix_out_bwd(n + "mix_out_bwd", dcat, sv["attn"], sv["o"], sv["proj"],
                                                            gains["attn_out_gain"], gains["hgrn_out_gain"])
    dqp, dkc, dkp, dvc, dvp, dsinks = attn_bwd(n + "attn_bwd", sv["qp"], sv["kp"], sv["vp"], sinks, dattn)
    dqkv = rope_bwd(n + "rope_bwd", dqp, dkc, dkp, dvc, dvp, cos, sin)
    dhq, dhf, dhi, dlb = hgrn_bwd(n + "hgrn_bwd", sv["proj"], lb_logits, l, sv["states"], do)
    dproj = jnp.concatenate([dqkv, dhq, dhf, dhi, dhg], axis=1)
    dh1 = mm_col_t(n + "in_proj_dx", dproj, wts["w_in"])
    dx, d_pre_mix = norm_bwd(n + "pre_mix_bwd", sv["x"], gains["pre_mix_gain"], dh1, dx1)
    big = dict(w_out=mm_wg_row(n + "out_proj_dw", sv["cat"], dm), w_in=mm_wg_col(n + "in_proj_dw", sv["h1"], dproj))
    small = dict(pre_mix_gain=d_pre_mix, post_mix_gain=d_post_mix, pre_ffn_gain=d_pre_ffn, attn_out_gain=d_attn_gain,
                 hgrn_out_gain=d_hgrn_gain, hgrn_lb_logits=dlb, attn_sinks=dsinks)
    return dx, big, small


def _reduce_start(tag, names, big, place):
    grads = [big[nm] for nm in names]
    got = reduce_to_sibling(tag + "_reduce_to_sibling", grads)
    parts, slots = add_halves(tag + "_add", place, grads, got)
    return chips_start(tag + "_chips_start", parts, slots)


def _reduce_begin(tag, names, big):
    return sibling_start(tag + "_sibling_start", [big[nm] for nm in names])


def _reduce_chips(tag, names, begun, place, after):
    send, recv, grads, lands, _ = begun
    grads, got = sibling_wait(tag + "_sibling_wait", send, recv, grads, lands, after)
    parts, slots = add_halves(tag + "_add", place, grads, got)
    return chips_start(tag + "_chips_start", parts, slots)


def _reduce_finish(tag, names, started, place, after):
    send, recv, parts, slots, _ = started
    slots = chips_wait(tag + "_chips_wait", send, recv, parts, slots, after)
    bufs = sum_chips(tag + "_sum", place, slots)
    return dict(zip(names, share_with_sibling(tag + "_share_with_sibling", bufs)))


def _with_token(gains, name, token):
    out = dict(gains)
    out[name] = gains[name] + token[0, 0]
    return out


def kernel(x, p, positions, w_in, attn_sinks, hgrn_lb_logits, attn_out_gain, hgrn_out_gain, w_out, pre_mix_gain, post_mix_gain, pre_ffn_gain, post_ffn_gain, w_ffn_gate, w_ffn_up, w_ffn_down, ple_gain, w_ple_gate, w_ple_proj, loss_target, m_w_in, m_attn_sinks, m_hgrn_lb_logits, m_attn_out_gain, m_hgrn_out_gain, m_w_out, m_pre_mix_gain, m_post_mix_gain, m_pre_ffn_gain, m_post_ffn_gain, m_w_ffn_gate, m_w_ffn_up, m_w_ffn_down, m_ple_gain, m_w_ple_gate, m_w_ple_proj, v_w_in, v_attn_sinks, v_hgrn_lb_logits, v_attn_out_gain, v_hgrn_out_gain, v_w_out, v_pre_mix_gain, v_post_mix_gain, v_pre_ffn_gain, v_post_ffn_gain, v_w_ffn_gate, v_w_ffn_up, v_w_ffn_down, v_ple_gain, v_w_ple_gate, v_w_ple_proj):
    given = dict(locals())
    depth = 2
    place = jnp.stack([lax.axis_index("c"), 2 * lax.axis_index("x") + lax.axis_index("y")]).astype(jnp.int32)
    xs = x[0]
    tgt = loss_target[0]
    pos_col = positions.reshape(-1, 1)
    half = 32
    inv_freq = ROPE_THETA ** (-jnp.arange(half, dtype=F32) / half)
    inv_freq = jnp.tile(inv_freq, 4).reshape(1, LANES)
    gains = [{nm: given[nm][l:l + 1] for nm, _ in SMALL[:7]} for l in range(depth)]
    names = [nm for nm, *_ in BIG]
    first, others = names[:1], names[1:]

    def specs(nms):
        return [BIG_BY_NAME[nm] for nm in nms]

    def cast(tag, l, nms, after):
        return cast_to_slots(tag + "_cast", place, [given[nm] for nm in nms], l, after)

    def finish_gather(tag, nms, started, after):
        bufs = gather_wait(tag + "_gather_wait", specs(nms), started[0], started[1], started[2], after)
        return dict(zip(nms, gather_pass(tag + "_gather_pass", specs(nms), bufs)))

    g0a = gather_start("l0a_gather_start", specs(first), cast("l0a", 0, first, place), place)
    g0b = gather_start("l0b_gather_start", specs(others), cast("l0b", 0, others, g0a[3]), g0a[3])
    started = {}

    def rest_of_layer0(after):
        got = finish_gather("l0b", others, g0b, after)
        started["l1"] = gather_start("l1_gather_start", specs(names), l1_shards, got["w_out"])
        return got, started["l1"][3]

    cos, sin = rope_tables("rope_tables", pos_col, inv_freq)
    h1 = pre_norm("l0_pre_mix", xs, _with_token(gains[0], "pre_mix_gain", g0b[3])["pre_mix_gain"])
    l1_shards = cast("l1", 1, names, h1)
    w_in0 = finish_gather("l0a", first, g0a, l1_shards[0])["w_in"]
    (x_mid, h1_next), sv0, wts0 = _layer_fwd(0, xs, h1, p[0, 0], w_in0, rest_of_layer0, gains[0], cos, sin, attn_sinks[0],
                                             hgrn_lb_logits, gains[1]["pre_mix_gain"], None)
    wts1 = finish_gather("l1", names, started["l1"], x_mid)
    (dy, loss_part), sv1, _ = _layer_fwd(1, x_mid, h1_next, p[1, 0], wts1["w_in"], lambda after: (wts1, None), gains[1], cos, sin,
                                         attn_sinks[1], hgrn_lb_logits, None, tgt)

    st1, early1, small1 = _layer_bwd_ffn(1, dy, sv1, wts1, gains[1])
    dx_mid, late1, small1b = _layer_bwd_mix(1, st1, sv1, wts1, gains[1], cos, sin, attn_sinks[1], hgrn_lb_logits)
    big1, small1 = {**early1, **late1}, {**small1, **small1b}
    chips = {}

    def chips_after(tag, nms, begun):
        def hook(x):
            chips[tag] = _reduce_chips(tag, nms, begun, place, x)
            return chips[tag][4]
        return hook

    b1 = _reduce_begin("l1", names, big1)
    st0, early0, small0 = _layer_bwd_ffn(0, dx_mid, sv0, wts0, gains[0], after=b1[4], hook=chips_after("l1", names, b1))
    b0e = _reduce_begin("l0e", EARLY, early0)
    dx0, late0, small0b = _layer_bwd_mix(0, st0, sv0, wts0, gains[0], cos, sin, attn_sinks[0], hgrn_lb_logits, after=b0e[4],
                                         hook=chips_after("l0e", EARLY, b0e))
    r1, r0e = chips["l1"], chips["l0e"]
    small0 = {**small0, **small0b}
    r0l = _reduce_start("l0l", LATE, late0, place)
    red1 = _reduce_finish("l1", names, r1, place, r0l[4])
    red0 = _reduce_finish("l0e", EARLY, r0e, place, red1[names[-1]])

    loss = lax.psum(loss_part[0, 0], ("x", "y", "c"))
    grad_x = dx0[None]

    out_big = {}
    for nm in EARLY:
        out_big[nm] = adamw_big("adamw_" + nm, given[nm], given["m_" + nm], given["v_" + nm], red0[nm], red1[nm], BIG_BY_NAME[nm][3])
    red0.update(_reduce_finish("l0l", LATE, r0l, place, out_big[EARLY[-1]][3]))
    for nm in LATE:
        out_big[nm] = adamw_big("adamw_" + nm, given[nm], given["m_" + nm], given["v_" + nm], red0[nm], red1[nm], BIG_BY_NAME[nm][3])

    widths = {nm: given[nm].shape[1] for nm, _ in SMALL}
    small_g = {nm: jnp.concatenate([small0[nm][:, :widths[nm]] if nm != "attn_sinks" else small0[nm][:, :LANES],
                                    small1[nm][:, :widths[nm]] if nm != "attn_sinks" else small1[nm][:, :LANES]], axis=0)
               for nm, _ in SMALL}
    g_sum = allreduce_small("allreduce_small", _pack_small(small_g))
    sm = adamw_small("adamw_small", _pack_small({nm: given[nm] for nm, _ in SMALL}),
                     _pack_small({nm: given["m_" + nm] for nm, _ in SMALL}),
                     _pack_small({nm: given["v_" + nm] for nm, _ in SMALL}), g_sum)
    out_small = [_unpack_small(a, widths) for a in sm]

    order = ["w_in", "attn_sinks", "hgrn_lb_logits", "attn_out_gain", "hgrn_out_gain", "w_out", "pre_mix_gain", "post_mix_gain",
             "pre_ffn_gain", "post_ffn_gain", "w_ffn_gate", "w_ffn_up", "w_ffn_down", "ple_gain", "w_ple_gate", "w_ple_proj"]
    res = [loss, grad_x]
    for k in range(4):
        for nm in order:
            res.append(out_big[nm][k] if nm in out_big else out_small[k][nm])
    return tuple(res)
```

```python
import jax
import jax.numpy as jnp
from jax import lax
from jax.experimental import pallas as pl
from jax.experimental.pallas import tpu as pltpu

F32, BF16 = jnp.float32, jnp.bfloat16
SDS = jax.ShapeDtypeStruct
MESH = pl.DeviceIdType.MESH

D_MODEL = 2048
ATTN_WIDTH = 1024
HGRN_WIDTH = 1024
KV_WIDTH = 256
N_Q_HEADS = 16
N_KV_HEADS = 4
Q_PER_KV = 4
WINDOW = 128
MASK_VALUE = -1e30
ROPE_THETA = 10000.0
HGRN_HEADS = 8
HGRN_CHUNK = 16
D_FF = 5632
D_PLE = 256
RMS_EPS = 1e-6
LANES = 128
N_CHIPS = 4
COL_Q, COL_K, COL_V, COL_HQ, COL_HF, COL_HI, COL_HG = 0, 8, 10, 12, 20, 28, 36

ADAM_LR, ADAM_B1, ADAM_B2, ADAM_EPS, ADAM_WD, ADAM_STEP = 0.001, 0.9, 0.999, 1e-08, 0.01, 10

VMEM_LIMIT = 56 * 1024 * 1024
ROW_TILE = 256
ROW_TILE_WIDE = 512

_NN = (((1,), (0,)), ((), ()))
_NT = (((1,), (1,)), ((), ()))
_TN = (((0,), (0,)), ((), ()))


def _pc(body, *, name, out_shape, in_specs, out_specs, grid=(), scratch=(), sem=None, grid_spec=None, **kw):
    params = dict(vmem_limit_bytes=VMEM_LIMIT)
    if sem is not None:
        params["dimension_semantics"] = sem
    if grid_spec is not None:
        return pl.pallas_call(body, name=name, out_shape=out_shape, grid_spec=grid_spec,
                              compiler_params=pltpu.CompilerParams(**params), **kw)
    return pl.pallas_call(body, name=name, out_shape=out_shape, grid=grid, in_specs=in_specs, out_specs=out_specs,
                          scratch_shapes=list(scratch), compiler_params=pltpu.CompilerParams(**params), **kw)


def _sigmoid(x):
    return 1.0 / (1.0 + jnp.exp(-x))


def _rstd(x):
    return lax.rsqrt(jnp.mean(x * x, axis=-1, keepdims=True) + RMS_EPS)


def _rows(t, w, col=0):
    return pl.BlockSpec((t, w), lambda i, col=col: (i, col))


def _fixed(shape):
    return pl.BlockSpec(shape, lambda *_: (0,) * len(shape))


def _mm(name, a, b, *, dims, grid, a_spec, b_spec, o_spec, out_shape, parts=1, add=None, add_spec=None, after=None):
    def body(*refs):
        a_ref, b_ref, o_ref = refs[0], refs[1], refs[-1]
        if parts == 1:
            r = lax.dot_general(a_ref[...].astype(BF16), b_ref[...].astype(BF16), dims, preferred_element_type=F32)
        else:
            w = a_ref.shape[1] // parts
            r = None
            for j in range(parts):
                t = lax.dot_general(a_ref[:, j * w:(j + 1) * w].astype(BF16), b_ref[j].astype(BF16), dims,
                                    preferred_element_type=F32)
                r = t if r is None else r + t
        if add is not None:
            r = r + refs[2][...]
        o_ref[...] = r.astype(o_ref.dtype)

    ins = [a, b] + ([] if add is None else [add]) + ([] if after is None else [after])
    specs = [a_spec, b_spec] + ([] if add is None else [add_spec]) + ([] if after is None else [pl.BlockSpec(memory_space=pl.ANY)])
    return _pc(body, name=name, grid=grid, in_specs=specs, out_specs=o_spec, out_shape=out_shape,
               sem=("parallel",) * len(grid))(*ins)


def _tile(n, t):
    if n <= t:
        return n
    while n % t:
        t //= 2
    assert t % 8 == 0
    return t


def mm_col(name, a, wg, out_dtype=F32):
    s, k = a.shape
    _, _, n = wg.shape
    tm = _tile(s, 512)
    return _mm(name, a, wg, dims=_NN, grid=(N_CHIPS, s // tm),
               a_spec=pl.BlockSpec((tm, k), lambda j, i: (i, 0)),
               b_spec=pl.BlockSpec((None, k, n), lambda j, i: (j, 0, 0)),
               o_spec=pl.BlockSpec((tm, n), lambda j, i: (i, j)),
               out_shape=SDS((s, N_CHIPS * n), out_dtype))


def mm_row(name, a, wg, out_dtype=F32):
    s, _ = a.shape
    _, r, n = wg.shape
    tm = _tile(s, 512)
    tn = _tile(n, 1024 if r > 512 else 2048)
    return _mm(name, a, wg, dims=_NN, grid=(n // tn, s // tm), parts=N_CHIPS,
               a_spec=pl.BlockSpec((tm, N_CHIPS * r), lambda j, i: (i, 0)),
               b_spec=pl.BlockSpec((N_CHIPS, r, tn), lambda j, i: (0, 0, j)),
               o_spec=pl.BlockSpec((tm, tn), lambda j, i: (i, j)),
               out_shape=SDS((s, n), out_dtype))


def mm_col_t(name, dy, wg, add=None, out_dtype=F32, after=None):
    s, _ = dy.shape
    _, k, n = wg.shape
    tm = _tile(s, 512)
    tk = _tile(k, 1024)
    return _mm(name, dy, wg, dims=_NT, grid=(k // tk, s // tm), parts=N_CHIPS,
               a_spec=pl.BlockSpec((tm, N_CHIPS * n), lambda j, i: (i, 0)),
               b_spec=pl.BlockSpec((N_CHIPS, tk, n), lambda j, i: (0, j, 0)),
               o_spec=pl.BlockSpec((tm, tk), lambda j, i: (i, j)),
               add=add, add_spec=pl.BlockSpec((tm, tk), lambda j, i: (i, j)),
               out_shape=SDS((s, k), out_dtype), after=after)


def in_proj_bwd(name, pieces, wg):
    s = pieces[0].shape[0]
    _, k, n = wg.shape
    tm = _tile(s, 512)
    tk = _tile(k, 512)
    width = sum(p.shape[1] for p in pieces)
    npc = len(pieces)
    segments, start = [], 0
    for t, p in enumerate(pieces):
        lo = start
        while lo < start + p.shape[1]:
            hi = min(start + p.shape[1], (lo // n + 1) * n)
            segments.append((t, lo - start, hi - start, lo // n, lo % n, lo % n + hi - lo))
            lo = hi
        start += p.shape[1]

    def body(*refs):
        w_ref, dproj_ref, o_ref = refs[npc], refs[npc + 1], refs[npc + 2]

        @pl.when(pl.program_id(1) == 0)
        def _():
            off = 0
            for t in range(npc):
                dproj_ref[:, off:off + pieces[t].shape[1]] = refs[t][...]
                off += pieces[t].shape[1]

        r = None
        for t, a, b, j, c, d in segments:
            part = lax.dot_general(refs[t][:, a:b], w_ref[j, :, c:d], _NT, preferred_element_type=F32)
            r = part if r is None else r + part
        o_ref[...] = r

    return _pc(body, name=name, grid=(s // tm, k // tk),
               in_specs=[pl.BlockSpec((tm, p.shape[1]), lambda i, j: (i, 0)) for p in pieces]
               + [pl.BlockSpec((N_CHIPS, tk, n), lambda i, j: (0, j, 0))],
               out_specs=[pl.BlockSpec((tm, width), lambda i, j: (i, 0)), pl.BlockSpec((tm, tk), lambda i, j: (i, j))],
               out_shape=[SDS((s, width), BF16), SDS((s, k), F32)], sem=("parallel", "arbitrary"))(*pieces, wg)


def mm_row_t(name, dy, wg, out_dtype=F32):
    s, n = dy.shape
    _, r, _ = wg.shape
    tm = _tile(s, 512)

    def body(dy_ref, w_ref, o_ref):
        dyv = dy_ref[...].astype(BF16)
        for j in range(N_CHIPS):
            o_ref[:, j * r:(j + 1) * r] = lax.dot_general(dyv, w_ref[j], _NT, preferred_element_type=F32).astype(out_dtype)

    return _pc(body, name=name, grid=(s // tm,), in_specs=[_rows(tm, n), _fixed(wg.shape)], out_specs=_rows(tm, N_CHIPS * r),
               out_shape=SDS((s, N_CHIPS * r), out_dtype), sem=("parallel",))(dy, wg)


def norm_bwd_pair(name, x_a, gain_a, dy, dres, x_b, gain_b):
    s, d = x_a.shape
    t = _tile(s, ROW_TILE)

    def one(xv, g, dyv):
        r = _rstd(xv)
        xh = xv * r
        dyg = dyv * g
        return r * (dyg - xh * jnp.mean(dyg * xh, axis=-1, keepdims=True)), jnp.sum(dyv * xh, axis=0, keepdims=True)

    def body(xa_ref, ga_ref, dy_ref, r_ref, xb_ref, gb_ref, dx_ref, db_ref, dga_ref, dgb_ref):
        dx, pa = one(xa_ref[...], ga_ref[...], dy_ref[...])
        dx = dx + r_ref[...]
        dx_ref[...] = dx
        db, pb = one(xb_ref[...], gb_ref[...], dx)
        db_ref[...] = db.astype(BF16)

        @pl.when(pl.program_id(0) == 0)
        def _():
            dga_ref[...] = pa
            dgb_ref[...] = pb

        @pl.when(pl.program_id(0) > 0)
        def _():
            dga_ref[...] += pa
            dgb_ref[...] += pb

    row, gain = _rows(t, d), _fixed((1, d))
    return _pc(body, name=name, grid=(s // t,), in_specs=[row, gain, row, row, row, gain], out_specs=[row, row, gain, gain],
               out_shape=[SDS((s, d), F32), SDS((s, d), BF16), SDS((1, d), F32), SDS((1, d), F32)],
               sem=("arbitrary",))(x_a, gain_a, dy, dres, x_b, gain_b)


def ffn_gate_up(name, h, wg_gate, wg_up):
    s, k = h.shape
    _, _, n = wg_gate.shape
    tm = _tile(s, 512)

    def body(h_ref, wg_ref, wu_ref, g_ref, u_ref, a_ref):
        hv = h_ref[...]
        g = jnp.dot(hv, wg_ref[...], preferred_element_type=F32)
        u = jnp.dot(hv, wu_ref[...], preferred_element_type=F32)
        g_ref[...] = g
        u_ref[...] = u
        a_ref[...] = ((g * _sigmoid(g)) * u).astype(BF16)

    wspec = pl.BlockSpec((None, k, n), lambda j, i: (j, 0, 0))
    ospec = pl.BlockSpec((tm, n), lambda j, i: (i, j))
    return _pc(body, name=name, grid=(N_CHIPS, s // tm), in_specs=[pl.BlockSpec((tm, k), lambda j, i: (i, 0)), wspec, wspec],
               out_specs=[ospec] * 3, out_shape=[SDS((s, N_CHIPS * n), F32)] * 2 + [SDS((s, N_CHIPS * n), BF16)],
               sem=("parallel", "parallel"))(h, wg_gate, wg_up)


def ffn_down_bwd(name, df, wg_down, g, u):
    s, n = df.shape
    _, r, _ = wg_down.shape
    tm = _tile(s, 512)

    def body(df_ref, w_ref, g_ref, u_ref, dg_ref, du_ref):
        da = lax.dot_general(df_ref[...], w_ref[...], _NT, preferred_element_type=F32)
        gv = g_ref[...]
        sg = _sigmoid(gv)
        du_ref[...] = (da * (gv * sg)).astype(BF16)
        dg_ref[...] = ((da * u_ref[...]) * (sg * (1.0 + gv * (1.0 - sg)))).astype(BF16)

    cspec = pl.BlockSpec((tm, r), lambda j, i: (i, j))
    return _pc(body, name=name, grid=(N_CHIPS, s // tm),
               in_specs=[pl.BlockSpec((tm, n), lambda j, i: (i, 0)), pl.BlockSpec((None, r, n), lambda j, i: (j, 0, 0)), cspec, cspec],
               out_specs=[cspec] * 2, out_shape=[SDS((s, N_CHIPS * r), BF16)] * 2,
               sem=("parallel", "parallel"))(df, wg_down, g, u)


def mm_wg_col(name, a, dy):
    s, k = a.shape
    n = dy.shape[1] // N_CHIPS
    tm = _tile(k // 2, 512)
    hb = (k // 2) // tm
    return _mm(name, a, dy, dims=_TN, grid=(N_CHIPS, k // tm),
               a_spec=pl.BlockSpec((s, tm), lambda j, i: (0, i)),
               b_spec=pl.BlockSpec((s, n), lambda j, i: (0, j)),
               o_spec=pl.BlockSpec((None, None, tm, n), lambda j, i: (i // hb, j, i % hb, 0)),
               out_shape=SDS((2, N_CHIPS, k // 2, n), BF16))


def mm_wg_row(name, a, dy):
    s, n = dy.shape
    r = a.shape[1] // N_CHIPS
    tn = _tile(n // 2, 512)
    nb = (n // 2) // tn
    return _mm(name, a, dy, dims=_TN, grid=(N_CHIPS, n // tn),
               a_spec=pl.BlockSpec((s, r), lambda j, i: (0, j)),
               b_spec=pl.BlockSpec((s, tn), lambda j, i: (0, i)),
               o_spec=pl.BlockSpec((None, None, r, tn), lambda j, i: (i // nb, j, 0, i % nb)),
               out_shape=SDS((2, N_CHIPS, r, n // 2), BF16))


def pre_norm(name, x, gain):
    s, d = x.shape
    t = _tile(s, ROW_TILE_WIDE)

    def body(x_ref, g_ref, o_ref):
        xv = x_ref[...]
        o_ref[...] = ((xv * _rstd(xv)) * g_ref[...]).astype(BF16)

    return _pc(body, name=name, grid=(s // t,), in_specs=[_rows(t, d), _fixed((1, d))], out_specs=_rows(t, d),
               out_shape=SDS((s, d), BF16), sem=("parallel",))(x, gain)


def post_pre_norm(name, m, g_post, res, g_pre):
    s, d = m.shape
    t = _tile(s, ROW_TILE_WIDE)

    def body(m_ref, gp_ref, r_ref, gn_ref, x_ref, h_ref):
        mv = m_ref[...]
        xn = r_ref[...] + (mv * _rstd(mv)) * gp_ref[...]
        x_ref[...] = xn
        h_ref[...] = ((xn * _rstd(xn)) * gn_ref[...]).astype(BF16)

    return _pc(body, name=name, grid=(s // t,),
               in_specs=[_rows(t, d), _fixed((1, d)), _rows(t, d), _fixed((1, d))],
               out_specs=[_rows(t, d), _rows(t, d)], out_shape=[SDS((s, d), F32), SDS((s, d), BF16)],
               sem=("parallel",))(m, g_post, res, g_pre)


def _row_dot(a_ref, w_ref):
    r = w_ref.shape[1]
    out = None
    for j in range(N_CHIPS):
        part = jnp.dot(a_ref[:, j * r:(j + 1) * r], w_ref[j], preferred_element_type=F32)
        out = part if out is None else out + part
    return out


def _row_dot_specs(t, a, wg):
    return [_rows(t, a.shape[1]), _fixed(wg.shape)]


def out_proj_post_mix(name, a, wg, g_post, res, g_pre):
    s, d = res.shape
    t = _tile(s, ROW_TILE)

    def body(a_ref, w_ref, gp_ref, r_ref, gn_ref, m_ref, x_ref, h_ref):
        mv = _row_dot(a_ref, w_ref)
        m_ref[...] = mv
        xn = r_ref[...] + (mv * _rstd(mv)) * gp_ref[...]
        x_ref[...] = xn
        h_ref[...] = ((xn * _rstd(xn)) * gn_ref[...]).astype(BF16)

    return _pc(body, name=name, grid=(s // t,),
               in_specs=_row_dot_specs(t, a, wg) + [_fixed((1, d)), _rows(t, d), _fixed((1, d))],
               out_specs=[_rows(t, d)] * 3, out_shape=[SDS((s, d), F32), SDS((s, d), F32), SDS((s, d), BF16)],
               sem=("parallel",))(a, wg, g_post, res, g_pre)


def _col_dot(p_ref, w_ref):
    pv = p_ref[...].astype(BF16)
    return jnp.concatenate([jnp.dot(pv, w_ref[j], preferred_element_type=F32) for j in range(N_CHIPS)], axis=1)


def ple_gate_fwd_mid(name, a, wg, p, wg_proj, x2, g_next):
    s, d = x2.shape
    t = _tile(s, ROW_TILE)

    def body(a_ref, w_ref, p_ref, wp_ref, x_ref, g_ref, z_ref, pp_ref, xo_ref, h_ref):
        z = _row_dot(a_ref, w_ref)
        z_ref[...] = z
        pv = _col_dot(p_ref, wp_ref)
        pp_ref[...] = pv
        xn = x_ref[...] + pv * _sigmoid(z)
        xo_ref[...] = xn
        h_ref[...] = ((xn * _rstd(xn)) * g_ref[...]).astype(BF16)

    return _pc(body, name=name, grid=(s // t,),
               in_specs=_row_dot_specs(t, a, wg) + [_rows(t, p.shape[1]), _fixed(wg_proj.shape), _rows(t, d), _fixed((1, d))],
               out_specs=[_rows(t, d)] * 4, out_shape=[SDS((s, d), F32)] * 3 + [SDS((s, d), BF16)],
               sem=("parallel",))(a, wg, p, wg_proj, x2, g_next)


def ple_gate_fwd_loss(name, a, wg, p, wg_proj, x2, target):
    s, d = x2.shape
    t = _tile(s, ROW_TILE)

    def body(a_ref, w_ref, p_ref, wp_ref, x_ref, t_ref, dy_ref, dpp_ref, dz_ref, l_ref):
        gate = _sigmoid(_row_dot(a_ref, w_ref))
        pv = _col_dot(p_ref, wp_ref)
        err = (x_ref[...] + pv * gate) - t_ref[...]
        dy = err * (1.0 / d)
        dy_ref[...] = dy
        dpp_ref[...] = (dy * gate).astype(BF16)
        dz_ref[...] = ((dy * pv) * (gate * (1.0 - gate))).astype(BF16)
        part = jnp.sum(jnp.sum(err * err, axis=-1, keepdims=True), axis=0, keepdims=True) * (0.5 / d)

        @pl.when(pl.program_id(0) == 0)
        def _():
            l_ref[...] = part

        @pl.when(pl.program_id(0) > 0)
        def _():
            l_ref[...] += part

    return _pc(body, name=name, grid=(s // t,),
               in_specs=_row_dot_specs(t, a, wg) + [_rows(t, p.shape[1]), _fixed(wg_proj.shape), _rows(t, d), _rows(t, d)],
               out_specs=[_rows(t, d), _rows(t, d), _rows(t, d), _fixed((1, 1))],
               out_shape=[SDS((s, d), F32), SDS((s, d), BF16), SDS((s, d), BF16), SDS((1, 1), F32)],
               sem=("arbitrary",))(a, wg, p, wg_proj, x2, target)


ANY_SPEC = pl.BlockSpec(memory_space=pl.ANY)


def ple_bwd(name, dx3, z, pp, after):
    s, d = z.shape
    t = _tile(s, ROW_TILE_WIDE)

    def body(d_ref, z_ref, p_ref, after_ref, dpp_ref, dz_ref):
        gate = _sigmoid(z_ref[...])
        dv = d_ref[...]
        dpp_ref[...] = (dv * gate).astype(BF16)
        dz_ref[...] = ((dv * p_ref[...]) * (gate * (1.0 - gate))).astype(BF16)

    return _pc(body, name=name, grid=(s // t,), in_specs=[_rows(t, d)] * 3 + [ANY_SPEC], out_specs=[_rows(t, d)] * 2,
               out_shape=[SDS((s, d), BF16)] * 2, sem=("parallel",))(dx3, z, pp, after)


def norm_bwd(name, xin, gain, dy, dres):
    s, d = xin.shape
    t = _tile(s, ROW_TILE_WIDE)

    def body(x_ref, g_ref, dy_ref, r_ref, dx_ref, dg_ref):
        xv = x_ref[...]
        r = _rstd(xv)
        xh = xv * r
        dyv = dy_ref[...]
        dyg = dyv * g_ref[...]
        c = jnp.mean(dyg * xh, axis=-1, keepdims=True)
        dx_ref[...] = r * (dyg - xh * c) + r_ref[...]
        part = jnp.sum(dyv * xh, axis=0, keepdims=True)

        @pl.when(pl.program_id(0) == 0)
        def _():
            dg_ref[...] = part

        @pl.when(pl.program_id(0) > 0)
        def _():
            dg_ref[...] += part

    return _pc(body, name=name, grid=(s // t,), in_specs=[_rows(t, d), _fixed((1, d)), _rows(t, d), _rows(t, d)],
               out_specs=[_rows(t, d), _fixed((1, d))], out_shape=[SDS((s, d), F32), SDS((1, d), F32)],
               sem=("arbitrary",))(xin, gain, dy, dres)


def _lane(shape):
    return lax.broadcasted_iota(jnp.int32, shape, 1)


def _swap_halves(x):
    lo = (_lane(x.shape) % 64) < 32
    return jnp.where(lo, pltpu.roll(x, 96, 1), pltpu.roll(x, 32, 1))


def rope_tables(name, pos_col, inv_freq):
    s = pos_col.shape[0]
    t = _tile(s, ROW_TILE_WIDE)

    def body(p_ref, f_ref, c_ref, s_ref):
        ang = p_ref[...].astype(F32) * f_ref[...]
        lo = (_lane(ang.shape) % 64) < 32
        c_ref[...] = jnp.cos(ang)
        sn = jnp.sin(ang)
        s_ref[...] = jnp.where(lo, -sn, sn)

    return _pc(body, name=name, grid=(s // t,), in_specs=[_rows(t, 1), _fixed((1, LANES))],
               out_specs=[_rows(t, LANES)] * 2, out_shape=[SDS((s, LANES), F32)] * 2, sem=("parallel",))(pos_col, inv_freq)


def _pad_heads(chunk, lo_mask):
    zero = jnp.zeros_like(chunk)
    return jnp.where(lo_mask, chunk, zero), jnp.where(lo_mask, pltpu.roll(chunk, 64, 1), zero)


def rope_qkv(name, proj, cos, sin):
    s = proj.shape[0]
    t = _tile(s, ROW_TILE_WIDE)

    def body(q_ref, kv_ref, c_ref, s_ref, qp_ref, kp_ref, vp_ref):
        cs, sn = c_ref[...], s_ref[...]
        lo_mask = _lane(cs.shape) < 64

        def rot(x):
            return x * cs + _swap_halves(x) * sn

        for j in range(ATTN_WIDTH // LANES):
            a, b = _pad_heads(rot(q_ref[:, j * LANES:(j + 1) * LANES]), lo_mask)
            qp_ref[:, (2 * j) * LANES:(2 * j + 1) * LANES] = a.astype(BF16)
            qp_ref[:, (2 * j + 1) * LANES:(2 * j + 2) * LANES] = b.astype(BF16)
        for j in range(KV_WIDTH // LANES):
            a, b = _pad_heads(rot(kv_ref[:, j * LANES:(j + 1) * LANES]), lo_mask)
            kp_ref[:, (2 * j) * LANES:(2 * j + 1) * LANES] = a.astype(BF16)
            kp_ref[:, (2 * j + 1) * LANES:(2 * j + 2) * LANES] = b.astype(BF16)
            a, b = _pad_heads(kv_ref[:, KV_WIDTH + j * LANES:KV_WIDTH + (j + 1) * LANES], lo_mask)
            vp_ref[:, (2 * j) * LANES:(2 * j + 1) * LANES] = a.astype(BF16)
            vp_ref[:, (2 * j + 1) * LANES:(2 * j + 2) * LANES] = b.astype(BF16)

    return _pc(body, name=name, grid=(s // t,),
               in_specs=[_rows(t, ATTN_WIDTH, 0), _rows(t, 2 * KV_WIDTH, 2), _rows(t, LANES), _rows(t, LANES)],
               out_specs=[_rows(t, N_Q_HEADS * LANES), _rows(t, N_KV_HEADS * LANES), _rows(t, N_KV_HEADS * LANES)],
               out_shape=[SDS((s, N_Q_HEADS * LANES), BF16), SDS((s, N_KV_HEADS * LANES), BF16),
                          SDS((s, N_KV_HEADS * LANES), BF16)],
               sem=("parallel",))(proj, proj, cos, sin)


def _attn_mask(n):
    L = WINDOW
    qi = lax.broadcasted_iota(jnp.int32, (L, 2 * L), 0) + L
    ki = lax.broadcasted_iota(jnp.int32, (L, 2 * L), 1)
    rel = qi - ki
    return (rel >= 0) & (rel < WINDOW) & ((n > 0) | (ki >= L))


def _attn_scores(qh, kk, valid):
    sc = lax.dot_general(qh, kk, _NT, preferred_element_type=F32) * 0.125
    return jnp.where(valid, sc, MASK_VALUE)


def _attn_softmax(sc, sink):
    m = jnp.maximum(jnp.max(sc, axis=-1, keepdims=True), sink)
    e = jnp.exp(sc - m)
    es = jnp.exp(sink - m)
    den = jnp.sum(e, axis=-1, keepdims=True) + es
    return e / den, es / den


def _attn_specs(s):
    L = WINDOW
    cur = lambda n: (n, 0)
    prev = lambda n: (jnp.maximum(n - 1, 0), 0)
    kvw = N_KV_HEADS * LANES
    return [pl.BlockSpec((L, N_Q_HEADS * LANES), cur), pl.BlockSpec((L, kvw), cur), pl.BlockSpec((L, kvw), prev),
            pl.BlockSpec((L, kvw), cur), pl.BlockSpec((L, kvw), prev), pl.BlockSpec(memory_space=pltpu.SMEM)]


def attn_fwd(name, qp, kp, vp, sinks):
    s = qp.shape[0]
    L = WINDOW

    def body(q_ref, kc_ref, kp_ref, vc_ref, vp_ref, sk_ref, o_ref):
        valid = _attn_mask(pl.program_id(0))
        kks, vvs = [], []
        for kvh in range(N_KV_HEADS):
            cols = slice(kvh * LANES, (kvh + 1) * LANES)
            kks.append(jnp.concatenate([kp_ref[:, cols], kc_ref[:, cols]], axis=0))
            vvs.append(jnp.concatenate([vp_ref[:, cols], vc_ref[:, cols]], axis=0))
        scs = [_attn_scores(q_ref[:, h * LANES:(h + 1) * LANES], kks[h // Q_PER_KV], valid) for h in range(N_Q_HEADS)]
        pbs = [_attn_softmax(scs[h], sk_ref[h])[0].astype(BF16) for h in range(N_Q_HEADS)]
        outs = [jnp.dot(pbs[h], vvs[h // Q_PER_KV], preferred_element_type=F32) for h in range(N_Q_HEADS)]
        for j in range(ATTN_WIDTH // LANES):
            o_ref[:, j * LANES:(j + 1) * LANES] = outs[2 * j] + pltpu.roll(outs[2 * j + 1], 64, 1)

    return _pc(body, name=name, grid=(s // L,), in_specs=_attn_specs(s),
               out_specs=pl.BlockSpec((L, ATTN_WIDTH), lambda n: (n, 0)),
               out_shape=SDS((s, ATTN_WIDTH), F32), sem=("parallel",))(qp, kp, kp, vp, vp, sinks)


def attn_bwd(name, qp, kp, vp, sinks, dattn):
    s = qp.shape[0]
    L = WINDOW
    kvw = N_KV_HEADS * LANES

    def body(q_ref, kc_ref, kp_ref, vc_ref, vp_ref, sk_ref, do_ref, dq_ref, dkc_ref, dkp_ref, dvc_ref, dvp_ref, ds_ref):
        n = pl.program_id(0)
        valid = _attn_mask(n)
        lo_mask = _lane((L, LANES)) < 64
        lane1 = _lane((1, LANES))
        dsink = jnp.zeros((1, LANES), F32)
        heads = range(N_Q_HEADS)
        kks, vvs = [], []
        for kvh in range(N_KV_HEADS):
            cols = slice(kvh * LANES, (kvh + 1) * LANES)
            kks.append(jnp.concatenate([kp_ref[:, cols], kc_ref[:, cols]], axis=0))
            vvs.append(jnp.concatenate([vp_ref[:, cols], vc_ref[:, cols]], axis=0))
        qs, dos, scs, dps = [], [], [], []
        for h in heads:
            qs.append(q_ref[:, h * LANES:(h + 1) * LANES])
            chunk = do_ref[:, (h // 2) * LANES:(h // 2 + 1) * LANES]
            if h % 2:
                chunk = pltpu.roll(chunk, 64, 1)
            dos.append(jnp.where(lo_mask, chunk, 0.0).astype(BF16))
            scs.append(_attn_scores(qs[h], kks[h // Q_PER_KV], valid))
            dps.append(lax.dot_general(dos[h], vvs[h // Q_PER_KV], _NT, preferred_element_type=F32))
        pbs, dsbs = [], []
        for h in heads:
            p, ps = _attn_softmax(scs[h], sk_ref[h])
            delta = jnp.sum(p * dps[h], axis=-1, keepdims=True)
            dsbs.append(((p * (dps[h] - delta)) * 0.125).astype(BF16))
            pbs.append(p.astype(BF16))
            dsink = dsink + jnp.where(lane1 == h, -jnp.sum(ps * delta, axis=0, keepdims=True), 0.0)
        for kvh in range(N_KV_HEADS):
            cols = slice(kvh * LANES, (kvh + 1) * LANES)
            dkk = jnp.zeros((2 * L, LANES), F32)
            dvv = jnp.zeros((2 * L, LANES), F32)
            for h in range(kvh * Q_PER_KV, (kvh + 1) * Q_PER_KV):
                dq_ref[:, h * LANES:(h + 1) * LANES] = jnp.dot(dsbs[h], kks[kvh], preferred_element_type=F32)
                dkk = dkk + lax.dot_general(dsbs[h], qs[h], _TN, preferred_element_type=F32)
                dvv = dvv + lax.dot_general(pbs[h], dos[h], _TN, preferred_element_type=F32)
            dkp_ref[:, cols] = dkk[:L]
            dkc_ref[:, cols] = dkk[L:]
            dvp_ref[:, cols] = dvv[:L]
            dvc_ref[:, cols] = dvv[L:]

        @pl.when(n == 0)
        def _():
            ds_ref[...] = dsink

        @pl.when(n > 0)
        def _():
            ds_ref[...] += dsink

    blk = lambda w: pl.BlockSpec((L, w), lambda n: (n, 0))
    return _pc(body, name=name, grid=(s // L,), in_specs=_attn_specs(s) + [blk(ATTN_WIDTH)],
               out_specs=[blk(N_Q_HEADS * LANES), blk(kvw), blk(kvw), blk(kvw), blk(kvw), _fixed((1, LANES))],
               out_shape=[SDS((s, N_Q_HEADS * LANES), F32)] + [SDS((s, kvw), F32)] * 4 + [SDS((1, LANES), F32)],
               sem=("arbitrary",))(qp, kp, kp, vp, vp, sinks, dattn)


def rope_bwd(name, dqp, dkc, dkp, dvc, dvp, cos, sin):
    s = dqp.shape[0]
    L = WINDOW
    nb = s // L
    kvw = N_KV_HEADS * LANES

    def body(dq_ref, dkc_ref, dkp_ref, dvc_ref, dvp_ref, c_ref, s_ref, o_ref):
        cs, sn = c_ref[...], s_ref[...]
        more = (pl.program_id(0) < nb - 1).astype(F32)

        def unrot(x):
            return x * cs - _swap_halves(x) * sn

        def compact(ref, j, nxt=None):
            a = ref[:, (2 * j) * LANES:(2 * j + 1) * LANES]
            b = ref[:, (2 * j + 1) * LANES:(2 * j + 2) * LANES]
            if nxt is not None:
                a = a + more * nxt[:, (2 * j) * LANES:(2 * j + 1) * LANES]
                b = b + more * nxt[:, (2 * j + 1) * LANES:(2 * j + 2) * LANES]
            return a + pltpu.roll(b, 64, 1)

        for j in range(ATTN_WIDTH // LANES):
            o_ref[:, j * LANES:(j + 1) * LANES] = unrot(compact(dq_ref, j)).astype(BF16)
        for j in range(KV_WIDTH // LANES):
            o_ref[:, (COL_K + j) * LANES:(COL_K + j + 1) * LANES] = unrot(compact(dkc_ref, j, dkp_ref)).astype(BF16)
            o_ref[:, (COL_V + j) * LANES:(COL_V + j + 1) * LANES] = compact(dvc_ref, j, dvp_ref).astype(BF16)

    cur = lambda n: (n, 0)
    nxt = lambda n: (jnp.minimum(n + 1, nb - 1), 0)
    return _pc(body, name=name, grid=(nb,),
               in_specs=[pl.BlockSpec((L, N_Q_HEADS * LANES), cur), pl.BlockSpec((L, kvw), cur), pl.BlockSpec((L, kvw), nxt),
                         pl.BlockSpec((L, kvw), cur), pl.BlockSpec((L, kvw), nxt), pl.BlockSpec((L, LANES), cur),
                         pl.BlockSpec((L, LANES), cur)],
               out_specs=pl.BlockSpec((L, COL_HQ * LANES), cur), out_shape=SDS((s, COL_HQ * LANES), BF16),
               sem=("parallel",))(dqp, dkc, dkp, dvc, dvp, cos, sin)


def _split3(x):
    a = x.astype(BF16)
    r = x - a.astype(F32)
    b = r.astype(BF16)
    c = (r - b.astype(F32)).astype(BF16)
    return a, b, c


def _chunk_sum(x, upper):
    t = x.shape[0]
    ri = lax.broadcasted_iota(jnp.int32, (t, t), 0)
    ci = lax.broadcasted_iota(jnp.int32, (t, t), 1)
    same = (ri // HGRN_CHUNK) == (ci // HGRN_CHUNK)
    tri = (ci >= ri) if upper else (ci <= ri)
    m = jnp.where(same & tri, 1.0, 0.0).astype(BF16)
    out = None
    for part in _split3(x):
        y = jnp.dot(m, part, preferred_element_type=F32)
        out = y if out is None else out + y
    return out


def _lower_bound(l_ref, layer):
    lv = l_ref[...]
    e = jnp.exp(lv - jnp.max(lv, axis=0, keepdims=True))
    sm = e / jnp.sum(e, axis=0, keepdims=True)
    s0 = sm[0:1]
    return (s0 - s0) if layer == 0 else ((s0 + sm[1:2]) - s0)


def _hgrn_gates(hq_ref, hf_ref, lb):
    z = hf_ref[...]
    sg = _sigmoid(z)
    f = lb + (1.0 - lb) * sg
    kin = (1.0 - lb) * _sigmoid(-z)
    hq = hq_ref[...]
    sq = _sigmoid(hq)
    return sg, f, kin, hq, sq


def _shift_down(x, d):
    return x if d == 0 else pltpu.roll(x, d, 0)


def _shift_up(x, d):
    return x if d == 0 else pltpu.roll(x, x.shape[0] - d, 0)


CHUNKS_PER_BLOCK = LANES // HGRN_CHUNK


def _chunk_iotas():
    shape = (HGRN_CHUNK, LANES)
    return lax.broadcasted_iota(jnp.int32, shape, 0), lax.broadcasted_iota(jnp.int32, shape, 1)


def _chunk_rows(block, chunk):
    start = block * LANES + chunk * HGRN_CHUNK
    return slice(start, start + HGRN_CHUNK)


HGRN_HEADS_PER_STEP = 4
HGRN_STEP_WIDTH = HGRN_HEADS_PER_STEP * LANES


def _hgrn_specs(t, rev, nt):
    row = (lambda h, i: nt - 1 - i) if rev else (lambda h, i: i)
    col = lambda base: pl.BlockSpec((t, HGRN_STEP_WIDTH),
                                    lambda h, i, base=base: (row(h, i), base // HGRN_HEADS_PER_STEP + h))
    return col, row


def _head_views(refs, hh):
    return [r.at[:, pl.ds(hh * LANES, LANES)] for r in refs]


def hgrn_fwd(name, proj, lb_logits, layer):
    s = proj.shape[0]
    t = _tile(s, ROW_TILE)
    nt = s // t
    nc = t // HGRN_CHUNK
    col, row = _hgrn_specs(t, False, nt)

    def body(hq_ref, hf_ref, hi_ref, l_ref, o_ref, st_ref, state):
        @pl.when(pl.program_id(1) == 0)
        def _():
            state[...] = jnp.zeros_like(state)

        for hh in range(HGRN_HEADS_PER_STEP):
            head(*_head_views((hq_ref, hf_ref, hi_ref, l_ref, o_ref), hh), st_ref.at[:, hh], state.at[hh])

    def head(hq_ref, hf_ref, hi_ref, l_ref, o_ref, st_ref, state):
        lb = _lower_bound(l_ref, layer)
        sg, f, kin, hq, sq = _hgrn_gates(hq_ref, hf_ref, lb)
        q = hq * sq
        vb = hi_ref[...].astype(BF16)
        b = _chunk_sum(jnp.log(f), False)
        qe = (q * jnp.exp(b)).astype(BF16)
        trow, lane = _chunk_iotas()
        chunks = [(j, cc) for j in range(t // LANES) for cc in range(CHUNKS_PER_BLOCK)]
        decay, update = [], []
        for j, cc in chunks:
            rs = _chunk_rows(j, cc)
            bc = b[rs]
            bl = bc[HGRN_CHUNK - 1:HGRN_CHUNK, :]
            ke = (kin[rs] * jnp.exp(bl - bc)).astype(BF16)
            decay.append(jnp.exp(bl))
            update.append(lax.dot_general(vb[rs], ke, _TN, preferred_element_type=F32))
        st = state[...]
        for c in range(nc):
            st_ref[c] = st
            st = st * decay[c] + update[c]
        state[...] = st
        o_inter = [lax.dot_general(qe[c * HGRN_CHUNK:(c + 1) * HGRN_CHUNK], st_ref[c].astype(BF16), _NT,
                                   preferred_element_type=F32) for c in range(nc)]
        for j in range(t // LANES):
            blk = slice(j * LANES, (j + 1) * LANES)
            rows = []
            for cc in range(CHUNKS_PER_BLOCK):
                rs = _chunk_rows(j, cc)
                bc, qc, kc = b[rs], q[rs], kin[rs]
                here = trow + cc * HGRN_CHUNK
                am = jnp.where(lane == here, jnp.sum(qc * kc, axis=-1, keepdims=True), 0.0)
                for d in range(1, HGRN_CHUNK):
                    e = jnp.exp(jnp.where(trow >= d, bc - _shift_down(bc, d), MASK_VALUE))
                    a = jnp.sum((qc * _shift_down(kc, d)) * e, axis=-1, keepdims=True)
                    am = jnp.where(lane == here - d, a, am)
                rows.append(am)
            o_intra = jnp.dot(jnp.concatenate(rows, axis=0).astype(BF16), vb[blk], preferred_element_type=F32)
            for cc in range(CHUNKS_PER_BLOCK):
                rs = _chunk_rows(j, cc)
                o_ref[rs, :] = o_intra[cc * HGRN_CHUNK:(cc + 1) * HGRN_CHUNK] + o_inter[j * CHUNKS_PER_BLOCK + cc]

    hp = HGRN_HEADS_PER_STEP
    return _pc(body, name=name, grid=(HGRN_HEADS // hp, nt),
               in_specs=[col(COL_HQ), col(COL_HF), col(COL_HI), pl.BlockSpec((2, HGRN_STEP_WIDTH), lambda h, i: (0, h))],
               out_specs=[pl.BlockSpec((t, HGRN_STEP_WIDTH), lambda h, i: (i, h)),
                          pl.BlockSpec((nc, hp, LANES, LANES), lambda h, i: (i, h, 0, 0))],
               out_shape=[SDS((s, HGRN_WIDTH), F32), SDS((s // HGRN_CHUNK, HGRN_HEADS, LANES, LANES), F32)],
               scratch=[pltpu.VMEM((hp, LANES, LANES), F32)],
               sem=("parallel", "arbitrary"))(proj, proj, proj, lb_logits)


def hgrn_bwd(name, proj, lb_logits, layer, states, do):
    s = proj.shape[0]
    t = _tile(s, ROW_TILE)
    nt = s // t
    nc = t // HGRN_CHUNK
    col, row = _hgrn_specs(t, True, nt)

    def body(hq_ref, hf_ref, hi_ref, l_ref, st_ref, do_ref, dhq_ref, dhf_ref, dhi_ref, dlb_ref, dstate):
        @pl.when(pl.program_id(1) == 0)
        def _():
            dstate[...] = jnp.zeros_like(dstate)

        for hh in range(HGRN_HEADS_PER_STEP):
            hq_v, hf_v, hi_v, l_v, do_v, dhq_v, dhf_v, dhi_v, dlb_v = _head_views(
                (hq_ref, hf_ref, hi_ref, l_ref, do_ref, dhq_ref, dhf_ref, dhi_ref, dlb_ref), hh)
            head(hq_v, hf_v, hi_v, l_v, st_ref.at[:, hh], do_v, dhq_v, dhf_v, dhi_v, dlb_v, dstate.at[hh])

    def head(hq_ref, hf_ref, hi_ref, l_ref, st_ref, do_ref, dhq_ref, dhf_ref, dhi_ref, dlb_ref, dstate):
        first = pl.program_id(1) == 0
        lb = _lower_bound(l_ref, layer)
        sg, f, kin, hq, sq = _hgrn_gates(hq_ref, hf_ref, lb)
        q = hq * sq
        vb = hi_ref[...].astype(BF16)
        b = _chunk_sum(jnp.log(f), False)
        dob = do_ref[...].astype(BF16)
        eb = jnp.exp(b)
        qe = q * eb
        qeb = qe.astype(BF16)
        trow, lane = _chunk_iotas()
        last_row = trow == HGRN_CHUNK - 1

        decay, update = [None] * nc, [None] * nc
        for c in range(nc):
            rs = slice(c * HGRN_CHUNK, (c + 1) * HGRN_CHUNK)
            decay[c] = jnp.exp(b[(c + 1) * HGRN_CHUNK - 1:(c + 1) * HGRN_CHUNK, :])
            update[c] = lax.dot_general(dob[rs], qeb[rs], _TN, preferred_element_type=F32)
        dn_in = [None] * nc
        dn = dstate[...]
        for c in reversed(range(nc)):
            dn_in[c] = dn
            dn = dn * decay[c] + update[c]
        dstate[...] = dn

        dq_c, dk_c, dv_c, dbl_c = [None] * nc, [None] * nc, [None] * nc, [None] * nc
        for c in range(nc):
            rs = slice(c * HGRN_CHUNK, (c + 1) * HGRN_CHUNK)
            bc = b[rs]
            ekb = jnp.exp(bc[HGRN_CHUNK - 1:HGRN_CHUNK, :] - bc)
            ke = kin[rs] * ekb
            st = st_ref[c]
            dnb = dn_in[c].astype(BF16)
            dke = jnp.dot(vb[rs], dnb, preferred_element_type=F32)
            dq_c[c] = jnp.dot(dob[rs], st.astype(BF16), preferred_element_type=F32) * eb[rs]
            dk_c[c] = dke * ekb
            dv_c[c] = lax.dot_general(ke.astype(BF16), dnb, _NT, preferred_element_type=F32)
            dbl_c[c] = jnp.sum(dn_in[c] * st, axis=0, keepdims=True) * decay[c] + jnp.sum(dke * ke, axis=0, keepdims=True)

        db_c = [None] * nc
        for j in range(t // LANES):
            blk = slice(j * LANES, (j + 1) * LANES)
            damat = lax.dot_general(dob[blk], vb[blk], _NT, preferred_element_type=F32)
            rows = [None] * CHUNKS_PER_BLOCK
            for cc in range(CHUNKS_PER_BLOCK):
                c = j * CHUNKS_PER_BLOCK + cc
                rs = _chunk_rows(j, cc)
                bc, qc, kc = b[rs], q[rs], kin[rs]
                dam = damat[cc * HGRN_CHUNK:(cc + 1) * HGRN_CHUNK]
                here = trow + cc * HGRN_CHUNK
                on = lane == here
                da = jnp.sum(jnp.where(on, dam, 0.0), axis=-1, keepdims=True)
                am = jnp.where(on, jnp.sum(qc * kc, axis=-1, keepdims=True), 0.0)
                dq = dq_c[c] + da * kc
                dk = dk_c[c] + da * qc
                for d in range(1, HGRN_CHUNK):
                    on = lane == here - d
                    e = jnp.exp(jnp.where(trow >= d, bc - _shift_down(bc, d), MASK_VALUE))
                    kse = _shift_down(kc, d) * e
                    am = jnp.where(on, jnp.sum(qc * kse, axis=-1, keepdims=True), am)
                    da = jnp.sum(jnp.where(on, dam, 0.0), axis=-1, keepdims=True)
                    dq = dq + da * kse
                    dk = dk + _shift_up(da * (qc * e), d)
                rows[cc] = am
                dq_c[c], dk_c[c] = dq, dk
                db_c[c] = (qc * dq - kc * dk) + jnp.where(last_row, dbl_c[c], 0.0)
            dv_blk = lax.dot_general(jnp.concatenate(rows, axis=0).astype(BF16), dob[blk], _TN, preferred_element_type=F32)
            for cc in range(CHUNKS_PER_BLOCK):
                c = j * CHUNKS_PER_BLOCK + cc
                dv_c[c] = dv_c[c] + dv_blk[cc * HGRN_CHUNK:(cc + 1) * HGRN_CHUNK]
        dq = jnp.concatenate(dq_c, axis=0)
        dk = jnp.concatenate(dk_c, axis=0)
        dv = jnp.concatenate(dv_c, axis=0)
        db = jnp.concatenate(db_c, axis=0)
        dg = _chunk_sum(db, True)
        dhq_ref[...] = (dq * (sq * (1.0 + hq * (1.0 - sq)))).astype(BF16)
        dhi_ref[...] = dv.astype(BF16)
        dfk = dg / f - dk
        dhf_ref[...] = ((dfk * (1.0 - lb)) * (sg * (1.0 - sg))).astype(BF16)
        part = jnp.sum(dfk * (1.0 - sg), axis=0, keepdims=True)

        @pl.when(first)
        def _():
            dlb_ref[...] = part

        @pl.when(jnp.logical_not(first))
        def _():
            dlb_ref[...] += part

    hp = HGRN_HEADS_PER_STEP
    out_col = pl.BlockSpec((t, HGRN_STEP_WIDTH), lambda h, i: (nt - 1 - i, h))
    return _pc(body, name=name, grid=(HGRN_HEADS // hp, nt),
               in_specs=[col(COL_HQ), col(COL_HF), col(COL_HI), pl.BlockSpec((2, HGRN_STEP_WIDTH), lambda h, i: (0, h)),
                         pl.BlockSpec((nc, hp, LANES, LANES), lambda h, i: (nt - 1 - i, h, 0, 0)), out_col],
               out_specs=[out_col, out_col, out_col, pl.BlockSpec((1, HGRN_STEP_WIDTH), lambda h, i: (0, h))],
               out_shape=[SDS((s, HGRN_WIDTH), BF16)] * 3 + [SDS((1, HGRN_WIDTH), F32)],
               scratch=[pltpu.VMEM((hp, LANES, LANES), F32)],
               sem=("parallel", "arbitrary"))(proj, proj, proj, lb_logits, states, do)


def mix_out_fwd(name, attn, o, proj, g_attn, g_hgrn):
    s = attn.shape[0]
    t = _tile(s, ROW_TILE_WIDE)
    half = HGRN_WIDTH // 2

    def body(a_ref, o_ref, hg0_ref, hg1_ref, ga_ref, gh_ref, c_ref):
        av = a_ref[...]
        c_ref[:, :ATTN_WIDTH] = ((av * _rstd(av)) * ga_ref[...]).astype(BF16)
        for j in range(HGRN_HEADS):
            cols = slice(j * LANES, (j + 1) * LANES)
            ov = o_ref[:, cols]
            hg_ref, hcols = (hg0_ref, cols) if j < 4 else (hg1_ref, slice((j - 4) * LANES, (j - 3) * LANES))
            hg = hg_ref[:, hcols]
            on = (ov * _rstd(ov)) * gh_ref[:, cols]
            c_ref[:, ATTN_WIDTH + j * LANES:ATTN_WIDTH + (j + 1) * LANES] = (on * (hg * _sigmoid(hg))).astype(BF16)

    return _pc(body, name=name, grid=(s // t,),
               in_specs=[_rows(t, ATTN_WIDTH), _rows(t, HGRN_WIDTH), _rows(t, half, COL_HG // 4), _rows(t, half, COL_HG // 4 + 1),
                         _fixed((1, ATTN_WIDTH)), _fixed((1, HGRN_WIDTH))],
               out_specs=_rows(t, D_MODEL), out_shape=SDS((s, D_MODEL), BF16), sem=("parallel",))(attn, o, proj, proj, g_attn, g_hgrn)


def mix_out_bwd(name, dcat, attn, o, proj, g_attn, g_hgrn):
    s = attn.shape[0]
    t = _tile(s, ROW_TILE_WIDE)
    half = HGRN_WIDTH // 2

    def body(dc_ref, a_ref, o_ref, hg0_ref, hg1_ref, ga_ref, gh_ref, da_ref, do_ref, dhg_ref, dga_ref, dgh_ref, pa_s, ph_s):
        av = a_ref[...]
        r = _rstd(av)
        xh = av * r
        dyv = dc_ref[:, :ATTN_WIDTH]
        dyg = dyv * ga_ref[...]
        da_ref[...] = r * (dyg - xh * jnp.mean(dyg * xh, axis=-1, keepdims=True))
        pa_s[...] = jnp.sum(dyv * xh, axis=0, keepdims=True)
        for j in range(HGRN_HEADS):
            cols = slice(j * LANES, (j + 1) * LANES)
            ov = o_ref[:, cols]
            hg_ref, hcols = (hg0_ref, cols) if j < 4 else (hg1_ref, slice((j - 4) * LANES, (j - 3) * LANES))
            hg = hg_ref[:, hcols]
            sg = _sigmoid(hg)
            r = _rstd(ov)
            xh = ov * r
            gain = gh_ref[:, cols]
            dh = dc_ref[:, ATTN_WIDTH + j * LANES:ATTN_WIDTH + (j + 1) * LANES]
            dhg_ref[:, cols] = ((dh * (xh * gain)) * (sg * (1.0 + hg * (1.0 - sg)))).astype(BF16)
            dyv = dh * (hg * sg)
            dyg = dyv * gain
            do_ref[:, cols] = r * (dyg - xh * jnp.mean(dyg * xh, axis=-1, keepdims=True))
            ph_s[:, cols] = jnp.sum(dyv * xh, axis=0, keepdims=True)

        @pl.when(pl.program_id(0) == 0)
        def _():
            dga_ref[...] = pa_s[...]
            dgh_ref[...] = ph_s[...]

        @pl.when(pl.program_id(0) > 0)
        def _():
            dga_ref[...] += pa_s[...]
            dgh_ref[...] += ph_s[...]

    return _pc(body, name=name, grid=(s // t,),
               in_specs=[_rows(t, D_MODEL), _rows(t, ATTN_WIDTH), _rows(t, HGRN_WIDTH), _rows(t, half, COL_HG // 4),
                         _rows(t, half, COL_HG // 4 + 1), _fixed((1, ATTN_WIDTH)), _fixed((1, HGRN_WIDTH))],
               out_specs=[_rows(t, ATTN_WIDTH), _rows(t, HGRN_WIDTH), _rows(t, HGRN_WIDTH), _fixed((1, ATTN_WIDTH)),
                          _fixed((1, HGRN_WIDTH))],
               out_shape=[SDS((s, ATTN_WIDTH), F32), SDS((s, HGRN_WIDTH), F32), SDS((s, HGRN_WIDTH), BF16),
                          SDS((1, ATTN_WIDTH), F32), SDS((1, HGRN_WIDTH), F32)],
               scratch=[pltpu.VMEM((1, ATTN_WIDTH), F32), pltpu.VMEM((1, HGRN_WIDTH), F32)],
               sem=("arbitrary",))(dcat, attn, o, proj, proj, g_attn, g_hgrn)


BIG = (("w_in", 2048, 1408, 0), ("w_out", 512, 2048, 1), ("w_ffn_gate", 2048, 1408, 0), ("w_ffn_up", 2048, 1408, 0),
       ("w_ffn_down", 1408, 2048, 1), ("w_ple_gate", 512, 2048, 1), ("w_ple_proj", 256, 512, 0))
BIG_BY_NAME = {spec[0]: spec for spec in BIG}
HBM_SPEC = pl.BlockSpec(memory_space=pltpu.HBM)
SEM_SPEC = pl.BlockSpec(memory_space=pltpu.SEMAPHORE)
TOKEN_SHAPE = (8, LANES)


def _split_call(body, *, name, in_specs, out_specs, out_shape, aliases):
    return pl.pallas_call(body, name=name, in_specs=in_specs, out_specs=out_specs, out_shape=out_shape,
                          input_output_aliases=aliases,
                          compiler_params=pltpu.CompilerParams(has_side_effects=pltpu.SideEffectType.DATAFLOW_SIDE_EFFECTING))


def _in_hbm(arrays):
    return [pltpu.with_memory_space_constraint(a, pltpu.HBM) for a in arrays]


GROUP_STEPS = 8


def _step_rows(rows):
    assert rows % (GROUP_STEPS * 16) == 0
    return rows // GROUP_STEPS


def cast_to_slots(name, place, ws, layer, after):
    nt = len(ws)

    def body(place_ref, *refs):
        for t in range(nt):
            refs[nt + 1 + t][...] = refs[t][...].astype(BF16)

    in_specs, out_specs = [], []
    for w in ws:
        block = (None, _step_rows(w.shape[1]), w.shape[2])
        in_specs.append(pl.BlockSpec(block, lambda i, pr: (layer, i, 0)))
        out_specs.append(pl.BlockSpec(block, lambda i, pr: (pr[1], i, 0)))
    gs = pltpu.PrefetchScalarGridSpec(num_scalar_prefetch=1, grid=(GROUP_STEPS,), in_specs=in_specs + [ANY_SPEC],
                                      out_specs=out_specs)
    return _pc(body, name=name, grid_spec=gs, in_specs=None, out_specs=None,
               out_shape=[SDS((N_CHIPS,) + w.shape[1:], BF16) for w in ws], sem=("parallel",))(place, *ws, after)


def _place():
    x, y, c = lax.axis_index("x"), lax.axis_index("y"), lax.axis_index("c")
    chips = [(1 - x, y), (x, 1 - y), (1 - x, 1 - y)]
    return x, y, c, chips


def _half(ref, axis, c, rows, cols):
    if axis == 0:
        return ref.at[pl.ds(pl.multiple_of(c * (rows // 2), 16), rows // 2), :]
    return ref.at[:, pl.ds(pl.multiple_of(c * (cols // 2), LANES), cols // 2)]


def _gather_copies(specs, bufs, send, recv):
    x, y, c, chips = _place()
    cps = []
    for t, (_, rows, cols, axis) in enumerate(specs):
        mine = _half(bufs[t].at[2 * x + y], axis, c, rows, cols)
        for k, (cx, cy) in enumerate(chips):
            cps.append(pltpu.make_async_remote_copy(src_ref=mine, dst_ref=mine, send_sem=send.at[3 * t + k],
                                                    recv_sem=recv.at[3 * t + k], device_id=(cx, cy, c), device_id_type=MESH))
    return cps


def gather_start(name, specs, bufs, after):
    nt = len(bufs)
    n = 3 * nt

    def body(*refs):
        send, recv, token = refs[nt + 1], refs[nt + 2], refs[-1]
        for cp in _gather_copies(specs, refs[:nt], send, recv):
            cp.start()
        token[...] = jnp.zeros(TOKEN_SHAPE, F32)

    out = _split_call(
        body, name=name, in_specs=[HBM_SPEC] * nt + [ANY_SPEC],
        out_specs=(SEM_SPEC, SEM_SPEC) + (HBM_SPEC,) * nt + (pl.BlockSpec(memory_space=pltpu.VMEM),),
        out_shape=(pltpu.SemaphoreType.DMA((n,)), pltpu.SemaphoreType.DMA((n,)))
        + tuple(pltpu.HBM(b.shape, b.dtype) for b in bufs) + (SDS(TOKEN_SHAPE, F32),),
        aliases={t: 2 + t for t in range(nt)})(*_in_hbm(bufs), after)
    return out[0], out[1], list(out[2:2 + nt]), out[-1]


def gather_wait(name, specs, send, recv, bufs, after):
    nt = len(bufs)

    def body(*refs):
        for cp in _gather_copies(specs, refs[:nt], refs[nt], refs[nt + 1]):
            cp.wait_send()
            cp.wait_recv()

    out = _split_call(
        body, name=name, in_specs=[HBM_SPEC] * nt + [SEM_SPEC, SEM_SPEC, pl.BlockSpec(memory_space=pl.ANY)],
        out_specs=(HBM_SPEC,) * nt, out_shape=tuple(pltpu.HBM(b.shape, b.dtype) for b in bufs),
        aliases={t: t for t in range(nt)})(*bufs, send, recv, after)
    return list(out)


def gather_pass(name, specs, bufs):
    nt = len(bufs)

    def body(*refs):
        ins, outs = refs[:nt], refs[nt:2 * nt]
        send, recv = refs[2 * nt:]
        x, y, c, chips = _place()
        cps = []
        for t, (_, rows, cols, axis) in enumerate(specs):
            for k, (cx, cy) in enumerate(chips):
                cp = pltpu.make_async_remote_copy(
                    src_ref=_half(ins[t].at[2 * cx + cy], axis, c, rows, cols),
                    dst_ref=_half(outs[t].at[2 * cx + cy], axis, c, rows, cols),
                    send_sem=send.at[3 * t + k], recv_sem=recv.at[3 * t + k], device_id=(x, y, 1 - c), device_id_type=MESH)
                cp.start()
                cps.append(cp)
        for t, (_, rows, cols, axis) in enumerate(specs):
            for k, (cx, cy) in enumerate(chips):
                theirs = _half(outs[t].at[2 * cx + cy], axis, 1 - c, rows, cols)
                pltpu.make_async_remote_copy(src_ref=theirs, dst_ref=theirs, send_sem=send.at[3 * t + k],
                                             recv_sem=recv.at[3 * t + k], device_id=(x, y, 1 - c), device_id_type=MESH).wait_recv()
        for cp in cps:
            cp.wait_send()

    return _pc(body, name=name, in_specs=[HBM_SPEC] * nt, out_specs=[HBM_SPEC] * nt,
               out_shape=[SDS(b.shape, b.dtype) for b in bufs], scratch=[pltpu.SemaphoreType.DMA((3 * nt,))] * 2,
               input_output_aliases={t: t for t in range(nt)})(*bufs)


def reduce_to_sibling(name, grads):
    nt = len(grads)

    def body(*refs):
        srcs, dsts = refs[:nt], refs[nt:2 * nt]
        send, recv = refs[2 * nt:]
        x, y, c, _ = _place()
        cps = []
        for t in range(nt):
            cp = pltpu.make_async_remote_copy(src_ref=srcs[t].at[1 - c], dst_ref=dsts[t], send_sem=send.at[t],
                                              recv_sem=recv.at[t], device_id=(x, y, 1 - c), device_id_type=MESH)
            cp.start()
            cps.append(cp)
        for cp in cps:
            cp.wait()

    return _pc(body, name=name, in_specs=[HBM_SPEC] * nt, out_specs=[HBM_SPEC] * nt,
               out_shape=[SDS(g.shape[1:], g.dtype) for g in grads],
               scratch=[pltpu.SemaphoreType.DMA((nt,))] * 2)(*grads)


def _sibling_copies(grads, lands, send, recv):
    x, y, c, _ = _place()
    return [pltpu.make_async_remote_copy(src_ref=grads[t].at[1 - c], dst_ref=lands[t], send_sem=send.at[t], recv_sem=recv.at[t],
                                         device_id=(x, y, 1 - c), device_id_type=MESH) for t in range(len(grads))]


def sibling_start(name, grads):
    nt = len(grads)
    lands = [lax.empty(g.shape[1:], g.dtype) for g in grads]

    def body(*refs):
        send, recv, token = refs[2 * nt], refs[2 * nt + 1], refs[-1]
        for cp in _sibling_copies(refs[:nt], refs[nt:2 * nt], send, recv):
            cp.start()
        token[...] = jnp.zeros(TOKEN_SHAPE, F32)

    both = list(grads) + lands
    out = _split_call(
        body, name=name, in_specs=[HBM_SPEC] * (2 * nt),
        out_specs=(SEM_SPEC, SEM_SPEC) + (HBM_SPEC,) * (2 * nt) + (pl.BlockSpec(memory_space=pltpu.VMEM),),
        out_shape=(pltpu.SemaphoreType.DMA((nt,)), pltpu.SemaphoreType.DMA((nt,)))
        + tuple(pltpu.HBM(b.shape, b.dtype) for b in both) + (SDS(TOKEN_SHAPE, F32),),
        aliases={t: 2 + t for t in range(2 * nt)})(*_in_hbm(both))
    return out[0], out[1], list(out[2:2 + nt]), list(out[2 + nt:2 + 2 * nt]), out[-1]


def sibling_wait(name, send, recv, grads, lands, after):
    nt = len(grads)

    def body(*refs):
        for cp in _sibling_copies(refs[:nt], refs[nt:2 * nt], refs[2 * nt], refs[2 * nt + 1]):
            cp.wait_send()
            cp.wait_recv()

    both = list(grads) + list(lands)
    out = _split_call(
        body, name=name, in_specs=[HBM_SPEC] * (2 * nt) + [SEM_SPEC, SEM_SPEC, pl.BlockSpec(memory_space=pl.ANY)],
        out_specs=(HBM_SPEC,) * (2 * nt), out_shape=tuple(pltpu.HBM(b.shape, b.dtype) for b in both),
        aliases={t: t for t in range(2 * nt)})(*both, send, recv, after)
    return list(out[:nt]), list(out[nt:])


def add_halves(name, place, grads, gots):
    nt = len(grads)

    def body(place_ref, *refs):
        for t in range(nt):
            val = (refs[t][...].astype(F32) + refs[nt + t][...].astype(F32)).astype(BF16)
            refs[2 * nt + t][...] = val

            @pl.when(pl.program_id(1) == place_ref[1])
            def _():
                refs[3 * nt + t][...] = val

    g_specs, o_specs, part_specs, slot_specs = [], [], [], []
    for g in grads:
        _, n, r, c = g.shape
        tr = _step_rows(r)
        g_specs.append(pl.BlockSpec((None, None, tr, c), lambda i, j, pr: (pr[0], j, i, 0)))
        o_specs.append(pl.BlockSpec((None, tr, c), lambda i, j, pr: (j, i, 0)))
        part_specs.append(pl.BlockSpec((None, tr, c), lambda i, j, pr: (j, i, 0)))
        slot_specs.append(pl.BlockSpec((None, tr, c), lambda i, j, pr: (pr[1], i, 0)))
    gs = pltpu.PrefetchScalarGridSpec(num_scalar_prefetch=1, grid=(GROUP_STEPS, N_CHIPS), in_specs=g_specs + o_specs,
                                      out_specs=part_specs + slot_specs)
    out = _pc(body, name=name, grid_spec=gs, in_specs=None, out_specs=None,
              out_shape=[SDS(g.shape[1:], BF16) for g in grads] * 2, sem=("parallel", "arbitrary"))(place, *grads, *gots)
    return list(out[:nt]), list(out[nt:])


def _chips_copies(parts, slots, send, recv):
    x, y, c, chips = _place()
    cps = []
    for t in range(len(parts)):
        for k, (cx, cy) in enumerate(chips):
            cps.append(pltpu.make_async_remote_copy(src_ref=parts[t].at[2 * cx + cy], dst_ref=slots[t].at[2 * x + y],
                                                    send_sem=send.at[3 * t + k], recv_sem=recv.at[3 * t + k],
                                                    device_id=(cx, cy, c), device_id_type=MESH))
    return cps


def chips_start(name, parts, slots):
    nt = len(parts)
    n = 3 * nt

    def body(*refs):
        send, recv, token = refs[2 * nt], refs[2 * nt + 1], refs[-1]
        for cp in _chips_copies(refs[:nt], refs[nt:2 * nt], send, recv):
            cp.start()
        token[...] = jnp.zeros(TOKEN_SHAPE, F32)

    both = list(parts) + list(slots)
    out = _split_call(
        body, name=name, in_specs=[HBM_SPEC] * (2 * nt),
        out_specs=(SEM_SPEC, SEM_SPEC) + (HBM_SPEC,) * (2 * nt) + (pl.BlockSpec(memory_space=pltpu.VMEM),),
        out_shape=(pltpu.SemaphoreType.DMA((n,)), pltpu.SemaphoreType.DMA((n,)))
        + tuple(pltpu.HBM(b.shape, b.dtype) for b in both) + (SDS(TOKEN_SHAPE, F32),),
        aliases={t: 2 + t for t in range(2 * nt)})(*_in_hbm(both))
    return out[0], out[1], list(out[2:2 + nt]), list(out[2 + nt:2 + 2 * nt]), out[-1]


def chips_wait(name, send, recv, parts, slots, after):
    nt = len(parts)

    def body(*refs):
        for cp in _chips_copies(refs[:nt], refs[nt:2 * nt], refs[2 * nt], refs[2 * nt + 1]):
            cp.wait_send()
            cp.wait_recv()

    both = list(parts) + list(slots)
    out = _split_call(
        body, name=name, in_specs=[HBM_SPEC] * (2 * nt) + [SEM_SPEC, SEM_SPEC, pl.BlockSpec(memory_space=pl.ANY)],
        out_specs=(HBM_SPEC,) * (2 * nt), out_shape=tuple(pltpu.HBM(b.shape, b.dtype) for b in both),
        aliases={t: t for t in range(2 * nt)})(*both, send, recv, after)
    return list(out[nt:])


def sum_chips(name, place, slots):
    nt = len(slots)

    def body(place_ref, *refs):
        for t in range(nt):
            s_ref = refs[t]
            acc = s_ref[0].astype(F32)
            for k in range(1, N_CHIPS):
                acc = acc + s_ref[k].astype(F32)
            refs[nt + t][...] = acc

    in_specs, out_specs = [], []
    for sl in slots:
        n, r, c = sl.shape
        tr = _step_rows(r)
        in_specs.append(pl.BlockSpec((n, tr, c), lambda i, pr: (0, i, 0)))
        out_specs.append(pl.BlockSpec((None, tr, c), lambda i, pr: (pr[0], i, 0)))
    gs = pltpu.PrefetchScalarGridSpec(num_scalar_prefetch=1, grid=(GROUP_STEPS,), in_specs=in_specs, out_specs=out_specs)
    return list(_pc(body, name=name, grid_spec=gs, in_specs=None, out_specs=None,
                    out_shape=[SDS((2,) + sl.shape[1:], F32) for sl in slots], sem=("parallel",))(place, *slots))


def share_with_sibling(name, bufs):
    nt = len(bufs)

    def body(*refs):
        ins, outs = refs[:nt], refs[nt:2 * nt]
        send, recv = refs[2 * nt:]
        x, y, c, _ = _place()
        cps = []
        for t in range(nt):
            cp = pltpu.make_async_remote_copy(src_ref=ins[t].at[c], dst_ref=outs[t].at[c], send_sem=send.at[t], recv_sem=recv.at[t],
                                              device_id=(x, y, 1 - c), device_id_type=MESH)
            cp.start()
            cps.append(cp)
        for t in range(nt):
            theirs = outs[t].at[1 - c]
            pltpu.make_async_remote_copy(src_ref=theirs, dst_ref=theirs, send_sem=send.at[t], recv_sem=recv.at[t],
                                         device_id=(x, y, 1 - c), device_id_type=MESH).wait_recv()
        for cp in cps:
            cp.wait_send()

    return _pc(body, name=name, in_specs=[HBM_SPEC] * nt, out_specs=[HBM_SPEC] * nt,
               out_shape=[SDS(b.shape, F32) for b in bufs], scratch=[pltpu.SemaphoreType.DMA((nt,))] * 2,
               input_output_aliases={t: t for t in range(nt)})(*bufs)


def _adamw(w, g, m, v):
    m = ADAM_B1 * m + (1.0 - ADAM_B1) * g
    v = ADAM_B2 * v + (1.0 - ADAM_B2) * (g * g)
    m_hat = m / (1.0 - ADAM_B1 ** ADAM_STEP)
    v_hat = v / (1.0 - ADAM_B2 ** ADAM_STEP)
    delta = -ADAM_LR * (m_hat / (jnp.sqrt(v_hat) + ADAM_EPS) + ADAM_WD * w)
    return delta, m, v


def adamw_big(name, w, m, v, g0, g1, axis):
    _, r, c = w.shape
    _, rh, ch = g0.shape
    tr = _tile(rh, 256)
    nb = rh // tr
    if axis == 0:
        wspec = pl.BlockSpec((None, tr, ch), lambda l, h, i: (l, h * nb + i, 0))
    else:
        wspec = pl.BlockSpec((None, tr, ch), lambda l, h, i: (l, i, h))
    g0spec = pl.BlockSpec((None, tr, ch), lambda l, h, i: (h * (1 - l), i * (1 - l), 0))
    g1spec = pl.BlockSpec((None, tr, ch), lambda l, h, i: (h * l, i * l, 0))

    def body(w_ref, m_ref, v_ref, g0_ref, g1_ref, go_ref, d_ref, mo_ref, vo_ref):
        def run(g_ref):
            g = g_ref[...]
            delta, mn, vn = _adamw(w_ref[...], g, m_ref[...], v_ref[...])
            go_ref[...] = g
            d_ref[...] = delta
            mo_ref[...] = mn
            vo_ref[...] = vn

        @pl.when(pl.program_id(0) == 0)
        def _():
            run(g0_ref)

        @pl.when(pl.program_id(0) == 1)
        def _():
            run(g1_ref)

    return _pc(body, name=name, grid=(2, 2, nb), in_specs=[wspec, wspec, wspec, g0spec, g1spec], out_specs=[wspec] * 4,
               out_shape=[SDS(w.shape, F32)] * 4, sem=("parallel", "parallel", "parallel"))(w, m, v, g0, g1)


SMALL = (("pre_mix_gain", 2048), ("post_mix_gain", 2048), ("pre_ffn_gain", 2048), ("post_ffn_gain", 2048), ("ple_gain", 2048),
         ("attn_out_gain", 1024), ("hgrn_out_gain", 1024), ("hgrn_lb_logits", 1024), ("attn_sinks", 128))
SMALL_ROWS = sum(2 * w // LANES for _, w in SMALL)
SMALL_PAD = -(-SMALL_ROWS // 8) * 8
LB_ROW = sum(2 * w // LANES for _, w in SMALL[:7])


def _pack_small(parts):
    rows = []
    for nm, w in SMALL:
        a = parts[nm].astype(F32)
        if a.shape[1] != w:
            a = jnp.pad(a, ((0, 0), (0, w - a.shape[1])))
        rows.append(a.reshape(2 * w // LANES, LANES))
    rows.append(jnp.zeros((SMALL_PAD - SMALL_ROWS, LANES), F32))
    return jnp.concatenate(rows, axis=0)


def _unpack_small(packed, widths):
    out, r = {}, 0
    for nm, w in SMALL:
        n = 2 * w // LANES
        out[nm] = packed[r:r + n].reshape(2, w)[:, :widths[nm]]
        r += n
    return out


def allreduce_small(name, packed):
    rows = packed.shape[0]

    def body(x_ref, o_ref, buf, send, recv, own_sem):
        x, y, c, _ = _place()
        me = 4 * x + 2 * y + c
        own = pltpu.make_async_copy(x_ref, buf.at[me], own_sem)
        own.start()
        cps = []
        for k in range(1, 8):
            px, py, pc = x ^ (k >> 2), y ^ ((k >> 1) & 1), c ^ (k & 1)
            cp = pltpu.make_async_remote_copy(src_ref=x_ref, dst_ref=buf.at[me], send_sem=send.at[k - 1], recv_sem=recv.at[k - 1],
                                              device_id=(px, py, pc), device_id_type=MESH)
            cp.start()
            cps.append(cp)
        for k in range(1, 8):
            px, py, pc = x ^ (k >> 2), y ^ ((k >> 1) & 1), c ^ (k & 1)
            slot = buf.at[4 * px + 2 * py + pc]
            pltpu.make_async_remote_copy(src_ref=slot, dst_ref=slot, send_sem=send.at[k - 1], recv_sem=recv.at[k - 1],
                                         device_id=(px, py, pc), device_id_type=MESH).wait_recv()
        for cp in cps:
            cp.wait_send()
        own.wait()
        acc = buf[0]
        for k in range(1, 8):
            acc = acc + buf[k]
        o_ref[...] = acc

    vm = pl.BlockSpec(memory_space=pltpu.VMEM)
    return _pc(body, name=name, in_specs=[vm], out_specs=vm, out_shape=SDS((rows, LANES), F32),
               scratch=[pltpu.VMEM((8, rows, LANES), F32), pltpu.SemaphoreType.DMA((7,)), pltpu.SemaphoreType.DMA((7,)),
                        pltpu.SemaphoreType.DMA])(packed)


def adamw_small(name, w, m, v, g):
    rows = w.shape[0]
    n = HGRN_WIDTH // LANES

    def body(w_ref, m_ref, v_ref, g_ref, go_ref, d_ref, mo_ref, vo_ref):
        go_ref[...] = g_ref[...]
        l0 = w_ref[LB_ROW:LB_ROW + n, :]
        l1 = w_ref[LB_ROW + n:LB_ROW + 2 * n, :]
        mx = jnp.maximum(l0, l1)
        e0, e1 = jnp.exp(l0 - mx), jnp.exp(l1 - mx)
        s0, s1 = e0 / (e0 + e1), e1 / (e0 + e1)
        dlb1 = g_ref[LB_ROW + n:LB_ROW + 2 * n, :]
        inner = s1 * dlb1
        go_ref[LB_ROW:LB_ROW + n, :] = s0 * (0.0 - inner)
        go_ref[LB_ROW + n:LB_ROW + 2 * n, :] = s1 * (dlb1 - inner)
        delta, mn, vn = _adamw(w_ref[...], go_ref[...], m_ref[...], v_ref[...])
        d_ref[...] = delta
        mo_ref[...] = mn
        vo_ref[...] = vn

    vm = pl.BlockSpec(memory_space=pltpu.VMEM)
    return _pc(body, name=name, in_specs=[vm] * 4, out_specs=[vm] * 4, out_shape=[SDS((rows, LANES), F32)] * 4)(w, m, v, g)


def _layer_fwd(l, x, h1, p_l, w_in_g, rest_of_weights, gains, cos, sin, sinks, lb_logits, g_next, target):
    n = f"l{l}_"
    proj = mm_col(n + "in_proj", h1, w_in_g)
    qp, kp, vp = rope_qkv(n + "rope_qkv", proj, cos, sin)
    attn = attn_fwd(n + "attn_fwd", qp, kp, vp, sinks)
    o, states = hgrn_fwd(n + "hgrn_fwd", proj, lb_logits, l)
    cat = mix_out_fwd(n + "mix_out_fwd", attn, o, proj, gains["attn_out_gain"], gains["hgrn_out_gain"])
    rest, token = rest_of_weights(cat)
    wts = dict(rest, w_in=w_in_g)
    if token is not None:
        gains = _with_token(gains, "post_mix_gain", token)
    m, x1, h2 = out_proj_post_mix(n + "out_proj_post_mix", cat, wts["w_out"], gains["post_mix_gain"], x, gains["pre_ffn_gain"])
    g, u, a = ffn_gate_up(n + "ffn_gate_up", h2, wts["w_ffn_gate"], wts["w_ffn_up"])
    f = mm_row(n + "ffn_down", a, wts["w_ffn_down"])
    x2, h3 = post_pre_norm(n + "post_ffn", f, gains["post_ffn_gain"], x1, gains["ple_gain"])
    saved = dict(x=x, h1=h1, proj=proj, qp=qp, kp=kp, vp=vp, attn=attn, o=o, states=states, cat=cat, m=m, x1=x1, h2=h2,
                 g=g, u=u, a=a, f=f, x2=x2, h3=h3, p=p_l)
    if target is None:
        z, pp, *out = ple_gate_fwd_mid(n + "ple_gate_fwd", h3, wts["w_ple_gate"], p_l, wts["w_ple_proj"], x2, g_next)
        saved.update(z=z, pp=pp)
    else:
        dy, dpp, dz, loss = ple_gate_fwd_loss(n + "ple_gate_loss", h3, wts["w_ple_gate"], p_l, wts["w_ple_proj"], x2, target)
        out = [dy, loss]
        saved.update(dpp=dpp, dz=dz)
    return out, saved, wts


EARLY = ("w_ple_gate", "w_ple_proj", "w_ffn_down", "w_ffn_gate", "w_ffn_up")
LATE = ("w_out", "w_in")


def _layer_bwd_ffn(l, dx3, sv, wts, gains, after=None, hook=None):
    n = f"l{l}_"
    dpp, dz = (sv["dpp"], sv["dz"]) if "dz" in sv else ple_bwd(n + "ple_bwd", dx3, sv["z"], sv["pp"],
                                                                 dx3 if after is None else after)
    dh3 = mm_row_t(n + "ple_gate_dx", dz, wts["w_ple_gate"])
    if hook is not None:
        gains = _with_token(gains, "ple_gain", hook(dh3))
    dx2, df, d_ple_gain, d_post_ffn = norm_bwd_pair(n + "ple_post_ffn_bwd", sv["x2"], gains["ple_gain"], dh3, dx3, sv["f"],
                                                    gains["post_ffn_gain"])
    dg, du = ffn_down_bwd(n + "ffn_down_bwd", df, wts["w_ffn_down"], sv["g"], sv["u"])
    big = dict(
        w_ple_gate=mm_wg_row(n + "ple_gate_dw", sv["h3"], dz),
        w_ple_proj=mm_wg_col(n + "ple_proj_dw", sv["p"], dpp),
        w_ffn_down=mm_wg_row(n + "ffn_down_dw", sv["a"], df),
        w_ffn_gate=mm_wg_col(n + "ffn_gate_dw", sv["h2"], dg),
        w_ffn_up=mm_wg_col(n + "ffn_up_dw", sv["h2"], du),
    )
    return dict(dx2=dx2, dg=dg, du=du), big, dict(ple_gain=d_ple_gain, post_ffn_gain=d_post_ffn)


def _layer_bwd_mix(l, st, sv, wts, gains, cos, sin, sinks, lb_logits, after=None, hook=None):
    n = f"l{l}_"
    dh2 = mm_col_t(n + "ffn_gate_dx", st["dg"], wts["w_ffn_gate"], after=after)
    dh2 = mm_col_t(n + "ffn_up_dx", st["du"], wts["w_ffn_up"], add=dh2)
    if hook is not None:
        gains = _with_token(gains, "pre_ffn_gain", hook(dh2))
    dx1, dm, d_pre_ffn, d_post_mix = norm_bwd_pair(n + "pre_ffn_post_mix_bwd", sv["x1"], gains["pre_ffn_gain"], dh2, st["dx2"],
                                                   sv["m"], gains["post_mix_gain"])
    dcat = mm_row_t(n + "out_proj_dx", dm, wts["w_out"])
    dattn, do, dhg, d_attn_gain, d_hgrn_gain = mix_out_bwd(n + "mix_out_bwd", dcat, sv["attn"], sv["o"], sv["proj"],
                                                            gains["attn_out_gain"], gains["hgrn_out_gain"])
    dqp, dkc, dkp, dvc, dvp, dsinks = attn_bwd(n + "attn_bwd", sv["qp"], sv["kp"], sv["vp"], sinks, dattn)
    dqkv = rope_bwd(n + "rope_bwd", dqp, dkc, dkp, dvc, dvp, cos, sin)
    dhq, dhf, dhi, dlb = hgrn_bwd(n + "hgrn_bwd", sv["proj"], lb_logits, l, sv["states"], do)
    dproj, dh1 = in_proj_bwd(n + "in_proj_dx", [dqkv, dhq, dhf, dhi, dhg], wts["w_in"])
    dx, d_pre_mix = norm_bwd(n + "pre_mix_bwd", sv["x"], gains["pre_mix_gain"], dh1, dx1)
    big = dict(w_out=mm_wg_row(n + "out_proj_dw", sv["cat"], dm), w_in=mm_wg_col(n + "in_proj_dw", sv["h1"], dproj))
    small = dict(pre_mix_gain=d_pre_mix, post_mix_gain=d_post_mix, pre_ffn_gain=d_pre_ffn, attn_out_gain=d_attn_gain,
                 hgrn_out_gain=d_hgrn_gain, hgrn_lb_logits=dlb, attn_sinks=dsinks)
    return dx, big, small


def _reduce_start(tag, names, big, place):
    grads = [big[nm] for nm in names]
    got = reduce_to_sibling(tag + "_reduce_to_sibling", grads)
    parts, slots = add_halves(tag + "_add", place, grads, got)
    return chips_start(tag + "_chips_start", parts, slots)


def _reduce_begin(tag, names, big):
    return sibling_start(tag + "_sibling_start", [big[nm] for nm in names])


def _reduce_chips(tag, names, begun, place, after):
    send, recv, grads, lands, _ = begun
    grads, got = sibling_wait(tag + "_sibling_wait", send, recv, grads, lands, after)
    parts, slots = add_halves(tag + "_add", place, grads, got)
    return chips_start(tag + "_chips_start", parts, slots)


def _reduce_finish(tag, names, started, place, after):
    send, recv, parts, slots, _ = started
    slots = chips_wait(tag + "_chips_wait", send, recv, parts, slots, after)
    bufs = sum_chips(tag + "_sum", place, slots)
    return dict(zip(names, share_with_sibling(tag + "_share_with_sibling", bufs)))


def _with_token(gains, name, token):
    out = dict(gains)
    out[name] = gains[name] + token[0, 0]
    return out


def kernel(x, p, positions, w_in, attn_sinks, hgrn_lb_logits, attn_out_gain, hgrn_out_gain, w_out, pre_mix_gain, post_mix_gain, pre_ffn_gain, post_ffn_gain, w_ffn_gate, w_ffn_up, w_ffn_down, ple_gain, w_ple_gate, w_ple_proj, loss_target, m_w_in, m_attn_sinks, m_hgrn_lb_logits, m_attn_out_gain, m_hgrn_out_gain, m_w_out, m_pre_mix_gain, m_post_mix_gain, m_pre_ffn_gain, m_post_ffn_gain, m_w_ffn_gate, m_w_ffn_up, m_w_ffn_down, m_ple_gain, m_w_ple_gate, m_w_ple_proj, v_w_in, v_attn_sinks, v_hgrn_lb_logits, v_attn_out_gain, v_hgrn_out_gain, v_w_out, v_pre_mix_gain, v_post_mix_gain, v_pre_ffn_gain, v_post_ffn_gain, v_w_ffn_gate, v_w_ffn_up, v_w_ffn_down, v_ple_gain, v_w_ple_gate, v_w_ple_proj):
    given = dict(locals())
    depth = 2
    place = jnp.stack([lax.axis_index("c"), 2 * lax.axis_index("x") + lax.axis_index("y")]).astype(jnp.int32)
    xs = x[0]
    tgt = loss_target[0]
    pos_col = positions.reshape(-1, 1)
    half = 32
    inv_freq = ROPE_THETA ** (-jnp.arange(half, dtype=F32) / half)
    inv_freq = jnp.tile(inv_freq, 4).reshape(1, LANES)
    gains = [{nm: given[nm][l:l + 1] for nm, _ in SMALL[:7]} for l in range(depth)]
    names = [nm for nm, *_ in BIG]
    first, others = names[:1], names[1:]

    def specs(nms):
        return [BIG_BY_NAME[nm] for nm in nms]

    def cast(tag, l, nms, after):
        return cast_to_slots(tag + "_cast", place, [given[nm] for nm in nms], l, after)

    def finish_gather(tag, nms, started, after):
        bufs = gather_wait(tag + "_gather_wait", specs(nms), started[0], started[1], started[2], after)
        return dict(zip(nms, gather_pass(tag + "_gather_pass", specs(nms), bufs)))

    g0a = gather_start("l0a_gather_start", specs(first), cast("l0a", 0, first, place), place)
    g0b = gather_start("l0b_gather_start", specs(others), cast("l0b", 0, others, g0a[3]), g0a[3])
    started = {}

    def rest_of_layer0(after):
        got = finish_gather("l0b", others, g0b, after)
        started["l1"] = gather_start("l1_gather_start", specs(names), l1_shards, got["w_out"])
        return got, started["l1"][3]

    cos, sin = rope_tables("rope_tables", pos_col, inv_freq)
    h1 = pre_norm("l0_pre_mix", xs, _with_token(gains[0], "pre_mix_gain", g0b[3])["pre_mix_gain"])
    l1_shards = cast("l1", 1, names, h1)
    w_in0 = finish_gather("l0a", first, g0a, l1_shards[0])["w_in"]
    (x_mid, h1_next), sv0, wts0 = _layer_fwd(0, xs, h1, p[0, 0], w_in0, rest_of_layer0, gains[0], cos, sin, attn_sinks[0],
                                             hgrn_lb_logits, gains[1]["pre_mix_gain"], None)
    wts1 = finish_gather("l1", names, started["l1"], x_mid)
    (dy, loss_part), sv1, _ = _layer_fwd(1, x_mid, h1_next, p[1, 0], wts1["w_in"], lambda after: (wts1, None), gains[1], cos, sin,
                                         attn_sinks[1], hgrn_lb_logits, None, tgt)

    st1, early1, small1 = _layer_bwd_ffn(1, dy, sv1, wts1, gains[1])
    dx_mid, late1, small1b = _layer_bwd_mix(1, st1, sv1, wts1, gains[1], cos, sin, attn_sinks[1], hgrn_lb_logits)
    big1, small1 = {**early1, **late1}, {**small1, **small1b}
    chips = {}

    def chips_after(tag, nms, begun):
        def hook(x):
            chips[tag] = _reduce_chips(tag, nms, begun, place, x)
            return chips[tag][4]
        return hook

    b1 = _reduce_begin("l1", names, big1)
    st0, early0, small0 = _layer_bwd_ffn(0, dx_mid, sv0, wts0, gains[0], after=b1[4], hook=chips_after("l1", names, b1))
    b0e = _reduce_begin("l0e", EARLY, early0)
    dx0, late0, small0b = _layer_bwd_mix(0, st0, sv0, wts0, gains[0], cos, sin, attn_sinks[0], hgrn_lb_logits, after=b0e[4],
                                         hook=chips_after("l0e", EARLY, b0e))
    r1, r0e = chips["l1"], chips["l0e"]
    small0 = {**small0, **small0b}
    r0l = _reduce_start("l0l", LATE, late0, place)
    red1 = _reduce_finish("l1", names, r1, place, r0l[4])
    red0 = _reduce_finish("l0e", EARLY, r0e, place, red1[names[-1]])

    loss = lax.psum(loss_part[0, 0], ("x", "y", "c"))
    grad_x = dx0[None]

    out_big = {}
    for nm in EARLY:
        out_big[nm] = adamw_big("adamw_" + nm, given[nm], given["m_" + nm], given["v_" + nm], red0[nm], red1[nm], BIG_BY_NAME[nm][3])
    red0.update(_reduce_finish("l0l", LATE, r0l, place, out_big[EARLY[-1]][3]))
    for nm in LATE:
        out_big[nm] = adamw_big("adamw_" + nm, given[nm], given["m_" + nm], given["v_" + nm], red0[nm], red1[nm], BIG_BY_NAME[nm][3])

    widths = {nm: given[nm].shape[1] for nm, _ in SMALL}
    small_g = {nm: jnp.concatenate([small0[nm][:, :widths[nm]] if nm != "attn_sinks" else small0[nm][:, :LANES],
                                    small1[nm][:, :widths[nm]] if nm != "attn_sinks" else small1[nm][:, :LANES]], axis=0)
               for nm, _ in SMALL}
    g_sum = allreduce_small("allreduce_small", _pack_small(small_g))
    sm = adamw_small("adamw_small", _pack_small({nm: given[nm] for nm, _ in SMALL}),
                     _pack_small({nm: given["m_" + nm] for nm, _ in SMALL}),
                     _pack_small({nm: given["v_" + nm] for nm, _ in SMALL}), g_sum)
    out_small = [_unpack_small(a, widths) for a in sm]

    order = ["w_in", "attn_sinks", "hgrn_lb_logits", "attn_out_gain", "hgrn_out_gain", "w_out", "pre_mix_gain", "post_mix_gain",
             "pre_ffn_gain", "post_ffn_gain", "w_ffn_gate", "w_ffn_up", "w_ffn_down", "ple_gain", "w_ple_gate", "w_ple_proj"]
    res = [loss, grad_x]
    for k in range(4):
        for nm in order:
            res.append(out_big[nm][k] if nm in out_big else out_small[k][nm])
    return tuple(res)
```

```python
import jax
import jax.numpy as jnp
from jax import lax
from jax.experimental import pallas as pl
from jax.experimental.pallas import tpu as pltpu

F32, BF16 = jnp.float32, jnp.bfloat16
SDS = jax.ShapeDtypeStruct
MESH = pl.DeviceIdType.MESH

D_MODEL = 2048
ATTN_WIDTH = 1024
HGRN_WIDTH = 1024
KV_WIDTH = 256
N_Q_HEADS = 16
N_KV_HEADS = 4
Q_PER_KV = 4
WINDOW = 128
MASK_VALUE = -1e30
ROPE_THETA = 10000.0
HGRN_HEADS = 8
HGRN_CHUNK = 16
RMS_EPS = 1e-6
LANES = 128
N_CHIPS = 4
COL_Q, COL_K, COL_V, COL_HQ, COL_HF, COL_HI, COL_HG = 0, 8, 10, 12, 20, 28, 36

ADAM_LR, ADAM_B1, ADAM_B2, ADAM_EPS, ADAM_WD, ADAM_STEP = 0.001, 0.9, 0.999, 1e-08, 0.01, 10

VMEM_LIMIT = 56 * 1024 * 1024
ROW_TILE = 256
ROW_TILE_WIDE = 512

_NN = (((1,), (0,)), ((), ()))
_NT = (((1,), (1,)), ((), ()))
_TN = (((0,), (0,)), ((), ()))


def _pc(body, *, name, out_shape, in_specs, out_specs, grid=(), scratch=(), sem=None, grid_spec=None, **kw):
    params = dict(vmem_limit_bytes=VMEM_LIMIT)
    if sem is not None:
        params["dimension_semantics"] = sem
    if grid_spec is not None:
        return pl.pallas_call(body, name=name, out_shape=out_shape, grid_spec=grid_spec,
                              compiler_params=pltpu.CompilerParams(**params), **kw)
    return pl.pallas_call(body, name=name, out_shape=out_shape, grid=grid, in_specs=in_specs, out_specs=out_specs,
                          scratch_shapes=list(scratch), compiler_params=pltpu.CompilerParams(**params), **kw)


def _sigmoid(x):
    return 1.0 / (1.0 + jnp.exp(-x))


def _rstd(x):
    return lax.rsqrt(jnp.mean(x * x, axis=-1, keepdims=True) + RMS_EPS)


def _rows(t, w, col=0):
    return pl.BlockSpec((t, w), lambda i, col=col: (i, col))


def _fixed(shape):
    return pl.BlockSpec(shape, lambda *_: (0,) * len(shape))


def _mm(name, a, b, *, dims, grid, a_spec, b_spec, o_spec, out_shape, parts=1, add=None, add_spec=None, after=None):
    def body(*refs):
        a_ref, b_ref, o_ref = refs[0], refs[1], refs[-1]
        if parts == 1:
            r = lax.dot_general(a_ref[...].astype(BF16), b_ref[...].astype(BF16), dims, preferred_element_type=F32)
        else:
            w = a_ref.shape[1] // parts
            r = None
            for j in range(parts):
                t = lax.dot_general(a_ref[:, j * w:(j + 1) * w].astype(BF16), b_ref[j].astype(BF16), dims,
                                    preferred_element_type=F32)
                r = t if r is None else r + t
        if add is not None:
            r = r + refs[2][...]
        o_ref[...] = r.astype(o_ref.dtype)

    ins = [a, b] + ([] if add is None else [add]) + ([] if after is None else [after])
    specs = [a_spec, b_spec] + ([] if add is None else [add_spec]) + ([] if after is None else [pl.BlockSpec(memory_space=pl.ANY)])
    return _pc(body, name=name, grid=grid, in_specs=specs, out_specs=o_spec, out_shape=out_shape,
               sem=("parallel",) * len(grid))(*ins)


def _tile(n, t):
    if n <= t:
        return n
    while n % t:
        t //= 2
    assert t % 8 == 0
    return t


def mm_col(name, a, wg, out_dtype=F32):
    s, k = a.shape
    _, _, n = wg.shape
    tm = _tile(s, 512)
    return _mm(name, a, wg, dims=_NN, grid=(N_CHIPS, s // tm),
               a_spec=pl.BlockSpec((tm, k), lambda j, i: (i, 0)),
               b_spec=pl.BlockSpec((None, k, n), lambda j, i: (j, 0, 0)),
               o_spec=pl.BlockSpec((tm, n), lambda j, i: (i, j)),
               out_shape=SDS((s, N_CHIPS * n), out_dtype))


def mm_row(name, a, wg, out_dtype=F32):
    s, _ = a.shape
    _, r, n = wg.shape
    tm = _tile(s, 512)
    tn = _tile(n, 1024 if r > 512 else 2048)
    return _mm(name, a, wg, dims=_NN, grid=(n // tn, s // tm), parts=N_CHIPS,
               a_spec=pl.BlockSpec((tm, N_CHIPS * r), lambda j, i: (i, 0)),
               b_spec=pl.BlockSpec((N_CHIPS, r, tn), lambda j, i: (0, 0, j)),
               o_spec=pl.BlockSpec((tm, tn), lambda j, i: (i, j)),
               out_shape=SDS((s, n), out_dtype))


def mm_col_t(name, dy, wg, add=None, out_dtype=F32, after=None):
    s, _ = dy.shape
    _, k, n = wg.shape
    tm = _tile(s, 512)
    tk = _tile(k, 1024)
    return _mm(name, dy, wg, dims=_NT, grid=(k // tk, s // tm), parts=N_CHIPS,
               a_spec=pl.BlockSpec((tm, N_CHIPS * n), lambda j, i: (i, 0)),
               b_spec=pl.BlockSpec((N_CHIPS, tk, n), lambda j, i: (0, j, 0)),
               o_spec=pl.BlockSpec((tm, tk), lambda j, i: (i, j)),
               add=add, add_spec=pl.BlockSpec((tm, tk), lambda j, i: (i, j)),
               out_shape=SDS((s, k), out_dtype), after=after)


def in_proj_bwd(name, pieces, wg):
    s = pieces[0].shape[0]
    _, k, n = wg.shape
    tm = _tile(s, 512)
    tk = _tile(k, 512)
    width = sum(p.shape[1] for p in pieces)
    npc = len(pieces)
    segments, start = [], 0
    for t, p in enumerate(pieces):
        lo = start
        while lo < start + p.shape[1]:
            hi = min(start + p.shape[1], (lo // n + 1) * n)
            segments.append((t, lo - start, hi - start, lo // n, lo % n, lo % n + hi - lo))
            lo = hi
        start += p.shape[1]

    def body(*refs):
        w_ref, dproj_ref, o_ref = refs[npc], refs[npc + 1], refs[npc + 2]

        @pl.when(pl.program_id(1) == 0)
        def _():
            off = 0
            for t in range(npc):
                dproj_ref[:, off:off + pieces[t].shape[1]] = refs[t][...]
                off += pieces[t].shape[1]

        r = None
        for t, a, b, j, c, d in segments:
            part = lax.dot_general(refs[t][:, a:b], w_ref[j, :, c:d], _NT, preferred_element_type=F32)
            r = part if r is None else r + part
        o_ref[...] = r

    return _pc(body, name=name, grid=(s // tm, k // tk),
               in_specs=[pl.BlockSpec((tm, p.shape[1]), lambda i, j: (i, 0)) for p in pieces]
               + [pl.BlockSpec((N_CHIPS, tk, n), lambda i, j: (0, j, 0))],
               out_specs=[pl.BlockSpec((tm, width), lambda i, j: (i, 0)), pl.BlockSpec((tm, tk), lambda i, j: (i, j))],
               out_shape=[SDS((s, width), BF16), SDS((s, k), F32)], sem=("parallel", "arbitrary"))(*pieces, wg)


def mm_row_t(name, dy, wg, out_dtype=F32):
    s, n = dy.shape
    _, r, _ = wg.shape
    tm = _tile(s, 512)

    def body(dy_ref, w_ref, o_ref):
        dyv = dy_ref[...].astype(BF16)
        for j in range(N_CHIPS):
            o_ref[:, j * r:(j + 1) * r] = lax.dot_general(dyv, w_ref[j], _NT, preferred_element_type=F32).astype(out_dtype)

    return _pc(body, name=name, grid=(s // tm,), in_specs=[_rows(tm, n), _fixed(wg.shape)], out_specs=_rows(tm, N_CHIPS * r),
               out_shape=SDS((s, N_CHIPS * r), out_dtype), sem=("parallel",))(dy, wg)


def norm_bwd_pair(name, x_a, gain_a, dy, dres, x_b, gain_b):
    s, d = x_a.shape
    t = _tile(s, ROW_TILE)

    def one(xv, g, dyv):
        r = _rstd(xv)
        xh = xv * r
        dyg = dyv * g
        return r * (dyg - xh * jnp.mean(dyg * xh, axis=-1, keepdims=True)), jnp.sum(dyv * xh, axis=0, keepdims=True)

    def body(xa_ref, ga_ref, dy_ref, r_ref, xb_ref, gb_ref, dx_ref, db_ref, dga_ref, dgb_ref):
        dx, pa = one(xa_ref[...], ga_ref[...], dy_ref[...])
        dx = dx + r_ref[...]
        dx_ref[...] = dx
        db, pb = one(xb_ref[...], gb_ref[...], dx)
        db_ref[...] = db.astype(BF16)

        @pl.when(pl.program_id(0) == 0)
        def _():
            dga_ref[...] = pa
            dgb_ref[...] = pb

        @pl.when(pl.program_id(0) > 0)
        def _():
            dga_ref[...] += pa
            dgb_ref[...] += pb

    row, gain = _rows(t, d), _fixed((1, d))
    return _pc(body, name=name, grid=(s // t,), in_specs=[row, gain, row, row, row, gain], out_specs=[row, row, gain, gain],
               out_shape=[SDS((s, d), F32), SDS((s, d), BF16), SDS((1, d), F32), SDS((1, d), F32)],
               sem=("arbitrary",))(x_a, gain_a, dy, dres, x_b, gain_b)


def ffn_gate_up(name, h, wg_gate, wg_up):
    s, k = h.shape
    _, _, n = wg_gate.shape
    tm = _tile(s, 512)

    def body(h_ref, wg_ref, wu_ref, g_ref, u_ref, a_ref):
        hv = h_ref[...]
        g = jnp.dot(hv, wg_ref[...], preferred_element_type=F32)
        u = jnp.dot(hv, wu_ref[...], preferred_element_type=F32)
        g_ref[...] = g
        u_ref[...] = u
        a_ref[...] = ((g * _sigmoid(g)) * u).astype(BF16)

    wspec = pl.BlockSpec((None, k, n), lambda j, i: (j, 0, 0))
    ospec = pl.BlockSpec((tm, n), lambda j, i: (i, j))
    return _pc(body, name=name, grid=(N_CHIPS, s // tm), in_specs=[pl.BlockSpec((tm, k), lambda j, i: (i, 0)), wspec, wspec],
               out_specs=[ospec] * 3, out_shape=[SDS((s, N_CHIPS * n), F32)] * 2 + [SDS((s, N_CHIPS * n), BF16)],
               sem=("parallel", "parallel"))(h, wg_gate, wg_up)


def ffn_down_bwd(name, df, wg_down, g, u):
    s, n = df.shape
    _, r, _ = wg_down.shape
    tm = _tile(s, 512)

    def body(df_ref, w_ref, g_ref, u_ref, dg_ref, du_ref):
        da = lax.dot_general(df_ref[...], w_ref[...], _NT, preferred_element_type=F32)
        gv = g_ref[...]
        sg = _sigmoid(gv)
        du_ref[...] = (da * (gv * sg)).astype(BF16)
        dg_ref[...] = ((da * u_ref[...]) * (sg * (1.0 + gv * (1.0 - sg)))).astype(BF16)

    cspec = pl.BlockSpec((tm, r), lambda j, i: (i, j))
    return _pc(body, name=name, grid=(N_CHIPS, s // tm),
               in_specs=[pl.BlockSpec((tm, n), lambda j, i: (i, 0)), pl.BlockSpec((None, r, n), lambda j, i: (j, 0, 0)), cspec, cspec],
               out_specs=[cspec] * 2, out_shape=[SDS((s, N_CHIPS * r), BF16)] * 2,
               sem=("parallel", "parallel"))(df, wg_down, g, u)


def mm_wg_col(name, a, dy):
    s, k = a.shape
    n = dy.shape[1] // N_CHIPS
    tm = _tile(k // 2, 512)
    hb = (k // 2) // tm
    return _mm(name, a, dy, dims=_TN, grid=(N_CHIPS, k // tm),
               a_spec=pl.BlockSpec((s, tm), lambda j, i: (0, i)),
               b_spec=pl.BlockSpec((s, n), lambda j, i: (0, j)),
               o_spec=pl.BlockSpec((None, None, tm, n), lambda j, i: (i // hb, j, i % hb, 0)),
               out_shape=SDS((2, N_CHIPS, k // 2, n), BF16))


def mm_wg_row(name, a, dy):
    s, n = dy.shape
    r = a.shape[1] // N_CHIPS
    tn = _tile(n // 2, 512)
    nb = (n // 2) // tn
    return _mm(name, a, dy, dims=_TN, grid=(N_CHIPS, n // tn),
               a_spec=pl.BlockSpec((s, r), lambda j, i: (0, j)),
               b_spec=pl.BlockSpec((s, tn), lambda j, i: (0, i)),
               o_spec=pl.BlockSpec((None, None, r, tn), lambda j, i: (i // nb, j, 0, i % nb)),
               out_shape=SDS((2, N_CHIPS, r, n // 2), BF16))


def pre_norm(name, x, gain):
    s, d = x.shape
    t = _tile(s, ROW_TILE_WIDE)

    def body(x_ref, g_ref, o_ref):
        xv = x_ref[...]
        o_ref[...] = ((xv * _rstd(xv)) * g_ref[...]).astype(BF16)

    return _pc(body, name=name, grid=(s // t,), in_specs=[_rows(t, d), _fixed((1, d))], out_specs=_rows(t, d),
               out_shape=SDS((s, d), BF16), sem=("parallel",))(x, gain)


def post_pre_norm(name, m, g_post, res, g_pre):
    s, d = m.shape
    t = _tile(s, ROW_TILE_WIDE)

    def body(m_ref, gp_ref, r_ref, gn_ref, x_ref, h_ref):
        mv = m_ref[...]
        xn = r_ref[...] + (mv * _rstd(mv)) * gp_ref[...]
        x_ref[...] = xn
        h_ref[...] = ((xn * _rstd(xn)) * gn_ref[...]).astype(BF16)

    return _pc(body, name=name, grid=(s // t,),
               in_specs=[_rows(t, d), _fixed((1, d)), _rows(t, d), _fixed((1, d))],
               out_specs=[_rows(t, d), _rows(t, d)], out_shape=[SDS((s, d), F32), SDS((s, d), BF16)],
               sem=("parallel",))(m, g_post, res, g_pre)


def _row_dot(a_ref, w_ref):
    r = w_ref.shape[1]
    out = None
    for j in range(N_CHIPS):
        part = jnp.dot(a_ref[:, j * r:(j + 1) * r], w_ref[j], preferred_element_type=F32)
        out = part if out is None else out + part
    return out


def _row_dot_specs(t, a, wg):
    return [_rows(t, a.shape[1]), _fixed(wg.shape)]


def out_proj_post_mix(name, a, wg, g_post, res, g_pre):
    s, d = res.shape
    t = _tile(s, ROW_TILE)

    def body(a_ref, w_ref, gp_ref, r_ref, gn_ref, m_ref, x_ref, h_ref):
        mv = _row_dot(a_ref, w_ref)
        m_ref[...] = mv
        xn = r_ref[...] + (mv * _rstd(mv)) * gp_ref[...]
        x_ref[...] = xn
        h_ref[...] = ((xn * _rstd(xn)) * gn_ref[...]).astype(BF16)

    return _pc(body, name=name, grid=(s // t,),
               in_specs=_row_dot_specs(t, a, wg) + [_fixed((1, d)), _rows(t, d), _fixed((1, d))],
               out_specs=[_rows(t, d)] * 3, out_shape=[SDS((s, d), F32), SDS((s, d), F32), SDS((s, d), BF16)],
               sem=("parallel",))(a, wg, g_post, res, g_pre)


def _col_dot(p_ref, w_ref):
    pv = p_ref[...].astype(BF16)
    return jnp.concatenate([jnp.dot(pv, w_ref[j], preferred_element_type=F32) for j in range(N_CHIPS)], axis=1)


def ple_gate_fwd_mid(name, a, wg, p, wg_proj, x2, g_next):
    s, d = x2.shape
    t = _tile(s, ROW_TILE)

    def body(a_ref, w_ref, p_ref, wp_ref, x_ref, g_ref, z_ref, pp_ref, xo_ref, h_ref):
        z = _row_dot(a_ref, w_ref)
        z_ref[...] = z
        pv = _col_dot(p_ref, wp_ref)
        pp_ref[...] = pv
        xn = x_ref[...] + pv * _sigmoid(z)
        xo_ref[...] = xn
        h_ref[...] = ((xn * _rstd(xn)) * g_ref[...]).astype(BF16)

    return _pc(body, name=name, grid=(s // t,),
               in_specs=_row_dot_specs(t, a, wg) + [_rows(t, p.shape[1]), _fixed(wg_proj.shape), _rows(t, d), _fixed((1, d))],
               out_specs=[_rows(t, d)] * 4, out_shape=[SDS((s, d), F32)] * 3 + [SDS((s, d), BF16)],
               sem=("parallel",))(a, wg, p, wg_proj, x2, g_next)


def ple_gate_fwd_loss(name, a, wg, p, wg_proj, x2, target):
    s, d = x2.shape
    t = _tile(s, ROW_TILE)

    def body(a_ref, w_ref, p_ref, wp_ref, x_ref, t_ref, dy_ref, dpp_ref, dz_ref, l_ref):
        gate = _sigmoid(_row_dot(a_ref, w_ref))
        pv = _col_dot(p_ref, wp_ref)
        err = (x_ref[...] + pv * gate) - t_ref[...]
        dy = err * (1.0 / d)
        dy_ref[...] = dy
        dpp_ref[...] = (dy * gate).astype(BF16)
        dz_ref[...] = ((dy * pv) * (gate * (1.0 - gate))).astype(BF16)
        part = jnp.sum(jnp.sum(err * err, axis=-1, keepdims=True), axis=0, keepdims=True) * (0.5 / d)

        @pl.when(pl.program_id(0) == 0)
        def _():
            l_ref[...] = part

        @pl.when(pl.program_id(0) > 0)
        def _():
            l_ref[...] += part

    return _pc(body, name=name, grid=(s // t,),
               in_specs=_row_dot_specs(t, a, wg) + [_rows(t, p.shape[1]), _fixed(wg_proj.shape), _rows(t, d), _rows(t, d)],
               out_specs=[_rows(t, d), _rows(t, d), _rows(t, d), _fixed((1, 1))],
               out_shape=[SDS((s, d), F32), SDS((s, d), BF16), SDS((s, d), BF16), SDS((1, 1), F32)],
               sem=("arbitrary",))(a, wg, p, wg_proj, x2, target)


ANY_SPEC = pl.BlockSpec(memory_space=pl.ANY)


def ple_bwd(name, dx3, z, pp, after):
    s, d = z.shape
    t = _tile(s, ROW_TILE_WIDE)

    def body(d_ref, z_ref, p_ref, after_ref, dpp_ref, dz_ref):
        gate = _sigmoid(z_ref[...])
        dv = d_ref[...]
        dpp_ref[...] = (dv * gate).astype(BF16)
        dz_ref[...] = ((dv * p_ref[...]) * (gate * (1.0 - gate))).astype(BF16)

    return _pc(body, name=name, grid=(s // t,), in_specs=[_rows(t, d)] * 3 + [ANY_SPEC], out_specs=[_rows(t, d)] * 2,
               out_shape=[SDS((s, d), BF16)] * 2, sem=("parallel",))(dx3, z, pp, after)


def norm_bwd(name, xin, gain, dy, dres):
    s, d = xin.shape
    t = _tile(s, ROW_TILE_WIDE)

    def body(x_ref, g_ref, dy_ref, r_ref, dx_ref, dg_ref):
        xv = x_ref[...]
        r = _rstd(xv)
        xh = xv * r
        dyv = dy_ref[...]
        dyg = dyv * g_ref[...]
        c = jnp.mean(dyg * xh, axis=-1, keepdims=True)
        dx_ref[...] = r * (dyg - xh * c) + r_ref[...]
        part = jnp.sum(dyv * xh, axis=0, keepdims=True)

        @pl.when(pl.program_id(0) == 0)
        def _():
            dg_ref[...] = part

        @pl.when(pl.program_id(0) > 0)
        def _():
            dg_ref[...] += part

    return _pc(body, name=name, grid=(s // t,), in_specs=[_rows(t, d), _fixed((1, d)), _rows(t, d), _rows(t, d)],
               out_specs=[_rows(t, d), _fixed((1, d))], out_shape=[SDS((s, d), F32), SDS((1, d), F32)],
               sem=("arbitrary",))(xin, gain, dy, dres)


def _lane(shape):
    return lax.broadcasted_iota(jnp.int32, shape, 1)


def _swap_halves(x):
    lo = (_lane(x.shape) % 64) < 32
    return jnp.where(lo, pltpu.roll(x, 96, 1), pltpu.roll(x, 32, 1))


def rope_tables(name, pos_col, inv_freq):
    s = pos_col.shape[0]
    t = _tile(s, ROW_TILE_WIDE)

    def body(p_ref, f_ref, c_ref, s_ref):
        ang = p_ref[...].astype(F32) * f_ref[...]
        lo = (_lane(ang.shape) % 64) < 32
        c_ref[...] = jnp.cos(ang)
        sn = jnp.sin(ang)
        s_ref[...] = jnp.where(lo, -sn, sn)

    return _pc(body, name=name, grid=(s // t,), in_specs=[_rows(t, 1), _fixed((1, LANES))],
               out_specs=[_rows(t, LANES)] * 2, out_shape=[SDS((s, LANES), F32)] * 2, sem=("parallel",))(pos_col, inv_freq)


def _pad_heads(chunk, lo_mask):
    zero = jnp.zeros_like(chunk)
    return jnp.where(lo_mask, chunk, zero), jnp.where(lo_mask, pltpu.roll(chunk, 64, 1), zero)


def rope_qkv(name, proj, cos, sin):
    s = proj.shape[0]
    t = _tile(s, ROW_TILE_WIDE)

    def body(q_ref, kv_ref, c_ref, s_ref, qp_ref, kp_ref, vp_ref):
        cs, sn = c_ref[...], s_ref[...]
        lo_mask = _lane(cs.shape) < 64

        def rot(x):
            return x * cs + _swap_halves(x) * sn

        for j in range(ATTN_WIDTH // LANES):
            a, b = _pad_heads(rot(q_ref[:, j * LANES:(j + 1) * LANES]), lo_mask)
            qp_ref[:, (2 * j) * LANES:(2 * j + 1) * LANES] = a.astype(BF16)
            qp_ref[:, (2 * j + 1) * LANES:(2 * j + 2) * LANES] = b.astype(BF16)
        for j in range(KV_WIDTH // LANES):
            a, b = _pad_heads(rot(kv_ref[:, j * LANES:(j + 1) * LANES]), lo_mask)
            kp_ref[:, (2 * j) * LANES:(2 * j + 1) * LANES] = a.astype(BF16)
            kp_ref[:, (2 * j + 1) * LANES:(2 * j + 2) * LANES] = b.astype(BF16)
            a, b = _pad_heads(kv_ref[:, KV_WIDTH + j * LANES:KV_WIDTH + (j + 1) * LANES], lo_mask)
            vp_ref[:, (2 * j) * LANES:(2 * j + 1) * LANES] = a.astype(BF16)
            vp_ref[:, (2 * j + 1) * LANES:(2 * j + 2) * LANES] = b.astype(BF16)

    return _pc(body, name=name, grid=(s // t,),
               in_specs=[_rows(t, ATTN_WIDTH, 0), _rows(t, 2 * KV_WIDTH, 2), _rows(t, LANES), _rows(t, LANES)],
               out_specs=[_rows(t, N_Q_HEADS * LANES), _rows(t, N_KV_HEADS * LANES), _rows(t, N_KV_HEADS * LANES)],
               out_shape=[SDS((s, N_Q_HEADS * LANES), BF16), SDS((s, N_KV_HEADS * LANES), BF16),
                          SDS((s, N_KV_HEADS * LANES), BF16)],
               sem=("parallel",))(proj, proj, cos, sin)


def _attn_mask(n):
    L = WINDOW
    qi = lax.broadcasted_iota(jnp.int32, (L, 2 * L), 0) + L
    ki = lax.broadcasted_iota(jnp.int32, (L, 2 * L), 1)
    rel = qi - ki
    return (rel >= 0) & (rel < WINDOW) & ((n > 0) | (ki >= L))


def _attn_scores(qh, kk, valid):
    sc = lax.dot_general(qh, kk, _NT, preferred_element_type=F32) * 0.125
    return jnp.where(valid, sc, MASK_VALUE)


def _attn_softmax(sc, sink):
    m = jnp.maximum(jnp.max(sc, axis=-1, keepdims=True), sink)
    e = jnp.exp(sc - m)
    es = jnp.exp(sink - m)
    den = jnp.sum(e, axis=-1, keepdims=True) + es
    return e / den, es / den


def _attn_specs(s):
    L = WINDOW
    cur = lambda n: (n, 0)
    prev = lambda n: (jnp.maximum(n - 1, 0), 0)
    kvw = N_KV_HEADS * LANES
    return [pl.BlockSpec((L, N_Q_HEADS * LANES), cur), pl.BlockSpec((L, kvw), cur), pl.BlockSpec((L, kvw), prev),
            pl.BlockSpec((L, kvw), cur), pl.BlockSpec((L, kvw), prev), pl.BlockSpec(memory_space=pltpu.SMEM)]


def attn_fwd(name, qp, kp, vp, sinks):
    s = qp.shape[0]
    L = WINDOW

    def body(q_ref, kc_ref, kp_ref, vc_ref, vp_ref, sk_ref, o_ref):
        valid = _attn_mask(pl.program_id(0))
        kks, vvs = [], []
        for kvh in range(N_KV_HEADS):
            cols = slice(kvh * LANES, (kvh + 1) * LANES)
            kks.append(jnp.concatenate([kp_ref[:, cols], kc_ref[:, cols]], axis=0))
            vvs.append(jnp.concatenate([vp_ref[:, cols], vc_ref[:, cols]], axis=0))
        scs = [_attn_scores(q_ref[:, h * LANES:(h + 1) * LANES], kks[h // Q_PER_KV], valid) for h in range(N_Q_HEADS)]
        pbs = [_attn_softmax(scs[h], sk_ref[h])[0].astype(BF16) for h in range(N_Q_HEADS)]
        outs = [jnp.dot(pbs[h], vvs[h // Q_PER_KV], preferred_element_type=F32) for h in range(N_Q_HEADS)]
        for j in range(ATTN_WIDTH // LANES):
            o_ref[:, j * LANES:(j + 1) * LANES] = outs[2 * j] + pltpu.roll(outs[2 * j + 1], 64, 1)

    return _pc(body, name=name, grid=(s // L,), in_specs=_attn_specs(s),
               out_specs=pl.BlockSpec((L, ATTN_WIDTH), lambda n: (n, 0)),
               out_shape=SDS((s, ATTN_WIDTH), F32), sem=("parallel",))(qp, kp, kp, vp, vp, sinks)


def attn_bwd(name, qp, kp, vp, sinks, dattn):
    s = qp.shape[0]
    L = WINDOW
    kvw = N_KV_HEADS * LANES

    def body(q_ref, kc_ref, kp_ref, vc_ref, vp_ref, sk_ref, do_ref, dq_ref, dkc_ref, dkp_ref, dvc_ref, dvp_ref, ds_ref):
        n = pl.program_id(0)
        valid = _attn_mask(n)
        lo_mask = _lane((L, LANES)) < 64
        lane1 = _lane((1, LANES))
        dsink = jnp.zeros((1, LANES), F32)
        heads = range(N_Q_HEADS)
        kks, vvs = [], []
        for kvh in range(N_KV_HEADS):
            cols = slice(kvh * LANES, (kvh + 1) * LANES)
            kks.append(jnp.concatenate([kp_ref[:, cols], kc_ref[:, cols]], axis=0))
            vvs.append(jnp.concatenate([vp_ref[:, cols], vc_ref[:, cols]], axis=0))
        qs, dos, scs, dps = [], [], [], []
        for h in heads:
            qs.append(q_ref[:, h * LANES:(h + 1) * LANES])
            chunk = do_ref[:, (h // 2) * LANES:(h // 2 + 1) * LANES]
            if h % 2:
                chunk = pltpu.roll(chunk, 64, 1)
            dos.append(jnp.where(lo_mask, chunk, 0.0).astype(BF16))
            scs.append(_attn_scores(qs[h], kks[h // Q_PER_KV], valid))
            dps.append(lax.dot_general(dos[h], vvs[h // Q_PER_KV], _NT, preferred_element_type=F32))
        pbs, dsbs = [], []
        for h in heads:
            p, ps = _attn_softmax(scs[h], sk_ref[h])
            delta = jnp.sum(p * dps[h], axis=-1, keepdims=True)
            dsbs.append(((p * (dps[h] - delta)) * 0.125).astype(BF16))
            pbs.append(p.astype(BF16))
            dsink = dsink + jnp.where(lane1 == h, -jnp.sum(ps * delta, axis=0, keepdims=True), 0.0)
        for kvh in range(N_KV_HEADS):
            cols = slice(kvh * LANES, (kvh + 1) * LANES)
            dkk = jnp.zeros((2 * L, LANES), F32)
            dvv = jnp.zeros((2 * L, LANES), F32)
            for h in range(kvh * Q_PER_KV, (kvh + 1) * Q_PER_KV):
                dq_ref[:, h * LANES:(h + 1) * LANES] = jnp.dot(dsbs[h], kks[kvh], preferred_element_type=F32)
                dkk = dkk + lax.dot_general(dsbs[h], qs[h], _TN, preferred_element_type=F32)
                dvv = dvv + lax.dot_general(pbs[h], dos[h], _TN, preferred_element_type=F32)
            dkp_ref[:, cols] = dkk[:L]
            dkc_ref[:, cols] = dkk[L:]
            dvp_ref[:, cols] = dvv[:L]
            dvc_ref[:, cols] = dvv[L:]

        @pl.when(n == 0)
        def _():
            ds_ref[...] = dsink

        @pl.when(n > 0)
        def _():
            ds_ref[...] += dsink

    blk = lambda w: pl.BlockSpec((L, w), lambda n: (n, 0))
    return _pc(body, name=name, grid=(s // L,), in_specs=_attn_specs(s) + [blk(ATTN_WIDTH)],
               out_specs=[blk(N_Q_HEADS * LANES), blk(kvw), blk(kvw), blk(kvw), blk(kvw), _fixed((1, LANES))],
               out_shape=[SDS((s, N_Q_HEADS * LANES), F32)] + [SDS((s, kvw), F32)] * 4 + [SDS((1, LANES), F32)],
               sem=("arbitrary",))(qp, kp, kp, vp, vp, sinks, dattn)


def rope_bwd(name, dqp, dkc, dkp, dvc, dvp, cos, sin):
    s = dqp.shape[0]
    L = WINDOW
    nb = s // L
    kvw = N_KV_HEADS * LANES

    def body(dq_ref, dkc_ref, dkp_ref, dvc_ref, dvp_ref, c_ref, s_ref, o_ref):
        cs, sn = c_ref[...], s_ref[...]
        more = (pl.program_id(0) < nb - 1).astype(F32)

        def unrot(x):
            return x * cs - _swap_halves(x) * sn

        def compact(ref, j, nxt=None):
            a = ref[:, (2 * j) * LANES:(2 * j + 1) * LANES]
            b = ref[:, (2 * j + 1) * LANES:(2 * j + 2) * LANES]
            if nxt is not None:
                a = a + more * nxt[:, (2 * j) * LANES:(2 * j + 1) * LANES]
                b = b + more * nxt[:, (2 * j + 1) * LANES:(2 * j + 2) * LANES]
            return a + pltpu.roll(b, 64, 1)

        for j in range(ATTN_WIDTH // LANES):
            o_ref[:, j * LANES:(j + 1) * LANES] = unrot(compact(dq_ref, j)).astype(BF16)
        for j in range(KV_WIDTH // LANES):
            o_ref[:, (COL_K + j) * LANES:(COL_K + j + 1) * LANES] = unrot(compact(dkc_ref, j, dkp_ref)).astype(BF16)
            o_ref[:, (COL_V + j) * LANES:(COL_V + j + 1) * LANES] = compact(dvc_ref, j, dvp_ref).astype(BF16)

    cur = lambda n: (n, 0)
    nxt = lambda n: (jnp.minimum(n + 1, nb - 1), 0)
    return _pc(body, name=name, grid=(nb,),
               in_specs=[pl.BlockSpec((L, N_Q_HEADS * LANES), cur), pl.BlockSpec((L, kvw), cur), pl.BlockSpec((L, kvw), nxt),
                         pl.BlockSpec((L, kvw), cur), pl.BlockSpec((L, kvw), nxt), pl.BlockSpec((L, LANES), cur),
                         pl.BlockSpec((L, LANES), cur)],
               out_specs=pl.BlockSpec((L, COL_HQ * LANES), cur), out_shape=SDS((s, COL_HQ * LANES), BF16),
               sem=("parallel",))(dqp, dkc, dkp, dvc, dvp, cos, sin)


def _split3(x):
    a = x.astype(BF16)
    r = x - a.astype(F32)
    b = r.astype(BF16)
    c = (r - b.astype(F32)).astype(BF16)
    return a, b, c


def _chunk_sum(x, upper):
    t = x.shape[0]
    ri = lax.broadcasted_iota(jnp.int32, (t, t), 0)
    ci = lax.broadcasted_iota(jnp.int32, (t, t), 1)
    same = (ri // HGRN_CHUNK) == (ci // HGRN_CHUNK)
    tri = (ci >= ri) if upper else (ci <= ri)
    m = jnp.where(same & tri, 1.0, 0.0).astype(BF16)
    out = None
    for part in _split3(x):
        y = jnp.dot(m, part, preferred_element_type=F32)
        out = y if out is None else out + y
    return out


def _lower_bound(l_ref, layer):
    lv = l_ref[...]
    e = jnp.exp(lv - jnp.max(lv, axis=0, keepdims=True))
    sm = e / jnp.sum(e, axis=0, keepdims=True)
    s0 = sm[0:1]
    return (s0 - s0) if layer == 0 else ((s0 + sm[1:2]) - s0)


def _hgrn_gates(hq_ref, hf_ref, lb):
    z = hf_ref[...]
    sg = _sigmoid(z)
    f = lb + (1.0 - lb) * sg
    kin = (1.0 - lb) * _sigmoid(-z)
    hq = hq_ref[...]
    sq = _sigmoid(hq)
    return sg, f, kin, hq, sq


def _shift_down(x, d):
    return x if d == 0 else pltpu.roll(x, d, 0)


def _shift_up(x, d):
    return x if d == 0 else pltpu.roll(x, x.shape[0] - d, 0)


CHUNKS_PER_BLOCK = LANES // HGRN_CHUNK


def _chunk_iotas():
    shape = (HGRN_CHUNK, LANES)
    return lax.broadcasted_iota(jnp.int32, shape, 0), lax.broadcasted_iota(jnp.int32, shape, 1)


def _chunk_rows(block, chunk):
    start = block * LANES + chunk * HGRN_CHUNK
    return slice(start, start + HGRN_CHUNK)


HGRN_HEADS_PER_STEP = 4
HGRN_STEP_WIDTH = HGRN_HEADS_PER_STEP * LANES


def _hgrn_specs(t, rev, nt):
    row = (lambda h, i: nt - 1 - i) if rev else (lambda h, i: i)
    col = lambda base: pl.BlockSpec((t, HGRN_STEP_WIDTH),
                                    lambda h, i, base=base: (row(h, i), base // HGRN_HEADS_PER_STEP + h))
    return col, row


def _head_views(refs, hh):
    return [r.at[:, pl.ds(hh * LANES, LANES)] for r in refs]


def hgrn_fwd(name, proj, lb_logits, layer):
    s = proj.shape[0]
    t = _tile(s, ROW_TILE)
    nt = s // t
    nc = t // HGRN_CHUNK
    col, row = _hgrn_specs(t, False, nt)

    def body(hq_ref, hf_ref, hi_ref, l_ref, o_ref, st_ref, state):
        @pl.when(pl.program_id(1) == 0)
        def _():
            state[...] = jnp.zeros_like(state)

        for hh in range(HGRN_HEADS_PER_STEP):
            head(*_head_views((hq_ref, hf_ref, hi_ref, l_ref, o_ref), hh), st_ref.at[:, hh], state.at[hh])

    def head(hq_ref, hf_ref, hi_ref, l_ref, o_ref, st_ref, state):
        lb = _lower_bound(l_ref, layer)
        sg, f, kin, hq, sq = _hgrn_gates(hq_ref, hf_ref, lb)
        q = hq * sq
        vb = hi_ref[...].astype(BF16)
        b = _chunk_sum(jnp.log(f), False)
        qe = (q * jnp.exp(b)).astype(BF16)
        trow, lane = _chunk_iotas()
        chunks = [(j, cc) for j in range(t // LANES) for cc in range(CHUNKS_PER_BLOCK)]
        decay, update = [], []
        for j, cc in chunks:
            rs = _chunk_rows(j, cc)
            bc = b[rs]
            bl = bc[HGRN_CHUNK - 1:HGRN_CHUNK, :]
            ke = (kin[rs] * jnp.exp(bl - bc)).astype(BF16)
            decay.append(jnp.exp(bl))
            update.append(lax.dot_general(vb[rs], ke, _TN, preferred_element_type=F32))
        st = state[...]
        for c in range(nc):
            st_ref[c] = st
            st = st * decay[c] + update[c]
        state[...] = st
        o_inter = [lax.dot_general(qe[c * HGRN_CHUNK:(c + 1) * HGRN_CHUNK], st_ref[c].astype(BF16), _NT,
                                   preferred_element_type=F32) for c in range(nc)]
        for j in range(t // LANES):
            blk = slice(j * LANES, (j + 1) * LANES)
            rows = []
            for cc in range(CHUNKS_PER_BLOCK):
                rs = _chunk_rows(j, cc)
                bc, qc, kc = b[rs], q[rs], kin[rs]
                here = trow + cc * HGRN_CHUNK
                am = jnp.where(lane == here, jnp.sum(qc * kc, axis=-1, keepdims=True), 0.0)
                for d in range(1, HGRN_CHUNK):
                    e = jnp.exp(jnp.where(trow >= d, bc - _shift_down(bc, d), MASK_VALUE))
                    a = jnp.sum((qc * _shift_down(kc, d)) * e, axis=-1, keepdims=True)
                    am = jnp.where(lane == here - d, a, am)
                rows.append(am)
            o_intra = jnp.dot(jnp.concatenate(rows, axis=0).astype(BF16), vb[blk], preferred_element_type=F32)
            for cc in range(CHUNKS_PER_BLOCK):
                rs = _chunk_rows(j, cc)
                o_ref[rs, :] = o_intra[cc * HGRN_CHUNK:(cc + 1) * HGRN_CHUNK] + o_inter[j * CHUNKS_PER_BLOCK + cc]

    hp = HGRN_HEADS_PER_STEP
    return _pc(body, name=name, grid=(HGRN_HEADS // hp, nt),
               in_specs=[col(COL_HQ), col(COL_HF), col(COL_HI), pl.BlockSpec((2, HGRN_STEP_WIDTH), lambda h, i: (0, h))],
               out_specs=[pl.BlockSpec((t, HGRN_STEP_WIDTH), lambda h, i: (i, h)),
                          pl.BlockSpec((nc, hp, LANES, LANES), lambda h, i: (i, h, 0, 0))],
               out_shape=[SDS((s, HGRN_WIDTH), F32), SDS((s // HGRN_CHUNK, HGRN_HEADS, LANES, LANES), F32)],
               scratch=[pltpu.VMEM((hp, LANES, LANES), F32)],
               sem=("parallel", "arbitrary"))(proj, proj, proj, lb_logits)


def hgrn_bwd(name, proj, lb_logits, layer, states, do):
    s = proj.shape[0]
    t = _tile(s, ROW_TILE)
    nt = s // t
    nc = t // HGRN_CHUNK
    col, row = _hgrn_specs(t, True, nt)

    def body(hq_ref, hf_ref, hi_ref, l_ref, st_ref, do_ref, dhq_ref, dhf_ref, dhi_ref, dlb_ref, dstate):
        @pl.when(pl.program_id(1) == 0)
        def _():
            dstate[...] = jnp.zeros_like(dstate)

        for hh in range(HGRN_HEADS_PER_STEP):
            hq_v, hf_v, hi_v, l_v, do_v, dhq_v, dhf_v, dhi_v, dlb_v = _head_views(
                (hq_ref, hf_ref, hi_ref, l_ref, do_ref, dhq_ref, dhf_ref, dhi_ref, dlb_ref), hh)
            head(hq_v, hf_v, hi_v, l_v, st_ref.at[:, hh], do_v, dhq_v, dhf_v, dhi_v, dlb_v, dstate.at[hh])

    def head(hq_ref, hf_ref, hi_ref, l_ref, st_ref, do_ref, dhq_ref, dhf_ref, dhi_ref, dlb_ref, dstate):
        first = pl.program_id(1) == 0
        lb = _lower_bound(l_ref, layer)
        sg, f, kin, hq, sq = _hgrn_gates(hq_ref, hf_ref, lb)
        q = hq * sq
        vb = hi_ref[...].astype(BF16)
        b = _chunk_sum(jnp.log(f), False)
        dob = do_ref[...].astype(BF16)
        eb = jnp.exp(b)
        qe = q * eb
        qeb = qe.astype(BF16)
        trow, lane = _chunk_iotas()
        last_row = trow == HGRN_CHUNK - 1

        decay, update = [None] * nc, [None] * nc
        for c in range(nc):
            rs = slice(c * HGRN_CHUNK, (c + 1) * HGRN_CHUNK)
            decay[c] = jnp.exp(b[(c + 1) * HGRN_CHUNK - 1:(c + 1) * HGRN_CHUNK, :])
            update[c] = lax.dot_general(dob[rs], qeb[rs], _TN, preferred_element_type=F32)
        dn_in = [None] * nc
        dn = dstate[...]
        for c in reversed(range(nc)):
            dn_in[c] = dn
            dn = dn * decay[c] + update[c]
        dstate[...] = dn

        dq_c, dk_c, dv_c, dbl_c = [None] * nc, [None] * nc, [None] * nc, [None] * nc
        for c in range(nc):
            rs = slice(c * HGRN_CHUNK, (c + 1) * HGRN_CHUNK)
            bc = b[rs]
            ekb = jnp.exp(bc[HGRN_CHUNK - 1:HGRN_CHUNK, :] - bc)
            ke = kin[rs] * ekb
            st = st_ref[c]
            dnb = dn_in[c].astype(BF16)
            dke = jnp.dot(vb[rs], dnb, preferred_element_type=F32)
            dq_c[c] = jnp.dot(dob[rs], st.astype(BF16), preferred_element_type=F32) * eb[rs]
            dk_c[c] = dke * ekb
            dv_c[c] = lax.dot_general(ke.astype(BF16), dnb, _NT, preferred_element_type=F32)
            dbl_c[c] = jnp.sum(dn_in[c] * st, axis=0, keepdims=True) * decay[c] + jnp.sum(dke * ke, axis=0, keepdims=True)

        db_c = [None] * nc
        for j in range(t // LANES):
            blk = slice(j * LANES, (j + 1) * LANES)
            damat = lax.dot_general(dob[blk], vb[blk], _NT, preferred_element_type=F32)
            rows = [None] * CHUNKS_PER_BLOCK
            for cc in range(CHUNKS_PER_BLOCK):
                c = j * CHUNKS_PER_BLOCK + cc
                rs = _chunk_rows(j, cc)
                bc, qc, kc = b[rs], q[rs], kin[rs]
                dam = damat[cc * HGRN_CHUNK:(cc + 1) * HGRN_CHUNK]
                here = trow + cc * HGRN_CHUNK
                on = lane == here
                da = jnp.sum(jnp.where(on, dam, 0.0), axis=-1, keepdims=True)
                am = jnp.where(on, jnp.sum(qc * kc, axis=-1, keepdims=True), 0.0)
                dq = dq_c[c] + da * kc
                dk = dk_c[c] + da * qc
                for d in range(1, HGRN_CHUNK):
                    on = lane == here - d
                    e = jnp.exp(jnp.where(trow >= d, bc - _shift_down(bc, d), MASK_VALUE))
                    kse = _shift_down(kc, d) * e
                    am = jnp.where(on, jnp.sum(qc * kse, axis=-1, keepdims=True), am)
                    da = jnp.sum(jnp.where(on, dam, 0.0), axis=-1, keepdims=True)
                    dq = dq + da * kse
                    dk = dk + _shift_up(da * (qc * e), d)
                rows[cc] = am
                dq_c[c], dk_c[c] = dq, dk
                db_c[c] = (qc * dq - kc * dk) + jnp.where(last_row, dbl_c[c], 0.0)
            dv_blk = lax.dot_general(jnp.concatenate(rows, axis=0).astype(BF16), dob[blk], _TN, preferred_element_type=F32)
            for cc in range(CHUNKS_PER_BLOCK):
                c = j * CHUNKS_PER_BLOCK + cc
                dv_c[c] = dv_c[c] + dv_blk[cc * HGRN_CHUNK:(cc + 1) * HGRN_CHUNK]
        dq = jnp.concatenate(dq_c, axis=0)
        dk = jnp.concatenate(dk_c, axis=0)
        dv = jnp.concatenate(dv_c, axis=0)
        db = jnp.concatenate(db_c, axis=0)
        dg = _chunk_sum(db, True)
        dhq_ref[...] = (dq * (sq * (1.0 + hq * (1.0 - sq)))).astype(BF16)
        dhi_ref[...] = dv.astype(BF16)
        dfk = dg / f - dk
        dhf_ref[...] = ((dfk * (1.0 - lb)) * (sg * (1.0 - sg))).astype(BF16)
        part = jnp.sum(dfk * (1.0 - sg), axis=0, keepdims=True)

        @pl.when(first)
        def _():
            dlb_ref[...] = part

        @pl.when(jnp.logical_not(first))
        def _():
            dlb_ref[...] += part

    hp = HGRN_HEADS_PER_STEP
    out_col = pl.BlockSpec((t, HGRN_STEP_WIDTH), lambda h, i: (nt - 1 - i, h))
    return _pc(body, name=name, grid=(HGRN_HEADS // hp, nt),
               in_specs=[col(COL_HQ), col(COL_HF), col(COL_HI), pl.BlockSpec((2, HGRN_STEP_WIDTH), lambda h, i: (0, h)),
                         pl.BlockSpec((nc, hp, LANES, LANES), lambda h, i: (nt - 1 - i, h, 0, 0)), out_col],
               out_specs=[out_col, out_col, out_col, pl.BlockSpec((1, HGRN_STEP_WIDTH), lambda h, i: (0, h))],
               out_shape=[SDS((s, HGRN_WIDTH), BF16)] * 3 + [SDS((1, HGRN_WIDTH), F32)],
               scratch=[pltpu.VMEM((hp, LANES, LANES), F32)],
               sem=("parallel", "arbitrary"))(proj, proj, proj, lb_logits, states, do)


def mix_out_fwd(name, attn, o, proj, g_attn, g_hgrn):
    s = attn.shape[0]
    t = _tile(s, ROW_TILE_WIDE)
    half = HGRN_WIDTH // 2

    def body(a_ref, o_ref, hg0_ref, hg1_ref, ga_ref, gh_ref, c_ref):
        av = a_ref[...]
        c_ref[:, :ATTN_WIDTH] = ((av * _rstd(av)) * ga_ref[...]).astype(BF16)
        for j in range(HGRN_HEADS):
            cols = slice(j * LANES, (j + 1) * LANES)
            ov = o_ref[:, cols]
            hg_ref, hcols = (hg0_ref, cols) if j < 4 else (hg1_ref, slice((j - 4) * LANES, (j - 3) * LANES))
            hg = hg_ref[:, hcols]
            on = (ov * _rstd(ov)) * gh_ref[:, cols]
            c_ref[:, ATTN_WIDTH + j * LANES:ATTN_WIDTH + (j + 1) * LANES] = (on * (hg * _sigmoid(hg))).astype(BF16)

    return _pc(body, name=name, grid=(s // t,),
               in_specs=[_rows(t, ATTN_WIDTH), _rows(t, HGRN_WIDTH), _rows(t, half, COL_HG // 4), _rows(t, half, COL_HG // 4 + 1),
                         _fixed((1, ATTN_WIDTH)), _fixed((1, HGRN_WIDTH))],
               out_specs=_rows(t, D_MODEL), out_shape=SDS((s, D_MODEL), BF16), sem=("parallel",))(attn, o, proj, proj, g_attn, g_hgrn)


def mix_out_bwd(name, dcat, attn, o, proj, g_attn, g_hgrn):
    s = attn.shape[0]
    t = _tile(s, ROW_TILE_WIDE)
    half = HGRN_WIDTH // 2

    def body(dc_ref, a_ref, o_ref, hg0_ref, hg1_ref, ga_ref, gh_ref, da_ref, do_ref, dhg_ref, dga_ref, dgh_ref, pa_s, ph_s):
        av = a_ref[...]
        r = _rstd(av)
        xh = av * r
        dyv = dc_ref[:, :ATTN_WIDTH]
        dyg = dyv * ga_ref[...]
        da_ref[...] = r * (dyg - xh * jnp.mean(dyg * xh, axis=-1, keepdims=True))
        pa_s[...] = jnp.sum(dyv * xh, axis=0, keepdims=True)
        for j in range(HGRN_HEADS):
            cols = slice(j * LANES, (j + 1) * LANES)
            ov = o_ref[:, cols]
            hg_ref, hcols = (hg0_ref, cols) if j < 4 else (hg1_ref, slice((j - 4) * LANES, (j - 3) * LANES))
            hg = hg_ref[:, hcols]
            sg = _sigmoid(hg)
            r = _rstd(ov)
            xh = ov * r
            gain = gh_ref[:, cols]
            dh = dc_ref[:, ATTN_WIDTH + j * LANES:ATTN_WIDTH + (j + 1) * LANES]
            dhg_ref[:, cols] = ((dh * (xh * gain)) * (sg * (1.0 + hg * (1.0 - sg)))).astype(BF16)
            dyv = dh * (hg * sg)
            dyg = dyv * gain
            do_ref[:, cols] = r * (dyg - xh * jnp.mean(dyg * xh, axis=-1, keepdims=True))
            ph_s[:, cols] = jnp.sum(dyv * xh, axis=0, keepdims=True)

        @pl.when(pl.program_id(0) == 0)
        def _():
            dga_ref[...] = pa_s[...]
            dgh_ref[...] = ph_s[...]

        @pl.when(pl.program_id(0) > 0)
        def _():
            dga_ref[...] += pa_s[...]
            dgh_ref[...] += ph_s[...]

    return _pc(body, name=name, grid=(s // t,),
               in_specs=[_rows(t, D_MODEL), _rows(t, ATTN_WIDTH), _rows(t, HGRN_WIDTH), _rows(t, half, COL_HG // 4),
                         _rows(t, half, COL_HG // 4 + 1), _fixed((1, ATTN_WIDTH)), _fixed((1, HGRN_WIDTH))],
               out_specs=[_rows(t, ATTN_WIDTH), _rows(t, HGRN_WIDTH), _rows(t, HGRN_WIDTH), _fixed((1, ATTN_WIDTH)),
                          _fixed((1, HGRN_WIDTH))],
               out_shape=[SDS((s, ATTN_WIDTH), F32), SDS((s, HGRN_WIDTH), F32), SDS((s, HGRN_WIDTH), BF16),
                          SDS((1, ATTN_WIDTH), F32), SDS((1, HGRN_WIDTH), F32)],
               scratch=[pltpu.VMEM((1, ATTN_WIDTH), F32), pltpu.VMEM((1, HGRN_WIDTH), F32)],
               sem=("arbitrary",))(dcat, attn, o, proj, proj, g_attn, g_hgrn)


BIG = (("w_in", 2048, 1408, 0), ("w_out", 512, 2048, 1), ("w_ffn_gate", 2048, 1408, 0), ("w_ffn_up", 2048, 1408, 0),
       ("w_ffn_down", 1408, 2048, 1), ("w_ple_gate", 512, 2048, 1), ("w_ple_proj", 256, 512, 0))
BIG_BY_NAME = {spec[0]: spec for spec in BIG}
HBM_SPEC = pl.BlockSpec(memory_space=pltpu.HBM)
SEM_SPEC = pl.BlockSpec(memory_space=pltpu.SEMAPHORE)
TOKEN_SHAPE = (8, LANES)


def _split_call(body, *, name, in_specs, out_specs, out_shape, aliases):
    return pl.pallas_call(body, name=name, in_specs=in_specs, out_specs=out_specs, out_shape=out_shape,
                          input_output_aliases=aliases,
                          compiler_params=pltpu.CompilerParams(has_side_effects=pltpu.SideEffectType.DATAFLOW_SIDE_EFFECTING))


def _in_hbm(arrays):
    return [pltpu.with_memory_space_constraint(a, pltpu.HBM) for a in arrays]


GROUP_STEPS = 8


def _step_rows(rows):
    assert rows % (GROUP_STEPS * 16) == 0
    return rows // GROUP_STEPS


def cast_to_slots(name, place, ws, layer, after):
    nt = len(ws)

    def body(place_ref, *refs):
        for t in range(nt):
            refs[nt + 1 + t][...] = refs[t][...].astype(BF16)

    in_specs, out_specs = [], []
    for w in ws:
        block = (None, _step_rows(w.shape[1]), w.shape[2])
        in_specs.append(pl.BlockSpec(block, lambda i, pr: (layer, i, 0)))
        out_specs.append(pl.BlockSpec(block, lambda i, pr: (pr[1], i, 0)))
    gs = pltpu.PrefetchScalarGridSpec(num_scalar_prefetch=1, grid=(GROUP_STEPS,), in_specs=in_specs + [ANY_SPEC],
                                      out_specs=out_specs)
    return _pc(body, name=name, grid_spec=gs, in_specs=None, out_specs=None,
               out_shape=[SDS((N_CHIPS,) + w.shape[1:], BF16) for w in ws], sem=("parallel",))(place, *ws, after)


def _place():
    x, y, c = lax.axis_index("x"), lax.axis_index("y"), lax.axis_index("c")
    chips = [(1 - x, y), (x, 1 - y), (1 - x, 1 - y)]
    return x, y, c, chips


def _half(ref, axis, c, rows, cols):
    if axis == 0:
        return ref.at[pl.ds(pl.multiple_of(c * (rows // 2), 16), rows // 2), :]
    return ref.at[:, pl.ds(pl.multiple_of(c * (cols // 2), LANES), cols // 2)]


def _gather_copies(specs, bufs, send, recv):
    x, y, c, chips = _place()
    cps = []
    for t, (_, rows, cols, axis) in enumerate(specs):
        mine = _half(bufs[t].at[2 * x + y], axis, c, rows, cols)
        for k, (cx, cy) in enumerate(chips):
            cps.append(pltpu.make_async_remote_copy(src_ref=mine, dst_ref=mine, send_sem=send.at[3 * t + k],
                                                    recv_sem=recv.at[3 * t + k], device_id=(cx, cy, c), device_id_type=MESH))
    return cps


def gather_start(name, specs, bufs, after):
    nt = len(bufs)
    n = 3 * nt

    def body(*refs):
        send, recv, token = refs[nt + 1], refs[nt + 2], refs[-1]
        for cp in _gather_copies(specs, refs[:nt], send, recv):
            cp.start()
        token[...] = jnp.zeros(TOKEN_SHAPE, F32)

    out = _split_call(
        body, name=name, in_specs=[HBM_SPEC] * nt + [ANY_SPEC],
        out_specs=(SEM_SPEC, SEM_SPEC) + (HBM_SPEC,) * nt + (pl.BlockSpec(memory_space=pltpu.VMEM),),
        out_shape=(pltpu.SemaphoreType.DMA((n,)), pltpu.SemaphoreType.DMA((n,)))
        + tuple(pltpu.HBM(b.shape, b.dtype) for b in bufs) + (SDS(TOKEN_SHAPE, F32),),
        aliases={t: 2 + t for t in range(nt)})(*_in_hbm(bufs), after)
    return out[0], out[1], list(out[2:2 + nt]), out[-1]


def gather_wait(name, specs, send, recv, bufs, after):
    nt = len(bufs)

    def body(*refs):
        for cp in _gather_copies(specs, refs[:nt], refs[nt], refs[nt + 1]):
            cp.wait_send()
            cp.wait_recv()

    out = _split_call(
        body, name=name, in_specs=[HBM_SPEC] * nt + [SEM_SPEC, SEM_SPEC, pl.BlockSpec(memory_space=pl.ANY)],
        out_specs=(HBM_SPEC,) * nt, out_shape=tuple(pltpu.HBM(b.shape, b.dtype) for b in bufs),
        aliases={t: t for t in range(nt)})(*bufs, send, recv, after)
    return list(out)


def gather_pass(name, specs, bufs):
    nt = len(bufs)

    def body(*refs):
        ins, outs = refs[:nt], refs[nt:2 * nt]
        send, recv = refs[2 * nt:]
        x, y, c, chips = _place()
        cps = []
        for t, (_, rows, cols, axis) in enumerate(specs):
            for k, (cx, cy) in enumerate(chips):
                cp = pltpu.make_async_remote_copy(
                    src_ref=_half(ins[t].at[2 * cx + cy], axis, c, rows, cols),
                    dst_ref=_half(outs[t].at[2 * cx + cy], axis, c, rows, cols),
                    send_sem=send.at[3 * t + k], recv_sem=recv.at[3 * t + k], device_id=(x, y, 1 - c), device_id_type=MESH)
                cp.start()
                cps.append(cp)
        for t, (_, rows, cols, axis) in enumerate(specs):
            for k, (cx, cy) in enumerate(chips):
                theirs = _half(outs[t].at[2 * cx + cy], axis, 1 - c, rows, cols)
                pltpu.make_async_remote_copy(src_ref=theirs, dst_ref=theirs, send_sem=send.at[3 * t + k],
                                             recv_sem=recv.at[3 * t + k], device_id=(x, y, 1 - c), device_id_type=MESH).wait_recv()
        for cp in cps:
            cp.wait_send()

    return _pc(body, name=name, in_specs=[HBM_SPEC] * nt, out_specs=[HBM_SPEC] * nt,
               out_shape=[SDS(b.shape, b.dtype) for b in bufs], scratch=[pltpu.SemaphoreType.DMA((3 * nt,))] * 2,
               input_output_aliases={t: t for t in range(nt)})(*bufs)


def reduce_to_sibling(name, grads):
    nt = len(grads)

    def body(*refs):
        srcs, dsts = refs[:nt], refs[nt:2 * nt]
        send, recv = refs[2 * nt:]
        x, y, c, _ = _place()
        cps = []
        for t in range(nt):
            cp = pltpu.make_async_remote_copy(src_ref=srcs[t].at[1 - c], dst_ref=dsts[t], send_sem=send.at[t],
                                              recv_sem=recv.at[t], device_id=(x, y, 1 - c), device_id_type=MESH)
            cp.start()
            cps.append(cp)
        for cp in cps:
            cp.wait()

    return _pc(body, name=name, in_specs=[HBM_SPEC] * nt, out_specs=[HBM_SPEC] * nt,
               out_shape=[SDS(g.shape[1:], g.dtype) for g in grads],
               scratch=[pltpu.SemaphoreType.DMA((nt,))] * 2)(*grads)


def _sibling_copies(grads, lands, send, recv):
    x, y, c, _ = _place()
    return [pltpu.make_async_remote_copy(src_ref=grads[t].at[1 - c], dst_ref=lands[t], send_sem=send.at[t], recv_sem=recv.at[t],
                                         device_id=(x, y, 1 - c), device_id_type=MESH) for t in range(len(grads))]


def sibling_start(name, grads):
    nt = len(grads)
    lands = [lax.empty(g.shape[1:], g.dtype) for g in grads]

    def body(*refs):
        send, recv, token = refs[2 * nt], refs[2 * nt + 1], refs[-1]
        for cp in _sibling_copies(refs[:nt], refs[nt:2 * nt], send, recv):
            cp.start()
        token[...] = jnp.zeros(TOKEN_SHAPE, F32)

    both = list(grads) + lands
    out = _split_call(
        body, name=name, in_specs=[HBM_SPEC] * (2 * nt),
        out_specs=(SEM_SPEC, SEM_SPEC) + (HBM_SPEC,) * (2 * nt) + (pl.BlockSpec(memory_space=pltpu.VMEM),),
        out_shape=(pltpu.SemaphoreType.DMA((nt,)), pltpu.SemaphoreType.DMA((nt,)))
        + tuple(pltpu.HBM(b.shape, b.dtype) for b in both) + (SDS(TOKEN_SHAPE, F32),),
        aliases={t: 2 + t for t in range(2 * nt)})(*_in_hbm(both))
    return out[0], out[1], list(out[2:2 + nt]), list(out[2 + nt:2 + 2 * nt]), out[-1]


def sibling_wait(name, send, recv, grads, lands, after):
    nt = len(grads)

    def body(*refs):
        for cp in _sibling_copies(refs[:nt], refs[nt:2 * nt], refs[2 * nt], refs[2 * nt + 1]):
            cp.wait_send()
            cp.wait_recv()

    both = list(grads) + list(lands)
    out = _split_call(
        body, name=name, in_specs=[HBM_SPEC] * (2 * nt) + [SEM_SPEC, SEM_SPEC, pl.BlockSpec(memory_space=pl.ANY)],
        out_specs=(HBM_SPEC,) * (2 * nt), out_shape=tuple(pltpu.HBM(b.shape, b.dtype) for b in both),
        aliases={t: t for t in range(2 * nt)})(*both, send, recv, after)
    return list(out[:nt]), list(out[nt:])


def add_halves(name, place, grads, gots):
    nt = len(grads)

    def body(place_ref, *refs):
        for t in range(nt):
            val = (refs[t][...].astype(F32) + refs[nt + t][...].astype(F32)).astype(BF16)
            refs[2 * nt + t][...] = val

            @pl.when(pl.program_id(1) == place_ref[1])
            def _():
                refs[3 * nt + t][...] = val

    g_specs, o_specs, part_specs, slot_specs = [], [], [], []
    for g in grads:
        _, n, r, c = g.shape
        tr = _step_rows(r)
        g_specs.append(pl.BlockSpec((None, None, tr, c), lambda i, j, pr: (pr[0], j, i, 0)))
        o_specs.append(pl.BlockSpec((None, tr, c), lambda i, j, pr: (j, i, 0)))
        part_specs.append(pl.BlockSpec((None, tr, c), lambda i, j, pr: (j, i, 0)))
        slot_specs.append(pl.BlockSpec((None, tr, c), lambda i, j, pr: (pr[1], i, 0)))
    gs = pltpu.PrefetchScalarGridSpec(num_scalar_prefetch=1, grid=(GROUP_STEPS, N_CHIPS), in_specs=g_specs + o_specs,
                                      out_specs=part_specs + slot_specs)
    out = _pc(body, name=name, grid_spec=gs, in_specs=None, out_specs=None,
              out_shape=[SDS(g.shape[1:], BF16) for g in grads] * 2, sem=("parallel", "arbitrary"))(place, *grads, *gots)
    return list(out[:nt]), list(out[nt:])


def _chips_copies(parts, slots, send, recv):
    x, y, c, chips = _place()
    cps = []
    for t in range(len(parts)):
        for k, (cx, cy) in enumerate(chips):
            cps.append(pltpu.make_async_remote_copy(src_ref=parts[t].at[2 * cx + cy], dst_ref=slots[t].at[2 * x + y],
                                                    send_sem=send.at[3 * t + k], recv_sem=recv.at[3 * t + k],
                                                    device_id=(cx, cy, c), device_id_type=MESH))
    return cps


def chips_start(name, parts, slots):
    nt = len(parts)
    n = 3 * nt

    def body(*refs):
        send, recv, token = refs[2 * nt], refs[2 * nt + 1], refs[-1]
        for cp in _chips_copies(refs[:nt], refs[nt:2 * nt], send, recv):
            cp.start()
        token[...] = jnp.zeros(TOKEN_SHAPE, F32)

    both = list(parts) + list(slots)
    out = _split_call(
        body, name=name, in_specs=[HBM_SPEC] * (2 * nt),
        out_specs=(SEM_SPEC, SEM_SPEC) + (HBM_SPEC,) * (2 * nt) + (pl.BlockSpec(memory_space=pltpu.VMEM),),
        out_shape=(pltpu.SemaphoreType.DMA((n,)), pltpu.SemaphoreType.DMA((n,)))
        + tuple(pltpu.HBM(b.shape, b.dtype) for b in both) + (SDS(TOKEN_SHAPE, F32),),
        aliases={t: 2 + t for t in range(2 * nt)})(*_in_hbm(both))
    return out[0], out[1], list(out[2:2 + nt]), list(out[2 + nt:2 + 2 * nt]), out[-1]


def chips_wait(name, send, recv, parts, slots, after):
    nt = len(parts)

    def body(*refs):
        for cp in _chips_copies(refs[:nt], refs[nt:2 * nt], refs[2 * nt], refs[2 * nt + 1]):
            cp.wait_send()
            cp.wait_recv()

    both = list(parts) + list(slots)
    out = _split_call(
        body, name=name, in_specs=[HBM_SPEC] * (2 * nt) + [SEM_SPEC, SEM_SPEC, pl.BlockSpec(memory_space=pl.ANY)],
        out_specs=(HBM_SPEC,) * (2 * nt), out_shape=tuple(pltpu.HBM(b.shape, b.dtype) for b in both),
        aliases={t: t for t in range(2 * nt)})(*both, send, recv, after)
    return list(out[nt:])


def sum_chips(name, place, slots):
    nt = len(slots)

    def body(place_ref, *refs):
        for t in range(nt):
            s_ref = refs[t]
            acc = s_ref[0].astype(F32)
            for k in range(1, N_CHIPS):
                acc = acc + s_ref[k].astype(F32)
            refs[nt + t][...] = acc

    in_specs, out_specs = [], []
    for sl in slots:
        n, r, c = sl.shape
        tr = _step_rows(r)
        in_specs.append(pl.BlockSpec((n, tr, c), lambda i, pr: (0, i, 0)))
        out_specs.append(pl.BlockSpec((None, tr, c), lambda i, pr: (pr[0], i, 0)))
    gs = pltpu.PrefetchScalarGridSpec(num_scalar_prefetch=1, grid=(GROUP_STEPS,), in_specs=in_specs, out_specs=out_specs)
    return list(_pc(body, name=name, grid_spec=gs, in_specs=None, out_specs=None,
                    out_shape=[SDS((2,) + sl.shape[1:], F32) for sl in slots], sem=("parallel",))(place, *slots))


def share_with_sibling(name, bufs):
    nt = len(bufs)

    def body(*refs):
        ins, outs = refs[:nt], refs[nt:2 * nt]
        send, recv = refs[2 * nt:]
        x, y, c, _ = _place()
        cps = []
        for t in range(nt):
            cp = pltpu.make_async_remote_copy(src_ref=ins[t].at[c], dst_ref=outs[t].at[c], send_sem=send.at[t], recv_sem=recv.at[t],
                                              device_id=(x, y, 1 - c), device_id_type=MESH)
            cp.start()
            cps.append(cp)
        for t in range(nt):
            theirs = outs[t].at[1 - c]
            pltpu.make_async_remote_copy(src_ref=theirs, dst_ref=theirs, send_sem=send.at[t], recv_sem=recv.at[t],
                                         device_id=(x, y, 1 - c), device_id_type=MESH).wait_recv()
        for cp in cps:
            cp.wait_send()

    return _pc(body, name=name, in_specs=[HBM_SPEC] * nt, out_specs=[HBM_SPEC] * nt,
               out_shape=[SDS(b.shape, F32) for b in bufs], scratch=[pltpu.SemaphoreType.DMA((nt,))] * 2,
               input_output_aliases={t: t for t in range(nt)})(*bufs)


def _adamw(w, g, m, v):
    m = ADAM_B1 * m + (1.0 - ADAM_B1) * g
    v = ADAM_B2 * v + (1.0 - ADAM_B2) * (g * g)
    m_hat = m / (1.0 - ADAM_B1 ** ADAM_STEP)
    v_hat = v / (1.0 - ADAM_B2 ** ADAM_STEP)
    delta = -ADAM_LR * (m_hat / (jnp.sqrt(v_hat) + ADAM_EPS) + ADAM_WD * w)
    return delta, m, v


def adamw_big(name, w, m, v, g0, g1, axis):
    _, r, c = w.shape
    _, rh, ch = g0.shape
    tr = _tile(rh, 256)
    nb = rh // tr
    if axis == 0:
        wspec = pl.BlockSpec((None, tr, ch), lambda l, h, i: (l, h * nb + i, 0))
    else:
        wspec = pl.BlockSpec((None, tr, ch), lambda l, h, i: (l, i, h))
    g0spec = pl.BlockSpec((None, tr, ch), lambda l, h, i: (h * (1 - l), i * (1 - l), 0))
    g1spec = pl.BlockSpec((None, tr, ch), lambda l, h, i: (h * l, i * l, 0))

    def body(w_ref, m_ref, v_ref, g0_ref, g1_ref, go_ref, d_ref, mo_ref, vo_ref):
        def run(g_ref):
            g = g_ref[...]
            delta, mn, vn = _adamw(w_ref[...], g, m_ref[...], v_ref[...])
            go_ref[...] = g
            d_ref[...] = delta
            mo_ref[...] = mn
            vo_ref[...] = vn

        @pl.when(pl.program_id(0) == 0)
        def _():
            run(g0_ref)

        @pl.when(pl.program_id(0) == 1)
        def _():
            run(g1_ref)

    return _pc(body, name=name, grid=(2, 2, nb), in_specs=[wspec, wspec, wspec, g0spec, g1spec], out_specs=[wspec] * 4,
               out_shape=[SDS(w.shape, F32)] * 4, sem=("parallel", "parallel", "parallel"))(w, m, v, g0, g1)


SMALL = (("pre_mix_gain", 2048), ("post_mix_gain", 2048), ("pre_ffn_gain", 2048), ("post_ffn_gain", 2048), ("ple_gain", 2048),
         ("attn_out_gain", 1024), ("hgrn_out_gain", 1024), ("hgrn_lb_logits", 1024), ("attn_sinks", 128))
SMALL_ROWS = sum(2 * w // LANES for _, w in SMALL)
SMALL_PAD = -(-SMALL_ROWS // 8) * 8
LB_ROW = sum(2 * w // LANES for _, w in SMALL[:7])


def _pack_small(parts):
    rows = []
    for nm, w in SMALL:
        a = parts[nm].astype(F32)
        if a.shape[1] != w:
            a = jnp.pad(a, ((0, 0), (0, w - a.shape[1])))
        rows.append(a.reshape(2 * w // LANES, LANES))
    rows.append(jnp.zeros((SMALL_PAD - SMALL_ROWS, LANES), F32))
    return jnp.concatenate(rows, axis=0)


def _unpack_small(packed, widths):
    out, r = {}, 0
    for nm, w in SMALL:
        n = 2 * w // LANES
        out[nm] = packed[r:r + n].reshape(2, w)[:, :widths[nm]]
        r += n
    return out


def allreduce_small(name, packed):
    rows = packed.shape[0]

    def body(x_ref, o_ref, buf, send, recv, own_sem):
        x, y, c, _ = _place()
        me = 4 * x + 2 * y + c
        own = pltpu.make_async_copy(x_ref, buf.at[me], own_sem)
        own.start()
        cps = []
        for k in range(1, 8):
            px, py, pc = x ^ (k >> 2), y ^ ((k >> 1) & 1), c ^ (k & 1)
            cp = pltpu.make_async_remote_copy(src_ref=x_ref, dst_ref=buf.at[me], send_sem=send.at[k - 1], recv_sem=recv.at[k - 1],
                                              device_id=(px, py, pc), device_id_type=MESH)
            cp.start()
            cps.append(cp)
        for k in range(1, 8):
            px, py, pc = x ^ (k >> 2), y ^ ((k >> 1) & 1), c ^ (k & 1)
            slot = buf.at[4 * px + 2 * py + pc]
            pltpu.make_async_remote_copy(src_ref=slot, dst_ref=slot, send_sem=send.at[k - 1], recv_sem=recv.at[k - 1],
                                         device_id=(px, py, pc), device_id_type=MESH).wait_recv()
        for cp in cps:
            cp.wait_send()
        own.wait()
        acc = buf[0]
        for k in range(1, 8):
            acc = acc + buf[k]
        o_ref[...] = acc

    vm = pl.BlockSpec(memory_space=pltpu.VMEM)
    return _pc(body, name=name, in_specs=[vm], out_specs=vm, out_shape=SDS((rows, LANES), F32),
               scratch=[pltpu.VMEM((8, rows, LANES), F32), pltpu.SemaphoreType.DMA((7,)), pltpu.SemaphoreType.DMA((7,)),
                        pltpu.SemaphoreType.DMA])(packed)


def adamw_small(name, w, m, v, g):
    rows = w.shape[0]
    n = HGRN_WIDTH // LANES

    def body(w_ref, m_ref, v_ref, g_ref, go_ref, d_ref, mo_ref, vo_ref):
        go_ref[...] = g_ref[...]
        l0 = w_ref[LB_ROW:LB_ROW + n, :]
        l1 = w_ref[LB_ROW + n:LB_ROW + 2 * n, :]
        mx = jnp.maximum(l0, l1)
        e0, e1 = jnp.exp(l0 - mx), jnp.exp(l1 - mx)
        s0, s1 = e0 / (e0 + e1), e1 / (e0 + e1)
        dlb1 = g_ref[LB_ROW + n:LB_ROW + 2 * n, :]
        inner = s1 * dlb1
        go_ref[LB_ROW:LB_ROW + n, :] = s0 * (0.0 - inner)
        go_ref[LB_ROW + n:LB_ROW + 2 * n, :] = s1 * (dlb1 - inner)
        delta, mn, vn = _adamw(w_ref[...], go_ref[...], m_ref[...], v_ref[...])
        d_ref[...] = delta
        mo_ref[...] = mn
        vo_ref[...] = vn

    vm = pl.BlockSpec(memory_space=pltpu.VMEM)
    return _pc(body, name=name, in_specs=[vm] * 4, out_specs=[vm] * 4, out_shape=[SDS((rows, LANES), F32)] * 4)(w, m, v, g)


def _layer_fwd(l, x, h1, p_l, w_in_g, rest_of_weights, gains, cos, sin, sinks, lb_logits, g_next, target):
    n = f"l{l}_"
    proj = mm_col(n + "in_proj", h1, w_in_g)
    qp, kp, vp = rope_qkv(n + "rope_qkv", proj, cos, sin)
    attn = attn_fwd(n + "attn_fwd", qp, kp, vp, sinks)
    o, states = hgrn_fwd(n + "hgrn_fwd", proj, lb_logits, l)
    cat = mix_out_fwd(n + "mix_out_fwd", attn, o, proj, gains["attn_out_gain"], gains["hgrn_out_gain"])
    rest, token = rest_of_weights(cat)
    wts = dict(rest, w_in=w_in_g)
    if token is not None:
        gains = _with_token(gains, "post_mix_gain", token)
    m, x1, h2 = out_proj_post_mix(n + "out_proj_post_mix", cat, wts["w_out"], gains["post_mix_gain"], x, gains["pre_ffn_gain"])
    g, u, a = ffn_gate_up(n + "ffn_gate_up", h2, wts["w_ffn_gate"], wts["w_ffn_up"])
    f = mm_row(n + "ffn_down", a, wts["w_ffn_down"])
    x2, h3 = post_pre_norm(n + "post_ffn", f, gains["post_ffn_gain"], x1, gains["ple_gain"])
    saved = dict(x=x, h1=h1, proj=proj, qp=qp, kp=kp, vp=vp, attn=attn, o=o, states=states, cat=cat, m=m, x1=x1, h2=h2,
                 g=g, u=u, a=a, f=f, x2=x2, h3=h3, p=p_l)
    if target is None:
        z, pp, *out = ple_gate_fwd_mid(n + "ple_gate_fwd", h3, wts["w_ple_gate"], p_l, wts["w_ple_proj"], x2, g_next)
        saved.update(z=z, pp=pp)
    else:
        dy, dpp, dz, loss = ple_gate_fwd_loss(n + "ple_gate_loss", h3, wts["w_ple_gate"], p_l, wts["w_ple_proj"], x2, target)
        out = [dy, loss]
        saved.update(dpp=dpp, dz=dz)
    return out, saved, wts


EARLY = ("w_ple_gate", "w_ple_proj", "w_ffn_down", "w_ffn_gate", "w_ffn_up")
LATE = ("w_out", "w_in")


def _layer_bwd_ffn(l, dx3, sv, wts, gains, after=None, hook=None):
    n = f"l{l}_"
    dpp, dz = (sv["dpp"], sv["dz"]) if "dz" in sv else ple_bwd(n + "ple_bwd", dx3, sv["z"], sv["pp"],
                                                                 dx3 if after is None else after)
    dh3 = mm_row_t(n + "ple_gate_dx", dz, wts["w_ple_gate"])
    if hook is not None:
        gains = _with_token(gains, "ple_gain", hook(dh3))
    dx2, df, d_ple_gain, d_post_ffn = norm_bwd_pair(n + "ple_post_ffn_bwd", sv["x2"], gains["ple_gain"], dh3, dx3, sv["f"],
                                                    gains["post_ffn_gain"])
    dg, du = ffn_down_bwd(n + "ffn_down_bwd", df, wts["w_ffn_down"], sv["g"], sv["u"])
    big = dict(
        w_ple_gate=mm_wg_row(n + "ple_gate_dw", sv["h3"], dz),
        w_ple_proj=mm_wg_col(n + "ple_proj_dw", sv["p"], dpp),
        w_ffn_down=mm_wg_row(n + "ffn_down_dw", sv["a"], df),
        w_ffn_gate=mm_wg_col(n + "ffn_gate_dw", sv["h2"], dg),
        w_ffn_up=mm_wg_col(n + "ffn_up_dw", sv["h2"], du),
    )
    return dict(dx2=dx2, dg=dg, du=du), big, dict(ple_gain=d_ple_gain, post_ffn_gain=d_post_ffn)


def _layer_bwd_mix(l, st, sv, wts, gains, cos, sin, sinks, lb_logits, after=None, hook=None):
    n = f"l{l}_"
    dh2 = mm_col_t(n + "ffn_gate_dx", st["dg"], wts["w_ffn_gate"], after=after)
    dh2 = mm_col_t(n + "ffn_up_dx", st["du"], wts["w_ffn_up"], add=dh2)
    if hook is not None:
        gains = _with_token(gains, "pre_ffn_gain", hook(dh2))
    dx1, dm, d_pre_ffn, d_post_mix = norm_bwd_pair(n + "pre_ffn_post_mix_bwd", sv["x1"], gains["pre_ffn_gain"], dh2, st["dx2"],
                                                   sv["m"], gains["post_mix_gain"])
    dcat = mm_row_t(n + "out_proj_dx", dm, wts["w_out"])
    dattn, do, dhg, d_attn_gain, d_hgrn_gain = mix_out_bwd(n + "mix_out_bwd", dcat, sv["attn"], sv["o"], sv["proj"],
                                                            gains["attn_out_gain"], gains["hgrn_out_gain"])
    dqp, dkc, dkp, dvc, dvp, dsinks = attn_bwd(n + "attn_bwd", sv["qp"], sv["kp"], sv["vp"], sinks, dattn)
    dqkv = rope_bwd(n + "rope_bwd", dqp, dkc, dkp, dvc, dvp, cos, sin)
    dhq, dhf, dhi, dlb = hgrn_bwd(n + "hgrn_bwd", sv["proj"], lb_logits, l, sv["states"], do)
    dproj, dh1 = in_proj_bwd(n + "in_proj_dx", [dqkv, dhq, dhf, dhi, dhg], wts["w_in"])
    dx, d_pre_mix = norm_bwd(n + "pre_mix_bwd", sv["x"], gains["pre_mix_gain"], dh1, dx1)
    big = dict(w_out=mm_wg_row(n + "out_proj_dw", sv["cat"], dm), w_in=mm_wg_col(n + "in_proj_dw", sv["h1"], dproj))
    small = dict(pre_mix_gain=d_pre_mix, post_mix_gain=d_post_mix, pre_ffn_gain=d_pre_ffn, attn_out_gain=d_attn_gain,
                 hgrn_out_gain=d_hgrn_gain, hgrn_lb_logits=dlb, attn_sinks=dsinks)
    return dx, big, small


def _reduce_start(tag, names, big, place):
    grads = [big[nm] for nm in names]
    got = reduce_to_sibling(tag + "_reduce_to_sibling", grads)
    parts, slots = add_halves(tag + "_add", place, grads, got)
    return chips_start(tag + "_chips_start", parts, slots)


def _reduce_begin(tag, names, big):
    return sibling_start(tag + "_sibling_start", [big[nm] for nm in names])


def _reduce_chips(tag, names, begun, place, after):
    send, recv, grads, lands, _ = begun
    grads, got = sibling_wait(tag + "_sibling_wait", send, recv, grads, lands, after)
    parts, slots = add_halves(tag + "_add", place, grads, got)
    return chips_start(tag + "_chips_start", parts, slots)


def _reduce_finish(tag, names, started, place, after):
    send, recv, parts, slots, _ = started
    slots = chips_wait(tag + "_chips_wait", send, recv, parts, slots, after)
    bufs = sum_chips(tag + "_sum", place, slots)
    return dict(zip(names, share_with_sibling(tag + "_share_with_sibling", bufs)))


def _reduce_finish_both(first, second, place, after):
    slots = []
    for tag, _, (send, recv, parts, slots_t, _) in (first, second):
        slots += chips_wait(tag + "_chips_wait", send, recv, parts, slots_t, after)
    tag = first[0] + "_" + second[0]
    shared = share_with_sibling(tag + "_share_with_sibling", sum_chips(tag + "_sum", place, slots))
    n1 = len(first[1])
    return dict(zip(first[1], shared[:n1])), dict(zip(second[1], shared[n1:]))


def _with_token(gains, name, token):
    out = dict(gains)
    out[name] = gains[name] + token[0, 0]
    return out


def kernel(x, p, positions, w_in, attn_sinks, hgrn_lb_logits, attn_out_gain, hgrn_out_gain, w_out, pre_mix_gain, post_mix_gain, pre_ffn_gain, post_ffn_gain, w_ffn_gate, w_ffn_up, w_ffn_down, ple_gain, w_ple_gate, w_ple_proj, loss_target, m_w_in, m_attn_sinks, m_hgrn_lb_logits, m_attn_out_gain, m_hgrn_out_gain, m_w_out, m_pre_mix_gain, m_post_mix_gain, m_pre_ffn_gain, m_post_ffn_gain, m_w_ffn_gate, m_w_ffn_up, m_w_ffn_down, m_ple_gain, m_w_ple_gate, m_w_ple_proj, v_w_in, v_attn_sinks, v_hgrn_lb_logits, v_attn_out_gain, v_hgrn_out_gain, v_w_out, v_pre_mix_gain, v_post_mix_gain, v_pre_ffn_gain, v_post_ffn_gain, v_w_ffn_gate, v_w_ffn_up, v_w_ffn_down, v_ple_gain, v_w_ple_gate, v_w_ple_proj):
    given = dict(locals())
    depth = 2
    place = jnp.stack([lax.axis_index("c"), 2 * lax.axis_index("x") + lax.axis_index("y")]).astype(jnp.int32)
    xs = x[0]
    tgt = loss_target[0]
    pos_col = positions.reshape(-1, 1)
    half = 32
    inv_freq = ROPE_THETA ** (-jnp.arange(half, dtype=F32) / half)
    inv_freq = jnp.tile(inv_freq, 4).reshape(1, LANES)
    gains = [{nm: given[nm][l:l + 1] for nm, _ in SMALL[:7]} for l in range(depth)]
    names = [nm for nm, *_ in BIG]
    first, others = names[:1], names[1:]

    def specs(nms):
        return [BIG_BY_NAME[nm] for nm in nms]

    def cast(tag, l, nms, after):
        return cast_to_slots(tag + "_cast", place, [given[nm] for nm in nms], l, after)

    def finish_gather(tag, nms, started, after):
        bufs = gather_wait(tag + "_gather_wait", specs(nms), started[0], started[1], started[2], after)
        return dict(zip(nms, gather_pass(tag + "_gather_pass", specs(nms), bufs)))

    g0a = gather_start("l0a_gather_start", specs(first), cast("l0a", 0, first, place), place)
    g0b = gather_start("l0b_gather_start", specs(others), cast("l0b", 0, others, g0a[3]), g0a[3])
    started = {}

    def rest_of_layer0(after):
        got = finish_gather("l0b", others, g0b, after)
        started["l1a"] = gather_start("l1a_gather_start", specs(first), l1_shards[:1], got["w_out"])
        started["l1b"] = gather_start("l1b_gather_start", specs(others), l1_shards[1:], started["l1a"][3])
        return got, started["l1b"][3]

    cos, sin = rope_tables("rope_tables", pos_col, inv_freq)
    h1 = pre_norm("l0_pre_mix", xs, _with_token(gains[0], "pre_mix_gain", g0b[3])["pre_mix_gain"])
    l1_shards = cast("l1", 1, names, h1)
    w_in0 = finish_gather("l0a", first, g0a, l1_shards[0])["w_in"]
    (x_mid, h1_next), sv0, wts0 = _layer_fwd(0, xs, h1, p[0, 0], w_in0, rest_of_layer0, gains[0], cos, sin, attn_sinks[0],
                                             hgrn_lb_logits, gains[1]["pre_mix_gain"], None)
    w_in1 = finish_gather("l1a", first, started["l1a"], x_mid)["w_in"]
    (dy, loss_part), sv1, wts1 = _layer_fwd(1, x_mid, h1_next, p[1, 0], w_in1,
                                            lambda after: (finish_gather("l1b", others, started["l1b"], after), None),
                                            gains[1], cos, sin, attn_sinks[1], hgrn_lb_logits, None, tgt)

    st1, early1, small1 = _layer_bwd_ffn(1, dy, sv1, wts1, gains[1])
    dx_mid, late1, small1b = _layer_bwd_mix(1, st1, sv1, wts1, gains[1], cos, sin, attn_sinks[1], hgrn_lb_logits)
    big1, small1 = {**early1, **late1}, {**small1, **small1b}
    chips = {}

    def chips_after(tag, nms, begun):
        def hook(x):
            chips[tag] = _reduce_chips(tag, nms, begun, place, x)
            return chips[tag][4]
        return hook

    b1 = _reduce_begin("l1", names, big1)
    st0, early0, small0 = _layer_bwd_ffn(0, dx_mid, sv0, wts0, gains[0], after=b1[4], hook=chips_after("l1", names, b1))
    b0e = _reduce_begin("l0e", EARLY, early0)
    dx0, late0, small0b = _layer_bwd_mix(0, st0, sv0, wts0, gains[0], cos, sin, attn_sinks[0], hgrn_lb_logits, after=b0e[4],
                                         hook=chips_after("l0e", EARLY, b0e))
    r1, r0e = chips["l1"], chips["l0e"]
    small0 = {**small0, **small0b}
    r0l = _reduce_start("l0l", LATE, late0, place)
    red1, red0 = _reduce_finish_both(("l1", names, r1), ("l0e", EARLY, r0e), place, r0l[4])

    loss = lax.psum(loss_part[0, 0], ("x", "y", "c"))
    grad_x = dx0[None]

    out_big = {}
    for nm in EARLY:
        out_big[nm] = adamw_big("adamw_" + nm, given[nm], given["m_" + nm], given["v_" + nm], red0[nm], red1[nm], BIG_BY_NAME[nm][3])
    red0.update(_reduce_finish("l0l", LATE, r0l, place, out_big[EARLY[-1]][3]))
    for nm in LATE:
        out_big[nm] = adamw_big("adamw_" + nm, given[nm], given["m_" + nm], given["v_" + nm], red0[nm], red1[nm], BIG_BY_NAME[nm][3])

    widths = {nm: given[nm].shape[1] for nm, _ in SMALL}
    small_g = {nm: jnp.concatenate([small0[nm][:, :widths[nm]] if nm != "attn_sinks" else small0[nm][:, :LANES],
                                    small1[nm][:, :widths[nm]] if nm != "attn_sinks" else small1[nm][:, :LANES]], axis=0)
               for nm, _ in SMALL}
    g_sum = allreduce_small("allreduce_small", _pack_small(small_g))
    sm = adamw_small("adamw_small", _pack_small({nm: given[nm] for nm, _ in SMALL}),
                     _pack_small({nm: given["m_" + nm] for nm, _ in SMALL}),
                     _pack_small({nm: given["v_" + nm] for nm, _ in SMALL}), g_sum)
    out_small = [_unpack_small(a, widths) for a in sm]

    order = ["w_in", "attn_sinks", "hgrn_lb_logits", "attn_out_gain", "hgrn_out_gain", "w_out", "pre_mix_gain", "post_mix_gain",
             "pre_ffn_gain", "post_ffn_gain", "w_ffn_gate", "w_ffn_up", "w_ffn_down", "ple_gain", "w_ple_gate", "w_ple_proj"]
    res = [loss, grad_x]
    for k in range(4):
        for nm in order:
            res.append(out_big[nm][k] if nm in out_big else out_small[k][nm])
    return tuple(res)
```

```python
import jax
import jax.numpy as jnp
from jax import lax
from jax.experimental import pallas as pl
from jax.experimental.pallas import tpu as pltpu

F32, BF16 = jnp.float32, jnp.bfloat16
SDS = jax.ShapeDtypeStruct
MESH = pl.DeviceIdType.MESH

D_MODEL = 2048
ATTN_WIDTH = 1024
HGRN_WIDTH = 1024
KV_WIDTH = 256
N_Q_HEADS = 16
N_KV_HEADS = 4
Q_PER_KV = 4
WINDOW = 128
MASK_VALUE = -1e30
ROPE_THETA = 10000.0
HGRN_HEADS = 8
HGRN_CHUNK = 16
RMS_EPS = 1e-6
LANES = 128
N_CHIPS = 4
COL_Q, COL_K, COL_V, COL_HQ, COL_HF, COL_HI, COL_HG = 0, 8, 10, 12, 20, 28, 36

ADAM_LR, ADAM_B1, ADAM_B2, ADAM_EPS, ADAM_WD, ADAM_STEP = 0.001, 0.9, 0.999, 1e-08, 0.01, 10

VMEM_LIMIT = 56 * 1024 * 1024
ROW_TILE = 256
ROW_TILE_WIDE = 512

_NN = (((1,), (0,)), ((), ()))
_NT = (((1,), (1,)), ((), ()))
_TN = (((0,), (0,)), ((), ()))


def _pc(body, *, name, out_shape, in_specs, out_specs, grid=(), scratch=(), sem=None, grid_spec=None, **kw):
    params = dict(vmem_limit_bytes=VMEM_LIMIT)
    if sem is not None:
        params["dimension_semantics"] = sem
    if grid_spec is not None:
        return pl.pallas_call(body, name=name, out_shape=out_shape, grid_spec=grid_spec,
                              compiler_params=pltpu.CompilerParams(**params), **kw)
    return pl.pallas_call(body, name=name, out_shape=out_shape, grid=grid, in_specs=in_specs, out_specs=out_specs,
                          scratch_shapes=list(scratch), compiler_params=pltpu.CompilerParams(**params), **kw)


def _sigmoid(x):
    return 1.0 / (1.0 + jnp.exp(-x))


def _rstd(x):
    return lax.rsqrt(jnp.mean(x * x, axis=-1, keepdims=True) + RMS_EPS)


def _rows(t, w, col=0):
    return pl.BlockSpec((t, w), lambda i, col=col: (i, col))


def _fixed(shape):
    return pl.BlockSpec(shape, lambda *_: (0,) * len(shape))


def _mm(name, a, b, *, dims, grid, a_spec, b_spec, o_spec, out_shape, parts=1, add=None, add_spec=None, after=None):
    def body(*refs):
        a_ref, b_ref, o_ref = refs[0], refs[1], refs[-1]
        if parts == 1:
            r = lax.dot_general(a_ref[...].astype(BF16), b_ref[...].astype(BF16), dims, preferred_element_type=F32)
        else:
            w = a_ref.shape[1] // parts
            r = None
            for j in range(parts):
                t = lax.dot_general(a_ref[:, j * w:(j + 1) * w].astype(BF16), b_ref[j].astype(BF16), dims,
                                    preferred_element_type=F32)
                r = t if r is None else r + t
        if add is not None:
            r = r + refs[2][...]
        o_ref[...] = r.astype(o_ref.dtype)

    ins = [a, b] + ([] if add is None else [add]) + ([] if after is None else [after])
    specs = [a_spec, b_spec] + ([] if add is None else [add_spec]) + ([] if after is None else [pl.BlockSpec(memory_space=pl.ANY)])
    return _pc(body, name=name, grid=grid, in_specs=specs, out_specs=o_spec, out_shape=out_shape,
               sem=("parallel",) * len(grid))(*ins)


def _tile(n, t):
    if n <= t:
        return n
    while n % t:
        t //= 2
    assert t % 8 == 0
    return t


def mm_col(name, a, wg, out_dtype=F32):
    s, k = a.shape
    _, _, n = wg.shape
    tm = _tile(s, 512)
    return _mm(name, a, wg, dims=_NN, grid=(N_CHIPS, s // tm),
               a_spec=pl.BlockSpec((tm, k), lambda j, i: (i, 0)),
               b_spec=pl.BlockSpec((None, k, n), lambda j, i: (j, 0, 0)),
               o_spec=pl.BlockSpec((tm, n), lambda j, i: (i, j)),
               out_shape=SDS((s, N_CHIPS * n), out_dtype))


def mm_row(name, a, wg, out_dtype=F32):
    s, _ = a.shape
    _, r, n = wg.shape
    tm = _tile(s, 512)
    tn = _tile(n, 1024 if r > 512 else 2048)
    return _mm(name, a, wg, dims=_NN, grid=(n // tn, s // tm), parts=N_CHIPS,
               a_spec=pl.BlockSpec((tm, N_CHIPS * r), lambda j, i: (i, 0)),
               b_spec=pl.BlockSpec((N_CHIPS, r, tn), lambda j, i: (0, 0, j)),
               o_spec=pl.BlockSpec((tm, tn), lambda j, i: (i, j)),
               out_shape=SDS((s, n), out_dtype))


def mm_col_t(name, dy, wg, add=None, out_dtype=F32, after=None):
    s, _ = dy.shape
    _, k, n = wg.shape
    tm = _tile(s, 512)
    tk = _tile(k, 1024)
    return _mm(name, dy, wg, dims=_NT, grid=(k // tk, s // tm), parts=N_CHIPS,
               a_spec=pl.BlockSpec((tm, N_CHIPS * n), lambda j, i: (i, 0)),
               b_spec=pl.BlockSpec((N_CHIPS, tk, n), lambda j, i: (0, j, 0)),
               o_spec=pl.BlockSpec((tm, tk), lambda j, i: (i, j)),
               add=add, add_spec=pl.BlockSpec((tm, tk), lambda j, i: (i, j)),
               out_shape=SDS((s, k), out_dtype), after=after)


def in_proj_bwd(name, pieces, wg):
    s = pieces[0].shape[0]
    _, k, n = wg.shape
    tm = _tile(s, 512)
    tk = _tile(k, 512)
    width = sum(p.shape[1] for p in pieces)
    npc = len(pieces)
    segments, start = [], 0
    for t, p in enumerate(pieces):
        lo = start
        while lo < start + p.shape[1]:
            hi = min(start + p.shape[1], (lo // n + 1) * n)
            segments.append((t, lo - start, hi - start, lo // n, lo % n, lo % n + hi - lo))
            lo = hi
        start += p.shape[1]

    def body(*refs):
        w_ref, dproj_ref, o_ref = refs[npc], refs[npc + 1], refs[npc + 2]

        @pl.when(pl.program_id(1) == 0)
        def _():
            off = 0
            for t in range(npc):
                dproj_ref[:, off:off + pieces[t].shape[1]] = refs[t][...]
                off += pieces[t].shape[1]

        r = None
        for t, a, b, j, c, d in segments:
            part = lax.dot_general(refs[t][:, a:b], w_ref[j, :, c:d], _NT, preferred_element_type=F32)
            r = part if r is None else r + part
        o_ref[...] = r

    return _pc(body, name=name, grid=(s // tm, k // tk),
               in_specs=[pl.BlockSpec((tm, p.shape[1]), lambda i, j: (i, 0)) for p in pieces]
               + [pl.BlockSpec((N_CHIPS, tk, n), lambda i, j: (0, j, 0))],
               out_specs=[pl.BlockSpec((tm, width), lambda i, j: (i, 0)), pl.BlockSpec((tm, tk), lambda i, j: (i, j))],
               out_shape=[SDS((s, width), BF16), SDS((s, k), F32)], sem=("parallel", "arbitrary"))(*pieces, wg)


def mm_row_t(name, dy, wg, out_dtype=F32):
    s, n = dy.shape
    _, r, _ = wg.shape
    tm = _tile(s, 512)

    def body(dy_ref, w_ref, o_ref):
        dyv = dy_ref[...].astype(BF16)
        for j in range(N_CHIPS):
            o_ref[:, j * r:(j + 1) * r] = lax.dot_general(dyv, w_ref[j], _NT, preferred_element_type=F32).astype(out_dtype)

    return _pc(body, name=name, grid=(s // tm,), in_specs=[_rows(tm, n), _fixed(wg.shape)], out_specs=_rows(tm, N_CHIPS * r),
               out_shape=SDS((s, N_CHIPS * r), out_dtype), sem=("parallel",))(dy, wg)


def norm_bwd_pair(name, x_a, gain_a, dy, dres, x_b, gain_b):
    s, d = x_a.shape
    t = _tile(s, ROW_TILE)

    def one(xv, g, dyv):
        r = _rstd(xv)
        xh = xv * r
        dyg = dyv * g
        return r * (dyg - xh * jnp.mean(dyg * xh, axis=-1, keepdims=True)), jnp.sum(dyv * xh, axis=0, keepdims=True)

    def body(xa_ref, ga_ref, dy_ref, r_ref, xb_ref, gb_ref, dx_ref, db_ref, dga_ref, dgb_ref):
        dx, pa = one(xa_ref[...], ga_ref[...], dy_ref[...])
        dx = dx + r_ref[...]
        dx_ref[...] = dx
        db, pb = one(xb_ref[...], gb_ref[...], dx)
        db_ref[...] = db.astype(BF16)

        @pl.when(pl.program_id(0) == 0)
        def _():
            dga_ref[...] = pa
            dgb_ref[...] = pb

        @pl.when(pl.program_id(0) > 0)
        def _():
            dga_ref[...] += pa
            dgb_ref[...] += pb

    row, gain = _rows(t, d), _fixed((1, d))
    return _pc(body, name=name, grid=(s // t,), in_specs=[row, gain, row, row, row, gain], out_specs=[row, row, gain, gain],
               out_shape=[SDS((s, d), F32), SDS((s, d), BF16), SDS((1, d), F32), SDS((1, d), F32)],
               sem=("arbitrary",))(x_a, gain_a, dy, dres, x_b, gain_b)


def ffn_gate_up(name, h, wg_gate, wg_up):
    s, k = h.shape
    _, _, n = wg_gate.shape
    tm = _tile(s, 512)

    def body(h_ref, wg_ref, wu_ref, g_ref, u_ref, a_ref):
        hv = h_ref[...]
        g = jnp.dot(hv, wg_ref[...], preferred_element_type=F32)
        u = jnp.dot(hv, wu_ref[...], preferred_element_type=F32)
        g_ref[...] = g
        u_ref[...] = u
        a_ref[...] = ((g * _sigmoid(g)) * u).astype(BF16)

    wspec = pl.BlockSpec((None, k, n), lambda j, i: (j, 0, 0))
    ospec = pl.BlockSpec((tm, n), lambda j, i: (i, j))
    return _pc(body, name=name, grid=(N_CHIPS, s // tm), in_specs=[pl.BlockSpec((tm, k), lambda j, i: (i, 0)), wspec, wspec],
               out_specs=[ospec] * 3, out_shape=[SDS((s, N_CHIPS * n), F32)] * 2 + [SDS((s, N_CHIPS * n), BF16)],
               sem=("parallel", "parallel"))(h, wg_gate, wg_up)


def ffn_down_bwd(name, df, wg_down, g, u):
    s, n = df.shape
    _, r, _ = wg_down.shape
    tm = _tile(s, 512)

    def body(df_ref, w_ref, g_ref, u_ref, dg_ref, du_ref):
        da = lax.dot_general(df_ref[...], w_ref[...], _NT, preferred_element_type=F32)
        gv = g_ref[...]
        sg = _sigmoid(gv)
        du_ref[...] = (da * (gv * sg)).astype(BF16)
        dg_ref[...] = ((da * u_ref[...]) * (sg * (1.0 + gv * (1.0 - sg)))).astype(BF16)

    cspec = pl.BlockSpec((tm, r), lambda j, i: (i, j))
    return _pc(body, name=name, grid=(N_CHIPS, s // tm),
               in_specs=[pl.BlockSpec((tm, n), lambda j, i: (i, 0)), pl.BlockSpec((None, r, n), lambda j, i: (j, 0, 0)), cspec, cspec],
               out_specs=[cspec] * 2, out_shape=[SDS((s, N_CHIPS * r), BF16)] * 2,
               sem=("parallel", "parallel"))(df, wg_down, g, u)


def mm_wg_col(name, a, dy):
    s, k = a.shape
    n = dy.shape[1] // N_CHIPS
    tm = _tile(k // 2, 512)
    hb = (k // 2) // tm
    return _mm(name, a, dy, dims=_TN, grid=(N_CHIPS, k // tm),
               a_spec=pl.BlockSpec((s, tm), lambda j, i: (0, i)),
               b_spec=pl.BlockSpec((s, n), lambda j, i: (0, j)),
               o_spec=pl.BlockSpec((None, None, tm, n), lambda j, i: (i // hb, j, i % hb, 0)),
               out_shape=SDS((2, N_CHIPS, k // 2, n), BF16))


def mm_wg_row(name, a, dy):
    s, n = dy.shape
    r = a.shape[1] // N_CHIPS
    tn = _tile(n // 2, 512)
    nb = (n // 2) // tn
    return _mm(name, a, dy, dims=_TN, grid=(N_CHIPS, n // tn),
               a_spec=pl.BlockSpec((s, r), lambda j, i: (0, j)),
               b_spec=pl.BlockSpec((s, tn), lambda j, i: (0, i)),
               o_spec=pl.BlockSpec((None, None, r, tn), lambda j, i: (i // nb, j, 0, i % nb)),
               out_shape=SDS((2, N_CHIPS, r, n // 2), BF16))


def pre_norm(name, x, gain):
    s, d = x.shape
    t = _tile(s, ROW_TILE_WIDE)

    def body(x_ref, g_ref, o_ref):
        xv = x_ref[...]
        o_ref[...] = ((xv * _rstd(xv)) * g_ref[...]).astype(BF16)

    return _pc(body, name=name, grid=(s // t,), in_specs=[_rows(t, d), _fixed((1, d))], out_specs=_rows(t, d),
               out_shape=SDS((s, d), BF16), sem=("parallel",))(x, gain)


def post_pre_norm(name, m, g_post, res, g_pre):
    s, d = m.shape
    t = _tile(s, ROW_TILE_WIDE)

    def body(m_ref, gp_ref, r_ref, gn_ref, x_ref, h_ref):
        mv = m_ref[...]
        xn = r_ref[...] + (mv * _rstd(mv)) * gp_ref[...]
        x_ref[...] = xn
        h_ref[...] = ((xn * _rstd(xn)) * gn_ref[...]).astype(BF16)

    return _pc(body, name=name, grid=(s // t,),
               in_specs=[_rows(t, d), _fixed((1, d)), _rows(t, d), _fixed((1, d))],
               out_specs=[_rows(t, d), _rows(t, d)], out_shape=[SDS((s, d), F32), SDS((s, d), BF16)],
               sem=("parallel",))(m, g_post, res, g_pre)


def _row_dot(a_ref, w_ref):
    r = w_ref.shape[1]
    out = None
    for j in range(N_CHIPS):
        part = jnp.dot(a_ref[:, j * r:(j + 1) * r], w_ref[j], preferred_element_type=F32)
        out = part if out is None else out + part
    return out


def _row_dot_specs(t, a, wg):
    return [_rows(t, a.shape[1]), _fixed(wg.shape)]


def out_proj_post_mix(name, a, wg, g_post, res, g_pre):
    s, d = res.shape
    t = _tile(s, ROW_TILE)

    def body(a_ref, w_ref, gp_ref, r_ref, gn_ref, m_ref, x_ref, h_ref):
        mv = _row_dot(a_ref, w_ref)
        m_ref[...] = mv
        xn = r_ref[...] + (mv * _rstd(mv)) * gp_ref[...]
        x_ref[...] = xn
        h_ref[...] = ((xn * _rstd(xn)) * gn_ref[...]).astype(BF16)

    return _pc(body, name=name, grid=(s // t,),
               in_specs=_row_dot_specs(t, a, wg) + [_fixed((1, d)), _rows(t, d), _fixed((1, d))],
               out_specs=[_rows(t, d)] * 3, out_shape=[SDS((s, d), F32), SDS((s, d), F32), SDS((s, d), BF16)],
               sem=("parallel",))(a, wg, g_post, res, g_pre)


def _col_dot(p_ref, w_ref):
    pv = p_ref[...].astype(BF16)
    return jnp.concatenate([jnp.dot(pv, w_ref[j], preferred_element_type=F32) for j in range(N_CHIPS)], axis=1)


def ple_gate_fwd_mid(name, a, wg, p, wg_proj, x2, g_next):
    s, d = x2.shape
    t = _tile(s, ROW_TILE)

    def body(a_ref, w_ref, p_ref, wp_ref, x_ref, g_ref, z_ref, pp_ref, xo_ref, h_ref):
        z = _row_dot(a_ref, w_ref)
        z_ref[...] = z
        pv = _col_dot(p_ref, wp_ref)
        pp_ref[...] = pv
        xn = x_ref[...] + pv * _sigmoid(z)
        xo_ref[...] = xn
        h_ref[...] = ((xn * _rstd(xn)) * g_ref[...]).astype(BF16)

    return _pc(body, name=name, grid=(s // t,),
               in_specs=_row_dot_specs(t, a, wg) + [_rows(t, p.shape[1]), _fixed(wg_proj.shape), _rows(t, d), _fixed((1, d))],
               out_specs=[_rows(t, d)] * 4, out_shape=[SDS((s, d), F32)] * 3 + [SDS((s, d), BF16)],
               sem=("parallel",))(a, wg, p, wg_proj, x2, g_next)


def ple_gate_fwd_loss(name, a, wg, p, wg_proj, x2, target):
    s, d = x2.shape
    t = _tile(s, ROW_TILE)

    def body(a_ref, w_ref, p_ref, wp_ref, x_ref, t_ref, dy_ref, dpp_ref, dz_ref, l_ref):
        gate = _sigmoid(_row_dot(a_ref, w_ref))
        pv = _col_dot(p_ref, wp_ref)
        err = (x_ref[...] + pv * gate) - t_ref[...]
        dy = err * (1.0 / d)
        dy_ref[...] = dy
        dpp_ref[...] = (dy * gate).astype(BF16)
        dz_ref[...] = ((dy * pv) * (gate * (1.0 - gate))).astype(BF16)
        part = jnp.sum(jnp.sum(err * err, axis=-1, keepdims=True), axis=0, keepdims=True) * (0.5 / d)

        @pl.when(pl.program_id(0) == 0)
        def _():
            l_ref[...] = part

        @pl.when(pl.program_id(0) > 0)
        def _():
            l_ref[...] += part

    return _pc(body, name=name, grid=(s // t,),
               in_specs=_row_dot_specs(t, a, wg) + [_rows(t, p.shape[1]), _fixed(wg_proj.shape), _rows(t, d), _rows(t, d)],
               out_specs=[_rows(t, d), _rows(t, d), _rows(t, d), _fixed((1, 1))],
               out_shape=[SDS((s, d), F32), SDS((s, d), BF16), SDS((s, d), BF16), SDS((1, 1), F32)],
               sem=("arbitrary",))(a, wg, p, wg_proj, x2, target)


ANY_SPEC = pl.BlockSpec(memory_space=pl.ANY)


def ple_bwd(name, dx3, z, pp, after):
    s, d = z.shape
    t = _tile(s, ROW_TILE_WIDE)

    def body(d_ref, z_ref, p_ref, after_ref, dpp_ref, dz_ref):
        gate = _sigmoid(z_ref[...])
        dv = d_ref[...]
        dpp_ref[...] = (dv * gate).astype(BF16)
        dz_ref[...] = ((dv * p_ref[...]) * (gate * (1.0 - gate))).astype(BF16)

    return _pc(body, name=name, grid=(s // t,), in_specs=[_rows(t, d)] * 3 + [ANY_SPEC], out_specs=[_rows(t, d)] * 2,
               out_shape=[SDS((s, d), BF16)] * 2, sem=("parallel",))(dx3, z, pp, after)


def norm_bwd(name, xin, gain, dy, dres):
    s, d = xin.shape
    t = _tile(s, ROW_TILE_WIDE)

    def body(x_ref, g_ref, dy_ref, r_ref, dx_ref, dg_ref):
        xv = x_ref[...]
        r = _rstd(xv)
        xh = xv * r
        dyv = dy_ref[...]
        dyg = dyv * g_ref[...]
        c = jnp.mean(dyg * xh, axis=-1, keepdims=True)
        dx_ref[...] = r * (dyg - xh * c) + r_ref[...]
        part = jnp.sum(dyv * xh, axis=0, keepdims=True)

        @pl.when(pl.program_id(0) == 0)
        def _():
            dg_ref[...] = part

        @pl.when(pl.program_id(0) > 0)
        def _():
            dg_ref[...] += part

    return _pc(body, name=name, grid=(s // t,), in_specs=[_rows(t, d), _fixed((1, d)), _rows(t, d), _rows(t, d)],
               out_specs=[_rows(t, d), _fixed((1, d))], out_shape=[SDS((s, d), F32), SDS((1, d), F32)],
               sem=("arbitrary",))(xin, gain, dy, dres)


def _lane(shape):
    return lax.broadcasted_iota(jnp.int32, shape, 1)


def _swap_halves(x):
    lo = (_lane(x.shape) % 64) < 32
    return jnp.where(lo, pltpu.roll(x, 96, 1), pltpu.roll(x, 32, 1))


def rope_tables(name, pos_col, inv_freq):
    s = pos_col.shape[0]
    t = _tile(s, ROW_TILE_WIDE)

    def body(p_ref, f_ref, c_ref, s_ref):
        ang = p_ref[...].astype(F32) * f_ref[...]
        lo = (_lane(ang.shape) % 64) < 32
        c_ref[...] = jnp.cos(ang)
        sn = jnp.sin(ang)
        s_ref[...] = jnp.where(lo, -sn, sn)

    return _pc(body, name=name, grid=(s // t,), in_specs=[_rows(t, 1), _fixed((1, LANES))],
               out_specs=[_rows(t, LANES)] * 2, out_shape=[SDS((s, LANES), F32)] * 2, sem=("parallel",))(pos_col, inv_freq)


def _pad_heads(chunk, lo_mask):
    zero = jnp.zeros_like(chunk)
    return jnp.where(lo_mask, chunk, zero), jnp.where(lo_mask, pltpu.roll(chunk, 64, 1), zero)


def rope_qkv(name, proj, cos, sin):
    s = proj.shape[0]
    t = _tile(s, ROW_TILE_WIDE)

    def body(q_ref, kv_ref, c_ref, s_ref, qp_ref, kp_ref, vp_ref):
        cs, sn = c_ref[...], s_ref[...]
        lo_mask = _lane(cs.shape) < 64

        def rot(x):
            return x * cs + _swap_halves(x) * sn

        for j in range(ATTN_WIDTH // LANES):
            a, b = _pad_heads(rot(q_ref[:, j * LANES:(j + 1) * LANES]), lo_mask)
            qp_ref[:, (2 * j) * LANES:(2 * j + 1) * LANES] = a.astype(BF16)
            qp_ref[:, (2 * j + 1) * LANES:(2 * j + 2) * LANES] = b.astype(BF16)
        for j in range(KV_WIDTH // LANES):
            a, b = _pad_heads(rot(kv_ref[:, j * LANES:(j + 1) * LANES]), lo_mask)
            kp_ref[:, (2 * j) * LANES:(2 * j + 1) * LANES] = a.astype(BF16)
            kp_ref[:, (2 * j + 1) * LANES:(2 * j + 2) * LANES] = b.astype(BF16)
            a, b = _pad_heads(kv_ref[:, KV_WIDTH + j * LANES:KV_WIDTH + (j + 1) * LANES], lo_mask)
            vp_ref[:, (2 * j) * LANES:(2 * j + 1) * LANES] = a.astype(BF16)
            vp_ref[:, (2 * j + 1) * LANES:(2 * j + 2) * LANES] = b.astype(BF16)

    return _pc(body, name=name, grid=(s // t,),
               in_specs=[_rows(t, ATTN_WIDTH, 0), _rows(t, 2 * KV_WIDTH, 2), _rows(t, LANES), _rows(t, LANES)],
               out_specs=[_rows(t, N_Q_HEADS * LANES), _rows(t, N_KV_HEADS * LANES), _rows(t, N_KV_HEADS * LANES)],
               out_shape=[SDS((s, N_Q_HEADS * LANES), BF16), SDS((s, N_KV_HEADS * LANES), BF16),
                          SDS((s, N_KV_HEADS * LANES), BF16)],
               sem=("parallel",))(proj, proj, cos, sin)


def _attn_mask(n):
    L = WINDOW
    qi = lax.broadcasted_iota(jnp.int32, (L, 2 * L), 0) + L
    ki = lax.broadcasted_iota(jnp.int32, (L, 2 * L), 1)
    rel = qi - ki
    return (rel >= 0) & (rel < WINDOW) & ((n > 0) | (ki >= L))


def _attn_scores(qh, kk, valid):
    sc = lax.dot_general(qh, kk, _NT, preferred_element_type=F32) * 0.125
    return jnp.where(valid, sc, MASK_VALUE)


def _attn_softmax(sc, sink):
    m = jnp.maximum(jnp.max(sc, axis=-1, keepdims=True), sink)
    e = jnp.exp(sc - m)
    es = jnp.exp(sink - m)
    den = jnp.sum(e, axis=-1, keepdims=True) + es
    return e / den, es / den


def _attn_specs(s):
    L = WINDOW
    cur = lambda n: (n, 0)
    prev = lambda n: (jnp.maximum(n - 1, 0), 0)
    kvw = N_KV_HEADS * LANES
    return [pl.BlockSpec((L, N_Q_HEADS * LANES), cur), pl.BlockSpec((L, kvw), cur), pl.BlockSpec((L, kvw), prev),
            pl.BlockSpec((L, kvw), cur), pl.BlockSpec((L, kvw), prev), pl.BlockSpec(memory_space=pltpu.SMEM)]


def attn_fwd(name, qp, kp, vp, sinks):
    s = qp.shape[0]
    L = WINDOW

    def body(q_ref, kc_ref, kp_ref, vc_ref, vp_ref, sk_ref, o_ref):
        valid = _attn_mask(pl.program_id(0))
        kks, vvs = [], []
        for kvh in range(N_KV_HEADS):
            cols = slice(kvh * LANES, (kvh + 1) * LANES)
            kks.append(jnp.concatenate([kp_ref[:, cols], kc_ref[:, cols]], axis=0))
            vvs.append(jnp.concatenate([vp_ref[:, cols], vc_ref[:, cols]], axis=0))
        scs = [_attn_scores(q_ref[:, h * LANES:(h + 1) * LANES], kks[h // Q_PER_KV], valid) for h in range(N_Q_HEADS)]
        pbs = [_attn_softmax(scs[h], sk_ref[h])[0].astype(BF16) for h in range(N_Q_HEADS)]
        outs = [jnp.dot(pbs[h], vvs[h // Q_PER_KV], preferred_element_type=F32) for h in range(N_Q_HEADS)]
        for j in range(ATTN_WIDTH // LANES):
            o_ref[:, j * LANES:(j + 1) * LANES] = outs[2 * j] + pltpu.roll(outs[2 * j + 1], 64, 1)

    return _pc(body, name=name, grid=(s // L,), in_specs=_attn_specs(s),
               out_specs=pl.BlockSpec((L, ATTN_WIDTH), lambda n: (n, 0)),
               out_shape=SDS((s, ATTN_WIDTH), F32), sem=("parallel",))(qp, kp, kp, vp, vp, sinks)


def attn_bwd(name, qp, kp, vp, sinks, dattn):
    s = qp.shape[0]
    L = WINDOW
    kvw = N_KV_HEADS * LANES

    def body(q_ref, kc_ref, kp_ref, vc_ref, vp_ref, sk_ref, do_ref, dq_ref, dkc_ref, dkp_ref, dvc_ref, dvp_ref, ds_ref):
        n = pl.program_id(0)
        valid = _attn_mask(n)
        lo_mask = _lane((L, LANES)) < 64
        lane1 = _lane((1, LANES))
        dsink = jnp.zeros((1, LANES), F32)
        heads = range(N_Q_HEADS)
        kks, vvs = [], []
        for kvh in range(N_KV_HEADS):
            cols = slice(kvh * LANES, (kvh + 1) * LANES)
            kks.append(jnp.concatenate([kp_ref[:, cols], kc_ref[:, cols]], axis=0))
            vvs.append(jnp.concatenate([vp_ref[:, cols], vc_ref[:, cols]], axis=0))
        qs, dos, scs, dps = [], [], [], []
        for h in heads:
            qs.append(q_ref[:, h * LANES:(h + 1) * LANES])
            chunk = do_ref[:, (h // 2) * LANES:(h // 2 + 1) * LANES]
            if h % 2:
                chunk = pltpu.roll(chunk, 64, 1)
            dos.append(jnp.where(lo_mask, chunk, 0.0).astype(BF16))
            scs.append(_attn_scores(qs[h], kks[h // Q_PER_KV], valid))
            dps.append(lax.dot_general(dos[h], vvs[h // Q_PER_KV], _NT, preferred_element_type=F32))
        pbs, dsbs = [], []
        for h in heads:
            p, ps = _attn_softmax(scs[h], sk_ref[h])
            delta = jnp.sum(p * dps[h], axis=-1, keepdims=True)
            dsbs.append(((p * (dps[h] - delta)) * 0.125).astype(BF16))
            pbs.append(p.astype(BF16))
            dsink = dsink + jnp.where(lane1 == h, -jnp.sum(ps * delta, axis=0, keepdims=True), 0.0)
        for kvh in range(N_KV_HEADS):
            cols = slice(kvh * LANES, (kvh + 1) * LANES)
            dkk = jnp.zeros((2 * L, LANES), F32)
            dvv = jnp.zeros((2 * L, LANES), F32)
            for h in range(kvh * Q_PER_KV, (kvh + 1) * Q_PER_KV):
                dq_ref[:, h * LANES:(h + 1) * LANES] = jnp.dot(dsbs[h], kks[kvh], preferred_element_type=F32)
                dkk = dkk + lax.dot_general(dsbs[h], qs[h], _TN, preferred_element_type=F32)
                dvv = dvv + lax.dot_general(pbs[h], dos[h], _TN, preferred_element_type=F32)
            dkp_ref[:, cols] = dkk[:L]
            dkc_ref[:, cols] = dkk[L:]
            dvp_ref[:, cols] = dvv[:L]
            dvc_ref[:, cols] = dvv[L:]

        @pl.when(n == 0)
        def _():
            ds_ref[...] = dsink

        @pl.when(n > 0)
        def _():
            ds_ref[...] += dsink

    blk = lambda w: pl.BlockSpec((L, w), lambda n: (n, 0))
    return _pc(body, name=name, grid=(s // L,), in_specs=_attn_specs(s) + [blk(ATTN_WIDTH)],
               out_specs=[blk(N_Q_HEADS * LANES), blk(kvw), blk(kvw), blk(kvw), blk(kvw), _fixed((1, LANES))],
               out_shape=[SDS((s, N_Q_HEADS * LANES), F32)] + [SDS((s, kvw), F32)] * 4 + [SDS((1, LANES), F32)],
               sem=("arbitrary",))(qp, kp, kp, vp, vp, sinks, dattn)


def rope_bwd(name, dqp, dkc, dkp, dvc, dvp, cos, sin):
    s = dqp.shape[0]
    L = WINDOW
    nb = s // L
    kvw = N_KV_HEADS * LANES

    def body(dq_ref, dkc_ref, dkp_ref, dvc_ref, dvp_ref, c_ref, s_ref, o_ref):
        cs, sn = c_ref[...], s_ref[...]
        more = (pl.program_id(0) < nb - 1).astype(F32)

        def unrot(x):
            return x * cs - _swap_halves(x) * sn

        def compact(ref, j, nxt=None):
            a = ref[:, (2 * j) * LANES:(2 * j + 1) * LANES]
            b = ref[:, (2 * j + 1) * LANES:(2 * j + 2) * LANES]
            if nxt is not None:
                a = a + more * nxt[:, (2 * j) * LANES:(2 * j + 1) * LANES]
                b = b + more * nxt[:, (2 * j + 1) * LANES:(2 * j + 2) * LANES]
            return a + pltpu.roll(b, 64, 1)

        for j in range(ATTN_WIDTH // LANES):
            o_ref[:, j * LANES:(j + 1) * LANES] = unrot(compact(dq_ref, j)).astype(BF16)
        for j in range(KV_WIDTH // LANES):
            o_ref[:, (COL_K + j) * LANES:(COL_K + j + 1) * LANES] = unrot(compact(dkc_ref, j, dkp_ref)).astype(BF16)
            o_ref[:, (COL_V + j) * LANES:(COL_V + j + 1) * LANES] = compact(dvc_ref, j, dvp_ref).astype(BF16)

    cur = lambda n: (n, 0)
    nxt = lambda n: (jnp.minimum(n + 1, nb - 1), 0)
    return _pc(body, name=name, grid=(nb,),
               in_specs=[pl.BlockSpec((L, N_Q_HEADS * LANES), cur), pl.BlockSpec((L, kvw), cur), pl.BlockSpec((L, kvw), nxt),
                         pl.BlockSpec((L, kvw), cur), pl.BlockSpec((L, kvw), nxt), pl.BlockSpec((L, LANES), cur),
                         pl.BlockSpec((L, LANES), cur)],
               out_specs=pl.BlockSpec((L, COL_HQ * LANES), cur), out_shape=SDS((s, COL_HQ * LANES), BF16),
               sem=("parallel",))(dqp, dkc, dkp, dvc, dvp, cos, sin)


def _split3(x):
    a = x.astype(BF16)
    r = x - a.astype(F32)
    b = r.astype(BF16)
    c = (r - b.astype(F32)).astype(BF16)
    return a, b, c


def _chunk_sum(x, upper):
    t = x.shape[0]
    ri = lax.broadcasted_iota(jnp.int32, (t, t), 0)
    ci = lax.broadcasted_iota(jnp.int32, (t, t), 1)
    same = (ri // HGRN_CHUNK) == (ci // HGRN_CHUNK)
    tri = (ci >= ri) if upper else (ci <= ri)
    m = jnp.where(same & tri, 1.0, 0.0).astype(BF16)
    out = None
    for part in _split3(x):
        y = jnp.dot(m, part, preferred_element_type=F32)
        out = y if out is None else out + y
    return out


def _lower_bound(l_ref, layer):
    lv = l_ref[...]
    e = jnp.exp(lv - jnp.max(lv, axis=0, keepdims=True))
    sm = e / jnp.sum(e, axis=0, keepdims=True)
    s0 = sm[0:1]
    return (s0 - s0) if layer == 0 else ((s0 + sm[1:2]) - s0)


def _hgrn_gates(hq_ref, hf_ref, lb):
    z = hf_ref[...]
    sg = _sigmoid(z)
    f = lb + (1.0 - lb) * sg
    kin = (1.0 - lb) * _sigmoid(-z)
    hq = hq_ref[...]
    sq = _sigmoid(hq)
    return sg, f, kin, hq, sq


def _shift_down(x, d):
    return x if d == 0 else pltpu.roll(x, d, 0)


def _shift_up(x, d):
    return x if d == 0 else pltpu.roll(x, x.shape[0] - d, 0)


CHUNKS_PER_BLOCK = LANES // HGRN_CHUNK


def _chunk_iotas():
    shape = (HGRN_CHUNK, LANES)
    return lax.broadcasted_iota(jnp.int32, shape, 0), lax.broadcasted_iota(jnp.int32, shape, 1)


def _chunk_rows(block, chunk):
    start = block * LANES + chunk * HGRN_CHUNK
    return slice(start, start + HGRN_CHUNK)


HGRN_HEADS_PER_STEP = 4
HGRN_STEP_WIDTH = HGRN_HEADS_PER_STEP * LANES


def _hgrn_specs(t, rev, nt):
    row = (lambda h, i: nt - 1 - i) if rev else (lambda h, i: i)
    col = lambda base: pl.BlockSpec((t, HGRN_STEP_WIDTH),
                                    lambda h, i, base=base: (row(h, i), base // HGRN_HEADS_PER_STEP + h))
    return col, row


def _head_views(refs, hh):
    return [r.at[:, pl.ds(hh * LANES, LANES)] for r in refs]


def hgrn_fwd(name, proj, lb_logits, layer):
    s = proj.shape[0]
    t = _tile(s, ROW_TILE)
    nt = s // t
    nc = t // HGRN_CHUNK
    col, row = _hgrn_specs(t, False, nt)

    def body(hq_ref, hf_ref, hi_ref, l_ref, o_ref, st_ref, state):
        @pl.when(pl.program_id(1) == 0)
        def _():
            state[...] = jnp.zeros_like(state)

        for hh in range(HGRN_HEADS_PER_STEP):
            head(*_head_views((hq_ref, hf_ref, hi_ref, l_ref, o_ref), hh), st_ref.at[:, hh], state.at[hh])

    def head(hq_ref, hf_ref, hi_ref, l_ref, o_ref, st_ref, state):
        lb = _lower_bound(l_ref, layer)
        sg, f, kin, hq, sq = _hgrn_gates(hq_ref, hf_ref, lb)
        q = hq * sq
        vb = hi_ref[...].astype(BF16)
        b = _chunk_sum(jnp.log(f), False)
        qe = (q * jnp.exp(b)).astype(BF16)
        trow, lane = _chunk_iotas()
        chunks = [(j, cc) for j in range(t // LANES) for cc in range(CHUNKS_PER_BLOCK)]
        decay, update = [], []
        for j, cc in chunks:
            rs = _chunk_rows(j, cc)
            bc = b[rs]
            bl = bc[HGRN_CHUNK - 1:HGRN_CHUNK, :]
            ke = (kin[rs] * jnp.exp(bl - bc)).astype(BF16)
            decay.append(jnp.exp(bl))
            update.append(lax.dot_general(vb[rs], ke, _TN, preferred_element_type=F32))
        st = state[...]
        for c in range(nc):
            st_ref[c] = st
            st = st * decay[c] + update[c]
        state[...] = st
        o_inter = [lax.dot_general(qe[c * HGRN_CHUNK:(c + 1) * HGRN_CHUNK], st_ref[c].astype(BF16), _NT,
                                   preferred_element_type=F32) for c in range(nc)]
        for j in range(t // LANES):
            blk = slice(j * LANES, (j + 1) * LANES)
            rows = []
            for cc in range(CHUNKS_PER_BLOCK):
                rs = _chunk_rows(j, cc)
                bc, qc, kc = b[rs], q[rs], kin[rs]
                here = trow + cc * HGRN_CHUNK
                am = jnp.where(lane == here, jnp.sum(qc * kc, axis=-1, keepdims=True), 0.0)
                for d in range(1, HGRN_CHUNK):
                    e = jnp.exp(jnp.where(trow >= d, bc - _shift_down(bc, d), MASK_VALUE))
                    a = jnp.sum((qc * _shift_down(kc, d)) * e, axis=-1, keepdims=True)
                    am = jnp.where(lane == here - d, a, am)
                rows.append(am)
            o_intra = jnp.dot(jnp.concatenate(rows, axis=0).astype(BF16), vb[blk], preferred_element_type=F32)
            for cc in range(CHUNKS_PER_BLOCK):
                rs = _chunk_rows(j, cc)
                o_ref[rs, :] = o_intra[cc * HGRN_CHUNK:(cc + 1) * HGRN_CHUNK] + o_inter[j * CHUNKS_PER_BLOCK + cc]

    hp = HGRN_HEADS_PER_STEP
    return _pc(body, name=name, grid=(HGRN_HEADS // hp, nt),
               in_specs=[col(COL_HQ), col(COL_HF), col(COL_HI), pl.BlockSpec((2, HGRN_STEP_WIDTH), lambda h, i: (0, h))],
               out_specs=[pl.BlockSpec((t, HGRN_STEP_WIDTH), lambda h, i: (i, h)),
                          pl.BlockSpec((nc, hp, LANES, LANES), lambda h, i: (i, h, 0, 0))],
               out_shape=[SDS((s, HGRN_WIDTH), F32), SDS((s // HGRN_CHUNK, HGRN_HEADS, LANES, LANES), F32)],
               scratch=[pltpu.VMEM((hp, LANES, LANES), F32)],
               sem=("parallel", "arbitrary"))(proj, proj, proj, lb_logits)


def hgrn_bwd(name, proj, lb_logits, layer, states, do):
    s = proj.shape[0]
    t = _tile(s, ROW_TILE)
    nt = s // t
    nc = t // HGRN_CHUNK
    col, row = _hgrn_specs(t, True, nt)

    def body(hq_ref, hf_ref, hi_ref, l_ref, st_ref, do_ref, dhq_ref, dhf_ref, dhi_ref, dlb_ref, dstate):
        @pl.when(pl.program_id(1) == 0)
        def _():
            dstate[...] = jnp.zeros_like(dstate)

        for hh in range(HGRN_HEADS_PER_STEP):
            hq_v, hf_v, hi_v, l_v, do_v, dhq_v, dhf_v, dhi_v, dlb_v = _head_views(
                (hq_ref, hf_ref, hi_ref, l_ref, do_ref, dhq_ref, dhf_ref, dhi_ref, dlb_ref), hh)
            head(hq_v, hf_v, hi_v, l_v, st_ref.at[:, hh], do_v, dhq_v, dhf_v, dhi_v, dlb_v, dstate.at[hh])

    def head(hq_ref, hf_ref, hi_ref, l_ref, st_ref, do_ref, dhq_ref, dhf_ref, dhi_ref, dlb_ref, dstate):
        first = pl.program_id(1) == 0
        lb = _lower_bound(l_ref, layer)
        sg, f, kin, hq, sq = _hgrn_gates(hq_ref, hf_ref, lb)
        q = hq * sq
        vb = hi_ref[...].astype(BF16)
        b = _chunk_sum(jnp.log(f), False)
        dob = do_ref[...].astype(BF16)
        eb = jnp.exp(b)
        qe = q * eb
        qeb = qe.astype(BF16)
        trow, lane = _chunk_iotas()
        last_row = trow == HGRN_CHUNK - 1

        decay, update = [None] * nc, [None] * nc
        for c in range(nc):
            rs = slice(c * HGRN_CHUNK, (c + 1) * HGRN_CHUNK)
            decay[c] = jnp.exp(b[(c + 1) * HGRN_CHUNK - 1:(c + 1) * HGRN_CHUNK, :])
            update[c] = lax.dot_general(dob[rs], qeb[rs], _TN, preferred_element_type=F32)
        dn_in = [None] * nc
        dn = dstate[...]
        for c in reversed(range(nc)):
            dn_in[c] = dn
            dn = dn * decay[c] + update[c]
        dstate[...] = dn

        dq_c, dk_c, dv_c, dbl_c = [None] * nc, [None] * nc, [None] * nc, [None] * nc
        for c in range(nc):
            rs = slice(c * HGRN_CHUNK, (c + 1) * HGRN_CHUNK)
            bc = b[rs]
            ekb = jnp.exp(bc[HGRN_CHUNK - 1:HGRN_CHUNK, :] - bc)
            ke = kin[rs] * ekb
            st = st_ref[c]
            dnb = dn_in[c].astype(BF16)
            dke = jnp.dot(vb[rs], dnb, preferred_element_type=F32)
            dq_c[c] = jnp.dot(dob[rs], st.astype(BF16), preferred_element_type=F32) * eb[rs]
            dk_c[c] = dke * ekb
            dv_c[c] = lax.dot_general(ke.astype(BF16), dnb, _NT, preferred_element_type=F32)
            dbl_c[c] = jnp.sum(dn_in[c] * st, axis=0, keepdims=True) * decay[c] + jnp.sum(dke * ke, axis=0, keepdims=True)

        db_c = [None] * nc
        for j in range(t // LANES):
            blk = slice(j * LANES, (j + 1) * LANES)
            damat = lax.dot_general(dob[blk], vb[blk], _NT, preferred_element_type=F32)
            rows = [None] * CHUNKS_PER_BLOCK
            for cc in range(CHUNKS_PER_BLOCK):
                c = j * CHUNKS_PER_BLOCK + cc
                rs = _chunk_rows(j, cc)
                bc, qc, kc = b[rs], q[rs], kin[rs]
                dam = damat[cc * HGRN_CHUNK:(cc + 1) * HGRN_CHUNK]
                here = trow + cc * HGRN_CHUNK
                on = lane == here
                da = jnp.sum(jnp.where(on, dam, 0.0), axis=-1, keepdims=True)
                am = jnp.where(on, jnp.sum(qc * kc, axis=-1, keepdims=True), 0.0)
                dq = dq_c[c] + da * kc
                dk = dk_c[c] + da * qc
                for d in range(1, HGRN_CHUNK):
                    on = lane == here - d
                    e = jnp.exp(jnp.where(trow >= d, bc - _shift_down(bc, d), MASK_VALUE))
                    kse = _shift_down(kc, d) * e
                    am = jnp.where(on, jnp.sum(qc * kse, axis=-1, keepdims=True), am)
                    da = jnp.sum(jnp.where(on, dam, 0.0), axis=-1, keepdims=True)
                    dq = dq + da * kse
                    dk = dk + _shift_up(da * (qc * e), d)
                rows[cc] = am
                dq_c[c], dk_c[c] = dq, dk
                db_c[c] = (qc * dq - kc * dk) + jnp.where(last_row, dbl_c[c], 0.0)
            dv_blk = lax.dot_general(jnp.concatenate(rows, axis=0).astype(BF16), dob[blk], _TN, preferred_element_type=F32)
            for cc in range(CHUNKS_PER_BLOCK):
                c = j * CHUNKS_PER_BLOCK + cc
                dv_c[c] = dv_c[c] + dv_blk[cc * HGRN_CHUNK:(cc + 1) * HGRN_CHUNK]
        dq = jnp.concatenate(dq_c, axis=0)
        dk = jnp.concatenate(dk_c, axis=0)
        dv = jnp.concatenate(dv_c, axis=0)
        db = jnp.concatenate(db_c, axis=0)
        dg = _chunk_sum(db, True)
        dhq_ref[...] = (dq * (sq * (1.0 + hq * (1.0 - sq)))).astype(BF16)
        dhi_ref[...] = dv.astype(BF16)
        dfk = dg / f - dk
        dhf_ref[...] = ((dfk * (1.0 - lb)) * (sg * (1.0 - sg))).astype(BF16)
        part = jnp.sum(dfk * (1.0 - sg), axis=0, keepdims=True)

        @pl.when(first)
        def _():
            dlb_ref[...] = part

        @pl.when(jnp.logical_not(first))
        def _():
            dlb_ref[...] += part

    hp = HGRN_HEADS_PER_STEP
    out_col = pl.BlockSpec((t, HGRN_STEP_WIDTH), lambda h, i: (nt - 1 - i, h))
    return _pc(body, name=name, grid=(HGRN_HEADS // hp, nt),
               in_specs=[col(COL_HQ), col(COL_HF), col(COL_HI), pl.BlockSpec((2, HGRN_STEP_WIDTH), lambda h, i: (0, h)),
                         pl.BlockSpec((nc, hp, LANES, LANES), lambda h, i: (nt - 1 - i, h, 0, 0)), out_col],
               out_specs=[out_col, out_col, out_col, pl.BlockSpec((1, HGRN_STEP_WIDTH), lambda h, i: (0, h))],
               out_shape=[SDS((s, HGRN_WIDTH), BF16)] * 3 + [SDS((1, HGRN_WIDTH), F32)],
               scratch=[pltpu.VMEM((hp, LANES, LANES), F32)],
               sem=("parallel", "arbitrary"))(proj, proj, proj, lb_logits, states, do)


def mix_out_fwd(name, attn, o, proj, g_attn, g_hgrn):
    s = attn.shape[0]
    t = _tile(s, ROW_TILE_WIDE)
    half = HGRN_WIDTH // 2

    def body(a_ref, o_ref, hg0_ref, hg1_ref, ga_ref, gh_ref, c_ref):
        av = a_ref[...]
        c_ref[:, :ATTN_WIDTH] = ((av * _rstd(av)) * ga_ref[...]).astype(BF16)
        for j in range(HGRN_HEADS):
            cols = slice(j * LANES, (j + 1) * LANES)
            ov = o_ref[:, cols]
            hg_ref, hcols = (hg0_ref, cols) if j < 4 else (hg1_ref, slice((j - 4) * LANES, (j - 3) * LANES))
            hg = hg_ref[:, hcols]
            on = (ov * _rstd(ov)) * gh_ref[:, cols]
            c_ref[:, ATTN_WIDTH + j * LANES:ATTN_WIDTH + (j + 1) * LANES] = (on * (hg * _sigmoid(hg))).astype(BF16)

    return _pc(body, name=name, grid=(s // t,),
               in_specs=[_rows(t, ATTN_WIDTH), _rows(t, HGRN_WIDTH), _rows(t, half, COL_HG // 4), _rows(t, half, COL_HG // 4 + 1),
                         _fixed((1, ATTN_WIDTH)), _fixed((1, HGRN_WIDTH))],
               out_specs=_rows(t, D_MODEL), out_shape=SDS((s, D_MODEL), BF16), sem=("parallel",))(attn, o, proj, proj, g_attn, g_hgrn)


def mix_out_bwd(name, dcat, attn, o, proj, g_attn, g_hgrn):
    s = attn.shape[0]
    t = _tile(s, ROW_TILE_WIDE)
    half = HGRN_WIDTH // 2

    def body(dc_ref, a_ref, o_ref, hg0_ref, hg1_ref, ga_ref, gh_ref, da_ref, do_ref, dhg_ref, dga_ref, dgh_ref, pa_s, ph_s):
        av = a_ref[...]
        r = _rstd(av)
        xh = av * r
        dyv = dc_ref[:, :ATTN_WIDTH]
        dyg = dyv * ga_ref[...]
        da_ref[...] = r * (dyg - xh * jnp.mean(dyg * xh, axis=-1, keepdims=True))
        pa_s[...] = jnp.sum(dyv * xh, axis=0, keepdims=True)
        for j in range(HGRN_HEADS):
            cols = slice(j * LANES, (j + 1) * LANES)
            ov = o_ref[:, cols]
            hg_ref, hcols = (hg0_ref, cols) if j < 4 else (hg1_ref, slice((j - 4) * LANES, (j - 3) * LANES))
            hg = hg_ref[:, hcols]
            sg = _sigmoid(hg)
            r = _rstd(ov)
            xh = ov * r
            gain = gh_ref[:, cols]
            dh = dc_ref[:, ATTN_WIDTH + j * LANES:ATTN_WIDTH + (j + 1) * LANES]
            dhg_ref[:, cols] = ((dh * (xh * gain)) * (sg * (1.0 + hg * (1.0 - sg)))).astype(BF16)
            dyv = dh * (hg * sg)
            dyg = dyv * gain
            do_ref[:, cols] = r * (dyg - xh * jnp.mean(dyg * xh, axis=-1, keepdims=True))
            ph_s[:, cols] = jnp.sum(dyv * xh, axis=0, keepdims=True)

        @pl.when(pl.program_id(0) == 0)
        def _():
            dga_ref[...] = pa_s[...]
            dgh_ref[...] = ph_s[...]

        @pl.when(pl.program_id(0) > 0)
        def _():
            dga_ref[...] += pa_s[...]
            dgh_ref[...] += ph_s[...]

    return _pc(body, name=name, grid=(s // t,),
               in_specs=[_rows(t, D_MODEL), _rows(t, ATTN_WIDTH), _rows(t, HGRN_WIDTH), _rows(t, half, COL_HG // 4),
                         _rows(t, half, COL_HG // 4 + 1), _fixed((1, ATTN_WIDTH)), _fixed((1, HGRN_WIDTH))],
               out_specs=[_rows(t, ATTN_WIDTH), _rows(t, HGRN_WIDTH), _rows(t, HGRN_WIDTH), _fixed((1, ATTN_WIDTH)),
                          _fixed((1, HGRN_WIDTH))],
               out_shape=[SDS((s, ATTN_WIDTH), F32), SDS((s, HGRN_WIDTH), F32), SDS((s, HGRN_WIDTH), BF16),
                          SDS((1, ATTN_WIDTH), F32), SDS((1, HGRN_WIDTH), F32)],
               scratch=[pltpu.VMEM((1, ATTN_WIDTH), F32), pltpu.VMEM((1, HGRN_WIDTH), F32)],
               sem=("arbitrary",))(dcat, attn, o, proj, proj, g_attn, g_hgrn)


BIG = (("w_in", 2048, 1408, 0), ("w_out", 512, 2048, 1), ("w_ffn_gate", 2048, 1408, 0), ("w_ffn_up", 2048, 1408, 0),
       ("w_ffn_down", 1408, 2048, 1), ("w_ple_gate", 512, 2048, 1), ("w_ple_proj", 256, 512, 0))
BIG_BY_NAME = {spec[0]: spec for spec in BIG}
HBM_SPEC = pl.BlockSpec(memory_space=pltpu.HBM)
SEM_SPEC = pl.BlockSpec(memory_space=pltpu.SEMAPHORE)
TOKEN_SHAPE = (8, LANES)


def _split_call(body, *, name, in_specs, out_specs, out_shape, aliases):
    return pl.pallas_call(body, name=name, in_specs=in_specs, out_specs=out_specs, out_shape=out_shape,
                          input_output_aliases=aliases,
                          compiler_params=pltpu.CompilerParams(has_side_effects=pltpu.SideEffectType.DATAFLOW_SIDE_EFFECTING))


def _in_hbm(arrays):
    return [pltpu.with_memory_space_constraint(a, pltpu.HBM) for a in arrays]


GROUP_STEPS = 8


def _step_rows(rows):
    assert rows % (GROUP_STEPS * 16) == 0
    return rows // GROUP_STEPS


def cast_to_slots(name, place, ws, layer, after):
    nt = len(ws)

    def body(place_ref, *refs):
        for t in range(nt):
            refs[nt + 1 + t][...] = refs[t][...].astype(BF16)

    in_specs, out_specs = [], []
    for w in ws:
        block = (None, _step_rows(w.shape[1]), w.shape[2])
        in_specs.append(pl.BlockSpec(block, lambda i, pr: (layer, i, 0)))
        out_specs.append(pl.BlockSpec(block, lambda i, pr: (pr[1], i, 0)))
    gs = pltpu.PrefetchScalarGridSpec(num_scalar_prefetch=1, grid=(GROUP_STEPS,), in_specs=in_specs + [ANY_SPEC],
                                      out_specs=out_specs)
    return _pc(body, name=name, grid_spec=gs, in_specs=None, out_specs=None,
               out_shape=[SDS((N_CHIPS,) + w.shape[1:], BF16) for w in ws], sem=("parallel",))(place, *ws, after)


def _place():
    x, y, c = lax.axis_index("x"), lax.axis_index("y"), lax.axis_index("c")
    chips = [(1 - x, y), (x, 1 - y), (1 - x, 1 - y)]
    return x, y, c, chips


def _half(ref, axis, c, rows, cols):
    if axis == 0:
        return ref.at[pl.ds(pl.multiple_of(c * (rows // 2), 16), rows // 2), :]
    return ref.at[:, pl.ds(pl.multiple_of(c * (cols // 2), LANES), cols // 2)]


def _gather_copies(specs, bufs, send, recv):
    x, y, c, chips = _place()
    cps = []
    for t, (_, rows, cols, axis) in enumerate(specs):
        mine = _half(bufs[t].at[2 * x + y], axis, c, rows, cols)
        for k, (cx, cy) in enumerate(chips):
            cps.append(pltpu.make_async_remote_copy(src_ref=mine, dst_ref=mine, send_sem=send.at[3 * t + k],
                                                    recv_sem=recv.at[3 * t + k], device_id=(cx, cy, c), device_id_type=MESH))
    return cps


def gather_start(name, specs, bufs, after):
    nt = len(bufs)
    n = 3 * nt

    def body(*refs):
        send, recv, token = refs[nt + 1], refs[nt + 2], refs[-1]
        for cp in _gather_copies(specs, refs[:nt], send, recv):
            cp.start()
        token[...] = jnp.zeros(TOKEN_SHAPE, F32)

    out = _split_call(
        body, name=name, in_specs=[HBM_SPEC] * nt + [ANY_SPEC],
        out_specs=(SEM_SPEC, SEM_SPEC) + (HBM_SPEC,) * nt + (pl.BlockSpec(memory_space=pltpu.VMEM),),
        out_shape=(pltpu.SemaphoreType.DMA((n,)), pltpu.SemaphoreType.DMA((n,)))
        + tuple(pltpu.HBM(b.shape, b.dtype) for b in bufs) + (SDS(TOKEN_SHAPE, F32),),
        aliases={t: 2 + t for t in range(nt)})(*_in_hbm(bufs), after)
    return out[0], out[1], list(out[2:2 + nt]), out[-1]


def gather_wait(name, specs, send, recv, bufs, after):
    nt = len(bufs)

    def body(*refs):
        for cp in _gather_copies(specs, refs[:nt], refs[nt], refs[nt + 1]):
            cp.wait_send()
            cp.wait_recv()

    out = _split_call(
        body, name=name, in_specs=[HBM_SPEC] * nt + [SEM_SPEC, SEM_SPEC, pl.BlockSpec(memory_space=pl.ANY)],
        out_specs=(HBM_SPEC,) * nt, out_shape=tuple(pltpu.HBM(b.shape, b.dtype) for b in bufs),
        aliases={t: t for t in range(nt)})(*bufs, send, recv, after)
    return list(out)


def gather_pass(name, specs, bufs):
    nt = len(bufs)

    def body(*refs):
        ins, outs = refs[:nt], refs[nt:2 * nt]
        send, recv = refs[2 * nt:]
        x, y, c, chips = _place()
        cps = []
        for t, (_, rows, cols, axis) in enumerate(specs):
            for k, (cx, cy) in enumerate(chips):
                cp = pltpu.make_async_remote_copy(
                    src_ref=_half(ins[t].at[2 * cx + cy], axis, c, rows, cols),
                    dst_ref=_half(outs[t].at[2 * cx + cy], axis, c, rows, cols),
                    send_sem=send.at[3 * t + k], recv_sem=recv.at[3 * t + k], device_id=(x, y, 1 - c), device_id_type=MESH)
                cp.start()
                cps.append(cp)
        for t, (_, rows, cols, axis) in enumerate(specs):
            for k, (cx, cy) in enumerate(chips):
                theirs = _half(outs[t].at[2 * cx + cy], axis, 1 - c, rows, cols)
                pltpu.make_async_remote_copy(src_ref=theirs, dst_ref=theirs, send_sem=send.at[3 * t + k],
                                             recv_sem=recv.at[3 * t + k], device_id=(x, y, 1 - c), device_id_type=MESH).wait_recv()
        for cp in cps:
            cp.wait_send()

    return _pc(body, name=name, in_specs=[HBM_SPEC] * nt, out_specs=[HBM_SPEC] * nt,
               out_shape=[SDS(b.shape, b.dtype) for b in bufs], scratch=[pltpu.SemaphoreType.DMA((3 * nt,))] * 2,
               input_output_aliases={t: t for t in range(nt)})(*bufs)


def reduce_to_sibling(name, grads):
    nt = len(grads)

    def body(*refs):
        srcs, dsts = refs[:nt], refs[nt:2 * nt]
        send, recv = refs[2 * nt:]
        x, y, c, _ = _place()
        cps = []
        for t in range(nt):
            cp = pltpu.make_async_remote_copy(src_ref=srcs[t].at[1 - c], dst_ref=dsts[t], send_sem=send.at[t],
                                              recv_sem=recv.at[t], device_id=(x, y, 1 - c), device_id_type=MESH)
            cp.start()
            cps.append(cp)
        for cp in cps:
            cp.wait()

    return _pc(body, name=name, in_specs=[HBM_SPEC] * nt, out_specs=[HBM_SPEC] * nt,
               out_shape=[SDS(g.shape[1:], g.dtype) for g in grads],
               scratch=[pltpu.SemaphoreType.DMA((nt,))] * 2)(*grads)


def _sibling_copies(grads, lands, send, recv):
    x, y, c, _ = _place()
    return [pltpu.make_async_remote_copy(src_ref=grads[t].at[1 - c], dst_ref=lands[t], send_sem=send.at[t], recv_sem=recv.at[t],
                                         device_id=(x, y, 1 - c), device_id_type=MESH) for t in range(len(grads))]


def sibling_start(name, grads):
    nt = len(grads)
    lands = [lax.empty(g.shape[1:], g.dtype) for g in grads]

    def body(*refs):
        send, recv, token = refs[2 * nt], refs[2 * nt + 1], refs[-1]
        for cp in _sibling_copies(refs[:nt], refs[nt:2 * nt], send, recv):
            cp.start()
        token[...] = jnp.zeros(TOKEN_SHAPE, F32)

    both = list(grads) + lands
    out = _split_call(
        body, name=name, in_specs=[HBM_SPEC] * (2 * nt),
        out_specs=(SEM_SPEC, SEM_SPEC) + (HBM_SPEC,) * (2 * nt) + (pl.BlockSpec(memory_space=pltpu.VMEM),),
        out_shape=(pltpu.SemaphoreType.DMA((nt,)), pltpu.SemaphoreType.DMA((nt,)))
        + tuple(pltpu.HBM(b.shape, b.dtype) for b in both) + (SDS(TOKEN_SHAPE, F32),),
        aliases={t: 2 + t for t in range(2 * nt)})(*_in_hbm(both))
    return out[0], out[1], list(out[2:2 + nt]), list(out[2 + nt:2 + 2 * nt]), out[-1]


def sibling_wait(name, send, recv, grads, lands, after):
    nt = len(grads)

    def body(*refs):
        for cp in _sibling_copies(refs[:nt], refs[nt:2 * nt], refs[2 * nt], refs[2 * nt + 1]):
            cp.wait_send()
            cp.wait_recv()

    both = list(grads) + list(lands)
    out = _split_call(
        body, name=name, in_specs=[HBM_SPEC] * (2 * nt) + [SEM_SPEC, SEM_SPEC, pl.BlockSpec(memory_space=pl.ANY)],
        out_specs=(HBM_SPEC,) * (2 * nt), out_shape=tuple(pltpu.HBM(b.shape, b.dtype) for b in both),
        aliases={t: t for t in range(2 * nt)})(*both, send, recv, after)
    return list(out[:nt]), list(out[nt:])


def add_halves(name, place, grads, gots):
    nt = len(grads)

    def body(place_ref, *refs):
        for t in range(nt):
            val = (refs[t][...].astype(F32) + refs[nt + t][...].astype(F32)).astype(BF16)
            refs[2 * nt + t][...] = val

            @pl.when(pl.program_id(1) == place_ref[1])
            def _():
                refs[3 * nt + t][...] = val

    g_specs, o_specs, part_specs, slot_specs = [], [], [], []
    for g in grads:
        _, n, r, c = g.shape
        tr = _step_rows(r)
        g_specs.append(pl.BlockSpec((None, None, tr, c), lambda i, j, pr: (pr[0], j, i, 0)))
        o_specs.append(pl.BlockSpec((None, tr, c), lambda i, j, pr: (j, i, 0)))
        part_specs.append(pl.BlockSpec((None, tr, c), lambda i, j, pr: (j, i, 0)))
        slot_specs.append(pl.BlockSpec((None, tr, c), lambda i, j, pr: (pr[1], i, 0)))
    gs = pltpu.PrefetchScalarGridSpec(num_scalar_prefetch=1, grid=(GROUP_STEPS, N_CHIPS), in_specs=g_specs + o_specs,
                                      out_specs=part_specs + slot_specs)
    out = _pc(body, name=name, grid_spec=gs, in_specs=None, out_specs=None,
              out_shape=[SDS(g.shape[1:], BF16) for g in grads] * 2, sem=("parallel", "arbitrary"))(place, *grads, *gots)
    return list(out[:nt]), list(out[nt:])


def _chips_copies(parts, slots, send, recv):
    x, y, c, chips = _place()
    cps = []
    for t in range(len(parts)):
        for k, (cx, cy) in enumerate(chips):
            cps.append(pltpu.make_async_remote_copy(src_ref=parts[t].at[2 * cx + cy], dst_ref=slots[t].at[2 * x + y],
                                                    send_sem=send.at[3 * t + k], recv_sem=recv.at[3 * t + k],
                                                    device_id=(cx, cy, c), device_id_type=MESH))
    return cps


def chips_start(name, parts, slots):
    nt = len(parts)
    n = 3 * nt

    def body(*refs):
        send, recv, token = refs[2 * nt], refs[2 * nt + 1], refs[-1]
        for cp in _chips_copies(refs[:nt], refs[nt:2 * nt], send, recv):
            cp.start()
        token[...] = jnp.zeros(TOKEN_SHAPE, F32)

    both = list(parts) + list(slots)
    out = _split_call(
        body, name=name, in_specs=[HBM_SPEC] * (2 * nt),
        out_specs=(SEM_SPEC, SEM_SPEC) + (HBM_SPEC,) * (2 * nt) + (pl.BlockSpec(memory_space=pltpu.VMEM),),
        out_shape=(pltpu.SemaphoreType.DMA((n,)), pltpu.SemaphoreType.DMA((n,)))
        + tuple(pltpu.HBM(b.shape, b.dtype) for b in both) + (SDS(TOKEN_SHAPE, F32),),
        aliases={t: 2 + t for t in range(2 * nt)})(*_in_hbm(both))
    return out[0], out[1], list(out[2:2 + nt]), list(out[2 + nt:2 + 2 * nt]), out[-1]


def chips_wait(name, send, recv, parts, slots, after):
    nt = len(parts)

    def body(*refs):
        for cp in _chips_copies(refs[:nt], refs[nt:2 * nt], refs[2 * nt], refs[2 * nt + 1]):
            cp.wait_send()
            cp.wait_recv()

    both = list(parts) + list(slots)
    out = _split_call(
        body, name=name, in_specs=[HBM_SPEC] * (2 * nt) + [SEM_SPEC, SEM_SPEC, pl.BlockSpec(memory_space=pl.ANY)],
        out_specs=(HBM_SPEC,) * (2 * nt), out_shape=tuple(pltpu.HBM(b.shape, b.dtype) for b in both),
        aliases={t: t for t in range(2 * nt)})(*both, send, recv, after)
    return list(out[nt:])


def sum_chips(name, place, slots):
    nt = len(slots)

    def body(place_ref, *refs):
        for t in range(nt):
            s_ref = refs[t]
            acc = s_ref[0].astype(F32)
            for k in range(1, N_CHIPS):
                acc = acc + s_ref[k].astype(F32)
            refs[nt + t][...] = acc

    in_specs, out_specs = [], []
    for sl in slots:
        n, r, c = sl.shape
        tr = _step_rows(r)
        in_specs.append(pl.BlockSpec((n, tr, c), lambda i, pr: (0, i, 0)))
        out_specs.append(pl.BlockSpec((None, tr, c), lambda i, pr: (pr[0], i, 0)))
    gs = pltpu.PrefetchScalarGridSpec(num_scalar_prefetch=1, grid=(GROUP_STEPS,), in_specs=in_specs, out_specs=out_specs)
    return list(_pc(body, name=name, grid_spec=gs, in_specs=None, out_specs=None,
                    out_shape=[SDS((2,) + sl.shape[1:], F32) for sl in slots], sem=("parallel",))(place, *slots))


def share_with_sibling(name, bufs):
    nt = len(bufs)

    def body(*refs):
        ins, outs = refs[:nt], refs[nt:2 * nt]
        send, recv = refs[2 * nt:]
        x, y, c, _ = _place()
        cps = []
        for t in range(nt):
            cp = pltpu.make_async_remote_copy(src_ref=ins[t].at[c], dst_ref=outs[t].at[c], send_sem=send.at[t], recv_sem=recv.at[t],
                                              device_id=(x, y, 1 - c), device_id_type=MESH)
            cp.start()
            cps.append(cp)
        for t in range(nt):
            theirs = outs[t].at[1 - c]
            pltpu.make_async_remote_copy(src_ref=theirs, dst_ref=theirs, send_sem=send.at[t], recv_sem=recv.at[t],
                                         device_id=(x, y, 1 - c), device_id_type=MESH).wait_recv()
        for cp in cps:
            cp.wait_send()

    return _pc(body, name=name, in_specs=[HBM_SPEC] * nt, out_specs=[HBM_SPEC] * nt,
               out_shape=[SDS(b.shape, F32) for b in bufs], scratch=[pltpu.SemaphoreType.DMA((nt,))] * 2,
               input_output_aliases={t: t for t in range(nt)})(*bufs)


def _adamw(w, g, m, v):
    m = ADAM_B1 * m + (1.0 - ADAM_B1) * g
    v = ADAM_B2 * v + (1.0 - ADAM_B2) * (g * g)
    m_hat = m / (1.0 - ADAM_B1 ** ADAM_STEP)
    v_hat = v / (1.0 - ADAM_B2 ** ADAM_STEP)
    delta = -ADAM_LR * (m_hat / (jnp.sqrt(v_hat) + ADAM_EPS) + ADAM_WD * w)
    return delta, m, v


def adamw_big(name, w, m, v, g0, g1, axis):
    _, r, c = w.shape
    _, rh, ch = g0.shape
    tr = _tile(rh, 256)
    nb = rh // tr
    if axis == 0:
        wspec = pl.BlockSpec((None, tr, ch), lambda l, h, i: (l, h * nb + i, 0))
    else:
        wspec = pl.BlockSpec((None, tr, ch), lambda l, h, i: (l, i, h))
    g0spec = pl.BlockSpec((None, tr, ch), lambda l, h, i: (h * (1 - l), i * (1 - l), 0))
    g1spec = pl.BlockSpec((None, tr, ch), lambda l, h, i: (h * l, i * l, 0))

    def body(w_ref, m_ref, v_ref, g0_ref, g1_ref, go_ref, d_ref, mo_ref, vo_ref):
        def run(g_ref):
            g = g_ref[...]
            delta, mn, vn = _adamw(w_ref[...], g, m_ref[...], v_ref[...])
            go_ref[...] = g
            d_ref[...] = delta
            mo_ref[...] = mn
            vo_ref[...] = vn

        @pl.when(pl.program_id(0) == 0)
        def _():
            run(g0_ref)

        @pl.when(pl.program_id(0) == 1)
        def _():
            run(g1_ref)

    return _pc(body, name=name, grid=(2, 2, nb), in_specs=[wspec, wspec, wspec, g0spec, g1spec], out_specs=[wspec] * 4,
               out_shape=[SDS(w.shape, F32)] * 4, sem=("parallel", "parallel", "parallel"))(w, m, v, g0, g1)


SMALL = (("pre_mix_gain", 2048), ("post_mix_gain", 2048), ("pre_ffn_gain", 2048), ("post_ffn_gain", 2048), ("ple_gain", 2048),
         ("attn_out_gain", 1024), ("hgrn_out_gain", 1024), ("hgrn_lb_logits", 1024), ("attn_sinks", 128))
SMALL_ROWS = sum(2 * w // LANES for _, w in SMALL)
SMALL_PAD = -(-SMALL_ROWS // 8) * 8
LB_ROW = sum(2 * w // LANES for _, w in SMALL[:7])


def _pack_small(parts):
    rows = []
    for nm, w in SMALL:
        a = parts[nm].astype(F32)
        if a.shape[1] != w:
            a = jnp.pad(a, ((0, 0), (0, w - a.shape[1])))
        rows.append(a.reshape(2 * w // LANES, LANES))
    rows.append(jnp.zeros((SMALL_PAD - SMALL_ROWS, LANES), F32))
    return jnp.concatenate(rows, axis=0)


def _unpack_small(packed, widths):
    out, r = {}, 0
    for nm, w in SMALL:
        n = 2 * w // LANES
        out[nm] = packed[r:r + n].reshape(2, w)[:, :widths[nm]]
        r += n
    return out


def allreduce_small(name, packed):
    rows = packed.shape[0]

    def body(x_ref, o_ref, buf, send, recv, own_sem):
        x, y, c, _ = _place()
        me = 4 * x + 2 * y + c
        own = pltpu.make_async_copy(x_ref, buf.at[me], own_sem)
        own.start()
        cps = []
        for k in range(1, 8):
            px, py, pc = x ^ (k >> 2), y ^ ((k >> 1) & 1), c ^ (k & 1)
            cp = pltpu.make_async_remote_copy(src_ref=x_ref, dst_ref=buf.at[me], send_sem=send.at[k - 1], recv_sem=recv.at[k - 1],
                                              device_id=(px, py, pc), device_id_type=MESH)
            cp.start()
            cps.append(cp)
        for k in range(1, 8):
            px, py, pc = x ^ (k >> 2), y ^ ((k >> 1) & 1), c ^ (k & 1)
            slot = buf.at[4 * px + 2 * py + pc]
            pltpu.make_async_remote_copy(src_ref=slot, dst_ref=slot, send_sem=send.at[k - 1], recv_sem=recv.at[k - 1],
                                         device_id=(px, py, pc), device_id_type=MESH).wait_recv()
        for cp in cps:
            cp.wait_send()
        own.wait()
        acc = buf[0]
        for k in range(1, 8):
            acc = acc + buf[k]
        o_ref[...] = acc

    vm = pl.BlockSpec(memory_space=pltpu.VMEM)
    return _pc(body, name=name, in_specs=[vm], out_specs=vm, out_shape=SDS((rows, LANES), F32),
               scratch=[pltpu.VMEM((8, rows, LANES), F32), pltpu.SemaphoreType.DMA((7,)), pltpu.SemaphoreType.DMA((7,)),
                        pltpu.SemaphoreType.DMA])(packed)


def adamw_small(name, w, m, v, g):
    rows = w.shape[0]
    n = HGRN_WIDTH // LANES

    def body(w_ref, m_ref, v_ref, g_ref, go_ref, d_ref, mo_ref, vo_ref):
        go_ref[...] = g_ref[...]
        l0 = w_ref[LB_ROW:LB_ROW + n, :]
        l1 = w_ref[LB_ROW + n:LB_ROW + 2 * n, :]
        mx = jnp.maximum(l0, l1)
        e0, e1 = jnp.exp(l0 - mx), jnp.exp(l1 - mx)
        s0, s1 = e0 / (e0 + e1), e1 / (e0 + e1)
        dlb1 = g_ref[LB_ROW + n:LB_ROW + 2 * n, :]
        inner = s1 * dlb1
        go_ref[LB_ROW:LB_ROW + n, :] = s0 * (0.0 - inner)
        go_ref[LB_ROW + n:LB_ROW + 2 * n, :] = s1 * (dlb1 - inner)
        delta, mn, vn = _adamw(w_ref[...], go_ref[...], m_ref[...], v_ref[...])
        d_ref[...] = delta
        mo_ref[...] = mn
        vo_ref[...] = vn

    vm = pl.BlockSpec(memory_space=pltpu.VMEM)
    return _pc(body, name=name, in_specs=[vm] * 4, out_specs=[vm] * 4, out_shape=[SDS((rows, LANES), F32)] * 4)(w, m, v, g)


def _layer_fwd(l, x, h1, p_l, w_in_g, rest_of_weights, gains, cos, sin, sinks, lb_logits, g_next, target):
    n = f"l{l}_"
    proj = mm_col(n + "in_proj", h1, w_in_g)
    qp, kp, vp = rope_qkv(n + "rope_qkv", proj, cos, sin)
    attn = attn_fwd(n + "attn_fwd", qp, kp, vp, sinks)
    o, states = hgrn_fwd(n + "hgrn_fwd", proj, lb_logits, l)
    cat = mix_out_fwd(n + "mix_out_fwd", attn, o, proj, gains["attn_out_gain"], gains["hgrn_out_gain"])
    rest, token = rest_of_weights("mix", cat)
    wts = dict(rest, w_in=w_in_g)
    if token is not None:
        gains = _with_token(gains, "post_mix_gain", token)
    m, x1, h2 = out_proj_post_mix(n + "out_proj_post_mix", cat, wts["w_out"], gains["post_mix_gain"], x, gains["pre_ffn_gain"])
    g, u, a = ffn_gate_up(n + "ffn_gate_up", h2, wts["w_ffn_gate"], wts["w_ffn_up"])
    wts.update(rest_of_weights("ffn", a)[0])
    f = mm_row(n + "ffn_down", a, wts["w_ffn_down"])
    x2, h3 = post_pre_norm(n + "post_ffn", f, gains["post_ffn_gain"], x1, gains["ple_gain"])
    saved = dict(x=x, h1=h1, proj=proj, qp=qp, kp=kp, vp=vp, attn=attn, o=o, states=states, cat=cat, m=m, x1=x1, h2=h2,
                 g=g, u=u, a=a, f=f, x2=x2, h3=h3, p=p_l)
    if target is None:
        z, pp, *out = ple_gate_fwd_mid(n + "ple_gate_fwd", h3, wts["w_ple_gate"], p_l, wts["w_ple_proj"], x2, g_next)
        saved.update(z=z, pp=pp)
    else:
        dy, dpp, dz, loss = ple_gate_fwd_loss(n + "ple_gate_loss", h3, wts["w_ple_gate"], p_l, wts["w_ple_proj"], x2, target)
        out = [dy, loss]
        saved.update(dpp=dpp, dz=dz)
    return out, saved, wts


EARLY = ("w_ple_gate", "w_ple_proj", "w_ffn_down", "w_ffn_gate", "w_ffn_up")
LATE = ("w_out", "w_in")


def _layer_bwd_ffn(l, dx3, sv, wts, gains, after=None, hook=None):
    n = f"l{l}_"
    dpp, dz = (sv["dpp"], sv["dz"]) if "dz" in sv else ple_bwd(n + "ple_bwd", dx3, sv["z"], sv["pp"],
                                                                 dx3 if after is None else after)
    dh3 = mm_row_t(n + "ple_gate_dx", dz, wts["w_ple_gate"])
    if hook is not None:
        gains = _with_token(gains, "ple_gain", hook(dh3))
    dx2, df, d_ple_gain, d_post_ffn = norm_bwd_pair(n + "ple_post_ffn_bwd", sv["x2"], gains["ple_gain"], dh3, dx3, sv["f"],
                                                    gains["post_ffn_gain"])
    dg, du = ffn_down_bwd(n + "ffn_down_bwd", df, wts["w_ffn_down"], sv["g"], sv["u"])
    big = dict(
        w_ple_gate=mm_wg_row(n + "ple_gate_dw", sv["h3"], dz),
        w_ple_proj=mm_wg_col(n + "ple_proj_dw", sv["p"], dpp),
        w_ffn_down=mm_wg_row(n + "ffn_down_dw", sv["a"], df),
        w_ffn_gate=mm_wg_col(n + "ffn_gate_dw", sv["h2"], dg),
        w_ffn_up=mm_wg_col(n + "ffn_up_dw", sv["h2"], du),
    )
    return dict(dx2=dx2, dg=dg, du=du), big, dict(ple_gain=d_ple_gain, post_ffn_gain=d_post_ffn)


def _layer_bwd_mix(l, st, sv, wts, gains, cos, sin, sinks, lb_logits, after=None, hook=None):
    n = f"l{l}_"
    dh2 = mm_col_t(n + "ffn_gate_dx", st["dg"], wts["w_ffn_gate"], after=after)
    dh2 = mm_col_t(n + "ffn_up_dx", st["du"], wts["w_ffn_up"], add=dh2)
    if hook is not None:
        gains = _with_token(gains, "pre_ffn_gain", hook(dh2))
    dx1, dm, d_pre_ffn, d_post_mix = norm_bwd_pair(n + "pre_ffn_post_mix_bwd", sv["x1"], gains["pre_ffn_gain"], dh2, st["dx2"],
                                                   sv["m"], gains["post_mix_gain"])
    dcat = mm_row_t(n + "out_proj_dx", dm, wts["w_out"])
    dattn, do, dhg, d_attn_gain, d_hgrn_gain = mix_out_bwd(n + "mix_out_bwd", dcat, sv["attn"], sv["o"], sv["proj"],
                                                            gains["attn_out_gain"], gains["hgrn_out_gain"])
    dqp, dkc, dkp, dvc, dvp, dsinks = attn_bwd(n + "attn_bwd", sv["qp"], sv["kp"], sv["vp"], sinks, dattn)
    dqkv = rope_bwd(n + "rope_bwd", dqp, dkc, dkp, dvc, dvp, cos, sin)
    dhq, dhf, dhi, dlb = hgrn_bwd(n + "hgrn_bwd", sv["proj"], lb_logits, l, sv["states"], do)
    dproj, dh1 = in_proj_bwd(n + "in_proj_dx", [dqkv, dhq, dhf, dhi, dhg], wts["w_in"])
    dx, d_pre_mix = norm_bwd(n + "pre_mix_bwd", sv["x"], gains["pre_mix_gain"], dh1, dx1)
    big = dict(w_out=mm_wg_row(n + "out_proj_dw", sv["cat"], dm), w_in=mm_wg_col(n + "in_proj_dw", sv["h1"], dproj))
    small = dict(pre_mix_gain=d_pre_mix, post_mix_gain=d_post_mix, pre_ffn_gain=d_pre_ffn, attn_out_gain=d_attn_gain,
                 hgrn_out_gain=d_hgrn_gain, hgrn_lb_logits=dlb, attn_sinks=dsinks)
    return dx, big, small


def _reduce_start(tag, names, big, place):
    grads = [big[nm] for nm in names]
    got = reduce_to_sibling(tag + "_reduce_to_sibling", grads)
    parts, slots = add_halves(tag + "_add", place, grads, got)
    return chips_start(tag + "_chips_start", parts, slots)


def _reduce_begin(tag, names, big):
    return sibling_start(tag + "_sibling_start", [big[nm] for nm in names])


def _reduce_chips(tag, names, begun, place, after):
    send, recv, grads, lands, _ = begun
    grads, got = sibling_wait(tag + "_sibling_wait", send, recv, grads, lands, after)
    parts, slots = add_halves(tag + "_add", place, grads, got)
    return chips_start(tag + "_chips_start", parts, slots)


def _reduce_finish(tag, names, started, place, after):
    send, recv, parts, slots, _ = started
    slots = chips_wait(tag + "_chips_wait", send, recv, parts, slots, after)
    bufs = sum_chips(tag + "_sum", place, slots)
    return dict(zip(names, share_with_sibling(tag + "_share_with_sibling", bufs)))


def _reduce_finish_both(first, second, place, after):
    slots = []
    for tag, _, (send, recv, parts, slots_t, _) in (first, second):
        slots += chips_wait(tag + "_chips_wait", send, recv, parts, slots_t, after)
    tag = first[0] + "_" + second[0]
    shared = share_with_sibling(tag + "_share_with_sibling", sum_chips(tag + "_sum", place, slots))
    n1 = len(first[1])
    return dict(zip(first[1], shared[:n1])), dict(zip(second[1], shared[n1:]))


def _with_token(gains, name, token):
    out = dict(gains)
    out[name] = gains[name] + token[0, 0]
    return out


def kernel(x, p, positions, w_in, attn_sinks, hgrn_lb_logits, attn_out_gain, hgrn_out_gain, w_out, pre_mix_gain, post_mix_gain, pre_ffn_gain, post_ffn_gain, w_ffn_gate, w_ffn_up, w_ffn_down, ple_gain, w_ple_gate, w_ple_proj, loss_target, m_w_in, m_attn_sinks, m_hgrn_lb_logits, m_attn_out_gain, m_hgrn_out_gain, m_w_out, m_pre_mix_gain, m_post_mix_gain, m_pre_ffn_gain, m_post_ffn_gain, m_w_ffn_gate, m_w_ffn_up, m_w_ffn_down, m_ple_gain, m_w_ple_gate, m_w_ple_proj, v_w_in, v_attn_sinks, v_hgrn_lb_logits, v_attn_out_gain, v_hgrn_out_gain, v_w_out, v_pre_mix_gain, v_post_mix_gain, v_pre_ffn_gain, v_post_ffn_gain, v_w_ffn_gate, v_w_ffn_up, v_w_ffn_down, v_ple_gain, v_w_ple_gate, v_w_ple_proj):
    given = dict(locals())
    depth = 2
    place = jnp.stack([lax.axis_index("c"), 2 * lax.axis_index("x") + lax.axis_index("y")]).astype(jnp.int32)
    xs = x[0]
    tgt = loss_target[0]
    pos_col = positions.reshape(-1, 1)
    half = 32
    inv_freq = ROPE_THETA ** (-jnp.arange(half, dtype=F32) / half)
    inv_freq = jnp.tile(inv_freq, 4).reshape(1, LANES)
    gains = [{nm: given[nm][l:l + 1] for nm, _ in SMALL[:7]} for l in range(depth)]
    names = [nm for nm, *_ in BIG]
    first, others = names[:1], names[1:]

    def specs(nms):
        return [BIG_BY_NAME[nm] for nm in nms]

    def cast(tag, l, nms, after):
        return cast_to_slots(tag + "_cast", place, [given[nm] for nm in nms], l, after)

    def finish_gather(tag, nms, started, after):
        bufs = gather_wait(tag + "_gather_wait", specs(nms), started[0], started[1], started[2], after)
        return dict(zip(nms, gather_pass(tag + "_gather_pass", specs(nms), bufs)))

    mid, last = names[1:4], names[4:]
    g0a = gather_start("l0a_gather_start", specs(first), cast("l0a", 0, first, place), place)
    g0b = gather_start("l0b_gather_start", specs(mid), cast("l0b", 0, mid, g0a[3]), g0a[3])
    g0c = gather_start("l0c_gather_start", specs(last), cast("l0c", 0, last, g0b[3]), g0b[3])
    started = {}

    def rest_of_layer0(stage, after):
        if stage == "ffn":
            return finish_gather("l0c", last, g0c, after), None
        got = finish_gather("l0b", mid, g0b, after)
        started["l1a"] = gather_start("l1a_gather_start", specs(first), l1_shards[:1], got["w_out"])
        started["l1b"] = gather_start("l1b_gather_start", specs(others), l1_shards[1:], started["l1a"][3])
        return got, started["l1b"][3]

    def rest_of_layer1(stage, after):
        return (finish_gather("l1b", others, started["l1b"], after) if stage == "mix" else {}), None

    cos, sin = rope_tables("rope_tables", pos_col, inv_freq)
    h1 = pre_norm("l0_pre_mix", xs, _with_token(gains[0], "pre_mix_gain", g0c[3])["pre_mix_gain"])
    l1_shards = cast("l1", 1, names, h1)
    w_in0 = finish_gather("l0a", first, g0a, l1_shards[0])["w_in"]
    (x_mid, h1_next), sv0, wts0 = _layer_fwd(0, xs, h1, p[0, 0], w_in0, rest_of_layer0, gains[0], cos, sin, attn_sinks[0],
                                             hgrn_lb_logits, gains[1]["pre_mix_gain"], None)
    w_in1 = finish_gather("l1a", first, started["l1a"], x_mid)["w_in"]
    (dy, loss_part), sv1, wts1 = _layer_fwd(1, x_mid, h1_next, p[1, 0], w_in1, rest_of_layer1, gains[1], cos, sin,
                                            attn_sinks[1], hgrn_lb_logits, None, tgt)

    st1, early1, small1 = _layer_bwd_ffn(1, dy, sv1, wts1, gains[1])
    dx_mid, late1, small1b = _layer_bwd_mix(1, st1, sv1, wts1, gains[1], cos, sin, attn_sinks[1], hgrn_lb_logits)
    big1, small1 = {**early1, **late1}, {**small1, **small1b}
    chips = {}

    def chips_after(tag, nms, begun):
        def hook(x):
            chips[tag] = _reduce_chips(tag, nms, begun, place, x)
            return chips[tag][4]
        return hook

    b1 = _reduce_begin("l1", names, big1)
    st0, early0, small0 = _layer_bwd_ffn(0, dx_mid, sv0, wts0, gains[0], after=b1[4], hook=chips_after("l1", names, b1))
    b0e = _reduce_begin("l0e", EARLY, early0)
    dx0, late0, small0b = _layer_bwd_mix(0, st0, sv0, wts0, gains[0], cos, sin, attn_sinks[0], hgrn_lb_logits, after=b0e[4],
                                         hook=chips_after("l0e", EARLY, b0e))
    r1, r0e = chips["l1"], chips["l0e"]
    small0 = {**small0, **small0b}
    r0l = _reduce_start("l0l", LATE, late0, place)
    red1, red0 = _reduce_finish_both(("l1", names, r1), ("l0e", EARLY, r0e), place, r0l[4])

    loss = lax.psum(loss_part[0, 0], ("x", "y", "c"))
    grad_x = dx0[None]

    out_big = {}
    for nm in EARLY:
        out_big[nm] = adamw_big("adamw_" + nm, given[nm], given["m_" + nm], given["v_" + nm], red0[nm], red1[nm], BIG_BY_NAME[nm][3])
    red0.update(_reduce_finish("l0l", LATE, r0l, place, out_big[EARLY[-1]][3]))
    for nm in LATE:
        out_big[nm] = adamw_big("adamw_" + nm, given[nm], given["m_" + nm], given["v_" + nm], red0[nm], red1[nm], BIG_BY_NAME[nm][3])

    widths = {nm: given[nm].shape[1] for nm, _ in SMALL}
    small_g = {nm: jnp.concatenate([small0[nm][:, :widths[nm]] if nm != "attn_sinks" else small0[nm][:, :LANES],
                                    small1[nm][:, :widths[nm]] if nm != "attn_sinks" else small1[nm][:, :LANES]], axis=0)
               for nm, _ in SMALL}
    g_sum = allreduce_small("allreduce_small", _pack_small(small_g))
    sm = adamw_small("adamw_small", _pack_small({nm: given[nm] for nm, _ in SMALL}),
                     _pack_small({nm: given["m_" + nm] for nm, _ in SMALL}),
                     _pack_small({nm: given["v_" + nm] for nm, _ in SMALL}), g_sum)
    out_small = [_unpack_small(a, widths) for a in sm]

    order = ["w_in", "attn_sinks", "hgrn_lb_logits", "attn_out_gain", "hgrn_out_gain", "w_out", "pre_mix_gain", "post_mix_gain",
             "pre_ffn_gain", "post_ffn_gain", "w_ffn_gate", "w_ffn_up", "w_ffn_down", "ple_gain", "w_ple_gate", "w_ple_proj"]
    res = [loss, grad_x]
    for k in range(4):
        for nm in order:
            res.append(out_big[nm][k] if nm in out_big else out_small[k][nm])
    return tuple(res)
```

```python
import jax
import jax.numpy as jnp
from jax import lax
from jax.experimental import pallas as pl
from jax.experimental.pallas import tpu as pltpu

F32, BF16 = jnp.float32, jnp.bfloat16
SDS = jax.ShapeDtypeStruct
MESH = pl.DeviceIdType.MESH

D_MODEL = 2048
ATTN_WIDTH = 1024
HGRN_WIDTH = 1024
KV_WIDTH = 256
N_Q_HEADS = 16
N_KV_HEADS = 4
Q_PER_KV = 4
WINDOW = 128
MASK_VALUE = -1e30
ROPE_THETA = 10000.0
HGRN_HEADS = 8
HGRN_CHUNK = 16
RMS_EPS = 1e-6
LANES = 128
N_CHIPS = 4
COL_Q, COL_K, COL_V, COL_HQ, COL_HF, COL_HI, COL_HG = 0, 8, 10, 12, 20, 28, 36

ADAM_LR, ADAM_B1, ADAM_B2, ADAM_EPS, ADAM_WD, ADAM_STEP = 0.001, 0.9, 0.999, 1e-08, 0.01, 10

VMEM_LIMIT = 56 * 1024 * 1024
ROW_TILE = 256
ROW_TILE_WIDE = 512

_NN = (((1,), (0,)), ((), ()))
_NT = (((1,), (1,)), ((), ()))
_TN = (((0,), (0,)), ((), ()))


def _pc(body, *, name, out_shape, in_specs, out_specs, grid=(), scratch=(), sem=None, grid_spec=None, **kw):
    params = dict(vmem_limit_bytes=VMEM_LIMIT)
    if sem is not None:
        params["dimension_semantics"] = sem
    if grid_spec is not None:
        return pl.pallas_call(body, name=name, out_shape=out_shape, grid_spec=grid_spec,
                              compiler_params=pltpu.CompilerParams(**params), **kw)
    return pl.pallas_call(body, name=name, out_shape=out_shape, grid=grid, in_specs=in_specs, out_specs=out_specs,
                          scratch_shapes=list(scratch), compiler_params=pltpu.CompilerParams(**params), **kw)


def _sigmoid(x):
    return 1.0 / (1.0 + jnp.exp(-x))


def _rstd(x):
    return lax.rsqrt(jnp.mean(x * x, axis=-1, keepdims=True) + RMS_EPS)


def _rows(t, w, col=0):
    return pl.BlockSpec((t, w), lambda i, col=col: (i, col))


def _fixed(shape):
    return pl.BlockSpec(shape, lambda *_: (0,) * len(shape))


def _mm(name, a, b, *, dims, grid, a_spec, b_spec, o_spec, out_shape, parts=1, add=None, add_spec=None, after=None):
    def body(*refs):
        a_ref, b_ref, o_ref = refs[0], refs[1], refs[-1]
        if parts == 1:
            r = lax.dot_general(a_ref[...].astype(BF16), b_ref[...].astype(BF16), dims, preferred_element_type=F32)
        else:
            w = a_ref.shape[1] // parts
            r = None
            for j in range(parts):
                t = lax.dot_general(a_ref[:, j * w:(j + 1) * w].astype(BF16), b_ref[j].astype(BF16), dims,
                                    preferred_element_type=F32)
                r = t if r is None else r + t
        if add is not None:
            r = r + refs[2][...]
        o_ref[...] = r.astype(o_ref.dtype)

    ins = [a, b] + ([] if add is None else [add]) + ([] if after is None else [after])
    specs = [a_spec, b_spec] + ([] if add is None else [add_spec]) + ([] if after is None else [pl.BlockSpec(memory_space=pl.ANY)])
    return _pc(body, name=name, grid=grid, in_specs=specs, out_specs=o_spec, out_shape=out_shape,
               sem=("parallel",) * len(grid))(*ins)


def _tile(n, t):
    if n <= t:
        return n
    while n % t:
        t //= 2
    assert t % 8 == 0
    return t


def mm_col(name, a, wg, out_dtype=F32):
    s, k = a.shape
    _, _, n = wg.shape
    tm = _tile(s, 1024)
    return _mm(name, a, wg, dims=_NN, grid=(N_CHIPS, s // tm),
               a_spec=pl.BlockSpec((tm, k), lambda j, i: (i, 0)),
               b_spec=pl.BlockSpec((None, k, n), lambda j, i: (j, 0, 0)),
               o_spec=pl.BlockSpec((tm, n), lambda j, i: (i, j)),
               out_shape=SDS((s, N_CHIPS * n), out_dtype))


def mm_row(name, a, wg, out_dtype=F32):
    s, _ = a.shape
    _, r, n = wg.shape
    tm = _tile(s, 512)
    tn = _tile(n, 1024 if r > 512 else 2048)
    return _mm(name, a, wg, dims=_NN, grid=(n // tn, s // tm), parts=N_CHIPS,
               a_spec=pl.BlockSpec((tm, N_CHIPS * r), lambda j, i: (i, 0)),
               b_spec=pl.BlockSpec((N_CHIPS, r, tn), lambda j, i: (0, 0, j)),
               o_spec=pl.BlockSpec((tm, tn), lambda j, i: (i, j)),
               out_shape=SDS((s, n), out_dtype))


def mm_col_t(name, dy, wg, add=None, out_dtype=F32, after=None):
    s, _ = dy.shape
    _, k, n = wg.shape
    tm = _tile(s, 512)
    tk = _tile(k, 1024)
    return _mm(name, dy, wg, dims=_NT, grid=(k // tk, s // tm), parts=N_CHIPS,
               a_spec=pl.BlockSpec((tm, N_CHIPS * n), lambda j, i: (i, 0)),
               b_spec=pl.BlockSpec((N_CHIPS, tk, n), lambda j, i: (0, j, 0)),
               o_spec=pl.BlockSpec((tm, tk), lambda j, i: (i, j)),
               add=add, add_spec=pl.BlockSpec((tm, tk), lambda j, i: (i, j)),
               out_shape=SDS((s, k), out_dtype), after=after)


def in_proj_bwd(name, pieces, wg):
    s = pieces[0].shape[0]
    _, k, n = wg.shape
    tm = _tile(s, 512)
    tk = _tile(k, 512)
    width = sum(p.shape[1] for p in pieces)
    npc = len(pieces)
    segments, start = [], 0
    for t, p in enumerate(pieces):
        lo = start
        while lo < start + p.shape[1]:
            hi = min(start + p.shape[1], (lo // n + 1) * n)
            segments.append((t, lo - start, hi - start, lo // n, lo % n, lo % n + hi - lo))
            lo = hi
        start += p.shape[1]

    def body(*refs):
        w_ref, dproj_ref, o_ref = refs[npc], refs[npc + 1], refs[npc + 2]

        @pl.when(pl.program_id(1) == 0)
        def _():
            off = 0
            for t in range(npc):
                dproj_ref[:, off:off + pieces[t].shape[1]] = refs[t][...]
                off += pieces[t].shape[1]

        r = None
        for t, a, b, j, c, d in segments:
            part = lax.dot_general(refs[t][:, a:b], w_ref[j, :, c:d], _NT, preferred_element_type=F32)
            r = part if r is None else r + part
        o_ref[...] = r

    return _pc(body, name=name, grid=(s // tm, k // tk),
               in_specs=[pl.BlockSpec((tm, p.shape[1]), lambda i, j: (i, 0)) for p in pieces]
               + [pl.BlockSpec((N_CHIPS, tk, n), lambda i, j: (0, j, 0))],
               out_specs=[pl.BlockSpec((tm, width), lambda i, j: (i, 0)), pl.BlockSpec((tm, tk), lambda i, j: (i, j))],
               out_shape=[SDS((s, width), BF16), SDS((s, k), F32)], sem=("parallel", "arbitrary"))(*pieces, wg)


def mm_row_t(name, dy, wg, out_dtype=F32):
    s, n = dy.shape
    _, r, _ = wg.shape
    tm = _tile(s, 512)

    def body(dy_ref, w_ref, o_ref):
        dyv = dy_ref[...].astype(BF16)
        for j in range(N_CHIPS):
            o_ref[:, j * r:(j + 1) * r] = lax.dot_general(dyv, w_ref[j], _NT, preferred_element_type=F32).astype(out_dtype)

    return _pc(body, name=name, grid=(s // tm,), in_specs=[_rows(tm, n), _fixed(wg.shape)], out_specs=_rows(tm, N_CHIPS * r),
               out_shape=SDS((s, N_CHIPS * r), out_dtype), sem=("parallel",))(dy, wg)


def norm_bwd_pair(name, x_a, gain_a, dy, dres, x_b, gain_b):
    s, d = x_a.shape
    t = _tile(s, ROW_TILE)

    def one(xv, g, dyv):
        r = _rstd(xv)
        xh = xv * r
        dyg = dyv * g
        return r * (dyg - xh * jnp.mean(dyg * xh, axis=-1, keepdims=True)), jnp.sum(dyv * xh, axis=0, keepdims=True)

    def body(xa_ref, ga_ref, dy_ref, r_ref, xb_ref, gb_ref, dx_ref, db_ref, dga_ref, dgb_ref):
        dx, pa = one(xa_ref[...], ga_ref[...], dy_ref[...])
        dx = dx + r_ref[...]
        dx_ref[...] = dx
        db, pb = one(xb_ref[...], gb_ref[...], dx)
        db_ref[...] = db.astype(BF16)

        @pl.when(pl.program_id(0) == 0)
        def _():
            dga_ref[...] = pa
            dgb_ref[...] = pb

        @pl.when(pl.program_id(0) > 0)
        def _():
            dga_ref[...] += pa
            dgb_ref[...] += pb

    row, gain = _rows(t, d), _fixed((1, d))
    return _pc(body, name=name, grid=(s // t,), in_specs=[row, gain, row, row, row, gain], out_specs=[row, row, gain, gain],
               out_shape=[SDS((s, d), F32), SDS((s, d), BF16), SDS((1, d), F32), SDS((1, d), F32)],
               sem=("arbitrary",))(x_a, gain_a, dy, dres, x_b, gain_b)


def ffn_gate_up(name, h, wg_gate, wg_up):
    s, k = h.shape
    _, _, n = wg_gate.shape
    tm = _tile(s, 512)

    def body(h_ref, wg_ref, wu_ref, g_ref, u_ref, a_ref):
        hv = h_ref[...]
        g = jnp.dot(hv, wg_ref[...], preferred_element_type=F32)
        u = jnp.dot(hv, wu_ref[...], preferred_element_type=F32)
        g_ref[...] = g
        u_ref[...] = u
        a_ref[...] = ((g * _sigmoid(g)) * u).astype(BF16)

    wspec = pl.BlockSpec((None, k, n), lambda j, i: (j, 0, 0))
    ospec = pl.BlockSpec((tm, n), lambda j, i: (i, j))
    return _pc(body, name=name, grid=(N_CHIPS, s // tm), in_specs=[pl.BlockSpec((tm, k), lambda j, i: (i, 0)), wspec, wspec],
               out_specs=[ospec] * 3, out_shape=[SDS((s, N_CHIPS * n), F32)] * 2 + [SDS((s, N_CHIPS * n), BF16)],
               sem=("parallel", "parallel"))(h, wg_gate, wg_up)


def ffn_down_bwd(name, df, wg_down, g, u):
    s, n = df.shape
    _, r, _ = wg_down.shape
    tm = _tile(s, 512)

    def body(df_ref, w_ref, g_ref, u_ref, dg_ref, du_ref):
        da = lax.dot_general(df_ref[...], w_ref[...], _NT, preferred_element_type=F32)
        gv = g_ref[...]
        sg = _sigmoid(gv)
        du_ref[...] = (da * (gv * sg)).astype(BF16)
        dg_ref[...] = ((da * u_ref[...]) * (sg * (1.0 + gv * (1.0 - sg)))).astype(BF16)

    cspec = pl.BlockSpec((tm, r), lambda j, i: (i, j))
    return _pc(body, name=name, grid=(N_CHIPS, s // tm),
               in_specs=[pl.BlockSpec((tm, n), lambda j, i: (i, 0)), pl.BlockSpec((None, r, n), lambda j, i: (j, 0, 0)), cspec, cspec],
               out_specs=[cspec] * 2, out_shape=[SDS((s, N_CHIPS * r), BF16)] * 2,
               sem=("parallel", "parallel"))(df, wg_down, g, u)


def mm_wg_col(name, a, dy):
    s, k = a.shape
    n = dy.shape[1] // N_CHIPS
    tm = _tile(k // 2, 1024)
    hb = (k // 2) // tm
    return _mm(name, a, dy, dims=_TN, grid=(N_CHIPS, k // tm),
               a_spec=pl.BlockSpec((s, tm), lambda j, i: (0, i)),
               b_spec=pl.BlockSpec((s, n), lambda j, i: (0, j)),
               o_spec=pl.BlockSpec((None, None, tm, n), lambda j, i: (i // hb, j, i % hb, 0)),
               out_shape=SDS((2, N_CHIPS, k // 2, n), BF16))


def mm_wg_row(name, a, dy):
    s, n = dy.shape
    r = a.shape[1] // N_CHIPS
    tn = _tile(n // 2, 512)
    nb = (n // 2) // tn
    return _mm(name, a, dy, dims=_TN, grid=(N_CHIPS, n // tn),
               a_spec=pl.BlockSpec((s, r), lambda j, i: (0, j)),
               b_spec=pl.BlockSpec((s, tn), lambda j, i: (0, i)),
               o_spec=pl.BlockSpec((None, None, r, tn), lambda j, i: (i // nb, j, 0, i % nb)),
               out_shape=SDS((2, N_CHIPS, r, n // 2), BF16))


def pre_norm(name, x, gain):
    s, d = x.shape
    t = _tile(s, ROW_TILE_WIDE)

    def body(x_ref, g_ref, o_ref):
        xv = x_ref[...]
        o_ref[...] = ((xv * _rstd(xv)) * g_ref[...]).astype(BF16)

    return _pc(body, name=name, grid=(s // t,), in_specs=[_rows(t, d), _fixed((1, d))], out_specs=_rows(t, d),
               out_shape=SDS((s, d), BF16), sem=("parallel",))(x, gain)


def post_pre_norm(name, m, g_post, res, g_pre):
    s, d = m.shape
    t = _tile(s, ROW_TILE_WIDE)

    def body(m_ref, gp_ref, r_ref, gn_ref, x_ref, h_ref):
        mv = m_ref[...]
        xn = r_ref[...] + (mv * _rstd(mv)) * gp_ref[...]
        x_ref[...] = xn
        h_ref[...] = ((xn * _rstd(xn)) * gn_ref[...]).astype(BF16)

    return _pc(body, name=name, grid=(s // t,),
               in_specs=[_rows(t, d), _fixed((1, d)), _rows(t, d), _fixed((1, d))],
               out_specs=[_rows(t, d), _rows(t, d)], out_shape=[SDS((s, d), F32), SDS((s, d), BF16)],
               sem=("parallel",))(m, g_post, res, g_pre)


def _row_dot(a_ref, w_ref):
    r = w_ref.shape[1]
    out = None
    for j in range(N_CHIPS):
        part = jnp.dot(a_ref[:, j * r:(j + 1) * r], w_ref[j], preferred_element_type=F32)
        out = part if out is None else out + part
    return out


def _row_dot_specs(t, a, wg):
    return [_rows(t, a.shape[1]), _fixed(wg.shape)]


def out_proj_post_mix(name, a, wg, g_post, res, g_pre):
    s, d = res.shape
    t = _tile(s, ROW_TILE)

    def body(a_ref, w_ref, gp_ref, r_ref, gn_ref, m_ref, x_ref, h_ref):
        mv = _row_dot(a_ref, w_ref)
        m_ref[...] = mv
        xn = r_ref[...] + (mv * _rstd(mv)) * gp_ref[...]
        x_ref[...] = xn
        h_ref[...] = ((xn * _rstd(xn)) * gn_ref[...]).astype(BF16)

    return _pc(body, name=name, grid=(s // t,),
               in_specs=_row_dot_specs(t, a, wg) + [_fixed((1, d)), _rows(t, d), _fixed((1, d))],
               out_specs=[_rows(t, d)] * 3, out_shape=[SDS((s, d), F32), SDS((s, d), F32), SDS((s, d), BF16)],
               sem=("parallel",))(a, wg, g_post, res, g_pre)


def _col_dot(p_ref, w_ref):
    pv = p_ref[...].astype(BF16)
    return jnp.concatenate([jnp.dot(pv, w_ref[j], preferred_element_type=F32) for j in range(N_CHIPS)], axis=1)


def ple_gate_fwd_mid(name, a, wg, p, wg_proj, x2, g_next):
    s, d = x2.shape
    t = _tile(s, ROW_TILE)

    def body(a_ref, w_ref, p_ref, wp_ref, x_ref, g_ref, z_ref, pp_ref, xo_ref, h_ref):
        z = _row_dot(a_ref, w_ref)
        z_ref[...] = z
        pv = _col_dot(p_ref, wp_ref)
        pp_ref[...] = pv
        xn = x_ref[...] + pv * _sigmoid(z)
        xo_ref[...] = xn
        h_ref[...] = ((xn * _rstd(xn)) * g_ref[...]).astype(BF16)

    return _pc(body, name=name, grid=(s // t,),
               in_specs=_row_dot_specs(t, a, wg) + [_rows(t, p.shape[1]), _fixed(wg_proj.shape), _rows(t, d), _fixed((1, d))],
               out_specs=[_rows(t, d)] * 4, out_shape=[SDS((s, d), F32)] * 3 + [SDS((s, d), BF16)],
               sem=("parallel",))(a, wg, p, wg_proj, x2, g_next)


def ple_gate_fwd_loss(name, a, wg, p, wg_proj, x2, target):
    s, d = x2.shape
    t = _tile(s, ROW_TILE)

    def body(a_ref, w_ref, p_ref, wp_ref, x_ref, t_ref, dy_ref, dpp_ref, dz_ref, l_ref):
        gate = _sigmoid(_row_dot(a_ref, w_ref))
        pv = _col_dot(p_ref, wp_ref)
        err = (x_ref[...] + pv * gate) - t_ref[...]
        dy = err * (1.0 / d)
        dy_ref[...] = dy
        dpp_ref[...] = (dy * gate).astype(BF16)
        dz_ref[...] = ((dy * pv) * (gate * (1.0 - gate))).astype(BF16)
        part = jnp.sum(jnp.sum(err * err, axis=-1, keepdims=True), axis=0, keepdims=True) * (0.5 / d)

        @pl.when(pl.program_id(0) == 0)
        def _():
            l_ref[...] = part

        @pl.when(pl.program_id(0) > 0)
        def _():
            l_ref[...] += part

    return _pc(body, name=name, grid=(s // t,),
               in_specs=_row_dot_specs(t, a, wg) + [_rows(t, p.shape[1]), _fixed(wg_proj.shape), _rows(t, d), _rows(t, d)],
               out_specs=[_rows(t, d), _rows(t, d), _rows(t, d), _fixed((1, 1))],
               out_shape=[SDS((s, d), F32), SDS((s, d), BF16), SDS((s, d), BF16), SDS((1, 1), F32)],
               sem=("arbitrary",))(a, wg, p, wg_proj, x2, target)


ANY_SPEC = pl.BlockSpec(memory_space=pl.ANY)


def ple_bwd(name, dx3, z, pp, after):
    s, d = z.shape
    t = _tile(s, ROW_TILE_WIDE)

    def body(d_ref, z_ref, p_ref, after_ref, dpp_ref, dz_ref):
        gate = _sigmoid(z_ref[...])
        dv = d_ref[...]
        dpp_ref[...] = (dv * gate).astype(BF16)
        dz_ref[...] = ((dv * p_ref[...]) * (gate * (1.0 - gate))).astype(BF16)

    return _pc(body, name=name, grid=(s // t,), in_specs=[_rows(t, d)] * 3 + [ANY_SPEC], out_specs=[_rows(t, d)] * 2,
               out_shape=[SDS((s, d), BF16)] * 2, sem=("parallel",))(dx3, z, pp, after)


def norm_bwd(name, xin, gain, dy, dres):
    s, d = xin.shape
    t = _tile(s, ROW_TILE_WIDE)

    def body(x_ref, g_ref, dy_ref, r_ref, dx_ref, dg_ref):
        xv = x_ref[...]
        r = _rstd(xv)
        xh = xv * r
        dyv = dy_ref[...]
        dyg = dyv * g_ref[...]
        c = jnp.mean(dyg * xh, axis=-1, keepdims=True)
        dx_ref[...] = r * (dyg - xh * c) + r_ref[...]
        part = jnp.sum(dyv * xh, axis=0, keepdims=True)

        @pl.when(pl.program_id(0) == 0)
        def _():
            dg_ref[...] = part

        @pl.when(pl.program_id(0) > 0)
        def _():
            dg_ref[...] += part

    return _pc(body, name=name, grid=(s // t,), in_specs=[_rows(t, d), _fixed((1, d)), _rows(t, d), _rows(t, d)],
               out_specs=[_rows(t, d), _fixed((1, d))], out_shape=[SDS((s, d), F32), SDS((1, d), F32)],
               sem=("arbitrary",))(xin, gain, dy, dres)


def _lane(shape):
    return lax.broadcasted_iota(jnp.int32, shape, 1)


def _swap_halves(x):
    lo = (_lane(x.shape) % 64) < 32
    return jnp.where(lo, pltpu.roll(x, 96, 1), pltpu.roll(x, 32, 1))


def rope_tables(name, pos_col, inv_freq):
    s = pos_col.shape[0]
    t = _tile(s, ROW_TILE_WIDE)

    def body(p_ref, f_ref, c_ref, s_ref):
        ang = p_ref[...].astype(F32) * f_ref[...]
        lo = (_lane(ang.shape) % 64) < 32
        c_ref[...] = jnp.cos(ang)
        sn = jnp.sin(ang)
        s_ref[...] = jnp.where(lo, -sn, sn)

    return _pc(body, name=name, grid=(s // t,), in_specs=[_rows(t, 1), _fixed((1, LANES))],
               out_specs=[_rows(t, LANES)] * 2, out_shape=[SDS((s, LANES), F32)] * 2, sem=("parallel",))(pos_col, inv_freq)


def _pad_heads(chunk, lo_mask):
    zero = jnp.zeros_like(chunk)
    return jnp.where(lo_mask, chunk, zero), jnp.where(lo_mask, pltpu.roll(chunk, 64, 1), zero)


def rope_qkv(name, proj, cos, sin):
    s = proj.shape[0]
    t = _tile(s, ROW_TILE_WIDE)

    def body(q_ref, kv_ref, c_ref, s_ref, qp_ref, kp_ref, vp_ref):
        cs, sn = c_ref[...], s_ref[...]
        lo_mask = _lane(cs.shape) < 64

        def rot(x):
            return x * cs + _swap_halves(x) * sn

        for j in range(ATTN_WIDTH // LANES):
            a, b = _pad_heads(rot(q_ref[:, j * LANES:(j + 1) * LANES]), lo_mask)
            qp_ref[:, (2 * j) * LANES:(2 * j + 1) * LANES] = a.astype(BF16)
            qp_ref[:, (2 * j + 1) * LANES:(2 * j + 2) * LANES] = b.astype(BF16)
        for j in range(KV_WIDTH // LANES):
            a, b = _pad_heads(rot(kv_ref[:, j * LANES:(j + 1) * LANES]), lo_mask)
            kp_ref[:, (2 * j) * LANES:(2 * j + 1) * LANES] = a.astype(BF16)
            kp_ref[:, (2 * j + 1) * LANES:(2 * j + 2) * LANES] = b.astype(BF16)
            a, b = _pad_heads(kv_ref[:, KV_WIDTH + j * LANES:KV_WIDTH + (j + 1) * LANES], lo_mask)
            vp_ref[:, (2 * j) * LANES:(2 * j + 1) * LANES] = a.astype(BF16)
            vp_ref[:, (2 * j + 1) * LANES:(2 * j + 2) * LANES] = b.astype(BF16)

    return _pc(body, name=name, grid=(s // t,),
               in_specs=[_rows(t, ATTN_WIDTH, 0), _rows(t, 2 * KV_WIDTH, 2), _rows(t, LANES), _rows(t, LANES)],
               out_specs=[_rows(t, N_Q_HEADS * LANES), _rows(t, N_KV_HEADS * LANES), _rows(t, N_KV_HEADS * LANES)],
               out_shape=[SDS((s, N_Q_HEADS * LANES), BF16), SDS((s, N_KV_HEADS * LANES), BF16),
                          SDS((s, N_KV_HEADS * LANES), BF16)],
               sem=("parallel",))(proj, proj, cos, sin)


def _attn_mask(n):
    L = WINDOW
    qi = lax.broadcasted_iota(jnp.int32, (L, 2 * L), 0) + L
    ki = lax.broadcasted_iota(jnp.int32, (L, 2 * L), 1)
    rel = qi - ki
    return (rel >= 0) & (rel < WINDOW) & ((n > 0) | (ki >= L))


def _attn_scores(qh, kk, valid):
    sc = lax.dot_general(qh, kk, _NT, preferred_element_type=F32) * 0.125
    return jnp.where(valid, sc, MASK_VALUE)


def _attn_softmax(sc, sink):
    m = jnp.maximum(jnp.max(sc, axis=-1, keepdims=True), sink)
    e = jnp.exp(sc - m)
    es = jnp.exp(sink - m)
    den = jnp.sum(e, axis=-1, keepdims=True) + es
    return e / den, es / den


def _attn_specs(s):
    L = WINDOW
    cur = lambda n: (n, 0)
    prev = lambda n: (jnp.maximum(n - 1, 0), 0)
    kvw = N_KV_HEADS * LANES
    return [pl.BlockSpec((L, N_Q_HEADS * LANES), cur), pl.BlockSpec((L, kvw), cur), pl.BlockSpec((L, kvw), prev),
            pl.BlockSpec((L, kvw), cur), pl.BlockSpec((L, kvw), prev), pl.BlockSpec(memory_space=pltpu.SMEM)]


def attn_fwd(name, qp, kp, vp, sinks):
    s = qp.shape[0]
    L = WINDOW

    def body(q_ref, kc_ref, kp_ref, vc_ref, vp_ref, sk_ref, o_ref):
        valid = _attn_mask(pl.program_id(0))
        kks, vvs = [], []
        for kvh in range(N_KV_HEADS):
            cols = slice(kvh * LANES, (kvh + 1) * LANES)
            kks.append(jnp.concatenate([kp_ref[:, cols], kc_ref[:, cols]], axis=0))
            vvs.append(jnp.concatenate([vp_ref[:, cols], vc_ref[:, cols]], axis=0))
        scs = [_attn_scores(q_ref[:, h * LANES:(h + 1) * LANES], kks[h // Q_PER_KV], valid) for h in range(N_Q_HEADS)]
        pbs = [_attn_softmax(scs[h], sk_ref[h])[0].astype(BF16) for h in range(N_Q_HEADS)]
        outs = [jnp.dot(pbs[h], vvs[h // Q_PER_KV], preferred_element_type=F32) for h in range(N_Q_HEADS)]
        for j in range(ATTN_WIDTH // LANES):
            o_ref[:, j * LANES:(j + 1) * LANES] = outs[2 * j] + pltpu.roll(outs[2 * j + 1], 64, 1)

    return _pc(body, name=name, grid=(s // L,), in_specs=_attn_specs(s),
               out_specs=pl.BlockSpec((L, ATTN_WIDTH), lambda n: (n, 0)),
               out_shape=SDS((s, ATTN_WIDTH), F32), sem=("parallel",))(qp, kp, kp, vp, vp, sinks)


def attn_bwd(name, qp, kp, vp, sinks, dattn):
    s = qp.shape[0]
    L = WINDOW
    kvw = N_KV_HEADS * LANES

    def body(q_ref, kc_ref, kp_ref, vc_ref, vp_ref, sk_ref, do_ref, dq_ref, dkc_ref, dkp_ref, dvc_ref, dvp_ref, ds_ref):
        n = pl.program_id(0)
        valid = _attn_mask(n)
        lo_mask = _lane((L, LANES)) < 64
        lane1 = _lane((1, LANES))
        dsink = jnp.zeros((1, LANES), F32)
        heads = range(N_Q_HEADS)
        kks, vvs = [], []
        for kvh in range(N_KV_HEADS):
            cols = slice(kvh * LANES, (kvh + 1) * LANES)
            kks.append(jnp.concatenate([kp_ref[:, cols], kc_ref[:, cols]], axis=0))
            vvs.append(jnp.concatenate([vp_ref[:, cols], vc_ref[:, cols]], axis=0))
        qs, dos, scs, dps = [], [], [], []
        for h in heads:
            qs.append(q_ref[:, h * LANES:(h + 1) * LANES])
            chunk = do_ref[:, (h // 2) * LANES:(h // 2 + 1) * LANES]
            if h % 2:
                chunk = pltpu.roll(chunk, 64, 1)
            dos.append(jnp.where(lo_mask, chunk, 0.0).astype(BF16))
            scs.append(_attn_scores(qs[h], kks[h // Q_PER_KV], valid))
            dps.append(lax.dot_general(dos[h], vvs[h // Q_PER_KV], _NT, preferred_element_type=F32))
        pbs, dsbs = [], []
        for h in heads:
            p, ps = _attn_softmax(scs[h], sk_ref[h])
            delta = jnp.sum(p * dps[h], axis=-1, keepdims=True)
            dsbs.append(((p * (dps[h] - delta)) * 0.125).astype(BF16))
            pbs.append(p.astype(BF16))
            dsink = dsink + jnp.where(lane1 == h, -jnp.sum(ps * delta, axis=0, keepdims=True), 0.0)
        for kvh in range(N_KV_HEADS):
            cols = slice(kvh * LANES, (kvh + 1) * LANES)
            dkk = jnp.zeros((2 * L, LANES), F32)
            dvv = jnp.zeros((2 * L, LANES), F32)
            for h in range(kvh * Q_PER_KV, (kvh + 1) * Q_PER_KV):
                dq_ref[:, h * LANES:(h + 1) * LANES] = jnp.dot(dsbs[h], kks[kvh], preferred_element_type=F32)
                dkk = dkk + lax.dot_general(dsbs[h], qs[h], _TN, preferred_element_type=F32)
                dvv = dvv + lax.dot_general(pbs[h], dos[h], _TN, preferred_element_type=F32)
            dkp_ref[:, cols] = dkk[:L]
            dkc_ref[:, cols] = dkk[L:]
            dvp_ref[:, cols] = dvv[:L]
            dvc_ref[:, cols] = dvv[L:]

        @pl.when(n == 0)
        def _():
            ds_ref[...] = dsink

        @pl.when(n > 0)
        def _():
            ds_ref[...] += dsink

    blk = lambda w: pl.BlockSpec((L, w), lambda n: (n, 0))
    return _pc(body, name=name, grid=(s // L,), in_specs=_attn_specs(s) + [blk(ATTN_WIDTH)],
               out_specs=[blk(N_Q_HEADS * LANES), blk(kvw), blk(kvw), blk(kvw), blk(kvw), _fixed((1, LANES))],
               out_shape=[SDS((s, N_Q_HEADS * LANES), F32)] + [SDS((s, kvw), F32)] * 4 + [SDS((1, LANES), F32)],
               sem=("arbitrary",))(qp, kp, kp, vp, vp, sinks, dattn)


def rope_bwd(name, dqp, dkc, dkp, dvc, dvp, cos, sin):
    s = dqp.shape[0]
    L = WINDOW
    nb = s // L
    kvw = N_KV_HEADS * LANES

    def body(dq_ref, dkc_ref, dkp_ref, dvc_ref, dvp_ref, c_ref, s_ref, o_ref):
        cs, sn = c_ref[...], s_ref[...]
        more = (pl.program_id(0) < nb - 1).astype(F32)

        def unrot(x):
            return x * cs - _swap_halves(x) * sn

        def compact(ref, j, nxt=None):
            a = ref[:, (2 * j) * LANES:(2 * j + 1) * LANES]
            b = ref[:, (2 * j + 1) * LANES:(2 * j + 2) * LANES]
            if nxt is not None:
                a = a + more * nxt[:, (2 * j) * LANES:(2 * j + 1) * LANES]
                b = b + more * nxt[:, (2 * j + 1) * LANES:(2 * j + 2) * LANES]
            return a + pltpu.roll(b, 64, 1)

        for j in range(ATTN_WIDTH // LANES):
            o_ref[:, j * LANES:(j + 1) * LANES] = unrot(compact(dq_ref, j)).astype(BF16)
        for j in range(KV_WIDTH // LANES):
            o_ref[:, (COL_K + j) * LANES:(COL_K + j + 1) * LANES] = unrot(compact(dkc_ref, j, dkp_ref)).astype(BF16)
            o_ref[:, (COL_V + j) * LANES:(COL_V + j + 1) * LANES] = compact(dvc_ref, j, dvp_ref).astype(BF16)

    cur = lambda n: (n, 0)
    nxt = lambda n: (jnp.minimum(n + 1, nb - 1), 0)
    return _pc(body, name=name, grid=(nb,),
               in_specs=[pl.BlockSpec((L, N_Q_HEADS * LANES), cur), pl.BlockSpec((L, kvw), cur), pl.BlockSpec((L, kvw), nxt),
                         pl.BlockSpec((L, kvw), cur), pl.BlockSpec((L, kvw), nxt), pl.BlockSpec((L, LANES), cur),
                         pl.BlockSpec((L, LANES), cur)],
               out_specs=pl.BlockSpec((L, COL_HQ * LANES), cur), out_shape=SDS((s, COL_HQ * LANES), BF16),
               sem=("parallel",))(dqp, dkc, dkp, dvc, dvp, cos, sin)


def _split3(x):
    a = x.astype(BF16)
    r = x - a.astype(F32)
    b = r.astype(BF16)
    c = (r - b.astype(F32)).astype(BF16)
    return a, b, c


def _chunk_sum(x, upper):
    t = x.shape[0]
    ri = lax.broadcasted_iota(jnp.int32, (t, t), 0)
    ci = lax.broadcasted_iota(jnp.int32, (t, t), 1)
    same = (ri // HGRN_CHUNK) == (ci // HGRN_CHUNK)
    tri = (ci >= ri) if upper else (ci <= ri)
    m = jnp.where(same & tri, 1.0, 0.0).astype(BF16)
    out = None
    for part in _split3(x):
        y = jnp.dot(m, part, preferred_element_type=F32)
        out = y if out is None else out + y
    return out


def _lower_bound(l_ref, layer):
    lv = l_ref[...]
    e = jnp.exp(lv - jnp.max(lv, axis=0, keepdims=True))
    sm = e / jnp.sum(e, axis=0, keepdims=True)
    s0 = sm[0:1]
    return (s0 - s0) if layer == 0 else ((s0 + sm[1:2]) - s0)


def _hgrn_gates(hq_ref, hf_ref, lb):
    z = hf_ref[...]
    sg = _sigmoid(z)
    f = lb + (1.0 - lb) * sg
    kin = (1.0 - lb) * _sigmoid(-z)
    hq = hq_ref[...]
    sq = _sigmoid(hq)
    return sg, f, kin, hq, sq


def _shift_down(x, d):
    return x if d == 0 else pltpu.roll(x, d, 0)


def _shift_up(x, d):
    return x if d == 0 else pltpu.roll(x, x.shape[0] - d, 0)


CHUNKS_PER_BLOCK = LANES // HGRN_CHUNK


def _chunk_iotas():
    shape = (HGRN_CHUNK, LANES)
    return lax.broadcasted_iota(jnp.int32, shape, 0), lax.broadcasted_iota(jnp.int32, shape, 1)


def _chunk_rows(block, chunk):
    start = block * LANES + chunk * HGRN_CHUNK
    return slice(start, start + HGRN_CHUNK)


HGRN_HEADS_PER_STEP = 4
HGRN_STEP_WIDTH = HGRN_HEADS_PER_STEP * LANES


def _hgrn_specs(t, rev, nt):
    row = (lambda h, i: nt - 1 - i) if rev else (lambda h, i: i)
    col = lambda base: pl.BlockSpec((t, HGRN_STEP_WIDTH),
                                    lambda h, i, base=base: (row(h, i), base // HGRN_HEADS_PER_STEP + h))
    return col, row


def _head_views(refs, hh):
    return [r.at[:, pl.ds(hh * LANES, LANES)] for r in refs]


def hgrn_fwd(name, proj, lb_logits, layer):
    s = proj.shape[0]
    t = _tile(s, ROW_TILE)
    nt = s // t
    nc = t // HGRN_CHUNK
    col, row = _hgrn_specs(t, False, nt)

    def body(hq_ref, hf_ref, hi_ref, l_ref, o_ref, st_ref, state):
        @pl.when(pl.program_id(1) == 0)
        def _():
            state[...] = jnp.zeros_like(state)

        for hh in range(HGRN_HEADS_PER_STEP):
            head(*_head_views((hq_ref, hf_ref, hi_ref, l_ref, o_ref), hh), st_ref.at[:, hh], state.at[hh])

    def head(hq_ref, hf_ref, hi_ref, l_ref, o_ref, st_ref, state):
        lb = _lower_bound(l_ref, layer)
        sg, f, kin, hq, sq = _hgrn_gates(hq_ref, hf_ref, lb)
        q = hq * sq
        vb = hi_ref[...].astype(BF16)
        b = _chunk_sum(jnp.log(f), False)
        qe = (q * jnp.exp(b)).astype(BF16)
        trow, lane = _chunk_iotas()
        chunks = [(j, cc) for j in range(t // LANES) for cc in range(CHUNKS_PER_BLOCK)]
        decay, update = [], []
        for j, cc in chunks:
            rs = _chunk_rows(j, cc)
            bc = b[rs]
            bl = bc[HGRN_CHUNK - 1:HGRN_CHUNK, :]
            ke = (kin[rs] * jnp.exp(bl - bc)).astype(BF16)
            decay.append(jnp.exp(bl))
            update.append(lax.dot_general(vb[rs], ke, _TN, preferred_element_type=F32))
        st = state[...]
        for c in range(nc):
            st_ref[c] = st
            st = st * decay[c] + update[c]
        state[...] = st
        o_inter = [lax.dot_general(qe[c * HGRN_CHUNK:(c + 1) * HGRN_CHUNK], st_ref[c].astype(BF16), _NT,
                                   preferred_element_type=F32) for c in range(nc)]
        for j in range(t // LANES):
            blk = slice(j * LANES, (j + 1) * LANES)
            rows = []
            for cc in range(CHUNKS_PER_BLOCK):
                rs = _chunk_rows(j, cc)
                bc, qc, kc = b[rs], q[rs], kin[rs]
                here = trow + cc * HGRN_CHUNK
                am = jnp.where(lane == here, jnp.sum(qc * kc, axis=-1, keepdims=True), 0.0)
                for d in range(1, HGRN_CHUNK):
                    e = jnp.exp(jnp.where(trow >= d, bc - _shift_down(bc, d), MASK_VALUE))
                    a = jnp.sum((qc * _shift_down(kc, d)) * e, axis=-1, keepdims=True)
                    am = jnp.where(lane == here - d, a, am)
                rows.append(am)
            o_intra = jnp.dot(jnp.concatenate(rows, axis=0).astype(BF16), vb[blk], preferred_element_type=F32)
            for cc in range(CHUNKS_PER_BLOCK):
                rs = _chunk_rows(j, cc)
                o_ref[rs, :] = o_intra[cc * HGRN_CHUNK:(cc + 1) * HGRN_CHUNK] + o_inter[j * CHUNKS_PER_BLOCK + cc]

    hp = HGRN_HEADS_PER_STEP
    return _pc(body, name=name, grid=(HGRN_HEADS // hp, nt),
               in_specs=[col(COL_HQ), col(COL_HF), col(COL_HI), pl.BlockSpec((2, HGRN_STEP_WIDTH), lambda h, i: (0, h))],
               out_specs=[pl.BlockSpec((t, HGRN_STEP_WIDTH), lambda h, i: (i, h)),
                          pl.BlockSpec((nc, hp, LANES, LANES), lambda h, i: (i, h, 0, 0))],
               out_shape=[SDS((s, HGRN_WIDTH), F32), SDS((s // HGRN_CHUNK, HGRN_HEADS, LANES, LANES), F32)],
               scratch=[pltpu.VMEM((hp, LANES, LANES), F32)],
               sem=("parallel", "arbitrary"))(proj, proj, proj, lb_logits)


def hgrn_bwd(name, proj, lb_logits, layer, states, do):
    s = proj.shape[0]
    t = _tile(s, ROW_TILE)
    nt = s // t
    nc = t // HGRN_CHUNK
    col, row = _hgrn_specs(t, True, nt)

    def body(hq_ref, hf_ref, hi_ref, l_ref, st_ref, do_ref, dhq_ref, dhf_ref, dhi_ref, dlb_ref, dstate):
        @pl.when(pl.program_id(1) == 0)
        def _():
            dstate[...] = jnp.zeros_like(dstate)

        for hh in range(HGRN_HEADS_PER_STEP):
            hq_v, hf_v, hi_v, l_v, do_v, dhq_v, dhf_v, dhi_v, dlb_v = _head_views(
                (hq_ref, hf_ref, hi_ref, l_ref, do_ref, dhq_ref, dhf_ref, dhi_ref, dlb_ref), hh)
            head(hq_v, hf_v, hi_v, l_v, st_ref.at[:, hh], do_v, dhq_v, dhf_v, dhi_v, dlb_v, dstate.at[hh])

    def head(hq_ref, hf_ref, hi_ref, l_ref, st_ref, do_ref, dhq_ref, dhf_ref, dhi_ref, dlb_ref, dstate):
        first = pl.program_id(1) == 0
        lb = _lower_bound(l_ref, layer)
        sg, f, kin, hq, sq = _hgrn_gates(hq_ref, hf_ref, lb)
        q = hq * sq
        vb = hi_ref[...].astype(BF16)
        b = _chunk_sum(jnp.log(f), False)
        dob = do_ref[...].astype(BF16)
        eb = jnp.exp(b)
        qe = q * eb
        qeb = qe.astype(BF16)
        trow, lane = _chunk_iotas()
        last_row = trow == HGRN_CHUNK - 1

        decay, update = [None] * nc, [None] * nc
        for c in range(nc):
            rs = slice(c * HGRN_CHUNK, (c + 1) * HGRN_CHUNK)
            decay[c] = jnp.exp(b[(c + 1) * HGRN_CHUNK - 1:(c + 1) * HGRN_CHUNK, :])
            update[c] = lax.dot_general(dob[rs], qeb[rs], _TN, preferred_element_type=F32)
        dn_in = [None] * nc
        dn = dstate[...]
        for c in reversed(range(nc)):
            dn_in[c] = dn
            dn = dn * decay[c] + update[c]
        dstate[...] = dn

        dq_c, dk_c, dv_c, dbl_c = [None] * nc, [None] * nc, [None] * nc, [None] * nc
        for c in range(nc):
            rs = slice(c * HGRN_CHUNK, (c + 1) * HGRN_CHUNK)
            bc = b[rs]
            ekb = jnp.exp(bc[HGRN_CHUNK - 1:HGRN_CHUNK, :] - bc)
            ke = kin[rs] * ekb
            st = st_ref[c]
            dnb = dn_in[c].astype(BF16)
            dke = jnp.dot(vb[rs], dnb, preferred_element_type=F32)
            dq_c[c] = jnp.dot(dob[rs], st.astype(BF16), preferred_element_type=F32) * eb[rs]
            dk_c[c] = dke * ekb
            dv_c[c] = lax.dot_general(ke.astype(BF16), dnb, _NT, preferred_element_type=F32)
            dbl_c[c] = jnp.sum(dn_in[c] * st, axis=0, keepdims=True) * decay[c] + jnp.sum(dke * ke, axis=0, keepdims=True)

        db_c = [None] * nc
        for j in range(t // LANES):
            blk = slice(j * LANES, (j + 1) * LANES)
            damat = lax.dot_general(dob[blk], vb[blk], _NT, preferred_element_type=F32)
            rows = [None] * CHUNKS_PER_BLOCK
            for cc in range(CHUNKS_PER_BLOCK):
                c = j * CHUNKS_PER_BLOCK + cc
                rs = _chunk_rows(j, cc)
                bc, qc, kc = b[rs], q[rs], kin[rs]
                dam = damat[cc * HGRN_CHUNK:(cc + 1) * HGRN_CHUNK]
                here = trow + cc * HGRN_CHUNK
                on = lane == here
                da = jnp.sum(jnp.where(on, dam, 0.0), axis=-1, keepdims=True)
                am = jnp.where(on, jnp.sum(qc * kc, axis=-1, keepdims=True), 0.0)
                dq = dq_c[c] + da * kc
                dk = dk_c[c] + da * qc
                for d in range(1, HGRN_CHUNK):
                    on = lane == here - d
                    e = jnp.exp(jnp.where(trow >= d, bc - _shift_down(bc, d), MASK_VALUE))
                    kse = _shift_down(kc, d) * e
                    am = jnp.where(on, jnp.sum(qc * kse, axis=-1, keepdims=True), am)
                    da = jnp.sum(jnp.where(on, dam, 0.0), axis=-1, keepdims=True)
                    dq = dq + da * kse
                    dk = dk + _shift_up(da * (qc * e), d)
                rows[cc] = am
                dq_c[c], dk_c[c] = dq, dk
                db_c[c] = (qc * dq - kc * dk) + jnp.where(last_row, dbl_c[c], 0.0)
            dv_blk = lax.dot_general(jnp.concatenate(rows, axis=0).astype(BF16), dob[blk], _TN, preferred_element_type=F32)
            for cc in range(CHUNKS_PER_BLOCK):
                c = j * CHUNKS_PER_BLOCK + cc
                dv_c[c] = dv_c[c] + dv_blk[cc * HGRN_CHUNK:(cc + 1) * HGRN_CHUNK]
        dq = jnp.concatenate(dq_c, axis=0)
        dk = jnp.concatenate(dk_c, axis=0)
        dv = jnp.concatenate(dv_c, axis=0)
        db = jnp.concatenate(db_c, axis=0)
        dg = _chunk_sum(db, True)
        dhq_ref[...] = (dq * (sq * (1.0 + hq * (1.0 - sq)))).astype(BF16)
        dhi_ref[...] = dv.astype(BF16)
        dfk = dg / f - dk
        dhf_ref[...] = ((dfk * (1.0 - lb)) * (sg * (1.0 - sg))).astype(BF16)
        part = jnp.sum(dfk * (1.0 - sg), axis=0, keepdims=True)

        @pl.when(first)
        def _():
            dlb_ref[...] = part

        @pl.when(jnp.logical_not(first))
        def _():
            dlb_ref[...] += part

    hp = HGRN_HEADS_PER_STEP
    out_col = pl.BlockSpec((t, HGRN_STEP_WIDTH), lambda h, i: (nt - 1 - i, h))
    return _pc(body, name=name, grid=(HGRN_HEADS // hp, nt),
               in_specs=[col(COL_HQ), col(COL_HF), col(COL_HI), pl.BlockSpec((2, HGRN_STEP_WIDTH), lambda h, i: (0, h)),
                         pl.BlockSpec((nc, hp, LANES, LANES), lambda h, i: (nt - 1 - i, h, 0, 0)), out_col],
               out_specs=[out_col, out_col, out_col, pl.BlockSpec((1, HGRN_STEP_WIDTH), lambda h, i: (0, h))],
               out_shape=[SDS((s, HGRN_WIDTH), BF16)] * 3 + [SDS((1, HGRN_WIDTH), F32)],
               scratch=[pltpu.VMEM((hp, LANES, LANES), F32)],
               sem=("parallel", "arbitrary"))(proj, proj, proj, lb_logits, states, do)


def mix_out_fwd(name, attn, o, proj, g_attn, g_hgrn):
    s = attn.shape[0]
    t = _tile(s, ROW_TILE_WIDE)
    half = HGRN_WIDTH // 2

    def body(a_ref, o_ref, hg0_ref, hg1_ref, ga_ref, gh_ref, c_ref):
        av = a_ref[...]
        c_ref[:, :ATTN_WIDTH] = ((av * _rstd(av)) * ga_ref[...]).astype(BF16)
        for j in range(HGRN_HEADS):
            cols = slice(j * LANES, (j + 1) * LANES)
            ov = o_ref[:, cols]
            hg_ref, hcols = (hg0_ref, cols) if j < 4 else (hg1_ref, slice((j - 4) * LANES, (j - 3) * LANES))
            hg = hg_ref[:, hcols]
            on = (ov * _rstd(ov)) * gh_ref[:, cols]
            c_ref[:, ATTN_WIDTH + j * LANES:ATTN_WIDTH + (j + 1) * LANES] = (on * (hg * _sigmoid(hg))).astype(BF16)

    return _pc(body, name=name, grid=(s // t,),
               in_specs=[_rows(t, ATTN_WIDTH), _rows(t, HGRN_WIDTH), _rows(t, half, COL_HG // 4), _rows(t, half, COL_HG // 4 + 1),
                         _fixed((1, ATTN_WIDTH)), _fixed((1, HGRN_WIDTH))],
               out_specs=_rows(t, D_MODEL), out_shape=SDS((s, D_MODEL), BF16), sem=("parallel",))(attn, o, proj, proj, g_attn, g_hgrn)


def mix_out_bwd(name, dcat, attn, o, proj, g_attn, g_hgrn):
    s = attn.shape[0]
    t = _tile(s, ROW_TILE_WIDE)
    half = HGRN_WIDTH // 2

    def body(dc_ref, a_ref, o_ref, hg0_ref, hg1_ref, ga_ref, gh_ref, da_ref, do_ref, dhg_ref, dga_ref, dgh_ref, pa_s, ph_s):
        av = a_ref[...]
        r = _rstd(av)
        xh = av * r
        dyv = dc_ref[:, :ATTN_WIDTH]
        dyg = dyv * ga_ref[...]
        da_ref[...] = r * (dyg - xh * jnp.mean(dyg * xh, axis=-1, keepdims=True))
        pa_s[...] = jnp.sum(dyv * xh, axis=0, keepdims=True)
        for j in range(HGRN_HEADS):
            cols = slice(j * LANES, (j + 1) * LANES)
            ov = o_ref[:, cols]
            hg_ref, hcols = (hg0_ref, cols) if j < 4 else (hg1_ref, slice((j - 4) * LANES, (j - 3) * LANES))
            hg = hg_ref[:, hcols]
            sg = _sigmoid(hg)
            r = _rstd(ov)
            xh = ov * r
            gain = gh_ref[:, cols]
            dh = dc_ref[:, ATTN_WIDTH + j * LANES:ATTN_WIDTH + (j + 1) * LANES]
            dhg_ref[:, cols] = ((dh * (xh * gain)) * (sg * (1.0 + hg * (1.0 - sg)))).astype(BF16)
            dyv = dh * (hg * sg)
            dyg = dyv * gain
            do_ref[:, cols] = r * (dyg - xh * jnp.mean(dyg * xh, axis=-1, keepdims=True))
            ph_s[:, cols] = jnp.sum(dyv * xh, axis=0, keepdims=True)

        @pl.when(pl.program_id(0) == 0)
        def _():
            dga_ref[...] = pa_s[...]
            dgh_ref[...] = ph_s[...]

        @pl.when(pl.program_id(0) > 0)
        def _():
            dga_ref[...] += pa_s[...]
            dgh_ref[...] += ph_s[...]

    return _pc(body, name=name, grid=(s // t,),
               in_specs=[_rows(t, D_MODEL), _rows(t, ATTN_WIDTH), _rows(t, HGRN_WIDTH), _rows(t, half, COL_HG // 4),
                         _rows(t, half, COL_HG // 4 + 1), _fixed((1, ATTN_WIDTH)), _fixed((1, HGRN_WIDTH))],
               out_specs=[_rows(t, ATTN_WIDTH), _rows(t, HGRN_WIDTH), _rows(t, HGRN_WIDTH), _fixed((1, ATTN_WIDTH)),
                          _fixed((1, HGRN_WIDTH))],
               out_shape=[SDS((s, ATTN_WIDTH), F32), SDS((s, HGRN_WIDTH), F32), SDS((s, HGRN_WIDTH), BF16),
                          SDS((1, ATTN_WIDTH), F32), SDS((1, HGRN_WIDTH), F32)],
               scratch=[pltpu.VMEM((1, ATTN_WIDTH), F32), pltpu.VMEM((1, HGRN_WIDTH), F32)],
               sem=("arbitrary",))(dcat, attn, o, proj, proj, g_attn, g_hgrn)


BIG = (("w_in", 2048, 1408, 0), ("w_out", 512, 2048, 1), ("w_ffn_gate", 2048, 1408, 0), ("w_ffn_up", 2048, 1408, 0),
       ("w_ffn_down", 1408, 2048, 1), ("w_ple_gate", 512, 2048, 1), ("w_ple_proj", 256, 512, 0))
BIG_BY_NAME = {spec[0]: spec for spec in BIG}
HBM_SPEC = pl.BlockSpec(memory_space=pltpu.HBM)
SEM_SPEC = pl.BlockSpec(memory_space=pltpu.SEMAPHORE)
TOKEN_SHAPE = (8, LANES)


def _split_call(body, *, name, in_specs, out_specs, out_shape, aliases):
    return pl.pallas_call(body, name=name, in_specs=in_specs, out_specs=out_specs, out_shape=out_shape,
                          input_output_aliases=aliases,
                          compiler_params=pltpu.CompilerParams(has_side_effects=pltpu.SideEffectType.DATAFLOW_SIDE_EFFECTING))


def _in_hbm(arrays):
    return [pltpu.with_memory_space_constraint(a, pltpu.HBM) for a in arrays]


GROUP_STEPS = 8


def _step_rows(rows):
    assert rows % (GROUP_STEPS * 16) == 0
    return rows // GROUP_STEPS


def cast_to_slots(name, place, ws, layer, after):
    nt = len(ws)

    def body(place_ref, *refs):
        for t in range(nt):
            refs[nt + 1 + t][...] = refs[t][...].astype(BF16)

    in_specs, out_specs = [], []
    for w in ws:
        block = (None, _step_rows(w.shape[1]), w.shape[2])
        in_specs.append(pl.BlockSpec(block, lambda i, pr: (layer, i, 0)))
        out_specs.append(pl.BlockSpec(block, lambda i, pr: (pr[1], i, 0)))
    gs = pltpu.PrefetchScalarGridSpec(num_scalar_prefetch=1, grid=(GROUP_STEPS,), in_specs=in_specs + [ANY_SPEC],
                                      out_specs=out_specs)
    return _pc(body, name=name, grid_spec=gs, in_specs=None, out_specs=None,
               out_shape=[SDS((N_CHIPS,) + w.shape[1:], BF16) for w in ws], sem=("parallel",))(place, *ws, after)


def _place():
    x, y, c = lax.axis_index("x"), lax.axis_index("y"), lax.axis_index("c")
    chips = [(1 - x, y), (x, 1 - y), (1 - x, 1 - y)]
    return x, y, c, chips


def _half(ref, axis, c, rows, cols):
    if axis == 0:
        return ref.at[pl.ds(pl.multiple_of(c * (rows // 2), 16), rows // 2), :]
    return ref.at[:, pl.ds(pl.multiple_of(c * (cols // 2), LANES), cols // 2)]


def _gather_copies(specs, bufs, send, recv):
    x, y, c, chips = _place()
    cps = []
    for t, (_, rows, cols, axis) in enumerate(specs):
        mine = _half(bufs[t].at[2 * x + y], axis, c, rows, cols)
        for k, (cx, cy) in enumerate(chips):
            cps.append(pltpu.make_async_remote_copy(src_ref=mine, dst_ref=mine, send_sem=send.at[3 * t + k],
                                                    recv_sem=recv.at[3 * t + k], device_id=(cx, cy, c), device_id_type=MESH))
    return cps


def gather_start(name, specs, bufs, after):
    nt = len(bufs)
    n = 3 * nt

    def body(*refs):
        send, recv, token = refs[nt + 1], refs[nt + 2], refs[-1]
        for cp in _gather_copies(specs, refs[:nt], send, recv):
            cp.start()
        token[...] = jnp.zeros(TOKEN_SHAPE, F32)

    out = _split_call(
        body, name=name, in_specs=[HBM_SPEC] * nt + [ANY_SPEC],
        out_specs=(SEM_SPEC, SEM_SPEC) + (HBM_SPEC,) * nt + (pl.BlockSpec(memory_space=pltpu.VMEM),),
        out_shape=(pltpu.SemaphoreType.DMA((n,)), pltpu.SemaphoreType.DMA((n,)))
        + tuple(pltpu.HBM(b.shape, b.dtype) for b in bufs) + (SDS(TOKEN_SHAPE, F32),),
        aliases={t: 2 + t for t in range(nt)})(*_in_hbm(bufs), after)
    return out[0], out[1], list(out[2:2 + nt]), out[-1]


def gather_wait(name, specs, send, recv, bufs, after):
    nt = len(bufs)

    def body(*refs):
        for cp in _gather_copies(specs, refs[:nt], refs[nt], refs[nt + 1]):
            cp.wait_send()
            cp.wait_recv()

    out = _split_call(
        body, name=name, in_specs=[HBM_SPEC] * nt + [SEM_SPEC, SEM_SPEC, pl.BlockSpec(memory_space=pl.ANY)],
        out_specs=(HBM_SPEC,) * nt, out_shape=tuple(pltpu.HBM(b.shape, b.dtype) for b in bufs),
        aliases={t: t for t in range(nt)})(*bufs, send, recv, after)
    return list(out)


def gather_pass(name, specs, bufs):
    nt = len(bufs)

    def body(*refs):
        ins, outs = refs[:nt], refs[nt:2 * nt]
        send, recv = refs[2 * nt:]
        x, y, c, chips = _place()
        cps = []
        for t, (_, rows, cols, axis) in enumerate(specs):
            for k, (cx, cy) in enumerate(chips):
                cp = pltpu.make_async_remote_copy(
                    src_ref=_half(ins[t].at[2 * cx + cy], axis, c, rows, cols),
                    dst_ref=_half(outs[t].at[2 * cx + cy], axis, c, rows, cols),
                    send_sem=send.at[3 * t + k], recv_sem=recv.at[3 * t + k], device_id=(x, y, 1 - c), device_id_type=MESH)
                cp.start()
                cps.append(cp)
        for t, (_, rows, cols, axis) in enumerate(specs):
            for k, (cx, cy) in enumerate(chips):
                theirs = _half(outs[t].at[2 * cx + cy], axis, 1 - c, rows, cols)
                pltpu.make_async_remote_copy(src_ref=theirs, dst_ref=theirs, send_sem=send.at[3 * t + k],
                                             recv_sem=recv.at[3 * t + k], device_id=(x, y, 1 - c), device_id_type=MESH).wait_recv()
        for cp in cps:
            cp.wait_send()

    return _pc(body, name=name, in_specs=[HBM_SPEC] * nt, out_specs=[HBM_SPEC] * nt,
               out_shape=[SDS(b.shape, b.dtype) for b in bufs], scratch=[pltpu.SemaphoreType.DMA((3 * nt,))] * 2,
               input_output_aliases={t: t for t in range(nt)})(*bufs)


def reduce_to_sibling(name, grads):
    nt = len(grads)

    def body(*refs):
        srcs, dsts = refs[:nt], refs[nt:2 * nt]
        send, recv = refs[2 * nt:]
        x, y, c, _ = _place()
        cps = []
        for t in range(nt):
            cp = pltpu.make_async_remote_copy(src_ref=srcs[t].at[1 - c], dst_ref=dsts[t], send_sem=send.at[t],
                                              recv_sem=recv.at[t], device_id=(x, y, 1 - c), device_id_type=MESH)
            cp.start()
            cps.append(cp)
        for cp in cps:
            cp.wait()

    return _pc(body, name=name, in_specs=[HBM_SPEC] * nt, out_specs=[HBM_SPEC] * nt,
               out_shape=[SDS(g.shape[1:], g.dtype) for g in grads],
               scratch=[pltpu.SemaphoreType.DMA((nt,))] * 2)(*grads)


def _sibling_copies(grads, lands, send, recv):
    x, y, c, _ = _place()
    return [pltpu.make_async_remote_copy(src_ref=grads[t].at[1 - c], dst_ref=lands[t], send_sem=send.at[t], recv_sem=recv.at[t],
                                         device_id=(x, y, 1 - c), device_id_type=MESH) for t in range(len(grads))]


def sibling_start(name, grads):
    nt = len(grads)
    lands = [lax.empty(g.shape[1:], g.dtype) for g in grads]

    def body(*refs):
        send, recv, token = refs[2 * nt], refs[2 * nt + 1], refs[-1]
        for cp in _sibling_copies(refs[:nt], refs[nt:2 * nt], send, recv):
            cp.start()
        token[...] = jnp.zeros(TOKEN_SHAPE, F32)

    both = list(grads) + lands
    out = _split_call(
        body, name=name, in_specs=[HBM_SPEC] * (2 * nt),
        out_specs=(SEM_SPEC, SEM_SPEC) + (HBM_SPEC,) * (2 * nt) + (pl.BlockSpec(memory_space=pltpu.VMEM),),
        out_shape=(pltpu.SemaphoreType.DMA((nt,)), pltpu.SemaphoreType.DMA((nt,)))
        + tuple(pltpu.HBM(b.shape, b.dtype) for b in both) + (SDS(TOKEN_SHAPE, F32),),
        aliases={t: 2 + t for t in range(2 * nt)})(*_in_hbm(both))
    return out[0], out[1], list(out[2:2 + nt]), list(out[2 + nt:2 + 2 * nt]), out[-1]


def sibling_wait(name, send, recv, grads, lands, after):
    nt = len(grads)

    def body(*refs):
        for cp in _sibling_copies(refs[:nt], refs[nt:2 * nt], refs[2 * nt], refs[2 * nt + 1]):
            cp.wait_send()
            cp.wait_recv()

    both = list(grads) + list(lands)
    out = _split_call(
        body, name=name, in_specs=[HBM_SPEC] * (2 * nt) + [SEM_SPEC, SEM_SPEC, pl.BlockSpec(memory_space=pl.ANY)],
        out_specs=(HBM_SPEC,) * (2 * nt), out_shape=tuple(pltpu.HBM(b.shape, b.dtype) for b in both),
        aliases={t: t for t in range(2 * nt)})(*both, send, recv, after)
    return list(out[:nt]), list(out[nt:])


def add_halves(name, place, grads, gots):
    nt = len(grads)

    def body(place_ref, *refs):
        for t in range(nt):
            val = (refs[t][...].astype(F32) + refs[nt + t][...].astype(F32)).astype(BF16)
            refs[2 * nt + t][...] = val

            @pl.when(pl.program_id(1) == place_ref[1])
            def _():
                refs[3 * nt + t][...] = val

    g_specs, o_specs, part_specs, slot_specs = [], [], [], []
    for g in grads:
        _, n, r, c = g.shape
        tr = _step_rows(r)
        g_specs.append(pl.BlockSpec((None, None, tr, c), lambda i, j, pr: (pr[0], j, i, 0)))
        o_specs.append(pl.BlockSpec((None, tr, c), lambda i, j, pr: (j, i, 0)))
        part_specs.append(pl.BlockSpec((None, tr, c), lambda i, j, pr: (j, i, 0)))
        slot_specs.append(pl.BlockSpec((None, tr, c), lambda i, j, pr: (pr[1], i, 0)))
    gs = pltpu.PrefetchScalarGridSpec(num_scalar_prefetch=1, grid=(GROUP_STEPS, N_CHIPS), in_specs=g_specs + o_specs,
                                      out_specs=part_specs + slot_specs)
    out = _pc(body, name=name, grid_spec=gs, in_specs=None, out_specs=None,
              out_shape=[SDS(g.shape[1:], BF16) for g in grads] * 2, sem=("parallel", "arbitrary"))(place, *grads, *gots)
    return list(out[:nt]), list(out[nt:])


def _chips_copies(parts, slots, send, recv):
    x, y, c, chips = _place()
    cps = []
    for t in range(len(parts)):
        for k, (cx, cy) in enumerate(chips):
            cps.append(pltpu.make_async_remote_copy(src_ref=parts[t].at[2 * cx + cy], dst_ref=slots[t].at[2 * x + y],
                                                    send_sem=send.at[3 * t + k], recv_sem=recv.at[3 * t + k],
                                                    device_id=(cx, cy, c), device_id_type=MESH))
    return cps


def chips_start(name, parts, slots):
    nt = len(parts)
    n = 3 * nt

    def body(*refs):
        send, recv, token = refs[2 * nt], refs[2 * nt + 1], refs[-1]
        for cp in _chips_copies(refs[:nt], refs[nt:2 * nt], send, recv):
            cp.start()
        token[...] = jnp.zeros(TOKEN_SHAPE, F32)

    both = list(parts) + list(slots)
    out = _split_call(
        body, name=name, in_specs=[HBM_SPEC] * (2 * nt),
        out_specs=(SEM_SPEC, SEM_SPEC) + (HBM_SPEC,) * (2 * nt) + (pl.BlockSpec(memory_space=pltpu.VMEM),),
        out_shape=(pltpu.SemaphoreType.DMA((n,)), pltpu.SemaphoreType.DMA((n,)))
        + tuple(pltpu.HBM(b.shape, b.dtype) for b in both) + (SDS(TOKEN_SHAPE, F32),),
        aliases={t: 2 + t for t in range(2 * nt)})(*_in_hbm(both))
    return out[0], out[1], list(out[2:2 + nt]), list(out[2 + nt:2 + 2 * nt]), out[-1]


def chips_wait(name, send, recv, parts, slots, after):
    nt = len(parts)

    def body(*refs):
        for cp in _chips_copies(refs[:nt], refs[nt:2 * nt], refs[2 * nt], refs[2 * nt + 1]):
            cp.wait_send()
            cp.wait_recv()

    both = list(parts) + list(slots)
    out = _split_call(
        body, name=name, in_specs=[HBM_SPEC] * (2 * nt) + [SEM_SPEC, SEM_SPEC, pl.BlockSpec(memory_space=pl.ANY)],
        out_specs=(HBM_SPEC,) * (2 * nt), out_shape=tuple(pltpu.HBM(b.shape, b.dtype) for b in both),
        aliases={t: t for t in range(2 * nt)})(*both, send, recv, after)
    return list(out[nt:])


def sum_chips(name, place, slots):
    nt = len(slots)

    def body(place_ref, *refs):
        for t in range(nt):
            s_ref = refs[t]
            acc = s_ref[0].astype(F32)
            for k in range(1, N_CHIPS):
                acc = acc + s_ref[k].astype(F32)
            refs[nt + t][...] = acc

    in_specs, out_specs = [], []
    for sl in slots:
        n, r, c = sl.shape
        tr = _step_rows(r)
        in_specs.append(pl.BlockSpec((n, tr, c), lambda i, pr: (0, i, 0)))
        out_specs.append(pl.BlockSpec((None, tr, c), lambda i, pr: (pr[0], i, 0)))
    gs = pltpu.PrefetchScalarGridSpec(num_scalar_prefetch=1, grid=(GROUP_STEPS,), in_specs=in_specs, out_specs=out_specs)
    return list(_pc(body, name=name, grid_spec=gs, in_specs=None, out_specs=None,
                    out_shape=[SDS((2,) + sl.shape[1:], F32) for sl in slots], sem=("parallel",))(place, *slots))


def share_with_sibling(name, bufs):
    nt = len(bufs)

    def body(*refs):
        ins, outs = refs[:nt], refs[nt:2 * nt]
        send, recv = refs[2 * nt:]
        x, y, c, _ = _place()
        cps = []
        for t in range(nt):
            cp = pltpu.make_async_remote_copy(src_ref=ins[t].at[c], dst_ref=outs[t].at[c], send_sem=send.at[t], recv_sem=recv.at[t],
                                              device_id=(x, y, 1 - c), device_id_type=MESH)
            cp.start()
            cps.append(cp)
        for t in range(nt):
            theirs = outs[t].at[1 - c]
            pltpu.make_async_remote_copy(src_ref=theirs, dst_ref=theirs, send_sem=send.at[t], recv_sem=recv.at[t],
                                         device_id=(x, y, 1 - c), device_id_type=MESH).wait_recv()
        for cp in cps:
            cp.wait_send()

    return _pc(body, name=name, in_specs=[HBM_SPEC] * nt, out_specs=[HBM_SPEC] * nt,
               out_shape=[SDS(b.shape, F32) for b in bufs], scratch=[pltpu.SemaphoreType.DMA((nt,))] * 2,
               input_output_aliases={t: t for t in range(nt)})(*bufs)


def _adamw(w, g, m, v):
    m = ADAM_B1 * m + (1.0 - ADAM_B1) * g
    v = ADAM_B2 * v + (1.0 - ADAM_B2) * (g * g)
    m_hat = m / (1.0 - ADAM_B1 ** ADAM_STEP)
    v_hat = v / (1.0 - ADAM_B2 ** ADAM_STEP)
    delta = -ADAM_LR * (m_hat / (jnp.sqrt(v_hat) + ADAM_EPS) + ADAM_WD * w)
    return delta, m, v


def adamw_big(name, w, m, v, g0, g1, axis):
    _, r, c = w.shape
    _, rh, ch = g0.shape
    tr = _tile(rh, 256)
    nb = rh // tr
    if axis == 0:
        wspec = pl.BlockSpec((None, tr, ch), lambda l, h, i: (l, h * nb + i, 0))
    else:
        wspec = pl.BlockSpec((None, tr, ch), lambda l, h, i: (l, i, h))
    g0spec = pl.BlockSpec((None, tr, ch), lambda l, h, i: (h * (1 - l), i * (1 - l), 0))
    g1spec = pl.BlockSpec((None, tr, ch), lambda l, h, i: (h * l, i * l, 0))

    def body(w_ref, m_ref, v_ref, g0_ref, g1_ref, go_ref, d_ref, mo_ref, vo_ref):
        def run(g_ref):
            g = g_ref[...]
            delta, mn, vn = _adamw(w_ref[...], g, m_ref[...], v_ref[...])
            go_ref[...] = g
            d_ref[...] = delta
            mo_ref[...] = mn
            vo_ref[...] = vn

        @pl.when(pl.program_id(0) == 0)
        def _():
            run(g0_ref)

        @pl.when(pl.program_id(0) == 1)
        def _():
            run(g1_ref)

    return _pc(body, name=name, grid=(2, 2, nb), in_specs=[wspec, wspec, wspec, g0spec, g1spec], out_specs=[wspec] * 4,
               out_shape=[SDS(w.shape, F32)] * 4, sem=("parallel", "parallel", "parallel"))(w, m, v, g0, g1)


SMALL = (("pre_mix_gain", 2048), ("post_mix_gain", 2048), ("pre_ffn_gain", 2048), ("post_ffn_gain", 2048), ("ple_gain", 2048),
         ("attn_out_gain", 1024), ("hgrn_out_gain", 1024), ("hgrn_lb_logits", 1024), ("attn_sinks", 128))
SMALL_ROWS = sum(2 * w // LANES for _, w in SMALL)
SMALL_PAD = -(-SMALL_ROWS // 8) * 8
LB_ROW = sum(2 * w // LANES for _, w in SMALL[:7])


def _pack_small(parts):
    rows = []
    for nm, w in SMALL:
        a = parts[nm].astype(F32)
        if a.shape[1] != w:
            a = jnp.pad(a, ((0, 0), (0, w - a.shape[1])))
        rows.append(a.reshape(2 * w // LANES, LANES))
    rows.append(jnp.zeros((SMALL_PAD - SMALL_ROWS, LANES), F32))
    return jnp.concatenate(rows, axis=0)


def _unpack_small(packed, widths):
    out, r = {}, 0
    for nm, w in SMALL:
        n = 2 * w // LANES
        out[nm] = packed[r:r + n].reshape(2, w)[:, :widths[nm]]
        r += n
    return out


def allreduce_small(name, packed):
    rows = packed.shape[0]

    def body(x_ref, o_ref, buf, send, recv, own_sem):
        x, y, c, _ = _place()
        me = 4 * x + 2 * y + c
        own = pltpu.make_async_copy(x_ref, buf.at[me], own_sem)
        own.start()
        cps = []
        for k in range(1, 8):
            px, py, pc = x ^ (k >> 2), y ^ ((k >> 1) & 1), c ^ (k & 1)
            cp = pltpu.make_async_remote_copy(src_ref=x_ref, dst_ref=buf.at[me], send_sem=send.at[k - 1], recv_sem=recv.at[k - 1],
                                              device_id=(px, py, pc), device_id_type=MESH)
            cp.start()
            cps.append(cp)
        for k in range(1, 8):
            px, py, pc = x ^ (k >> 2), y ^ ((k >> 1) & 1), c ^ (k & 1)
            slot = buf.at[4 * px + 2 * py + pc]
            pltpu.make_async_remote_copy(src_ref=slot, dst_ref=slot, send_sem=send.at[k - 1], recv_sem=recv.at[k - 1],
                                         device_id=(px, py, pc), device_id_type=MESH).wait_recv()
        for cp in cps:
            cp.wait_send()
        own.wait()
        acc = buf[0]
        for k in range(1, 8):
            acc = acc + buf[k]
        o_ref[...] = acc

    vm = pl.BlockSpec(memory_space=pltpu.VMEM)
    return _pc(body, name=name, in_specs=[vm], out_specs=vm, out_shape=SDS((rows, LANES), F32),
               scratch=[pltpu.VMEM((8, rows, LANES), F32), pltpu.SemaphoreType.DMA((7,)), pltpu.SemaphoreType.DMA((7,)),
                        pltpu.SemaphoreType.DMA])(packed)


def adamw_small(name, w, m, v, g):
    rows = w.shape[0]
    n = HGRN_WIDTH // LANES

    def body(w_ref, m_ref, v_ref, g_ref, go_ref, d_ref, mo_ref, vo_ref):
        go_ref[...] = g_ref[...]
        l0 = w_ref[LB_ROW:LB_ROW + n, :]
        l1 = w_ref[LB_ROW + n:LB_ROW + 2 * n, :]
        mx = jnp.maximum(l0, l1)
        e0, e1 = jnp.exp(l0 - mx), jnp.exp(l1 - mx)
        s0, s1 = e0 / (e0 + e1), e1 / (e0 + e1)
        dlb1 = g_ref[LB_ROW + n:LB_ROW + 2 * n, :]
        inner = s1 * dlb1
        go_ref[LB_ROW:LB_ROW + n, :] = s0 * (0.0 - inner)
        go_ref[LB_ROW + n:LB_ROW + 2 * n, :] = s1 * (dlb1 - inner)
        delta, mn, vn = _adamw(w_ref[...], go_ref[...], m_ref[...], v_ref[...])
        d_ref[...] = delta
        mo_ref[...] = mn
        vo_ref[...] = vn

    vm = pl.BlockSpec(memory_space=pltpu.VMEM)
    return _pc(body, name=name, in_specs=[vm] * 4, out_specs=[vm] * 4, out_shape=[SDS((rows, LANES), F32)] * 4)(w, m, v, g)


def _layer_fwd(l, x, h1, p_l, w_in_g, rest_of_weights, gains, cos, sin, sinks, lb_logits, g_next, target):
    n = f"l{l}_"
    proj = mm_col(n + "in_proj", h1, w_in_g)
    qp, kp, vp = rope_qkv(n + "rope_qkv", proj, cos, sin)
    attn = attn_fwd(n + "attn_fwd", qp, kp, vp, sinks)
    o, states = hgrn_fwd(n + "hgrn_fwd", proj, lb_logits, l)
    cat = mix_out_fwd(n + "mix_out_fwd", attn, o, proj, gains["attn_out_gain"], gains["hgrn_out_gain"])
    rest, token = rest_of_weights("mix", cat)
    wts = dict(rest, w_in=w_in_g)
    if token is not None:
        gains = _with_token(gains, "post_mix_gain", token)
    m, x1, h2 = out_proj_post_mix(n + "out_proj_post_mix", cat, wts["w_out"], gains["post_mix_gain"], x, gains["pre_ffn_gain"])
    g, u, a = ffn_gate_up(n + "ffn_gate_up", h2, wts["w_ffn_gate"], wts["w_ffn_up"])
    wts.update(rest_of_weights("ffn", a)[0])
    f = mm_row(n + "ffn_down", a, wts["w_ffn_down"])
    x2, h3 = post_pre_norm(n + "post_ffn", f, gains["post_ffn_gain"], x1, gains["ple_gain"])
    saved = dict(x=x, h1=h1, proj=proj, qp=qp, kp=kp, vp=vp, attn=attn, o=o, states=states, cat=cat, m=m, x1=x1, h2=h2,
                 g=g, u=u, a=a, f=f, x2=x2, h3=h3, p=p_l)
    if target is None:
        z, pp, *out = ple_gate_fwd_mid(n + "ple_gate_fwd", h3, wts["w_ple_gate"], p_l, wts["w_ple_proj"], x2, g_next)
        saved.update(z=z, pp=pp)
    else:
        dy, dpp, dz, loss = ple_gate_fwd_loss(n + "ple_gate_loss", h3, wts["w_ple_gate"], p_l, wts["w_ple_proj"], x2, target)
        out = [dy, loss]
        saved.update(dpp=dpp, dz=dz)
    return out, saved, wts


EARLY = ("w_ple_gate", "w_ple_proj", "w_ffn_down", "w_ffn_gate", "w_ffn_up")
LATE = ("w_out", "w_in")


def _layer_bwd_ffn(l, dx3, sv, wts, gains, after=None, hook=None):
    n = f"l{l}_"
    dpp, dz = (sv["dpp"], sv["dz"]) if "dz" in sv else ple_bwd(n + "ple_bwd", dx3, sv["z"], sv["pp"],
                                                                 dx3 if after is None else after)
    dh3 = mm_row_t(n + "ple_gate_dx", dz, wts["w_ple_gate"])
    if hook is not None:
        gains = _with_token(gains, "ple_gain", hook(dh3))
    dx2, df, d_ple_gain, d_post_ffn = norm_bwd_pair(n + "ple_post_ffn_bwd", sv["x2"], gains["ple_gain"], dh3, dx3, sv["f"],
                                                    gains["post_ffn_gain"])
    dg, du = ffn_down_bwd(n + "ffn_down_bwd", df, wts["w_ffn_down"], sv["g"], sv["u"])
    big = dict(
        w_ple_gate=mm_wg_row(n + "ple_gate_dw", sv["h3"], dz),
        w_ple_proj=mm_wg_col(n + "ple_proj_dw", sv["p"], dpp),
        w_ffn_down=mm_wg_row(n + "ffn_down_dw", sv["a"], df),
        w_ffn_gate=mm_wg_col(n + "ffn_gate_dw", sv["h2"], dg),
        w_ffn_up=mm_wg_col(n + "ffn_up_dw", sv["h2"], du),
    )
    return dict(dx2=dx2, dg=dg, du=du), big, dict(ple_gain=d_ple_gain, post_ffn_gain=d_post_ffn)


def _layer_bwd_mix(l, st, sv, wts, gains, cos, sin, sinks, lb_logits, after=None, hook=None):
    n = f"l{l}_"
    dh2 = mm_col_t(n + "ffn_gate_dx", st["dg"], wts["w_ffn_gate"], after=after)
    dh2 = mm_col_t(n + "ffn_up_dx", st["du"], wts["w_ffn_up"], add=dh2)
    if hook is not None:
        gains = _with_token(gains, "pre_ffn_gain", hook(dh2))
    dx1, dm, d_pre_ffn, d_post_mix = norm_bwd_pair(n + "pre_ffn_post_mix_bwd", sv["x1"], gains["pre_ffn_gain"], dh2, st["dx2"],
                                                   sv["m"], gains["post_mix_gain"])
    dcat = mm_row_t(n + "out_proj_dx", dm, wts["w_out"])
    dattn, do, dhg, d_attn_gain, d_hgrn_gain = mix_out_bwd(n + "mix_out_bwd", dcat, sv["attn"], sv["o"], sv["proj"],
                                                            gains["attn_out_gain"], gains["hgrn_out_gain"])
    dqp, dkc, dkp, dvc, dvp, dsinks = attn_bwd(n + "attn_bwd", sv["qp"], sv["kp"], sv["vp"], sinks, dattn)
    dqkv = rope_bwd(n + "rope_bwd", dqp, dkc, dkp, dvc, dvp, cos, sin)
    dhq, dhf, dhi, dlb = hgrn_bwd(n + "hgrn_bwd", sv["proj"], lb_logits, l, sv["states"], do)
    dproj, dh1 = in_proj_bwd(n + "in_proj_dx", [dqkv, dhq, dhf, dhi, dhg], wts["w_in"])
    dx, d_pre_mix = norm_bwd(n + "pre_mix_bwd", sv["x"], gains["pre_mix_gain"], dh1, dx1)
    big = dict(w_out=mm_wg_row(n + "out_proj_dw", sv["cat"], dm), w_in=mm_wg_col(n + "in_proj_dw", sv["h1"], dproj))
    small = dict(pre_mix_gain=d_pre_mix, post_mix_gain=d_post_mix, pre_ffn_gain=d_pre_ffn, attn_out_gain=d_attn_gain,
                 hgrn_out_gain=d_hgrn_gain, hgrn_lb_logits=dlb, attn_sinks=dsinks)
    return dx, big, small


def _reduce_start(tag, names, big, place):
    grads = [big[nm] for nm in names]
    got = reduce_to_sibling(tag + "_reduce_to_sibling", grads)
    parts, slots = add_halves(tag + "_add", place, grads, got)
    return chips_start(tag + "_chips_start", parts, slots)


def _reduce_begin(tag, names, big):
    return sibling_start(tag + "_sibling_start", [big[nm] for nm in names])


def _reduce_chips(tag, names, begun, place, after):
    send, recv, grads, lands, _ = begun
    grads, got = sibling_wait(tag + "_sibling_wait", send, recv, grads, lands, after)
    parts, slots = add_halves(tag + "_add", place, grads, got)
    return chips_start(tag + "_chips_start", parts, slots)


def _reduce_finish(tag, names, started, place, after):
    send, recv, parts, slots, _ = started
    slots = chips_wait(tag + "_chips_wait", send, recv, parts, slots, after)
    bufs = sum_chips(tag + "_sum", place, slots)
    return dict(zip(names, share_with_sibling(tag + "_share_with_sibling", bufs)))


def _reduce_finish_both(first, second, place, after):
    slots = []
    for tag, _, (send, recv, parts, slots_t, _) in (first, second):
        slots += chips_wait(tag + "_chips_wait", send, recv, parts, slots_t, after)
    tag = first[0] + "_" + second[0]
    shared = share_with_sibling(tag + "_share_with_sibling", sum_chips(tag + "_sum", place, slots))
    n1 = len(first[1])
    return dict(zip(first[1], shared[:n1])), dict(zip(second[1], shared[n1:]))


def _with_token(gains, name, token):
    out = dict(gains)
    out[name] = gains[name] + token[0, 0]
    return out


def kernel(x, p, positions, w_in, attn_sinks, hgrn_lb_logits, attn_out_gain, hgrn_out_gain, w_out, pre_mix_gain, post_mix_gain, pre_ffn_gain, post_ffn_gain, w_ffn_gate, w_ffn_up, w_ffn_down, ple_gain, w_ple_gate, w_ple_proj, loss_target, m_w_in, m_attn_sinks, m_hgrn_lb_logits, m_attn_out_gain, m_hgrn_out_gain, m_w_out, m_pre_mix_gain, m_post_mix_gain, m_pre_ffn_gain, m_post_ffn_gain, m_w_ffn_gate, m_w_ffn_up, m_w_ffn_down, m_ple_gain, m_w_ple_gate, m_w_ple_proj, v_w_in, v_attn_sinks, v_hgrn_lb_logits, v_attn_out_gain, v_hgrn_out_gain, v_w_out, v_pre_mix_gain, v_post_mix_gain, v_pre_ffn_gain, v_post_ffn_gain, v_w_ffn_gate, v_w_ffn_up, v_w_ffn_down, v_ple_gain, v_w_ple_gate, v_w_ple_proj):
    given = dict(locals())
    depth = 2
    place = jnp.stack([lax.axis_index("c"), 2 * lax.axis_index("x") + lax.axis_index("y")]).astype(jnp.int32)
    xs = x[0]
    tgt = loss_target[0]
    pos_col = positions.reshape(-1, 1)
    half = 32
    inv_freq = ROPE_THETA ** (-jnp.arange(half, dtype=F32) / half)
    inv_freq = jnp.tile(inv_freq, 4).reshape(1, LANES)
    gains = [{nm: given[nm][l:l + 1] for nm, _ in SMALL[:7]} for l in range(depth)]
    names = [nm for nm, *_ in BIG]
    first, others = names[:1], names[1:]

    def specs(nms):
        return [BIG_BY_NAME[nm] for nm in nms]

    def cast(tag, l, nms, after):
        return cast_to_slots(tag + "_cast", place, [given[nm] for nm in nms], l, after)

    def finish_gather(tag, nms, started, after):
        bufs = gather_wait(tag + "_gather_wait", specs(nms), started[0], started[1], started[2], after)
        return dict(zip(nms, gather_pass(tag + "_gather_pass", specs(nms), bufs)))

    mid, last = names[1:4], names[4:]
    g0a = gather_start("l0a_gather_start", specs(first), cast("l0a", 0, first, place), place)
    g0b = gather_start("l0b_gather_start", specs(mid), cast("l0b", 0, mid, g0a[3]), g0a[3])
    g0c = gather_start("l0c_gather_start", specs(last), cast("l0c", 0, last, g0b[3]), g0b[3])
    started = {}

    def rest_of_layer0(stage, after):
        if stage == "ffn":
            return finish_gather("l0c", last, g0c, after), None
        got = finish_gather("l0b", mid, g0b, after)
        started["l1a"] = gather_start("l1a_gather_start", specs(first), l1_shards[:1], got["w_out"])
        started["l1b"] = gather_start("l1b_gather_start", specs(others), l1_shards[1:], started["l1a"][3])
        return got, started["l1b"][3]

    def rest_of_layer1(stage, after):
        return (finish_gather("l1b", others, started["l1b"], after) if stage == "mix" else {}), None

    cos, sin = rope_tables("rope_tables", pos_col, inv_freq)
    h1 = pre_norm("l0_pre_mix", xs, _with_token(gains[0], "pre_mix_gain", g0c[3])["pre_mix_gain"])
    l1_shards = cast("l1", 1, names, h1)
    w_in0 = finish_gather("l0a", first, g0a, l1_shards[0])["w_in"]
    (x_mid, h1_next), sv0, wts0 = _layer_fwd(0, xs, h1, p[0, 0], w_in0, rest_of_layer0, gains[0], cos, sin, attn_sinks[0],
                                             hgrn_lb_logits, gains[1]["pre_mix_gain"], None)
    w_in1 = finish_gather("l1a", first, started["l1a"], x_mid)["w_in"]
    (dy, loss_part), sv1, wts1 = _layer_fwd(1, x_mid, h1_next, p[1, 0], w_in1, rest_of_layer1, gains[1], cos, sin,
                                            attn_sinks[1], hgrn_lb_logits, None, tgt)

    st1, early1, small1 = _layer_bwd_ffn(1, dy, sv1, wts1, gains[1])
    dx_mid, late1, small1b = _layer_bwd_mix(1, st1, sv1, wts1, gains[1], cos, sin, attn_sinks[1], hgrn_lb_logits)
    big1, small1 = {**early1, **late1}, {**small1, **small1b}
    chips = {}

    def chips_after(tag, nms, begun):
        def hook(x):
            chips[tag] = _reduce_chips(tag, nms, begun, place, x)
            return chips[tag][4]
        return hook

    b1 = _reduce_begin("l1", names, big1)
    st0, early0, small0 = _layer_bwd_ffn(0, dx_mid, sv0, wts0, gains[0], after=b1[4], hook=chips_after("l1", names, b1))
    b0e = _reduce_begin("l0e", EARLY, early0)
    dx0, late0, small0b = _layer_bwd_mix(0, st0, sv0, wts0, gains[0], cos, sin, attn_sinks[0], hgrn_lb_logits, after=b0e[4],
                                         hook=chips_after("l0e", EARLY, b0e))
    r1, r0e = chips["l1"], chips["l0e"]
    small0 = {**small0, **small0b}
    r0l = _reduce_start("l0l", LATE, late0, place)
    red1, red0 = _reduce_finish_both(("l1", names, r1), ("l0e", EARLY, r0e), place, r0l[4])

    loss = lax.psum(loss_part[0, 0], ("x", "y", "c"))
    grad_x = dx0[None]

    out_big = {}
    for nm in EARLY:
        out_big[nm] = adamw_big("adamw_" + nm, given[nm], given["m_" + nm], given["v_" + nm], red0[nm], red1[nm], BIG_BY_NAME[nm][3])
    red0.update(_reduce_finish("l0l", LATE, r0l, place, out_big[EARLY[-1]][3]))
    for nm in LATE:
        out_big[nm] = adamw_big("adamw_" + nm, given[nm], given["m_" + nm], given["v_" + nm], red0[nm], red1[nm], BIG_BY_NAME[nm][3])

    widths = {nm: given[nm].shape[1] for nm, _ in SMALL}
    small_g = {nm: jnp.concatenate([small0[nm][:, :widths[nm]] if nm != "attn_sinks" else small0[nm][:, :LANES],
                                    small1[nm][:, :widths[nm]] if nm != "attn_sinks" else small1[nm][:, :LANES]], axis=0)
               for nm, _ in SMALL}
    g_sum = allreduce_small("allreduce_small", _pack_small(small_g))
    sm = adamw_small("adamw_small", _pack_small({nm: given[nm] for nm, _ in SMALL}),
                     _pack_small({nm: given["m_" + nm] for nm, _ in SMALL}),
                     _pack_small({nm: given["v_" + nm] for nm, _ in SMALL}), g_sum)
    out_small = [_unpack_small(a, widths) for a in sm]

    order = ["w_in", "attn_sinks", "hgrn_lb_logits", "attn_out_gain", "hgrn_out_gain", "w_out", "pre_mix_gain", "post_mix_gain",
             "pre_ffn_gain", "post_ffn_gain", "w_ffn_gate", "w_ffn_up", "w_ffn_down", "ple_gain", "w_ple_gate", "w_ple_proj"]
    res = [loss, grad_x]
    for k in range(4):
        for nm in order:
            res.append(out_big[nm][k] if nm in out_big else out_small[k][nm])
    return tuple(res)
```

```python
import jax
import jax.numpy as jnp
from jax import lax
from jax.experimental import pallas as pl
from jax.experimental.pallas import tpu as pltpu

F32, BF16 = jnp.float32, jnp.bfloat16
SDS = jax.ShapeDtypeStruct
MESH = pl.DeviceIdType.MESH

D_MODEL = 2048
ATTN_WIDTH = 1024
HGRN_WIDTH = 1024
KV_WIDTH = 256
N_Q_HEADS = 16
N_KV_HEADS = 4
Q_PER_KV = 4
WINDOW = 128
MASK_VALUE = -1e30
ROPE_THETA = 10000.0
HGRN_HEADS = 8
HGRN_CHUNK = 16
RMS_EPS = 1e-6
LANES = 128
N_CHIPS = 4
COL_Q, COL_K, COL_V, COL_HQ, COL_HF, COL_HI, COL_HG = 0, 8, 10, 12, 20, 28, 36

ADAM_LR, ADAM_B1, ADAM_B2, ADAM_EPS, ADAM_WD, ADAM_STEP = 0.001, 0.9, 0.999, 1e-08, 0.01, 10

VMEM_LIMIT = 56 * 1024 * 1024
ROW_TILE = 256
ROW_TILE_WIDE = 512

_NN = (((1,), (0,)), ((), ()))
_NT = (((1,), (1,)), ((), ()))
_TN = (((0,), (0,)), ((), ()))


def _pc(body, *, name, out_shape, in_specs, out_specs, grid=(), scratch=(), sem=None, grid_spec=None, **kw):
    params = dict(vmem_limit_bytes=VMEM_LIMIT)
    if sem is not None:
        params["dimension_semantics"] = sem
    if grid_spec is not None:
        return pl.pallas_call(body, name=name, out_shape=out_shape, grid_spec=grid_spec,
                              compiler_params=pltpu.CompilerParams(**params), **kw)
    return pl.pallas_call(body, name=name, out_shape=out_shape, grid=grid, in_specs=in_specs, out_specs=out_specs,
                          scratch_shapes=list(scratch), compiler_params=pltpu.CompilerParams(**params), **kw)


def _sigmoid(x):
    return 1.0 / (1.0 + jnp.exp(-x))


def _rstd(x):
    return lax.rsqrt(jnp.mean(x * x, axis=-1, keepdims=True) + RMS_EPS)


def _rows(t, w, col=0):
    return pl.BlockSpec((t, w), lambda i, col=col: (i, col))


def _fixed(shape):
    return pl.BlockSpec(shape, lambda *_: (0,) * len(shape))


def _mm(name, a, b, *, dims, grid, a_spec, b_spec, o_spec, out_shape, parts=1, add=None, add_spec=None, after=None):
    def body(*refs):
        a_ref, b_ref, o_ref = refs[0], refs[1], refs[-1]
        if parts == 1:
            r = lax.dot_general(a_ref[...].astype(BF16), b_ref[...].astype(BF16), dims, preferred_element_type=F32)
        else:
            w = a_ref.shape[1] // parts
            r = None
            for j in range(parts):
                t = lax.dot_general(a_ref[:, j * w:(j + 1) * w].astype(BF16), b_ref[j].astype(BF16), dims,
                                    preferred_element_type=F32)
                r = t if r is None else r + t
        if add is not None:
            r = r + refs[2][...]
        o_ref[...] = r.astype(o_ref.dtype)

    ins = [a, b] + ([] if add is None else [add]) + ([] if after is None else [after])
    specs = [a_spec, b_spec] + ([] if add is None else [add_spec]) + ([] if after is None else [pl.BlockSpec(memory_space=pl.ANY)])
    return _pc(body, name=name, grid=grid, in_specs=specs, out_specs=o_spec, out_shape=out_shape,
               sem=("parallel",) * len(grid))(*ins)


def _tile(n, t):
    if n <= t:
        return n
    while n % t:
        t //= 2
    assert t % 8 == 0
    return t


def mm_col(name, a, wg, out_dtype=F32):
    s, k = a.shape
    _, _, n = wg.shape
    tm = _tile(s, 1024)
    return _mm(name, a, wg, dims=_NN, grid=(N_CHIPS, s // tm),
               a_spec=pl.BlockSpec((tm, k), lambda j, i: (i, 0)),
               b_spec=pl.BlockSpec((None, k, n), lambda j, i: (j, 0, 0)),
               o_spec=pl.BlockSpec((tm, n), lambda j, i: (i, j)),
               out_shape=SDS((s, N_CHIPS * n), out_dtype))


def mm_row(name, a, wg, out_dtype=F32):
    s, _ = a.shape
    _, r, n = wg.shape
    tm = _tile(s, 512)
    tn = _tile(n, 1024 if r > 512 else 2048)
    return _mm(name, a, wg, dims=_NN, grid=(n // tn, s // tm), parts=N_CHIPS,
               a_spec=pl.BlockSpec((tm, N_CHIPS * r), lambda j, i: (i, 0)),
               b_spec=pl.BlockSpec((N_CHIPS, r, tn), lambda j, i: (0, 0, j)),
               o_spec=pl.BlockSpec((tm, tn), lambda j, i: (i, j)),
               out_shape=SDS((s, n), out_dtype))


def mm_col_t(name, dy, wg, add=None, out_dtype=F32, after=None):
    s, _ = dy.shape
    _, k, n = wg.shape
    tm = _tile(s, 1024)
    tk = _tile(k, 512)
    return _mm(name, dy, wg, dims=_NT, grid=(k // tk, s // tm), parts=N_CHIPS,
               a_spec=pl.BlockSpec((tm, N_CHIPS * n), lambda j, i: (i, 0)),
               b_spec=pl.BlockSpec((N_CHIPS, tk, n), lambda j, i: (0, j, 0)),
               o_spec=pl.BlockSpec((tm, tk), lambda j, i: (i, j)),
               add=add, add_spec=pl.BlockSpec((tm, tk), lambda j, i: (i, j)),
               out_shape=SDS((s, k), out_dtype), after=after)


def in_proj_bwd(name, pieces, wg):
    s = pieces[0].shape[0]
    _, k, n = wg.shape
    tm = _tile(s, 512)
    tk = _tile(k, 512)
    width = sum(p.shape[1] for p in pieces)
    npc = len(pieces)
    segments, start = [], 0
    for t, p in enumerate(pieces):
        lo = start
        while lo < start + p.shape[1]:
            hi = min(start + p.shape[1], (lo // n + 1) * n)
            segments.append((t, lo - start, hi - start, lo // n, lo % n, lo % n + hi - lo))
            lo = hi
        start += p.shape[1]

    def body(*refs):
        w_ref, dproj_ref, o_ref = refs[npc], refs[npc + 1], refs[npc + 2]

        @pl.when(pl.program_id(1) == 0)
        def _():
            off = 0
            for t in range(npc):
                dproj_ref[:, off:off + pieces[t].shape[1]] = refs[t][...]
                off += pieces[t].shape[1]

        r = None
        for t, a, b, j, c, d in segments:
            part = lax.dot_general(refs[t][:, a:b], w_ref[j, :, c:d], _NT, preferred_element_type=F32)
            r = part if r is None else r + part
        o_ref[...] = r

    return _pc(body, name=name, grid=(s // tm, k // tk),
               in_specs=[pl.BlockSpec((tm, p.shape[1]), lambda i, j: (i, 0)) for p in pieces]
               + [pl.BlockSpec((N_CHIPS, tk, n), lambda i, j: (0, j, 0))],
               out_specs=[pl.BlockSpec((tm, width), lambda i, j: (i, 0)), pl.BlockSpec((tm, tk), lambda i, j: (i, j))],
               out_shape=[SDS((s, width), BF16), SDS((s, k), F32)], sem=("parallel", "arbitrary"))(*pieces, wg)


def mm_row_t(name, dy, wg, out_dtype=F32):
    s, n = dy.shape
    _, r, _ = wg.shape
    tm = _tile(s, 512)

    def body(dy_ref, w_ref, o_ref):
        dyv = dy_ref[...].astype(BF16)
        for j in range(N_CHIPS):
            o_ref[:, j * r:(j + 1) * r] = lax.dot_general(dyv, w_ref[j], _NT, preferred_element_type=F32).astype(out_dtype)

    return _pc(body, name=name, grid=(s // tm,), in_specs=[_rows(tm, n), _fixed(wg.shape)], out_specs=_rows(tm, N_CHIPS * r),
               out_shape=SDS((s, N_CHIPS * r), out_dtype), sem=("parallel",))(dy, wg)


def norm_bwd_pair(name, x_a, gain_a, dy, dres, x_b, gain_b):
    s, d = x_a.shape
    t = _tile(s, ROW_TILE)

    def one(xv, g, dyv):
        r = _rstd(xv)
        xh = xv * r
        dyg = dyv * g
        return r * (dyg - xh * jnp.mean(dyg * xh, axis=-1, keepdims=True)), jnp.sum(dyv * xh, axis=0, keepdims=True)

    def body(xa_ref, ga_ref, dy_ref, r_ref, xb_ref, gb_ref, dx_ref, db_ref, dga_ref, dgb_ref):
        dx, pa = one(xa_ref[...], ga_ref[...], dy_ref[...])
        dx = dx + r_ref[...]
        dx_ref[...] = dx
        db, pb = one(xb_ref[...], gb_ref[...], dx)
        db_ref[...] = db.astype(BF16)

        @pl.when(pl.program_id(0) == 0)
        def _():
            dga_ref[...] = pa
            dgb_ref[...] = pb

        @pl.when(pl.program_id(0) > 0)
        def _():
            dga_ref[...] += pa
            dgb_ref[...] += pb

    row, gain = _rows(t, d), _fixed((1, d))
    return _pc(body, name=name, grid=(s // t,), in_specs=[row, gain, row, row, row, gain], out_specs=[row, row, gain, gain],
               out_shape=[SDS((s, d), F32), SDS((s, d), BF16), SDS((1, d), F32), SDS((1, d), F32)],
               sem=("arbitrary",))(x_a, gain_a, dy, dres, x_b, gain_b)


def ffn_gate_up(name, h, wg_gate, wg_up):
    s, k = h.shape
    _, _, n = wg_gate.shape
    tm = _tile(s, 512)

    def body(h_ref, wg_ref, wu_ref, g_ref, u_ref, a_ref):
        hv = h_ref[...]
        g = jnp.dot(hv, wg_ref[...], preferred_element_type=F32)
        u = jnp.dot(hv, wu_ref[...], preferred_element_type=F32)
        g_ref[...] = g
        u_ref[...] = u
        a_ref[...] = ((g * _sigmoid(g)) * u).astype(BF16)

    wspec = pl.BlockSpec((None, k, n), lambda j, i: (j, 0, 0))
    ospec = pl.BlockSpec((tm, n), lambda j, i: (i, j))
    return _pc(body, name=name, grid=(N_CHIPS, s // tm), in_specs=[pl.BlockSpec((tm, k), lambda j, i: (i, 0)), wspec, wspec],
               out_specs=[ospec] * 3, out_shape=[SDS((s, N_CHIPS * n), F32)] * 2 + [SDS((s, N_CHIPS * n), BF16)],
               sem=("parallel", "parallel"))(h, wg_gate, wg_up)


def ffn_down_bwd(name, df, wg_down, g, u):
    s, n = df.shape
    _, r, _ = wg_down.shape
    tm = _tile(s, 512)

    def body(df_ref, w_ref, g_ref, u_ref, dg_ref, du_ref):
        da = lax.dot_general(df_ref[...], w_ref[...], _NT, preferred_element_type=F32)
        gv = g_ref[...]
        sg = _sigmoid(gv)
        du_ref[...] = (da * (gv * sg)).astype(BF16)
        dg_ref[...] = ((da * u_ref[...]) * (sg * (1.0 + gv * (1.0 - sg)))).astype(BF16)

    cspec = pl.BlockSpec((tm, r), lambda j, i: (i, j))
    return _pc(body, name=name, grid=(N_CHIPS, s // tm),
               in_specs=[pl.BlockSpec((tm, n), lambda j, i: (i, 0)), pl.BlockSpec((None, r, n), lambda j, i: (j, 0, 0)), cspec, cspec],
               out_specs=[cspec] * 2, out_shape=[SDS((s, N_CHIPS * r), BF16)] * 2,
               sem=("parallel", "parallel"))(df, wg_down, g, u)


def mm_wg_col(name, a, dy):
    s, k = a.shape
    n = dy.shape[1] // N_CHIPS
    tm = _tile(k // 2, 1024)
    hb = (k // 2) // tm
    return _mm(name, a, dy, dims=_TN, grid=(N_CHIPS, k // tm),
               a_spec=pl.BlockSpec((s, tm), lambda j, i: (0, i)),
               b_spec=pl.BlockSpec((s, n), lambda j, i: (0, j)),
               o_spec=pl.BlockSpec((None, None, tm, n), lambda j, i: (i // hb, j, i % hb, 0)),
               out_shape=SDS((2, N_CHIPS, k // 2, n), BF16))


def mm_wg_row(name, a, dy):
    s, n = dy.shape
    r = a.shape[1] // N_CHIPS
    tn = _tile(n // 2, 512)
    nb = (n // 2) // tn
    return _mm(name, a, dy, dims=_TN, grid=(N_CHIPS, n // tn),
               a_spec=pl.BlockSpec((s, r), lambda j, i: (0, j)),
               b_spec=pl.BlockSpec((s, tn), lambda j, i: (0, i)),
               o_spec=pl.BlockSpec((None, None, r, tn), lambda j, i: (i // nb, j, 0, i % nb)),
               out_shape=SDS((2, N_CHIPS, r, n // 2), BF16))


def pre_norm(name, x, gain):
    s, d = x.shape
    t = _tile(s, ROW_TILE_WIDE)

    def body(x_ref, g_ref, o_ref):
        xv = x_ref[...]
        o_ref[...] = ((xv * _rstd(xv)) * g_ref[...]).astype(BF16)

    return _pc(body, name=name, grid=(s // t,), in_specs=[_rows(t, d), _fixed((1, d))], out_specs=_rows(t, d),
               out_shape=SDS((s, d), BF16), sem=("parallel",))(x, gain)


def post_pre_norm(name, m, g_post, res, g_pre):
    s, d = m.shape
    t = _tile(s, ROW_TILE_WIDE)

    def body(m_ref, gp_ref, r_ref, gn_ref, x_ref, h_ref):
        mv = m_ref[...]
        xn = r_ref[...] + (mv * _rstd(mv)) * gp_ref[...]
        x_ref[...] = xn
        h_ref[...] = ((xn * _rstd(xn)) * gn_ref[...]).astype(BF16)

    return _pc(body, name=name, grid=(s // t,),
               in_specs=[_rows(t, d), _fixed((1, d)), _rows(t, d), _fixed((1, d))],
               out_specs=[_rows(t, d), _rows(t, d)], out_shape=[SDS((s, d), F32), SDS((s, d), BF16)],
               sem=("parallel",))(m, g_post, res, g_pre)


def _row_dot(a_ref, w_ref):
    r = w_ref.shape[1]
    out = None
    for j in range(N_CHIPS):
        part = jnp.dot(a_ref[:, j * r:(j + 1) * r], w_ref[j], preferred_element_type=F32)
        out = part if out is None else out + part
    return out


def _row_dot_specs(t, a, wg):
    return [_rows(t, a.shape[1]), _fixed(wg.shape)]


def out_proj_post_mix(name, a, wg, g_post, res, g_pre):
    s, d = res.shape
    t = _tile(s, ROW_TILE)

    def body(a_ref, w_ref, gp_ref, r_ref, gn_ref, m_ref, x_ref, h_ref):
        mv = _row_dot(a_ref, w_ref)
        m_ref[...] = mv
        xn = r_ref[...] + (mv * _rstd(mv)) * gp_ref[...]
        x_ref[...] = xn
        h_ref[...] = ((xn * _rstd(xn)) * gn_ref[...]).astype(BF16)

    return _pc(body, name=name, grid=(s // t,),
               in_specs=_row_dot_specs(t, a, wg) + [_fixed((1, d)), _rows(t, d), _fixed((1, d))],
               out_specs=[_rows(t, d)] * 3, out_shape=[SDS((s, d), F32), SDS((s, d), F32), SDS((s, d), BF16)],
               sem=("parallel",))(a, wg, g_post, res, g_pre)


def _col_dot(p_ref, w_ref):
    pv = p_ref[...].astype(BF16)
    return jnp.concatenate([jnp.dot(pv, w_ref[j], preferred_element_type=F32) for j in range(N_CHIPS)], axis=1)


def ple_gate_fwd_mid(name, a, wg, p, wg_proj, x2, g_next):
    s, d = x2.shape
    t = _tile(s, ROW_TILE)

    def body(a_ref, w_ref, p_ref, wp_ref, x_ref, g_ref, z_ref, pp_ref, xo_ref, h_ref):
        z = _row_dot(a_ref, w_ref)
        z_ref[...] = z
        pv = _col_dot(p_ref, wp_ref)
        pp_ref[...] = pv
        xn = x_ref[...] + pv * _sigmoid(z)
        xo_ref[...] = xn
        h_ref[...] = ((xn * _rstd(xn)) * g_ref[...]).astype(BF16)

    return _pc(body, name=name, grid=(s // t,),
               in_specs=_row_dot_specs(t, a, wg) + [_rows(t, p.shape[1]), _fixed(wg_proj.shape), _rows(t, d), _fixed((1, d))],
               out_specs=[_rows(t, d)] * 4, out_shape=[SDS((s, d), F32)] * 3 + [SDS((s, d), BF16)],
               sem=("parallel",))(a, wg, p, wg_proj, x2, g_next)


def ple_gate_fwd_loss(name, a, wg, p, wg_proj, x2, target):
    s, d = x2.shape
    t = _tile(s, ROW_TILE)

    def body(a_ref, w_ref, p_ref, wp_ref, x_ref, t_ref, dy_ref, dpp_ref, dz_ref, l_ref):
        gate = _sigmoid(_row_dot(a_ref, w_ref))
        pv = _col_dot(p_ref, wp_ref)
        err = (x_ref[...] + pv * gate) - t_ref[...]
        dy = err * (1.0 / d)
        dy_ref[...] = dy
        dpp_ref[...] = (dy * gate).astype(BF16)
        dz_ref[...] = ((dy * pv) * (gate * (1.0 - gate))).astype(BF16)
        part = jnp.sum(jnp.sum(err * err, axis=-1, keepdims=True), axis=0, keepdims=True) * (0.5 / d)

        @pl.when(pl.program_id(0) == 0)
        def _():
            l_ref[...] = part

        @pl.when(pl.program_id(0) > 0)
        def _():
            l_ref[...] += part

    return _pc(body, name=name, grid=(s // t,),
               in_specs=_row_dot_specs(t, a, wg) + [_rows(t, p.shape[1]), _fixed(wg_proj.shape), _rows(t, d), _rows(t, d)],
               out_specs=[_rows(t, d), _rows(t, d), _rows(t, d), _fixed((1, 1))],
               out_shape=[SDS((s, d), F32), SDS((s, d), BF16), SDS((s, d), BF16), SDS((1, 1), F32)],
               sem=("arbitrary",))(a, wg, p, wg_proj, x2, target)


ANY_SPEC = pl.BlockSpec(memory_space=pl.ANY)


def ple_bwd(name, dx3, z, pp, after):
    s, d = z.shape
    t = _tile(s, ROW_TILE_WIDE)

    def body(d_ref, z_ref, p_ref, after_ref, dpp_ref, dz_ref):
        gate = _sigmoid(z_ref[...])
        dv = d_ref[...]
        dpp_ref[...] = (dv * gate).astype(BF16)
        dz_ref[...] = ((dv * p_ref[...]) * (gate * (1.0 - gate))).astype(BF16)

    return _pc(body, name=name, grid=(s // t,), in_specs=[_rows(t, d)] * 3 + [ANY_SPEC], out_specs=[_rows(t, d)] * 2,
               out_shape=[SDS((s, d), BF16)] * 2, sem=("parallel",))(dx3, z, pp, after)


def norm_bwd(name, xin, gain, dy, dres):
    s, d = xin.shape
    t = _tile(s, ROW_TILE_WIDE)

    def body(x_ref, g_ref, dy_ref, r_ref, dx_ref, dg_ref):
        xv = x_ref[...]
        r = _rstd(xv)
        xh = xv * r
        dyv = dy_ref[...]
        dyg = dyv * g_ref[...]
        c = jnp.mean(dyg * xh, axis=-1, keepdims=True)
        dx_ref[...] = r * (dyg - xh * c) + r_ref[...]
        part = jnp.sum(dyv * xh, axis=0, keepdims=True)

        @pl.when(pl.program_id(0) == 0)
        def _():
            dg_ref[...] = part

        @pl.when(pl.program_id(0) > 0)
        def _():
            dg_ref[...] += part

    return _pc(body, name=name, grid=(s // t,), in_specs=[_rows(t, d), _fixed((1, d)), _rows(t, d), _rows(t, d)],
               out_specs=[_rows(t, d), _fixed((1, d))], out_shape=[SDS((s, d), F32), SDS((1, d), F32)],
               sem=("arbitrary",))(xin, gain, dy, dres)


def _lane(shape):
    return lax.broadcasted_iota(jnp.int32, shape, 1)


def _swap_halves(x):
    lo = (_lane(x.shape) % 64) < 32
    return jnp.where(lo, pltpu.roll(x, 96, 1), pltpu.roll(x, 32, 1))


def rope_tables(name, pos_col, inv_freq):
    s = pos_col.shape[0]
    t = _tile(s, ROW_TILE_WIDE)

    def body(p_ref, f_ref, c_ref, s_ref):
        ang = p_ref[...].astype(F32) * f_ref[...]
        lo = (_lane(ang.shape) % 64) < 32
        c_ref[...] = jnp.cos(ang)
        sn = jnp.sin(ang)
        s_ref[...] = jnp.where(lo, -sn, sn)

    return _pc(body, name=name, grid=(s // t,), in_specs=[_rows(t, 1), _fixed((1, LANES))],
               out_specs=[_rows(t, LANES)] * 2, out_shape=[SDS((s, LANES), F32)] * 2, sem=("parallel",))(pos_col, inv_freq)


def _pad_heads(chunk, lo_mask):
    zero = jnp.zeros_like(chunk)
    return jnp.where(lo_mask, chunk, zero), jnp.where(lo_mask, pltpu.roll(chunk, 64, 1), zero)


def rope_qkv(name, proj, cos, sin):
    s = proj.shape[0]
    t = _tile(s, ROW_TILE_WIDE)

    def body(q_ref, kv_ref, c_ref, s_ref, qp_ref, kp_ref, vp_ref):
        cs, sn = c_ref[...], s_ref[...]
        lo_mask = _lane(cs.shape) < 64

        def rot(x):
            return x * cs + _swap_halves(x) * sn

        for j in range(ATTN_WIDTH // LANES):
            a, b = _pad_heads(rot(q_ref[:, j * LANES:(j + 1) * LANES]), lo_mask)
            qp_ref[:, (2 * j) * LANES:(2 * j + 1) * LANES] = a.astype(BF16)
            qp_ref[:, (2 * j + 1) * LANES:(2 * j + 2) * LANES] = b.astype(BF16)
        for j in range(KV_WIDTH // LANES):
            a, b = _pad_heads(rot(kv_ref[:, j * LANES:(j + 1) * LANES]), lo_mask)
            kp_ref[:, (2 * j) * LANES:(2 * j + 1) * LANES] = a.astype(BF16)
            kp_ref[:, (2 * j + 1) * LANES:(2 * j + 2) * LANES] = b.astype(BF16)
            a, b = _pad_heads(kv_ref[:, KV_WIDTH + j * LANES:KV_WIDTH + (j + 1) * LANES], lo_mask)
            vp_ref[:, (2 * j) * LANES:(2 * j + 1) * LANES] = a.astype(BF16)
            vp_ref[:, (2 * j + 1) * LANES:(2 * j + 2) * LANES] = b.astype(BF16)

    return _pc(body, name=name, grid=(s // t,),
               in_specs=[_rows(t, ATTN_WIDTH, 0), _rows(t, 2 * KV_WIDTH, 2), _rows(t, LANES), _rows(t, LANES)],
               out_specs=[_rows(t, N_Q_HEADS * LANES), _rows(t, N_KV_HEADS * LANES), _rows(t, N_KV_HEADS * LANES)],
               out_shape=[SDS((s, N_Q_HEADS * LANES), BF16), SDS((s, N_KV_HEADS * LANES), BF16),
                          SDS((s, N_KV_HEADS * LANES), BF16)],
               sem=("parallel",))(proj, proj, cos, sin)


def _attn_mask(n):
    L = WINDOW
    qi = lax.broadcasted_iota(jnp.int32, (L, 2 * L), 0) + L
    ki = lax.broadcasted_iota(jnp.int32, (L, 2 * L), 1)
    rel = qi - ki
    return (rel >= 0) & (rel < WINDOW) & ((n > 0) | (ki >= L))


def _attn_scores(qh, kk, valid):
    sc = lax.dot_general(qh, kk, _NT, preferred_element_type=F32) * 0.125
    return jnp.where(valid, sc, MASK_VALUE)


def _attn_softmax(sc, sink):
    m = jnp.maximum(jnp.max(sc, axis=-1, keepdims=True), sink)
    e = jnp.exp(sc - m)
    es = jnp.exp(sink - m)
    den = jnp.sum(e, axis=-1, keepdims=True) + es
    return e / den, es / den


def _attn_specs(s):
    L = WINDOW
    cur = lambda n: (n, 0)
    prev = lambda n: (jnp.maximum(n - 1, 0), 0)
    kvw = N_KV_HEADS * LANES
    return [pl.BlockSpec((L, N_Q_HEADS * LANES), cur), pl.BlockSpec((L, kvw), cur), pl.BlockSpec((L, kvw), prev),
            pl.BlockSpec((L, kvw), cur), pl.BlockSpec((L, kvw), prev), pl.BlockSpec(memory_space=pltpu.SMEM)]


def attn_fwd(name, qp, kp, vp, sinks):
    s = qp.shape[0]
    L = WINDOW

    def body(q_ref, kc_ref, kp_ref, vc_ref, vp_ref, sk_ref, o_ref):
        valid = _attn_mask(pl.program_id(0))
        kks, vvs = [], []
        for kvh in range(N_KV_HEADS):
            cols = slice(kvh * LANES, (kvh + 1) * LANES)
            kks.append(jnp.concatenate([kp_ref[:, cols], kc_ref[:, cols]], axis=0))
            vvs.append(jnp.concatenate([vp_ref[:, cols], vc_ref[:, cols]], axis=0))
        scs = [_attn_scores(q_ref[:, h * LANES:(h + 1) * LANES], kks[h // Q_PER_KV], valid) for h in range(N_Q_HEADS)]
        pbs = [_attn_softmax(scs[h], sk_ref[h])[0].astype(BF16) for h in range(N_Q_HEADS)]
        outs = [jnp.dot(pbs[h], vvs[h // Q_PER_KV], preferred_element_type=F32) for h in range(N_Q_HEADS)]
        for j in range(ATTN_WIDTH // LANES):
            o_ref[:, j * LANES:(j + 1) * LANES] = outs[2 * j] + pltpu.roll(outs[2 * j + 1], 64, 1)

    return _pc(body, name=name, grid=(s // L,), in_specs=_attn_specs(s),
               out_specs=pl.BlockSpec((L, ATTN_WIDTH), lambda n: (n, 0)),
               out_shape=SDS((s, ATTN_WIDTH), F32), sem=("parallel",))(qp, kp, kp, vp, vp, sinks)


def attn_bwd(name, qp, kp, vp, sinks, dattn):
    s = qp.shape[0]
    L = WINDOW
    kvw = N_KV_HEADS * LANES

    def body(q_ref, kc_ref, kp_ref, vc_ref, vp_ref, sk_ref, do_ref, dq_ref, dkc_ref, dkp_ref, dvc_ref, dvp_ref, ds_ref):
        n = pl.program_id(0)
        valid = _attn_mask(n)
        lo_mask = _lane((L, LANES)) < 64
        lane1 = _lane((1, LANES))
        dsink = jnp.zeros((1, LANES), F32)
        heads = range(N_Q_HEADS)
        kks, vvs = [], []
        for kvh in range(N_KV_HEADS):
            cols = slice(kvh * LANES, (kvh + 1) * LANES)
            kks.append(jnp.concatenate([kp_ref[:, cols], kc_ref[:, cols]], axis=0))
            vvs.append(jnp.concatenate([vp_ref[:, cols], vc_ref[:, cols]], axis=0))
        qs, dos, scs, dps = [], [], [], []
        for h in heads:
            qs.append(q_ref[:, h * LANES:(h + 1) * LANES])
            chunk = do_ref[:, (h // 2) * LANES:(h // 2 + 1) * LANES]
            if h % 2:
                chunk = pltpu.roll(chunk, 64, 1)
            dos.append(jnp.where(lo_mask, chunk, 0.0).astype(BF16))
            scs.append(_attn_scores(qs[h], kks[h // Q_PER_KV], valid))
            dps.append(lax.dot_general(dos[h], vvs[h // Q_PER_KV], _NT, preferred_element_type=F32))
        pbs, dsbs = [], []
        for h in heads:
            p, ps = _attn_softmax(scs[h], sk_ref[h])
            delta = jnp.sum(p * dps[h], axis=-1, keepdims=True)
            dsbs.append(((p * (dps[h] - delta)) * 0.125).astype(BF16))
            pbs.append(p.astype(BF16))
            dsink = dsink + jnp.where(lane1 == h, -jnp.sum(ps * delta, axis=0, keepdims=True), 0.0)
        for kvh in range(N_KV_HEADS):
            cols = slice(kvh * LANES, (kvh + 1) * LANES)
            dkk = jnp.zeros((2 * L, LANES), F32)
            dvv = jnp.zeros((2 * L, LANES), F32)
            for h in range(kvh * Q_PER_KV, (kvh + 1) * Q_PER_KV):
                dq_ref[:, h * LANES:(h + 1) * LANES] = jnp.dot(dsbs[h], kks[kvh], preferred_element_type=F32)
                dkk = dkk + lax.dot_general(dsbs[h], qs[h], _TN, preferred_element_type=F32)
                dvv = dvv + lax.dot_general(pbs[h], dos[h], _TN, preferred_element_type=F32)
            dkp_ref[:, cols] = dkk[:L]
            dkc_ref[:, cols] = dkk[L:]
            dvp_ref[:, cols] = dvv[:L]
            dvc_ref[:, cols] = dvv[L:]

        @pl.when(n == 0)
        def _():
            ds_ref[...] = dsink

        @pl.when(n > 0)
        def _():
            ds_ref[...] += dsink

    blk = lambda w: pl.BlockSpec((L, w), lambda n: (n, 0))
    return _pc(body, name=name, grid=(s // L,), in_specs=_attn_specs(s) + [blk(ATTN_WIDTH)],
               out_specs=[blk(N_Q_HEADS * LANES), blk(kvw), blk(kvw), blk(kvw), blk(kvw), _fixed((1, LANES))],
               out_shape=[SDS((s, N_Q_HEADS * LANES), F32)] + [SDS((s, kvw), F32)] * 4 + [SDS((1, LANES), F32)],
               sem=("arbitrary",))(qp, kp, kp, vp, vp, sinks, dattn)


def rope_bwd(name, dqp, dkc, dkp, dvc, dvp, cos, sin):
    s = dqp.shape[0]
    L = WINDOW
    nb = s // L
    kvw = N_KV_HEADS * LANES

    def body(dq_ref, dkc_ref, dkp_ref, dvc_ref, dvp_ref, c_ref, s_ref, o_ref):
        cs, sn = c_ref[...], s_ref[...]
        more = (pl.program_id(0) < nb - 1).astype(F32)

        def unrot(x):
            return x * cs - _swap_halves(x) * sn

        def compact(ref, j, nxt=None):
            a = ref[:, (2 * j) * LANES:(2 * j + 1) * LANES]
            b = ref[:, (2 * j + 1) * LANES:(2 * j + 2) * LANES]
            if nxt is not None:
                a = a + more * nxt[:, (2 * j) * LANES:(2 * j + 1) * LANES]
                b = b + more * nxt[:, (2 * j + 1) * LANES:(2 * j + 2) * LANES]
            return a + pltpu.roll(b, 64, 1)

        for j in range(ATTN_WIDTH // LANES):
            o_ref[:, j * LANES:(j + 1) * LANES] = unrot(compact(dq_ref, j)).astype(BF16)
        for j in range(KV_WIDTH // LANES):
            o_ref[:, (COL_K + j) * LANES:(COL_K + j + 1) * LANES] = unrot(compact(dkc_ref, j, dkp_ref)).astype(BF16)
            o_ref[:, (COL_V + j) * LANES:(COL_V + j + 1) * LANES] = compact(dvc_ref, j, dvp_ref).astype(BF16)

    cur = lambda n: (n, 0)
    nxt = lambda n: (jnp.minimum(n + 1, nb - 1), 0)
    return _pc(body, name=name, grid=(nb,),
               in_specs=[pl.BlockSpec((L, N_Q_HEADS * LANES), cur), pl.BlockSpec((L, kvw), cur), pl.BlockSpec((L, kvw), nxt),
                         pl.BlockSpec((L, kvw), cur), pl.BlockSpec((L, kvw), nxt), pl.BlockSpec((L, LANES), cur),
                         pl.BlockSpec((L, LANES), cur)],
               out_specs=pl.BlockSpec((L, COL_HQ * LANES), cur), out_shape=SDS((s, COL_HQ * LANES), BF16),
               sem=("parallel",))(dqp, dkc, dkp, dvc, dvp, cos, sin)


def _split3(x):
    a = x.astype(BF16)
    r = x - a.astype(F32)
    b = r.astype(BF16)
    c = (r - b.astype(F32)).astype(BF16)
    return a, b, c


def _chunk_sum(x, upper):
    t = x.shape[0]
    ri = lax.broadcasted_iota(jnp.int32, (t, t), 0)
    ci = lax.broadcasted_iota(jnp.int32, (t, t), 1)
    same = (ri // HGRN_CHUNK) == (ci // HGRN_CHUNK)
    tri = (ci >= ri) if upper else (ci <= ri)
    m = jnp.where(same & tri, 1.0, 0.0).astype(BF16)
    out = None
    for part in _split3(x):
        y = jnp.dot(m, part, preferred_element_type=F32)
        out = y if out is None else out + y
    return out


def _lower_bound(l_ref, layer):
    lv = l_ref[...]
    e = jnp.exp(lv - jnp.max(lv, axis=0, keepdims=True))
    sm = e / jnp.sum(e, axis=0, keepdims=True)
    s0 = sm[0:1]
    return (s0 - s0) if layer == 0 else ((s0 + sm[1:2]) - s0)


def _hgrn_gates(hq_ref, hf_ref, lb):
    z = hf_ref[...]
    sg = _sigmoid(z)
    f = lb + (1.0 - lb) * sg
    kin = (1.0 - lb) * _sigmoid(-z)
    hq = hq_ref[...]
    sq = _sigmoid(hq)
    return sg, f, kin, hq, sq


def _shift_down(x, d):
    return x if d == 0 else pltpu.roll(x, d, 0)


def _shift_up(x, d):
    return x if d == 0 else pltpu.roll(x, x.shape[0] - d, 0)


CHUNKS_PER_BLOCK = LANES // HGRN_CHUNK


def _chunk_iotas():
    shape = (HGRN_CHUNK, LANES)
    return lax.broadcasted_iota(jnp.int32, shape, 0), lax.broadcasted_iota(jnp.int32, shape, 1)


def _chunk_rows(block, chunk):
    start = block * LANES + chunk * HGRN_CHUNK
    return slice(start, start + HGRN_CHUNK)


HGRN_HEADS_PER_STEP = 4
HGRN_STEP_WIDTH = HGRN_HEADS_PER_STEP * LANES


def _hgrn_specs(t, rev, nt):
    row = (lambda h, i: nt - 1 - i) if rev else (lambda h, i: i)
    col = lambda base: pl.BlockSpec((t, HGRN_STEP_WIDTH),
                                    lambda h, i, base=base: (row(h, i), base // HGRN_HEADS_PER_STEP + h))
    return col, row


def _head_views(refs, hh):
    return [r.at[:, pl.ds(hh * LANES, LANES)] for r in refs]


def hgrn_fwd(name, proj, lb_logits, layer):
    s = proj.shape[0]
    t = _tile(s, ROW_TILE)
    nt = s // t
    nc = t // HGRN_CHUNK
    col, row = _hgrn_specs(t, False, nt)

    def body(hq_ref, hf_ref, hi_ref, l_ref, o_ref, st_ref, state):
        @pl.when(pl.program_id(1) == 0)
        def _():
            state[...] = jnp.zeros_like(state)

        for hh in range(HGRN_HEADS_PER_STEP):
            head(*_head_views((hq_ref, hf_ref, hi_ref, l_ref, o_ref), hh), st_ref.at[:, hh], state.at[hh])

    def head(hq_ref, hf_ref, hi_ref, l_ref, o_ref, st_ref, state):
        lb = _lower_bound(l_ref, layer)
        sg, f, kin, hq, sq = _hgrn_gates(hq_ref, hf_ref, lb)
        q = hq * sq
        vb = hi_ref[...].astype(BF16)
        b = _chunk_sum(jnp.log(f), False)
        qe = (q * jnp.exp(b)).astype(BF16)
        trow, lane = _chunk_iotas()
        chunks = [(j, cc) for j in range(t // LANES) for cc in range(CHUNKS_PER_BLOCK)]
        decay, update = [], []
        for j, cc in chunks:
            rs = _chunk_rows(j, cc)
            bc = b[rs]
            bl = bc[HGRN_CHUNK - 1:HGRN_CHUNK, :]
            ke = (kin[rs] * jnp.exp(bl - bc)).astype(BF16)
            decay.append(jnp.exp(bl))
            update.append(lax.dot_general(vb[rs], ke, _TN, preferred_element_type=F32))
        st = state[...]
        for c in range(nc):
            st_ref[c] = st
            st = st * decay[c] + update[c]
        state[...] = st
        o_inter = [lax.dot_general(qe[c * HGRN_CHUNK:(c + 1) * HGRN_CHUNK], st_ref[c].astype(BF16), _NT,
                                   preferred_element_type=F32) for c in range(nc)]
        for j in range(t // LANES):
            blk = slice(j * LANES, (j + 1) * LANES)
            rows = []
            for cc in range(CHUNKS_PER_BLOCK):
                rs = _chunk_rows(j, cc)
                bc, qc, kc = b[rs], q[rs], kin[rs]
                here = trow + cc * HGRN_CHUNK
                am = jnp.where(lane == here, jnp.sum(qc * kc, axis=-1, keepdims=True), 0.0)
                for d in range(1, HGRN_CHUNK):
                    e = jnp.exp(jnp.where(trow >= d, bc - _shift_down(bc, d), MASK_VALUE))
                    a = jnp.sum((qc * _shift_down(kc, d)) * e, axis=-1, keepdims=True)
                    am = jnp.where(lane == here - d, a, am)
                rows.append(am)
            o_intra = jnp.dot(jnp.concatenate(rows, axis=0).astype(BF16), vb[blk], preferred_element_type=F32)
            for cc in range(CHUNKS_PER_BLOCK):
                rs = _chunk_rows(j, cc)
                o_ref[rs, :] = o_intra[cc * HGRN_CHUNK:(cc + 1) * HGRN_CHUNK] + o_inter[j * CHUNKS_PER_BLOCK + cc]

    hp = HGRN_HEADS_PER_STEP
    return _pc(body, name=name, grid=(HGRN_HEADS // hp, nt),
               in_specs=[col(COL_HQ), col(COL_HF), col(COL_HI), pl.BlockSpec((2, HGRN_STEP_WIDTH), lambda h, i: (0, h))],
               out_specs=[pl.BlockSpec((t, HGRN_STEP_WIDTH), lambda h, i: (i, h)),
                          pl.BlockSpec((nc, hp, LANES, LANES), lambda h, i: (i, h, 0, 0))],
               out_shape=[SDS((s, HGRN_WIDTH), F32), SDS((s // HGRN_CHUNK, HGRN_HEADS, LANES, LANES), F32)],
               scratch=[pltpu.VMEM((hp, LANES, LANES), F32)],
               sem=("parallel", "arbitrary"))(proj, proj, proj, lb_logits)


def hgrn_bwd(name, proj, lb_logits, layer, states, do):
    s = proj.shape[0]
    t = _tile(s, ROW_TILE)
    nt = s // t
    nc = t // HGRN_CHUNK
    col, row = _hgrn_specs(t, True, nt)

    def body(hq_ref, hf_ref, hi_ref, l_ref, st_ref, do_ref, dhq_ref, dhf_ref, dhi_ref, dlb_ref, dstate):
        @pl.when(pl.program_id(1) == 0)
        def _():
            dstate[...] = jnp.zeros_like(dstate)

        for hh in range(HGRN_HEADS_PER_STEP):
            hq_v, hf_v, hi_v, l_v, do_v, dhq_v, dhf_v, dhi_v, dlb_v = _head_views(
                (hq_ref, hf_ref, hi_ref, l_ref, do_ref, dhq_ref, dhf_ref, dhi_ref, dlb_ref), hh)
            head(hq_v, hf_v, hi_v, l_v, st_ref.at[:, hh], do_v, dhq_v, dhf_v, dhi_v, dlb_v, dstate.at[hh])

    def head(hq_ref, hf_ref, hi_ref, l_ref, st_ref, do_ref, dhq_ref, dhf_ref, dhi_ref, dlb_ref, dstate):
        first = pl.program_id(1) == 0
        lb = _lower_bound(l_ref, layer)
        sg, f, kin, hq, sq = _hgrn_gates(hq_ref, hf_ref, lb)
        q = hq * sq
        vb = hi_ref[...].astype(BF16)
        b = _chunk_sum(jnp.log(f), False)
        dob = do_ref[...].astype(BF16)
        eb = jnp.exp(b)
        qe = q * eb
        qeb = qe.astype(BF16)
        trow, lane = _chunk_iotas()
        last_row = trow == HGRN_CHUNK - 1

        decay, update = [None] * nc, [None] * nc
        for c in range(nc):
            rs = slice(c * HGRN_CHUNK, (c + 1) * HGRN_CHUNK)
            decay[c] = jnp.exp(b[(c + 1) * HGRN_CHUNK - 1:(c + 1) * HGRN_CHUNK, :])
            update[c] = lax.dot_general(dob[rs], qeb[rs], _TN, preferred_element_type=F32)
        dn_in = [None] * nc
        dn = dstate[...]
        for c in reversed(range(nc)):
            dn_in[c] = dn
            dn = dn * decay[c] + update[c]
        dstate[...] = dn

        dq_c, dk_c, dv_c, dbl_c = [None] * nc, [None] * nc, [None] * nc, [None] * nc
        for c in range(nc):
            rs = slice(c * HGRN_CHUNK, (c + 1) * HGRN_CHUNK)
            bc = b[rs]
            ekb = jnp.exp(bc[HGRN_CHUNK - 1:HGRN_CHUNK, :] - bc)
            ke = kin[rs] * ekb
            st = st_ref[c]
            dnb = dn_in[c].astype(BF16)
            dke = jnp.dot(vb[rs], dnb, preferred_element_type=F32)
            dq_c[c] = jnp.dot(dob[rs], st.astype(BF16), preferred_element_type=F32) * eb[rs]
            dk_c[c] = dke * ekb
            dv_c[c] = lax.dot_general(ke.astype(BF16), dnb, _NT, preferred_element_type=F32)
            dbl_c[c] = jnp.sum(dn_in[c] * st, axis=0, keepdims=True) * decay[c] + jnp.sum(dke * ke, axis=0, keepdims=True)

        db_c = [None] * nc
        for j in range(t // LANES):
            blk = slice(j * LANES, (j + 1) * LANES)
            damat = lax.dot_general(dob[blk], vb[blk], _NT, preferred_element_type=F32)
            rows = [None] * CHUNKS_PER_BLOCK
            for cc in range(CHUNKS_PER_BLOCK):
                c = j * CHUNKS_PER_BLOCK + cc
                rs = _chunk_rows(j, cc)
                bc, qc, kc = b[rs], q[rs], kin[rs]
                dam = damat[cc * HGRN_CHUNK:(cc + 1) * HGRN_CHUNK]
                here = trow + cc * HGRN_CHUNK
                on = lane == here
                da = jnp.sum(jnp.where(on, dam, 0.0), axis=-1, keepdims=True)
                am = jnp.where(on, jnp.sum(qc * kc, axis=-1, keepdims=True), 0.0)
                dq = dq_c[c] + da * kc
                dk = dk_c[c] + da * qc
                for d in range(1, HGRN_CHUNK):
                    on = lane == here - d
                    e = jnp.exp(jnp.where(trow >= d, bc - _shift_down(bc, d), MASK_VALUE))
                    kse = _shift_down(kc, d) * e
                    am = jnp.where(on, jnp.sum(qc * kse, axis=-1, keepdims=True), am)
                    da = jnp.sum(jnp.where(on, dam, 0.0), axis=-1, keepdims=True)
                    dq = dq + da * kse
                    dk = dk + _shift_up(da * (qc * e), d)
                rows[cc] = am
                dq_c[c], dk_c[c] = dq, dk
                db_c[c] = (qc * dq - kc * dk) + jnp.where(last_row, dbl_c[c], 0.0)
            dv_blk = lax.dot_general(jnp.concatenate(rows, axis=0).astype(BF16), dob[blk], _TN, preferred_element_type=F32)
            for cc in range(CHUNKS_PER_BLOCK):
                c = j * CHUNKS_PER_BLOCK + cc
                dv_c[c] = dv_c[c] + dv_blk[cc * HGRN_CHUNK:(cc + 1) * HGRN_CHUNK]
        dq = jnp.concatenate(dq_c, axis=0)
        dk = jnp.concatenate(dk_c, axis=0)
        dv = jnp.concatenate(dv_c, axis=0)
        db = jnp.concatenate(db_c, axis=0)
        dg = _chunk_sum(db, True)
        dhq_ref[...] = (dq * (sq * (1.0 + hq * (1.0 - sq)))).astype(BF16)
        dhi_ref[...] = dv.astype(BF16)
        dfk = dg / f - dk
        dhf_ref[...] = ((dfk * (1.0 - lb)) * (sg * (1.0 - sg))).astype(BF16)
        part = jnp.sum(dfk * (1.0 - sg), axis=0, keepdims=True)

        @pl.when(first)
        def _():
            dlb_ref[...] = part

        @pl.when(jnp.logical_not(first))
        def _():
            dlb_ref[...] += part

    hp = HGRN_HEADS_PER_STEP
    out_col = pl.BlockSpec((t, HGRN_STEP_WIDTH), lambda h, i: (nt - 1 - i, h))
    return _pc(body, name=name, grid=(HGRN_HEADS // hp, nt),
               in_specs=[col(COL_HQ), col(COL_HF), col(COL_HI), pl.BlockSpec((2, HGRN_STEP_WIDTH), lambda h, i: (0, h)),
                         pl.BlockSpec((nc, hp, LANES, LANES), lambda h, i: (nt - 1 - i, h, 0, 0)), out_col],
               out_specs=[out_col, out_col, out_col, pl.BlockSpec((1, HGRN_STEP_WIDTH), lambda h, i: (0, h))],
               out_shape=[SDS((s, HGRN_WIDTH), BF16)] * 3 + [SDS((1, HGRN_WIDTH), F32)],
               scratch=[pltpu.VMEM((hp, LANES, LANES), F32)],
               sem=("parallel", "arbitrary"))(proj, proj, proj, lb_logits, states, do)


def mix_out_fwd(name, attn, o, proj, g_attn, g_hgrn):
    s = attn.shape[0]
    t = _tile(s, ROW_TILE_WIDE)
    half = HGRN_WIDTH // 2

    def body(a_ref, o_ref, hg0_ref, hg1_ref, ga_ref, gh_ref, c_ref):
        av = a_ref[...]
        c_ref[:, :ATTN_WIDTH] = ((av * _rstd(av)) * ga_ref[...]).astype(BF16)
        for j in range(HGRN_HEADS):
            cols = slice(j * LANES, (j + 1) * LANES)
            ov = o_ref[:, cols]
            hg_ref, hcols = (hg0_ref, cols) if j < 4 else (hg1_ref, slice((j - 4) * LANES, (j - 3) * LANES))
            hg = hg_ref[:, hcols]
            on = (ov * _rstd(ov)) * gh_ref[:, cols]
            c_ref[:, ATTN_WIDTH + j * LANES:ATTN_WIDTH + (j + 1) * LANES] = (on * (hg * _sigmoid(hg))).astype(BF16)

    return _pc(body, name=name, grid=(s // t,),
               in_specs=[_rows(t, ATTN_WIDTH), _rows(t, HGRN_WIDTH), _rows(t, half, COL_HG // 4), _rows(t, half, COL_HG // 4 + 1),
                         _fixed((1, ATTN_WIDTH)), _fixed((1, HGRN_WIDTH))],
               out_specs=_rows(t, D_MODEL), out_shape=SDS((s, D_MODEL), BF16), sem=("parallel",))(attn, o, proj, proj, g_attn, g_hgrn)


def mix_out_bwd(name, dcat, attn, o, proj, g_attn, g_hgrn):
    s = attn.shape[0]
    t = _tile(s, ROW_TILE_WIDE)
    half = HGRN_WIDTH // 2

    def body(dc_ref, a_ref, o_ref, hg0_ref, hg1_ref, ga_ref, gh_ref, da_ref, do_ref, dhg_ref, dga_ref, dgh_ref, pa_s, ph_s):
        av = a_ref[...]
        r = _rstd(av)
        xh = av * r
        dyv = dc_ref[:, :ATTN_WIDTH]
        dyg = dyv * ga_ref[...]
        da_ref[...] = r * (dyg - xh * jnp.mean(dyg * xh, axis=-1, keepdims=True))
        pa_s[...] = jnp.sum(dyv * xh, axis=0, keepdims=True)
        for j in range(HGRN_HEADS):
            cols = slice(j * LANES, (j + 1) * LANES)
            ov = o_ref[:, cols]
            hg_ref, hcols = (hg0_ref, cols) if j < 4 else (hg1_ref, slice((j - 4) * LANES, (j - 3) * LANES))
            hg = hg_ref[:, hcols]
            sg = _sigmoid(hg)
            r = _rstd(ov)
            xh = ov * r
            gain = gh_ref[:, cols]
            dh = dc_ref[:, ATTN_WIDTH + j * LANES:ATTN_WIDTH + (j + 1) * LANES]
            dhg_ref[:, cols] = ((dh * (xh * gain)) * (sg * (1.0 + hg * (1.0 - sg)))).astype(BF16)
            dyv = dh * (hg * sg)
            dyg = dyv * gain
            do_ref[:, cols] = r * (dyg - xh * jnp.mean(dyg * xh, axis=-1, keepdims=True))
            ph_s[:, cols] = jnp.sum(dyv * xh, axis=0, keepdims=True)

        @pl.when(pl.program_id(0) == 0)
        def _():
            dga_ref[...] = pa_s[...]
            dgh_ref[...] = ph_s[...]

        @pl.when(pl.program_id(0) > 0)
        def _():
            dga_ref[...] += pa_s[...]
            dgh_ref[...] += ph_s[...]

    return _pc(body, name=name, grid=(s // t,),
               in_specs=[_rows(t, D_MODEL), _rows(t, ATTN_WIDTH), _rows(t, HGRN_WIDTH), _rows(t, half, COL_HG // 4),
                         _rows(t, half, COL_HG // 4 + 1), _fixed((1, ATTN_WIDTH)), _fixed((1, HGRN_WIDTH))],
               out_specs=[_rows(t, ATTN_WIDTH), _rows(t, HGRN_WIDTH), _rows(t, HGRN_WIDTH), _fixed((1, ATTN_WIDTH)),
                          _fixed((1, HGRN_WIDTH))],
               out_shape=[SDS((s, ATTN_WIDTH), F32), SDS((s, HGRN_WIDTH), F32), SDS((s, HGRN_WIDTH), BF16),
                          SDS((1, ATTN_WIDTH), F32), SDS((1, HGRN_WIDTH), F32)],
               scratch=[pltpu.VMEM((1, ATTN_WIDTH), F32), pltpu.VMEM((1, HGRN_WIDTH), F32)],
               sem=("arbitrary",))(dcat, attn, o, proj, proj, g_attn, g_hgrn)


BIG = (("w_in", 2048, 1408, 0), ("w_out", 512, 2048, 1), ("w_ffn_gate", 2048, 1408, 0), ("w_ffn_up", 2048, 1408, 0),
       ("w_ffn_down", 1408, 2048, 1), ("w_ple_gate", 512, 2048, 1), ("w_ple_proj", 256, 512, 0))
BIG_BY_NAME = {spec[0]: spec for spec in BIG}
HBM_SPEC = pl.BlockSpec(memory_space=pltpu.HBM)
SEM_SPEC = pl.BlockSpec(memory_space=pltpu.SEMAPHORE)
TOKEN_SHAPE = (8, LANES)


def _split_call(body, *, name, in_specs, out_specs, out_shape, aliases):
    return pl.pallas_call(body, name=name, in_specs=in_specs, out_specs=out_specs, out_shape=out_shape,
                          input_output_aliases=aliases,
                          compiler_params=pltpu.CompilerParams(has_side_effects=pltpu.SideEffectType.DATAFLOW_SIDE_EFFECTING))


def _in_hbm(arrays):
    return [pltpu.with_memory_space_constraint(a, pltpu.HBM) for a in arrays]


GROUP_STEPS = 8


def _step_rows(rows):
    assert rows % (GROUP_STEPS * 16) == 0
    return rows // GROUP_STEPS


def cast_to_slots(name, place, ws, layer, after):
    nt = len(ws)

    def body(place_ref, *refs):
        for t in range(nt):
            refs[nt + 1 + t][...] = refs[t][...].astype(BF16)

    in_specs, out_specs = [], []
    for w in ws:
        block = (None, _step_rows(w.shape[1]), w.shape[2])
        in_specs.append(pl.BlockSpec(block, lambda i, pr: (layer, i, 0)))
        out_specs.append(pl.BlockSpec(block, lambda i, pr: (pr[1], i, 0)))
    gs = pltpu.PrefetchScalarGridSpec(num_scalar_prefetch=1, grid=(GROUP_STEPS,), in_specs=in_specs + [ANY_SPEC],
                                      out_specs=out_specs)
    return _pc(body, name=name, grid_spec=gs, in_specs=None, out_specs=None,
               out_shape=[SDS((N_CHIPS,) + w.shape[1:], BF16) for w in ws], sem=("parallel",))(place, *ws, after)


def _place():
    x, y, c = lax.axis_index("x"), lax.axis_index("y"), lax.axis_index("c")
    chips = [(1 - x, y), (x, 1 - y), (1 - x, 1 - y)]
    return x, y, c, chips


def _half(ref, axis, c, rows, cols):
    if axis == 0:
        return ref.at[pl.ds(pl.multiple_of(c * (rows // 2), 16), rows // 2), :]
    return ref.at[:, pl.ds(pl.multiple_of(c * (cols // 2), LANES), cols // 2)]


def _gather_copies(specs, bufs, send, recv):
    x, y, c, chips = _place()
    cps = []
    for t, (_, rows, cols, axis) in enumerate(specs):
        mine = _half(bufs[t].at[2 * x + y], axis, c, rows, cols)
        for k, (cx, cy) in enumerate(chips):
            cps.append(pltpu.make_async_remote_copy(src_ref=mine, dst_ref=mine, send_sem=send.at[3 * t + k],
                                                    recv_sem=recv.at[3 * t + k], device_id=(cx, cy, c), device_id_type=MESH))
    return cps


def gather_start(name, specs, bufs, after):
    nt = len(bufs)
    n = 3 * nt

    def body(*refs):
        send, recv, token = refs[nt + 1], refs[nt + 2], refs[-1]
        for cp in _gather_copies(specs, refs[:nt], send, recv):
            cp.start()
        token[...] = jnp.zeros(TOKEN_SHAPE, F32)

    out = _split_call(
        body, name=name, in_specs=[HBM_SPEC] * nt + [ANY_SPEC],
        out_specs=(SEM_SPEC, SEM_SPEC) + (HBM_SPEC,) * nt + (pl.BlockSpec(memory_space=pltpu.VMEM),),
        out_shape=(pltpu.SemaphoreType.DMA((n,)), pltpu.SemaphoreType.DMA((n,)))
        + tuple(pltpu.HBM(b.shape, b.dtype) for b in bufs) + (SDS(TOKEN_SHAPE, F32),),
        aliases={t: 2 + t for t in range(nt)})(*_in_hbm(bufs), after)
    return out[0], out[1], list(out[2:2 + nt]), out[-1]


def gather_wait(name, specs, send, recv, bufs, after):
    nt = len(bufs)

    def body(*refs):
        for cp in _gather_copies(specs, refs[:nt], refs[nt], refs[nt + 1]):
            cp.wait_send()
            cp.wait_recv()

    out = _split_call(
        body, name=name, in_specs=[HBM_SPEC] * nt + [SEM_SPEC, SEM_SPEC, pl.BlockSpec(memory_space=pl.ANY)],
        out_specs=(HBM_SPEC,) * nt, out_shape=tuple(pltpu.HBM(b.shape, b.dtype) for b in bufs),
        aliases={t: t for t in range(nt)})(*bufs, send, recv, after)
    return list(out)


def gather_pass(name, specs, bufs):
    nt = len(bufs)

    def body(*refs):
        ins, outs = refs[:nt], refs[nt:2 * nt]
        send, recv = refs[2 * nt:]
        x, y, c, chips = _place()
        cps = []
        for t, (_, rows, cols, axis) in enumerate(specs):
            for k, (cx, cy) in enumerate(chips):
                cp = pltpu.make_async_remote_copy(
                    src_ref=_half(ins[t].at[2 * cx + cy], axis, c, rows, cols),
                    dst_ref=_half(outs[t].at[2 * cx + cy], axis, c, rows, cols),
                    send_sem=send.at[3 * t + k], recv_sem=recv.at[3 * t + k], device_id=(x, y, 1 - c), device_id_type=MESH)
                cp.start()
                cps.append(cp)
        for t, (_, rows, cols, axis) in enumerate(specs):
            for k, (cx, cy) in enumerate(chips):
                theirs = _half(outs[t].at[2 * cx + cy], axis, 1 - c, rows, cols)
                pltpu.make_async_remote_copy(src_ref=theirs, dst_ref=theirs, send_sem=send.at[3 * t + k],
                                             recv_sem=recv.at[3 * t + k], device_id=(x, y, 1 - c), device_id_type=MESH).wait_recv()
        for cp in cps:
            cp.wait_send()

    return _pc(body, name=name, in_specs=[HBM_SPEC] * nt, out_specs=[HBM_SPEC] * nt,
               out_shape=[SDS(b.shape, b.dtype) for b in bufs], scratch=[pltpu.SemaphoreType.DMA((3 * nt,))] * 2,
               input_output_aliases={t: t for t in range(nt)})(*bufs)


def reduce_to_sibling(name, grads):
    nt = len(grads)

    def body(*refs):
        srcs, dsts = refs[:nt], refs[nt:2 * nt]
        send, recv = refs[2 * nt:]
        x, y, c, _ = _place()
        cps = []
        for t in range(nt):
            cp = pltpu.make_async_remote_copy(src_ref=srcs[t].at[1 - c], dst_ref=dsts[t], send_sem=send.at[t],
                                              recv_sem=recv.at[t], device_id=(x, y, 1 - c), device_id_type=MESH)
            cp.start()
            cps.append(cp)
        for cp in cps:
            cp.wait()

    return _pc(body, name=name, in_specs=[HBM_SPEC] * nt, out_specs=[HBM_SPEC] * nt,
               out_shape=[SDS(g.shape[1:], g.dtype) for g in grads],
               scratch=[pltpu.SemaphoreType.DMA((nt,))] * 2)(*grads)


def _sibling_copies(grads, lands, send, recv):
    x, y, c, _ = _place()
    return [pltpu.make_async_remote_copy(src_ref=grads[t].at[1 - c], dst_ref=lands[t], send_sem=send.at[t], recv_sem=recv.at[t],
                                         device_id=(x, y, 1 - c), device_id_type=MESH) for t in range(len(grads))]


def sibling_start(name, grads):
    nt = len(grads)
    lands = [lax.empty(g.shape[1:], g.dtype) for g in grads]

    def body(*refs):
        send, recv, token = refs[2 * nt], refs[2 * nt + 1], refs[-1]
        for cp in _sibling_copies(refs[:nt], refs[nt:2 * nt], send, recv):
            cp.start()
        token[...] = jnp.zeros(TOKEN_SHAPE, F32)

    both = list(grads) + lands
    out = _split_call(
        body, name=name, in_specs=[HBM_SPEC] * (2 * nt),
        out_specs=(SEM_SPEC, SEM_SPEC) + (HBM_SPEC,) * (2 * nt) + (pl.BlockSpec(memory_space=pltpu.VMEM),),
        out_shape=(pltpu.SemaphoreType.DMA((nt,)), pltpu.SemaphoreType.DMA((nt,)))
        + tuple(pltpu.HBM(b.shape, b.dtype) for b in both) + (SDS(TOKEN_SHAPE, F32),),
        aliases={t: 2 + t for t in range(2 * nt)})(*_in_hbm(both))
    return out[0], out[1], list(out[2:2 + nt]), list(out[2 + nt:2 + 2 * nt]), out[-1]


def sibling_wait(name, send, recv, grads, lands, after):
    nt = len(grads)

    def body(*refs):
        for cp in _sibling_copies(refs[:nt], refs[nt:2 * nt], refs[2 * nt], refs[2 * nt + 1]):
            cp.wait_send()
            cp.wait_recv()

    both = list(grads) + list(lands)
    out = _split_call(
        body, name=name, in_specs=[HBM_SPEC] * (2 * nt) + [SEM_SPEC, SEM_SPEC, pl.BlockSpec(memory_space=pl.ANY)],
        out_specs=(HBM_SPEC,) * (2 * nt), out_shape=tuple(pltpu.HBM(b.shape, b.dtype) for b in both),
        aliases={t: t for t in range(2 * nt)})(*both, send, recv, after)
    return list(out[:nt]), list(out[nt:])


def add_halves(name, place, grads, gots):
    nt = len(grads)

    def body(place_ref, *refs):
        for t in range(nt):
            val = (refs[t][...].astype(F32) + refs[nt + t][...].astype(F32)).astype(BF16)
            refs[2 * nt + t][...] = val

            @pl.when(pl.program_id(1) == place_ref[1])
            def _():
                refs[3 * nt + t][...] = val

    g_specs, o_specs, part_specs, slot_specs = [], [], [], []
    for g in grads:
        _, n, r, c = g.shape
        tr = _step_rows(r)
        g_specs.append(pl.BlockSpec((None, None, tr, c), lambda i, j, pr: (pr[0], j, i, 0)))
        o_specs.append(pl.BlockSpec((None, tr, c), lambda i, j, pr: (j, i, 0)))
        part_specs.append(pl.BlockSpec((None, tr, c), lambda i, j, pr: (j, i, 0)))
        slot_specs.append(pl.BlockSpec((None, tr, c), lambda i, j, pr: (pr[1], i, 0)))
    gs = pltpu.PrefetchScalarGridSpec(num_scalar_prefetch=1, grid=(GROUP_STEPS, N_CHIPS), in_specs=g_specs + o_specs,
                                      out_specs=part_specs + slot_specs)
    out = _pc(body, name=name, grid_spec=gs, in_specs=None, out_specs=None,
              out_shape=[SDS(g.shape[1:], BF16) for g in grads] * 2, sem=("parallel", "arbitrary"))(place, *grads, *gots)
    return list(out[:nt]), list(out[nt:])


def _chips_copies(parts, slots, send, recv):
    x, y, c, chips = _place()
    cps = []
    for t in range(len(parts)):
        for k, (cx, cy) in enumerate(chips):
            cps.append(pltpu.make_async_remote_copy(src_ref=parts[t].at[2 * cx + cy], dst_ref=slots[t].at[2 * x + y],
                                                    send_sem=send.at[3 * t + k], recv_sem=recv.at[3 * t + k],
                                                    device_id=(cx, cy, c), device_id_type=MESH))
    return cps


def chips_start(name, parts, slots):
    nt = len(parts)
    n = 3 * nt

    def body(*refs):
        send, recv, token = refs[2 * nt], refs[2 * nt + 1], refs[-1]
        for cp in _chips_copies(refs[:nt], refs[nt:2 * nt], send, recv):
            cp.start()
        token[...] = jnp.zeros(TOKEN_SHAPE, F32)

    both = list(parts) + list(slots)
    out = _split_call(
        body, name=name, in_specs=[HBM_SPEC] * (2 * nt),
        out_specs=(SEM_SPEC, SEM_SPEC) + (HBM_SPEC,) * (2 * nt) + (pl.BlockSpec(memory_space=pltpu.VMEM),),
        out_shape=(pltpu.SemaphoreType.DMA((n,)), pltpu.SemaphoreType.DMA((n,)))
        + tuple(pltpu.HBM(b.shape, b.dtype) for b in both) + (SDS(TOKEN_SHAPE, F32),),
        aliases={t: 2 + t for t in range(2 * nt)})(*_in_hbm(both))
    return out[0], out[1], list(out[2:2 + nt]), list(out[2 + nt:2 + 2 * nt]), out[-1]


def chips_wait(name, send, recv, parts, slots, after):
    nt = len(parts)

    def body(*refs):
        for cp in _chips_copies(refs[:nt], refs[nt:2 * nt], refs[2 * nt], refs[2 * nt + 1]):
            cp.wait_send()
            cp.wait_recv()

    both = list(parts) + list(slots)
    out = _split_call(
        body, name=name, in_specs=[HBM_SPEC] * (2 * nt) + [SEM_SPEC, SEM_SPEC, pl.BlockSpec(memory_space=pl.ANY)],
        out_specs=(HBM_SPEC,) * (2 * nt), out_shape=tuple(pltpu.HBM(b.shape, b.dtype) for b in both),
        aliases={t: t for t in range(2 * nt)})(*both, send, recv, after)
    return list(out[nt:])


def sum_chips(name, place, slots):
    nt = len(slots)

    def body(place_ref, *refs):
        for t in range(nt):
            s_ref = refs[t]
            acc = s_ref[0].astype(F32)
            for k in range(1, N_CHIPS):
                acc = acc + s_ref[k].astype(F32)
            refs[nt + t][...] = acc

    in_specs, out_specs = [], []
    for sl in slots:
        n, r, c = sl.shape
        tr = _step_rows(r)
        in_specs.append(pl.BlockSpec((n, tr, c), lambda i, pr: (0, i, 0)))
        out_specs.append(pl.BlockSpec((None, tr, c), lambda i, pr: (pr[0], i, 0)))
    gs = pltpu.PrefetchScalarGridSpec(num_scalar_prefetch=1, grid=(GROUP_STEPS,), in_specs=in_specs, out_specs=out_specs)
    return list(_pc(body, name=name, grid_spec=gs, in_specs=None, out_specs=None,
                    out_shape=[SDS((2,) + sl.shape[1:], F32) for sl in slots], sem=("parallel",))(place, *slots))


def share_with_sibling(name, bufs):
    nt = len(bufs)

    def body(*refs):
        ins, outs = refs[:nt], refs[nt:2 * nt]
        send, recv = refs[2 * nt:]
        x, y, c, _ = _place()
        cps = []
        for t in range(nt):
            cp = pltpu.make_async_remote_copy(src_ref=ins[t].at[c], dst_ref=outs[t].at[c], send_sem=send.at[t], recv_sem=recv.at[t],
                                              device_id=(x, y, 1 - c), device_id_type=MESH)
            cp.start()
            cps.append(cp)
        for t in range(nt):
            theirs = outs[t].at[1 - c]
            pltpu.make_async_remote_copy(src_ref=theirs, dst_ref=theirs, send_sem=send.at[t], recv_sem=recv.at[t],
                                         device_id=(x, y, 1 - c), device_id_type=MESH).wait_recv()
        for cp in cps:
            cp.wait_send()

    return _pc(body, name=name, in_specs=[HBM_SPEC] * nt, out_specs=[HBM_SPEC] * nt,
               out_shape=[SDS(b.shape, F32) for b in bufs], scratch=[pltpu.SemaphoreType.DMA((nt,))] * 2,
               input_output_aliases={t: t for t in range(nt)})(*bufs)


def _adamw(w, g, m, v):
    m = ADAM_B1 * m + (1.0 - ADAM_B1) * g
    v = ADAM_B2 * v + (1.0 - ADAM_B2) * (g * g)
    m_hat = m / (1.0 - ADAM_B1 ** ADAM_STEP)
    v_hat = v / (1.0 - ADAM_B2 ** ADAM_STEP)
    delta = -ADAM_LR * (m_hat / (jnp.sqrt(v_hat) + ADAM_EPS) + ADAM_WD * w)
    return delta, m, v


def adamw_big(name, w, m, v, g0, g1, axis):
    _, r, c = w.shape
    _, rh, ch = g0.shape
    tr = _tile(rh, 256)
    nb = rh // tr
    if axis == 0:
        wspec = pl.BlockSpec((None, tr, ch), lambda l, h, i: (l, h * nb + i, 0))
    else:
        wspec = pl.BlockSpec((None, tr, ch), lambda l, h, i: (l, i, h))
    g0spec = pl.BlockSpec((None, tr, ch), lambda l, h, i: (h * (1 - l), i * (1 - l), 0))
    g1spec = pl.BlockSpec((None, tr, ch), lambda l, h, i: (h * l, i * l, 0))

    def body(w_ref, m_ref, v_ref, g0_ref, g1_ref, go_ref, d_ref, mo_ref, vo_ref):
        def run(g_ref):
            g = g_ref[...]
            delta, mn, vn = _adamw(w_ref[...], g, m_ref[...], v_ref[...])
            go_ref[...] = g
            d_ref[...] = delta
            mo_ref[...] = mn
            vo_ref[...] = vn

        @pl.when(pl.program_id(0) == 0)
        def _():
            run(g0_ref)

        @pl.when(pl.program_id(0) == 1)
        def _():
            run(g1_ref)

    return _pc(body, name=name, grid=(2, 2, nb), in_specs=[wspec, wspec, wspec, g0spec, g1spec], out_specs=[wspec] * 4,
               out_shape=[SDS(w.shape, F32)] * 4, sem=("parallel", "parallel", "parallel"))(w, m, v, g0, g1)


SMALL = (("pre_mix_gain", 2048), ("post_mix_gain", 2048), ("pre_ffn_gain", 2048), ("post_ffn_gain", 2048), ("ple_gain", 2048),
         ("attn_out_gain", 1024), ("hgrn_out_gain", 1024), ("hgrn_lb_logits", 1024), ("attn_sinks", 128))
SMALL_ROWS = sum(2 * w // LANES for _, w in SMALL)
SMALL_PAD = -(-SMALL_ROWS // 8) * 8
LB_ROW = sum(2 * w // LANES for _, w in SMALL[:7])


def _pack_small(parts):
    rows = []
    for nm, w in SMALL:
        a = parts[nm].astype(F32)
        if a.shape[1] != w:
            a = jnp.pad(a, ((0, 0), (0, w - a.shape[1])))
        rows.append(a.reshape(2 * w // LANES, LANES))
    rows.append(jnp.zeros((SMALL_PAD - SMALL_ROWS, LANES), F32))
    return jnp.concatenate(rows, axis=0)


def _unpack_small(packed, widths):
    out, r = {}, 0
    for nm, w in SMALL:
        n = 2 * w // LANES
        out[nm] = packed[r:r + n].reshape(2, w)[:, :widths[nm]]
        r += n
    return out


def allreduce_small(name, packed):
    rows = packed.shape[0]

    def body(x_ref, o_ref, buf, send, recv, own_sem):
        x, y, c, _ = _place()
        me = 4 * x + 2 * y + c
        own = pltpu.make_async_copy(x_ref, buf.at[me], own_sem)
        own.start()
        cps = []
        for k in range(1, 8):
            px, py, pc = x ^ (k >> 2), y ^ ((k >> 1) & 1), c ^ (k & 1)
            cp = pltpu.make_async_remote_copy(src_ref=x_ref, dst_ref=buf.at[me], send_sem=send.at[k - 1], recv_sem=recv.at[k - 1],
                                              device_id=(px, py, pc), device_id_type=MESH)
            cp.start()
            cps.append(cp)
        for k in range(1, 8):
            px, py, pc = x ^ (k >> 2), y ^ ((k >> 1) & 1), c ^ (k & 1)
            slot = buf.at[4 * px + 2 * py + pc]
            pltpu.make_async_remote_copy(src_ref=slot, dst_ref=slot, send_sem=send.at[k - 1], recv_sem=recv.at[k - 1],
                                         device_id=(px, py, pc), device_id_type=MESH).wait_recv()
        for cp in cps:
            cp.wait_send()
        own.wait()
        acc = buf[0]
        for k in range(1, 8):
            acc = acc + buf[k]
        o_ref[...] = acc

    vm = pl.BlockSpec(memory_space=pltpu.VMEM)
    return _pc(body, name=name, in_specs=[vm], out_specs=vm, out_shape=SDS((rows, LANES), F32),
               scratch=[pltpu.VMEM((8, rows, LANES), F32), pltpu.SemaphoreType.DMA((7,)), pltpu.SemaphoreType.DMA((7,)),
                        pltpu.SemaphoreType.DMA])(packed)


def adamw_small(name, w, m, v, g):
    rows = w.shape[0]
    n = HGRN_WIDTH // LANES

    def body(w_ref, m_ref, v_ref, g_ref, go_ref, d_ref, mo_ref, vo_ref):
        go_ref[...] = g_ref[...]
        l0 = w_ref[LB_ROW:LB_ROW + n, :]
        l1 = w_ref[LB_ROW + n:LB_ROW + 2 * n, :]
        mx = jnp.maximum(l0, l1)
        e0, e1 = jnp.exp(l0 - mx), jnp.exp(l1 - mx)
        s0, s1 = e0 / (e0 + e1), e1 / (e0 + e1)
        dlb1 = g_ref[LB_ROW + n:LB_ROW + 2 * n, :]
        inner = s1 * dlb1
        go_ref[LB_ROW:LB_ROW + n, :] = s0 * (0.0 - inner)
        go_ref[LB_ROW + n:LB_ROW + 2 * n, :] = s1 * (dlb1 - inner)
        delta, mn, vn = _adamw(w_ref[...], go_ref[...], m_ref[...], v_ref[...])
        d_ref[...] = delta
        mo_ref[...] = mn
        vo_ref[...] = vn

    vm = pl.BlockSpec(memory_space=pltpu.VMEM)
    return _pc(body, name=name, in_specs=[vm] * 4, out_specs=[vm] * 4, out_shape=[SDS((rows, LANES), F32)] * 4)(w, m, v, g)


def _layer_fwd(l, x, h1, p_l, w_in_g, rest_of_weights, gains, cos, sin, sinks, lb_logits, g_next, target):
    n = f"l{l}_"
    proj = mm_col(n + "in_proj", h1, w_in_g)
    qp, kp, vp = rope_qkv(n + "rope_qkv", proj, cos, sin)
    attn = attn_fwd(n + "attn_fwd", qp, kp, vp, sinks)
    o, states = hgrn_fwd(n + "hgrn_fwd", proj, lb_logits, l)
    cat = mix_out_fwd(n + "mix_out_fwd", attn, o, proj, gains["attn_out_gain"], gains["hgrn_out_gain"])
    rest, token = rest_of_weights("mix", cat)
    wts = dict(rest, w_in=w_in_g)
    if token is not None:
        gains = _with_token(gains, "post_mix_gain", token)
    m, x1, h2 = out_proj_post_mix(n + "out_proj_post_mix", cat, wts["w_out"], gains["post_mix_gain"], x, gains["pre_ffn_gain"])
    g, u, a = ffn_gate_up(n + "ffn_gate_up", h2, wts["w_ffn_gate"], wts["w_ffn_up"])
    wts.update(rest_of_weights("ffn", a)[0])
    f = mm_row(n + "ffn_down", a, wts["w_ffn_down"])
    x2, h3 = post_pre_norm(n + "post_ffn", f, gains["post_ffn_gain"], x1, gains["ple_gain"])
    saved = dict(x=x, h1=h1, proj=proj, qp=qp, kp=kp, vp=vp, attn=attn, o=o, states=states, cat=cat, m=m, x1=x1, h2=h2,
                 g=g, u=u, a=a, f=f, x2=x2, h3=h3, p=p_l)
    if target is None:
        z, pp, *out = ple_gate_fwd_mid(n + "ple_gate_fwd", h3, wts["w_ple_gate"], p_l, wts["w_ple_proj"], x2, g_next)
        saved.update(z=z, pp=pp)
    else:
        dy, dpp, dz, loss = ple_gate_fwd_loss(n + "ple_gate_loss", h3, wts["w_ple_gate"], p_l, wts["w_ple_proj"], x2, target)
        out = [dy, loss]
        saved.update(dpp=dpp, dz=dz)
    return out, saved, wts


EARLY = ("w_ple_gate", "w_ple_proj", "w_ffn_down", "w_ffn_gate", "w_ffn_up")
LATE = ("w_out", "w_in")


def _layer_bwd_ffn(l, dx3, sv, wts, gains, after=None, hook=None):
    n = f"l{l}_"
    dpp, dz = (sv["dpp"], sv["dz"]) if "dz" in sv else ple_bwd(n + "ple_bwd", dx3, sv["z"], sv["pp"],
                                                                 dx3 if after is None else after)
    dh3 = mm_row_t(n + "ple_gate_dx", dz, wts["w_ple_gate"])
    if hook is not None:
        gains = _with_token(gains, "ple_gain", hook(dh3))
    dx2, df, d_ple_gain, d_post_ffn = norm_bwd_pair(n + "ple_post_ffn_bwd", sv["x2"], gains["ple_gain"], dh3, dx3, sv["f"],
                                                    gains["post_ffn_gain"])
    dg, du = ffn_down_bwd(n + "ffn_down_bwd", df, wts["w_ffn_down"], sv["g"], sv["u"])
    big = dict(
        w_ple_gate=mm_wg_row(n + "ple_gate_dw", sv["h3"], dz),
        w_ple_proj=mm_wg_col(n + "ple_proj_dw", sv["p"], dpp),
        w_ffn_down=mm_wg_row(n + "ffn_down_dw", sv["a"], df),
        w_ffn_gate=mm_wg_col(n + "ffn_gate_dw", sv["h2"], dg),
        w_ffn_up=mm_wg_col(n + "ffn_up_dw", sv["h2"], du),
    )
    return dict(dx2=dx2, dg=dg, du=du), big, dict(ple_gain=d_ple_gain, post_ffn_gain=d_post_ffn)


def _layer_bwd_mix(l, st, sv, wts, gains, cos, sin, sinks, lb_logits, after=None, hook=None):
    n = f"l{l}_"
    dh2 = mm_col_t(n + "ffn_gate_dx", st["dg"], wts["w_ffn_gate"], after=after)
    dh2 = mm_col_t(n + "ffn_up_dx", st["du"], wts["w_ffn_up"], add=dh2)
    if hook is not None:
        gains = _with_token(gains, "pre_ffn_gain", hook(dh2))
    dx1, dm, d_pre_ffn, d_post_mix = norm_bwd_pair(n + "pre_ffn_post_mix_bwd", sv["x1"], gains["pre_ffn_gain"], dh2, st["dx2"],
                                                   sv["m"], gains["post_mix_gain"])
    dcat = mm_row_t(n + "out_proj_dx", dm, wts["w_out"])
    dattn, do, dhg, d_attn_gain, d_hgrn_gain = mix_out_bwd(n + "mix_out_bwd", dcat, sv["attn"], sv["o"], sv["proj"],
                                                            gains["attn_out_gain"], gains["hgrn_out_gain"])
    dqp, dkc, dkp, dvc, dvp, dsinks = attn_bwd(n + "attn_bwd", sv["qp"], sv["kp"], sv["vp"], sinks, dattn)
    dqkv = rope_bwd(n + "rope_bwd", dqp, dkc, dkp, dvc, dvp, cos, sin)
    dhq, dhf, dhi, dlb = hgrn_bwd(n + "hgrn_bwd", sv["proj"], lb_logits, l, sv["states"], do)
    dproj, dh1 = in_proj_bwd(n + "in_proj_dx", [dqkv, dhq, dhf, dhi, dhg], wts["w_in"])
    dx, d_pre_mix = norm_bwd(n + "pre_mix_bwd", sv["x"], gains["pre_mix_gain"], dh1, dx1)
    big = dict(w_out=mm_wg_row(n + "out_proj_dw", sv["cat"], dm), w_in=mm_wg_col(n + "in_proj_dw", sv["h1"], dproj))
    small = dict(pre_mix_gain=d_pre_mix, post_mix_gain=d_post_mix, pre_ffn_gain=d_pre_ffn, attn_out_gain=d_attn_gain,
                 hgrn_out_gain=d_hgrn_gain, hgrn_lb_logits=dlb, attn_sinks=dsinks)
    return dx, big, small


def _reduce_start(tag, names, big, place):
    grads = [big[nm] for nm in names]
    got = reduce_to_sibling(tag + "_reduce_to_sibling", grads)
    parts, slots = add_halves(tag + "_add", place, grads, got)
    return chips_start(tag + "_chips_start", parts, slots)


def _reduce_begin(tag, names, big):
    return sibling_start(tag + "_sibling_start", [big[nm] for nm in names])


def _reduce_chips(tag, names, begun, place, after):
    send, recv, grads, lands, _ = begun
    grads, got = sibling_wait(tag + "_sibling_wait", send, recv, grads, lands, after)
    parts, slots = add_halves(tag + "_add", place, grads, got)
    return chips_start(tag + "_chips_start", parts, slots)


def _reduce_finish(tag, names, started, place, after):
    send, recv, parts, slots, _ = started
    slots = chips_wait(tag + "_chips_wait", send, recv, parts, slots, after)
    bufs = sum_chips(tag + "_sum", place, slots)
    return dict(zip(names, share_with_sibling(tag + "_share_with_sibling", bufs)))


def _reduce_finish_both(first, second, place, after):
    slots = []
    for tag, _, (send, recv, parts, slots_t, _) in (first, second):
        slots += chips_wait(tag + "_chips_wait", send, recv, parts, slots_t, after)
    tag = first[0] + "_" + second[0]
    shared = share_with_sibling(tag + "_share_with_sibling", sum_chips(tag + "_sum", place, slots))
    n1 = len(first[1])
    return dict(zip(first[1], shared[:n1])), dict(zip(second[1], shared[n1:]))


def _with_token(gains, name, token):
    out = dict(gains)
    out[name] = gains[name] + token[0, 0]
    return out


def kernel(x, p, positions, w_in, attn_sinks, hgrn_lb_logits, attn_out_gain, hgrn_out_gain, w_out, pre_mix_gain, post_mix_gain, pre_ffn_gain, post_ffn_gain, w_ffn_gate, w_ffn_up, w_ffn_down, ple_gain, w_ple_gate, w_ple_proj, loss_target, m_w_in, m_attn_sinks, m_hgrn_lb_logits, m_attn_out_gain, m_hgrn_out_gain, m_w_out, m_pre_mix_gain, m_post_mix_gain, m_pre_ffn_gain, m_post_ffn_gain, m_w_ffn_gate, m_w_ffn_up, m_w_ffn_down, m_ple_gain, m_w_ple_gate, m_w_ple_proj, v_w_in, v_attn_sinks, v_hgrn_lb_logits, v_attn_out_gain, v_hgrn_out_gain, v_w_out, v_pre_mix_gain, v_post_mix_gain, v_pre_ffn_gain, v_post_ffn_gain, v_w_ffn_gate, v_w_ffn_up, v_w_ffn_down, v_ple_gain, v_w_ple_gate, v_w_ple_proj):
    given = dict(locals())
    depth = 2
    place = jnp.stack([lax.axis_index("c"), 2 * lax.axis_index("x") + lax.axis_index("y")]).astype(jnp.int32)
    xs = x[0]
    tgt = loss_target[0]
    pos_col = positions.reshape(-1, 1)
    half = 32
    inv_freq = ROPE_THETA ** (-jnp.arange(half, dtype=F32) / half)
    inv_freq = jnp.tile(inv_freq, 4).reshape(1, LANES)
    gains = [{nm: given[nm][l:l + 1] for nm, _ in SMALL[:7]} for l in range(depth)]
    names = [nm for nm, *_ in BIG]
    first, others = names[:1], names[1:]

    def specs(nms):
        return [BIG_BY_NAME[nm] for nm in nms]

    def cast(tag, l, nms, after):
        return cast_to_slots(tag + "_cast", place, [given[nm] for nm in nms], l, after)

    def finish_gather(tag, nms, started, after):
        bufs = gather_wait(tag + "_gather_wait", specs(nms), started[0], started[1], started[2], after)
        return dict(zip(nms, gather_pass(tag + "_gather_pass", specs(nms), bufs)))

    mid, last = names[1:4], names[4:]
    g0a = gather_start("l0a_gather_start", specs(first), cast("l0a", 0, first, place), place)
    g0b = gather_start("l0b_gather_start", specs(mid), cast("l0b", 0, mid, g0a[3]), g0a[3])
    g0c = gather_start("l0c_gather_start", specs(last), cast("l0c", 0, last, g0b[3]), g0b[3])
    started = {}

    def rest_of_layer0(stage, after):
        if stage == "ffn":
            return finish_gather("l0c", last, g0c, after), None
        got = finish_gather("l0b", mid, g0b, after)
        started["l1a"] = gather_start("l1a_gather_start", specs(first), l1_shards[:1], got["w_out"])
        started["l1b"] = gather_start("l1b_gather_start", specs(others), l1_shards[1:], started["l1a"][3])
        return got, started["l1b"][3]

    def rest_of_layer1(stage, after):
        return (finish_gather("l1b", others, started["l1b"], after) if stage == "mix" else {}), None

    cos, sin = rope_tables("rope_tables", pos_col, inv_freq)
    h1 = pre_norm("l0_pre_mix", xs, _with_token(gains[0], "pre_mix_gain", g0c[3])["pre_mix_gain"])
    l1_shards = cast("l1", 1, names, h1)
    w_in0 = finish_gather("l0a", first, g0a, l1_shards[0])["w_in"]
    (x_mid, h1_next), sv0, wts0 = _layer_fwd(0, xs, h1, p[0, 0], w_in0, rest_of_layer0, gains[0], cos, sin, attn_sinks[0],
                                             hgrn_lb_logits, gains[1]["pre_mix_gain"], None)
    w_in1 = finish_gather("l1a", first, started["l1a"], x_mid)["w_in"]
    (dy, loss_part), sv1, wts1 = _layer_fwd(1, x_mid, h1_next, p[1, 0], w_in1, rest_of_layer1, gains[1], cos, sin,
                                            attn_sinks[1], hgrn_lb_logits, None, tgt)

    st1, early1, small1 = _layer_bwd_ffn(1, dy, sv1, wts1, gains[1])
    dx_mid, late1, small1b = _layer_bwd_mix(1, st1, sv1, wts1, gains[1], cos, sin, attn_sinks[1], hgrn_lb_logits)
    big1, small1 = {**early1, **late1}, {**small1, **small1b}
    chips = {}

    def chips_after(tag, nms, begun):
        def hook(x):
            chips[tag] = _reduce_chips(tag, nms, begun, place, x)
            return chips[tag][4]
        return hook

    b1 = _reduce_begin("l1", names, big1)
    st0, early0, small0 = _layer_bwd_ffn(0, dx_mid, sv0, wts0, gains[0], after=b1[4], hook=chips_after("l1", names, b1))
    b0e = _reduce_begin("l0e", EARLY, early0)
    dx0, late0, small0b = _layer_bwd_mix(0, st0, sv0, wts0, gains[0], cos, sin, attn_sinks[0], hgrn_lb_logits, after=b0e[4],
                                         hook=chips_after("l0e", EARLY, b0e))
    r1, r0e = chips["l1"], chips["l0e"]
    small0 = {**small0, **small0b}
    r0l = _reduce_start("l0l", LATE, late0, place)
    red1, red0 = _reduce_finish_both(("l1", names, r1), ("l0e", EARLY, r0e), place, r0l[4])

    loss = lax.psum(loss_part[0, 0], ("x", "y", "c"))
    grad_x = dx0[None]

    out_big = {}
    for nm in EARLY:
        out_big[nm] = adamw_big("adamw_" + nm, given[nm], given["m_" + nm], given["v_" + nm], red0[nm], red1[nm], BIG_BY_NAME[nm][3])
    red0.update(_reduce_finish("l0l", LATE, r0l, place, out_big[EARLY[-1]][3]))
    for nm in LATE:
        out_big[nm] = adamw_big("adamw_" + nm, given[nm], given["m_" + nm], given["v_" + nm], red0[nm], red1[nm], BIG_BY_NAME[nm][3])

    widths = {nm: given[nm].shape[1] for nm, _ in SMALL}
    small_g = {nm: jnp.concatenate([small0[nm][:, :widths[nm]] if nm != "attn_sinks" else small0[nm][:, :LANES],
                                    small1[nm][:, :widths[nm]] if nm != "attn_sinks" else small1[nm][:, :LANES]], axis=0)
               for nm, _ in SMALL}
    g_sum = allreduce_small("allreduce_small", _pack_small(small_g))
    sm = adamw_small("adamw_small", _pack_small({nm: given[nm] for nm, _ in SMALL}),
                     _pack_small({nm: given["m_" + nm] for nm, _ in SMALL}),
                     _pack_small({nm: given["v_" + nm] for nm, _ in SMALL}), g_sum)
    out_small = [_unpack_small(a, widths) for a in sm]

    order = ["w_in", "attn_sinks", "hgrn_lb_logits", "attn_out_gain", "hgrn_out_gain", "w_out", "pre_mix_gain", "post_mix_gain",
             "pre_ffn_gain", "post_ffn_gain", "w_ffn_gate", "w_ffn_up", "w_ffn_down", "ple_gain", "w_ple_gate", "w_ple_proj"]
    res = [loss, grad_x]
    for k in range(4):
        for nm in order:
            res.append(out_big[nm][k] if nm in out_big else out_small[k][nm])
    return tuple(res)
```
